```python
import math
import jax, jax.numpy as jnp
from jax import lax
import numpy as np

D_MODEL = 1024
BATCH = 8
SEQ = 8192
DEPTH = 2

GRID_W = 64
CTX_LEN = 256
RMS_EPS = 1e-6
ROPE_BASE = 10000.0
N_MOD = 6
RET_HEADS = 4
RET_DK = 64
RET_DV = 128
RET_CHUNK = 128
CONV_WIDTH = 512
MLA_HEADS = 8
MLA_Q_RANK = 384
MLA_KV_RANK = 256
MLA_NOPE = 64
MLA_ROPE = 32
MLA_V = 64
ATTN_BLOCK = 128
N_BRANCH = 3
N_EXPERTS = 16
N_GROUPS = 4
EXPERTS_PER_GROUP = N_EXPERTS // N_GROUPS
TOP_K = 2
D_EXPERT = 512
IN_SIZES = (RET_HEADS * RET_DK, RET_HEADS * RET_DK, RET_HEADS * RET_DV, RET_HEADS * RET_DV,
            CONV_WIDTH, CONV_WIDTH, CONV_WIDTH, MLA_Q_RANK, MLA_KV_RANK, MLA_ROPE, N_BRANCH * D_MODEL)
IN_COLS = sum(IN_SIZES)

kernel_name = "hybrid_retention_conv_mla_moe_dit"


def split_cols(p):
    out = []
    off = 0
    for s in IN_SIZES:
        out.append(p[..., off:off + s])
        off += s
    return out


def rms_norm(x, w):
    xf = x.astype(jnp.float32)
    y = xf * lax.rsqrt(jnp.mean(xf * xf, axis=-1, keepdims=True) + RMS_EPS)
    return (y * w.astype(jnp.float32)).astype(x.dtype)


def rotary(x, pos):
    half = x.shape[-1] // 2
    inv = ROPE_BASE ** (-jnp.arange(half, dtype=jnp.float32) / half)
    ang = pos.astype(jnp.float32)[:, None] * inv[None, :]
    cos = jnp.cos(ang)[None, :, None, :]
    sin = jnp.sin(ang)[None, :, None, :]
    xf = x.astype(jnp.float32)
    x1, x2 = xf[..., :half], xf[..., half:]
    return jnp.concatenate([x1 * cos - x2 * sin, x2 * cos + x1 * sin], axis=-1).astype(x.dtype)


def axial_rotary(x, rows, cols):
    half = x.shape[-1] // 2
    return jnp.concatenate([rotary(x[..., :half], rows), rotary(x[..., half:], cols)], axis=-1)


def ret_split(p, dim, pos):
    b, l, _ = p.shape
    t = p.reshape(b, l, RET_HEADS, dim)
    if pos is not None:
        t = rotary(t, pos)
    return t.astype(jnp.float32)


def retention_chunked(q, k, v, log_g, s0, strict):
    b, l, h, dk = q.shape
    dv = v.shape[-1]
    n = l // RET_CHUNK
    qc = q.reshape(b, n, RET_CHUNK, h, dk)
    kc = k.reshape(b, n, RET_CHUNK, h, dk)
    vc = v.reshape(b, n, RET_CHUNK, h, dv)
    idx = jnp.arange(RET_CHUNK, dtype=jnp.float32)
    rel = idx[:, None] - idx[None, :]
    keep = (rel > 0) if strict else (rel >= 0)
    decay = jnp.where(keep[None], jnp.exp(log_g[:, None, None] * jnp.maximum(rel, 0.0)[None]), 0.0)
    scores = jnp.einsum('bnihd,bnjhd->bnhij', qc, kc) * decay[None, None]
    inner = jnp.einsum('bnhij,bnjhe->bnihe', scores, vc)
    zeta = jnp.exp(log_g[None, :] * (RET_CHUNK - 1 - idx)[:, None])
    u = jnp.einsum('bnjhd,bnjhe->nbhde', kc * zeta[None, None, :, :, None], vc)
    chunk_decay = jnp.exp(log_g * RET_CHUNK)[None, :, None, None]

    def step(s, u_n):
        return chunk_decay * s + u_n, s

    _, s_prev = lax.scan(step, s0, u)
    xi = jnp.exp(log_g[None, :] * (idx + 1.0)[:, None])
    cross = jnp.einsum('bnihd,nbhde->bnihe', qc * xi[None, None, :, :, None], s_prev)
    return (inner + cross).reshape(b, l, h, dv)


def ret_final_state(k, v, log_g, reverse):
    l = k.shape[1]
    pos = jnp.arange(l, dtype=jnp.float32)
    dist = pos if reverse else (l - 1.0 - pos)
    w = jnp.exp(dist[:, None] * log_g[None, :])
    return jnp.einsum('blhd,blhe->bhde', k * w[None, :, :, None], v)


def retention_bidir(q, k, v, log_gf, log_gb, s_f, s_b):
    y_f = retention_chunked(q, k, v, log_gf, s_f, False)
    y_b = retention_chunked(jnp.flip(q, 1), jnp.flip(k, 1), jnp.flip(v, 1), log_gb, s_b, True)
    return y_f + jnp.flip(y_b, 1)


def ret_output(y, pg, gn_w, w_o):
    b, l = y.shape[:2]
    yn = y * lax.rsqrt(jnp.mean(y * y, axis=-1, keepdims=True) + RMS_EPS)
    yn = yn.reshape(b, l, -1) * gn_w.astype(jnp.float32)
    return (jax.nn.silu(pg) * yn.astype(pg.dtype)) @ w_o


def short_conv(pb, pc, px, conv_w):
    u = pc * px
    up = jnp.pad(u, ((0, 0), (1, 1), (0, 0)))
    y = up[:, :-2] * conv_w[:, 0] + up[:, 1:-1] * conv_w[:, 1] + up[:, 2:] * conv_w[:, 2]
    return pb * y


def mla_queries(pq, q_norm, w_uq, rows, cols):
    b, l, _ = pq.shape
    q = (rms_norm(pq, q_norm) @ w_uq).reshape(b, l, MLA_HEADS, MLA_NOPE + MLA_ROPE)
    qn, qr = q[..., :MLA_NOPE], q[..., MLA_NOPE:]
    if rows is not None:
        qr = axial_rotary(qr, rows, cols)
    return qn, qr


def mla_keys(pkv, pkr, kv_norm, w_ukv, rows, cols):
    b, l, _ = pkv.shape
    kv = (rms_norm(pkv, kv_norm) @ w_ukv).reshape(b, l, MLA_HEADS, MLA_NOPE + MLA_V)
    kn, v = kv[..., :MLA_NOPE], kv[..., MLA_NOPE:]
    kr = pkr.reshape(b, l, 1, MLA_ROPE)
    if rows is not None:
        kr = axial_rotary(kr, rows, cols)
    return kn, kr, v


def mla_attend(qn, qr, kn, kr, v):
    scale = (MLA_NOPE + MLA_ROPE) ** -0.5
    s = jnp.einsum('bqhd,bkhd->bhqk', qn, kn) + jnp.einsum('bqhr,bkr->bhqk', qr, kr[:, :, 0])
    p = jax.nn.softmax(s.astype(jnp.float32) * scale, axis=-1).astype(v.dtype)
    o = jnp.einsum('bhqk,bkhd->bqhd', p, v)
    return o.reshape(o.shape[0], o.shape[1], -1)


def mla_attend_blocked(qn, qr, kn, kr, v):
    b, l = qn.shape[:2]
    nb = l // ATTN_BLOCK

    def to_blocks(t):
        return t.reshape(b, nb, ATTN_BLOCK, *t.shape[2:]).swapaxes(0, 1)

    out = lax.map(lambda qs: mla_attend(qs[0], qs[1], kn, kr, v), (to_blocks(qn), to_blocks(qr)))
    return out.swapaxes(0, 1).reshape(b, l, -1)


def merge_branches(pgate, y_ret, y_conv, y_mla, w_out):
    g = jax.nn.sigmoid(pgate.reshape(*pgate.shape[:-1], N_BRANCH, D_MODEL))
    m = g[..., 0, :] * y_ret + g[..., 1, :] * y_conv + g[..., 2, :] * y_mla
    return m @ w_out


def moe(h, w_router, router_bias, w1, w3, w2):
    shp = h.shape
    t = h.reshape(-1, shp[-1])
    scores = jax.nn.sigmoid((t @ w_router).astype(jnp.float32))
    biased = scores + router_bias.astype(jnp.float32)
    grp = biased.reshape(-1, N_GROUPS, EXPERTS_PER_GROUP)
    group_score = jnp.sum(lax.top_k(grp, TOP_K)[0], axis=-1)
    g_sel = jnp.argmax(group_score, axis=-1)
    in_grp = jnp.take_along_axis(grp, g_sel[:, None, None], axis=1)[:, 0]
    idx = g_sel[:, None] * EXPERTS_PER_GROUP + lax.top_k(in_grp, TOP_K)[1]
    w = jnp.take_along_axis(scores, idx, axis=1)
    w = w / jnp.sum(w, axis=-1, keepdims=True)
    gate = jnp.sum(jax.nn.one_hot(idx, N_EXPERTS, dtype=jnp.float32) * w[..., None], axis=1).astype(h.dtype)
    out = jnp.zeros_like(t)
    for e in range(N_EXPERTS):
        he = (jax.nn.silu(t @ w1[e]) * (t @ w3[e])) @ w2[e]
        out = out + gate[:, e:e + 1] * he
    return out.reshape(shp)


def trunk_layer(x, xc, silu_c, silu_cc, rows, cols, pos, w_ada, b_ada, norm1, norm2, w_in,
                ret_decay, ret_gn, w_ret_o, conv_w, w_conv_o, mla_q_norm, w_uq, mla_kv_norm,
                w_ukv, w_mla_o, w_out, w_router, router_bias, w1, w3, w2, update_ctx):
    mod_x = jnp.split((silu_c @ w_ada + b_ada)[:, None, :], N_MOD, axis=-1)
    mod_c = jnp.split((silu_cc @ w_ada + b_ada)[None, None, :], N_MOD, axis=-1)
    hx = rms_norm(x, norm1) * (1.0 + mod_x[1]) + mod_x[0]
    hc = rms_norm(xc, norm1) * (1.0 + mod_c[1]) + mod_c[0]
    px = split_cols(hx @ w_in)
    pc = split_cols(hc @ w_in)
    log_gf = jax.nn.log_sigmoid(ret_decay[0].astype(jnp.float32))
    log_gb = jax.nn.log_sigmoid(ret_decay[1].astype(jnp.float32))

    kc = ret_split(pc[1], RET_DK, None) * (RET_DK ** -0.5)
    vc = ret_split(pc[2], RET_DV, None)
    s_f = ret_final_state(kc, vc, log_gf, False)
    s_b = ret_final_state(kc, vc, log_gb, True)
    qx = ret_split(px[0], RET_DK, pos)
    kx = ret_split(px[1], RET_DK, pos) * (RET_DK ** -0.5)
    vx = ret_split(px[2], RET_DV, None)
    y_ret = ret_output(retention_bidir(qx, kx, vx, log_gf, log_gb, s_f, s_b), px[3], ret_gn, w_ret_o)

    y_conv = short_conv(px[4], px[5], px[6], conv_w) @ w_conv_o

    kn_c, kr_c, v_c = mla_keys(pc[8], pc[9], mla_kv_norm, w_ukv, None, None)
    kn_x, kr_x, v_x = mla_keys(px[8], px[9], mla_kv_norm, w_ukv, rows, cols)
    qn_x, qr_x = mla_queries(px[7], mla_q_norm, w_uq, rows, cols)
    kn = jnp.concatenate([kn_c, kn_x], axis=1)
    kr = jnp.concatenate([kr_c, kr_x], axis=1)
    v = jnp.concatenate([v_c, v_x], axis=1)
    y_mla = mla_attend_blocked(qn_x, qr_x, kn, kr, v) @ w_mla_o

    x_mid = x + mod_x[2] * merge_branches(px[10], y_ret, y_conv, y_mla, w_out)
    h2x = rms_norm(x_mid, norm2) * (1.0 + mod_x[4]) + mod_x[3]

    if update_ctx:
        zero = jnp.zeros_like(s_f)
        qc = ret_split(pc[0], RET_DK, None)
        yc_ret = ret_output(retention_bidir(qc, kc, vc, log_gf, log_gb, zero, zero), pc[3], ret_gn, w_ret_o)
        yc_conv = short_conv(pc[4], pc[5], pc[6], conv_w) @ w_conv_o
        qn_c, qr_c = mla_queries(pc[7], mla_q_norm, w_uq, None, None)
        yc_mla = mla_attend(qn_c, qr_c, kn_c, kr_c, v_c) @ w_mla_o
        xc_mid = xc + mod_c[2] * merge_branches(pc[10], yc_ret, yc_conv, yc_mla, w_out)
        h2c = rms_norm(xc_mid, norm2) * (1.0 + mod_c[4]) + mod_c[3]
        n_ctx = xc.shape[1]
        f = moe(jnp.concatenate([h2c, h2x], axis=1), w_router, router_bias, w1, w3, w2)
        return x_mid + mod_x[5] * f[:, n_ctx:], xc_mid + mod_c[5] * f[:, :n_ctx]

    return x_mid + mod_x[5] * moe(h2x, w_router, router_bias, w1, w3, w2), xc


def setup_inputs(seed: int = 0) -> dict:
    key = jax.random.key(seed)
    ks = jax.random.split(key, 32)
    f32 = jnp.float32

    def nrm(k, shape, scale):
        return jax.random.normal(k, shape, f32) * scale

    base_decay = jnp.log(2.0 ** (5.0 + jnp.arange(RET_HEADS, dtype=f32)) - 1.0)
    return {
        'x': nrm(ks[0], (BATCH, SEQ, D_MODEL), 1.0),
        'c': nrm(ks[1], (BATCH, D_MODEL), 1.0),
        'ctx': nrm(ks[2], (BATCH, CTX_LEN, D_MODEL), 1.0),
        'c_ctx': nrm(ks[3], (D_MODEL,), 1.0),
        'w_ada': nrm(ks[4], (DEPTH, D_MODEL, N_MOD * D_MODEL), 0.5 * D_MODEL ** -0.5),
        'b_ada': nrm(ks[5], (DEPTH, N_MOD * D_MODEL), 0.02),
        'norm1': 1.0 + nrm(ks[6], (DEPTH, D_MODEL), 0.05),
        'norm2': 1.0 + nrm(ks[7], (DEPTH, D_MODEL), 0.05),
        'w_in': nrm(ks[8], (DEPTH, D_MODEL, IN_COLS), D_MODEL ** -0.5),
        'ret_decay': base_decay[None, None, :] + nrm(ks[9], (DEPTH, 2, RET_HEADS), 0.1),
        'ret_gn': 1.0 + nrm(ks[10], (DEPTH, RET_HEADS * RET_DV), 0.05),
        'w_ret_o': nrm(ks[11], (DEPTH, RET_HEADS * RET_DV, D_MODEL), (RET_HEADS * RET_DV) ** -0.5),
        'conv_w': nrm(ks[12], (DEPTH, CONV_WIDTH, 3), 3.0 ** -0.5),
        'w_conv_o': nrm(ks[13], (DEPTH, CONV_WIDTH, D_MODEL), CONV_WIDTH ** -0.5),
        'mla_q_norm': 1.0 + nrm(ks[14], (DEPTH, MLA_Q_RANK), 0.05),
        'w_uq': nrm(ks[15], (DEPTH, MLA_Q_RANK, MLA_HEADS * (MLA_NOPE + MLA_ROPE)), MLA_Q_RANK ** -0.5),
        'mla_kv_norm': 1.0 + nrm(ks[16], (DEPTH, MLA_KV_RANK), 0.05),
        'w_ukv': nrm(ks[17], (DEPTH, MLA_KV_RANK, MLA_HEADS * (MLA_NOPE + MLA_V)), MLA_KV_RANK ** -0.5),
        'w_mla_o': nrm(ks[18], (DEPTH, MLA_HEADS * MLA_V, D_MODEL), (MLA_HEADS * MLA_V) ** -0.5),
        'w_out': nrm(ks[19], (DEPTH, D_MODEL, D_MODEL), D_MODEL ** -0.5),
        'w_router': nrm(ks[20], (D_MODEL, N_EXPERTS), D_MODEL ** -0.5),
        'router_bias': nrm(ks[21], (N_EXPERTS,), 0.01),
        'w1': nrm(ks[22], (DEPTH, N_EXPERTS, D_MODEL, D_EXPERT), D_MODEL ** -0.5),
        'w3': nrm(ks[23], (DEPTH, N_EXPERTS, D_MODEL, D_EXPERT), D_MODEL ** -0.5),
        'w2': nrm(ks[24], (DEPTH, N_EXPERTS, D_EXPERT, D_MODEL), D_EXPERT ** -0.5),
        'final_norm': 1.0 + nrm(ks[25], (D_MODEL,), 0.05),
    }


def reference(x, c, ctx, c_ctx, w_ada, b_ada, norm1, norm2, w_in, ret_decay, ret_gn, w_ret_o,
              conv_w, w_conv_o, mla_q_norm, w_uq, mla_kv_norm, w_ukv, w_mla_o, w_out,
              w_router, router_bias, w1, w3, w2, final_norm):
    n_tok = x.shape[1]
    n_rows = n_tok // GRID_W
    pos = jnp.arange(n_tok, dtype=jnp.int32)
    rows = jnp.repeat(jnp.arange(n_rows, dtype=jnp.int32), GRID_W)
    cols = pos - rows * GRID_W
    silu_c = jax.nn.silu(c)
    silu_cc = jax.nn.silu(c_ctx)
    xc = ctx
    for l in range(DEPTH):
        x, xc = trunk_layer(x, xc, silu_c, silu_cc, rows, cols, pos, w_ada[l], b_ada[l], norm1[l], norm2[l],
                            w_in[l], ret_decay[l], ret_gn[l], w_ret_o[l], conv_w[l], w_conv_o[l],
                            mla_q_norm[l], w_uq[l], mla_kv_norm[l], w_ukv[l], w_mla_o[l], w_out[l],
                            w_router, router_bias, w1[l], w3[l], w2[l], l < DEPTH - 1)
    return rms_norm(x, final_norm)
```

```python
import functools
import math

import jax
import jax.numpy as jnp
from jax import lax
from jax.experimental import pallas as pl
from jax.experimental.pallas import tpu as pltpu

F32 = jnp.float32
BF16 = jnp.bfloat16

GRID_W = 64
RMS_EPS = 1e-6
ROPE_BASE = 10000.0
N_MOD = 6
RET_HEADS = 4
RET_DK = 64
RET_DV = 128
RET_CHUNK = 128
CONV_WIDTH = 512
MLA_HEADS = 8
MLA_Q_RANK = 384
MLA_KV_RANK = 256
MLA_NOPE = 64
MLA_ROPE = 32
MLA_V = 64
N_BRANCH = 3
N_EXPERTS = 16
N_GROUPS = 4
EXPERTS_PER_GROUP = N_EXPERTS // N_GROUPS
D_EXPERT = 512
IN_SIZES = (RET_HEADS * RET_DK, RET_HEADS * RET_DK, RET_HEADS * RET_DV, RET_HEADS * RET_DV,
            CONV_WIDTH, CONV_WIDTH, CONV_WIDTH, MLA_Q_RANK, MLA_KV_RANK, MLA_ROPE, 0)

LANES = 128
BF16_SUBLANES = 16
VMEM_LIMIT = 56 * 1024 * 1024

HEAD_PAD = LANES
ROW_BLOCK = 256
MOE_TILE = 256
PAIRS = [(a, b) for a in range(EXPERTS_PER_GROUP) for b in range(a + 1, EXPERTS_PER_GROUP)]
N_CLASSES = N_GROUPS * len(PAIRS)

RQ_W = RET_HEADS * RET_DK
RV_W = RET_HEADS * RET_DV
MLA_W = MLA_HEADS * HEAD_PAD
MLA_O = MLA_HEADS * MLA_V


def _const_spec(shape):
    nd = len(shape)
    return pl.BlockSpec(shape, lambda *_: (0,) * nd, pipeline_mode=pl.Buffered(1))


def _params(n_axes):
    return pltpu.CompilerParams(dimension_semantics=("arbitrary",) * n_axes,
                                vmem_limit_bytes=VMEM_LIMIT)


def _dot(a, b):
    return jnp.dot(a, b, preferred_element_type=F32)


def _split_bf16(a):
    hi = a.astype(BF16)
    lo = (a - hi.astype(F32)).astype(BF16)
    return hi, lo


def _silu(v):
    return v * jax.nn.sigmoid(v)


def _rms(v):
    return v * lax.rsqrt(jnp.mean(v * v, axis=-1, keepdims=True) + RMS_EPS)


def _ada_body(c_ref, w_ref, b_ref, o_ref):
    a_hi, a_lo = _split_bf16(_silu(c_ref[...]))
    w_hi, w_lo = _split_bf16(w_ref[...])
    o_ref[...] = _dot(a_hi, w_hi) + _dot(a_hi, w_lo) + _dot(a_lo, w_hi) + b_ref[...]


def _ada_call(cc, w_ada, b_ada):
    depth, d, nm = w_ada.shape
    rows = cc.shape[0]
    cb = nm // 4
    return pl.pallas_call(
        _ada_body,
        out_shape=jax.ShapeDtypeStruct((depth, rows, nm), F32),
        grid=(depth, nm // cb),
        in_specs=[pl.BlockSpec((rows, d), lambda l, n: (0, 0)),
                  pl.BlockSpec((None, d, cb), lambda l, n: (l, 0, n)),
                  pl.BlockSpec((None, 1, cb), lambda l, n: (l, 0, n))],
        out_specs=pl.BlockSpec((None, rows, cb), lambda l, n: (l, 0, n)),
        compiler_params=_params(2),
        name="ada_mod",
    )(cc, w_ada, b_ada.reshape(depth, 1, nm))


_O_RQ, _O_RQR, _O_RK, _O_RKR = 0, 256, 512, 768
_O_RV, _O_RG, _O_CB, _O_CC, _O_CX = 1024, 1536, 2048, 2560, 3072
_O_QD = 3584
_O_KVD = _O_QD + MLA_Q_RANK
_O_KR = _O_KVD + MLA_KV_RANK
_O_KRR = _O_KR + LANES
_O_GT = _O_KRR + LANES


def _inproj_body(x_ref, a1_ref, b1_ref, cr_ref, sr_ref, cq_ref, sq_ref, ck_ref, sk_ref,
                 w_ref, wuq_ref, wk_ref, wv_ref, qn_ref, kvn_ref, ones_ref,
                 rq_ref, rk_ref, rv_ref, rg_ref, cb_ref, cu_ref, qm_ref, km_ref, vm_ref, gt_ref,
                 *, d_model):
    h = (_rms(x_ref[...]) * a1_ref[...] + b1_ref[...]).astype(BF16)

    def mm(off, width):
        return _dot(h, w_ref[:, off:off + width])

    cr = cr_ref[...]
    sr = sr_ref[...]
    rq_ref[...] = (mm(_O_RQ, RQ_W) * cr + mm(_O_RQR, RQ_W) * sr).astype(BF16)
    rk_ref[...] = ((mm(_O_RK, RQ_W) * cr + mm(_O_RKR, RQ_W) * sr) * (RET_DK ** -0.5)).astype(BF16)
    rv_ref[...] = mm(_O_RV, RV_W).astype(BF16)
    rg_ref[...] = _silu(mm(_O_RG, RV_W)).astype(BF16)
    cb_ref[...] = mm(_O_CB, CONV_WIDTH).astype(BF16)
    cu_ref[...] = (mm(_O_CC, CONV_WIDTH) * mm(_O_CX, CONV_WIDTH)).astype(BF16)

    qn = (_rms(mm(_O_QD, MLA_Q_RANK)) * qn_ref[...]).astype(BF16)
    cq = cq_ref[...]
    sq = sq_ref[...]
    for hd in range(MLA_HEADS):
        lo = hd * HEAD_PAD
        qa = _dot(qn, wuq_ref[:, lo:lo + HEAD_PAD])
        qb = _dot(qn, wuq_ref[:, MLA_W + lo:MLA_W + lo + HEAD_PAD])
        qm_ref[:, lo:lo + HEAD_PAD] = (qa * cq + qb * sq).astype(BF16)

    kvn = (_rms(mm(_O_KVD, MLA_KV_RANK)) * kvn_ref[...]).astype(BF16)
    kr = (mm(_O_KR, LANES) * ck_ref[...] + mm(_O_KRR, LANES) * sk_ref[...]).astype(BF16)
    km_ref[...] = _dot(jnp.concatenate([kvn, kr], axis=-1), wk_ref[...]).astype(BF16)
    vm_ref[...] = (_dot(kvn, wv_ref[...]) + ones_ref[...]).astype(BF16)

    for br in range(N_BRANCH):
        gt_ref[:, br * d_model:(br + 1) * d_model] = jax.nn.sigmoid(
            mm(_O_GT + br * d_model, d_model)).astype(BF16)


def _inproj_call(xa, a1, b1, tabs, wts, *, n_batch, nj, ncb):
    rows, d = xa.shape
    tm = ROW_BLOCK
    row = lambda j, b: (b * nj + j, 0)
    mod = lambda j, b: (2 * b + (j >= ncb).astype(jnp.int32), 0, 0)
    tab = lambda j, b: (j, 0)
    in_specs = [pl.BlockSpec((tm, d), row),
                pl.BlockSpec((None, 1, d), mod), pl.BlockSpec((None, 1, d), mod)]
    in_specs += [pl.BlockSpec((tm, t.shape[1]), tab) for t in tabs]
    in_specs += [_const_spec(w.shape) for w in wts]
    widths = (RQ_W, RQ_W, RV_W, RV_W, CONV_WIDTH, CONV_WIDTH, MLA_W, MLA_W, MLA_W, N_BRANCH * d)
    return pl.pallas_call(
        functools.partial(_inproj_body, d_model=d),
        out_shape=[jax.ShapeDtypeStruct((rows, w), BF16) for w in widths],
        grid=(nj, n_batch),
        in_specs=in_specs,
        out_specs=[pl.BlockSpec((tm, w), row) for w in widths],
        compiler_params=_params(2),
        name="in_proj",
    )(xa, a1, b1, *tabs, *wts)


def _ret_direction(q_ref, k_ref, v_ref, y_ref, s_ref, dm_ref, xi_ref, zt_ref, cd_ref, chunk_order):
    for c in chunk_order:
        rows = slice(c * RET_CHUNK, (c + 1) * RET_CHUNK)
        q = q_ref[rows, :]
        k = k_ref[rows, :]
        v = v_ref[rows, :]
        qx = (q.astype(F32) * xi_ref[...]).astype(BF16)
        kz = (k.astype(F32) * zt_ref[...]).astype(BF16)
        for hd in range(RET_HEADS):
            ks = slice(hd * RET_DK, (hd + 1) * RET_DK)
            vs = slice(hd * RET_DV, (hd + 1) * RET_DV)
            vh = v[:, vs]
            sc = lax.dot_general(q[:, ks], k[:, ks], (((1,), (1,)), ((), ())),
                                 preferred_element_type=F32)
            inner = _dot((sc * dm_ref[hd]).astype(BF16), vh)
            state = s_ref[hd]
            cross = _dot(qx[:, ks], state.astype(BF16))
            y_ref[rows, vs] = inner + cross
            upd = lax.dot_general(kz[:, ks], vh, (((0,), (0,)), ((), ())),
                                  preferred_element_type=F32)
            s_ref[hd] = cd_ref[hd] * state + upd


def _ret_body(qf_ref, kf_ref, vf_ref, qb_ref, kb_ref, vb_ref,
              dmf_ref, dmb_ref, xif_ref, ztf_ref, xib_ref, ztb_ref, cdf_ref, cdb_ref,
              yf_ref, yb_ref, sf_ref, sb_ref):
    @pl.when(pl.program_id(1) == 0)
    def _():
        sf_ref[...] = jnp.zeros_like(sf_ref)
        sb_ref[...] = jnp.zeros_like(sb_ref)

    n_chunks = ROW_BLOCK // RET_CHUNK
    _ret_direction(qf_ref, kf_ref, vf_ref, yf_ref, sf_ref, dmf_ref, xif_ref, ztf_ref, cdf_ref,
                   range(n_chunks))
    _ret_direction(qb_ref, kb_ref, vb_ref, yb_ref, sb_ref, dmb_ref, xib_ref, ztb_ref, cdb_ref,
                   range(n_chunks - 1, -1, -1))


def _ret_call(rq, rk, rv, consts, *, n_batch, nj, ncb):
    rows = rq.shape[0]
    tm = ROW_BLOCK
    fwd = lambda b, s: (b * nj + s, 0)

    def bwd(b, s):
        return (b * nj + jnp.where(s < ncb, ncb - 1 - s, nj - 1 - (s - ncb)), 0)

    specs = []
    for im in (fwd, bwd):
        specs += [pl.BlockSpec((tm, RQ_W), im), pl.BlockSpec((tm, RQ_W), im),
                  pl.BlockSpec((tm, RV_W), im)]
    specs += [_const_spec(c.shape) for c in consts]
    return pl.pallas_call(
        _ret_body,
        out_shape=[jax.ShapeDtypeStruct((rows, RV_W), F32)] * 2,
        grid=(n_batch, nj),
        in_specs=specs,
        out_specs=[pl.BlockSpec((tm, RV_W), fwd), pl.BlockSpec((tm, RV_W), bwd)],
        scratch_shapes=[pltpu.VMEM((RET_HEADS, RET_DK, RET_DV), F32)] * 2,
        compiler_params=_params(2),
        name="retention",
    )(rq, rk, rv, rq, rk, rv, *consts)


def _pick_tile(n, candidates):
    for c in candidates:
        if n % c == 0:
            return c
    raise ValueError(f"no tile for {n}")


def _attn_body(q_ref, k_ref, v_ref, o_ref, *, n_ctx, n_all, ncb):
    tq = q_ref.shape[0]

    def attend(n_keys, tk):
        outs = []
        for hh in range(2):
            hs = slice(hh * HEAD_PAD, (hh + 1) * HEAD_PAD)
            q = q_ref[:, hs]

            def step(t, carry):
                m, acc = carry
                r0 = pl.multiple_of(t * tk, tk)
                k = k_ref[pl.ds(r0, tk), hs]
                v = v_ref[pl.ds(r0, tk), hs]
                s = lax.dot_general(q, k, (((1,), (1,)), ((), ())), preferred_element_type=F32)
                m_new = jnp.maximum(m, jnp.max(s, axis=-1, keepdims=True))
                p = jnp.exp(s - m_new).astype(BF16)
                acc = jnp.exp(m - m_new) * acc + _dot(p, v)
                return m_new, acc

            m0 = jnp.full((tq, 1), -jnp.inf, F32)
            acc0 = jnp.zeros((tq, HEAD_PAD), F32)
            _, acc = lax.fori_loop(0, n_keys // tk, step, (m0, acc0))
            outs.append(acc[:, :MLA_V] / acc[:, MLA_V:MLA_V + 1])
        o_ref[...] = jnp.concatenate(outs, axis=-1).astype(BF16)

    j = pl.program_id(2)

    @pl.when(j < ncb)
    def _():
        attend(n_ctx, _pick_tile(n_ctx, (256, 128)))

    @pl.when(j >= ncb)
    def _():
        attend(n_all, _pick_tile(n_all, (768, 512, 256, 128)))


def _attn_call(qm, km, vm, *, n_batch, nj, ncb, n_ctx):
    rows = qm.shape[0]
    tm = ROW_BLOCK
    t_all = nj * tm
    qmap = lambda b, hp, j: (b * nj + j, hp)
    kmap = lambda b, hp, j: (b, hp)
    return pl.pallas_call(
        functools.partial(_attn_body, n_ctx=n_ctx, n_all=t_all, ncb=ncb),
        out_shape=jax.ShapeDtypeStruct((rows, MLA_O), BF16),
        grid=(n_batch, MLA_HEADS // 2, nj),
        in_specs=[pl.BlockSpec((tm, 2 * HEAD_PAD), qmap),
                  pl.BlockSpec((t_all, 2 * HEAD_PAD), kmap),
                  pl.BlockSpec((t_all, 2 * HEAD_PAD), kmap)],
        out_specs=pl.BlockSpec((tm, 2 * MLA_V), qmap),
        compiler_params=_params(3),
        name="mla_attention",
    )(qm, km, vm)


def _merge_body(yf_ref, yb_ref, rg_ref, cb_ref, cu_ref, cup_ref, cun_ref, om_ref, gt_ref, x_ref,
                g1_ref, a2_ref, b2_ref, gn_ref, cw_ref, wro_ref, wco_ref, wmo_ref, wout_ref,
                wrh_ref, wrl_ref, xmid_ref, h2_ref, lg_ref, *, nj, ncb, d_model):
    tm = x_ref.shape[0]
    j = pl.program_id(0) % nj
    seg_first = jnp.logical_or(j == 0, j == ncb)
    seg_last = jnp.logical_or(j == ncb - 1, j == nj - 1)

    y = yf_ref[...] + yb_ref[...]
    yn = jnp.concatenate([_rms(y[:, hd * RET_DV:(hd + 1) * RET_DV]) for hd in range(RET_HEADS)],
                         axis=-1) * gn_ref[...]
    y_ret = _dot((rg_ref[...].astype(F32) * yn).astype(BF16), wro_ref[...])

    u = cu_ref[...].astype(F32)
    ridx = lax.broadcasted_iota(jnp.int32, u.shape, 0)
    prev_row = cup_ref[...].astype(F32)[BF16_SUBLANES - 1:, :] * jnp.where(seg_first, 0.0, 1.0)
    next_row = cun_ref[...].astype(F32)[0:1, :] * jnp.where(seg_last, 0.0, 1.0)
    u_prev = jnp.where(ridx == 0, prev_row, pltpu.roll(u, 1, 0))
    u_next = jnp.where(ridx == tm - 1, next_row, pltpu.roll(u, tm - 1, 0))
    conv = u_prev * cw_ref[0:1, :] + u * cw_ref[1:2, :] + u_next * cw_ref[2:3, :]
    y_conv = _dot((cb_ref[...].astype(F32) * conv).astype(BF16), wco_ref[...])

    y_mla = _dot(om_ref[...], wmo_ref[...])

    merged = (gt_ref[:, 0:d_model].astype(F32) * y_ret
              + gt_ref[:, d_model:2 * d_model].astype(F32) * y_conv
              + gt_ref[:, 2 * d_model:3 * d_model].astype(F32) * y_mla)
    x_mid = x_ref[...] + g1_ref[...] * _dot(merged.astype(BF16), wout_ref[...])
    xmid_ref[...] = x_mid

    h2 = _rms(x_mid) * a2_ref[...] + b2_ref[...]
    h_hi, h_lo = _split_bf16(h2)
    h2_ref[...] = h_hi
    lg_ref[...] = _dot(h_hi, wrh_ref[...]) + _dot(h_hi, wrl_ref[...]) + _dot(h_lo, wrh_ref[...])


def _merge_call(yf, yb, rg, cb, cu, om, gt, xa, g1, a2, b2, wts, *, nj, ncb):
    rows, d = xa.shape
    tm = ROW_BLOCK
    nb = rows // tm
    halo = BF16_SUBLANES
    per_blk = tm // halo
    row = lambda i: (i, 0)
    mod = lambda i: (2 * (i // nj) + ((i % nj) >= ncb).astype(jnp.int32), 0, 0)
    prev = lambda i: (jnp.maximum(i * per_blk - 1, 0), 0)
    nxt = lambda i: (jnp.minimum((i + 1) * per_blk, rows // halo - 1), 0)
    in_specs = [pl.BlockSpec((tm, RV_W), row), pl.BlockSpec((tm, RV_W), row),
                pl.BlockSpec((tm, RV_W), row),
                pl.BlockSpec((tm, CONV_WIDTH), row), pl.BlockSpec((tm, CONV_WIDTH), row),
                pl.BlockSpec((halo, CONV_WIDTH), prev), pl.BlockSpec((halo, CONV_WIDTH), nxt),
                pl.BlockSpec((tm, MLA_O), row), pl.BlockSpec((tm, N_BRANCH * d), row),
                pl.BlockSpec((tm, d), row),
                pl.BlockSpec((None, 1, d), mod), pl.BlockSpec((None, 1, d), mod),
                pl.BlockSpec((None, 1, d), mod)]
    in_specs += [_const_spec(w.shape) for w in wts]
    return pl.pallas_call(
        functools.partial(_merge_body, nj=nj, ncb=ncb, d_model=d),
        out_shape=[jax.ShapeDtypeStruct((rows, d), F32), jax.ShapeDtypeStruct((rows, d), BF16),
                   jax.ShapeDtypeStruct((rows, LANES), F32)],
        grid=(nb,),
        in_specs=in_specs,
        out_specs=[pl.BlockSpec((tm, d), row), pl.BlockSpec((tm, d), row),
                   pl.BlockSpec((tm, LANES), row)],
        compiler_params=_params(1),
        name="merge_out_proj",
    )(yf, yb, rg, cb, cu, cu, cu, om, gt, xa, g1, a2, b2, *wts)


def _moe_body(ea_ref, eb_ref, nu_ref, h_ref, gw_ref, w13a_ref, w2a_ref, w13b_ref, w2b_ref, o_ref):
    t = pl.program_id(0)

    @pl.when(t < nu_ref[0])
    def _():
        h = h_ref[...]
        gw = gw_ref[...]

        def expert(w13_ref, w2_ref, wt):
            a = _dot(h, w13_ref[...])
            act = _silu(a[:, :D_EXPERT]) * a[:, D_EXPERT:] * wt
            return _dot(act.astype(BF16), w2_ref[...])

        o_ref[...] = (expert(w13a_ref, w2a_ref, gw[:, 0:1])
                      + expert(w13b_ref, w2b_ref, gw[:, 1:2])).astype(BF16)

    @pl.when(t >= nu_ref[0])
    def _():
        o_ref[...] = jnp.zeros_like(o_ref)


def _moe_call(tile_ea, tile_eb, n_used, hs, gw, w13, w2):
    npad, d = hs.shape
    tmo = MOE_TILE
    row = lambda t, ea, eb, nu: (t, 0)
    wa = lambda t, ea, eb, nu: (ea[t], 0, 0)
    wb = lambda t, ea, eb, nu: (eb[t], 0, 0)
    grid_spec = pltpu.PrefetchScalarGridSpec(
        num_scalar_prefetch=3,
        grid=(npad // tmo,),
        in_specs=[pl.BlockSpec((tmo, d), row), pl.BlockSpec((tmo, 2), row),
                  pl.BlockSpec((None, d, 2 * D_EXPERT), wa), pl.BlockSpec((None, D_EXPERT, d), wa),
                  pl.BlockSpec((None, d, 2 * D_EXPERT), wb), pl.BlockSpec((None, D_EXPERT, d), wb)],
        out_specs=pl.BlockSpec((tmo, d), row))
    return pl.pallas_call(
        _moe_body,
        out_shape=jax.ShapeDtypeStruct((npad, d), BF16),
        grid_spec=grid_spec,
        compiler_params=_params(1),
        name="moe_experts",
    )(tile_ea, tile_eb, n_used, hs, gw, w13, w2, w13, w2)


def _route(logits, router_bias):
    scores = jax.nn.sigmoid(logits[:, :N_EXPERTS])
    biased = scores + router_bias.astype(F32)
    grp = biased.reshape(-1, N_GROUPS, EXPERTS_PER_GROUP)
    group_score = jnp.sum(lax.top_k(grp, 2)[0], axis=-1)
    g_sel = jnp.argmax(group_score, axis=-1).astype(jnp.int32)
    in_grp = jnp.take_along_axis(grp, g_sel[:, None, None], axis=1)[:, 0]
    top = lax.top_k(in_grp, 2)[1].astype(jnp.int32)
    lo = jnp.min(top, axis=-1)
    hi = jnp.max(top, axis=-1)
    idx = g_sel[:, None] * EXPERTS_PER_GROUP + jnp.stack([lo, hi], axis=-1)
    w = jnp.take_along_axis(scores, idx, axis=1)
    w = w / jnp.sum(w, axis=-1, keepdims=True)
    pair_id = jnp.asarray(
        [[PAIRS.index((a, b)) if a < b else 0 for b in range(EXPERTS_PER_GROUP)]
         for a in range(EXPERTS_PER_GROUP)], jnp.int32)
    cls = g_sel * len(PAIRS) + pair_id[lo, hi]
    return cls, w


def _dispatch(cls, w, n_tok):
    tmo = MOE_TILE
    n_tiles = n_tok // tmo + N_CLASSES
    onehot = (cls[:, None] == jnp.arange(N_CLASSES, dtype=jnp.int32)[None, :]).astype(jnp.int32)
    ranks = jnp.cumsum(onehot, axis=0) - onehot
    rank = jnp.sum(ranks * onehot, axis=1)
    counts = jnp.sum(onehot, axis=0)
    tiles_per = (counts + tmo - 1) // tmo
    tile_end = jnp.cumsum(tiles_per)
    offs = (tile_end - tiles_per) * tmo
    dest = offs[cls] + rank
    npad = n_tiles * tmo
    src = jnp.zeros((npad,), jnp.int32).at[dest].set(jnp.arange(n_tok, dtype=jnp.int32))
    gw = jnp.zeros((npad, 2), F32).at[dest].set(w)
    tile_ids = jnp.arange(n_tiles, dtype=jnp.int32)
    tile_cls = jnp.sum((tile_end[None, :] <= tile_ids[:, None]).astype(jnp.int32), axis=1)
    tile_cls = jnp.minimum(tile_cls, N_CLASSES - 1)
    pa = jnp.asarray([p[0] for p in PAIRS], jnp.int32)
    pb = jnp.asarray([p[1] for p in PAIRS], jnp.int32)
    base = (tile_cls // len(PAIRS)) * EXPERTS_PER_GROUP
    tile_ea = base + pa[tile_cls % len(PAIRS)]
    tile_eb = base + pb[tile_cls % len(PAIRS)]
    n_used = tile_end[-1:].astype(jnp.int32)
    return src, dest, gw, tile_ea, tile_eb, n_used


def _resid_body(x_ref, f_ref, g2_ref, o_ref):
    o_ref[...] = x_ref[...] + g2_ref[...] * f_ref[...].astype(F32)


def _final_body(x_ref, f_ref, g2_ref, fn_ref, o_ref):
    o_ref[...] = _rms(x_ref[...] + g2_ref[...] * f_ref[...].astype(F32)) * fn_ref[...]


def _resid_call(xmid, fg, g2, *, nj, ncb):
    rows, d = xmid.shape
    tm = ROW_BLOCK
    row = lambda i: (i, 0)
    mod = lambda i: (2 * (i // nj) + ((i % nj) >= ncb).astype(jnp.int32), 0, 0)
    return pl.pallas_call(
        _resid_body,
        out_shape=jax.ShapeDtypeStruct((rows, d), F32),
        grid=(rows // tm,),
        in_specs=[pl.BlockSpec((tm, d), row), pl.BlockSpec((tm, d), row),
                  pl.BlockSpec((None, 1, d), mod)],
        out_specs=pl.BlockSpec((tm, d), row),
        compiler_params=_params(1),
        name="moe_residual",
    )(xmid, fg, g2)


def _final_call(xmid, fg, g2, final_norm, *, n_batch, nj, ncb):
    rows, d = xmid.shape
    tm = ROW_BLOCK
    njl = nj - ncb
    src = lambda b, j: (b * nj + ncb + j, 0)
    return pl.pallas_call(
        _final_body,
        out_shape=jax.ShapeDtypeStruct((n_batch * njl * tm, d), F32),
        grid=(n_batch, njl),
        in_specs=[pl.BlockSpec((tm, d), src), pl.BlockSpec((tm, d), src),
                  pl.BlockSpec((None, 1, d), lambda b, j: (2 * b + 1, 0, 0)),
                  _const_spec((1, d))],
        out_specs=pl.BlockSpec((tm, d), lambda b, j: (b * njl + j, 0)),
        compiler_params=_params(2),
        name="final_norm",
    )(xmid, fg, g2, final_norm.reshape(1, d))


def _rot_cols(w, group):
    w3 = w.reshape(w.shape[0], -1, group)
    half = group // 2
    return jnp.concatenate([-w3[..., half:], w3[..., :half]], axis=-1).reshape(w.shape)


def _split_cols(w, sizes):
    out, off = [], 0
    for s in sizes:
        out.append(w[:, off:off + s])
        off += s
    out.append(w[:, off:])
    return out


def _layer_weights(w_in, w_uq, w_ukv, q_norm, kv_norm):
    d = w_in.shape[0]
    wq, wk, wv, wg, wcb, wcc, wcx, wqd, wkvd, wkr, wgate = _split_cols(w_in, IN_SIZES[:-1])
    pad = lambda w: jnp.pad(w, ((0, 0), (0, LANES - w.shape[1])))
    w_ext = jnp.concatenate(
        [wq, _rot_cols(wq, RET_DK), wk, _rot_cols(wk, RET_DK), wv, wg, wcb, wcc, wcx, wqd, wkvd,
         pad(wkr), pad(_rot_cols(wkr, MLA_ROPE // 2)), wgate], axis=1).astype(BF16)

    uq = w_uq.reshape(MLA_Q_RANK, MLA_HEADS, MLA_NOPE + MLA_ROPE)
    tail = HEAD_PAD - MLA_NOPE - MLA_ROPE
    uq_pad = jnp.pad(uq, ((0, 0), (0, 0), (0, tail)))
    uq_rope = uq[..., MLA_NOPE:].reshape(MLA_Q_RANK, MLA_HEADS * MLA_ROPE)
    uq_rot = _rot_cols(uq_rope, MLA_ROPE // 2).reshape(MLA_Q_RANK, MLA_HEADS, MLA_ROPE)
    uq_rot = jnp.pad(uq_rot, ((0, 0), (0, 0), (MLA_NOPE, tail)))
    wuq = jnp.concatenate([uq_pad.reshape(MLA_Q_RANK, MLA_W), uq_rot.reshape(MLA_Q_RANK, MLA_W)],
                          axis=1).astype(BF16)

    ukv = w_ukv.reshape(MLA_KV_RANK, MLA_HEADS, MLA_NOPE + MLA_V)
    wk_up = jnp.pad(ukv[..., :MLA_NOPE], ((0, 0), (0, 0), (0, HEAD_PAD - MLA_NOPE)))
    place = jnp.zeros((LANES, MLA_HEADS, HEAD_PAD), F32)
    r = jnp.arange(MLA_ROPE)
    place = place.at[r, :, MLA_NOPE + r].set(1.0)
    wk_ext = jnp.concatenate([wk_up.reshape(MLA_KV_RANK, MLA_W), place.reshape(LANES, MLA_W)],
                             axis=0).astype(BF16)
    wv_pad = jnp.pad(ukv[..., MLA_NOPE:], ((0, 0), (0, 0), (0, HEAD_PAD - MLA_V)))
    wv_pad = wv_pad.reshape(MLA_KV_RANK, MLA_W).astype(BF16)
    ones_row = jnp.zeros((MLA_HEADS, HEAD_PAD), F32).at[:, MLA_V].set(1.0).reshape(1, MLA_W)
    return (w_ext, wuq, wk_ext, wv_pad, q_norm.reshape(1, -1).astype(F32),
            kv_norm.reshape(1, -1).astype(F32), ones_row)


def _rotary_tables(n_ctx, n_lat):
    pos = jnp.arange(n_lat, dtype=jnp.int32)
    rows = pos // GRID_W
    cols = pos - rows * GRID_W

    def angles(p, half):
        inv = ROPE_BASE ** (-jnp.arange(half, dtype=F32) / half)
        return p.astype(F32)[:, None] * inv[None, :]

    def with_ctx(t, fill):
        return jnp.concatenate([jnp.full((n_ctx, t.shape[1]), fill, F32), t], axis=0)

    ang = angles(pos, RET_DK // 2)
    cr = with_ctx(jnp.tile(jnp.cos(ang), (1, 2 * RET_HEADS)), 1.0)
    sr = with_ctx(jnp.tile(jnp.sin(ang), (1, 2 * RET_HEADS)), 0.0)

    quarter = MLA_ROPE // 4
    ar, ac = angles(rows, quarter), angles(cols, quarter)
    cos32 = with_ctx(jnp.concatenate([jnp.cos(ar)] * 2 + [jnp.cos(ac)] * 2, axis=1), 1.0)
    sin32 = with_ctx(jnp.concatenate([jnp.sin(ar)] * 2 + [jnp.sin(ac)] * 2, axis=1), 0.0)
    n_all = n_ctx + n_lat
    scale = (MLA_NOPE + MLA_ROPE) ** -0.5
    tail = jnp.zeros((n_all, HEAD_PAD - MLA_NOPE - MLA_ROPE), F32)
    cq = scale * jnp.concatenate([jnp.ones((n_all, MLA_NOPE), F32), cos32, tail], axis=1)
    sq = scale * jnp.concatenate([jnp.zeros((n_all, MLA_NOPE), F32), sin32, tail], axis=1)
    kpad = jnp.zeros((n_all, LANES - MLA_ROPE), F32)
    ck = jnp.concatenate([cos32, kpad], axis=1)
    sk = jnp.concatenate([sin32, kpad], axis=1)
    return cr, sr, cq, sq, ck, sk


def _retention_consts(ret_decay):
    log_gf = jax.nn.log_sigmoid(ret_decay[0].astype(F32))
    log_gb = jax.nn.log_sigmoid(ret_decay[1].astype(F32))
    idx = jnp.arange(RET_CHUNK, dtype=F32)
    rel = idx[:, None] - idx[None, :]
    dm_f = jnp.where(rel >= 0, jnp.exp(log_gf[:, None, None] * jnp.maximum(rel, 0.0)[None]), 0.0)
    dm_b = jnp.where(rel < 0, jnp.exp(log_gb[:, None, None] * jnp.maximum(-rel, 0.0)[None]), 0.0)

    def lanes(t):
        return jnp.repeat(t, RET_DK, axis=1)

    xi_f = lanes(jnp.exp(log_gf[None, :] * (idx + 1.0)[:, None]))
    zt_f = lanes(jnp.exp(log_gf[None, :] * (RET_CHUNK - 1 - idx)[:, None]))
    xi_b = lanes(jnp.exp(log_gb[None, :] * (RET_CHUNK - idx)[:, None]))
    zt_b = lanes(jnp.exp(log_gb[None, :] * idx[:, None]))
    cd = lambda lg: jnp.broadcast_to(jnp.exp(lg * RET_CHUNK)[:, None, None], (RET_HEADS, 1, RET_DV))
    return dm_f, dm_b, xi_f, zt_f, xi_b, zt_b, cd(log_gf), cd(log_gb)


def kernel(x, c, ctx, c_ctx, w_ada, b_ada, norm1, norm2, w_in, ret_decay, ret_gn, w_ret_o, conv_w,
           w_conv_o, mla_q_norm, w_uq, mla_kv_norm, w_ukv, w_mla_o, w_out, w_router, router_bias,
           w1, w3, w2, final_norm):
    n_batch, n_lat, d = x.shape
    n_ctx = ctx.shape[1]
    depth = w_ada.shape[0]
    t_all = n_ctx + n_lat
    assert n_ctx % ROW_BLOCK == 0 and n_lat % ROW_BLOCK == 0 and n_lat % GRID_W == 0
    nj = t_all // ROW_BLOCK
    ncb = n_ctx // ROW_BLOCK
    n_tok = n_batch * t_all
    assert n_tok % MOE_TILE == 0
    geom = dict(n_batch=n_batch, nj=nj, ncb=ncb)

    cc = jnp.concatenate([c, c_ctx[None, :]], axis=0)
    cc = jnp.pad(cc, ((0, -cc.shape[0] % 8), (0, 0)))
    mod = _ada_call(cc, w_ada, b_ada)[:, :n_batch + 1].reshape(depth, n_batch + 1, N_MOD, d)
    pick = jnp.stack([jnp.full((n_batch,), n_batch, jnp.int32),
                      jnp.arange(n_batch, dtype=jnp.int32)], axis=1).reshape(-1)
    mod = mod[:, pick]

    tabs = _rotary_tables(n_ctx, n_lat)
    wr_hi, wr_lo = _split_bf16(jnp.pad(w_router.astype(F32), ((0, 0), (0, LANES - N_EXPERTS))))
    xa = jnp.concatenate([ctx, x], axis=1).reshape(n_tok, d)

    out = None
    for l in range(depth):
        m = mod[l]
        rowvec = lambda v: v.reshape(2 * n_batch, 1, d)
        a1 = rowvec(norm1[l][None, :] * (1.0 + m[:, 1]))
        b1 = rowvec(m[:, 0])
        g1 = rowvec(m[:, 2])
        a2 = rowvec(norm2[l][None, :] * (1.0 + m[:, 4]))
        b2 = rowvec(m[:, 3])
        g2 = rowvec(m[:, 5])

        wts = _layer_weights(w_in[l], w_uq[l], w_ukv[l], mla_q_norm[l], mla_kv_norm[l])
        rq, rk, rv, rg, cb, cu, qm, km, vm, gt = _inproj_call(xa, a1, b1, tabs, wts, **geom)

        yf, yb = _ret_call(rq, rk, rv, _retention_consts(ret_decay[l]), **geom)
        om = _attn_call(qm, km, vm, n_ctx=n_ctx, **geom)

        merge_wts = (ret_gn[l].reshape(1, -1).astype(F32), conv_w[l].T.astype(F32),
                     w_ret_o[l].astype(BF16), w_conv_o[l].astype(BF16), w_mla_o[l].astype(BF16),
                     w_out[l].astype(BF16), wr_hi, wr_lo)
        xmid, h2, logits = _merge_call(yf, yb, rg, cb, cu, om, gt, xa, g1, a2, b2, merge_wts,
                                       nj=nj, ncb=ncb)

        cls, gate_w = _route(logits, router_bias)
        src, dest, gw, tile_ea, tile_eb, n_used = _dispatch(cls, gate_w, n_tok)
        hs = jnp.take(h2, src, axis=0)
        w13 = jnp.concatenate([w1[l], w3[l]], axis=-1).astype(BF16)
        f_sorted = _moe_call(tile_ea, tile_eb, n_used, hs, gw, w13, w2[l].astype(BF16))
        fg = jnp.take(f_sorted, dest, axis=0)

        if l < depth - 1:
            xa = _resid_call(xmid, fg, g2, nj=nj, ncb=ncb)
        else:
            out = _final_call(xmid, fg, g2, final_norm, **geom)
    return out.reshape(n_batch, n_lat, d)
```

```python
import functools
import math

import jax
import jax.numpy as jnp
from jax import lax
from jax.experimental import pallas as pl
from jax.experimental.pallas import tpu as pltpu

F32 = jnp.float32
BF16 = jnp.bfloat16

GRID_W = 64
RMS_EPS = 1e-6
ROPE_BASE = 10000.0
N_MOD = 6
RET_HEADS = 4
RET_DK = 64
RET_DV = 128
RET_CHUNK = 128
CONV_WIDTH = 512
MLA_HEADS = 8
MLA_Q_RANK = 384
MLA_KV_RANK = 256
MLA_NOPE = 64
MLA_ROPE = 32
MLA_V = 64
N_BRANCH = 3
N_EXPERTS = 16
N_GROUPS = 4
EXPERTS_PER_GROUP = N_EXPERTS // N_GROUPS
D_EXPERT = 512
IN_SIZES = (RET_HEADS * RET_DK, RET_HEADS * RET_DK, RET_HEADS * RET_DV, RET_HEADS * RET_DV,
            CONV_WIDTH, CONV_WIDTH, CONV_WIDTH, MLA_Q_RANK, MLA_KV_RANK, MLA_ROPE, 0)

LANES = 128
BF16_SUBLANES = 16
VMEM_LIMIT = 56 * 1024 * 1024

HEAD_PAD = LANES
ROW_BLOCK = 256
MOE_TILE = 256
ATTN_KEY_TILES = (768, 512, 256, 128)
ROUTE_ROWS = 8
PAIRS = [(a, b) for a in range(EXPERTS_PER_GROUP) for b in range(a + 1, EXPERTS_PER_GROUP)]
N_CLASSES = N_GROUPS * len(PAIRS)

RQ_W = RET_HEADS * RET_DK
RV_W = RET_HEADS * RET_DV
MLA_W = MLA_HEADS * HEAD_PAD
MLA_O = MLA_HEADS * MLA_V


def _const_spec(shape):
    nd = len(shape)
    return pl.BlockSpec(shape, lambda *_: (0,) * nd, pipeline_mode=pl.Buffered(1))


def _params(n_axes):
    return pltpu.CompilerParams(dimension_semantics=("arbitrary",) * n_axes,
                                vmem_limit_bytes=VMEM_LIMIT)


def _dot(a, b):
    return jnp.dot(a, b, preferred_element_type=F32)


def _split_bf16(a):
    hi = a.astype(BF16)
    lo = (a - hi.astype(F32)).astype(BF16)
    return hi, lo


def _silu(v):
    return v * jax.nn.sigmoid(v)


def _rms(v):
    return v * lax.rsqrt(jnp.mean(v * v, axis=-1, keepdims=True) + RMS_EPS)


def _ada_body(c_ref, w_ref, b_ref, o_ref):
    a_hi, a_lo = _split_bf16(_silu(c_ref[...]))
    w_hi, w_lo = _split_bf16(w_ref[...])
    o_ref[...] = _dot(a_hi, w_hi) + _dot(a_hi, w_lo) + _dot(a_lo, w_hi) + b_ref[...]


def _ada_call(cc, w_ada, b_ada):
    depth, d, nm = w_ada.shape
    rows = cc.shape[0]
    cb = nm // 4
    return pl.pallas_call(
        _ada_body,
        out_shape=jax.ShapeDtypeStruct((depth, rows, nm), F32),
        grid=(depth, nm // cb),
        in_specs=[pl.BlockSpec((rows, d), lambda l, n: (0, 0)),
                  pl.BlockSpec((None, d, cb), lambda l, n: (l, 0, n)),
                  pl.BlockSpec((None, 1, cb), lambda l, n: (l, 0, n))],
        out_specs=pl.BlockSpec((None, rows, cb), lambda l, n: (l, 0, n)),
        compiler_params=_params(2),
        name="ada_mod",
    )(cc, w_ada, b_ada.reshape(depth, 1, nm))


_O_RQ, _O_RQR, _O_RK, _O_RKR = 0, 256, 512, 768
_O_RV, _O_RG, _O_CB, _O_CC, _O_CX = 1024, 1536, 2048, 2560, 3072
_O_QD = 3584
_O_KVD = _O_QD + MLA_Q_RANK
_O_KR = _O_KVD + MLA_KV_RANK
_O_KRR = _O_KR + LANES
_O_GT = _O_KRR + LANES


def _inproj_body(x_ref, a1_ref, b1_ref, cr_ref, sr_ref, cq_ref, sq_ref, ck_ref, sk_ref,
                 w_ref, wuq_ref, wk_ref, wv_ref, qn_ref, kvn_ref, ones_ref,
                 rq_ref, rk_ref, rv_ref, rg_ref, cb_ref, cu_ref, qm_ref, km_ref, vm_ref, gt_ref,
                 *, d_model):
    h = (_rms(x_ref[...]) * a1_ref[...] + b1_ref[...]).astype(BF16)

    def mm(off, width):
        return _dot(h, w_ref[:, off:off + width])

    cr = cr_ref[...]
    sr = sr_ref[...]
    rq_ref[...] = (mm(_O_RQ, RQ_W) * cr + mm(_O_RQR, RQ_W) * sr).astype(BF16)
    rk_ref[...] = ((mm(_O_RK, RQ_W) * cr + mm(_O_RKR, RQ_W) * sr) * (RET_DK ** -0.5)).astype(BF16)
    rv_ref[...] = mm(_O_RV, RV_W).astype(BF16)
    rg_ref[...] = _silu(mm(_O_RG, RV_W)).astype(BF16)
    cb_ref[...] = mm(_O_CB, CONV_WIDTH).astype(BF16)
    cu_ref[...] = (mm(_O_CC, CONV_WIDTH) * mm(_O_CX, CONV_WIDTH)).astype(BF16)

    qn = (_rms(mm(_O_QD, MLA_Q_RANK)) * qn_ref[...]).astype(BF16)
    cq = cq_ref[...]
    sq = sq_ref[...]
    for hd in range(MLA_HEADS):
        lo = hd * HEAD_PAD
        qa = _dot(qn, wuq_ref[:, lo:lo + HEAD_PAD])
        qb = _dot(qn, wuq_ref[:, MLA_W + lo:MLA_W + lo + HEAD_PAD])
        qm_ref[:, lo:lo + HEAD_PAD] = (qa * cq + qb * sq).astype(BF16)

    kvn = (_rms(mm(_O_KVD, MLA_KV_RANK)) * kvn_ref[...]).astype(BF16)
    kr = (mm(_O_KR, LANES) * ck_ref[...] + mm(_O_KRR, LANES) * sk_ref[...]).astype(BF16)
    km_ref[...] = _dot(jnp.concatenate([kvn, kr], axis=-1), wk_ref[...]).astype(BF16)
    vm_ref[...] = (_dot(kvn, wv_ref[...]) + ones_ref[...]).astype(BF16)

    for br in range(N_BRANCH):
        gt_ref[:, br * d_model:(br + 1) * d_model] = jax.nn.sigmoid(
            mm(_O_GT + br * d_model, d_model)).astype(BF16)


def _inproj_call(xa, a1, b1, tabs, wts, *, n_batch, nj, ncb):
    rows, d = xa.shape
    tm = ROW_BLOCK
    row = lambda j, b: (b * nj + j, 0)
    mod = lambda j, b: (2 * b + (j >= ncb).astype(jnp.int32), 0, 0)
    tab = lambda j, b: (j, 0)
    in_specs = [pl.BlockSpec((tm, d), row),
                pl.BlockSpec((None, 1, d), mod), pl.BlockSpec((None, 1, d), mod)]
    in_specs += [pl.BlockSpec((tm, t.shape[1]), tab) for t in tabs]
    in_specs += [_const_spec(w.shape) for w in wts]
    widths = (RQ_W, RQ_W, RV_W, RV_W, CONV_WIDTH, CONV_WIDTH, MLA_W, MLA_W, MLA_W, N_BRANCH * d)
    return pl.pallas_call(
        functools.partial(_inproj_body, d_model=d),
        out_shape=[jax.ShapeDtypeStruct((rows, w), BF16) for w in widths],
        grid=(nj, n_batch),
        in_specs=in_specs,
        out_specs=[pl.BlockSpec((tm, w), row) for w in widths],
        compiler_params=_params(2),
        name="in_proj",
    )(xa, a1, b1, *tabs, *wts)


def _ret_direction(q_ref, k_ref, v_ref, y_ref, s_ref, dm_ref, xi_ref, zt_ref, cd_ref, chunk_order):
    for c in chunk_order:
        rows = slice(c * RET_CHUNK, (c + 1) * RET_CHUNK)
        q = q_ref[rows, :]
        k = k_ref[rows, :]
        v = v_ref[rows, :]
        qx = (q.astype(F32) * xi_ref[...]).astype(BF16)
        kz = (k.astype(F32) * zt_ref[...]).astype(BF16)
        for hd in range(RET_HEADS):
            ks = slice(hd * RET_DK, (hd + 1) * RET_DK)
            vs = slice(hd * RET_DV, (hd + 1) * RET_DV)
            vh = v[:, vs]
            sc = lax.dot_general(q[:, ks], k[:, ks], (((1,), (1,)), ((), ())),
                                 preferred_element_type=F32)
            inner = _dot((sc * dm_ref[hd]).astype(BF16), vh)
            state = s_ref[hd]
            cross = _dot(qx[:, ks], state.astype(BF16))
            y_ref[rows, vs] = inner + cross
            upd = lax.dot_general(kz[:, ks], vh, (((0,), (0,)), ((), ())),
                                  preferred_element_type=F32)
            s_ref[hd] = cd_ref[hd] * state + upd


def _ret_body(qf_ref, kf_ref, vf_ref, qb_ref, kb_ref, vb_ref,
              dmf_ref, dmb_ref, xif_ref, ztf_ref, xib_ref, ztb_ref, cdf_ref, cdb_ref,
              yf_ref, yb_ref, sf_ref, sb_ref):
    @pl.when(pl.program_id(1) == 0)
    def _():
        sf_ref[...] = jnp.zeros_like(sf_ref)
        sb_ref[...] = jnp.zeros_like(sb_ref)

    n_chunks = ROW_BLOCK // RET_CHUNK
    _ret_direction(qf_ref, kf_ref, vf_ref, yf_ref, sf_ref, dmf_ref, xif_ref, ztf_ref, cdf_ref,
                   range(n_chunks))
    _ret_direction(qb_ref, kb_ref, vb_ref, yb_ref, sb_ref, dmb_ref, xib_ref, ztb_ref, cdb_ref,
                   range(n_chunks - 1, -1, -1))


def _ret_call(rq, rk, rv, consts, *, n_batch, nj, ncb):
    rows = rq.shape[0]
    tm = ROW_BLOCK
    fwd = lambda b, s: (b * nj + s, 0)

    def bwd(b, s):
        return (b * nj + jnp.where(s < ncb, ncb - 1 - s, nj - 1 - (s - ncb)), 0)

    specs = []
    for im in (fwd, bwd):
        specs += [pl.BlockSpec((tm, RQ_W), im), pl.BlockSpec((tm, RQ_W), im),
                  pl.BlockSpec((tm, RV_W), im)]
    specs += [_const_spec(c.shape) for c in consts]
    return pl.pallas_call(
        _ret_body,
        out_shape=[jax.ShapeDtypeStruct((rows, RV_W), F32)] * 2,
        grid=(n_batch, nj),
        in_specs=specs,
        out_specs=[pl.BlockSpec((tm, RV_W), fwd), pl.BlockSpec((tm, RV_W), bwd)],
        scratch_shapes=[pltpu.VMEM((RET_HEADS, RET_DK, RET_DV), F32)] * 2,
        compiler_params=_params(2),
        name="retention",
    )(rq, rk, rv, rq, rk, rv, *consts)


def _pick_tile(n, candidates):
    for c in candidates:
        if n % c == 0:
            return c
    raise ValueError(f"no tile for {n}")


def _attn_body(q_ref, k_ref, v_ref, o_ref, s_ref, *, n_ctx, n_all, ncb):
    tq = q_ref.shape[0]
    heads = [slice(hh * HEAD_PAD, (hh + 1) * HEAD_PAD) for hh in range(2)]

    def attend(n_keys, tk):
        nt = n_keys // tk
        qs = [q_ref[:, hs] for hs in heads]

        def qk(t, mrun):
            r0 = pl.multiple_of(t * tk, tk)
            out = []
            for hh, hs in enumerate(heads):
                s = lax.dot_general(qs[hh], k_ref[pl.ds(r0, tk), hs], (((1,), (1,)), ((), ())),
                                    preferred_element_type=F32)
                s_ref[hh, t, :, 0:tk] = s
                m = mrun[hh]
                for cc in range(tk // LANES):
                    m = jnp.maximum(m, s[:, cc * LANES:(cc + 1) * LANES])
                out.append(m)
            return tuple(out)

        mrun = lax.fori_loop(0, nt, qk, tuple(jnp.full((tq, LANES), -jnp.inf, F32) for _ in heads),
                             unroll=True)
        mrow = [jnp.max(m, axis=-1, keepdims=True) for m in mrun]

        def pv(t, accs):
            r0 = pl.multiple_of(t * tk, tk)
            out = []
            for hh, hs in enumerate(heads):
                p = jnp.exp2(s_ref[hh, t, :, 0:tk] - mrow[hh]).astype(BF16)
                out.append(accs[hh] + _dot(p, v_ref[pl.ds(r0, tk), hs]))
            return tuple(out)

        accs = lax.fori_loop(0, nt, pv, tuple(jnp.zeros((tq, HEAD_PAD), F32) for _ in heads),
                             unroll=True)
        o_ref[...] = jnp.concatenate([a[:, :MLA_V] / a[:, MLA_V:MLA_V + 1] for a in accs],
                                     axis=-1).astype(BF16)

    j = pl.program_id(2)

    @pl.when(j < ncb)
    def _():
        attend(n_ctx, _pick_tile(n_ctx, (256, 128)))

    @pl.when(j >= ncb)
    def _():
        attend(n_all, _pick_tile(n_all, ATTN_KEY_TILES))


def _attn_call(qm, km, vm, *, n_batch, nj, ncb, n_ctx):
    rows = qm.shape[0]
    tm = ROW_BLOCK
    t_all = nj * tm
    tk = _pick_tile(t_all, ATTN_KEY_TILES)
    qmap = lambda b, hp, j: (b * nj + j, hp)
    kmap = lambda b, hp, j: (b, hp)
    return pl.pallas_call(
        functools.partial(_attn_body, n_ctx=n_ctx, n_all=t_all, ncb=ncb),
        out_shape=jax.ShapeDtypeStruct((rows, MLA_O), BF16),
        grid=(n_batch, MLA_HEADS // 2, nj),
        in_specs=[pl.BlockSpec((tm, 2 * HEAD_PAD), qmap),
                  pl.BlockSpec((t_all, 2 * HEAD_PAD), kmap),
                  pl.BlockSpec((t_all, 2 * HEAD_PAD), kmap)],
        out_specs=pl.BlockSpec((tm, 2 * MLA_V), qmap),
        scratch_shapes=[pltpu.VMEM((2, t_all // tk, tm, tk), F32)],
        compiler_params=_params(3),
        name="mla_attention",
    )(qm, km, vm)


def _merge_body(yf_ref, yb_ref, rg_ref, cb_ref, cu_ref, cup_ref, cun_ref, om_ref, gt_ref, x_ref,
                g1_ref, a2_ref, b2_ref, gn_ref, cw_ref, wro_ref, wco_ref, wmo_ref, wout_ref,
                wrh_ref, wrl_ref, rb_ref, xmid_ref, h2_ref, rt_ref, *, nj, ncb, d_model):
    tm = x_ref.shape[0]
    j = pl.program_id(0) % nj
    seg_first = jnp.logical_or(j == 0, j == ncb)
    seg_last = jnp.logical_or(j == ncb - 1, j == nj - 1)

    y = yf_ref[...] + yb_ref[...]
    yn = jnp.concatenate([_rms(y[:, hd * RET_DV:(hd + 1) * RET_DV]) for hd in range(RET_HEADS)],
                         axis=-1) * gn_ref[...]
    y_ret = _dot((rg_ref[...].astype(F32) * yn).astype(BF16), wro_ref[...])

    u = cu_ref[...].astype(F32)
    ridx = lax.broadcasted_iota(jnp.int32, u.shape, 0)
    prev_row = cup_ref[...].astype(F32)[BF16_SUBLANES - 1:, :] * jnp.where(seg_first, 0.0, 1.0)
    next_row = cun_ref[...].astype(F32)[0:1, :] * jnp.where(seg_last, 0.0, 1.0)
    u_prev = jnp.where(ridx == 0, prev_row, pltpu.roll(u, 1, 0))
    u_next = jnp.where(ridx == tm - 1, next_row, pltpu.roll(u, tm - 1, 0))
    conv = u_prev * cw_ref[0:1, :] + u * cw_ref[1:2, :] + u_next * cw_ref[2:3, :]
    y_conv = _dot((cb_ref[...].astype(F32) * conv).astype(BF16), wco_ref[...])

    y_mla = _dot(om_ref[...], wmo_ref[...])

    merged = (gt_ref[:, 0:d_model].astype(F32) * y_ret
              + gt_ref[:, d_model:2 * d_model].astype(F32) * y_conv
              + gt_ref[:, 2 * d_model:3 * d_model].astype(F32) * y_mla)
    x_mid = x_ref[...] + g1_ref[...] * _dot(merged.astype(BF16), wout_ref[...])
    xmid_ref[...] = x_mid

    h2 = _rms(x_mid) * a2_ref[...] + b2_ref[...]
    h_hi, h_lo = _split_bf16(h2)
    h2_ref[...] = h_hi
    nt_dot = lambda w, h: lax.dot_general(w, h, (((1,), (1,)), ((), ())), preferred_element_type=F32)
    logits_t = nt_dot(wrh_ref[...], h_hi) + nt_dot(wrl_ref[...], h_hi) + nt_dot(wrh_ref[...], h_lo)
    cls, w_lo, w_hi = _route_rows(logits_t, rb_ref[...])
    rt_ref[...] = jnp.concatenate(
        [cls.astype(F32), w_lo, w_hi, jnp.zeros((ROUTE_ROWS - 3, tm), F32)], axis=0)


def _top2_of4(v):
    def first_max(rows):
        best, idx = rows[0], jnp.zeros(rows[0].shape, jnp.int32)
        for e in range(1, len(rows)):
            better = rows[e] > best
            idx = jnp.where(better, e, idx)
            best = jnp.where(better, rows[e], best)
        return best, idx

    b1, i1 = first_max(v)
    b2, i2 = first_max([jnp.where(i1 == e, -jnp.inf, v[e]) for e in range(len(v))])
    return i1, i2, b1, b2


def _route_rows(logits_t, bias):
    scores = jax.nn.sigmoid(logits_t)
    biased = scores + bias
    row = lambda a, e: a[e:e + 1, :]
    best = None
    for g in range(N_GROUPS):
        v = [row(biased, g * EXPERTS_PER_GROUP + e) for e in range(EXPERTS_PER_GROUP)]
        i1, i2, b1, b2 = _top2_of4(v)
        cand = (b1 + b2, jnp.full(i1.shape, g, jnp.int32), i1, i2)
        if best is None:
            best = cand
        else:
            better = cand[0] > best[0]
            best = tuple(jnp.where(better, c, o) for c, o in zip(cand, best))
    _, g_sel, i1, i2 = best
    lo = jnp.minimum(i1, i2)
    hi = jnp.maximum(i1, i2)
    e_lo = g_sel * EXPERTS_PER_GROUP + lo
    e_hi = g_sel * EXPERTS_PER_GROUP + hi
    s_lo = jnp.zeros_like(best[0])
    s_hi = jnp.zeros_like(best[0])
    for e in range(N_EXPERTS):
        s_lo = jnp.where(e_lo == e, row(scores, e), s_lo)
        s_hi = jnp.where(e_hi == e, row(scores, e), s_hi)
    total = s_lo + s_hi
    pair_base = jnp.where(lo == 0, 0, jnp.where(lo == 1, 3, 5))
    cls = g_sel * len(PAIRS) + pair_base + (hi - lo - 1)
    return cls, s_lo / total, s_hi / total


def _merge_call(yf, yb, rg, cb, cu, om, gt, xa, g1, a2, b2, wts, *, nj, ncb):
    rows, d = xa.shape
    tm = ROW_BLOCK
    nb = rows // tm
    halo = BF16_SUBLANES
    per_blk = tm // halo
    row = lambda i: (i, 0)
    mod = lambda i: (2 * (i // nj) + ((i % nj) >= ncb).astype(jnp.int32), 0, 0)
    prev = lambda i: (jnp.maximum(i * per_blk - 1, 0), 0)
    nxt = lambda i: (jnp.minimum((i + 1) * per_blk, rows // halo - 1), 0)
    in_specs = [pl.BlockSpec((tm, RV_W), row), pl.BlockSpec((tm, RV_W), row),
                pl.BlockSpec((tm, RV_W), row),
                pl.BlockSpec((tm, CONV_WIDTH), row), pl.BlockSpec((tm, CONV_WIDTH), row),
                pl.BlockSpec((halo, CONV_WIDTH), prev), pl.BlockSpec((halo, CONV_WIDTH), nxt),
                pl.BlockSpec((tm, MLA_O), row), pl.BlockSpec((tm, N_BRANCH * d), row),
                pl.BlockSpec((tm, d), row),
                pl.BlockSpec((None, 1, d), mod), pl.BlockSpec((None, 1, d), mod),
                pl.BlockSpec((None, 1, d), mod)]
    in_specs += [_const_spec(w.shape) for w in wts]
    return pl.pallas_call(
        functools.partial(_merge_body, nj=nj, ncb=ncb, d_model=d),
        out_shape=[jax.ShapeDtypeStruct((rows, d), F32), jax.ShapeDtypeStruct((rows, d), BF16),
                   jax.ShapeDtypeStruct((nb * ROUTE_ROWS, tm), F32)],
        grid=(nb,),
        in_specs=in_specs,
        out_specs=[pl.BlockSpec((tm, d), row), pl.BlockSpec((tm, d), row),
                   pl.BlockSpec((ROUTE_ROWS, tm), row)],
        compiler_params=_params(1),
        name="merge_out_proj",
    )(yf, yb, rg, cb, cu, cu, cu, om, gt, xa, g1, a2, b2, *wts)


def _moe_body(ea_ref, eb_ref, nu_ref, h_ref, gw_ref, w13a_ref, w2a_ref, w13b_ref, w2b_ref, o_ref):
    t = pl.program_id(0)

    @pl.when(t < nu_ref[0])
    def _():
        h = h_ref[...]
        gw = gw_ref[...]

        def expert(w13_ref, w2_ref, wt):
            a = _dot(h, w13_ref[...])
            act = _silu(a[:, :D_EXPERT]) * a[:, D_EXPERT:] * wt
            return _dot(act.astype(BF16), w2_ref[...])

        o_ref[...] = (expert(w13a_ref, w2a_ref, gw[:, 0:1])
                      + expert(w13b_ref, w2b_ref, gw[:, 1:2])).astype(BF16)

    @pl.when(t >= nu_ref[0])
    def _():
        o_ref[...] = jnp.zeros_like(o_ref)


def _moe_call(tile_ea, tile_eb, n_used, hs, gw, w13, w2):
    npad, d = hs.shape
    tmo = MOE_TILE
    row = lambda t, ea, eb, nu: (t, 0)
    wa = lambda t, ea, eb, nu: (ea[t], 0, 0)
    wb = lambda t, ea, eb, nu: (eb[t], 0, 0)
    grid_spec = pltpu.PrefetchScalarGridSpec(
        num_scalar_prefetch=3,
        grid=(npad // tmo,),
        in_specs=[pl.BlockSpec((tmo, d), row), pl.BlockSpec((tmo, 2), row),
                  pl.BlockSpec((None, d, 2 * D_EXPERT), wa), pl.BlockSpec((None, D_EXPERT, d), wa),
                  pl.BlockSpec((None, d, 2 * D_EXPERT), wb), pl.BlockSpec((None, D_EXPERT, d), wb)],
        out_specs=pl.BlockSpec((tmo, d), row))
    return pl.pallas_call(
        _moe_body,
        out_shape=jax.ShapeDtypeStruct((npad, d), BF16),
        grid_spec=grid_spec,
        compiler_params=_params(1),
        name="moe_experts",
    )(tile_ea, tile_eb, n_used, hs, gw, w13, w2, w13, w2)


def _dispatch(cls, w, n_tok):
    tmo = MOE_TILE
    n_tiles = n_tok // tmo + N_CLASSES
    onehot = (cls[:, None] == jnp.arange(N_CLASSES, dtype=jnp.int32)[None, :]).astype(jnp.int32)
    ranks = jnp.cumsum(onehot, axis=0) - onehot
    rank = jnp.sum(ranks * onehot, axis=1)
    counts = jnp.sum(onehot, axis=0)
    tiles_per = (counts + tmo - 1) // tmo
    tile_end = jnp.cumsum(tiles_per)
    offs = (tile_end - tiles_per) * tmo
    dest = offs[cls] + rank
    npad = n_tiles * tmo
    src = jnp.zeros((npad,), jnp.int32).at[dest].set(jnp.arange(n_tok, dtype=jnp.int32))
    gw = jnp.zeros((npad, 2), F32).at[dest].set(w)
    tile_ids = jnp.arange(n_tiles, dtype=jnp.int32)
    tile_cls = jnp.sum((tile_end[None, :] <= tile_ids[:, None]).astype(jnp.int32), axis=1)
    tile_cls = jnp.minimum(tile_cls, N_CLASSES - 1)
    pa = jnp.asarray([p[0] for p in PAIRS], jnp.int32)
    pb = jnp.asarray([p[1] for p in PAIRS], jnp.int32)
    base = (tile_cls // len(PAIRS)) * EXPERTS_PER_GROUP
    tile_ea = base + pa[tile_cls % len(PAIRS)]
    tile_eb = base + pb[tile_cls % len(PAIRS)]
    n_used = tile_end[-1:].astype(jnp.int32)
    return src, dest, gw, tile_ea, tile_eb, n_used


def _resid_body(x_ref, f_ref, g2_ref, o_ref):
    o_ref[...] = x_ref[...] + g2_ref[...] * f_ref[...].astype(F32)


def _final_body(x_ref, f_ref, g2_ref, fn_ref, o_ref):
    o_ref[...] = _rms(x_ref[...] + g2_ref[...] * f_ref[...].astype(F32)) * fn_ref[...]


def _resid_call(xmid, fg, g2, *, nj, ncb):
    rows, d = xmid.shape
    tm = ROW_BLOCK
    row = lambda i: (i, 0)
    mod = lambda i: (2 * (i // nj) + ((i % nj) >= ncb).astype(jnp.int32), 0, 0)
    return pl.pallas_call(
        _resid_body,
        out_shape=jax.ShapeDtypeStruct((rows, d), F32),
        grid=(rows // tm,),
        in_specs=[pl.BlockSpec((tm, d), row), pl.BlockSpec((tm, d), row),
                  pl.BlockSpec((None, 1, d), mod)],
        out_specs=pl.BlockSpec((tm, d), row),
        compiler_params=_params(1),
        name="moe_residual",
    )(xmid, fg, g2)


def _final_call(xmid, fg, g2, final_norm, *, n_batch, nj, ncb):
    rows, d = xmid.shape
    tm = ROW_BLOCK
    njl = nj - ncb
    src = lambda b, j: (b * nj + ncb + j, 0)
    return pl.pallas_call(
        _final_body,
        out_shape=jax.ShapeDtypeStruct((n_batch * njl * tm, d), F32),
        grid=(n_batch, njl),
        in_specs=[pl.BlockSpec((tm, d), src), pl.BlockSpec((tm, d), src),
                  pl.BlockSpec((None, 1, d), lambda b, j: (2 * b + 1, 0, 0)),
                  _const_spec((1, d))],
        out_specs=pl.BlockSpec((tm, d), lambda b, j: (b * njl + j, 0)),
        compiler_params=_params(2),
        name="final_norm",
    )(xmid, fg, g2, final_norm.reshape(1, d))


def _rot_cols(w, group):
    w3 = w.reshape(w.shape[0], -1, group)
    half = group // 2
    return jnp.concatenate([-w3[..., half:], w3[..., :half]], axis=-1).reshape(w.shape)


def _split_cols(w, sizes):
    out, off = [], 0
    for s in sizes:
        out.append(w[:, off:off + s])
        off += s
    out.append(w[:, off:])
    return out


def _layer_weights(w_in, w_uq, w_ukv, q_norm, kv_norm):
    d = w_in.shape[0]
    wq, wk, wv, wg, wcb, wcc, wcx, wqd, wkvd, wkr, wgate = _split_cols(w_in, IN_SIZES[:-1])
    pad = lambda w: jnp.pad(w, ((0, 0), (0, LANES - w.shape[1])))
    w_ext = jnp.concatenate(
        [wq, _rot_cols(wq, RET_DK), wk, _rot_cols(wk, RET_DK), wv, wg, wcb, wcc, wcx, wqd, wkvd,
         pad(wkr), pad(_rot_cols(wkr, MLA_ROPE // 2)), wgate], axis=1).astype(BF16)

    uq = w_uq.reshape(MLA_Q_RANK, MLA_HEADS, MLA_NOPE + MLA_ROPE)
    tail = HEAD_PAD - MLA_NOPE - MLA_ROPE
    uq_pad = jnp.pad(uq, ((0, 0), (0, 0), (0, tail)))
    uq_rope = uq[..., MLA_NOPE:].reshape(MLA_Q_RANK, MLA_HEADS * MLA_ROPE)
    uq_rot = _rot_cols(uq_rope, MLA_ROPE // 2).reshape(MLA_Q_RANK, MLA_HEADS, MLA_ROPE)
    uq_rot = jnp.pad(uq_rot, ((0, 0), (0, 0), (MLA_NOPE, tail)))
    wuq = jnp.concatenate([uq_pad.reshape(MLA_Q_RANK, MLA_W), uq_rot.reshape(MLA_Q_RANK, MLA_W)],
                          axis=1).astype(BF16)

    ukv = w_ukv.reshape(MLA_KV_RANK, MLA_HEADS, MLA_NOPE + MLA_V)
    wk_up = jnp.pad(ukv[..., :MLA_NOPE], ((0, 0), (0, 0), (0, HEAD_PAD - MLA_NOPE)))
    place = jnp.zeros((LANES, MLA_HEADS, HEAD_PAD), F32)
    r = jnp.arange(MLA_ROPE)
    place = place.at[r, :, MLA_NOPE + r].set(1.0)
    wk_ext = jnp.concatenate([wk_up.reshape(MLA_KV_RANK, MLA_W), place.reshape(LANES, MLA_W)],
                             axis=0).astype(BF16)
    wv_pad = jnp.pad(ukv[..., MLA_NOPE:], ((0, 0), (0, 0), (0, HEAD_PAD - MLA_V)))
    wv_pad = wv_pad.reshape(MLA_KV_RANK, MLA_W).astype(BF16)
    ones_row = jnp.zeros((MLA_HEADS, HEAD_PAD), F32).at[:, MLA_V].set(1.0).reshape(1, MLA_W)
    return (w_ext, wuq, wk_ext, wv_pad, q_norm.reshape(1, -1).astype(F32),
            kv_norm.reshape(1, -1).astype(F32), ones_row)


def _rotary_tables(n_ctx, n_lat):
    pos = jnp.arange(n_lat, dtype=jnp.int32)
    rows = pos // GRID_W
    cols = pos - rows * GRID_W

    def angles(p, half):
        inv = ROPE_BASE ** (-jnp.arange(half, dtype=F32) / half)
        return p.astype(F32)[:, None] * inv[None, :]

    def with_ctx(t, fill):
        return jnp.concatenate([jnp.full((n_ctx, t.shape[1]), fill, F32), t], axis=0)

    ang = angles(pos, RET_DK // 2)
    cr = with_ctx(jnp.tile(jnp.cos(ang), (1, 2 * RET_HEADS)), 1.0)
    sr = with_ctx(jnp.tile(jnp.sin(ang), (1, 2 * RET_HEADS)), 0.0)

    quarter = MLA_ROPE // 4
    ar, ac = angles(rows, quarter), angles(cols, quarter)
    cos32 = with_ctx(jnp.concatenate([jnp.cos(ar)] * 2 + [jnp.cos(ac)] * 2, axis=1), 1.0)
    sin32 = with_ctx(jnp.concatenate([jnp.sin(ar)] * 2 + [jnp.sin(ac)] * 2, axis=1), 0.0)
    n_all = n_ctx + n_lat
    scale = (MLA_NOPE + MLA_ROPE) ** -0.5 * math.log2(math.e)
    tail = jnp.zeros((n_all, HEAD_PAD - MLA_NOPE - MLA_ROPE), F32)
    cq = scale * jnp.concatenate([jnp.ones((n_all, MLA_NOPE), F32), cos32, tail], axis=1)
    sq = scale * jnp.concatenate([jnp.zeros((n_all, MLA_NOPE), F32), sin32, tail], axis=1)
    kpad = jnp.zeros((n_all, LANES - MLA_ROPE), F32)
    ck = jnp.concatenate([cos32, kpad], axis=1)
    sk = jnp.concatenate([sin32, kpad], axis=1)
    return cr, sr, cq, sq, ck, sk


def _retention_consts(ret_decay):
    log_gf = jax.nn.log_sigmoid(ret_decay[0].astype(F32))
    log_gb = jax.nn.log_sigmoid(ret_decay[1].astype(F32))
    idx = jnp.arange(RET_CHUNK, dtype=F32)
    rel = idx[:, None] - idx[None, :]
    dm_f = jnp.where(rel >= 0, jnp.exp(log_gf[:, None, None] * jnp.maximum(rel, 0.0)[None]), 0.0)
    dm_b = jnp.where(rel < 0, jnp.exp(log_gb[:, None, None] * jnp.maximum(-rel, 0.0)[None]), 0.0)

    def lanes(t):
        return jnp.repeat(t, RET_DK, axis=1)

    xi_f = lanes(jnp.exp(log_gf[None, :] * (idx + 1.0)[:, None]))
    zt_f = lanes(jnp.exp(log_gf[None, :] * (RET_CHUNK - 1 - idx)[:, None]))
    xi_b = lanes(jnp.exp(log_gb[None, :] * (RET_CHUNK - idx)[:, None]))
    zt_b = lanes(jnp.exp(log_gb[None, :] * idx[:, None]))
    cd = lambda lg: jnp.broadcast_to(jnp.exp(lg * RET_CHUNK)[:, None, None], (RET_HEADS, 1, RET_DV))
    return dm_f, dm_b, xi_f, zt_f, xi_b, zt_b, cd(log_gf), cd(log_gb)


def kernel(x, c, ctx, c_ctx, w_ada, b_ada, norm1, norm2, w_in, ret_decay, ret_gn, w_ret_o, conv_w,
           w_conv_o, mla_q_norm, w_uq, mla_kv_norm, w_ukv, w_mla_o, w_out, w_router, router_bias,
           w1, w3, w2, final_norm):
    n_batch, n_lat, d = x.shape
    n_ctx = ctx.shape[1]
    depth = w_ada.shape[0]
    t_all = n_ctx + n_lat
    assert n_ctx % ROW_BLOCK == 0 and n_lat % ROW_BLOCK == 0 and n_lat % GRID_W == 0
    nj = t_all // ROW_BLOCK
    ncb = n_ctx // ROW_BLOCK
    n_tok = n_batch * t_all
    assert n_tok % MOE_TILE == 0
    geom = dict(n_batch=n_batch, nj=nj, ncb=ncb)

    cc = jnp.concatenate([c, c_ctx[None, :]], axis=0)
    cc = jnp.pad(cc, ((0, -cc.shape[0] % 8), (0, 0)))
    mod = _ada_call(cc, w_ada, b_ada)[:, :n_batch + 1].reshape(depth, n_batch + 1, N_MOD, d)
    pick = jnp.stack([jnp.full((n_batch,), n_batch, jnp.int32),
                      jnp.arange(n_batch, dtype=jnp.int32)], axis=1).reshape(-1)
    mod = mod[:, pick]

    tabs = _rotary_tables(n_ctx, n_lat)
    wr_hi, wr_lo = _split_bf16(w_router.astype(F32).T)
    rbias = router_bias.astype(F32).reshape(N_EXPERTS, 1)
    xa = jnp.concatenate([ctx, x], axis=1).reshape(n_tok, d)

    out = None
    for l in range(depth):
        m = mod[l]
        rowvec = lambda v: v.reshape(2 * n_batch, 1, d)
        a1 = rowvec(norm1[l][None, :] * (1.0 + m[:, 1]))
        b1 = rowvec(m[:, 0])
        g1 = rowvec(m[:, 2])
        a2 = rowvec(norm2[l][None, :] * (1.0 + m[:, 4]))
        b2 = rowvec(m[:, 3])
        g2 = rowvec(m[:, 5])

        wts = _layer_weights(w_in[l], w_uq[l], w_ukv[l], mla_q_norm[l], mla_kv_norm[l])
        rq, rk, rv, rg, cb, cu, qm, km, vm, gt = _inproj_call(xa, a1, b1, tabs, wts, **geom)

        yf, yb = _ret_call(rq, rk, rv, _retention_consts(ret_decay[l]), **geom)
        om = _attn_call(qm, km, vm, n_ctx=n_ctx, **geom)

        merge_wts = (ret_gn[l].reshape(1, -1).astype(F32), conv_w[l].T.astype(F32),
                     w_ret_o[l].astype(BF16), w_conv_o[l].astype(BF16), w_mla_o[l].astype(BF16),
                     w_out[l].astype(BF16), wr_hi, wr_lo, rbias)
        xmid, h2, route = _merge_call(yf, yb, rg, cb, cu, om, gt, xa, g1, a2, b2, merge_wts,
                                      nj=nj, ncb=ncb)
        route = route.reshape(-1, ROUTE_ROWS, ROW_BLOCK)
        cls = route[:, 0, :].reshape(-1).astype(jnp.int32)
        gate_w = jnp.stack([route[:, 1, :].reshape(-1), route[:, 2, :].reshape(-1)], axis=-1)
        src, dest, gw, tile_ea, tile_eb, n_used = _dispatch(cls, gate_w, n_tok)
        hs = jnp.take(h2, src, axis=0)
        w13 = jnp.concatenate([w1[l], w3[l]], axis=-1).astype(BF16)
        f_sorted = _moe_call(tile_ea, tile_eb, n_used, hs, gw, w13, w2[l].astype(BF16))
        fg = jnp.take(f_sorted, dest, axis=0)

        if l < depth - 1:
            xa = _resid_call(xmid, fg, g2, nj=nj, ncb=ncb)
        else:
            out = _final_call(xmid, fg, g2, final_norm, **geom)
    return out.reshape(n_batch, n_lat, d)
```

```python
import functools
import math

import jax
import jax.numpy as jnp
from jax import lax
from jax.experimental import pallas as pl
from jax.experimental.pallas import tpu as pltpu

F32 = jnp.float32
BF16 = jnp.bfloat16

GRID_W = 64
RMS_EPS = 1e-6
ROPE_BASE = 10000.0
N_MOD = 6
RET_HEADS = 4
RET_DK = 64
RET_DV = 128
RET_CHUNK = 256
CONV_WIDTH = 512
MLA_HEADS = 8
MLA_Q_RANK = 384
MLA_KV_RANK = 256
MLA_NOPE = 64
MLA_ROPE = 32
MLA_V = 64
N_BRANCH = 3
N_EXPERTS = 16
N_GROUPS = 4
EXPERTS_PER_GROUP = N_EXPERTS // N_GROUPS
D_EXPERT = 512
IN_SIZES = (RET_HEADS * RET_DK, RET_HEADS * RET_DK, RET_HEADS * RET_DV, RET_HEADS * RET_DV,
            CONV_WIDTH, CONV_WIDTH, CONV_WIDTH, MLA_Q_RANK, MLA_KV_RANK, MLA_ROPE, 0)

LANES = 128
BF16_SUBLANES = 16
VMEM_LIMIT = 56 * 1024 * 1024

HEAD_PAD = LANES
ROW_BLOCK = 256
MOE_TILE = 256
ATTN_KEY_TILES = (768, 512, 256, 128)
ROUTE_ROWS = 8
PAIRS = [(a, b) for a in range(EXPERTS_PER_GROUP) for b in range(a + 1, EXPERTS_PER_GROUP)]
N_CLASSES = N_GROUPS * len(PAIRS)

RQ_W = RET_HEADS * RET_DK
RV_W = RET_HEADS * RET_DV
MLA_W = MLA_HEADS * HEAD_PAD
MLA_O = MLA_HEADS * MLA_V


def _const_spec(shape):
    nd = len(shape)
    return pl.BlockSpec(shape, lambda *_: (0,) * nd, pipeline_mode=pl.Buffered(1))


def _params(n_axes):
    return pltpu.CompilerParams(dimension_semantics=("arbitrary",) * n_axes,
                                vmem_limit_bytes=VMEM_LIMIT)


def _dot(a, b):
    return jnp.dot(a, b, preferred_element_type=F32)


def _split_bf16(a):
    hi = a.astype(BF16)
    lo = (a - hi.astype(F32)).astype(BF16)
    return hi, lo


def _silu(v):
    return v * jax.nn.sigmoid(v)


def _rms(v):
    return v * lax.rsqrt(jnp.mean(v * v, axis=-1, keepdims=True) + RMS_EPS)


def _ada_body(c_ref, w_ref, b_ref, o_ref):
    a_hi, a_lo = _split_bf16(_silu(c_ref[...]))
    w_hi, w_lo = _split_bf16(w_ref[...])
    o_ref[...] = _dot(a_hi, w_hi) + _dot(a_hi, w_lo) + _dot(a_lo, w_hi) + b_ref[...]


def _ada_call(cc, w_ada, b_ada):
    depth, d, nm = w_ada.shape
    rows = cc.shape[0]
    cb = nm // 4
    return pl.pallas_call(
        _ada_body,
        out_shape=jax.ShapeDtypeStruct((depth, rows, nm), F32),
        grid=(depth, nm // cb),
        in_specs=[pl.BlockSpec((rows, d), lambda l, n: (0, 0)),
                  pl.BlockSpec((None, d, cb), lambda l, n: (l, 0, n)),
                  pl.BlockSpec((None, 1, cb), lambda l, n: (l, 0, n))],
        out_specs=pl.BlockSpec((None, rows, cb), lambda l, n: (l, 0, n)),
        compiler_params=_params(2),
        name="ada_mod",
    )(cc, w_ada, b_ada.reshape(depth, 1, nm))


_O_RQ = 0
_O_RK = _O_RQ + RQ_W
_O_RV = _O_RK + RQ_W
_O_RG = _O_RV + RV_W
_O_CB = _O_RG + RV_W
_O_CC = _O_CB + CONV_WIDTH
_O_CX = _O_CC + CONV_WIDTH
_O_QD = _O_CX + CONV_WIDTH
_O_KVD = _O_QD + MLA_Q_RANK
_O_KR = _O_KVD + MLA_KV_RANK
_O_GT = _O_KR + LANES


def _rot_half(v, half):
    width = v.shape[1]
    lane = lax.broadcasted_iota(jnp.int32, v.shape, 1)
    first = (lane % (2 * half)) < half
    return jnp.where(first, pltpu.roll(v, width - half, 1), pltpu.roll(v, half, 1))


def _inproj_body(x_ref, a1_ref, b1_ref, cr_ref, sr_ref, cq_ref, sq_ref, ck_ref, sk_ref,
                 w_ref, wuq_ref, wk_ref, wv_ref, qn_ref, kvn_ref, ones_ref,
                 rq_ref, rk_ref, rv_ref, rg_ref, cb_ref, cu_ref, qm_ref, km_ref, vm_ref, gt_ref,
                 *, d_model):
    h = (_rms(x_ref[...]) * a1_ref[...] + b1_ref[...]).astype(BF16)

    def mm(off, width):
        return _dot(h, w_ref[:, off:off + width])

    cr = cr_ref[...]
    sr = sr_ref[...]
    q = mm(_O_RQ, RQ_W)
    rq_ref[...] = (q * cr + _rot_half(q, RET_DK // 2) * sr).astype(BF16)
    k = mm(_O_RK, RQ_W)
    rk_ref[...] = ((k * cr + _rot_half(k, RET_DK // 2) * sr) * (RET_DK ** -0.5)).astype(BF16)
    rv_ref[...] = mm(_O_RV, RV_W).astype(BF16)
    rg_ref[...] = _silu(mm(_O_RG, RV_W)).astype(BF16)
    cb_ref[...] = mm(_O_CB, CONV_WIDTH).astype(BF16)
    cu_ref[...] = (mm(_O_CC, CONV_WIDTH) * mm(_O_CX, CONV_WIDTH)).astype(BF16)

    qn = (_rms(mm(_O_QD, MLA_Q_RANK)) * qn_ref[...]).astype(BF16)
    cq = cq_ref[...]
    sq = sq_ref[...]
    for hd in range(MLA_HEADS):
        lo = hd * HEAD_PAD
        qa = _dot(qn, wuq_ref[:, lo:lo + HEAD_PAD])
        qm_ref[:, lo:lo + HEAD_PAD] = (qa * cq + _rot_half(qa, MLA_ROPE // 4) * sq).astype(BF16)

    kvn = (_rms(mm(_O_KVD, MLA_KV_RANK)) * kvn_ref[...]).astype(BF16)
    kr = mm(_O_KR, LANES)
    kr = (kr * ck_ref[...] + _rot_half(kr, MLA_ROPE // 4) * sk_ref[...]).astype(BF16)
    km_ref[...] = _dot(jnp.concatenate([kvn, kr], axis=-1), wk_ref[...]).astype(BF16)
    vm_ref[...] = (_dot(kvn, wv_ref[...]) + ones_ref[...]).astype(BF16)

    for br in range(N_BRANCH):
        gt_ref[:, br * d_model:(br + 1) * d_model] = jax.nn.sigmoid(
            mm(_O_GT + br * d_model, d_model)).astype(BF16)


def _inproj_call(xa, a1, b1, tabs, wts, *, n_batch, nj, ncb):
    rows, d = xa.shape
    tm = ROW_BLOCK
    row = lambda j, b: (b * nj + j, 0)
    mod = lambda j, b: (2 * b + (j >= ncb).astype(jnp.int32), 0, 0)
    tab = lambda j, b: (j, 0)
    in_specs = [pl.BlockSpec((tm, d), row),
                pl.BlockSpec((None, 1, d), mod), pl.BlockSpec((None, 1, d), mod)]
    in_specs += [pl.BlockSpec((tm, t.shape[1]), tab) for t in tabs]
    in_specs += [_const_spec(w.shape) for w in wts]
    widths = (RQ_W, RQ_W, RV_W, RV_W, CONV_WIDTH, CONV_WIDTH, MLA_W, MLA_W, MLA_W, N_BRANCH * d)
    return pl.pallas_call(
        functools.partial(_inproj_body, d_model=d),
        out_shape=[jax.ShapeDtypeStruct((rows, w), BF16) for w in widths],
        grid=(nj, n_batch),
        in_specs=in_specs,
        out_specs=[pl.BlockSpec((tm, w), row) for w in widths],
        compiler_params=_params(2),
        name="in_proj",
    )(xa, a1, b1, *tabs, *wts)


def _ret_direction(q_ref, k_ref, v_ref, y_ref, s_ref, dm_ref, xi_ref, zt_ref, cd_ref, chunk_order):
    states = [s_ref[hd] for hd in range(RET_HEADS)]
    for c in chunk_order:
        rows = slice(c * RET_CHUNK, (c + 1) * RET_CHUNK)
        q = q_ref[rows, :]
        k = k_ref[rows, :]
        v = v_ref[rows, :]
        qx = (q.astype(F32) * xi_ref[...]).astype(BF16)
        kz = (k.astype(F32) * zt_ref[...]).astype(BF16)
        for hd in range(RET_HEADS):
            ks = slice(hd * RET_DK, (hd + 1) * RET_DK)
            vs = slice(hd * RET_DV, (hd + 1) * RET_DV)
            vh = v[:, vs]
            sc = lax.dot_general(q[:, ks], k[:, ks], (((1,), (1,)), ((), ())),
                                 preferred_element_type=F32)
            inner = _dot((sc * dm_ref[hd]).astype(BF16), vh)
            cross = _dot(qx[:, ks], states[hd].astype(BF16))
            y_ref[rows, vs] = inner + cross
            upd = lax.dot_general(kz[:, ks], vh, (((0,), (0,)), ((), ())),
                                  preferred_element_type=F32)
            states[hd] = cd_ref[hd] * states[hd] + upd
    for hd in range(RET_HEADS):
        s_ref[hd] = states[hd]


def _ret_body(qf_ref, kf_ref, vf_ref, qb_ref, kb_ref, vb_ref,
              dmf_ref, dmb_ref, xif_ref, ztf_ref, xib_ref, ztb_ref, cdf_ref, cdb_ref,
              yf_ref, yb_ref, sf_ref, sb_ref):
    @pl.when(pl.program_id(1) == 0)
    def _():
        sf_ref[...] = jnp.zeros_like(sf_ref)
        sb_ref[...] = jnp.zeros_like(sb_ref)

    n_chunks = ROW_BLOCK // RET_CHUNK
    _ret_direction(qf_ref, kf_ref, vf_ref, yf_ref, sf_ref, dmf_ref, xif_ref, ztf_ref, cdf_ref,
                   range(n_chunks))
    _ret_direction(qb_ref, kb_ref, vb_ref, yb_ref, sb_ref, dmb_ref, xib_ref, ztb_ref, cdb_ref,
                   range(n_chunks - 1, -1, -1))


def _ret_call(rq, rk, rv, consts, *, n_batch, nj, ncb):
    rows = rq.shape[0]
    tm = ROW_BLOCK
    fwd = lambda b, s: (b * nj + s, 0)

    def bwd(b, s):
        return (b * nj + jnp.where(s < ncb, ncb - 1 - s, nj - 1 - (s - ncb)), 0)

    specs = []
    for im in (fwd, bwd):
        specs += [pl.BlockSpec((tm, RQ_W), im), pl.BlockSpec((tm, RQ_W), im),
                  pl.BlockSpec((tm, RV_W), im)]
    specs += [_const_spec(c.shape) for c in consts]
    return pl.pallas_call(
        _ret_body,
        out_shape=[jax.ShapeDtypeStruct((rows, RV_W), F32)] * 2,
        grid=(n_batch, nj),
        in_specs=specs,
        out_specs=[pl.BlockSpec((tm, RV_W), fwd), pl.BlockSpec((tm, RV_W), bwd)],
        scratch_shapes=[pltpu.VMEM((RET_HEADS, RET_DK, RET_DV), F32)] * 2,
        compiler_params=_params(2),
        name="retention",
    )(rq, rk, rv, rq, rk, rv, *consts)


def _pick_tile(n, candidates):
    for c in candidates:
        if n % c == 0:
            return c
    raise ValueError(f"no tile for {n}")


def _attn_body(q_ref, k_ref, v_ref, o_ref, s_ref, *, n_ctx, n_all, ncb):
    tq = q_ref.shape[0]
    heads = [slice(hh * HEAD_PAD, (hh + 1) * HEAD_PAD) for hh in range(2)]

    def attend(n_keys, tk):
        nt = n_keys // tk
        qs = [q_ref[:, hs] for hs in heads]

        def qk(t, mrun):
            r0 = pl.multiple_of(t * tk, tk)
            out = []
            for hh, hs in enumerate(heads):
                s = lax.dot_general(qs[hh], k_ref[pl.ds(r0, tk), hs], (((1,), (1,)), ((), ())),
                                    preferred_element_type=F32)
                s_ref[hh, t, :, 0:tk] = s
                m = mrun[hh]
                for cc in range(tk // LANES):
                    m = jnp.maximum(m, s[:, cc * LANES:(cc + 1) * LANES])
                out.append(m)
            return tuple(out)

        mrun = lax.fori_loop(0, nt, qk, tuple(jnp.full((tq, LANES), -jnp.inf, F32) for _ in heads),
                             unroll=True)
        mrow = [jnp.max(m, axis=-1, keepdims=True) for m in mrun]

        def pv(t, accs):
            r0 = pl.multiple_of(t * tk, tk)
            out = []
            for hh, hs in enumerate(heads):
                p = jnp.exp2(s_ref[hh, t, :, 0:tk] - mrow[hh]).astype(BF16)
                out.append(accs[hh] + _dot(p, v_ref[pl.ds(r0, tk), hs]))
            return tuple(out)

        accs = lax.fori_loop(0, nt, pv, tuple(jnp.zeros((tq, HEAD_PAD), F32) for _ in heads),
                             unroll=True)
        o_ref[...] = jnp.concatenate([a[:, :MLA_V] / a[:, MLA_V:MLA_V + 1] for a in accs],
                                     axis=-1).astype(BF16)

    j = pl.program_id(2)

    @pl.when(j < ncb)
    def _():
        attend(n_ctx, _pick_tile(n_ctx, (256, 128)))

    @pl.when(j >= ncb)
    def _():
        attend(n_all, _pick_tile(n_all, ATTN_KEY_TILES))


def _attn_call(qm, km, vm, *, n_batch, nj, ncb, n_ctx):
    rows = qm.shape[0]
    tm = ROW_BLOCK
    t_all = nj * tm
    tk = _pick_tile(t_all, ATTN_KEY_TILES)
    qmap = lambda b, hp, j: (b * nj + j, hp)
    kmap = lambda b, hp, j: (b, hp)
    return pl.pallas_call(
        functools.partial(_attn_body, n_ctx=n_ctx, n_all=t_all, ncb=ncb),
        out_shape=jax.ShapeDtypeStruct((rows, MLA_O), BF16),
        grid=(n_batch, MLA_HEADS // 2, nj),
        in_specs=[pl.BlockSpec((tm, 2 * HEAD_PAD), qmap),
                  pl.BlockSpec((t_all, 2 * HEAD_PAD), kmap),
                  pl.BlockSpec((t_all, 2 * HEAD_PAD), kmap)],
        out_specs=pl.BlockSpec((tm, 2 * MLA_V), qmap),
        scratch_shapes=[pltpu.VMEM((2, t_all // tk, tm, tk), F32)],
        compiler_params=_params(3),
        name="mla_attention",
    )(qm, km, vm)


def _merge_body(yf_ref, yb_ref, rg_ref, cb_ref, cu_ref, cup_ref, cun_ref, om_ref, gt_ref, x_ref,
                g1_ref, a2_ref, b2_ref, gn_ref, cw_ref, wro_ref, wco_ref, wmo_ref, wout_ref,
                wrh_ref, wrl_ref, rb_ref, xmid_ref, h2_ref, rt_ref, *, nj, ncb, d_model):
    tm = x_ref.shape[0]
    j = pl.program_id(0) % nj
    seg_first = jnp.logical_or(j == 0, j == ncb)
    seg_last = jnp.logical_or(j == ncb - 1, j == nj - 1)

    y = yf_ref[...] + yb_ref[...]
    yn = jnp.concatenate([_rms(y[:, hd * RET_DV:(hd + 1) * RET_DV]) for hd in range(RET_HEADS)],
                         axis=-1) * gn_ref[...]
    y_ret = _dot((rg_ref[...].astype(F32) * yn).astype(BF16), wro_ref[...])

    u = cu_ref[...].astype(F32)
    ridx = lax.broadcasted_iota(jnp.int32, u.shape, 0)
    prev_row = cup_ref[...].astype(F32)[BF16_SUBLANES - 1:, :] * jnp.where(seg_first, 0.0, 1.0)
    next_row = cun_ref[...].astype(F32)[0:1, :] * jnp.where(seg_last, 0.0, 1.0)
    u_prev = jnp.where(ridx == 0, prev_row, pltpu.roll(u, 1, 0))
    u_next = jnp.where(ridx == tm - 1, next_row, pltpu.roll(u, tm - 1, 0))
    conv = u_prev * cw_ref[0:1, :] + u * cw_ref[1:2, :] + u_next * cw_ref[2:3, :]
    y_conv = _dot((cb_ref[...].astype(F32) * conv).astype(BF16), wco_ref[...])

    y_mla = _dot(om_ref[...], wmo_ref[...])

    merged = (gt_ref[:, 0:d_model].astype(F32) * y_ret
              + gt_ref[:, d_model:2 * d_model].astype(F32) * y_conv
              + gt_ref[:, 2 * d_model:3 * d_model].astype(F32) * y_mla)
    x_mid = x_ref[...] + g1_ref[...] * _dot(merged.astype(BF16), wout_ref[...])
    xmid_ref[...] = x_mid

    h2 = _rms(x_mid) * a2_ref[...] + b2_ref[...]
    h_hi, h_lo = _split_bf16(h2)
    logits = _dot(h_hi, wrh_ref[...]) + _dot(h_hi, wrl_ref[...]) + _dot(h_lo, wrh_ref[...])
    cls, w_lo, w_hi = _route_rows(jnp.transpose(logits)[:N_EXPERTS, :], rb_ref[...])
    rt_ref[...] = jnp.concatenate([cls.astype(F32), jnp.zeros((ROUTE_ROWS - 1, tm), F32)], axis=0)
    parts = [p.astype(F32) for w in (w_lo, w_hi) for p in _split_bf16(w)]
    rec = jnp.concatenate(parts + [jnp.zeros((LANES - len(parts), tm), F32)], axis=0)
    h2_ref[:, :d_model] = h_hi
    h2_ref[:, d_model:] = jnp.transpose(rec).astype(BF16)


def _top2_of4(v):
    def first_max(rows):
        best, idx = rows[0], jnp.zeros(rows[0].shape, jnp.int32)
        for e in range(1, len(rows)):
            better = rows[e] > best
            idx = jnp.where(better, e, idx)
            best = jnp.where(better, rows[e], best)
        return best, idx

    b1, i1 = first_max(v)
    b2, i2 = first_max([jnp.where(i1 == e, -jnp.inf, v[e]) for e in range(len(v))])
    return i1, i2, b1, b2


def _route_rows(logits_t, bias):
    scores = jax.nn.sigmoid(logits_t)
    biased = scores + bias
    row = lambda a, e: a[e:e + 1, :]
    best = None
    for g in range(N_GROUPS):
        v = [row(biased, g * EXPERTS_PER_GROUP + e) for e in range(EXPERTS_PER_GROUP)]
        i1, i2, b1, b2 = _top2_of4(v)
        cand = (b1 + b2, jnp.full(i1.shape, g, jnp.int32), i1, i2)
        if best is None:
            best = cand
        else:
            better = cand[0] > best[0]
            best = tuple(jnp.where(better, c, o) for c, o in zip(cand, best))
    _, g_sel, i1, i2 = best
    lo = jnp.minimum(i1, i2)
    hi = jnp.maximum(i1, i2)
    e_lo = g_sel * EXPERTS_PER_GROUP + lo
    e_hi = g_sel * EXPERTS_PER_GROUP + hi
    s_lo = jnp.zeros_like(best[0])
    s_hi = jnp.zeros_like(best[0])
    for e in range(N_EXPERTS):
        s_lo = jnp.where(e_lo == e, row(scores, e), s_lo)
        s_hi = jnp.where(e_hi == e, row(scores, e), s_hi)
    total = s_lo + s_hi
    pair_base = jnp.where(lo == 0, 0, jnp.where(lo == 1, 3, 5))
    cls = g_sel * len(PAIRS) + pair_base + (hi - lo - 1)
    return cls, s_lo / total, s_hi / total


def _merge_call(yf, yb, rg, cb, cu, om, gt, xa, g1, a2, b2, wts, *, nj, ncb):
    rows, d = xa.shape
    tm = ROW_BLOCK
    nb = rows // tm
    halo = BF16_SUBLANES
    per_blk = tm // halo
    row = lambda i: (i, 0)
    mod = lambda i: (2 * (i // nj) + ((i % nj) >= ncb).astype(jnp.int32), 0, 0)
    prev = lambda i: (jnp.maximum(i * per_blk - 1, 0), 0)
    nxt = lambda i: (jnp.minimum((i + 1) * per_blk, rows // halo - 1), 0)
    in_specs = [pl.BlockSpec((tm, RV_W), row), pl.BlockSpec((tm, RV_W), row),
                pl.BlockSpec((tm, RV_W), row),
                pl.BlockSpec((tm, CONV_WIDTH), row), pl.BlockSpec((tm, CONV_WIDTH), row),
                pl.BlockSpec((halo, CONV_WIDTH), prev), pl.BlockSpec((halo, CONV_WIDTH), nxt),
                pl.BlockSpec((tm, MLA_O), row), pl.BlockSpec((tm, N_BRANCH * d), row),
                pl.BlockSpec((tm, d), row),
                pl.BlockSpec((None, 1, d), mod), pl.BlockSpec((None, 1, d), mod),
                pl.BlockSpec((None, 1, d), mod)]
    in_specs += [_const_spec(w.shape) for w in wts]
    return pl.pallas_call(
        functools.partial(_merge_body, nj=nj, ncb=ncb, d_model=d),
        out_shape=[jax.ShapeDtypeStruct((rows, d), F32), jax.ShapeDtypeStruct((rows, d + LANES), BF16),
                   jax.ShapeDtypeStruct((nb * ROUTE_ROWS, tm), F32)],
        grid=(nb,),
        in_specs=in_specs,
        out_specs=[pl.BlockSpec((tm, d), row), pl.BlockSpec((tm, d + LANES), row),
                   pl.BlockSpec((ROUTE_ROWS, tm), row)],
        compiler_params=_params(1),
        name="merge_out_proj",
    )(yf, yb, rg, cb, cu, cu, cu, om, gt, xa, g1, a2, b2, *wts)


def _moe_body(ea_ref, eb_ref, nu_ref, h_ref, w13a_ref, w2a_ref, w13b_ref, w2b_ref, o_ref):
    t = pl.program_id(0)

    @pl.when(t < nu_ref[0])
    def _():
        d_model = o_ref.shape[1]
        h = h_ref[:, :d_model]
        gw = h_ref[:, d_model:].astype(F32)

        def expert(w13_ref, w2_ref, wt):
            a = _dot(h, w13_ref[...])
            act = _silu(a[:, :D_EXPERT]) * a[:, D_EXPERT:] * wt
            return _dot(act.astype(BF16), w2_ref[...])

        o_ref[...] = (expert(w13a_ref, w2a_ref, gw[:, 0:1] + gw[:, 1:2])
                      + expert(w13b_ref, w2b_ref, gw[:, 2:3] + gw[:, 3:4])).astype(BF16)

    @pl.when(t >= nu_ref[0])
    def _():
        o_ref[...] = jnp.zeros_like(o_ref)


def _moe_call(tile_ea, tile_eb, n_used, hs, w13, w2):
    npad = hs.shape[0]
    d = w13.shape[1]
    tmo = MOE_TILE
    row = lambda t, ea, eb, nu: (t, 0)
    wa = lambda t, ea, eb, nu: (ea[t], 0, 0)
    wb = lambda t, ea, eb, nu: (eb[t], 0, 0)
    grid_spec = pltpu.PrefetchScalarGridSpec(
        num_scalar_prefetch=3,
        grid=(npad // tmo,),
        in_specs=[pl.BlockSpec((tmo, d + LANES), row),
                  pl.BlockSpec((None, d, 2 * D_EXPERT), wa), pl.BlockSpec((None, D_EXPERT, d), wa),
                  pl.BlockSpec((None, d, 2 * D_EXPERT), wb), pl.BlockSpec((None, D_EXPERT, d), wb)],
        out_specs=pl.BlockSpec((tmo, d), row))
    return pl.pallas_call(
        _moe_body,
        out_shape=jax.ShapeDtypeStruct((npad, d), BF16),
        grid_spec=grid_spec,
        compiler_params=_params(1),
        name="moe_experts",
    )(tile_ea, tile_eb, n_used, hs, w13, w2, w13, w2)


def _dispatch(cls, n_tok):
    tmo = MOE_TILE
    n_tiles = n_tok // tmo + N_CLASSES
    onehot = (cls[:, None] == jnp.arange(N_CLASSES, dtype=jnp.int32)[None, :]).astype(jnp.int32)
    ranks = jnp.cumsum(onehot, axis=0) - onehot
    rank = jnp.sum(ranks * onehot, axis=1)
    counts = jnp.sum(onehot, axis=0)
    tiles_per = (counts + tmo - 1) // tmo
    tile_end = jnp.cumsum(tiles_per)
    offs = (tile_end - tiles_per) * tmo
    dest = offs[cls] + rank
    npad = n_tiles * tmo
    src = jnp.zeros((npad,), jnp.int32).at[dest].set(jnp.arange(n_tok, dtype=jnp.int32))
    tile_ids = jnp.arange(n_tiles, dtype=jnp.int32)
    tile_cls = jnp.sum((tile_end[None, :] <= tile_ids[:, None]).astype(jnp.int32), axis=1)
    tile_cls = jnp.minimum(tile_cls, N_CLASSES - 1)
    pa = jnp.asarray([p[0] for p in PAIRS], jnp.int32)
    pb = jnp.asarray([p[1] for p in PAIRS], jnp.int32)
    base = (tile_cls // len(PAIRS)) * EXPERTS_PER_GROUP
    tile_ea = base + pa[tile_cls % len(PAIRS)]
    tile_eb = base + pb[tile_cls % len(PAIRS)]
    n_used = tile_end[-1:].astype(jnp.int32)
    return src, dest, tile_ea, tile_eb, n_used


def _resid_body(x_ref, f_ref, g2_ref, o_ref):
    o_ref[...] = x_ref[...] + g2_ref[...] * f_ref[...].astype(F32)


def _final_body(x_ref, f_ref, g2_ref, fn_ref, o_ref):
    o_ref[...] = _rms(x_ref[...] + g2_ref[...] * f_ref[...].astype(F32)) * fn_ref[...]


def _resid_call(xmid, fg, g2, *, nj, ncb):
    rows, d = xmid.shape
    tm = ROW_BLOCK
    row = lambda i: (i, 0)
    mod = lambda i: (2 * (i // nj) + ((i % nj) >= ncb).astype(jnp.int32), 0, 0)
    return pl.pallas_call(
        _resid_body,
        out_shape=jax.ShapeDtypeStruct((rows, d), F32),
        grid=(rows // tm,),
        in_specs=[pl.BlockSpec((tm, d), row), pl.BlockSpec((tm, d), row),
                  pl.BlockSpec((None, 1, d), mod)],
        out_specs=pl.BlockSpec((tm, d), row),
        compiler_params=_params(1),
        name="moe_residual",
    )(xmid, fg, g2)


def _final_call(xmid, fg, g2, final_norm, *, n_batch, nj, ncb):
    rows, d = xmid.shape
    tm = ROW_BLOCK
    njl = nj - ncb
    src = lambda b, j: (b * nj + ncb + j, 0)
    return pl.pallas_call(
        _final_body,
        out_shape=jax.ShapeDtypeStruct((n_batch * njl * tm, d), F32),
        grid=(n_batch, njl),
        in_specs=[pl.BlockSpec((tm, d), src), pl.BlockSpec((tm, d), src),
                  pl.BlockSpec((None, 1, d), lambda b, j: (2 * b + 1, 0, 0)),
                  _const_spec((1, d))],
        out_specs=pl.BlockSpec((tm, d), lambda b, j: (b * njl + j, 0)),
        compiler_params=_params(2),
        name="final_norm",
    )(xmid, fg, g2, final_norm.reshape(1, d))


def _split_cols(w, sizes):
    out, off = [], 0
    for s in sizes:
        out.append(w[:, off:off + s])
        off += s
    out.append(w[:, off:])
    return out


def _layer_weights(w_in, w_uq, w_ukv, q_norm, kv_norm):
    d = w_in.shape[0]
    wq, wk, wv, wg, wcb, wcc, wcx, wqd, wkvd, wkr, wgate = _split_cols(w_in, IN_SIZES[:-1])
    pad = lambda w: jnp.pad(w, ((0, 0), (0, LANES - w.shape[1])))
    w_ext = jnp.concatenate(
        [wq, wk, wv, wg, wcb, wcc, wcx, wqd, wkvd, pad(wkr), wgate], axis=1).astype(BF16)

    uq = w_uq.reshape(MLA_Q_RANK, MLA_HEADS, MLA_NOPE + MLA_ROPE)
    tail = HEAD_PAD - MLA_NOPE - MLA_ROPE
    uq_pad = jnp.pad(uq, ((0, 0), (0, 0), (0, tail)))
    wuq = uq_pad.reshape(MLA_Q_RANK, MLA_W).astype(BF16)

    ukv = w_ukv.reshape(MLA_KV_RANK, MLA_HEADS, MLA_NOPE + MLA_V)
    wk_up = jnp.pad(ukv[..., :MLA_NOPE], ((0, 0), (0, 0), (0, HEAD_PAD - MLA_NOPE)))
    place = jnp.zeros((LANES, MLA_HEADS, HEAD_PAD), F32)
    r = jnp.arange(MLA_ROPE)
    place = place.at[r, :, MLA_NOPE + r].set(1.0)
    wk_ext = jnp.concatenate([wk_up.reshape(MLA_KV_RANK, MLA_W), place.reshape(LANES, MLA_W)],
                             axis=0).astype(BF16)
    wv_pad = jnp.pad(ukv[..., MLA_NOPE:], ((0, 0), (0, 0), (0, HEAD_PAD - MLA_V)))
    wv_pad = wv_pad.reshape(MLA_KV_RANK, MLA_W).astype(BF16)
    ones_row = jnp.zeros((MLA_HEADS, HEAD_PAD), F32).at[:, MLA_V].set(1.0).reshape(1, MLA_W)
    return (w_ext, wuq, wk_ext, wv_pad, q_norm.reshape(1, -1).astype(F32),
            kv_norm.reshape(1, -1).astype(F32), ones_row)


def _rotary_tables(n_ctx, n_lat):
    pos = jnp.arange(n_lat, dtype=jnp.int32)
    rows = pos // GRID_W
    cols = pos - rows * GRID_W

    def angles(p, half):
        inv = ROPE_BASE ** (-jnp.arange(half, dtype=F32) / half)
        return p.astype(F32)[:, None] * inv[None, :]

    def with_ctx(t, fill):
        return jnp.concatenate([jnp.full((n_ctx, t.shape[1]), fill, F32), t], axis=0)

    ang = angles(pos, RET_DK // 2)
    cr = with_ctx(jnp.tile(jnp.cos(ang), (1, 2 * RET_HEADS)), 1.0)
    sr = with_ctx(jnp.tile(jnp.concatenate([-jnp.sin(ang), jnp.sin(ang)], axis=1), (1, RET_HEADS)), 0.0)

    quarter = MLA_ROPE // 4
    ar, ac = angles(rows, quarter), angles(cols, quarter)
    cos32 = with_ctx(jnp.concatenate([jnp.cos(ar)] * 2 + [jnp.cos(ac)] * 2, axis=1), 1.0)
    sin32 = with_ctx(jnp.concatenate([-jnp.sin(ar), jnp.sin(ar), -jnp.sin(ac), jnp.sin(ac)], axis=1), 0.0)
    n_all = n_ctx + n_lat
    scale = (MLA_NOPE + MLA_ROPE) ** -0.5 * math.log2(math.e)
    tail = jnp.zeros((n_all, HEAD_PAD - MLA_NOPE - MLA_ROPE), F32)
    cq = scale * jnp.concatenate([jnp.ones((n_all, MLA_NOPE), F32), cos32, tail], axis=1)
    sq = scale * jnp.concatenate([jnp.zeros((n_all, MLA_NOPE), F32), sin32, tail], axis=1)
    kpad = jnp.zeros((n_all, LANES - MLA_ROPE), F32)
    ck = jnp.concatenate([cos32, kpad], axis=1)
    sk = jnp.concatenate([sin32, kpad], axis=1)
    return cr, sr, cq, sq, ck, sk


def _retention_consts(ret_decay):
    log_gf = jax.nn.log_sigmoid(ret_decay[0].astype(F32))
    log_gb = jax.nn.log_sigmoid(ret_decay[1].astype(F32))
    idx = jnp.arange(RET_CHUNK, dtype=F32)
    rel = idx[:, None] - idx[None, :]
    dm_f = jnp.where(rel >= 0, jnp.exp(log_gf[:, None, None] * jnp.maximum(rel, 0.0)[None]), 0.0)
    dm_b = jnp.where(rel < 0, jnp.exp(log_gb[:, None, None] * jnp.maximum(-rel, 0.0)[None]), 0.0)

    def lanes(t):
        return jnp.repeat(t, RET_DK, axis=1)

    xi_f = lanes(jnp.exp(log_gf[None, :] * (idx + 1.0)[:, None]))
    zt_f = lanes(jnp.exp(log_gf[None, :] * (RET_CHUNK - 1 - idx)[:, None]))
    xi_b = lanes(jnp.exp(log_gb[None, :] * (RET_CHUNK - idx)[:, None]))
    zt_b = lanes(jnp.exp(log_gb[None, :] * idx[:, None]))
    cd = lambda lg: jnp.broadcast_to(jnp.exp(lg * RET_CHUNK)[:, None, None], (RET_HEADS, 1, RET_DV))
    return dm_f, dm_b, xi_f, zt_f, xi_b, zt_b, cd(log_gf), cd(log_gb)


def kernel(x, c, ctx, c_ctx, w_ada, b_ada, norm1, norm2, w_in, ret_decay, ret_gn, w_ret_o, conv_w,
           w_conv_o, mla_q_norm, w_uq, mla_kv_norm, w_ukv, w_mla_o, w_out, w_router, router_bias,
           w1, w3, w2, final_norm):
    n_batch, n_lat, d = x.shape
    n_ctx = ctx.shape[1]
    depth = w_ada.shape[0]
    t_all = n_ctx + n_lat
    assert n_ctx % ROW_BLOCK == 0 and n_lat % ROW_BLOCK == 0 and n_lat % GRID_W == 0
    nj = t_all // ROW_BLOCK
    ncb = n_ctx // ROW_BLOCK
    n_tok = n_batch * t_all
    assert n_tok % MOE_TILE == 0
    geom = dict(n_batch=n_batch, nj=nj, ncb=ncb)

    cc = jnp.concatenate([c, c_ctx[None, :]], axis=0)
    cc = jnp.pad(cc, ((0, -cc.shape[0] % 8), (0, 0)))
    mod = _ada_call(cc, w_ada, b_ada)[:, :n_batch + 1].reshape(depth, n_batch + 1, N_MOD, d)
    pick = jnp.stack([jnp.full((n_batch,), n_batch, jnp.int32),
                      jnp.arange(n_batch, dtype=jnp.int32)], axis=1).reshape(-1)
    mod = mod[:, pick]

    tabs = _rotary_tables(n_ctx, n_lat)
    wr_hi, wr_lo = _split_bf16(jnp.pad(w_router.astype(F32), ((0, 0), (0, LANES - N_EXPERTS))))
    rbias = router_bias.astype(F32).reshape(N_EXPERTS, 1)
    xa = jnp.concatenate([ctx, x], axis=1).reshape(n_tok, d)

    out = None
    for l in range(depth):
        m = mod[l]
        rowvec = lambda v: v.reshape(2 * n_batch, 1, d)
        a1 = rowvec(norm1[l][None, :] * (1.0 + m[:, 1]))
        b1 = rowvec(m[:, 0])
        g1 = rowvec(m[:, 2])
        a2 = rowvec(norm2[l][None, :] * (1.0 + m[:, 4]))
        b2 = rowvec(m[:, 3])
        g2 = rowvec(m[:, 5])

        wts = _layer_weights(w_in[l], w_uq[l], w_ukv[l], mla_q_norm[l], mla_kv_norm[l])
        rq, rk, rv, rg, cb, cu, qm, km, vm, gt = _inproj_call(xa, a1, b1, tabs, wts, **geom)

        yf, yb = _ret_call(rq, rk, rv, _retention_consts(ret_decay[l]), **geom)
        om = _attn_call(qm, km, vm, n_ctx=n_ctx, **geom)

        merge_wts = (ret_gn[l].reshape(1, -1).astype(F32), conv_w[l].T.astype(F32),
                     w_ret_o[l].astype(BF16), w_conv_o[l].astype(BF16), w_mla_o[l].astype(BF16),
                     w_out[l].astype(BF16), wr_hi, wr_lo, rbias)
        xmid, h2, route = _merge_call(yf, yb, rg, cb, cu, om, gt, xa, g1, a2, b2, merge_wts,
                                      nj=nj, ncb=ncb)
        route = route.reshape(-1, ROUTE_ROWS, ROW_BLOCK)
        cls = route[:, 0, :].reshape(-1).astype(jnp.int32)
        src, dest, tile_ea, tile_eb, n_used = _dispatch(cls, n_tok)
        hs = jnp.take(h2, src, axis=0)
        w13 = jnp.concatenate([w1[l], w3[l]], axis=-1).astype(BF16)
        f_sorted = _moe_call(tile_ea, tile_eb, n_used, hs, w13, w2[l].astype(BF16))
        fg = jnp.take(f_sorted, dest, axis=0)

        if l < depth - 1:
            xa = _resid_call(xmid, fg, g2, nj=nj, ncb=ncb)
        else:
            out = _final_call(xmid, fg, g2, final_norm, **geom)
    return out.reshape(n_batch, n_lat, d)
```

```python
import functools
import math

import jax
import jax.numpy as jnp
from jax import lax
from jax.experimental import pallas as pl
from jax.experimental.pallas import tpu as pltpu

F32 = jnp.float32
BF16 = jnp.bfloat16

GRID_W = 64
RMS_EPS = 1e-6
ROPE_BASE = 10000.0
N_MOD = 6
RET_HEADS = 4
RET_DK = 64
RET_DV = 128
RET_CHUNK = 256
CONV_WIDTH = 512
MLA_HEADS = 8
MLA_Q_RANK = 384
MLA_KV_RANK = 256
MLA_NOPE = 64
MLA_ROPE = 32
MLA_V = 64
N_BRANCH = 3
N_EXPERTS = 16
N_GROUPS = 4
EXPERTS_PER_GROUP = N_EXPERTS // N_GROUPS
D_EXPERT = 512
IN_SIZES = (RET_HEADS * RET_DK, RET_HEADS * RET_DK, RET_HEADS * RET_DV, RET_HEADS * RET_DV,
            CONV_WIDTH, CONV_WIDTH, CONV_WIDTH, MLA_Q_RANK, MLA_KV_RANK, MLA_ROPE, 0)

LANES = 128
BF16_SUBLANES = 16
VMEM_LIMIT = 56 * 1024 * 1024

HEAD_PAD = LANES
ROW_BLOCK = 256
MOE_TILE = 256
ATTN_KEY_TILES = (768, 512, 256, 128)
ROUTE_ROWS = 8
PAIRS = [(a, b) for a in range(EXPERTS_PER_GROUP) for b in range(a + 1, EXPERTS_PER_GROUP)]
N_CLASSES = N_GROUPS * len(PAIRS)

RQ_W = RET_HEADS * RET_DK
RV_W = RET_HEADS * RET_DV
MLA_W = MLA_HEADS * HEAD_PAD
MLA_O = MLA_HEADS * MLA_V


def _const_spec(shape):
    nd = len(shape)
    return pl.BlockSpec(shape, lambda *_: (0,) * nd, pipeline_mode=pl.Buffered(1))


def _params(n_axes):
    return pltpu.CompilerParams(dimension_semantics=("arbitrary",) * n_axes,
                                vmem_limit_bytes=VMEM_LIMIT)


def _dot(a, b):
    return jnp.dot(a, b, preferred_element_type=F32)


def _split_bf16(a):
    hi = a.astype(BF16)
    lo = (a - hi.astype(F32)).astype(BF16)
    return hi, lo


def _silu(v):
    return v * jax.nn.sigmoid(v)


def _rms(v):
    return v * lax.rsqrt(jnp.mean(v * v, axis=-1, keepdims=True) + RMS_EPS)


def _ada_body(c_ref, w_ref, b_ref, o_ref):
    a_hi, a_lo = _split_bf16(_silu(c_ref[...]))
    w_hi, w_lo = _split_bf16(w_ref[...])
    o_ref[...] = _dot(a_hi, w_hi) + _dot(a_hi, w_lo) + _dot(a_lo, w_hi) + b_ref[...]


def _ada_call(cc, w_ada, b_ada):
    depth, d, nm = w_ada.shape
    rows = cc.shape[0]
    cb = nm // 4
    return pl.pallas_call(
        _ada_body,
        out_shape=jax.ShapeDtypeStruct((depth, rows, nm), F32),
        grid=(depth, nm // cb),
        in_specs=[pl.BlockSpec((rows, d), lambda l, n: (0, 0)),
                  pl.BlockSpec((None, d, cb), lambda l, n: (l, 0, n)),
                  pl.BlockSpec((None, 1, cb), lambda l, n: (l, 0, n))],
        out_specs=pl.BlockSpec((None, rows, cb), lambda l, n: (l, 0, n)),
        compiler_params=_params(2),
        name="ada_mod",
    )(cc, w_ada, b_ada.reshape(depth, 1, nm))


_O_RQ = 0
_O_RK = _O_RQ + RQ_W
_O_RV = _O_RK + RQ_W
_O_RG = _O_RV + RV_W
_O_CB = _O_RG + RV_W
_O_CC = _O_CB + CONV_WIDTH
_O_CX = _O_CC + CONV_WIDTH
_O_QD = _O_CX + CONV_WIDTH
_O_KVD = _O_QD + MLA_Q_RANK
_O_KR = _O_KVD + MLA_KV_RANK
_O_GT = _O_KR + LANES


def _rot_half(v, half):
    width = v.shape[1]
    lane = lax.broadcasted_iota(jnp.int32, v.shape, 1)
    first = (lane % (2 * half)) < half
    return jnp.where(first, pltpu.roll(v, width - half, 1), pltpu.roll(v, half, 1))


def _inproj_body(*refs, d_model, fused):
    if fused:
        xm_ref, f_ref, g2_ref = refs[:3]
        refs = refs[3:]
        xo_ref = refs[-1]
        x = xm_ref[...] + g2_ref[...] * f_ref[...].astype(F32)
        xo_ref[...] = x
    else:
        x = refs[0][...]
        refs = refs[1:]
    (a1_ref, b1_ref, cr_ref, sr_ref, cm_ref, sm_ref, w_ref, wuq_ref, wk_ref, wv_ref, qn_ref, kvn_ref,
     ones_ref, rq_ref, rk_ref, rv_ref, rg_ref, cb_ref, cu_ref, qm_ref, km_ref, vm_ref, gt_ref) = refs[:23]
    h = (_rms(x) * a1_ref[...] + b1_ref[...]).astype(BF16)

    def mm(off, width):
        return _dot(h, w_ref[:, off:off + width])

    cr = jnp.concatenate([cr_ref[...]] * (RQ_W // cr_ref.shape[1]), axis=1)
    sr = jnp.concatenate([sr_ref[...]] * (RQ_W // sr_ref.shape[1]), axis=1)
    q = mm(_O_RQ, RQ_W)
    rq_ref[...] = (q * cr + _rot_half(q, RET_DK // 2) * sr).astype(BF16)
    k = mm(_O_RK, RQ_W)
    rk_ref[...] = ((k * cr + _rot_half(k, RET_DK // 2) * sr) * (RET_DK ** -0.5)).astype(BF16)
    rv_ref[...] = mm(_O_RV, RV_W).astype(BF16)
    rg_ref[...] = _silu(mm(_O_RG, RV_W)).astype(BF16)
    cb_ref[...] = mm(_O_CB, CONV_WIDTH).astype(BF16)
    cu_ref[...] = (mm(_O_CC, CONV_WIDTH) * mm(_O_CX, CONV_WIDTH)).astype(BF16)

    cm = cm_ref[...]
    sm = sm_ref[...]
    qn = (_rms(mm(_O_QD, MLA_Q_RANK)) * qn_ref[...]).astype(BF16)
    for hd in range(MLA_HEADS):
        lo = hd * HEAD_PAD
        qa = _dot(qn, wuq_ref[:, lo:lo + HEAD_PAD])
        qm_ref[:, lo:lo + HEAD_PAD] = (qa * cm + _rot_half(qa, MLA_ROPE // 4) * sm).astype(BF16)

    kvn = (_rms(mm(_O_KVD, MLA_KV_RANK)) * kvn_ref[...]).astype(BF16)
    kr = mm(_O_KR, HEAD_PAD)
    kr = kr * cm + _rot_half(kr, MLA_ROPE // 4) * sm
    kn = _dot(kvn, wk_ref[...])
    for hd in range(MLA_HEADS):
        lo = hd * HEAD_PAD
        km_ref[:, lo:lo + HEAD_PAD] = (kn[:, lo:lo + HEAD_PAD] + kr).astype(BF16)
    vm_ref[...] = (_dot(kvn, wv_ref[...]) + ones_ref[...]).astype(BF16)

    for br in range(N_BRANCH):
        gt_ref[:, br * d_model:(br + 1) * d_model] = jax.nn.sigmoid(
            mm(_O_GT + br * d_model, d_model)).astype(BF16)


def _inproj_call(x_parts, a1, b1, tabs, wts, *, n_batch, nj, ncb):
    fused = len(x_parts) == 3
    rows, d = x_parts[0].shape
    tm = ROW_BLOCK
    row = lambda j, b: (b * nj + j, 0)
    mod = lambda j, b: (2 * b + (j >= ncb).astype(jnp.int32), 0, 0)
    tab = lambda j, b: (j, 0)
    in_specs = [pl.BlockSpec((tm, d), row)]
    if fused:
        in_specs += [pl.BlockSpec((tm, d), row), pl.BlockSpec((None, 1, d), mod)]
    in_specs += [pl.BlockSpec((None, 1, d), mod), pl.BlockSpec((None, 1, d), mod)]
    in_specs += [pl.BlockSpec((tm, t.shape[1]), tab) for t in tabs]
    in_specs += [_const_spec(w.shape) for w in wts]
    widths = (RQ_W, RQ_W, RV_W, RV_W, CONV_WIDTH, CONV_WIDTH, MLA_W, MLA_W, MLA_W, N_BRANCH * d)
    out_shape = [jax.ShapeDtypeStruct((rows, w), BF16) for w in widths]
    out_specs = [pl.BlockSpec((tm, w), row) for w in widths]
    if fused:
        out_shape.append(jax.ShapeDtypeStruct((rows, d), F32))
        out_specs.append(pl.BlockSpec((tm, d), row))
    return pl.pallas_call(
        functools.partial(_inproj_body, d_model=d, fused=fused),
        out_shape=out_shape,
        grid=(nj, n_batch),
        in_specs=in_specs,
        out_specs=out_specs,
        compiler_params=_params(2),
        name="in_proj",
    )(*x_parts, a1, b1, *tabs, *wts)


def _ret_direction(q_ref, k_ref, v_ref, y_ref, s_ref, dm_ref, xi_ref, zt_ref, cd_ref, chunk_order):
    states = [s_ref[hd] for hd in range(RET_HEADS)]
    for c in chunk_order:
        rows = slice(c * RET_CHUNK, (c + 1) * RET_CHUNK)
        q = q_ref[rows, :]
        k = k_ref[rows, :]
        v = v_ref[rows, :]
        qx = (q.astype(F32) * xi_ref[...]).astype(BF16)
        kz = (k.astype(F32) * zt_ref[...]).astype(BF16)
        for hd in range(RET_HEADS):
            ks = slice(hd * RET_DK, (hd + 1) * RET_DK)
            vs = slice(hd * RET_DV, (hd + 1) * RET_DV)
            vh = v[:, vs]
            sc = lax.dot_general(q[:, ks], k[:, ks], (((1,), (1,)), ((), ())),
                                 preferred_element_type=F32)
            inner = _dot((sc * dm_ref[hd]).astype(BF16), vh)
            cross = _dot(qx[:, ks], states[hd].astype(BF16))
            y_ref[rows, vs] = inner + cross
            upd = lax.dot_general(kz[:, ks], vh, (((0,), (0,)), ((), ())),
                                  preferred_element_type=F32)
            states[hd] = cd_ref[hd] * states[hd] + upd
    for hd in range(RET_HEADS):
        s_ref[hd] = states[hd]


def _ret_body(qf_ref, kf_ref, vf_ref, qb_ref, kb_ref, vb_ref,
              dmf_ref, dmb_ref, xif_ref, ztf_ref, xib_ref, ztb_ref, cdf_ref, cdb_ref,
              yf_ref, yb_ref, sf_ref, sb_ref):
    @pl.when(pl.program_id(1) == 0)
    def _():
        sf_ref[...] = jnp.zeros_like(sf_ref)
        sb_ref[...] = jnp.zeros_like(sb_ref)

    n_chunks = ROW_BLOCK // RET_CHUNK
    _ret_direction(qf_ref, kf_ref, vf_ref, yf_ref, sf_ref, dmf_ref, xif_ref, ztf_ref, cdf_ref,
                   range(n_chunks))
    _ret_direction(qb_ref, kb_ref, vb_ref, yb_ref, sb_ref, dmb_ref, xib_ref, ztb_ref, cdb_ref,
                   range(n_chunks - 1, -1, -1))


def _ret_call(rq, rk, rv, consts, *, n_batch, nj, ncb):
    rows = rq.shape[0]
    tm = ROW_BLOCK
    fwd = lambda b, s: (b * nj + s, 0)

    def bwd(b, s):
        return (b * nj + jnp.where(s < ncb, ncb - 1 - s, nj - 1 - (s - ncb)), 0)

    specs = []
    for im in (fwd, bwd):
        specs += [pl.BlockSpec((tm, RQ_W), im), pl.BlockSpec((tm, RQ_W), im),
                  pl.BlockSpec((tm, RV_W), im)]
    specs += [_const_spec(c.shape) for c in consts]
    return pl.pallas_call(
        _ret_body,
        out_shape=[jax.ShapeDtypeStruct((rows, RV_W), F32)] * 2,
        grid=(n_batch, nj),
        in_specs=specs,
        out_specs=[pl.BlockSpec((tm, RV_W), fwd), pl.BlockSpec((tm, RV_W), bwd)],
        scratch_shapes=[pltpu.VMEM((RET_HEADS, RET_DK, RET_DV), F32)] * 2,
        compiler_params=_params(2),
        name="retention",
    )(rq, rk, rv, rq, rk, rv, *consts)


def _pick_tile(n, candidates):
    for c in candidates:
        if n % c == 0:
            return c
    raise ValueError(f"no tile for {n}")


def _attn_body(q_ref, k_ref, v_ref, o_ref, s_ref, *, n_ctx, n_all, ncb, ctx_queries):
    tq = q_ref.shape[0]
    heads = [slice(hh * HEAD_PAD, (hh + 1) * HEAD_PAD) for hh in range(2)]

    def attend(n_keys, tk):
        nt = n_keys // tk
        qs = [q_ref[:, hs] for hs in heads]

        def qk(t, mrun):
            r0 = pl.multiple_of(t * tk, tk)
            out = []
            for hh, hs in enumerate(heads):
                s = lax.dot_general(qs[hh], k_ref[pl.ds(r0, tk), hs], (((1,), (1,)), ((), ())),
                                    preferred_element_type=F32)
                s_ref[hh, t, :, 0:tk] = s
                m = mrun[hh]
                for cc in range(tk // LANES):
                    m = jnp.maximum(m, s[:, cc * LANES:(cc + 1) * LANES])
                out.append(m)
            return tuple(out)

        mrun = lax.fori_loop(0, nt, qk, tuple(jnp.full((tq, LANES), -jnp.inf, F32) for _ in heads),
                             unroll=True)
        mrow = [jnp.max(m, axis=-1, keepdims=True) for m in mrun]

        def pv(t, accs):
            r0 = pl.multiple_of(t * tk, tk)
            out = []
            for hh, hs in enumerate(heads):
                p = jnp.exp2(s_ref[hh, t, :, 0:tk] - mrow[hh]).astype(BF16)
                out.append(accs[hh] + _dot(p, v_ref[pl.ds(r0, tk), hs]))
            return tuple(out)

        accs = lax.fori_loop(0, nt, pv, tuple(jnp.zeros((tq, HEAD_PAD), F32) for _ in heads),
                             unroll=True)
        o_ref[...] = jnp.concatenate([a[:, :MLA_V] / a[:, MLA_V:MLA_V + 1] for a in accs],
                                     axis=-1).astype(BF16)

    j = pl.program_id(2)

    @pl.when(j < ncb)
    def _():
        if ctx_queries:
            attend(n_ctx, _pick_tile(n_ctx, (256, 128)))
        else:
            o_ref[...] = jnp.zeros_like(o_ref)

    @pl.when(j >= ncb)
    def _():
        attend(n_all, _pick_tile(n_all, ATTN_KEY_TILES))


def _attn_call(qm, km, vm, *, n_batch, nj, ncb, n_ctx, ctx_queries):
    rows = qm.shape[0]
    tm = ROW_BLOCK
    t_all = nj * tm
    tk = _pick_tile(t_all, ATTN_KEY_TILES)
    qmap = lambda b, hp, j: (b * nj + j, hp)
    kmap = lambda b, hp, j: (b, hp)
    return pl.pallas_call(
        functools.partial(_attn_body, n_ctx=n_ctx, n_all=t_all, ncb=ncb, ctx_queries=ctx_queries),
        out_shape=jax.ShapeDtypeStruct((rows, MLA_O), BF16),
        grid=(n_batch, MLA_HEADS // 2, nj),
        in_specs=[pl.BlockSpec((tm, 2 * HEAD_PAD), qmap),
                  pl.BlockSpec((t_all, 2 * HEAD_PAD), kmap),
                  pl.BlockSpec((t_all, 2 * HEAD_PAD), kmap)],
        out_specs=pl.BlockSpec((tm, 2 * MLA_V), qmap),
        scratch_shapes=[pltpu.VMEM((2, t_all // tk, tm, tk), F32)],
        compiler_params=_params(3),
        name="mla_attention",
    )(qm, km, vm)


def _merge_body(yf_ref, yb_ref, rg_ref, cb_ref, cu_ref, cup_ref, cun_ref, om_ref, gt_ref, x_ref,
                g1_ref, a2_ref, b2_ref, gn_ref, cw_ref, wro_ref, wco_ref, wmo_ref, wout_ref,
                wrh_ref, wrl_ref, rb_ref, xmid_ref, h2_ref, rt_ref, *, nj, ncb, d_model):
    tm = x_ref.shape[0]
    j = pl.program_id(0) % nj
    seg_first = jnp.logical_or(j == 0, j == ncb)
    seg_last = jnp.logical_or(j == ncb - 1, j == nj - 1)

    y = yf_ref[...] + yb_ref[...]
    yn = jnp.concatenate([_rms(y[:, hd * RET_DV:(hd + 1) * RET_DV]) for hd in range(RET_HEADS)],
                         axis=-1) * gn_ref[...]
    y_ret = _dot((rg_ref[...].astype(F32) * yn).astype(BF16), wro_ref[...])

    u = cu_ref[...].astype(F32)
    ridx = lax.broadcasted_iota(jnp.int32, u.shape, 0)
    prev_row = cup_ref[...].astype(F32)[BF16_SUBLANES - 1:, :] * jnp.where(seg_first, 0.0, 1.0)
    next_row = cun_ref[...].astype(F32)[0:1, :] * jnp.where(seg_last, 0.0, 1.0)
    u_prev = jnp.where(ridx == 0, prev_row, pltpu.roll(u, 1, 0))
    u_next = jnp.where(ridx == tm - 1, next_row, pltpu.roll(u, tm - 1, 0))
    conv = u_prev * cw_ref[0:1, :] + u * cw_ref[1:2, :] + u_next * cw_ref[2:3, :]
    y_conv = _dot((cb_ref[...].astype(F32) * conv).astype(BF16), wco_ref[...])

    y_mla = _dot(om_ref[...], wmo_ref[...])

    merged = (gt_ref[:, 0:d_model].astype(F32) * y_ret
              + gt_ref[:, d_model:2 * d_model].astype(F32) * y_conv
              + gt_ref[:, 2 * d_model:3 * d_model].astype(F32) * y_mla)
    x_mid = x_ref[...] + g1_ref[...] * _dot(merged.astype(BF16), wout_ref[...])
    xmid_ref[...] = x_mid

    h2 = _rms(x_mid) * a2_ref[...] + b2_ref[...]
    h_hi, h_lo = _split_bf16(h2)
    logits = _dot(h_hi, wrh_ref[...]) + _dot(h_hi, wrl_ref[...]) + _dot(h_lo, wrh_ref[...])
    cls, w_lo, w_hi = _route_rows(jnp.transpose(logits)[:N_EXPERTS, :], rb_ref[...])
    rt_ref[...] = jnp.concatenate(
        [cls.astype(F32), w_lo, w_hi, jnp.zeros((ROUTE_ROWS - 3, tm), F32)], axis=0)
    h2_ref[...] = h_hi


def _top2_of4(v):
    def first_max(rows):
        best, idx = rows[0], jnp.zeros(rows[0].shape, jnp.int32)
        for e in range(1, len(rows)):
            better = rows[e] > best
            idx = jnp.where(better, e, idx)
            best = jnp.where(better, rows[e], best)
        return best, idx

    b1, i1 = first_max(v)
    b2, i2 = first_max([jnp.where(i1 == e, -jnp.inf, v[e]) for e in range(len(v))])
    return i1, i2, b1, b2


def _route_rows(logits_t, bias):
    scores = jax.nn.sigmoid(logits_t)
    biased = scores + bias
    row = lambda a, e: a[e:e + 1, :]
    best = None
    for g in range(N_GROUPS):
        v = [row(biased, g * EXPERTS_PER_GROUP + e) for e in range(EXPERTS_PER_GROUP)]
        i1, i2, b1, b2 = _top2_of4(v)
        cand = (b1 + b2, jnp.full(i1.shape, g, jnp.int32), i1, i2)
        if best is None:
            best = cand
        else:
            better = cand[0] > best[0]
            best = tuple(jnp.where(better, c, o) for c, o in zip(cand, best))
    _, g_sel, i1, i2 = best
    lo = jnp.minimum(i1, i2)
    hi = jnp.maximum(i1, i2)
    e_lo = g_sel * EXPERTS_PER_GROUP + lo
    e_hi = g_sel * EXPERTS_PER_GROUP + hi
    s_lo = jnp.zeros_like(best[0])
    s_hi = jnp.zeros_like(best[0])
    for e in range(N_EXPERTS):
        s_lo = jnp.where(e_lo == e, row(scores, e), s_lo)
        s_hi = jnp.where(e_hi == e, row(scores, e), s_hi)
    total = s_lo + s_hi
    pair_base = jnp.where(lo == 0, 0, jnp.where(lo == 1, 3, 5))
    cls = g_sel * len(PAIRS) + pair_base + (hi - lo - 1)
    return cls, s_lo / total, s_hi / total


def _merge_call(yf, yb, rg, cb, cu, om, gt, xa, g1, a2, b2, wts, *, nj, ncb):
    rows, d = xa.shape
    tm = ROW_BLOCK
    nb = rows // tm
    halo = BF16_SUBLANES
    per_blk = tm // halo
    row = lambda i: (i, 0)
    mod = lambda i: (2 * (i // nj) + ((i % nj) >= ncb).astype(jnp.int32), 0, 0)
    prev = lambda i: (jnp.maximum(i * per_blk - 1, 0), 0)
    nxt = lambda i: (jnp.minimum((i + 1) * per_blk, rows // halo - 1), 0)
    in_specs = [pl.BlockSpec((tm, RV_W), row), pl.BlockSpec((tm, RV_W), row),
                pl.BlockSpec((tm, RV_W), row),
                pl.BlockSpec((tm, CONV_WIDTH), row), pl.BlockSpec((tm, CONV_WIDTH), row),
                pl.BlockSpec((halo, CONV_WIDTH), prev), pl.BlockSpec((halo, CONV_WIDTH), nxt),
                pl.BlockSpec((tm, MLA_O), row), pl.BlockSpec((tm, N_BRANCH * d), row),
                pl.BlockSpec((tm, d), row),
                pl.BlockSpec((None, 1, d), mod), pl.BlockSpec((None, 1, d), mod),
                pl.BlockSpec((None, 1, d), mod)]
    in_specs += [_const_spec(w.shape) for w in wts]
    return pl.pallas_call(
        functools.partial(_merge_body, nj=nj, ncb=ncb, d_model=d),
        out_shape=[jax.ShapeDtypeStruct((rows, d), F32), jax.ShapeDtypeStruct((rows, d), BF16),
                   jax.ShapeDtypeStruct((nb * ROUTE_ROWS, tm), F32)],
        grid=(nb,),
        in_specs=in_specs,
        out_specs=[pl.BlockSpec((tm, d), row), pl.BlockSpec((tm, d), row),
                   pl.BlockSpec((ROUTE_ROWS, tm), row)],
        compiler_params=_params(1),
        name="merge_out_proj",
    )(yf, yb, rg, cb, cu, cu, cu, om, gt, xa, g1, a2, b2, *wts)


def _moe_body(ea_ref, eb_ref, nu_ref, h_ref, gw_ref, w13a_ref, w2a_ref, w13b_ref, w2b_ref, o_ref):
    t = pl.program_id(0)

    @pl.when(t < nu_ref[0])
    def _():
        h = h_ref[...]
        gw = gw_ref[...]

        def expert(w13_ref, w2_ref, wt):
            a = _dot(h, w13_ref[...])
            act = _silu(a[:, :D_EXPERT]) * a[:, D_EXPERT:] * wt
            return _dot(act.astype(BF16), w2_ref[...])

        o_ref[...] = (expert(w13a_ref, w2a_ref, gw[:, 0:1])
                      + expert(w13b_ref, w2b_ref, gw[:, 1:2])).astype(BF16)

    @pl.when(t >= nu_ref[0])
    def _():
        o_ref[...] = jnp.zeros_like(o_ref)


def _moe_call(tile_ea, tile_eb, n_used, hs, gw, w13, w2):
    npad, d = hs.shape
    tmo = MOE_TILE
    row = lambda t, ea, eb, nu: (t, 0)
    wa = lambda t, ea, eb, nu: (ea[t], 0, 0)
    wb = lambda t, ea, eb, nu: (eb[t], 0, 0)
    grid_spec = pltpu.PrefetchScalarGridSpec(
        num_scalar_prefetch=3,
        grid=(npad // tmo,),
        in_specs=[pl.BlockSpec((tmo, d), row), pl.BlockSpec((tmo, 2), row),
                  pl.BlockSpec((None, d, 2 * D_EXPERT), wa), pl.BlockSpec((None, D_EXPERT, d), wa),
                  pl.BlockSpec((None, d, 2 * D_EXPERT), wb), pl.BlockSpec((None, D_EXPERT, d), wb)],
        out_specs=pl.BlockSpec((tmo, d), row))
    return pl.pallas_call(
        _moe_body,
        out_shape=jax.ShapeDtypeStruct((npad, d), BF16),
        grid_spec=grid_spec,
        compiler_params=_params(1),
        name="moe_experts",
    )(tile_ea, tile_eb, n_used, hs, gw, w13, w2, w13, w2)


def _dispatch(cls, n_tok):
    tmo = MOE_TILE
    n_tiles = n_tok // tmo + N_CLASSES
    onehot = (cls[:, None] == jnp.arange(N_CLASSES, dtype=jnp.int32)[None, :]).astype(jnp.int32)
    ranks = jnp.cumsum(onehot, axis=0) - onehot
    rank = jnp.sum(ranks * onehot, axis=1)
    counts = jnp.sum(onehot, axis=0)
    tiles_per = (counts + tmo - 1) // tmo
    tile_end = jnp.cumsum(tiles_per)
    offs = (tile_end - tiles_per) * tmo
    dest = offs[cls] + rank
    npad = n_tiles * tmo
    src = jnp.zeros((npad,), jnp.int32).at[dest].set(jnp.arange(n_tok, dtype=jnp.int32))
    tile_ids = jnp.arange(n_tiles, dtype=jnp.int32)
    tile_cls = jnp.sum((tile_end[None, :] <= tile_ids[:, None]).astype(jnp.int32), axis=1)
    tile_cls = jnp.minimum(tile_cls, N_CLASSES - 1)
    pa = jnp.asarray([p[0] for p in PAIRS], jnp.int32)
    pb = jnp.asarray([p[1] for p in PAIRS], jnp.int32)
    base = (tile_cls // len(PAIRS)) * EXPERTS_PER_GROUP
    tile_ea = base + pa[tile_cls % len(PAIRS)]
    tile_eb = base + pb[tile_cls % len(PAIRS)]
    n_used = tile_end[-1:].astype(jnp.int32)
    return src, dest, tile_ea, tile_eb, n_used


def _final_body(x_ref, f_ref, g2_ref, fn_ref, o_ref):
    o_ref[...] = _rms(x_ref[...] + g2_ref[...] * f_ref[...].astype(F32)) * fn_ref[...]


def _final_call(xmid, fg, g2, final_norm, *, n_batch, nj, ncb):
    rows, d = xmid.shape
    tm = ROW_BLOCK
    njl = nj - ncb
    src = lambda b, j: (b * nj + ncb + j, 0)
    return pl.pallas_call(
        _final_body,
        out_shape=jax.ShapeDtypeStruct((n_batch * njl * tm, d), F32),
        grid=(n_batch, njl),
        in_specs=[pl.BlockSpec((tm, d), src), pl.BlockSpec((tm, d), src),
                  pl.BlockSpec((None, 1, d), lambda b, j: (2 * b + 1, 0, 0)),
                  _const_spec((1, d))],
        out_specs=pl.BlockSpec((tm, d), lambda b, j: (b * njl + j, 0)),
        compiler_params=_params(2),
        name="final_norm",
    )(xmid, fg, g2, final_norm.reshape(1, d))


def _split_cols(w, sizes):
    out, off = [], 0
    for s in sizes:
        out.append(w[:, off:off + s])
        off += s
    out.append(w[:, off:])
    return out


def _layer_weights(w_in, w_uq, w_ukv, q_norm, kv_norm):
    wq, wk, wv, wg, wcb, wcc, wcx, wqd, wkvd, wkr, wgate = _split_cols(w_in, IN_SIZES[:-1])
    rope_lanes = lambda w: jnp.pad(w, ((0, 0), (MLA_NOPE, HEAD_PAD - MLA_NOPE - MLA_ROPE)))
    w_ext = jnp.concatenate(
        [wq, wk, wv, wg, wcb, wcc, wcx, wqd, wkvd, rope_lanes(wkr), wgate], axis=1).astype(BF16)

    uq = w_uq.reshape(MLA_Q_RANK, MLA_HEADS, MLA_NOPE + MLA_ROPE)
    tail = HEAD_PAD - MLA_NOPE - MLA_ROPE
    uq_pad = jnp.pad(uq, ((0, 0), (0, 0), (0, tail)))
    wuq = uq_pad.reshape(MLA_Q_RANK, MLA_W).astype(BF16)

    ukv = w_ukv.reshape(MLA_KV_RANK, MLA_HEADS, MLA_NOPE + MLA_V)
    wk_up = jnp.pad(ukv[..., :MLA_NOPE], ((0, 0), (0, 0), (0, HEAD_PAD - MLA_NOPE)))
    wk_up = wk_up.reshape(MLA_KV_RANK, MLA_W).astype(BF16)
    wv_pad = jnp.pad(ukv[..., MLA_NOPE:], ((0, 0), (0, 0), (0, HEAD_PAD - MLA_V)))
    wv_pad = wv_pad.reshape(MLA_KV_RANK, MLA_W).astype(BF16)
    ones_row = jnp.zeros((MLA_HEADS, HEAD_PAD), F32).at[:, MLA_V].set(1.0).reshape(1, MLA_W)
    q_gain = q_norm.astype(F32) * ((MLA_NOPE + MLA_ROPE) ** -0.5 * math.log2(math.e))
    return (w_ext, wuq, wk_up, wv_pad, q_gain.reshape(1, -1),
            kv_norm.reshape(1, -1).astype(F32), ones_row)


def _rotary_tables(n_ctx, n_lat):
    pos = jnp.arange(n_lat, dtype=jnp.int32)
    rows = pos // GRID_W
    cols = pos - rows * GRID_W

    def angles(p, half):
        inv = ROPE_BASE ** (-jnp.arange(half, dtype=F32) / half)
        return p.astype(F32)[:, None] * inv[None, :]

    def with_ctx(t, fill):
        return jnp.concatenate([jnp.full((n_ctx, t.shape[1]), fill, F32), t], axis=0)

    ang = angles(pos, RET_DK // 2)
    cr = with_ctx(jnp.tile(jnp.cos(ang), (1, 4)), 1.0)
    sr = with_ctx(jnp.tile(jnp.concatenate([-jnp.sin(ang), jnp.sin(ang)], axis=1), (1, 2)), 0.0)

    quarter = MLA_ROPE // 4
    ar, ac = angles(rows, quarter), angles(cols, quarter)
    cos32 = with_ctx(jnp.concatenate([jnp.cos(ar)] * 2 + [jnp.cos(ac)] * 2, axis=1), 1.0)
    sin32 = with_ctx(jnp.concatenate([-jnp.sin(ar), jnp.sin(ar), -jnp.sin(ac), jnp.sin(ac)], axis=1), 0.0)
    n_all = n_ctx + n_lat
    tail = jnp.zeros((n_all, HEAD_PAD - MLA_NOPE - MLA_ROPE), F32)
    cm = jnp.concatenate([jnp.ones((n_all, MLA_NOPE), F32), cos32, tail], axis=1)
    sm = jnp.concatenate([jnp.zeros((n_all, MLA_NOPE), F32), sin32, tail], axis=1)
    return cr, sr, cm, sm


def _retention_consts(ret_decay):
    log_gf = jax.nn.log_sigmoid(ret_decay[0].astype(F32))
    log_gb = jax.nn.log_sigmoid(ret_decay[1].astype(F32))
    idx = jnp.arange(RET_CHUNK, dtype=F32)
    rel = idx[:, None] - idx[None, :]
    dm_f = jnp.where(rel >= 0, jnp.exp(log_gf[:, None, None] * jnp.maximum(rel, 0.0)[None]), 0.0)
    dm_b = jnp.where(rel < 0, jnp.exp(log_gb[:, None, None] * jnp.maximum(-rel, 0.0)[None]), 0.0)

    def lanes(t):
        return jnp.repeat(t, RET_DK, axis=1)

    xi_f = lanes(jnp.exp(log_gf[None, :] * (idx + 1.0)[:, None]))
    zt_f = lanes(jnp.exp(log_gf[None, :] * (RET_CHUNK - 1 - idx)[:, None]))
    xi_b = lanes(jnp.exp(log_gb[None, :] * (RET_CHUNK - idx)[:, None]))
    zt_b = lanes(jnp.exp(log_gb[None, :] * idx[:, None]))
    cd = lambda lg: jnp.broadcast_to(jnp.exp(lg * RET_CHUNK)[:, None, None], (RET_HEADS, 1, RET_DV))
    return dm_f, dm_b, xi_f, zt_f, xi_b, zt_b, cd(log_gf), cd(log_gb)


def kernel(x, c, ctx, c_ctx, w_ada, b_ada, norm1, norm2, w_in, ret_decay, ret_gn, w_ret_o, conv_w,
           w_conv_o, mla_q_norm, w_uq, mla_kv_norm, w_ukv, w_mla_o, w_out, w_router, router_bias,
           w1, w3, w2, final_norm):
    n_batch, n_lat, d = x.shape
    n_ctx = ctx.shape[1]
    depth = w_ada.shape[0]
    t_all = n_ctx + n_lat
    assert n_ctx % ROW_BLOCK == 0 and n_lat % ROW_BLOCK == 0 and n_lat % GRID_W == 0
    nj = t_all // ROW_BLOCK
    ncb = n_ctx // ROW_BLOCK
    n_tok = n_batch * t_all
    assert n_tok % MOE_TILE == 0
    geom = dict(n_batch=n_batch, nj=nj, ncb=ncb)

    cc = jnp.concatenate([c, c_ctx[None, :]], axis=0)
    cc = jnp.pad(cc, ((0, -cc.shape[0] % 8), (0, 0)))
    mod = _ada_call(cc, w_ada, b_ada)[:, :n_batch + 1].reshape(depth, n_batch + 1, N_MOD, d)
    pick = jnp.stack([jnp.full((n_batch,), n_batch, jnp.int32),
                      jnp.arange(n_batch, dtype=jnp.int32)], axis=1).reshape(-1)
    mod = mod[:, pick]

    tabs = _rotary_tables(n_ctx, n_lat)
    wr_hi, wr_lo = _split_bf16(jnp.pad(w_router.astype(F32), ((0, 0), (0, LANES - N_EXPERTS))))
    rbias = router_bias.astype(F32).reshape(N_EXPERTS, 1)
    xa = jnp.concatenate([ctx, x], axis=1).reshape(n_tok, d)

    out = None
    x_parts = (xa,)
    for l in range(depth):
        m = mod[l]
        rowvec = lambda v: v.reshape(2 * n_batch, 1, d)
        a1 = rowvec(norm1[l][None, :] * (1.0 + m[:, 1]))
        b1 = rowvec(m[:, 0])
        g1 = rowvec(m[:, 2])
        a2 = rowvec(norm2[l][None, :] * (1.0 + m[:, 4]))
        b2 = rowvec(m[:, 3])
        g2 = rowvec(m[:, 5])

        wts = _layer_weights(w_in[l], w_uq[l], w_ukv[l], mla_q_norm[l], mla_kv_norm[l])
        proj = _inproj_call(x_parts, a1, b1, tabs, wts, **geom)
        rq, rk, rv, rg, cb, cu, qm, km, vm, gt = proj[:10]
        if len(x_parts) == 3:
            xa = proj[10]

        yf, yb = _ret_call(rq, rk, rv, _retention_consts(ret_decay[l]), **geom)
        om = _attn_call(qm, km, vm, n_ctx=n_ctx, ctx_queries=l < depth - 1, **geom)

        merge_wts = (ret_gn[l].reshape(1, -1).astype(F32), conv_w[l].T.astype(F32),
                     w_ret_o[l].astype(BF16), w_conv_o[l].astype(BF16), w_mla_o[l].astype(BF16),
                     w_out[l].astype(BF16), wr_hi, wr_lo, rbias)
        xmid, h2, route = _merge_call(yf, yb, rg, cb, cu, om, gt, xa, g1, a2, b2, merge_wts,
                                      nj=nj, ncb=ncb)
        route = route.reshape(-1, ROUTE_ROWS, ROW_BLOCK)
        cls = route[:, 0, :].reshape(-1).astype(jnp.int32)
        gate_w = jnp.stack([route[:, 1, :].reshape(-1), route[:, 2, :].reshape(-1)], axis=-1)
        src, dest, tile_ea, tile_eb, n_used = _dispatch(cls, n_tok)
        hs = jnp.take(h2, src, axis=0)
        gw = jnp.take(gate_w, src, axis=0)
        w13 = jnp.concatenate([w1[l], w3[l]], axis=-1).astype(BF16)
        f_sorted = _moe_call(tile_ea, tile_eb, n_used, hs, gw, w13, w2[l].astype(BF16))
        fg = jnp.take(f_sorted, dest, axis=0)

        if l < depth - 1:
            x_parts = (xmid, fg, g2)
        else:
            out = _final_call(xmid, fg, g2, final_norm, **geom)
    return out.reshape(n_batch, n_lat, d)
```

```python
import functools
import math

import jax
import jax.numpy as jnp
from jax import lax
from jax.experimental import pallas as pl
from jax.experimental.pallas import tpu as pltpu

F32 = jnp.float32
BF16 = jnp.bfloat16

GRID_W = 64
RMS_EPS = 1e-6
ROPE_BASE = 10000.0
N_MOD = 6
RET_HEADS = 4
RET_DK = 64
RET_DV = 128
RET_CHUNK = 256
CONV_WIDTH = 512
MLA_HEADS = 8
MLA_Q_RANK = 384
MLA_KV_RANK = 256
MLA_NOPE = 64
MLA_ROPE = 32
MLA_V = 64
N_BRANCH = 3
N_EXPERTS = 16
N_GROUPS = 4
EXPERTS_PER_GROUP = N_EXPERTS // N_GROUPS
D_EXPERT = 512
IN_SIZES = (RET_HEADS * RET_DK, RET_HEADS * RET_DK, RET_HEADS * RET_DV, RET_HEADS * RET_DV,
            CONV_WIDTH, CONV_WIDTH, CONV_WIDTH, MLA_Q_RANK, MLA_KV_RANK, MLA_ROPE, 0)

LANES = 128
BF16_SUBLANES = 16
VMEM_LIMIT = 56 * 1024 * 1024

HEAD_PAD = LANES
ROW_BLOCK = 256
MOE_TILE = 256
ATTN_KEY_TILES = (768, 512, 256, 128)
ROUTE_ROWS = 8
PAIRS = [(a, b) for a in range(EXPERTS_PER_GROUP) for b in range(a + 1, EXPERTS_PER_GROUP)]
N_CLASSES = N_GROUPS * len(PAIRS)

RQ_W = RET_HEADS * RET_DK
RV_W = RET_HEADS * RET_DV
MLA_W = MLA_HEADS * HEAD_PAD
MLA_O = MLA_HEADS * MLA_V


def _const_spec(shape):
    nd = len(shape)
    return pl.BlockSpec(shape, lambda *_: (0,) * nd, pipeline_mode=pl.Buffered(1))


def _params(n_axes):
    return pltpu.CompilerParams(dimension_semantics=("arbitrary",) * n_axes,
                                vmem_limit_bytes=VMEM_LIMIT)


def _dot(a, b):
    return jnp.dot(a, b, preferred_element_type=F32)


def _split_bf16(a):
    hi = a.astype(BF16)
    lo = (a - hi.astype(F32)).astype(BF16)
    return hi, lo


def _silu(v):
    return v * jax.nn.sigmoid(v)


def _rms(v):
    return v * lax.rsqrt(jnp.mean(v * v, axis=-1, keepdims=True) + RMS_EPS)


def _ada_body(c_ref, w_ref, b_ref, o_ref):
    a_hi, a_lo = _split_bf16(_silu(c_ref[...]))
    w_hi, w_lo = _split_bf16(w_ref[...])
    o_ref[...] = _dot(a_hi, w_hi) + _dot(a_hi, w_lo) + _dot(a_lo, w_hi) + b_ref[...]


def _ada_call(cc, w_ada, b_ada):
    depth, d, nm = w_ada.shape
    rows = cc.shape[0]
    cb = nm // 4
    return pl.pallas_call(
        _ada_body,
        out_shape=jax.ShapeDtypeStruct((depth, rows, nm), F32),
        grid=(depth, nm // cb),
        in_specs=[pl.BlockSpec((rows, d), lambda l, n: (0, 0)),
                  pl.BlockSpec((None, d, cb), lambda l, n: (l, 0, n)),
                  pl.BlockSpec((None, 1, cb), lambda l, n: (l, 0, n))],
        out_specs=pl.BlockSpec((None, rows, cb), lambda l, n: (l, 0, n)),
        compiler_params=_params(2),
        name="ada_mod",
    )(cc, w_ada, b_ada.reshape(depth, 1, nm))


_O_RQ = 0
_O_RK = _O_RQ + RQ_W
_O_RV = _O_RK + RQ_W
_O_RG = _O_RV + RV_W
_O_CB = _O_RG + RV_W
_O_CC = _O_CB + CONV_WIDTH
_O_CX = _O_CC + CONV_WIDTH
_O_QD = _O_CX + CONV_WIDTH
_O_KVD = _O_QD + MLA_Q_RANK
_O_KR = _O_KVD + MLA_KV_RANK
_O_GT = _O_KR + LANES


def _rot_half(v, half):
    width = v.shape[1]
    lane = lax.broadcasted_iota(jnp.int32, v.shape, 1)
    first = (lane % (2 * half)) < half
    return jnp.where(first, pltpu.roll(v, width - half, 1), pltpu.roll(v, half, 1))


def _inproj_body(*refs, d_model, fused):
    if fused:
        xm_ref, f_ref, g2_ref = refs[:3]
        refs = refs[3:]
        xo_ref = refs[-1]
        x = xm_ref[...] + g2_ref[...] * f_ref[...].astype(F32)
        xo_ref[...] = x
    else:
        x = refs[0][...]
        refs = refs[1:]
    (a1_ref, b1_ref, cr_ref, sr_ref, cm_ref, sm_ref, w_ref, wuq_ref, wk_ref, wv_ref, qn_ref, kvn_ref,
     ones_ref, rq_ref, rk_ref, rv_ref, rg_ref, cb_ref, cu_ref, qm_ref, km_ref, vm_ref, gt_ref) = refs[:23]
    h = (_rms(x) * a1_ref[...] + b1_ref[...]).astype(BF16)

    def mm(off, width):
        return _dot(h, w_ref[:, off:off + width])

    cr = jnp.concatenate([cr_ref[...]] * (RQ_W // cr_ref.shape[1]), axis=1)
    sr = jnp.concatenate([sr_ref[...]] * (RQ_W // sr_ref.shape[1]), axis=1)
    q = mm(_O_RQ, RQ_W)
    rq_ref[...] = (q * cr + _rot_half(q, RET_DK // 2) * sr).astype(BF16)
    k = mm(_O_RK, RQ_W)
    rk_ref[...] = ((k * cr + _rot_half(k, RET_DK // 2) * sr) * (RET_DK ** -0.5)).astype(BF16)
    rv_ref[...] = mm(_O_RV, RV_W).astype(BF16)
    rg_ref[...] = _silu(mm(_O_RG, RV_W)).astype(BF16)
    cb_ref[...] = mm(_O_CB, CONV_WIDTH).astype(BF16)
    cu_ref[...] = (mm(_O_CC, CONV_WIDTH) * mm(_O_CX, CONV_WIDTH)).astype(BF16)

    cm = cm_ref[...]
    sm = sm_ref[...]
    qn = (_rms(mm(_O_QD, MLA_Q_RANK)) * qn_ref[...]).astype(BF16)
    for hd in range(MLA_HEADS):
        lo = hd * HEAD_PAD
        qa = _dot(qn, wuq_ref[:, lo:lo + HEAD_PAD])
        qm_ref[:, lo:lo + HEAD_PAD] = (qa * cm + _rot_half(qa, MLA_ROPE // 4) * sm).astype(BF16)

    kvn = (_rms(mm(_O_KVD, MLA_KV_RANK)) * kvn_ref[...]).astype(BF16)
    kr = mm(_O_KR, HEAD_PAD)
    kr = kr * cm + _rot_half(kr, MLA_ROPE // 4) * sm
    kn = _dot(kvn, wk_ref[...])
    for hd in range(MLA_HEADS):
        lo = hd * HEAD_PAD
        km_ref[:, lo:lo + HEAD_PAD] = (kn[:, lo:lo + HEAD_PAD] + kr).astype(BF16)
    vm_ref[...] = (_dot(kvn, wv_ref[...]) + ones_ref[...]).astype(BF16)

    for br in range(N_BRANCH):
        gt_ref[:, br * d_model:(br + 1) * d_model] = jax.nn.sigmoid(
            mm(_O_GT + br * d_model, d_model)).astype(BF16)


def _inproj_call(x_parts, a1, b1, tabs, wts, *, n_batch, nj, ncb):
    fused = len(x_parts) == 3
    rows, d = x_parts[0].shape
    tm = ROW_BLOCK
    row = lambda j, b: (b * nj + j, 0)
    mod = lambda j, b: (2 * b + (j >= ncb).astype(jnp.int32), 0, 0)
    tab = lambda j, b: (j, 0)
    in_specs = [pl.BlockSpec((tm, d), row)]
    if fused:
        in_specs += [pl.BlockSpec((tm, d), row), pl.BlockSpec((None, 1, d), mod)]
    in_specs += [pl.BlockSpec((None, 1, d), mod), pl.BlockSpec((None, 1, d), mod)]
    in_specs += [pl.BlockSpec((tm, t.shape[1]), tab) for t in tabs]
    in_specs += [_const_spec(w.shape) for w in wts]
    widths = (RQ_W, RQ_W, RV_W, RV_W, CONV_WIDTH, CONV_WIDTH, MLA_W, MLA_W, MLA_W, N_BRANCH * d)
    out_shape = [jax.ShapeDtypeStruct((rows, w), BF16) for w in widths]
    out_specs = [pl.BlockSpec((tm, w), row) for w in widths]
    if fused:
        out_shape.append(jax.ShapeDtypeStruct((rows, d), F32))
        out_specs.append(pl.BlockSpec((tm, d), row))
    return pl.pallas_call(
        functools.partial(_inproj_body, d_model=d, fused=fused),
        out_shape=out_shape,
        grid=(nj, n_batch),
        in_specs=in_specs,
        out_specs=out_specs,
        compiler_params=_params(2),
        name="in_proj",
    )(*x_parts, a1, b1, *tabs, *wts)


def _ret_direction(q_ref, k_ref, v_ref, y_ref, s_ref, dm_ref, xi_ref, zt_ref, cd_ref, chunk_order):
    states = [s_ref[hd] for hd in range(RET_HEADS)]
    for c in chunk_order:
        rows = slice(c * RET_CHUNK, (c + 1) * RET_CHUNK)
        q = q_ref[rows, :]
        k = k_ref[rows, :]
        v = v_ref[rows, :]
        qx = (q.astype(F32) * xi_ref[...]).astype(BF16)
        kz = (k.astype(F32) * zt_ref[...]).astype(BF16)
        for hd in range(RET_HEADS):
            ks = slice(hd * RET_DK, (hd + 1) * RET_DK)
            vs = slice(hd * RET_DV, (hd + 1) * RET_DV)
            vh = v[:, vs]
            sc = lax.dot_general(q[:, ks], k[:, ks], (((1,), (1,)), ((), ())),
                                 preferred_element_type=F32)
            inner = _dot((sc * dm_ref[hd]).astype(BF16), vh)
            cross = _dot(qx[:, ks], states[hd].astype(BF16))
            y_ref[rows, vs] = inner + cross
            upd = lax.dot_general(kz[:, ks], vh, (((0,), (0,)), ((), ())),
                                  preferred_element_type=F32)
            states[hd] = cd_ref[hd] * states[hd] + upd
    for hd in range(RET_HEADS):
        s_ref[hd] = states[hd]


def _ret_body(qf_ref, kf_ref, vf_ref, qb_ref, kb_ref, vb_ref,
              dmf_ref, dmb_ref, xif_ref, ztf_ref, xib_ref, ztb_ref, cdf_ref, cdb_ref,
              yf_ref, yb_ref, sf_ref, sb_ref):
    @pl.when(pl.program_id(1) == 0)
    def _():
        sf_ref[...] = jnp.zeros_like(sf_ref)
        sb_ref[...] = jnp.zeros_like(sb_ref)

    n_chunks = ROW_BLOCK // RET_CHUNK
    _ret_direction(qf_ref, kf_ref, vf_ref, yf_ref, sf_ref, dmf_ref, xif_ref, ztf_ref, cdf_ref,
                   range(n_chunks))
    _ret_direction(qb_ref, kb_ref, vb_ref, yb_ref, sb_ref, dmb_ref, xib_ref, ztb_ref, cdb_ref,
                   range(n_chunks - 1, -1, -1))


def _ret_call(rq, rk, rv, consts, *, n_batch, nj, ncb):
    rows = rq.shape[0]
    tm = ROW_BLOCK
    fwd = lambda b, s: (b * nj + s, 0)

    def bwd(b, s):
        return (b * nj + jnp.where(s < ncb, ncb - 1 - s, nj - 1 - (s - ncb)), 0)

    specs = []
    for im in (fwd, bwd):
        specs += [pl.BlockSpec((tm, RQ_W), im), pl.BlockSpec((tm, RQ_W), im),
                  pl.BlockSpec((tm, RV_W), im)]
    specs += [_const_spec(c.shape) for c in consts]
    return pl.pallas_call(
        _ret_body,
        out_shape=[jax.ShapeDtypeStruct((rows, RV_W), F32)] * 2,
        grid=(n_batch, nj),
        in_specs=specs,
        out_specs=[pl.BlockSpec((tm, RV_W), fwd), pl.BlockSpec((tm, RV_W), bwd)],
        scratch_shapes=[pltpu.VMEM((RET_HEADS, RET_DK, RET_DV), F32)] * 2,
        compiler_params=_params(2),
        name="retention",
    )(rq, rk, rv, rq, rk, rv, *consts)


def _pick_tile(n, candidates):
    for c in candidates:
        if n % c == 0:
            return c
    raise ValueError(f"no tile for {n}")


def _attn_body(q_ref, k_ref, v_ref, o_ref, s_ref, *, n_ctx, n_all, ncb):
    tq = q_ref.shape[0]
    heads = [slice(hh * HEAD_PAD, (hh + 1) * HEAD_PAD) for hh in range(2)]

    def attend(n_keys, tk):
        nt = n_keys // tk
        qs = [q_ref[:, hs] for hs in heads]

        def qk(t, mrun):
            r0 = pl.multiple_of(t * tk, tk)
            out = []
            for hh, hs in enumerate(heads):
                s = lax.dot_general(qs[hh], k_ref[pl.ds(r0, tk), hs], (((1,), (1,)), ((), ())),
                                    preferred_element_type=F32)
                s_ref[hh, t, :, 0:tk] = s
                m = mrun[hh]
                for cc in range(tk // LANES):
                    m = jnp.maximum(m, s[:, cc * LANES:(cc + 1) * LANES])
                out.append(m)
            return tuple(out)

        mrun = lax.fori_loop(0, nt, qk, tuple(jnp.full((tq, LANES), -jnp.inf, F32) for _ in heads),
                             unroll=True)
        mrow = [jnp.max(m, axis=-1, keepdims=True) for m in mrun]

        def pv(t, accs):
            r0 = pl.multiple_of(t * tk, tk)
            out = []
            for hh, hs in enumerate(heads):
                p = jnp.exp2(s_ref[hh, t, :, 0:tk] - mrow[hh]).astype(BF16)
                out.append(accs[hh] + _dot(p, v_ref[pl.ds(r0, tk), hs]))
            return tuple(out)

        accs = lax.fori_loop(0, nt, pv, tuple(jnp.zeros((tq, HEAD_PAD), F32) for _ in heads),
                             unroll=True)
        o_ref[...] = jnp.concatenate([a[:, :MLA_V] / a[:, MLA_V:MLA_V + 1] for a in accs],
                                     axis=-1).astype(BF16)

    j = pl.program_id(2)

    @pl.when(j < ncb)
    def _():
        attend(n_ctx, _pick_tile(n_ctx, (256, 128)))

    @pl.when(j >= ncb)
    def _():
        attend(n_all, _pick_tile(n_all, ATTN_KEY_TILES))


def _attn_call(qm, km, vm, *, n_batch, nj, ncb, n_ctx):
    rows = qm.shape[0]
    tm = ROW_BLOCK
    t_all = nj * tm
    tk = _pick_tile(t_all, ATTN_KEY_TILES)
    qmap = lambda b, hp, j: (b * nj + j, hp)
    kmap = lambda b, hp, j: (b, hp)
    return pl.pallas_call(
        functools.partial(_attn_body, n_ctx=n_ctx, n_all=t_all, ncb=ncb),
        out_shape=jax.ShapeDtypeStruct((rows, MLA_O), BF16),
        grid=(n_batch, MLA_HEADS // 2, nj),
        in_specs=[pl.BlockSpec((tm, 2 * HEAD_PAD), qmap),
                  pl.BlockSpec((t_all, 2 * HEAD_PAD), kmap),
                  pl.BlockSpec((t_all, 2 * HEAD_PAD), kmap)],
        out_specs=pl.BlockSpec((tm, 2 * MLA_V), qmap),
        scratch_shapes=[pltpu.VMEM((2, t_all // tk, tm, tk), F32)],
        compiler_params=_params(3),
        name="mla_attention",
    )(qm, km, vm)


def _merge_body(yf_ref, yb_ref, rg_ref, cb_ref, cu_ref, cup_ref, cun_ref, om_ref, gt_ref, x_ref,
                g1_ref, a2_ref, b2_ref, gn_ref, cw_ref, wro_ref, wco_ref, wmo_ref, wout_ref,
                wrh_ref, wrl_ref, rb_ref, xmid_ref, h2_ref, rt_ref, *, nj, ncb, d_model):
    tm = x_ref.shape[0]
    j = pl.program_id(0) % nj
    seg_first = jnp.logical_or(j == 0, j == ncb)
    seg_last = jnp.logical_or(j == ncb - 1, j == nj - 1)

    y = yf_ref[...] + yb_ref[...]
    yn = jnp.concatenate([_rms(y[:, hd * RET_DV:(hd + 1) * RET_DV]) for hd in range(RET_HEADS)],
                         axis=-1) * gn_ref[...]
    y_ret = _dot((rg_ref[...].astype(F32) * yn).astype(BF16), wro_ref[...])

    u = cu_ref[...].astype(F32)
    ridx = lax.broadcasted_iota(jnp.int32, u.shape, 0)
    prev_row = cup_ref[...].astype(F32)[BF16_SUBLANES - 1:, :] * jnp.where(seg_first, 0.0, 1.0)
    next_row = cun_ref[...].astype(F32)[0:1, :] * jnp.where(seg_last, 0.0, 1.0)
    u_prev = jnp.where(ridx == 0, prev_row, pltpu.roll(u, 1, 0))
    u_next = jnp.where(ridx == tm - 1, next_row, pltpu.roll(u, tm - 1, 0))
    conv = u_prev * cw_ref[0:1, :] + u * cw_ref[1:2, :] + u_next * cw_ref[2:3, :]
    y_conv = _dot((cb_ref[...].astype(F32) * conv).astype(BF16), wco_ref[...])

    y_mla = _dot(om_ref[...], wmo_ref[...])

    merged = (gt_ref[:, 0:d_model].astype(F32) * y_ret
              + gt_ref[:, d_model:2 * d_model].astype(F32) * y_conv
              + gt_ref[:, 2 * d_model:3 * d_model].astype(F32) * y_mla)
    x_mid = x_ref[...] + g1_ref[...] * _dot(merged.astype(BF16), wout_ref[...])
    xmid_ref[...] = x_mid

    h2 = _rms(x_mid) * a2_ref[...] + b2_ref[...]
    h_hi, h_lo = _split_bf16(h2)
    logits = _dot(h_hi, wrh_ref[...]) + _dot(h_hi, wrl_ref[...]) + _dot(h_lo, wrh_ref[...])
    cls, w_lo, w_hi = _route_rows(jnp.transpose(logits)[:N_EXPERTS, :], rb_ref[...])
    rt_ref[...] = jnp.concatenate(
        [cls.astype(F32), w_lo, w_hi, jnp.zeros((ROUTE_ROWS - 3, tm), F32)], axis=0)
    h2_ref[...] = h_hi


def _top2_of4(v):
    def first_max(rows):
        best, idx = rows[0], jnp.zeros(rows[0].shape, jnp.int32)
        for e in range(1, len(rows)):
            better = rows[e] > best
            idx = jnp.where(better, e, idx)
            best = jnp.where(better, rows[e], best)
        return best, idx

    b1, i1 = first_max(v)
    b2, i2 = first_max([jnp.where(i1 == e, -jnp.inf, v[e]) for e in range(len(v))])
    return i1, i2, b1, b2


def _route_rows(logits_t, bias):
    scores = jax.nn.sigmoid(logits_t)
    biased = scores + bias
    row = lambda a, e: a[e:e + 1, :]
    best = None
    for g in range(N_GROUPS):
        v = [row(biased, g * EXPERTS_PER_GROUP + e) for e in range(EXPERTS_PER_GROUP)]
        i1, i2, b1, b2 = _top2_of4(v)
        cand = (b1 + b2, jnp.full(i1.shape, g, jnp.int32), i1, i2)
        if best is None:
            best = cand
        else:
            better = cand[0] > best[0]
            best = tuple(jnp.where(better, c, o) for c, o in zip(cand, best))
    _, g_sel, i1, i2 = best
    lo = jnp.minimum(i1, i2)
    hi = jnp.maximum(i1, i2)
    e_lo = g_sel * EXPERTS_PER_GROUP + lo
    e_hi = g_sel * EXPERTS_PER_GROUP + hi
    s_lo = jnp.zeros_like(best[0])
    s_hi = jnp.zeros_like(best[0])
    for e in range(N_EXPERTS):
        s_lo = jnp.where(e_lo == e, row(scores, e), s_lo)
        s_hi = jnp.where(e_hi == e, row(scores, e), s_hi)
    total = s_lo + s_hi
    pair_base = jnp.where(lo == 0, 0, jnp.where(lo == 1, 3, 5))
    cls = g_sel * len(PAIRS) + pair_base + (hi - lo - 1)
    return cls, s_lo / total, s_hi / total


def _merge_call(yf, yb, rg, cb, cu, om, gt, xa, g1, a2, b2, wts, *, nj, ncb):
    rows, d = xa.shape
    tm = ROW_BLOCK
    nb = rows // tm
    halo = BF16_SUBLANES
    per_blk = tm // halo
    row = lambda i: (i, 0)
    mod = lambda i: (2 * (i // nj) + ((i % nj) >= ncb).astype(jnp.int32), 0, 0)
    prev = lambda i: (jnp.maximum(i * per_blk - 1, 0), 0)
    nxt = lambda i: (jnp.minimum((i + 1) * per_blk, rows // halo - 1), 0)
    in_specs = [pl.BlockSpec((tm, RV_W), row), pl.BlockSpec((tm, RV_W), row),
                pl.BlockSpec((tm, RV_W), row),
                pl.BlockSpec((tm, CONV_WIDTH), row), pl.BlockSpec((tm, CONV_WIDTH), row),
                pl.BlockSpec((halo, CONV_WIDTH), prev), pl.BlockSpec((halo, CONV_WIDTH), nxt),
                pl.BlockSpec((tm, MLA_O), row), pl.BlockSpec((tm, N_BRANCH * d), row),
                pl.BlockSpec((tm, d), row),
                pl.BlockSpec((None, 1, d), mod), pl.BlockSpec((None, 1, d), mod),
                pl.BlockSpec((None, 1, d), mod)]
    in_specs += [_const_spec(w.shape) for w in wts]
    return pl.pallas_call(
        functools.partial(_merge_body, nj=nj, ncb=ncb, d_model=d),
        out_shape=[jax.ShapeDtypeStruct((rows, d), F32), jax.ShapeDtypeStruct((rows, d), BF16),
                   jax.ShapeDtypeStruct((nb * ROUTE_ROWS, tm), F32)],
        grid=(nb,),
        in_specs=in_specs,
        out_specs=[pl.BlockSpec((tm, d), row), pl.BlockSpec((tm, d), row),
                   pl.BlockSpec((ROUTE_ROWS, tm), row)],
        compiler_params=_params(1),
        name="merge_out_proj",
    )(yf, yb, rg, cb, cu, cu, cu, om, gt, xa, g1, a2, b2, *wts)


def _moe_body(ea_ref, eb_ref, nu_ref, h_ref, gw_ref, w13a_ref, w2a_ref, w13b_ref, w2b_ref, o_ref):
    t = pl.program_id(0)

    @pl.when(t < nu_ref[0])
    def _():
        h = h_ref[...]
        gw = gw_ref[...]

        def expert(w13_ref, w2_ref, wt):
            a = _dot(h, w13_ref[...])
            act = _silu(a[:, :D_EXPERT]) * a[:, D_EXPERT:] * wt
            return _dot(act.astype(BF16), w2_ref[...])

        o_ref[...] = (expert(w13a_ref, w2a_ref, gw[:, 0:1])
                      + expert(w13b_ref, w2b_ref, gw[:, 1:2])).astype(BF16)

    @pl.when(t >= nu_ref[0])
    def _():
        o_ref[...] = jnp.zeros_like(o_ref)


def _moe_call(tile_ea, tile_eb, n_used, hs, gw, w13, w2):
    npad, d = hs.shape
    tmo = MOE_TILE
    row = lambda t, ea, eb, nu: (t, 0)
    wa = lambda t, ea, eb, nu: (ea[t], 0, 0)
    wb = lambda t, ea, eb, nu: (eb[t], 0, 0)
    grid_spec = pltpu.PrefetchScalarGridSpec(
        num_scalar_prefetch=3,
        grid=(npad // tmo,),
        in_specs=[pl.BlockSpec((tmo, d), row), pl.BlockSpec((tmo, 2), row),
                  pl.BlockSpec((None, d, 2 * D_EXPERT), wa), pl.BlockSpec((None, D_EXPERT, d), wa),
                  pl.BlockSpec((None, d, 2 * D_EXPERT), wb), pl.BlockSpec((None, D_EXPERT, d), wb)],
        out_specs=pl.BlockSpec((tmo, d), row))
    return pl.pallas_call(
        _moe_body,
        out_shape=jax.ShapeDtypeStruct((npad, d), BF16),
        grid_spec=grid_spec,
        compiler_params=_params(1),
        name="moe_experts",
    )(tile_ea, tile_eb, n_used, hs, gw, w13, w2, w13, w2)


def _dispatch(cls, gate_w, n_tok):
    tmo = MOE_TILE
    n_tiles = n_tok // tmo + N_CLASSES
    onehot = (cls[:, None] == jnp.arange(N_CLASSES, dtype=jnp.int32)[None, :]).astype(jnp.int32)
    ranks = jnp.cumsum(onehot, axis=0) - onehot
    rank = jnp.sum(ranks * onehot, axis=1)
    counts = jnp.sum(onehot, axis=0)
    tiles_per = (counts + tmo - 1) // tmo
    tile_end = jnp.cumsum(tiles_per)
    offs = (tile_end - tiles_per) * tmo
    dest = offs[cls] + rank
    npad = n_tiles * tmo
    record = jnp.concatenate([jnp.arange(n_tok, dtype=F32)[:, None], gate_w,
                              jnp.zeros((n_tok, 1), F32)], axis=1)
    slots = jnp.zeros((npad, record.shape[1]), F32).at[dest].set(record)
    src = slots[:, 0].astype(jnp.int32)
    gw = slots[:, 1:3]
    tile_ids = jnp.arange(n_tiles, dtype=jnp.int32)
    tile_cls = jnp.sum((tile_end[None, :] <= tile_ids[:, None]).astype(jnp.int32), axis=1)
    tile_cls = jnp.minimum(tile_cls, N_CLASSES - 1)
    pa = jnp.asarray([p[0] for p in PAIRS], jnp.int32)
    pb = jnp.asarray([p[1] for p in PAIRS], jnp.int32)
    base = (tile_cls // len(PAIRS)) * EXPERTS_PER_GROUP
    tile_ea = base + pa[tile_cls % len(PAIRS)]
    tile_eb = base + pb[tile_cls % len(PAIRS)]
    n_used = tile_end[-1:].astype(jnp.int32)
    return src, dest, gw, tile_ea, tile_eb, n_used


def _final_body(x_ref, f_ref, g2_ref, fn_ref, o_ref):
    o_ref[...] = _rms(x_ref[...] + g2_ref[...] * f_ref[...].astype(F32)) * fn_ref[...]


def _final_call(xmid, fg, g2, final_norm, *, n_batch, nj, ncb):
    rows, d = xmid.shape
    tm = ROW_BLOCK
    njl = nj - ncb
    src = lambda b, j: (b * nj + ncb + j, 0)
    return pl.pallas_call(
        _final_body,
        out_shape=jax.ShapeDtypeStruct((n_batch * njl * tm, d), F32),
        grid=(n_batch, njl),
        in_specs=[pl.BlockSpec((tm, d), src), pl.BlockSpec((tm, d), src),
                  pl.BlockSpec((None, 1, d), lambda b, j: (2 * b + 1, 0, 0)),
                  _const_spec((1, d))],
        out_specs=pl.BlockSpec((tm, d), lambda b, j: (b * njl + j, 0)),
        compiler_params=_params(2),
        name="final_norm",
    )(xmid, fg, g2, final_norm.reshape(1, d))


def _split_cols(w, sizes):
    out, off = [], 0
    for s in sizes:
        out.append(w[:, off:off + s])
        off += s
    out.append(w[:, off:])
    return out


def _layer_weights(w_in, w_uq, w_ukv, q_norm, kv_norm):
    wq, wk, wv, wg, wcb, wcc, wcx, wqd, wkvd, wkr, wgate = _split_cols(w_in, IN_SIZES[:-1])
    rope_lanes = lambda w: jnp.pad(w, ((0, 0), (MLA_NOPE, HEAD_PAD - MLA_NOPE - MLA_ROPE)))
    w_ext = jnp.concatenate(
        [wq, wk, wv, wg, wcb, wcc, wcx, wqd, wkvd, rope_lanes(wkr), wgate], axis=1).astype(BF16)

    uq = w_uq.reshape(MLA_Q_RANK, MLA_HEADS, MLA_NOPE + MLA_ROPE)
    tail = HEAD_PAD - MLA_NOPE - MLA_ROPE
    uq_pad = jnp.pad(uq, ((0, 0), (0, 0), (0, tail)))
    wuq = uq_pad.reshape(MLA_Q_RANK, MLA_W).astype(BF16)

    ukv = w_ukv.reshape(MLA_KV_RANK, MLA_HEADS, MLA_NOPE + MLA_V)
    wk_up = jnp.pad(ukv[..., :MLA_NOPE], ((0, 0), (0, 0), (0, HEAD_PAD - MLA_NOPE)))
    wk_up = wk_up.reshape(MLA_KV_RANK, MLA_W).astype(BF16)
    wv_pad = jnp.pad(ukv[..., MLA_NOPE:], ((0, 0), (0, 0), (0, HEAD_PAD - MLA_V)))
    wv_pad = wv_pad.reshape(MLA_KV_RANK, MLA_W).astype(BF16)
    ones_row = jnp.zeros((MLA_HEADS, HEAD_PAD), F32).at[:, MLA_V].set(1.0).reshape(1, MLA_W)
    q_gain = q_norm.astype(F32) * ((MLA_NOPE + MLA_ROPE) ** -0.5 * math.log2(math.e))
    return (w_ext, wuq, wk_up, wv_pad, q_gain.reshape(1, -1),
            kv_norm.reshape(1, -1).astype(F32), ones_row)


def _rotary_tables(n_ctx, n_lat):
    pos = jnp.arange(n_lat, dtype=jnp.int32)
    rows = pos // GRID_W
    cols = pos - rows * GRID_W

    def cos_sin(p, half):
        inv = ROPE_BASE ** (-jnp.arange(half, dtype=F32) / half)
        ang = p.astype(F32)[:, None] * inv[None, :]
        return lax.optimization_barrier((jnp.cos(ang), jnp.sin(ang)))

    def with_ctx(t, fill):
        return jnp.concatenate([jnp.full((n_ctx, t.shape[1]), fill, F32), t], axis=0)

    cos_p, sin_p = cos_sin(pos, RET_DK // 2)
    cr = with_ctx(jnp.tile(cos_p, (1, 4)), 1.0)
    sr = with_ctx(jnp.tile(jnp.concatenate([-sin_p, sin_p], axis=1), (1, 2)), 0.0)

    quarter = MLA_ROPE // 4
    cos_r, sin_r = cos_sin(rows, quarter)
    cos_c, sin_c = cos_sin(cols, quarter)
    cos32 = with_ctx(jnp.concatenate([cos_r] * 2 + [cos_c] * 2, axis=1), 1.0)
    sin32 = with_ctx(jnp.concatenate([-sin_r, sin_r, -sin_c, sin_c], axis=1), 0.0)
    n_all = n_ctx + n_lat
    tail = jnp.zeros((n_all, HEAD_PAD - MLA_NOPE - MLA_ROPE), F32)
    cm = jnp.concatenate([jnp.ones((n_all, MLA_NOPE), F32), cos32, tail], axis=1)
    sm = jnp.concatenate([jnp.zeros((n_all, MLA_NOPE), F32), sin32, tail], axis=1)
    return cr, sr, cm, sm


def _retention_consts(ret_decay):
    log_gf = jax.nn.log_sigmoid(ret_decay[0].astype(F32))
    log_gb = jax.nn.log_sigmoid(ret_decay[1].astype(F32))
    idx = jnp.arange(RET_CHUNK, dtype=F32)
    rel = idx[:, None] - idx[None, :]
    dm_f = jnp.where(rel >= 0, jnp.exp(log_gf[:, None, None] * jnp.maximum(rel, 0.0)[None]), 0.0)
    dm_b = jnp.where(rel < 0, jnp.exp(log_gb[:, None, None] * jnp.maximum(-rel, 0.0)[None]), 0.0)

    def lanes(t):
        return jnp.repeat(t, RET_DK, axis=1)

    xi_f = lanes(jnp.exp(log_gf[None, :] * (idx + 1.0)[:, None]))
    zt_f = lanes(jnp.exp(log_gf[None, :] * (RET_CHUNK - 1 - idx)[:, None]))
    xi_b = lanes(jnp.exp(log_gb[None, :] * (RET_CHUNK - idx)[:, None]))
    zt_b = lanes(jnp.exp(log_gb[None, :] * idx[:, None]))
    cd = lambda lg: jnp.broadcast_to(jnp.exp(lg * RET_CHUNK)[:, None, None], (RET_HEADS, 1, RET_DV))
    return dm_f, dm_b, xi_f, zt_f, xi_b, zt_b, cd(log_gf), cd(log_gb)


def kernel(x, c, ctx, c_ctx, w_ada, b_ada, norm1, norm2, w_in, ret_decay, ret_gn, w_ret_o, conv_w,
           w_conv_o, mla_q_norm, w_uq, mla_kv_norm, w_ukv, w_mla_o, w_out, w_router, router_bias,
           w1, w3, w2, final_norm):
    n_batch, n_lat, d = x.shape
    n_ctx = ctx.shape[1]
    depth = w_ada.shape[0]
    t_all = n_ctx + n_lat
    assert n_ctx % ROW_BLOCK == 0 and n_lat % ROW_BLOCK == 0 and n_lat % GRID_W == 0
    nj = t_all // ROW_BLOCK
    ncb = n_ctx // ROW_BLOCK
    n_tok = n_batch * t_all
    assert n_tok % MOE_TILE == 0
    geom = dict(n_batch=n_batch, nj=nj, ncb=ncb)

    cc = jnp.concatenate([c, c_ctx[None, :]], axis=0)
    cc = jnp.pad(cc, ((0, -cc.shape[0] % 8), (0, 0)))
    mod = _ada_call(cc, w_ada, b_ada)[:, :n_batch + 1].reshape(depth, n_batch + 1, N_MOD, d)
    pick = jnp.stack([jnp.full((n_batch,), n_batch, jnp.int32),
                      jnp.arange(n_batch, dtype=jnp.int32)], axis=1).reshape(-1)
    mod = mod[:, pick]

    tabs = _rotary_tables(n_ctx, n_lat)
    wr_hi, wr_lo = _split_bf16(jnp.pad(w_router.astype(F32), ((0, 0), (0, LANES - N_EXPERTS))))
    rbias = router_bias.astype(F32).reshape(N_EXPERTS, 1)
    xa = jnp.concatenate([ctx, x], axis=1).reshape(n_tok, d)

    out = None
    x_parts = (xa,)
    for l in range(depth):
        m = mod[l]
        rowvec = lambda v: v.reshape(2 * n_batch, 1, d)
        a1 = rowvec(norm1[l][None, :] * (1.0 + m[:, 1]))
        b1 = rowvec(m[:, 0])
        g1 = rowvec(m[:, 2])
        a2 = rowvec(norm2[l][None, :] * (1.0 + m[:, 4]))
        b2 = rowvec(m[:, 3])
        g2 = rowvec(m[:, 5])

        wts = _layer_weights(w_in[l], w_uq[l], w_ukv[l], mla_q_norm[l], mla_kv_norm[l])
        proj = _inproj_call(x_parts, a1, b1, tabs, wts, **geom)
        rq, rk, rv, rg, cb, cu, qm, km, vm, gt = proj[:10]
        if len(x_parts) == 3:
            xa = proj[10]

        yf, yb = _ret_call(rq, rk, rv, _retention_consts(ret_decay[l]), **geom)
        om = _attn_call(qm, km, vm, n_ctx=n_ctx, **geom)

        merge_wts = (ret_gn[l].reshape(1, -1).astype(F32), conv_w[l].T.astype(F32),
                     w_ret_o[l].astype(BF16), w_conv_o[l].astype(BF16), w_mla_o[l].astype(BF16),
                     w_out[l].astype(BF16), wr_hi, wr_lo, rbias)
        xmid, h2, route = _merge_call(yf, yb, rg, cb, cu, om, gt, xa, g1, a2, b2, merge_wts,
                                      nj=nj, ncb=ncb)
        route = route.reshape(-1, ROUTE_ROWS, ROW_BLOCK)
        cls = route[:, 0, :].reshape(-1).astype(jnp.int32)
        gate_w = jnp.stack([route[:, 1, :].reshape(-1), route[:, 2, :].reshape(-1)], axis=-1)
        src, dest, gw, tile_ea, tile_eb, n_used = _dispatch(cls, gate_w, n_tok)
        hs = jnp.take(h2, src, axis=0)
        w13 = jnp.concatenate([w1[l], w3[l]], axis=-1).astype(BF16)
        f_sorted = _moe_call(tile_ea, tile_eb, n_used, hs, gw, w13, w2[l].astype(BF16))
        fg = jnp.take(f_sorted, dest, axis=0)

        if l < depth - 1:
            x_parts = (xmid, fg, g2)
        else:
            out = _final_call(xmid, fg, g2, final_norm, **geom)
    return out.reshape(n_batch, n_lat, d)
```

```python
import functools
import math

import jax
import jax.numpy as jnp
from jax import lax
from jax.experimental import pallas as pl
from jax.experimental.pallas import tpu as pltpu

F32 = jnp.float32
BF16 = jnp.bfloat16

GRID_W = 64
RMS_EPS = 1e-6
ROPE_BASE = 10000.0
N_MOD = 6
RET_HEADS = 4
RET_DK = 64
RET_DV = 128
RET_CHUNK = 256
CONV_WIDTH = 512
MLA_HEADS = 8
MLA_Q_RANK = 384
MLA_KV_RANK = 256
MLA_NOPE = 64
MLA_ROPE = 32
MLA_V = 64
N_BRANCH = 3
N_EXPERTS = 16
N_GROUPS = 4
EXPERTS_PER_GROUP = N_EXPERTS // N_GROUPS
D_EXPERT = 512
IN_SIZES = (RET_HEADS * RET_DK, RET_HEADS * RET_DK, RET_HEADS * RET_DV, RET_HEADS * RET_DV,
            CONV_WIDTH, CONV_WIDTH, CONV_WIDTH, MLA_Q_RANK, MLA_KV_RANK, MLA_ROPE, 0)

LANES = 128
BF16_SUBLANES = 16
VMEM_LIMIT = 56 * 1024 * 1024

HEAD_PAD = LANES
ROW_BLOCK = 256
MOE_TILE = 256
ATTN_KEY_TILES = (768, 512, 256, 128)
ROUTE_ROWS = 8
PAIRS = [(a, b) for a in range(EXPERTS_PER_GROUP) for b in range(a + 1, EXPERTS_PER_GROUP)]
N_CLASSES = N_GROUPS * len(PAIRS)

RQ_W = RET_HEADS * RET_DK
RV_W = RET_HEADS * RET_DV
MLA_W = MLA_HEADS * HEAD_PAD
MLA_O = MLA_HEADS * MLA_V


def _const_spec(shape):
    nd = len(shape)
    return pl.BlockSpec(shape, lambda *_: (0,) * nd, pipeline_mode=pl.Buffered(1))


def _params(n_axes):
    return pltpu.CompilerParams(dimension_semantics=("arbitrary",) * n_axes,
                                vmem_limit_bytes=VMEM_LIMIT)


def _dot(a, b):
    return jnp.dot(a, b, preferred_element_type=F32)


def _split_bf16(a):
    hi = a.astype(BF16)
    lo = (a - hi.astype(F32)).astype(BF16)
    return hi, lo


def _silu(v):
    return v * jax.nn.sigmoid(v)


def _rms(v):
    return v * lax.rsqrt(jnp.mean(v * v, axis=-1, keepdims=True) + RMS_EPS)


def _ada_body(c_ref, w_ref, b_ref, o_ref):
    a_hi, a_lo = _split_bf16(_silu(c_ref[...]))
    w_hi, w_lo = _split_bf16(w_ref[...])
    o_ref[...] = _dot(a_hi, w_hi) + _dot(a_hi, w_lo) + _dot(a_lo, w_hi) + b_ref[...]


def _ada_call(cc, w_ada, b_ada):
    depth, d, nm = w_ada.shape
    rows = cc.shape[0]
    cb = nm // 4
    return pl.pallas_call(
        _ada_body,
        out_shape=jax.ShapeDtypeStruct((depth, rows, nm), F32),
        grid=(depth, nm // cb),
        in_specs=[pl.BlockSpec((rows, d), lambda l, n: (0, 0)),
                  pl.BlockSpec((None, d, cb), lambda l, n: (l, 0, n)),
                  pl.BlockSpec((None, 1, cb), lambda l, n: (l, 0, n))],
        out_specs=pl.BlockSpec((None, rows, cb), lambda l, n: (l, 0, n)),
        compiler_params=_params(2),
        name="ada_mod",
    )(cc, w_ada, b_ada.reshape(depth, 1, nm))


_O_RQ = 0
_O_RK = _O_RQ + RQ_W
_O_RV = _O_RK + RQ_W
_O_RG = _O_RV + RV_W
_O_CB = _O_RG + RV_W
_O_CC = _O_CB + CONV_WIDTH
_O_CX = _O_CC + CONV_WIDTH
_O_QD = _O_CX + CONV_WIDTH
_O_KVD = _O_QD + MLA_Q_RANK
_O_KR = _O_KVD + MLA_KV_RANK
_O_GT = _O_KR + LANES


def _rot_half(v, half):
    width = v.shape[1]
    lane = lax.broadcasted_iota(jnp.int32, v.shape, 1)
    first = (lane % (2 * half)) < half
    return jnp.where(first, pltpu.roll(v, width - half, 1), pltpu.roll(v, half, 1))


def _inproj_body(*refs, d_model, fused):
    if fused:
        xm_ref, f_ref, g2_ref = refs[:3]
        refs = refs[3:]
        xo_ref = refs[-1]
        x = xm_ref[...] + g2_ref[...] * f_ref[...].astype(F32)
        xo_ref[...] = x
    else:
        x = refs[0][...]
        refs = refs[1:]
    (a1_ref, b1_ref, cr_ref, sr_ref, cm_ref, sm_ref, w_ref, wuq_ref, wk_ref, wv_ref, qn_ref, kvn_ref,
     ones_ref, rq_ref, rk_ref, rv_ref, rg_ref, cb_ref, cu_ref, qm_ref, km_ref, vm_ref, gt_ref) = refs[:23]
    h = (_rms(x) * a1_ref[...] + b1_ref[...]).astype(BF16)

    def mm(off, width):
        return _dot(h, w_ref[:, off:off + width])

    cr = jnp.concatenate([cr_ref[...]] * (RQ_W // cr_ref.shape[1]), axis=1)
    sr = jnp.concatenate([sr_ref[...]] * (RQ_W // sr_ref.shape[1]), axis=1)
    q = mm(_O_RQ, RQ_W)
    rq_ref[...] = (q * cr + _rot_half(q, RET_DK // 2) * sr).astype(BF16)
    k = mm(_O_RK, RQ_W)
    rk_ref[...] = ((k * cr + _rot_half(k, RET_DK // 2) * sr) * (RET_DK ** -0.5)).astype(BF16)
    rv_ref[...] = mm(_O_RV, RV_W).astype(BF16)
    rg_ref[...] = _silu(mm(_O_RG, RV_W)).astype(BF16)
    cb_ref[...] = mm(_O_CB, CONV_WIDTH).astype(BF16)
    cu_ref[...] = (mm(_O_CC, CONV_WIDTH) * mm(_O_CX, CONV_WIDTH)).astype(BF16)

    cm = cm_ref[...]
    sm = sm_ref[...]
    qn = (_rms(mm(_O_QD, MLA_Q_RANK)) * qn_ref[...]).astype(BF16)
    for hd in range(MLA_HEADS):
        lo = hd * HEAD_PAD
        qa = _dot(qn, wuq_ref[:, lo:lo + HEAD_PAD])
        qm_ref[:, lo:lo + HEAD_PAD] = (qa * cm + _rot_half(qa, MLA_ROPE // 4) * sm).astype(BF16)

    kvn = (_rms(mm(_O_KVD, MLA_KV_RANK)) * kvn_ref[...]).astype(BF16)
    kr = mm(_O_KR, HEAD_PAD)
    kr = kr * cm + _rot_half(kr, MLA_ROPE // 4) * sm
    kn = _dot(kvn, wk_ref[...])
    for hd in range(MLA_HEADS):
        lo = hd * HEAD_PAD
        km_ref[:, lo:lo + HEAD_PAD] = (kn[:, lo:lo + HEAD_PAD] + kr).astype(BF16)
    vm_ref[...] = (_dot(kvn, wv_ref[...]) + ones_ref[...]).astype(BF16)

    for br in range(N_BRANCH):
        gt_ref[:, br * d_model:(br + 1) * d_model] = jax.nn.sigmoid(
            mm(_O_GT + br * d_model, d_model)).astype(BF16)


def _inproj_call(x_parts, a1, b1, tabs, wts, *, n_batch, nj, ncb):
    fused = len(x_parts) == 3
    rows, d = x_parts[0].shape
    tm = ROW_BLOCK
    row = lambda j, b: (b * nj + j, 0)
    mod = lambda j, b: (2 * b + (j >= ncb).astype(jnp.int32), 0, 0)
    tab = lambda j, b: (j, 0)
    in_specs = [pl.BlockSpec((tm, d), row)]
    if fused:
        in_specs += [pl.BlockSpec((tm, d), row), pl.BlockSpec((None, 1, d), mod)]
    in_specs += [pl.BlockSpec((None, 1, d), mod), pl.BlockSpec((None, 1, d), mod)]
    in_specs += [pl.BlockSpec((tm, t.shape[1]), tab) for t in tabs]
    in_specs += [_const_spec(w.shape) for w in wts]
    widths = (RQ_W, RQ_W, RV_W, RV_W, CONV_WIDTH, CONV_WIDTH, MLA_W, MLA_W, MLA_W, N_BRANCH * d)
    out_shape = [jax.ShapeDtypeStruct((rows, w), BF16) for w in widths]
    out_specs = [pl.BlockSpec((tm, w), row) for w in widths]
    if fused:
        out_shape.append(jax.ShapeDtypeStruct((rows, d), F32))
        out_specs.append(pl.BlockSpec((tm, d), row))
    return pl.pallas_call(
        functools.partial(_inproj_body, d_model=d, fused=fused),
        out_shape=out_shape,
        grid=(nj, n_batch),
        in_specs=in_specs,
        out_specs=out_specs,
        compiler_params=_params(2),
        name="in_proj",
    )(*x_parts, a1, b1, *tabs, *wts)


def _ret_direction(q_ref, k_ref, v_ref, y_ref, s_ref, dm_ref, xi_ref, zt_ref, cd_ref, chunk_order):
    states = [s_ref[hd] for hd in range(RET_HEADS)]
    for c in chunk_order:
        rows = slice(c * RET_CHUNK, (c + 1) * RET_CHUNK)
        q = q_ref[rows, :]
        k = k_ref[rows, :]
        v = v_ref[rows, :]
        qx = (q.astype(F32) * xi_ref[...]).astype(BF16)
        kz = (k.astype(F32) * zt_ref[...]).astype(BF16)
        for hd in range(RET_HEADS):
            ks = slice(hd * RET_DK, (hd + 1) * RET_DK)
            vs = slice(hd * RET_DV, (hd + 1) * RET_DV)
            vh = v[:, vs]
            sc = lax.dot_general(q[:, ks], k[:, ks], (((1,), (1,)), ((), ())),
                                 preferred_element_type=F32)
            inner = _dot((sc * dm_ref[hd]).astype(BF16), vh)
            cross = _dot(qx[:, ks], states[hd].astype(BF16))
            y_ref[rows, vs] = inner + cross
            upd = lax.dot_general(kz[:, ks], vh, (((0,), (0,)), ((), ())),
                                  preferred_element_type=F32)
            states[hd] = cd_ref[hd] * states[hd] + upd
    for hd in range(RET_HEADS):
        s_ref[hd] = states[hd]


def _ret_body(qf_ref, kf_ref, vf_ref, qb_ref, kb_ref, vb_ref,
              dmf_ref, dmb_ref, xif_ref, ztf_ref, xib_ref, ztb_ref, cdf_ref, cdb_ref,
              yf_ref, yb_ref, sf_ref, sb_ref):
    @pl.when(pl.program_id(1) == 0)
    def _():
        sf_ref[...] = jnp.zeros_like(sf_ref)
        sb_ref[...] = jnp.zeros_like(sb_ref)

    n_chunks = ROW_BLOCK // RET_CHUNK
    _ret_direction(qf_ref, kf_ref, vf_ref, yf_ref, sf_ref, dmf_ref, xif_ref, ztf_ref, cdf_ref,
                   range(n_chunks))
    _ret_direction(qb_ref, kb_ref, vb_ref, yb_ref, sb_ref, dmb_ref, xib_ref, ztb_ref, cdb_ref,
                   range(n_chunks - 1, -1, -1))


def _ret_call(rq, rk, rv, consts, *, n_batch, nj, ncb):
    rows = rq.shape[0]
    tm = ROW_BLOCK
    fwd = lambda b, s: (b * nj + s, 0)

    def bwd(b, s):
        return (b * nj + jnp.where(s < ncb, ncb - 1 - s, nj - 1 - (s - ncb)), 0)

    specs = []
    for im in (fwd, bwd):
        specs += [pl.BlockSpec((tm, RQ_W), im), pl.BlockSpec((tm, RQ_W), im),
                  pl.BlockSpec((tm, RV_W), im)]
    specs += [_const_spec(c.shape) for c in consts]
    return pl.pallas_call(
        _ret_body,
        out_shape=[jax.ShapeDtypeStruct((rows, RV_W), F32)] * 2,
        grid=(n_batch, nj),
        in_specs=specs,
        out_specs=[pl.BlockSpec((tm, RV_W), fwd), pl.BlockSpec((tm, RV_W), bwd)],
        scratch_shapes=[pltpu.VMEM((RET_HEADS, RET_DK, RET_DV), F32)] * 2,
        compiler_params=_params(2),
        name="retention",
    )(rq, rk, rv, rq, rk, rv, *consts)


def _pick_tile(n, candidates):
    for c in candidates:
        if n % c == 0:
            return c
    raise ValueError(f"no tile for {n}")


def _attn_body(q_ref, k_ref, v_ref, o_ref, s_ref, *, n_ctx, n_all, ncb):
    tq = q_ref.shape[0]
    heads = [slice(hh * HEAD_PAD, (hh + 1) * HEAD_PAD) for hh in range(2)]

    def attend(n_keys, tk):
        nt = n_keys // tk
        qs = [q_ref[:, hs] for hs in heads]

        def qk(t, mrun):
            r0 = pl.multiple_of(t * tk, tk)
            out = []
            for hh, hs in enumerate(heads):
                s = lax.dot_general(qs[hh], k_ref[pl.ds(r0, tk), hs], (((1,), (1,)), ((), ())),
                                    preferred_element_type=F32)
                s_ref[hh, t, :, 0:tk] = s
                m = mrun[hh]
                for cc in range(tk // LANES):
                    m = jnp.maximum(m, s[:, cc * LANES:(cc + 1) * LANES])
                out.append(m)
            return tuple(out)

        mrun = lax.fori_loop(0, nt, qk, tuple(jnp.full((tq, LANES), -jnp.inf, F32) for _ in heads),
                             unroll=True)
        mrow = [jnp.max(m, axis=-1, keepdims=True) for m in mrun]

        def pv(t, accs):
            r0 = pl.multiple_of(t * tk, tk)
            out = []
            for hh, hs in enumerate(heads):
                p = jnp.exp2(s_ref[hh, t, :, 0:tk] - mrow[hh]).astype(BF16)
                out.append(accs[hh] + _dot(p, v_ref[pl.ds(r0, tk), hs]))
            return tuple(out)

        accs = lax.fori_loop(0, nt, pv, tuple(jnp.zeros((tq, HEAD_PAD), F32) for _ in heads),
                             unroll=True)
        o_ref[...] = jnp.concatenate([a[:, :MLA_V] / a[:, MLA_V:MLA_V + 1] for a in accs],
                                     axis=-1).astype(BF16)

    j = pl.program_id(2)

    @pl.when(j < ncb)
    def _():
        attend(n_ctx, _pick_tile(n_ctx, (256, 128)))

    @pl.when(j >= ncb)
    def _():
        attend(n_all, _pick_tile(n_all, ATTN_KEY_TILES))


def _attn_call(qm, km, vm, *, n_batch, nj, ncb, n_ctx):
    rows = qm.shape[0]
    tm = ROW_BLOCK
    t_all = nj * tm
    tk = _pick_tile(t_all, ATTN_KEY_TILES)
    qmap = lambda b, hp, j: (b * nj + j, hp)
    kmap = lambda b, hp, j: (b, hp)
    return pl.pallas_call(
        functools.partial(_attn_body, n_ctx=n_ctx, n_all=t_all, ncb=ncb),
        out_shape=jax.ShapeDtypeStruct((rows, MLA_O), BF16),
        grid=(n_batch, MLA_HEADS // 2, nj),
        in_specs=[pl.BlockSpec((tm, 2 * HEAD_PAD), qmap),
                  pl.BlockSpec((t_all, 2 * HEAD_PAD), kmap),
                  pl.BlockSpec((t_all, 2 * HEAD_PAD), kmap)],
        out_specs=pl.BlockSpec((tm, 2 * MLA_V), qmap),
        scratch_shapes=[pltpu.VMEM((2, t_all // tk, tm, tk), F32)],
        compiler_params=_params(3),
        name="mla_attention",
    )(qm, km, vm)


def _merge_body(yf_ref, yb_ref, rg_ref, cb_ref, cu_ref, cup_ref, cun_ref, om_ref, gt_ref, x_ref,
                g1_ref, a2_ref, b2_ref, gn_ref, cw_ref, wro_ref, wco_ref, wmo_ref, wout_ref,
                wrh_ref, wrl_ref, rb_ref, xmid_ref, h2_ref, rt_ref, *, nj, ncb, d_model):
    tm = x_ref.shape[0]
    j = pl.program_id(0) % nj
    seg_first = jnp.logical_or(j == 0, j == ncb)
    seg_last = jnp.logical_or(j == ncb - 1, j == nj - 1)

    y = yf_ref[...] + yb_ref[...]
    yn = jnp.concatenate([_rms(y[:, hd * RET_DV:(hd + 1) * RET_DV]) for hd in range(RET_HEADS)],
                         axis=-1) * gn_ref[...]
    y_ret = _dot((rg_ref[...].astype(F32) * yn).astype(BF16), wro_ref[...])

    u = cu_ref[...].astype(F32)
    ridx = lax.broadcasted_iota(jnp.int32, u.shape, 0)
    prev_row = cup_ref[...].astype(F32)[BF16_SUBLANES - 1:, :] * jnp.where(seg_first, 0.0, 1.0)
    next_row = cun_ref[...].astype(F32)[0:1, :] * jnp.where(seg_last, 0.0, 1.0)
    u_prev = jnp.where(ridx == 0, prev_row, pltpu.roll(u, 1, 0))
    u_next = jnp.where(ridx == tm - 1, next_row, pltpu.roll(u, tm - 1, 0))
    conv = u_prev * cw_ref[0:1, :] + u * cw_ref[1:2, :] + u_next * cw_ref[2:3, :]
    y_conv = _dot((cb_ref[...].astype(F32) * conv).astype(BF16), wco_ref[...])

    y_mla = _dot(om_ref[...], wmo_ref[...])

    merged = (gt_ref[:, 0:d_model].astype(F32) * y_ret
              + gt_ref[:, d_model:2 * d_model].astype(F32) * y_conv
              + gt_ref[:, 2 * d_model:3 * d_model].astype(F32) * y_mla)
    x_mid = x_ref[...] + g1_ref[...] * _dot(merged.astype(BF16), wout_ref[...])
    xmid_ref[...] = x_mid

    h2 = _rms(x_mid) * a2_ref[...] + b2_ref[...]
    h_hi, h_lo = _split_bf16(h2)
    logits = _dot(h_hi, wrh_ref[...]) + _dot(h_hi, wrl_ref[...]) + _dot(h_lo, wrh_ref[...])
    cls, w_lo, w_hi = _route_rows(jnp.transpose(logits)[:N_EXPERTS, :], rb_ref[...])
    rank, counts = _block_ranks(cls.astype(F32))
    rt_ref[...] = jnp.concatenate(
        [cls.astype(F32), w_lo, w_hi, rank, counts, jnp.zeros((ROUTE_ROWS - 5, tm), F32)], axis=0)
    h2_ref[...] = h_hi


def _top2_of4(v):
    def first_max(rows):
        best, idx = rows[0], jnp.zeros(rows[0].shape, jnp.int32)
        for e in range(1, len(rows)):
            better = rows[e] > best
            idx = jnp.where(better, e, idx)
            best = jnp.where(better, rows[e], best)
        return best, idx

    b1, i1 = first_max(v)
    b2, i2 = first_max([jnp.where(i1 == e, -jnp.inf, v[e]) for e in range(len(v))])
    return i1, i2, b1, b2


def _route_rows(logits_t, bias):
    scores = jax.nn.sigmoid(logits_t)
    biased = scores + bias
    row = lambda a, e: a[e:e + 1, :]
    best = None
    for g in range(N_GROUPS):
        v = [row(biased, g * EXPERTS_PER_GROUP + e) for e in range(EXPERTS_PER_GROUP)]
        i1, i2, b1, b2 = _top2_of4(v)
        cand = (b1 + b2, jnp.full(i1.shape, g, jnp.int32), i1, i2)
        if best is None:
            best = cand
        else:
            better = cand[0] > best[0]
            best = tuple(jnp.where(better, c, o) for c, o in zip(cand, best))
    _, g_sel, i1, i2 = best
    lo = jnp.minimum(i1, i2)
    hi = jnp.maximum(i1, i2)
    e_lo = g_sel * EXPERTS_PER_GROUP + lo
    e_hi = g_sel * EXPERTS_PER_GROUP + hi
    s_lo = jnp.zeros_like(best[0])
    s_hi = jnp.zeros_like(best[0])
    for e in range(N_EXPERTS):
        s_lo = jnp.where(e_lo == e, row(scores, e), s_lo)
        s_hi = jnp.where(e_hi == e, row(scores, e), s_hi)
    total = s_lo + s_hi
    pair_base = jnp.where(lo == 0, 0, jnp.where(lo == 1, 3, 5))
    cls = g_sel * len(PAIRS) + pair_base + (hi - lo - 1)
    return cls, s_lo / total, s_hi / total


def _block_ranks(cls_row):
    n = cls_row.shape[1]
    cls_col = jnp.transpose(jnp.broadcast_to(cls_row, (LANES, n)))[:, 0:1]
    ii = lax.broadcasted_iota(jnp.int32, (n, n), 0)
    jj = lax.broadcasted_iota(jnp.int32, (n, n), 1)
    earlier_same = jnp.logical_and(cls_col == cls_row, ii < jj)
    rank = jnp.sum(jnp.where(earlier_same, 1.0, 0.0), axis=0, keepdims=True)
    lane = lax.broadcasted_iota(jnp.int32, (1, n), 1)
    counts = jnp.zeros((1, n), F32)
    for c in range(N_CLASSES):
        cnt = jnp.sum(jnp.where(cls_row == c, 1.0, 0.0), axis=1, keepdims=True)
        counts = jnp.where(lane == c, cnt, counts)
    return rank, counts


def _merge_call(yf, yb, rg, cb, cu, om, gt, xa, g1, a2, b2, wts, *, nj, ncb):
    rows, d = xa.shape
    tm = ROW_BLOCK
    nb = rows // tm
    halo = BF16_SUBLANES
    per_blk = tm // halo
    row = lambda i: (i, 0)
    mod = lambda i: (2 * (i // nj) + ((i % nj) >= ncb).astype(jnp.int32), 0, 0)
    prev = lambda i: (jnp.maximum(i * per_blk - 1, 0), 0)
    nxt = lambda i: (jnp.minimum((i + 1) * per_blk, rows // halo - 1), 0)
    in_specs = [pl.BlockSpec((tm, RV_W), row), pl.BlockSpec((tm, RV_W), row),
                pl.BlockSpec((tm, RV_W), row),
                pl.BlockSpec((tm, CONV_WIDTH), row), pl.BlockSpec((tm, CONV_WIDTH), row),
                pl.BlockSpec((halo, CONV_WIDTH), prev), pl.BlockSpec((halo, CONV_WIDTH), nxt),
                pl.BlockSpec((tm, MLA_O), row), pl.BlockSpec((tm, N_BRANCH * d), row),
                pl.BlockSpec((tm, d), row),
                pl.BlockSpec((None, 1, d), mod), pl.BlockSpec((None, 1, d), mod),
                pl.BlockSpec((None, 1, d), mod)]
    in_specs += [_const_spec(w.shape) for w in wts]
    return pl.pallas_call(
        functools.partial(_merge_body, nj=nj, ncb=ncb, d_model=d),
        out_shape=[jax.ShapeDtypeStruct((rows, d), F32), jax.ShapeDtypeStruct((rows, d), BF16),
                   jax.ShapeDtypeStruct((nb * ROUTE_ROWS, tm), F32)],
        grid=(nb,),
        in_specs=in_specs,
        out_specs=[pl.BlockSpec((tm, d), row), pl.BlockSpec((tm, d), row),
                   pl.BlockSpec((ROUTE_ROWS, tm), row)],
        compiler_params=_params(1),
        name="merge_out_proj",
    )(yf, yb, rg, cb, cu, cu, cu, om, gt, xa, g1, a2, b2, *wts)


def _moe_body(ea_ref, eb_ref, nu_ref, h_ref, gw_ref, w1a_ref, w3a_ref, w2a_ref, w1b_ref, w3b_ref,
              w2b_ref, o_ref, w13a_s, w2a_s, w13b_s, w2b_s):
    t = pl.program_id(0)
    used = t < nu_ref[0]
    prev = jnp.maximum(t - 1, 0)

    def refresh(e_ref, w1_ref, w3_ref, w2_ref, w13_s, w2_s):
        @pl.when(jnp.logical_and(used, jnp.logical_or(t == 0, e_ref[t] != e_ref[prev])))
        def _():
            w13_s[:, :D_EXPERT] = w1_ref[...].astype(BF16)
            w13_s[:, D_EXPERT:] = w3_ref[...].astype(BF16)
            w2_s[...] = w2_ref[...].astype(BF16)

    refresh(ea_ref, w1a_ref, w3a_ref, w2a_ref, w13a_s, w2a_s)
    refresh(eb_ref, w1b_ref, w3b_ref, w2b_ref, w13b_s, w2b_s)

    @pl.when(used)
    def _():
        h = h_ref[...]
        gw = gw_ref[...]

        def expert(w13_s, w2_s, wt):
            a = _dot(h, w13_s[...])
            act = _silu(a[:, :D_EXPERT]) * a[:, D_EXPERT:] * wt
            return _dot(act.astype(BF16), w2_s[...])

        o_ref[...] = (expert(w13a_s, w2a_s, gw[:, 0:1])
                      + expert(w13b_s, w2b_s, gw[:, 1:2])).astype(BF16)

    @pl.when(jnp.logical_not(used))
    def _():
        o_ref[...] = jnp.zeros_like(o_ref)


def _moe_call(tile_ea, tile_eb, n_used, hs, gw, w1, w3, w2):
    npad, d = hs.shape
    tmo = MOE_TILE
    row = lambda t, ea, eb, nu: (t, 0)
    wa = lambda t, ea, eb, nu: (ea[t], 0, 0)
    wb = lambda t, ea, eb, nu: (eb[t], 0, 0)
    up = lambda im: pl.BlockSpec((None, d, D_EXPERT), im)
    down = lambda im: pl.BlockSpec((None, D_EXPERT, d), im)
    grid_spec = pltpu.PrefetchScalarGridSpec(
        num_scalar_prefetch=3,
        grid=(npad // tmo,),
        in_specs=[pl.BlockSpec((tmo, d), row), pl.BlockSpec((tmo, 2), row),
                  up(wa), up(wa), down(wa), up(wb), up(wb), down(wb)],
        out_specs=pl.BlockSpec((tmo, d), row),
        scratch_shapes=[pltpu.VMEM((d, 2 * D_EXPERT), BF16), pltpu.VMEM((D_EXPERT, d), BF16)] * 2)
    return pl.pallas_call(
        _moe_body,
        out_shape=jax.ShapeDtypeStruct((npad, d), BF16),
        grid_spec=grid_spec,
        compiler_params=_params(1),
        name="moe_experts",
    )(tile_ea, tile_eb, n_used, hs, gw, w1, w3, w2, w1, w3, w2)


def _dispatch(route, n_tok):
    tmo = MOE_TILE
    n_tiles = n_tok // tmo + N_CLASSES
    npad = n_tiles * tmo
    cls = route[:, 0, :].astype(jnp.int32)
    rank = route[:, 3, :].astype(jnp.int32)
    counts = route[:, 4, :N_CLASSES].astype(jnp.int32)
    tiles_per = (jnp.sum(counts, axis=0) + tmo - 1) // tmo
    tile_end = jnp.cumsum(tiles_per)
    offs = (tile_end - tiles_per) * tmo
    block_base = offs[None, :] + jnp.cumsum(counts, axis=0) - counts
    classes = jnp.arange(N_CLASSES, dtype=jnp.int32)
    base = jnp.sum(jnp.where(cls[:, :, None] == classes, block_base[:, None, :], 0), axis=-1)
    dest = (base + rank).reshape(-1)
    record = jnp.stack([jnp.arange(n_tok, dtype=F32), route[:, 1, :].reshape(-1),
                        route[:, 2, :].reshape(-1), jnp.zeros((n_tok,), F32)], axis=1)
    slots = jnp.zeros((npad, record.shape[1]), F32).at[dest].set(record)
    src = slots[:, 0].astype(jnp.int32)
    gw = slots[:, 1:3]
    tile_ids = jnp.arange(n_tiles, dtype=jnp.int32)
    tile_cls = jnp.sum((tile_end[None, :] <= tile_ids[:, None]).astype(jnp.int32), axis=1)
    tile_cls = jnp.minimum(tile_cls, N_CLASSES - 1)
    pa = jnp.asarray([p[0] for p in PAIRS], jnp.int32)
    pb = jnp.asarray([p[1] for p in PAIRS], jnp.int32)
    group = (tile_cls // len(PAIRS)) * EXPERTS_PER_GROUP
    tile_ea = group + pa[tile_cls % len(PAIRS)]
    tile_eb = group + pb[tile_cls % len(PAIRS)]
    n_used = tile_end[-1:].astype(jnp.int32)
    return src, dest, gw, tile_ea, tile_eb, n_used


def _final_body(x_ref, f_ref, g2_ref, fn_ref, o_ref):
    o_ref[...] = _rms(x_ref[...] + g2_ref[...] * f_ref[...].astype(F32)) * fn_ref[...]


def _final_call(xmid, fg, g2, final_norm, *, n_batch, nj, ncb):
    rows, d = xmid.shape
    tm = ROW_BLOCK
    njl = nj - ncb
    src = lambda b, j: (b * nj + ncb + j, 0)
    return pl.pallas_call(
        _final_body,
        out_shape=jax.ShapeDtypeStruct((n_batch * njl * tm, d), F32),
        grid=(n_batch, njl),
        in_specs=[pl.BlockSpec((tm, d), src), pl.BlockSpec((tm, d), src),
                  pl.BlockSpec((None, 1, d), lambda b, j: (2 * b + 1, 0, 0)),
                  _const_spec((1, d))],
        out_specs=pl.BlockSpec((tm, d), lambda b, j: (b * njl + j, 0)),
        compiler_params=_params(2),
        name="final_norm",
    )(xmid, fg, g2, final_norm.reshape(1, d))


def _split_cols(w, sizes):
    out, off = [], 0
    for s in sizes:
        out.append(w[:, off:off + s])
        off += s
    out.append(w[:, off:])
    return out


def _layer_weights(w_in, w_uq, w_ukv, q_norm, kv_norm):
    wq, wk, wv, wg, wcb, wcc, wcx, wqd, wkvd, wkr, wgate = _split_cols(w_in, IN_SIZES[:-1])
    rope_lanes = lambda w: jnp.pad(w, ((0, 0), (MLA_NOPE, HEAD_PAD - MLA_NOPE - MLA_ROPE)))
    w_ext = jnp.concatenate(
        [wq, wk, wv, wg, wcb, wcc, wcx, wqd, wkvd, rope_lanes(wkr), wgate], axis=1).astype(BF16)

    uq = w_uq.reshape(MLA_Q_RANK, MLA_HEADS, MLA_NOPE + MLA_ROPE)
    tail = HEAD_PAD - MLA_NOPE - MLA_ROPE
    uq_pad = jnp.pad(uq, ((0, 0), (0, 0), (0, tail)))
    wuq = uq_pad.reshape(MLA_Q_RANK, MLA_W).astype(BF16)

    ukv = w_ukv.reshape(MLA_KV_RANK, MLA_HEADS, MLA_NOPE + MLA_V)
    wk_up = jnp.pad(ukv[..., :MLA_NOPE], ((0, 0), (0, 0), (0, HEAD_PAD - MLA_NOPE)))
    wk_up = wk_up.reshape(MLA_KV_RANK, MLA_W).astype(BF16)
    wv_pad = jnp.pad(ukv[..., MLA_NOPE:], ((0, 0), (0, 0), (0, HEAD_PAD - MLA_V)))
    wv_pad = wv_pad.reshape(MLA_KV_RANK, MLA_W).astype(BF16)
    ones_row = jnp.zeros((MLA_HEADS, HEAD_PAD), F32).at[:, MLA_V].set(1.0).reshape(1, MLA_W)
    q_gain = q_norm.astype(F32) * ((MLA_NOPE + MLA_ROPE) ** -0.5 * math.log2(math.e))
    return (w_ext, wuq, wk_up, wv_pad, q_gain.reshape(1, -1),
            kv_norm.reshape(1, -1).astype(F32), ones_row)


def _rotary_tables(n_ctx, n_lat):
    pos = jnp.arange(n_lat, dtype=jnp.int32)
    rows = pos // GRID_W
    cols = pos - rows * GRID_W

    def cos_sin(p, half):
        inv = ROPE_BASE ** (-jnp.arange(half, dtype=F32) / half)
        ang = p.astype(F32)[:, None] * inv[None, :]
        return lax.optimization_barrier((jnp.cos(ang), jnp.sin(ang)))

    def with_ctx(t, fill):
        return jnp.concatenate([jnp.full((n_ctx, t.shape[1]), fill, F32), t], axis=0)

    cos_p, sin_p = cos_sin(pos, RET_DK // 2)
    cr = with_ctx(jnp.tile(cos_p, (1, 4)), 1.0)
    sr = with_ctx(jnp.tile(jnp.concatenate([-sin_p, sin_p], axis=1), (1, 2)), 0.0)

    quarter = MLA_ROPE // 4
    cos_r, sin_r = cos_sin(rows, quarter)
    cos_c, sin_c = cos_sin(cols, quarter)
    cos32 = with_ctx(jnp.concatenate([cos_r] * 2 + [cos_c] * 2, axis=1), 1.0)
    sin32 = with_ctx(jnp.concatenate([-sin_r, sin_r, -sin_c, sin_c], axis=1), 0.0)
    n_all = n_ctx + n_lat
    tail = jnp.zeros((n_all, HEAD_PAD - MLA_NOPE - MLA_ROPE), F32)
    cm = jnp.concatenate([jnp.ones((n_all, MLA_NOPE), F32), cos32, tail], axis=1)
    sm = jnp.concatenate([jnp.zeros((n_all, MLA_NOPE), F32), sin32, tail], axis=1)
    return cr, sr, cm, sm


def _retention_consts(ret_decay):
    log_gf = jax.nn.log_sigmoid(ret_decay[0].astype(F32))
    log_gb = jax.nn.log_sigmoid(ret_decay[1].astype(F32))
    idx = jnp.arange(RET_CHUNK, dtype=F32)
    rel = idx[:, None] - idx[None, :]
    dm_f = jnp.where(rel >= 0, jnp.exp(log_gf[:, None, None] * jnp.maximum(rel, 0.0)[None]), 0.0)
    dm_b = jnp.where(rel < 0, jnp.exp(log_gb[:, None, None] * jnp.maximum(-rel, 0.0)[None]), 0.0)

    def lanes(t):
        return jnp.repeat(t, RET_DK, axis=1)

    xi_f = lanes(jnp.exp(log_gf[None, :] * (idx + 1.0)[:, None]))
    zt_f = lanes(jnp.exp(log_gf[None, :] * (RET_CHUNK - 1 - idx)[:, None]))
    xi_b = lanes(jnp.exp(log_gb[None, :] * (RET_CHUNK - idx)[:, None]))
    zt_b = lanes(jnp.exp(log_gb[None, :] * idx[:, None]))
    cd = lambda lg: jnp.broadcast_to(jnp.exp(lg * RET_CHUNK)[:, None, None], (RET_HEADS, 1, RET_DV))
    return dm_f, dm_b, xi_f, zt_f, xi_b, zt_b, cd(log_gf), cd(log_gb)


def kernel(x, c, ctx, c_ctx, w_ada, b_ada, norm1, norm2, w_in, ret_decay, ret_gn, w_ret_o, conv_w,
           w_conv_o, mla_q_norm, w_uq, mla_kv_norm, w_ukv, w_mla_o, w_out, w_router, router_bias,
           w1, w3, w2, final_norm):
    n_batch, n_lat, d = x.shape
    n_ctx = ctx.shape[1]
    depth = w_ada.shape[0]
    t_all = n_ctx + n_lat
    assert n_ctx % ROW_BLOCK == 0 and n_lat % ROW_BLOCK == 0 and n_lat % GRID_W == 0
    nj = t_all // ROW_BLOCK
    ncb = n_ctx // ROW_BLOCK
    n_tok = n_batch * t_all
    assert n_tok % MOE_TILE == 0
    geom = dict(n_batch=n_batch, nj=nj, ncb=ncb)

    cc = jnp.concatenate([c, c_ctx[None, :]], axis=0)
    cc = jnp.pad(cc, ((0, -cc.shape[0] % 8), (0, 0)))
    mod = _ada_call(cc, w_ada, b_ada)[:, :n_batch + 1].reshape(depth, n_batch + 1, N_MOD, d)
    pick = jnp.stack([jnp.full((n_batch,), n_batch, jnp.int32),
                      jnp.arange(n_batch, dtype=jnp.int32)], axis=1).reshape(-1)
    mod = mod[:, pick]

    tabs = _rotary_tables(n_ctx, n_lat)
    wr_hi, wr_lo = _split_bf16(jnp.pad(w_router.astype(F32), ((0, 0), (0, LANES - N_EXPERTS))))
    rbias = router_bias.astype(F32).reshape(N_EXPERTS, 1)
    xa = jnp.concatenate([ctx, x], axis=1).reshape(n_tok, d)

    out = None
    x_parts = (xa,)
    for l in range(depth):
        m = mod[l]
        rowvec = lambda v: v.reshape(2 * n_batch, 1, d)
        a1 = rowvec(norm1[l][None, :] * (1.0 + m[:, 1]))
        b1 = rowvec(m[:, 0])
        g1 = rowvec(m[:, 2])
        a2 = rowvec(norm2[l][None, :] * (1.0 + m[:, 4]))
        b2 = rowvec(m[:, 3])
        g2 = rowvec(m[:, 5])

        wts = _layer_weights(w_in[l], w_uq[l], w_ukv[l], mla_q_norm[l], mla_kv_norm[l])
        proj = _inproj_call(x_parts, a1, b1, tabs, wts, **geom)
        rq, rk, rv, rg, cb, cu, qm, km, vm, gt = proj[:10]
        if len(x_parts) == 3:
            xa = proj[10]

        yf, yb = _ret_call(rq, rk, rv, _retention_consts(ret_decay[l]), **geom)
        om = _attn_call(qm, km, vm, n_ctx=n_ctx, **geom)

        merge_wts = (ret_gn[l].reshape(1, -1).astype(F32), conv_w[l].T.astype(F32),
                     w_ret_o[l].astype(BF16), w_conv_o[l].astype(BF16), w_mla_o[l].astype(BF16),
                     w_out[l].astype(BF16), wr_hi, wr_lo, rbias)
        xmid, h2, route = _merge_call(yf, yb, rg, cb, cu, om, gt, xa, g1, a2, b2, merge_wts,
                                      nj=nj, ncb=ncb)
        route = route.reshape(-1, ROUTE_ROWS, ROW_BLOCK)
        src, dest, gw, tile_ea, tile_eb, n_used = _dispatch(route, n_tok)
        hs = jnp.take(h2, src, axis=0)
        f_sorted = _moe_call(tile_ea, tile_eb, n_used, hs, gw, w1[l], w3[l], w2[l])
        fg = jnp.take(f_sorted, dest, axis=0)

        if l < depth - 1:
            x_parts = (xmid, fg, g2)
        else:
            out = _final_call(xmid, fg, g2, final_norm, **geom)
    return out.reshape(n_batch, n_lat, d)
```

```python
import functools
import math

import jax
import jax.numpy as jnp
from jax import lax
from jax.experimental import pallas as pl
from jax.experimental.pallas import tpu as pltpu

F32 = jnp.float32
BF16 = jnp.bfloat16

GRID_W = 64
RMS_EPS = 1e-6
ROPE_BASE = 10000.0
N_MOD = 6
RET_HEADS = 4
RET_DK = 64
RET_DV = 128
RET_CHUNK = 256
CONV_WIDTH = 512
MLA_HEADS = 8
MLA_Q_RANK = 384
MLA_KV_RANK = 256
MLA_NOPE = 64
MLA_ROPE = 32
MLA_V = 64
N_BRANCH = 3
N_EXPERTS = 16
N_GROUPS = 4
EXPERTS_PER_GROUP = N_EXPERTS // N_GROUPS
D_EXPERT = 512
IN_SIZES = (RET_HEADS * RET_DK, RET_HEADS * RET_DK, RET_HEADS * RET_DV, RET_HEADS * RET_DV,
            CONV_WIDTH, CONV_WIDTH, CONV_WIDTH, MLA_Q_RANK, MLA_KV_RANK, MLA_ROPE, 0)

LANES = 128
BF16_SUBLANES = 16
VMEM_LIMIT = 56 * 1024 * 1024

HEAD_PAD = LANES
ROW_BLOCK = 256
MOE_TILE = 256
ATTN_KEY_TILES = (768, 512, 256, 128)
ROUTE_ROWS = 8
PAIRS = [(a, b) for a in range(EXPERTS_PER_GROUP) for b in range(a + 1, EXPERTS_PER_GROUP)]
N_CLASSES = N_GROUPS * len(PAIRS)

RQ_W = RET_HEADS * RET_DK
RV_W = RET_HEADS * RET_DV
MLA_W = MLA_HEADS * HEAD_PAD
MLA_O = MLA_HEADS * MLA_V


def _const_spec(shape):
    nd = len(shape)
    return pl.BlockSpec(shape, lambda *_: (0,) * nd, pipeline_mode=pl.Buffered(1))


def _params(n_axes):
    return pltpu.CompilerParams(dimension_semantics=("arbitrary",) * n_axes,
                                vmem_limit_bytes=VMEM_LIMIT)


def _dot(a, b):
    return jnp.dot(a, b, preferred_element_type=F32)


def _split_bf16(a):
    hi = a.astype(BF16)
    lo = (a - hi.astype(F32)).astype(BF16)
    return hi, lo


def _silu(v):
    return v * jax.nn.sigmoid(v)


def _rms(v):
    return v * lax.rsqrt(jnp.mean(v * v, axis=-1, keepdims=True) + RMS_EPS)


def _ada_body(c_ref, w_ref, b_ref, o_ref):
    a_hi, a_lo = _split_bf16(_silu(c_ref[...]))
    w_hi, w_lo = _split_bf16(w_ref[...])
    o_ref[...] = _dot(a_hi, w_hi) + _dot(a_hi, w_lo) + _dot(a_lo, w_hi) + b_ref[...]


def _ada_call(cc, w_ada, b_ada):
    depth, d, nm = w_ada.shape
    rows = cc.shape[0]
    cb = nm // 4
    return pl.pallas_call(
        _ada_body,
        out_shape=jax.ShapeDtypeStruct((depth, rows, nm), F32),
        grid=(depth, nm // cb),
        in_specs=[pl.BlockSpec((rows, d), lambda l, n: (0, 0)),
                  pl.BlockSpec((None, d, cb), lambda l, n: (l, 0, n)),
                  pl.BlockSpec((None, 1, cb), lambda l, n: (l, 0, n))],
        out_specs=pl.BlockSpec((None, rows, cb), lambda l, n: (l, 0, n)),
        compiler_params=_params(2),
        name="ada_mod",
    )(cc, w_ada, b_ada.reshape(depth, 1, nm))


_O_RQ = 0
_O_RK = _O_RQ + RQ_W
_O_RV = _O_RK + RQ_W
_O_RG = _O_RV + RV_W
_O_CB = _O_RG + RV_W
_O_CC = _O_CB + CONV_WIDTH
_O_CX = _O_CC + CONV_WIDTH
_O_QD = _O_CX + CONV_WIDTH
_O_KVD = _O_QD + MLA_Q_RANK
_O_KR = _O_KVD + MLA_KV_RANK
_O_GT = _O_KR + LANES


def _rot_half(v, half):
    width = v.shape[1]
    lane = lax.broadcasted_iota(jnp.int32, v.shape, 1)
    first = (lane % (2 * half)) < half
    return jnp.where(first, pltpu.roll(v, width - half, 1), pltpu.roll(v, half, 1))


def _inproj_body(*refs, d_model, fused, ncb):
    xo_ref = refs[-1]
    if fused:
        xm_ref, f_ref, g2_ref = refs[:3]
        refs = refs[3:]
        x = xm_ref[...] + g2_ref[...] * f_ref[...].astype(F32)
    else:
        ctx_ref, lat_ref = refs[:2]
        refs = refs[2:]
        x = jnp.where(pl.program_id(0) < ncb, ctx_ref[...], lat_ref[...])
    xo_ref[...] = x
    (a1_ref, b1_ref, cr_ref, sr_ref, cm_ref, sm_ref, w_ref, wuq_ref, wk_ref, wv_ref, qn_ref, kvn_ref,
     ones_ref, rq_ref, rk_ref, rv_ref, rg_ref, cb_ref, cu_ref, qm_ref, km_ref, vm_ref, gt_ref) = refs[:23]
    h = (_rms(x) * a1_ref[...] + b1_ref[...]).astype(BF16)

    def mm(off, width):
        return _dot(h, w_ref[:, off:off + width])

    cr = jnp.concatenate([cr_ref[...]] * (RQ_W // cr_ref.shape[1]), axis=1)
    sr = jnp.concatenate([sr_ref[...]] * (RQ_W // sr_ref.shape[1]), axis=1)
    q = mm(_O_RQ, RQ_W)
    rq_ref[...] = (q * cr + _rot_half(q, RET_DK // 2) * sr).astype(BF16)
    k = mm(_O_RK, RQ_W)
    rk_ref[...] = ((k * cr + _rot_half(k, RET_DK // 2) * sr) * (RET_DK ** -0.5)).astype(BF16)
    rv_ref[...] = mm(_O_RV, RV_W).astype(BF16)
    rg_ref[...] = _silu(mm(_O_RG, RV_W)).astype(BF16)
    cb_ref[...] = mm(_O_CB, CONV_WIDTH).astype(BF16)
    cu_ref[...] = (mm(_O_CC, CONV_WIDTH) * mm(_O_CX, CONV_WIDTH)).astype(BF16)

    cm = cm_ref[...]
    sm = sm_ref[...]
    qn = (_rms(mm(_O_QD, MLA_Q_RANK)) * qn_ref[...]).astype(BF16)
    for hd in range(MLA_HEADS):
        lo = hd * HEAD_PAD
        qa = _dot(qn, wuq_ref[:, lo:lo + HEAD_PAD])
        qm_ref[:, lo:lo + HEAD_PAD] = (qa * cm + _rot_half(qa, MLA_ROPE // 4) * sm).astype(BF16)

    kvn = (_rms(mm(_O_KVD, MLA_KV_RANK)) * kvn_ref[...]).astype(BF16)
    kr = mm(_O_KR, HEAD_PAD)
    kr = kr * cm + _rot_half(kr, MLA_ROPE // 4) * sm
    kn = _dot(kvn, wk_ref[...])
    for hd in range(MLA_HEADS):
        lo = hd * HEAD_PAD
        km_ref[:, lo:lo + HEAD_PAD] = (kn[:, lo:lo + HEAD_PAD] + kr).astype(BF16)
    vm_ref[...] = (_dot(kvn, wv_ref[...]) + ones_ref[...]).astype(BF16)

    for br in range(N_BRANCH):
        gt_ref[:, br * d_model:(br + 1) * d_model] = jax.nn.sigmoid(
            mm(_O_GT + br * d_model, d_model)).astype(BF16)


def _inproj_call(x_parts, a1, b1, tabs, wts, *, n_batch, nj, ncb):
    fused = len(x_parts) == 3
    d = x_parts[0].shape[1]
    rows = n_batch * nj * ROW_BLOCK
    tm = ROW_BLOCK
    row = lambda j, b: (b * nj + j, 0)
    mod = lambda j, b: (2 * b + (j >= ncb).astype(jnp.int32), 0, 0)
    tab = lambda j, b: (j, 0)
    if fused:
        in_specs = [pl.BlockSpec((tm, d), row), pl.BlockSpec((tm, d), row),
                    pl.BlockSpec((None, 1, d), mod)]
    else:
        in_specs = [pl.BlockSpec((tm, d), lambda j, b: (jnp.where(j < ncb, b * ncb + j, 0), 0)),
                    pl.BlockSpec((tm, d),
                                 lambda j, b: (jnp.where(j < ncb, 0, b * (nj - ncb) + j - ncb), 0))]
    in_specs += [pl.BlockSpec((None, 1, d), mod), pl.BlockSpec((None, 1, d), mod)]
    in_specs += [pl.BlockSpec((tm, t.shape[1]), tab) for t in tabs]
    in_specs += [_const_spec(w.shape) for w in wts]
    widths = (RQ_W, RQ_W, RV_W, RV_W, CONV_WIDTH, CONV_WIDTH, MLA_W, MLA_W, MLA_W, N_BRANCH * d)
    out_shape = [jax.ShapeDtypeStruct((rows, w), BF16) for w in widths]
    out_specs = [pl.BlockSpec((tm, w), row) for w in widths]
    out_shape.append(jax.ShapeDtypeStruct((rows, d), F32))
    out_specs.append(pl.BlockSpec((tm, d), row))
    return pl.pallas_call(
        functools.partial(_inproj_body, d_model=d, fused=fused, ncb=ncb),
        out_shape=out_shape,
        grid=(nj, n_batch),
        in_specs=in_specs,
        out_specs=out_specs,
        compiler_params=_params(2),
        name="in_proj",
    )(*x_parts, a1, b1, *tabs, *wts)


def _ret_direction(q_ref, k_ref, v_ref, y_ref, s_ref, dm_ref, xi_ref, zt_ref, cd_ref, chunk_order):
    states = [s_ref[hd] for hd in range(RET_HEADS)]
    for c in chunk_order:
        rows = slice(c * RET_CHUNK, (c + 1) * RET_CHUNK)
        q = q_ref[rows, :]
        k = k_ref[rows, :]
        v = v_ref[rows, :]
        qx = (q.astype(F32) * xi_ref[...]).astype(BF16)
        kz = (k.astype(F32) * zt_ref[...]).astype(BF16)
        for hd in range(RET_HEADS):
            ks = slice(hd * RET_DK, (hd + 1) * RET_DK)
            vs = slice(hd * RET_DV, (hd + 1) * RET_DV)
            vh = v[:, vs]
            sc = lax.dot_general(q[:, ks], k[:, ks], (((1,), (1,)), ((), ())),
                                 preferred_element_type=F32)
            inner = _dot((sc * dm_ref[hd]).astype(BF16), vh)
            cross = _dot(qx[:, ks], states[hd].astype(BF16))
            y_ref[rows, vs] = inner + cross
            upd = lax.dot_general(kz[:, ks], vh, (((0,), (0,)), ((), ())),
                                  preferred_element_type=F32)
            states[hd] = cd_ref[hd] * states[hd] + upd
    for hd in range(RET_HEADS):
        s_ref[hd] = states[hd]


def _ret_body(qf_ref, kf_ref, vf_ref, qb_ref, kb_ref, vb_ref,
              dmf_ref, dmb_ref, xif_ref, ztf_ref, xib_ref, ztb_ref, cdf_ref, cdb_ref,
              yf_ref, yb_ref, sf_ref, sb_ref):
    @pl.when(pl.program_id(1) == 0)
    def _():
        sf_ref[...] = jnp.zeros_like(sf_ref)
        sb_ref[...] = jnp.zeros_like(sb_ref)

    n_chunks = ROW_BLOCK // RET_CHUNK
    _ret_direction(qf_ref, kf_ref, vf_ref, yf_ref, sf_ref, dmf_ref, xif_ref, ztf_ref, cdf_ref,
                   range(n_chunks))
    _ret_direction(qb_ref, kb_ref, vb_ref, yb_ref, sb_ref, dmb_ref, xib_ref, ztb_ref, cdb_ref,
                   range(n_chunks - 1, -1, -1))


def _ret_call(rq, rk, rv, consts, *, n_batch, nj, ncb):
    rows = rq.shape[0]
    tm = ROW_BLOCK
    fwd = lambda b, s: (b * nj + s, 0)

    def bwd(b, s):
        return (b * nj + jnp.where(s < ncb, ncb - 1 - s, nj - 1 - (s - ncb)), 0)

    specs = []
    for im in (fwd, bwd):
        specs += [pl.BlockSpec((tm, RQ_W), im), pl.BlockSpec((tm, RQ_W), im),
                  pl.BlockSpec((tm, RV_W), im)]
    specs += [_const_spec(c.shape) for c in consts]
    return pl.pallas_call(
        _ret_body,
        out_shape=[jax.ShapeDtypeStruct((rows, RV_W), F32)] * 2,
        grid=(n_batch, nj),
        in_specs=specs,
        out_specs=[pl.BlockSpec((tm, RV_W), fwd), pl.BlockSpec((tm, RV_W), bwd)],
        scratch_shapes=[pltpu.VMEM((RET_HEADS, RET_DK, RET_DV), F32)] * 2,
        compiler_params=_params(2),
        name="retention",
    )(rq, rk, rv, rq, rk, rv, *consts)


def _pick_tile(n, candidates):
    for c in candidates:
        if n % c == 0:
            return c
    raise ValueError(f"no tile for {n}")


def _attn_body(q_ref, k_ref, v_ref, o_ref, s_ref, *, n_ctx, n_all, ncb):
    tq = q_ref.shape[0]
    heads = [slice(hh * HEAD_PAD, (hh + 1) * HEAD_PAD) for hh in range(2)]

    def attend(n_keys, tk):
        nt = n_keys // tk
        qs = [q_ref[:, hs] for hs in heads]

        def qk(t, mrun):
            r0 = pl.multiple_of(t * tk, tk)
            out = []
            for hh, hs in enumerate(heads):
                s = lax.dot_general(qs[hh], k_ref[pl.ds(r0, tk), hs], (((1,), (1,)), ((), ())),
                                    preferred_element_type=F32)
                s_ref[hh, t, :, 0:tk] = s
                m = mrun[hh]
                for cc in range(tk // LANES):
                    m = jnp.maximum(m, s[:, cc * LANES:(cc + 1) * LANES])
                out.append(m)
            return tuple(out)

        mrun = lax.fori_loop(0, nt, qk, tuple(jnp.full((tq, LANES), -jnp.inf, F32) for _ in heads),
                             unroll=True)
        mrow = [jnp.max(m, axis=-1, keepdims=True) for m in mrun]

        def pv(t, accs):
            r0 = pl.multiple_of(t * tk, tk)
            out = []
            for hh, hs in enumerate(heads):
                p = jnp.exp2(s_ref[hh, t, :, 0:tk] - mrow[hh]).astype(BF16)
                out.append(accs[hh] + _dot(p, v_ref[pl.ds(r0, tk), hs]))
            return tuple(out)

        accs = lax.fori_loop(0, nt, pv, tuple(jnp.zeros((tq, HEAD_PAD), F32) for _ in heads),
                             unroll=True)
        o_ref[...] = jnp.concatenate([a[:, :MLA_V] / a[:, MLA_V:MLA_V + 1] for a in accs],
                                     axis=-1).astype(BF16)

    j = pl.program_id(2)

    @pl.when(j < ncb)
    def _():
        attend(n_ctx, _pick_tile(n_ctx, (256, 128)))

    @pl.when(j >= ncb)
    def _():
        attend(n_all, _pick_tile(n_all, ATTN_KEY_TILES))


def _attn_call(qm, km, vm, *, n_batch, nj, ncb, n_ctx):
    rows = qm.shape[0]
    tm = ROW_BLOCK
    t_all = nj * tm
    tk = _pick_tile(t_all, ATTN_KEY_TILES)
    qmap = lambda b, hp, j: (b * nj + j, hp)
    kmap = lambda b, hp, j: (b, hp)
    return pl.pallas_call(
        functools.partial(_attn_body, n_ctx=n_ctx, n_all=t_all, ncb=ncb),
        out_shape=jax.ShapeDtypeStruct((rows, MLA_O), BF16),
        grid=(n_batch, MLA_HEADS // 2, nj),
        in_specs=[pl.BlockSpec((tm, 2 * HEAD_PAD), qmap),
                  pl.BlockSpec((t_all, 2 * HEAD_PAD), kmap),
                  pl.BlockSpec((t_all, 2 * HEAD_PAD), kmap)],
        out_specs=pl.BlockSpec((tm, 2 * MLA_V), qmap),
        scratch_shapes=[pltpu.VMEM((2, t_all // tk, tm, tk), F32)],
        compiler_params=_params(3),
        name="mla_attention",
    )(qm, km, vm)


def _merge_body(yf_ref, yb_ref, rg_ref, cb_ref, cu_ref, cup_ref, cun_ref, om_ref, gt_ref, x_ref,
                g1_ref, a2_ref, b2_ref, gn_ref, cw_ref, wro_ref, wco_ref, wmo_ref, wout_ref,
                wrh_ref, wrl_ref, rb_ref, xmid_ref, h2_ref, rt_ref, *, nj, ncb, d_model):
    tm = x_ref.shape[0]
    j = pl.program_id(0) % nj
    seg_first = jnp.logical_or(j == 0, j == ncb)
    seg_last = jnp.logical_or(j == ncb - 1, j == nj - 1)

    y = yf_ref[...] + yb_ref[...]
    yn = jnp.concatenate([_rms(y[:, hd * RET_DV:(hd + 1) * RET_DV]) for hd in range(RET_HEADS)],
                         axis=-1) * gn_ref[...]
    y_ret = _dot((rg_ref[...].astype(F32) * yn).astype(BF16), wro_ref[...])

    u = cu_ref[...].astype(F32)
    ridx = lax.broadcasted_iota(jnp.int32, u.shape, 0)
    prev_row = cup_ref[...].astype(F32)[BF16_SUBLANES - 1:, :] * jnp.where(seg_first, 0.0, 1.0)
    next_row = cun_ref[...].astype(F32)[0:1, :] * jnp.where(seg_last, 0.0, 1.0)
    u_prev = jnp.where(ridx == 0, prev_row, pltpu.roll(u, 1, 0))
    u_next = jnp.where(ridx == tm - 1, next_row, pltpu.roll(u, tm - 1, 0))
    conv = u_prev * cw_ref[0:1, :] + u * cw_ref[1:2, :] + u_next * cw_ref[2:3, :]
    y_conv = _dot((cb_ref[...].astype(F32) * conv).astype(BF16), wco_ref[...])

    y_mla = _dot(om_ref[...], wmo_ref[...])

    merged = (gt_ref[:, 0:d_model].astype(F32) * y_ret
              + gt_ref[:, d_model:2 * d_model].astype(F32) * y_conv
              + gt_ref[:, 2 * d_model:3 * d_model].astype(F32) * y_mla)
    x_mid = x_ref[...] + g1_ref[...] * _dot(merged.astype(BF16), wout_ref[...])
    xmid_ref[...] = x_mid

    h2 = _rms(x_mid) * a2_ref[...] + b2_ref[...]
    h_hi, h_lo = _split_bf16(h2)
    logits = _dot(h_hi, wrh_ref[...]) + _dot(h_hi, wrl_ref[...]) + _dot(h_lo, wrh_ref[...])
    cls, w_lo, w_hi = _route_rows(jnp.transpose(logits)[:N_EXPERTS, :], rb_ref[...])
    rank, counts = _block_ranks(cls.astype(F32))
    rt_ref[...] = jnp.concatenate(
        [cls.astype(F32), w_lo, w_hi, rank, counts, jnp.zeros((ROUTE_ROWS - 5, tm), F32)], axis=0)
    h2_ref[...] = h_hi


def _top2_of4(v):
    def first_max(rows):
        best, idx = rows[0], jnp.zeros(rows[0].shape, jnp.int32)
        for e in range(1, len(rows)):
            better = rows[e] > best
            idx = jnp.where(better, e, idx)
            best = jnp.where(better, rows[e], best)
        return best, idx

    b1, i1 = first_max(v)
    b2, i2 = first_max([jnp.where(i1 == e, -jnp.inf, v[e]) for e in range(len(v))])
    return i1, i2, b1, b2


def _route_rows(logits_t, bias):
    scores = jax.nn.sigmoid(logits_t)
    biased = scores + bias
    row = lambda a, e: a[e:e + 1, :]
    best = None
    for g in range(N_GROUPS):
        v = [row(biased, g * EXPERTS_PER_GROUP + e) for e in range(EXPERTS_PER_GROUP)]
        i1, i2, b1, b2 = _top2_of4(v)
        cand = (b1 + b2, jnp.full(i1.shape, g, jnp.int32), i1, i2)
        if best is None:
            best = cand
        else:
            better = cand[0] > best[0]
            best = tuple(jnp.where(better, c, o) for c, o in zip(cand, best))
    _, g_sel, i1, i2 = best
    lo = jnp.minimum(i1, i2)
    hi = jnp.maximum(i1, i2)
    e_lo = g_sel * EXPERTS_PER_GROUP + lo
    e_hi = g_sel * EXPERTS_PER_GROUP + hi
    s_lo = jnp.zeros_like(best[0])
    s_hi = jnp.zeros_like(best[0])
    for e in range(N_EXPERTS):
        s_lo = jnp.where(e_lo == e, row(scores, e), s_lo)
        s_hi = jnp.where(e_hi == e, row(scores, e), s_hi)
    total = s_lo + s_hi
    pair_base = jnp.where(lo == 0, 0, jnp.where(lo == 1, 3, 5))
    cls = g_sel * len(PAIRS) + pair_base + (hi - lo - 1)
    return cls, s_lo / total, s_hi / total


def _block_ranks(cls_row):
    n = cls_row.shape[1]
    cls_col = jnp.transpose(jnp.broadcast_to(cls_row, (LANES, n)))[:, 0:1]
    ii = lax.broadcasted_iota(jnp.int32, (n, n), 0)
    jj = lax.broadcasted_iota(jnp.int32, (n, n), 1)
    earlier_same = jnp.logical_and(cls_col == cls_row, ii < jj)
    rank = jnp.sum(jnp.where(earlier_same, 1.0, 0.0), axis=0, keepdims=True)
    lane = lax.broadcasted_iota(jnp.int32, (1, n), 1)
    counts = jnp.zeros((1, n), F32)
    for c in range(N_CLASSES):
        cnt = jnp.sum(jnp.where(cls_row == c, 1.0, 0.0), axis=1, keepdims=True)
        counts = jnp.where(lane == c, cnt, counts)
    return rank, counts


def _merge_call(yf, yb, rg, cb, cu, om, gt, xa, g1, a2, b2, wts, *, nj, ncb):
    rows, d = xa.shape
    tm = ROW_BLOCK
    nb = rows // tm
    halo = BF16_SUBLANES
    per_blk = tm // halo
    row = lambda i: (i, 0)
    mod = lambda i: (2 * (i // nj) + ((i % nj) >= ncb).astype(jnp.int32), 0, 0)
    prev = lambda i: (jnp.maximum(i * per_blk - 1, 0), 0)
    nxt = lambda i: (jnp.minimum((i + 1) * per_blk, rows // halo - 1), 0)
    in_specs = [pl.BlockSpec((tm, RV_W), row), pl.BlockSpec((tm, RV_W), row),
                pl.BlockSpec((tm, RV_W), row),
                pl.BlockSpec((tm, CONV_WIDTH), row), pl.BlockSpec((tm, CONV_WIDTH), row),
                pl.BlockSpec((halo, CONV_WIDTH), prev), pl.BlockSpec((halo, CONV_WIDTH), nxt),
                pl.BlockSpec((tm, MLA_O), row), pl.BlockSpec((tm, N_BRANCH * d), row),
                pl.BlockSpec((tm, d), row),
                pl.BlockSpec((None, 1, d), mod), pl.BlockSpec((None, 1, d), mod),
                pl.BlockSpec((None, 1, d), mod)]
    in_specs += [_const_spec(w.shape) for w in wts]
    return pl.pallas_call(
        functools.partial(_merge_body, nj=nj, ncb=ncb, d_model=d),
        out_shape=[jax.ShapeDtypeStruct((rows, d), F32), jax.ShapeDtypeStruct((rows, d), BF16),
                   jax.ShapeDtypeStruct((nb * ROUTE_ROWS, tm), F32)],
        grid=(nb,),
        in_specs=in_specs,
        out_specs=[pl.BlockSpec((tm, d), row), pl.BlockSpec((tm, d), row),
                   pl.BlockSpec((ROUTE_ROWS, tm), row)],
        compiler_params=_params(1),
        name="merge_out_proj",
    )(yf, yb, rg, cb, cu, cu, cu, om, gt, xa, g1, a2, b2, *wts)


def _moe_body(ea_ref, eb_ref, nu_ref, h_ref, gw_ref, w1a_ref, w3a_ref, w2a_ref, w1b_ref, w3b_ref,
              w2b_ref, o_ref, w13a_s, w2a_s, w13b_s, w2b_s):
    t = pl.program_id(0)
    used = t < nu_ref[0]
    prev = jnp.maximum(t - 1, 0)

    def refresh(e_ref, w1_ref, w3_ref, w2_ref, w13_s, w2_s):
        @pl.when(jnp.logical_and(used, jnp.logical_or(t == 0, e_ref[t] != e_ref[prev])))
        def _():
            w13_s[:, :D_EXPERT] = w1_ref[...].astype(BF16)
            w13_s[:, D_EXPERT:] = w3_ref[...].astype(BF16)
            w2_s[...] = w2_ref[...].astype(BF16)

    refresh(ea_ref, w1a_ref, w3a_ref, w2a_ref, w13a_s, w2a_s)
    refresh(eb_ref, w1b_ref, w3b_ref, w2b_ref, w13b_s, w2b_s)

    @pl.when(used)
    def _():
        h = h_ref[...]
        gw = gw_ref[...]

        def expert(w13_s, w2_s, wt):
            a = _dot(h, w13_s[...])
            act = _silu(a[:, :D_EXPERT]) * a[:, D_EXPERT:] * wt
            return _dot(act.astype(BF16), w2_s[...])

        o_ref[...] = (expert(w13a_s, w2a_s, gw[:, 0:1])
                      + expert(w13b_s, w2b_s, gw[:, 1:2])).astype(BF16)

    @pl.when(jnp.logical_not(used))
    def _():
        o_ref[...] = jnp.zeros_like(o_ref)


def _moe_call(tile_ea, tile_eb, n_used, hs, gw, w1, w3, w2, layer):
    npad, d = hs.shape
    tmo = MOE_TILE
    row = lambda t, ea, eb, nu: (t, 0)
    wa = lambda t, ea, eb, nu: (layer, ea[t], 0, 0)
    wb = lambda t, ea, eb, nu: (layer, eb[t], 0, 0)
    up = lambda im: pl.BlockSpec((None, None, d, D_EXPERT), im)
    down = lambda im: pl.BlockSpec((None, None, D_EXPERT, d), im)
    grid_spec = pltpu.PrefetchScalarGridSpec(
        num_scalar_prefetch=3,
        grid=(npad // tmo,),
        in_specs=[pl.BlockSpec((tmo, d), row), pl.BlockSpec((tmo, 2), row),
                  up(wa), up(wa), down(wa), up(wb), up(wb), down(wb)],
        out_specs=pl.BlockSpec((tmo, d), row),
        scratch_shapes=[pltpu.VMEM((d, 2 * D_EXPERT), BF16), pltpu.VMEM((D_EXPERT, d), BF16)] * 2)
    return pl.pallas_call(
        _moe_body,
        out_shape=jax.ShapeDtypeStruct((npad, d), BF16),
        grid_spec=grid_spec,
        compiler_params=_params(1),
        name="moe_experts",
    )(tile_ea, tile_eb, n_used, hs, gw, w1, w3, w2, w1, w3, w2)


def _dispatch(route, n_tok):
    tmo = MOE_TILE
    n_tiles = n_tok // tmo + N_CLASSES
    npad = n_tiles * tmo
    cls = route[:, 0, :].astype(jnp.int32)
    rank = route[:, 3, :].astype(jnp.int32)
    counts = route[:, 4, :N_CLASSES].astype(jnp.int32)
    tiles_per = (jnp.sum(counts, axis=0) + tmo - 1) // tmo
    tile_end = jnp.cumsum(tiles_per)
    offs = (tile_end - tiles_per) * tmo
    block_base = offs[None, :] + jnp.cumsum(counts, axis=0) - counts
    classes = jnp.arange(N_CLASSES, dtype=jnp.int32)
    base = jnp.sum(jnp.where(cls[:, :, None] == classes, block_base[:, None, :], 0), axis=-1)
    dest = (base + rank).reshape(-1)
    record = jnp.stack([jnp.arange(n_tok, dtype=F32), route[:, 1, :].reshape(-1),
                        route[:, 2, :].reshape(-1), jnp.zeros((n_tok,), F32)], axis=1)
    slots = jnp.zeros((npad, record.shape[1]), F32).at[dest].set(record)
    src = slots[:, 0].astype(jnp.int32)
    gw = slots[:, 1:3]
    tile_ids = jnp.arange(n_tiles, dtype=jnp.int32)
    tile_cls = jnp.sum((tile_end[None, :] <= tile_ids[:, None]).astype(jnp.int32), axis=1)
    tile_cls = jnp.minimum(tile_cls, N_CLASSES - 1)
    pa = jnp.asarray([p[0] for p in PAIRS], jnp.int32)
    pb = jnp.asarray([p[1] for p in PAIRS], jnp.int32)
    group = (tile_cls // len(PAIRS)) * EXPERTS_PER_GROUP
    tile_ea = group + pa[tile_cls % len(PAIRS)]
    tile_eb = group + pb[tile_cls % len(PAIRS)]
    n_used = tile_end[-1:].astype(jnp.int32)
    return src, dest, gw, tile_ea, tile_eb, n_used


def _final_body(x_ref, f_ref, g2_ref, fn_ref, o_ref):
    o_ref[...] = _rms(x_ref[...] + g2_ref[...] * f_ref[...].astype(F32)) * fn_ref[...]


def _final_call(xmid, fg, g2, final_norm, *, n_batch, nj, ncb):
    rows, d = xmid.shape
    tm = ROW_BLOCK
    njl = nj - ncb
    src = lambda b, j: (b * nj + ncb + j, 0)
    return pl.pallas_call(
        _final_body,
        out_shape=jax.ShapeDtypeStruct((n_batch * njl * tm, d), F32),
        grid=(n_batch, njl),
        in_specs=[pl.BlockSpec((tm, d), src), pl.BlockSpec((tm, d), src),
                  pl.BlockSpec((None, 1, d), lambda b, j: (2 * b + 1, 0, 0)),
                  _const_spec((1, d))],
        out_specs=pl.BlockSpec((tm, d), lambda b, j: (b * njl + j, 0)),
        compiler_params=_params(2),
        name="final_norm",
    )(xmid, fg, g2, final_norm.reshape(1, d))


def _split_cols(w, sizes):
    out, off = [], 0
    for s in sizes:
        out.append(w[:, off:off + s])
        off += s
    out.append(w[:, off:])
    return out


def _layer_weights(w_in, w_uq, w_ukv, q_norm, kv_norm):
    wq, wk, wv, wg, wcb, wcc, wcx, wqd, wkvd, wkr, wgate = _split_cols(w_in, IN_SIZES[:-1])
    rope_lanes = lambda w: jnp.pad(w, ((0, 0), (MLA_NOPE, HEAD_PAD - MLA_NOPE - MLA_ROPE)))
    w_ext = jnp.concatenate(
        [wq, wk, wv, wg, wcb, wcc, wcx, wqd, wkvd, rope_lanes(wkr), wgate], axis=1).astype(BF16)

    uq = w_uq.reshape(MLA_Q_RANK, MLA_HEADS, MLA_NOPE + MLA_ROPE)
    tail = HEAD_PAD - MLA_NOPE - MLA_ROPE
    uq_pad = jnp.pad(uq, ((0, 0), (0, 0), (0, tail)))
    wuq = uq_pad.reshape(MLA_Q_RANK, MLA_W).astype(BF16)

    ukv = w_ukv.reshape(MLA_KV_RANK, MLA_HEADS, MLA_NOPE + MLA_V)
    wk_up = jnp.pad(ukv[..., :MLA_NOPE], ((0, 0), (0, 0), (0, HEAD_PAD - MLA_NOPE)))
    wk_up = wk_up.reshape(MLA_KV_RANK, MLA_W).astype(BF16)
    wv_pad = jnp.pad(ukv[..., MLA_NOPE:], ((0, 0), (0, 0), (0, HEAD_PAD - MLA_V)))
    wv_pad = wv_pad.reshape(MLA_KV_RANK, MLA_W).astype(BF16)
    ones_row = jnp.zeros((MLA_HEADS, HEAD_PAD), F32).at[:, MLA_V].set(1.0).reshape(1, MLA_W)
    q_gain = q_norm.astype(F32) * ((MLA_NOPE + MLA_ROPE) ** -0.5 * math.log2(math.e))
    return (w_ext, wuq, wk_up, wv_pad, q_gain.reshape(1, -1),
            kv_norm.reshape(1, -1).astype(F32), ones_row)


def _rotary_tables(n_ctx, n_lat):
    n_all = n_ctx + n_lat
    row = lax.broadcasted_iota(jnp.int32, (n_all, LANES), 0)
    lane = lax.broadcasted_iota(jnp.int32, (n_all, LANES), 1)
    is_ctx = row < n_ctx
    pos = row - n_ctx
    grid_row = pos // GRID_W
    grid_col = pos - grid_row * GRID_W

    def inv_freq(idx, half):
        return ROPE_BASE ** (-idx.astype(F32) / half)

    def table(ang, first_half, live):
        cos = jnp.where(is_ctx, 1.0, jnp.cos(ang))
        sin = jnp.where(is_ctx, 0.0, jnp.where(first_half, -jnp.sin(ang), jnp.sin(ang)))
        return jnp.where(live, cos, 0.0), jnp.where(live, sin, 0.0)

    half = RET_DK // 2
    ang = pos.astype(F32) * inv_freq(lane % half, half)
    cr, sr = table(ang, (lane % RET_DK) < half, True)

    quarter = MLA_ROPE // 4
    rl = lane - MLA_NOPE
    coord = jnp.where(rl < MLA_ROPE // 2, grid_row, grid_col)
    ang = coord.astype(F32) * inv_freq(rl % quarter, quarter)
    cm, sm = table(ang, (rl % (2 * quarter)) < quarter,
                   jnp.logical_and(rl >= 0, rl < MLA_ROPE))
    cm = jnp.where(rl < 0, 1.0, cm)
    return cr, sr, cm, sm


def _retention_consts(ret_decay):
    log_gf = jax.nn.log_sigmoid(ret_decay[0].astype(F32))
    log_gb = jax.nn.log_sigmoid(ret_decay[1].astype(F32))
    idx = jnp.arange(RET_CHUNK, dtype=F32)
    rel = idx[:, None] - idx[None, :]
    dm_f = jnp.where(rel >= 0, jnp.exp(log_gf[:, None, None] * jnp.maximum(rel, 0.0)[None]), 0.0)
    dm_b = jnp.where(rel < 0, jnp.exp(log_gb[:, None, None] * jnp.maximum(-rel, 0.0)[None]), 0.0)

    def lanes(t):
        return jnp.repeat(t, RET_DK, axis=1)

    xi_f = lanes(jnp.exp(log_gf[None, :] * (idx + 1.0)[:, None]))
    zt_f = lanes(jnp.exp(log_gf[None, :] * (RET_CHUNK - 1 - idx)[:, None]))
    xi_b = lanes(jnp.exp(log_gb[None, :] * (RET_CHUNK - idx)[:, None]))
    zt_b = lanes(jnp.exp(log_gb[None, :] * idx[:, None]))
    cd = lambda lg: jnp.broadcast_to(jnp.exp(lg * RET_CHUNK)[:, None, None], (RET_HEADS, 1, RET_DV))
    return dm_f, dm_b, xi_f, zt_f, xi_b, zt_b, cd(log_gf), cd(log_gb)


def kernel(x, c, ctx, c_ctx, w_ada, b_ada, norm1, norm2, w_in, ret_decay, ret_gn, w_ret_o, conv_w,
           w_conv_o, mla_q_norm, w_uq, mla_kv_norm, w_ukv, w_mla_o, w_out, w_router, router_bias,
           w1, w3, w2, final_norm):
    n_batch, n_lat, d = x.shape
    n_ctx = ctx.shape[1]
    depth = w_ada.shape[0]
    t_all = n_ctx + n_lat
    assert n_ctx % ROW_BLOCK == 0 and n_lat % ROW_BLOCK == 0 and n_lat % GRID_W == 0
    nj = t_all // ROW_BLOCK
    ncb = n_ctx // ROW_BLOCK
    n_tok = n_batch * t_all
    assert n_tok % MOE_TILE == 0
    geom = dict(n_batch=n_batch, nj=nj, ncb=ncb)

    cc = jnp.concatenate([c, c_ctx[None, :]], axis=0)
    cc = jnp.pad(cc, ((0, -cc.shape[0] % 8), (0, 0)))
    mod = _ada_call(cc, w_ada, b_ada)[:, :n_batch + 1].reshape(depth, n_batch + 1, N_MOD, d)
    pick = jnp.stack([jnp.full((n_batch,), n_batch, jnp.int32),
                      jnp.arange(n_batch, dtype=jnp.int32)], axis=1).reshape(-1)
    mod = mod[:, pick]

    tabs = _rotary_tables(n_ctx, n_lat)
    wr_hi, wr_lo = _split_bf16(jnp.pad(w_router.astype(F32), ((0, 0), (0, LANES - N_EXPERTS))))
    rbias = router_bias.astype(F32).reshape(N_EXPERTS, 1)

    out = None
    x_parts = (ctx.reshape(n_batch * n_ctx, d), x.reshape(n_batch * n_lat, d))
    for l in range(depth):
        m = mod[l]
        rowvec = lambda v: v.reshape(2 * n_batch, 1, d)
        a1 = rowvec(norm1[l][None, :] * (1.0 + m[:, 1]))
        b1 = rowvec(m[:, 0])
        g1 = rowvec(m[:, 2])
        a2 = rowvec(norm2[l][None, :] * (1.0 + m[:, 4]))
        b2 = rowvec(m[:, 3])
        g2 = rowvec(m[:, 5])

        wts = _layer_weights(w_in[l], w_uq[l], w_ukv[l], mla_q_norm[l], mla_kv_norm[l])
        proj = _inproj_call(x_parts, a1, b1, tabs, wts, **geom)
        rq, rk, rv, rg, cb, cu, qm, km, vm, gt, xa = proj

        yf, yb = _ret_call(rq, rk, rv, _retention_consts(ret_decay[l]), **geom)
        om = _attn_call(qm, km, vm, n_ctx=n_ctx, **geom)

        merge_wts = (ret_gn[l].reshape(1, -1).astype(F32), conv_w[l].T.astype(F32),
                     w_ret_o[l].astype(BF16), w_conv_o[l].astype(BF16), w_mla_o[l].astype(BF16),
                     w_out[l].astype(BF16), wr_hi, wr_lo, rbias)
        xmid, h2, route = _merge_call(yf, yb, rg, cb, cu, om, gt, xa, g1, a2, b2, merge_wts,
                                      nj=nj, ncb=ncb)
        route = route.reshape(-1, ROUTE_ROWS, ROW_BLOCK)
        src, dest, gw, tile_ea, tile_eb, n_used = _dispatch(route, n_tok)
        hs = jnp.take(h2, src, axis=0)
        f_sorted = _moe_call(tile_ea, tile_eb, n_used, hs, gw, w1, w3, w2, l)
        fg = jnp.take(f_sorted, dest, axis=0)

        if l < depth - 1:
            x_parts = (xmid, fg, g2)
        else:
            out = _final_call(xmid, fg, g2, final_norm, **geom)
    return out.reshape(n_batch, n_lat, d)
```

```python
import functools
import math

import jax
import jax.numpy as jnp
from jax import lax
from jax.experimental import pallas as pl
from jax.experimental.pallas import tpu as pltpu

F32 = jnp.float32
BF16 = jnp.bfloat16

GRID_W = 64
RMS_EPS = 1e-6
ROPE_BASE = 10000.0
N_MOD = 6
RET_HEADS = 4
RET_DK = 64
RET_DV = 128
RET_CHUNK = 256
CONV_WIDTH = 512
MLA_HEADS = 8
MLA_Q_RANK = 384
MLA_KV_RANK = 256
MLA_NOPE = 64
MLA_ROPE = 32
MLA_V = 64
N_BRANCH = 3
N_EXPERTS = 16
N_GROUPS = 4
EXPERTS_PER_GROUP = N_EXPERTS // N_GROUPS
D_EXPERT = 512
IN_SIZES = (RET_HEADS * RET_DK, RET_HEADS * RET_DK, RET_HEADS * RET_DV, RET_HEADS * RET_DV,
            CONV_WIDTH, CONV_WIDTH, CONV_WIDTH, MLA_Q_RANK, MLA_KV_RANK, MLA_ROPE, 0)

LANES = 128
BF16_SUBLANES = 16
VMEM_LIMIT = 56 * 1024 * 1024

HEAD_PAD = LANES
ROW_BLOCK = 256
MOE_TILE = 256
ATTN_KEY_TILES = (768, 512, 256, 128)
ROUTE_ROWS = 8
PAIRS = [(a, b) for a in range(EXPERTS_PER_GROUP) for b in range(a + 1, EXPERTS_PER_GROUP)]
N_CLASSES = N_GROUPS * len(PAIRS)

RQ_W = RET_HEADS * RET_DK
RV_W = RET_HEADS * RET_DV
MLA_W = MLA_HEADS * HEAD_PAD
MLA_O = MLA_HEADS * MLA_V


def _const_spec(shape):
    nd = len(shape)
    return pl.BlockSpec(shape, lambda *_: (0,) * nd, pipeline_mode=pl.Buffered(1))


def _params(n_axes):
    return pltpu.CompilerParams(dimension_semantics=("arbitrary",) * n_axes,
                                vmem_limit_bytes=VMEM_LIMIT)


def _dot(a, b):
    return jnp.dot(a, b, preferred_element_type=F32)


def _split_bf16(a):
    hi = a.astype(BF16)
    lo = (a - hi.astype(F32)).astype(BF16)
    return hi, lo


def _silu(v):
    return v * jax.nn.sigmoid(v)


def _rms(v):
    return v * lax.rsqrt(jnp.mean(v * v, axis=-1, keepdims=True) + RMS_EPS)


def _ada_body(c_ref, w_ref, b_ref, o_ref):
    a_hi, a_lo = _split_bf16(_silu(c_ref[...]))
    w_hi, w_lo = _split_bf16(w_ref[...])
    o_ref[...] = _dot(a_hi, w_hi) + _dot(a_hi, w_lo) + _dot(a_lo, w_hi) + b_ref[...]


def _ada_call(cc, w_ada, b_ada):
    depth, d, nm = w_ada.shape
    rows = cc.shape[0]
    cb = nm // 4
    return pl.pallas_call(
        _ada_body,
        out_shape=jax.ShapeDtypeStruct((depth, rows, nm), F32),
        grid=(depth, nm // cb),
        in_specs=[pl.BlockSpec((rows, d), lambda l, n: (0, 0)),
                  pl.BlockSpec((None, d, cb), lambda l, n: (l, 0, n)),
                  pl.BlockSpec((None, 1, cb), lambda l, n: (l, 0, n))],
        out_specs=pl.BlockSpec((None, rows, cb), lambda l, n: (l, 0, n)),
        compiler_params=_params(2),
        name="ada_mod",
    )(cc, w_ada, b_ada.reshape(depth, 1, nm))


_O_RQ = 0
_O_RK = _O_RQ + RQ_W
_O_RV = _O_RK + RQ_W
_O_RG = _O_RV + RV_W
_O_CB = _O_RG + RV_W
_O_CC = _O_CB + CONV_WIDTH
_O_CX = _O_CC + CONV_WIDTH
_O_QD = _O_CX + CONV_WIDTH
_O_KVD = _O_QD + MLA_Q_RANK
_O_KR = _O_KVD + MLA_KV_RANK
_O_GT = _O_KR + LANES


def _rot_half(v, half):
    width = v.shape[1]
    lane = lax.broadcasted_iota(jnp.int32, v.shape, 1)
    first = (lane % (2 * half)) < half
    return jnp.where(first, pltpu.roll(v, width - half, 1), pltpu.roll(v, half, 1))


def _inproj_body(*refs, d_model, fused, ncb):
    xo_ref = refs[-1]
    if fused:
        xm_ref, f_ref, g2_ref = refs[:3]
        refs = refs[3:]
        x = xm_ref[...] + g2_ref[...] * f_ref[...].astype(F32)
    else:
        ctx_ref, lat_ref = refs[:2]
        refs = refs[2:]
        x = jnp.where(pl.program_id(0) < ncb, ctx_ref[...], lat_ref[...])
    xo_ref[...] = x
    (a1_ref, b1_ref, cr_ref, sr_ref, cm_ref, sm_ref, w_ref, wuq_ref, wk_ref, wv_ref, qn_ref, kvn_ref,
     ones_ref, rq_ref, rk_ref, rv_ref, rg_ref, cb_ref, cu_ref, qm_ref, km_ref, vm_ref, gt_ref) = refs[:23]
    h = (_rms(x) * a1_ref[...] + b1_ref[...]).astype(BF16)

    def mm(off, width):
        return _dot(h, w_ref[:, off:off + width])

    cr = jnp.concatenate([cr_ref[...]] * (RQ_W // cr_ref.shape[1]), axis=1)
    sr = jnp.concatenate([sr_ref[...]] * (RQ_W // sr_ref.shape[1]), axis=1)
    q = mm(_O_RQ, RQ_W)
    rq_ref[...] = (q * cr + _rot_half(q, RET_DK // 2) * sr).astype(BF16)
    k = mm(_O_RK, RQ_W)
    rk_ref[...] = ((k * cr + _rot_half(k, RET_DK // 2) * sr) * (RET_DK ** -0.5)).astype(BF16)
    rv_ref[...] = mm(_O_RV, RV_W).astype(BF16)
    rg_ref[...] = _silu(mm(_O_RG, RV_W)).astype(BF16)
    cb_ref[...] = mm(_O_CB, CONV_WIDTH).astype(BF16)
    cu_ref[...] = (mm(_O_CC, CONV_WIDTH) * mm(_O_CX, CONV_WIDTH)).astype(BF16)

    cm = cm_ref[...]
    sm = sm_ref[...]
    qn = (_rms(mm(_O_QD, MLA_Q_RANK)) * qn_ref[...]).astype(BF16)
    for hd in range(MLA_HEADS):
        lo = hd * HEAD_PAD
        qa = _dot(qn, wuq_ref[:, lo:lo + HEAD_PAD])
        qm_ref[:, lo:lo + HEAD_PAD] = (qa * cm + _rot_half(qa, MLA_ROPE // 4) * sm).astype(BF16)

    kvn = (_rms(mm(_O_KVD, MLA_KV_RANK)) * kvn_ref[...]).astype(BF16)
    kr = mm(_O_KR, HEAD_PAD)
    kr = kr * cm + _rot_half(kr, MLA_ROPE // 4) * sm
    kn = _dot(kvn, wk_ref[...])
    for hd in range(MLA_HEADS):
        lo = hd * HEAD_PAD
        km_ref[:, lo:lo + HEAD_PAD] = (kn[:, lo:lo + HEAD_PAD] + kr).astype(BF16)
    vm_ref[...] = (_dot(kvn, wv_ref[...]) + ones_ref[...]).astype(BF16)

    for br in range(N_BRANCH):
        gt_ref[:, br * d_model:(br + 1) * d_model] = jax.nn.sigmoid(
            mm(_O_GT + br * d_model, d_model)).astype(BF16)


def _inproj_call(x_parts, a1, b1, tabs, wts, *, n_batch, nj, ncb):
    fused = len(x_parts) == 3
    d = x_parts[0].shape[1]
    rows = n_batch * nj * ROW_BLOCK
    tm = ROW_BLOCK
    row = lambda j, b: (b * nj + j, 0)
    mod = lambda j, b: (2 * b + (j >= ncb).astype(jnp.int32), 0, 0)
    tab = lambda j, b: (j, 0)
    if fused:
        in_specs = [pl.BlockSpec((tm, d), row), pl.BlockSpec((tm, d), row),
                    pl.BlockSpec((None, 1, d), mod)]
    else:
        in_specs = [pl.BlockSpec((tm, d), lambda j, b: (jnp.where(j < ncb, b * ncb + j, 0), 0)),
                    pl.BlockSpec((tm, d),
                                 lambda j, b: (jnp.where(j < ncb, 0, b * (nj - ncb) + j - ncb), 0))]
    in_specs += [pl.BlockSpec((None, 1, d), mod), pl.BlockSpec((None, 1, d), mod)]
    in_specs += [pl.BlockSpec((tm, t.shape[1]), tab) for t in tabs]
    in_specs += [_const_spec(w.shape) for w in wts]
    widths = (RQ_W, RQ_W, RV_W, RV_W, CONV_WIDTH, CONV_WIDTH, MLA_W, MLA_W, MLA_W, N_BRANCH * d)
    out_shape = [jax.ShapeDtypeStruct((rows, w), BF16) for w in widths]
    out_specs = [pl.BlockSpec((tm, w), row) for w in widths]
    out_shape.append(jax.ShapeDtypeStruct((rows, d), F32))
    out_specs.append(pl.BlockSpec((tm, d), row))
    return pl.pallas_call(
        functools.partial(_inproj_body, d_model=d, fused=fused, ncb=ncb),
        out_shape=out_shape,
        grid=(nj, n_batch),
        in_specs=in_specs,
        out_specs=out_specs,
        compiler_params=_params(2),
        name="in_proj",
    )(*x_parts, a1, b1, *tabs, *wts)


def _ret_direction(q_ref, k_ref, v_ref, y_ref, s_ref, dm_ref, xi_ref, zt_ref, cd_ref, chunk_order):
    states = [s_ref[hd] for hd in range(RET_HEADS)]
    for c in chunk_order:
        rows = slice(c * RET_CHUNK, (c + 1) * RET_CHUNK)
        q = q_ref[rows, :]
        k = k_ref[rows, :]
        v = v_ref[rows, :]
        qx = (q.astype(F32) * xi_ref[...]).astype(BF16)
        kz = (k.astype(F32) * zt_ref[...]).astype(BF16)
        for hd in range(RET_HEADS):
            ks = slice(hd * RET_DK, (hd + 1) * RET_DK)
            vs = slice(hd * RET_DV, (hd + 1) * RET_DV)
            vh = v[:, vs]
            sc = lax.dot_general(q[:, ks], k[:, ks], (((1,), (1,)), ((), ())),
                                 preferred_element_type=F32)
            inner = _dot((sc * dm_ref[hd]).astype(BF16), vh)
            cross = _dot(qx[:, ks], states[hd].astype(BF16))
            y_ref[rows, vs] = inner + cross
            upd = lax.dot_general(kz[:, ks], vh, (((0,), (0,)), ((), ())),
                                  preferred_element_type=F32)
            states[hd] = cd_ref[hd] * states[hd] + upd
    for hd in range(RET_HEADS):
        s_ref[hd] = states[hd]


def _ret_body(qf_ref, kf_ref, vf_ref, qb_ref, kb_ref, vb_ref,
              dmf_ref, dmb_ref, xif_ref, ztf_ref, xib_ref, ztb_ref, cdf_ref, cdb_ref,
              yf_ref, yb_ref, sf_ref, sb_ref):
    @pl.when(pl.program_id(1) == 0)
    def _():
        sf_ref[...] = jnp.zeros_like(sf_ref)
        sb_ref[...] = jnp.zeros_like(sb_ref)

    n_chunks = ROW_BLOCK // RET_CHUNK
    _ret_direction(qf_ref, kf_ref, vf_ref, yf_ref, sf_ref, dmf_ref, xif_ref, ztf_ref, cdf_ref,
                   range(n_chunks))
    _ret_direction(qb_ref, kb_ref, vb_ref, yb_ref, sb_ref, dmb_ref, xib_ref, ztb_ref, cdb_ref,
                   range(n_chunks - 1, -1, -1))


def _ret_call(rq, rk, rv, consts, *, n_batch, nj, ncb):
    rows = rq.shape[0]
    tm = ROW_BLOCK
    fwd = lambda b, s: (b * nj + s, 0)

    def bwd(b, s):
        return (b * nj + jnp.where(s < ncb, ncb - 1 - s, nj - 1 - (s - ncb)), 0)

    specs = []
    for im in (fwd, bwd):
        specs += [pl.BlockSpec((tm, RQ_W), im), pl.BlockSpec((tm, RQ_W), im),
                  pl.BlockSpec((tm, RV_W), im)]
    specs += [_const_spec(c.shape) for c in consts]
    return pl.pallas_call(
        _ret_body,
        out_shape=[jax.ShapeDtypeStruct((rows, RV_W), F32)] * 2,
        grid=(n_batch, nj),
        in_specs=specs,
        out_specs=[pl.BlockSpec((tm, RV_W), fwd), pl.BlockSpec((tm, RV_W), bwd)],
        scratch_shapes=[pltpu.VMEM((RET_HEADS, RET_DK, RET_DV), F32)] * 2,
        compiler_params=_params(2),
        name="retention",
    )(rq, rk, rv, rq, rk, rv, *consts)


def _pick_tile(n, candidates):
    for c in candidates:
        if n % c == 0:
            return c
    raise ValueError(f"no tile for {n}")


def _attn_body(q_ref, k_ref, v_ref, o_ref, s_ref, *, n_ctx, n_all, ncb):
    tq = q_ref.shape[0]
    heads = [slice(hh * HEAD_PAD, (hh + 1) * HEAD_PAD) for hh in range(2)]

    def attend(n_keys, tk):
        nt = n_keys // tk
        qs = [q_ref[:, hs] for hs in heads]

        def qk(t, mrun):
            r0 = pl.multiple_of(t * tk, tk)
            out = []
            for hh, hs in enumerate(heads):
                s = lax.dot_general(qs[hh], k_ref[pl.ds(r0, tk), hs], (((1,), (1,)), ((), ())),
                                    preferred_element_type=F32)
                s_ref[hh, t, :, 0:tk] = s
                m = mrun[hh]
                for cc in range(tk // LANES):
                    m = jnp.maximum(m, s[:, cc * LANES:(cc + 1) * LANES])
                out.append(m)
            return tuple(out)

        mrun = lax.fori_loop(0, nt, qk, tuple(jnp.full((tq, LANES), -jnp.inf, F32) for _ in heads),
                             unroll=True)
        mrow = [jnp.max(m, axis=-1, keepdims=True) for m in mrun]

        def pv(t, accs):
            r0 = pl.multiple_of(t * tk, tk)
            out = []
            for hh, hs in enumerate(heads):
                p = jnp.exp2(s_ref[hh, t, :, 0:tk] - mrow[hh]).astype(BF16)
                out.append(accs[hh] + _dot(p, v_ref[pl.ds(r0, tk), hs]))
            return tuple(out)

        accs = lax.fori_loop(0, nt, pv, tuple(jnp.zeros((tq, HEAD_PAD), F32) for _ in heads),
                             unroll=True)
        o_ref[...] = jnp.concatenate([a[:, :MLA_V] / a[:, MLA_V:MLA_V + 1] for a in accs],
                                     axis=-1).astype(BF16)

    j = pl.program_id(2)

    @pl.when(j < ncb)
    def _():
        attend(n_ctx, _pick_tile(n_ctx, (256, 128)))

    @pl.when(j >= ncb)
    def _():
        attend(n_all, _pick_tile(n_all, ATTN_KEY_TILES))


def _attn_call(qm, km, vm, *, n_batch, nj, ncb, n_ctx):
    rows = qm.shape[0]
    tm = ROW_BLOCK
    t_all = nj * tm
    tk = _pick_tile(t_all, ATTN_KEY_TILES)
    qmap = lambda b, hp, j: (b * nj + j, hp)
    kmap = lambda b, hp, j: (b, hp)
    return pl.pallas_call(
        functools.partial(_attn_body, n_ctx=n_ctx, n_all=t_all, ncb=ncb),
        out_shape=jax.ShapeDtypeStruct((rows, MLA_O), BF16),
        grid=(n_batch, MLA_HEADS // 2, nj),
        in_specs=[pl.BlockSpec((tm, 2 * HEAD_PAD), qmap),
                  pl.BlockSpec((t_all, 2 * HEAD_PAD), kmap),
                  pl.BlockSpec((t_all, 2 * HEAD_PAD), kmap)],
        out_specs=pl.BlockSpec((tm, 2 * MLA_V), qmap),
        scratch_shapes=[pltpu.VMEM((2, t_all // tk, tm, tk), F32)],
        compiler_params=_params(3),
        name="mla_attention",
    )(qm, km, vm)


def _merge_body(yf_ref, yb_ref, rg_ref, cb_ref, cu_ref, cup_ref, cun_ref, om_ref, gt_ref, x_ref,
                g1_ref, a2_ref, b2_ref, gn_ref, cw_ref, wro_ref, wco_ref, wmo_ref, wout_ref,
                wrh_ref, wrl_ref, rb_ref, xmid_ref, h2_ref, rt_ref, *, nj, ncb, d_model):
    tm = x_ref.shape[0]
    j = pl.program_id(0) % nj
    seg_first = jnp.logical_or(j == 0, j == ncb)
    seg_last = jnp.logical_or(j == ncb - 1, j == nj - 1)

    y = yf_ref[...] + yb_ref[...]
    yn = jnp.concatenate([_rms(y[:, hd * RET_DV:(hd + 1) * RET_DV]) for hd in range(RET_HEADS)],
                         axis=-1) * gn_ref[...]
    y_ret = _dot((rg_ref[...].astype(F32) * yn).astype(BF16), wro_ref[...])

    u = cu_ref[...].astype(F32)
    ridx = lax.broadcasted_iota(jnp.int32, u.shape, 0)
    prev_row = cup_ref[...].astype(F32)[BF16_SUBLANES - 1:, :] * jnp.where(seg_first, 0.0, 1.0)
    next_row = cun_ref[...].astype(F32)[0:1, :] * jnp.where(seg_last, 0.0, 1.0)
    u_prev = jnp.where(ridx == 0, prev_row, pltpu.roll(u, 1, 0))
    u_next = jnp.where(ridx == tm - 1, next_row, pltpu.roll(u, tm - 1, 0))
    conv = u_prev * cw_ref[0:1, :] + u * cw_ref[1:2, :] + u_next * cw_ref[2:3, :]
    y_conv = _dot((cb_ref[...].astype(F32) * conv).astype(BF16), wco_ref[...])

    y_mla = _dot(om_ref[...], wmo_ref[...])

    merged = (gt_ref[:, 0:d_model].astype(F32) * y_ret
              + gt_ref[:, d_model:2 * d_model].astype(F32) * y_conv
              + gt_ref[:, 2 * d_model:3 * d_model].astype(F32) * y_mla)
    x_mid = x_ref[...] + g1_ref[...] * _dot(merged.astype(BF16), wout_ref[...])
    xmid_ref[...] = x_mid

    h2 = _rms(x_mid) * a2_ref[...] + b2_ref[...]
    h_hi, h_lo = _split_bf16(h2)
    logits = _dot(h_hi, wrh_ref[...]) + _dot(h_hi, wrl_ref[...]) + _dot(h_lo, wrh_ref[...])
    cls, w_lo, w_hi = _route_rows(jnp.transpose(logits)[:N_EXPERTS, :], rb_ref[...])
    rank, counts = _block_ranks(cls.astype(F32))
    rt_ref[...] = jnp.concatenate(
        [cls.astype(F32), w_lo, w_hi, rank, counts, jnp.zeros((ROUTE_ROWS - 5, tm), F32)], axis=0)
    h2_ref[...] = h_hi


def _top2_of4(v):
    def first_max(rows):
        best, idx = rows[0], jnp.zeros(rows[0].shape, jnp.int32)
        for e in range(1, len(rows)):
            better = rows[e] > best
            idx = jnp.where(better, e, idx)
            best = jnp.where(better, rows[e], best)
        return best, idx

    b1, i1 = first_max(v)
    b2, i2 = first_max([jnp.where(i1 == e, -jnp.inf, v[e]) for e in range(len(v))])
    return i1, i2, b1, b2


def _route_rows(logits_t, bias):
    scores = jax.nn.sigmoid(logits_t)
    biased = scores + bias
    row = lambda a, e: a[e:e + 1, :]
    best = None
    for g in range(N_GROUPS):
        v = [row(biased, g * EXPERTS_PER_GROUP + e) for e in range(EXPERTS_PER_GROUP)]
        i1, i2, b1, b2 = _top2_of4(v)
        cand = (b1 + b2, jnp.full(i1.shape, g, jnp.int32), i1, i2)
        if best is None:
            best = cand
        else:
            better = cand[0] > best[0]
            best = tuple(jnp.where(better, c, o) for c, o in zip(cand, best))
    _, g_sel, i1, i2 = best
    lo = jnp.minimum(i1, i2)
    hi = jnp.maximum(i1, i2)
    e_lo = g_sel * EXPERTS_PER_GROUP + lo
    e_hi = g_sel * EXPERTS_PER_GROUP + hi
    s_lo = jnp.zeros_like(best[0])
    s_hi = jnp.zeros_like(best[0])
    for e in range(N_EXPERTS):
        s_lo = jnp.where(e_lo == e, row(scores, e), s_lo)
        s_hi = jnp.where(e_hi == e, row(scores, e), s_hi)
    total = s_lo + s_hi
    pair_base = jnp.where(lo == 0, 0, jnp.where(lo == 1, 3, 5))
    cls = g_sel * len(PAIRS) + pair_base + (hi - lo - 1)
    return cls, s_lo / total, s_hi / total


def _block_ranks(cls_row):
    n = cls_row.shape[1]
    cls_col = jnp.transpose(jnp.broadcast_to(cls_row, (LANES, n)))[:, 0:1]
    ii = lax.broadcasted_iota(jnp.int32, (n, n), 0)
    jj = lax.broadcasted_iota(jnp.int32, (n, n), 1)
    earlier_same = jnp.logical_and(cls_col == cls_row, ii < jj)
    rank = jnp.sum(jnp.where(earlier_same, 1.0, 0.0), axis=0, keepdims=True)
    lane = lax.broadcasted_iota(jnp.int32, (1, n), 1)
    counts = jnp.zeros((1, n), F32)
    for c in range(N_CLASSES):
        cnt = jnp.sum(jnp.where(cls_row == c, 1.0, 0.0), axis=1, keepdims=True)
        counts = jnp.where(lane == c, cnt, counts)
    return rank, counts


def _merge_call(yf, yb, rg, cb, cu, om, gt, xa, g1, a2, b2, wts, *, nj, ncb):
    rows, d = xa.shape
    tm = ROW_BLOCK
    nb = rows // tm
    halo = BF16_SUBLANES
    per_blk = tm // halo
    row = lambda i: (i, 0)
    mod = lambda i: (2 * (i // nj) + ((i % nj) >= ncb).astype(jnp.int32), 0, 0)
    prev = lambda i: (jnp.maximum(i * per_blk - 1, 0), 0)
    nxt = lambda i: (jnp.minimum((i + 1) * per_blk, rows // halo - 1), 0)
    in_specs = [pl.BlockSpec((tm, RV_W), row), pl.BlockSpec((tm, RV_W), row),
                pl.BlockSpec((tm, RV_W), row),
                pl.BlockSpec((tm, CONV_WIDTH), row), pl.BlockSpec((tm, CONV_WIDTH), row),
                pl.BlockSpec((halo, CONV_WIDTH), prev), pl.BlockSpec((halo, CONV_WIDTH), nxt),
                pl.BlockSpec((tm, MLA_O), row), pl.BlockSpec((tm, N_BRANCH * d), row),
                pl.BlockSpec((tm, d), row),
                pl.BlockSpec((None, 1, d), mod), pl.BlockSpec((None, 1, d), mod),
                pl.BlockSpec((None, 1, d), mod)]
    in_specs += [_const_spec(w.shape) for w in wts]
    return pl.pallas_call(
        functools.partial(_merge_body, nj=nj, ncb=ncb, d_model=d),
        out_shape=[jax.ShapeDtypeStruct((rows, d), F32), jax.ShapeDtypeStruct((rows, d), BF16),
                   jax.ShapeDtypeStruct((nb * ROUTE_ROWS, tm), F32)],
        grid=(nb,),
        in_specs=in_specs,
        out_specs=[pl.BlockSpec((tm, d), row), pl.BlockSpec((tm, d), row),
                   pl.BlockSpec((ROUTE_ROWS, tm), row)],
        compiler_params=_params(1),
        name="merge_out_proj",
    )(yf, yb, rg, cb, cu, cu, cu, om, gt, xa, g1, a2, b2, *wts)


def _moe_body(ea_ref, eb_ref, nu_ref, h_ref, gw_ref, w1a_ref, w3a_ref, w2a_ref, w1b_ref, w3b_ref,
              w2b_ref, o_ref, w13a_s, w2a_s, w13b_s, w2b_s):
    t = pl.program_id(0)
    used = t < nu_ref[0]
    prev = jnp.maximum(t - 1, 0)

    def refresh(e_ref, w1_ref, w3_ref, w2_ref, w13_s, w2_s):
        @pl.when(jnp.logical_and(used, jnp.logical_or(t == 0, e_ref[t] != e_ref[prev])))
        def _():
            w13_s[:, :D_EXPERT] = w1_ref[...].astype(BF16)
            w13_s[:, D_EXPERT:] = w3_ref[...].astype(BF16)
            w2_s[...] = w2_ref[...].astype(BF16)

    refresh(ea_ref, w1a_ref, w3a_ref, w2a_ref, w13a_s, w2a_s)
    refresh(eb_ref, w1b_ref, w3b_ref, w2b_ref, w13b_s, w2b_s)

    @pl.when(used)
    def _():
        h = h_ref[...]
        gw = gw_ref[...]

        def expert(w13_s, w2_s, wt):
            a = _dot(h, w13_s[...])
            act = _silu(a[:, :D_EXPERT]) * a[:, D_EXPERT:] * wt
            return _dot(act.astype(BF16), w2_s[...])

        o_ref[...] = (expert(w13a_s, w2a_s, gw[:, 0:1])
                      + expert(w13b_s, w2b_s, gw[:, 1:2])).astype(BF16)

    @pl.when(jnp.logical_not(used))
    def _():
        o_ref[...] = jnp.zeros_like(o_ref)


def _moe_call(tile_ea, tile_eb, n_used, hs, gw, w1, w3, w2, layer):
    npad, d = hs.shape
    tmo = MOE_TILE
    row = lambda t, ea, eb, nu: (t, 0)
    wa = lambda t, ea, eb, nu: (layer, ea[t], 0, 0)
    wb = lambda t, ea, eb, nu: (layer, eb[t], 0, 0)
    up = lambda im: pl.BlockSpec((None, None, d, D_EXPERT), im)
    down = lambda im: pl.BlockSpec((None, None, D_EXPERT, d), im)
    grid_spec = pltpu.PrefetchScalarGridSpec(
        num_scalar_prefetch=3,
        grid=(npad // tmo,),
        in_specs=[pl.BlockSpec((tmo, d), row), pl.BlockSpec((tmo, 2), row),
                  up(wa), up(wa), down(wa), up(wb), up(wb), down(wb)],
        out_specs=pl.BlockSpec((tmo, d), row),
        scratch_shapes=[pltpu.VMEM((d, 2 * D_EXPERT), BF16), pltpu.VMEM((D_EXPERT, d), BF16)] * 2)
    return pl.pallas_call(
        _moe_body,
        out_shape=jax.ShapeDtypeStruct((npad, d), BF16),
        grid_spec=grid_spec,
        compiler_params=_params(1),
        name="moe_experts",
    )(tile_ea, tile_eb, n_used, hs, gw, w1, w3, w2, w1, w3, w2)


def _dispatch(route, n_tok):
    tmo = MOE_TILE
    n_tiles = n_tok // tmo + N_CLASSES
    npad = n_tiles * tmo
    cls = route[:, 0, :].astype(jnp.int32)
    rank = route[:, 3, :].astype(jnp.int32)
    counts = route[:, 4, :N_CLASSES].astype(jnp.int32)
    tiles_per = (jnp.sum(counts, axis=0) + tmo - 1) // tmo
    tile_end = jnp.cumsum(tiles_per)
    offs = (tile_end - tiles_per) * tmo
    block_base = offs[None, :] + jnp.cumsum(counts, axis=0) - counts
    classes = jnp.arange(N_CLASSES, dtype=jnp.int32)
    base = jnp.sum(jnp.where(cls[:, :, None] == classes, block_base[:, None, :], 0), axis=-1)
    dest = (base + rank).reshape(-1)
    record = jnp.stack([jnp.arange(n_tok, dtype=F32), route[:, 1, :].reshape(-1),
                        route[:, 2, :].reshape(-1), jnp.zeros((n_tok,), F32)], axis=1)
    slots = jnp.zeros((npad, record.shape[1]), F32).at[dest].set(
        record, mode="promise_in_bounds", unique_indices=True)
    src = slots[:, 0].astype(jnp.int32)
    gw = slots[:, 1:3]
    tile_ids = jnp.arange(n_tiles, dtype=jnp.int32)
    tile_cls = jnp.sum((tile_end[None, :] <= tile_ids[:, None]).astype(jnp.int32), axis=1)
    tile_cls = jnp.minimum(tile_cls, N_CLASSES - 1)
    pa = jnp.asarray([p[0] for p in PAIRS], jnp.int32)
    pb = jnp.asarray([p[1] for p in PAIRS], jnp.int32)
    group = (tile_cls // len(PAIRS)) * EXPERTS_PER_GROUP
    tile_ea = group + pa[tile_cls % len(PAIRS)]
    tile_eb = group + pb[tile_cls % len(PAIRS)]
    n_used = tile_end[-1:].astype(jnp.int32)
    return src, dest, gw, tile_ea, tile_eb, n_used


def _final_body(x_ref, f_ref, g2_ref, fn_ref, o_ref):
    o_ref[...] = _rms(x_ref[...] + g2_ref[...] * f_ref[...].astype(F32)) * fn_ref[...]


def _final_call(xmid, fg, g2, final_norm, *, n_batch, nj, ncb):
    rows, d = xmid.shape
    tm = ROW_BLOCK
    njl = nj - ncb
    src = lambda b, j: (b * nj + ncb + j, 0)
    return pl.pallas_call(
        _final_body,
        out_shape=jax.ShapeDtypeStruct((n_batch * njl * tm, d), F32),
        grid=(n_batch, njl),
        in_specs=[pl.BlockSpec((tm, d), src), pl.BlockSpec((tm, d), src),
                  pl.BlockSpec((None, 1, d), lambda b, j: (2 * b + 1, 0, 0)),
                  _const_spec((1, d))],
        out_specs=pl.BlockSpec((tm, d), lambda b, j: (b * njl + j, 0)),
        compiler_params=_params(2),
        name="final_norm",
    )(xmid, fg, g2, final_norm.reshape(1, d))


def _split_cols(w, sizes):
    out, off = [], 0
    for s in sizes:
        out.append(w[:, off:off + s])
        off += s
    out.append(w[:, off:])
    return out


def _layer_weights(w_in, w_uq, w_ukv, q_norm, kv_norm):
    wq, wk, wv, wg, wcb, wcc, wcx, wqd, wkvd, wkr, wgate = _split_cols(w_in, IN_SIZES[:-1])
    rope_lanes = lambda w: jnp.pad(w, ((0, 0), (MLA_NOPE, HEAD_PAD - MLA_NOPE - MLA_ROPE)))
    w_ext = jnp.concatenate(
        [wq, wk, wv, wg, wcb, wcc, wcx, wqd, wkvd, rope_lanes(wkr), wgate], axis=1).astype(BF16)

    uq = w_uq.reshape(MLA_Q_RANK, MLA_HEADS, MLA_NOPE + MLA_ROPE)
    tail = HEAD_PAD - MLA_NOPE - MLA_ROPE
    uq_pad = jnp.pad(uq, ((0, 0), (0, 0), (0, tail)))
    wuq = uq_pad.reshape(MLA_Q_RANK, MLA_W).astype(BF16)

    ukv = w_ukv.reshape(MLA_KV_RANK, MLA_HEADS, MLA_NOPE + MLA_V)
    wk_up = jnp.pad(ukv[..., :MLA_NOPE], ((0, 0), (0, 0), (0, HEAD_PAD - MLA_NOPE)))
    wk_up = wk_up.reshape(MLA_KV_RANK, MLA_W).astype(BF16)
    wv_pad = jnp.pad(ukv[..., MLA_NOPE:], ((0, 0), (0, 0), (0, HEAD_PAD - MLA_V)))
    wv_pad = wv_pad.reshape(MLA_KV_RANK, MLA_W).astype(BF16)
    ones_row = jnp.zeros((MLA_HEADS, HEAD_PAD), F32).at[:, MLA_V].set(1.0).reshape(1, MLA_W)
    q_gain = q_norm.astype(F32) * ((MLA_NOPE + MLA_ROPE) ** -0.5 * math.log2(math.e))
    return (w_ext, wuq, wk_up, wv_pad, q_gain.reshape(1, -1),
            kv_norm.reshape(1, -1).astype(F32), ones_row)


def _rotary_tables(n_ctx, n_lat):
    n_all = n_ctx + n_lat
    row = lax.broadcasted_iota(jnp.int32, (n_all, LANES), 0)
    lane = lax.broadcasted_iota(jnp.int32, (n_all, LANES), 1)
    is_ctx = row < n_ctx
    pos = row - n_ctx
    grid_row = pos // GRID_W
    grid_col = pos - grid_row * GRID_W

    def inv_freq(idx, half):
        return ROPE_BASE ** (-idx.astype(F32) / half)

    def table(ang, first_half, live):
        cos = jnp.where(is_ctx, 1.0, jnp.cos(ang))
        sin = jnp.where(is_ctx, 0.0, jnp.where(first_half, -jnp.sin(ang), jnp.sin(ang)))
        return jnp.where(live, cos, 0.0), jnp.where(live, sin, 0.0)

    half = RET_DK // 2
    ang = pos.astype(F32) * inv_freq(lane % half, half)
    cr, sr = table(ang, (lane % RET_DK) < half, True)

    quarter = MLA_ROPE // 4
    rl = lane - MLA_NOPE
    coord = jnp.where(rl < MLA_ROPE // 2, grid_row, grid_col)
    ang = coord.astype(F32) * inv_freq(rl % quarter, quarter)
    cm, sm = table(ang, (rl % (2 * quarter)) < quarter,
                   jnp.logical_and(rl >= 0, rl < MLA_ROPE))
    cm = jnp.where(rl < 0, 1.0, cm)
    return cr, sr, cm, sm


def _retention_consts(ret_decay):
    log_gf = jax.nn.log_sigmoid(ret_decay[0].astype(F32))
    log_gb = jax.nn.log_sigmoid(ret_decay[1].astype(F32))
    idx = jnp.arange(RET_CHUNK, dtype=F32)
    rel = idx[:, None] - idx[None, :]
    dm_f = jnp.where(rel >= 0, jnp.exp(log_gf[:, None, None] * jnp.maximum(rel, 0.0)[None]), 0.0)
    dm_b = jnp.where(rel < 0, jnp.exp(log_gb[:, None, None] * jnp.maximum(-rel, 0.0)[None]), 0.0)

    def lanes(t):
        return jnp.repeat(t, RET_DK, axis=1)

    xi_f = lanes(jnp.exp(log_gf[None, :] * (idx + 1.0)[:, None]))
    zt_f = lanes(jnp.exp(log_gf[None, :] * (RET_CHUNK - 1 - idx)[:, None]))
    xi_b = lanes(jnp.exp(log_gb[None, :] * (RET_CHUNK - idx)[:, None]))
    zt_b = lanes(jnp.exp(log_gb[None, :] * idx[:, None]))
    cd = lambda lg: jnp.broadcast_to(jnp.exp(lg * RET_CHUNK)[:, None, None], (RET_HEADS, 1, RET_DV))
    return dm_f, dm_b, xi_f, zt_f, xi_b, zt_b, cd(log_gf), cd(log_gb)


def kernel(x, c, ctx, c_ctx, w_ada, b_ada, norm1, norm2, w_in, ret_decay, ret_gn, w_ret_o, conv_w,
           w_conv_o, mla_q_norm, w_uq, mla_kv_norm, w_ukv, w_mla_o, w_out, w_router, router_bias,
           w1, w3, w2, final_norm):
    n_batch, n_lat, d = x.shape
    n_ctx = ctx.shape[1]
    depth = w_ada.shape[0]
    t_all = n_ctx + n_lat
    assert n_ctx % ROW_BLOCK == 0 and n_lat % ROW_BLOCK == 0 and n_lat % GRID_W == 0
    nj = t_all // ROW_BLOCK
    ncb = n_ctx // ROW_BLOCK
    n_tok = n_batch * t_all
    assert n_tok % MOE_TILE == 0
    geom = dict(n_batch=n_batch, nj=nj, ncb=ncb)

    cc = jnp.concatenate([c, c_ctx[None, :]], axis=0)
    cc = jnp.pad(cc, ((0, -cc.shape[0] % 8), (0, 0)))
    mod = _ada_call(cc, w_ada, b_ada)[:, :n_batch + 1].reshape(depth, n_batch + 1, N_MOD, d)
    pick = jnp.stack([jnp.full((n_batch,), n_batch, jnp.int32),
                      jnp.arange(n_batch, dtype=jnp.int32)], axis=1).reshape(-1)
    mod = mod[:, pick]

    tabs = _rotary_tables(n_ctx, n_lat)
    wr_hi, wr_lo = _split_bf16(jnp.pad(w_router.astype(F32), ((0, 0), (0, LANES - N_EXPERTS))))
    rbias = router_bias.astype(F32).reshape(N_EXPERTS, 1)

    out = None
    x_parts = (ctx.reshape(n_batch * n_ctx, d), x.reshape(n_batch * n_lat, d))
    for l in range(depth):
        m = mod[l]
        rowvec = lambda v: v.reshape(2 * n_batch, 1, d)
        a1 = rowvec(norm1[l][None, :] * (1.0 + m[:, 1]))
        b1 = rowvec(m[:, 0])
        g1 = rowvec(m[:, 2])
        a2 = rowvec(norm2[l][None, :] * (1.0 + m[:, 4]))
        b2 = rowvec(m[:, 3])
        g2 = rowvec(m[:, 5])

        wts = _layer_weights(w_in[l], w_uq[l], w_ukv[l], mla_q_norm[l], mla_kv_norm[l])
        proj = _inproj_call(x_parts, a1, b1, tabs, wts, **geom)
        rq, rk, rv, rg, cb, cu, qm, km, vm, gt, xa = proj

        yf, yb = _ret_call(rq, rk, rv, _retention_consts(ret_decay[l]), **geom)
        om = _attn_call(qm, km, vm, n_ctx=n_ctx, **geom)

        merge_wts = (ret_gn[l].reshape(1, -1).astype(F32), conv_w[l].T.astype(F32),
                     w_ret_o[l].astype(BF16), w_conv_o[l].astype(BF16), w_mla_o[l].astype(BF16),
                     w_out[l].astype(BF16), wr_hi, wr_lo, rbias)
        xmid, h2, route = _merge_call(yf, yb, rg, cb, cu, om, gt, xa, g1, a2, b2, merge_wts,
                                      nj=nj, ncb=ncb)
        route = route.reshape(-1, ROUTE_ROWS, ROW_BLOCK)
        src, dest, gw, tile_ea, tile_eb, n_used = _dispatch(route, n_tok)
        hs = h2.at[src].get(mode="promise_in_bounds")
        f_sorted = _moe_call(tile_ea, tile_eb, n_used, hs, gw, w1, w3, w2, l)
        fg = f_sorted.at[dest].get(mode="promise_in_bounds")

        if l < depth - 1:
            x_parts = (xmid, fg, g2)
        else:
            out = _final_call(xmid, fg, g2, final_norm, **geom)
    return out.reshape(n_batch, n_lat, d)
```

```python
import functools
import math

import jax
import jax.numpy as jnp
from jax import lax
from jax.experimental import pallas as pl
from jax.experimental.pallas import tpu as pltpu

F32 = jnp.float32
BF16 = jnp.bfloat16

GRID_W = 64
RMS_EPS = 1e-6
ROPE_BASE = 10000.0
N_MOD = 6
RET_HEADS = 4
RET_DK = 64
RET_DV = 128
RET_CHUNK = 256
CONV_WIDTH = 512
MLA_HEADS = 8
MLA_Q_RANK = 384
MLA_KV_RANK = 256
MLA_NOPE = 64
MLA_ROPE = 32
MLA_V = 64
N_BRANCH = 3
N_EXPERTS = 16
N_GROUPS = 4
EXPERTS_PER_GROUP = N_EXPERTS // N_GROUPS
D_EXPERT = 512
IN_SIZES = (RET_HEADS * RET_DK, RET_HEADS * RET_DK, RET_HEADS * RET_DV, RET_HEADS * RET_DV,
            CONV_WIDTH, CONV_WIDTH, CONV_WIDTH, MLA_Q_RANK, MLA_KV_RANK, MLA_ROPE, 0)

LANES = 128
BF16_SUBLANES = 16
VMEM_LIMIT = 56 * 1024 * 1024

HEAD_PAD = LANES
ROW_BLOCK = 256
MOE_TILE = 256
ATTN_KEY_TILES = (768, 512, 256, 128)
ROUTE_ROWS = 8
PAIRS = [(a, b) for a in range(EXPERTS_PER_GROUP) for b in range(a + 1, EXPERTS_PER_GROUP)]
N_CLASSES = N_GROUPS * len(PAIRS)

RQ_W = RET_HEADS * RET_DK
RV_W = RET_HEADS * RET_DV
MLA_W = MLA_HEADS * HEAD_PAD
MLA_O = MLA_HEADS * MLA_V


def _const_spec(shape):
    nd = len(shape)
    return pl.BlockSpec(shape, lambda *_: (0,) * nd, pipeline_mode=pl.Buffered(1))


def _params(n_axes):
    return pltpu.CompilerParams(dimension_semantics=("arbitrary",) * n_axes,
                                vmem_limit_bytes=VMEM_LIMIT)


def _dot(a, b):
    return jnp.dot(a, b, preferred_element_type=F32)


def _split_bf16(a):
    hi = a.astype(BF16)
    lo = (a - hi.astype(F32)).astype(BF16)
    return hi, lo


def _silu(v):
    return v * jax.nn.sigmoid(v)


def _rms(v):
    return v * lax.rsqrt(jnp.mean(v * v, axis=-1, keepdims=True) + RMS_EPS)


def _ada_body(c_ref, w_ref, b_ref, o_ref):
    a_hi, a_lo = _split_bf16(_silu(c_ref[...]))
    w_hi, w_lo = _split_bf16(w_ref[...])
    o_ref[...] = _dot(a_hi, w_hi) + _dot(a_hi, w_lo) + _dot(a_lo, w_hi) + b_ref[...]


def _ada_call(cc, w_ada, b_ada):
    depth, d, nm = w_ada.shape
    rows = cc.shape[0]
    cb = nm // 4
    return pl.pallas_call(
        _ada_body,
        out_shape=jax.ShapeDtypeStruct((depth, rows, nm), F32),
        grid=(depth, nm // cb),
        in_specs=[pl.BlockSpec((rows, d), lambda l, n: (0, 0)),
                  pl.BlockSpec((None, d, cb), lambda l, n: (l, 0, n)),
                  pl.BlockSpec((None, 1, cb), lambda l, n: (l, 0, n))],
        out_specs=pl.BlockSpec((None, rows, cb), lambda l, n: (l, 0, n)),
        compiler_params=_params(2),
        name="ada_mod",
    )(cc, w_ada, b_ada.reshape(depth, 1, nm))


_O_RQ = 0
_O_RK = _O_RQ + RQ_W
_O_RV = _O_RK + RQ_W
_O_RG = _O_RV + RV_W
_O_CB = _O_RG + RV_W
_O_CC = _O_CB + CONV_WIDTH
_O_CX = _O_CC + CONV_WIDTH
_O_QD = _O_CX + CONV_WIDTH
_O_KVD = _O_QD + MLA_Q_RANK
_O_KR = _O_KVD + MLA_KV_RANK
_O_GT = _O_KR + LANES


def _rot_half(v, half):
    width = v.shape[1]
    lane = lax.broadcasted_iota(jnp.int32, v.shape, 1)
    first = (lane % (2 * half)) < half
    return jnp.where(first, pltpu.roll(v, width - half, 1), pltpu.roll(v, half, 1))


def _inproj_body(*refs, d_model, fused, ncb):
    xo_ref = refs[-1]
    if fused:
        xm_ref, f_ref, g2_ref = refs[:3]
        refs = refs[3:]
        x = xm_ref[...] + g2_ref[...] * f_ref[...].astype(F32)
    else:
        ctx_ref, lat_ref = refs[:2]
        refs = refs[2:]
        x = jnp.where(pl.program_id(0) < ncb, ctx_ref[...], lat_ref[...])
    xo_ref[...] = x
    (a1_ref, b1_ref, cr_ref, sr_ref, cm_ref, sm_ref, w_ref, wuq_ref, wk_ref, wv_ref, qn_ref, kvn_ref,
     ones_ref, rq_ref, rk_ref, rv_ref, rg_ref, cb_ref, cu_ref, qm_ref, km_ref, vm_ref, gt_ref) = refs[:23]
    h = (_rms(x) * a1_ref[...] + b1_ref[...]).astype(BF16)

    def mm(off, width):
        return _dot(h, w_ref[:, off:off + width])

    cr = jnp.concatenate([cr_ref[...]] * (RQ_W // cr_ref.shape[1]), axis=1)
    sr = jnp.concatenate([sr_ref[...]] * (RQ_W // sr_ref.shape[1]), axis=1)
    q = mm(_O_RQ, RQ_W)
    rq_ref[...] = (q * cr + _rot_half(q, RET_DK // 2) * sr).astype(BF16)
    k = mm(_O_RK, RQ_W)
    rk_ref[...] = ((k * cr + _rot_half(k, RET_DK // 2) * sr) * (RET_DK ** -0.5)).astype(BF16)
    rv_ref[...] = mm(_O_RV, RV_W).astype(BF16)
    rg_ref[...] = _silu(mm(_O_RG, RV_W)).astype(BF16)
    cb_ref[...] = mm(_O_CB, CONV_WIDTH).astype(BF16)
    cu_ref[...] = (mm(_O_CC, CONV_WIDTH) * mm(_O_CX, CONV_WIDTH)).astype(BF16)

    cm = cm_ref[...]
    sm = sm_ref[...]
    qn = (_rms(mm(_O_QD, MLA_Q_RANK)) * qn_ref[...]).astype(BF16)
    for hd in range(MLA_HEADS):
        lo = hd * HEAD_PAD
        qa = _dot(qn, wuq_ref[:, lo:lo + HEAD_PAD])
        qm_ref[:, lo:lo + HEAD_PAD] = (qa * cm + _rot_half(qa, MLA_ROPE // 4) * sm).astype(BF16)

    kvn = (_rms(mm(_O_KVD, MLA_KV_RANK)) * kvn_ref[...]).astype(BF16)
    kr = mm(_O_KR, HEAD_PAD)
    kr = kr * cm + _rot_half(kr, MLA_ROPE // 4) * sm
    kn = _dot(kvn, wk_ref[...])
    for hd in range(MLA_HEADS):
        lo = hd * HEAD_PAD
        km_ref[:, lo:lo + HEAD_PAD] = (kn[:, lo:lo + HEAD_PAD] + kr).astype(BF16)
    vm_ref[...] = (_dot(kvn, wv_ref[...]) + ones_ref[...]).astype(BF16)

    for br in range(N_BRANCH):
        gt_ref[:, br * d_model:(br + 1) * d_model] = jax.nn.sigmoid(
            mm(_O_GT + br * d_model, d_model)).astype(BF16)


def _inproj_call(x_parts, a1, b1, tabs, wts, *, n_batch, nj, ncb, b0=0):
    fused = len(x_parts) == 3
    d = x_parts[0].shape[1]
    rows = n_batch * nj * ROW_BLOCK
    tm = ROW_BLOCK
    row = lambda j, b: (b * nj + j, 0)
    mod = lambda j, b: (2 * b + (j >= ncb).astype(jnp.int32), 0, 0)
    tab = lambda j, b: (j, 0)
    if fused:
        in_specs = [pl.BlockSpec((tm, d), row), pl.BlockSpec((tm, d), row),
                    pl.BlockSpec((None, 1, d), mod)]
    else:
        in_specs = [pl.BlockSpec((tm, d),
                                 lambda j, b: (jnp.where(j < ncb, (b0 + b) * ncb + j, 0), 0)),
                    pl.BlockSpec((tm, d), lambda j, b: (
                        jnp.where(j < ncb, 0, (b0 + b) * (nj - ncb) + j - ncb), 0))]
    in_specs += [pl.BlockSpec((None, 1, d), mod), pl.BlockSpec((None, 1, d), mod)]
    in_specs += [pl.BlockSpec((tm, t.shape[1]), tab) for t in tabs]
    in_specs += [_const_spec(w.shape) for w in wts]
    widths = (RQ_W, RQ_W, RV_W, RV_W, CONV_WIDTH, CONV_WIDTH, MLA_W, MLA_W, MLA_W, N_BRANCH * d)
    out_shape = [jax.ShapeDtypeStruct((rows, w), BF16) for w in widths]
    out_specs = [pl.BlockSpec((tm, w), row) for w in widths]
    out_shape.append(jax.ShapeDtypeStruct((rows, d), F32))
    out_specs.append(pl.BlockSpec((tm, d), row))
    return pl.pallas_call(
        functools.partial(_inproj_body, d_model=d, fused=fused, ncb=ncb),
        out_shape=out_shape,
        grid=(nj, n_batch),
        in_specs=in_specs,
        out_specs=out_specs,
        compiler_params=_params(2),
        name="in_proj",
    )(*x_parts, a1, b1, *tabs, *wts)


def _ret_direction(q_ref, k_ref, v_ref, y_ref, s_ref, dm_ref, xi_ref, zt_ref, cd_ref, chunk_order):
    states = [s_ref[hd] for hd in range(RET_HEADS)]
    for c in chunk_order:
        rows = slice(c * RET_CHUNK, (c + 1) * RET_CHUNK)
        q = q_ref[rows, :]
        k = k_ref[rows, :]
        v = v_ref[rows, :]
        qx = (q.astype(F32) * xi_ref[...]).astype(BF16)
        kz = (k.astype(F32) * zt_ref[...]).astype(BF16)
        for hd in range(RET_HEADS):
            ks = slice(hd * RET_DK, (hd + 1) * RET_DK)
            vs = slice(hd * RET_DV, (hd + 1) * RET_DV)
            vh = v[:, vs]
            sc = lax.dot_general(q[:, ks], k[:, ks], (((1,), (1,)), ((), ())),
                                 preferred_element_type=F32)
            inner = _dot((sc * dm_ref[hd]).astype(BF16), vh)
            cross = _dot(qx[:, ks], states[hd].astype(BF16))
            y_ref[rows, vs] = inner + cross
            upd = lax.dot_general(kz[:, ks], vh, (((0,), (0,)), ((), ())),
                                  preferred_element_type=F32)
            states[hd] = cd_ref[hd] * states[hd] + upd
    for hd in range(RET_HEADS):
        s_ref[hd] = states[hd]


def _ret_body(qf_ref, kf_ref, vf_ref, qb_ref, kb_ref, vb_ref,
              dmf_ref, dmb_ref, xif_ref, ztf_ref, xib_ref, ztb_ref, cdf_ref, cdb_ref,
              yf_ref, yb_ref, sf_ref, sb_ref):
    @pl.when(pl.program_id(1) == 0)
    def _():
        sf_ref[...] = jnp.zeros_like(sf_ref)
        sb_ref[...] = jnp.zeros_like(sb_ref)

    n_chunks = ROW_BLOCK // RET_CHUNK
    _ret_direction(qf_ref, kf_ref, vf_ref, yf_ref, sf_ref, dmf_ref, xif_ref, ztf_ref, cdf_ref,
                   range(n_chunks))
    _ret_direction(qb_ref, kb_ref, vb_ref, yb_ref, sb_ref, dmb_ref, xib_ref, ztb_ref, cdb_ref,
                   range(n_chunks - 1, -1, -1))


def _ret_call(rq, rk, rv, consts, *, n_batch, nj, ncb):
    rows = rq.shape[0]
    tm = ROW_BLOCK
    fwd = lambda b, s: (b * nj + s, 0)

    def bwd(b, s):
        return (b * nj + jnp.where(s < ncb, ncb - 1 - s, nj - 1 - (s - ncb)), 0)

    specs = []
    for im in (fwd, bwd):
        specs += [pl.BlockSpec((tm, RQ_W), im), pl.BlockSpec((tm, RQ_W), im),
                  pl.BlockSpec((tm, RV_W), im)]
    specs += [_const_spec(c.shape) for c in consts]
    return pl.pallas_call(
        _ret_body,
        out_shape=[jax.ShapeDtypeStruct((rows, RV_W), F32)] * 2,
        grid=(n_batch, nj),
        in_specs=specs,
        out_specs=[pl.BlockSpec((tm, RV_W), fwd), pl.BlockSpec((tm, RV_W), bwd)],
        scratch_shapes=[pltpu.VMEM((RET_HEADS, RET_DK, RET_DV), F32)] * 2,
        compiler_params=_params(2),
        name="retention",
    )(rq, rk, rv, rq, rk, rv, *consts)


def _pick_tile(n, candidates):
    for c in candidates:
        if n % c == 0:
            return c
    raise ValueError(f"no tile for {n}")


def _attn_body(q_ref, k_ref, v_ref, o_ref, s_ref, *, n_ctx, n_all, ncb):
    tq = q_ref.shape[0]
    heads = [slice(hh * HEAD_PAD, (hh + 1) * HEAD_PAD) for hh in range(2)]

    def attend(n_keys, tk):
        nt = n_keys // tk
        qs = [q_ref[:, hs] for hs in heads]

        def qk(t, mrun):
            r0 = pl.multiple_of(t * tk, tk)
            out = []
            for hh, hs in enumerate(heads):
                s = lax.dot_general(qs[hh], k_ref[pl.ds(r0, tk), hs], (((1,), (1,)), ((), ())),
                                    preferred_element_type=F32)
                s_ref[hh, t, :, 0:tk] = s
                m = mrun[hh]
                for cc in range(tk // LANES):
                    m = jnp.maximum(m, s[:, cc * LANES:(cc + 1) * LANES])
                out.append(m)
            return tuple(out)

        mrun = lax.fori_loop(0, nt, qk, tuple(jnp.full((tq, LANES), -jnp.inf, F32) for _ in heads),
                             unroll=True)
        mrow = [jnp.max(m, axis=-1, keepdims=True) for m in mrun]

        def pv(t, accs):
            r0 = pl.multiple_of(t * tk, tk)
            out = []
            for hh, hs in enumerate(heads):
                p = jnp.exp2(s_ref[hh, t, :, 0:tk] - mrow[hh]).astype(BF16)
                out.append(accs[hh] + _dot(p, v_ref[pl.ds(r0, tk), hs]))
            return tuple(out)

        accs = lax.fori_loop(0, nt, pv, tuple(jnp.zeros((tq, HEAD_PAD), F32) for _ in heads),
                             unroll=True)
        o_ref[...] = jnp.concatenate([a[:, :MLA_V] / a[:, MLA_V:MLA_V + 1] for a in accs],
                                     axis=-1).astype(BF16)

    j = pl.program_id(2)

    @pl.when(j < ncb)
    def _():
        attend(n_ctx, _pick_tile(n_ctx, (256, 128)))

    @pl.when(j >= ncb)
    def _():
        attend(n_all, _pick_tile(n_all, ATTN_KEY_TILES))


def _attn_call(qm, km, vm, *, n_batch, nj, ncb, n_ctx):
    rows = qm.shape[0]
    tm = ROW_BLOCK
    t_all = nj * tm
    tk = _pick_tile(t_all, ATTN_KEY_TILES)
    qmap = lambda b, hp, j: (b * nj + j, hp)
    kmap = lambda b, hp, j: (b, hp)
    return pl.pallas_call(
        functools.partial(_attn_body, n_ctx=n_ctx, n_all=t_all, ncb=ncb),
        out_shape=jax.ShapeDtypeStruct((rows, MLA_O), BF16),
        grid=(n_batch, MLA_HEADS // 2, nj),
        in_specs=[pl.BlockSpec((tm, 2 * HEAD_PAD), qmap),
                  pl.BlockSpec((t_all, 2 * HEAD_PAD), kmap),
                  pl.BlockSpec((t_all, 2 * HEAD_PAD), kmap)],
        out_specs=pl.BlockSpec((tm, 2 * MLA_V), qmap),
        scratch_shapes=[pltpu.VMEM((2, t_all // tk, tm, tk), F32)],
        compiler_params=_params(3),
        name="mla_attention",
    )(qm, km, vm)


def _merge_body(yf_ref, yb_ref, rg_ref, cb_ref, cu_ref, cup_ref, cun_ref, om_ref, gt_ref, x_ref,
                g1_ref, a2_ref, b2_ref, gn_ref, cw_ref, wro_ref, wco_ref, wmo_ref, wout_ref,
                wrh_ref, wrl_ref, rb_ref, xmid_ref, h2_ref, rt_ref, *, nj, ncb, d_model):
    tm = x_ref.shape[0]
    j = pl.program_id(0) % nj
    seg_first = jnp.logical_or(j == 0, j == ncb)
    seg_last = jnp.logical_or(j == ncb - 1, j == nj - 1)

    y = yf_ref[...] + yb_ref[...]
    yn = jnp.concatenate([_rms(y[:, hd * RET_DV:(hd + 1) * RET_DV]) for hd in range(RET_HEADS)],
                         axis=-1) * gn_ref[...]
    y_ret = _dot((rg_ref[...].astype(F32) * yn).astype(BF16), wro_ref[...])

    u = cu_ref[...].astype(F32)
    ridx = lax.broadcasted_iota(jnp.int32, u.shape, 0)
    prev_row = cup_ref[...].astype(F32)[BF16_SUBLANES - 1:, :] * jnp.where(seg_first, 0.0, 1.0)
    next_row = cun_ref[...].astype(F32)[0:1, :] * jnp.where(seg_last, 0.0, 1.0)
    u_prev = jnp.where(ridx == 0, prev_row, pltpu.roll(u, 1, 0))
    u_next = jnp.where(ridx == tm - 1, next_row, pltpu.roll(u, tm - 1, 0))
    conv = u_prev * cw_ref[0:1, :] + u * cw_ref[1:2, :] + u_next * cw_ref[2:3, :]
    y_conv = _dot((cb_ref[...].astype(F32) * conv).astype(BF16), wco_ref[...])

    y_mla = _dot(om_ref[...], wmo_ref[...])

    merged = (gt_ref[:, 0:d_model].astype(F32) * y_ret
              + gt_ref[:, d_model:2 * d_model].astype(F32) * y_conv
              + gt_ref[:, 2 * d_model:3 * d_model].astype(F32) * y_mla)
    x_mid = x_ref[...] + g1_ref[...] * _dot(merged.astype(BF16), wout_ref[...])
    xmid_ref[...] = x_mid

    h2 = _rms(x_mid) * a2_ref[...] + b2_ref[...]
    h_hi, h_lo = _split_bf16(h2)
    logits = _dot(h_hi, wrh_ref[...]) + _dot(h_hi, wrl_ref[...]) + _dot(h_lo, wrh_ref[...])
    cls, w_lo, w_hi = _route_rows(jnp.transpose(logits)[:N_EXPERTS, :], rb_ref[...])
    rank, counts = _block_ranks(cls.astype(F32))
    rt_ref[...] = jnp.concatenate(
        [cls.astype(F32), w_lo, w_hi, rank, counts, jnp.zeros((ROUTE_ROWS - 5, tm), F32)], axis=0)
    h2_ref[...] = h_hi


def _top2_of4(v):
    def first_max(rows):
        best, idx = rows[0], jnp.zeros(rows[0].shape, jnp.int32)
        for e in range(1, len(rows)):
            better = rows[e] > best
            idx = jnp.where(better, e, idx)
            best = jnp.where(better, rows[e], best)
        return best, idx

    b1, i1 = first_max(v)
    b2, i2 = first_max([jnp.where(i1 == e, -jnp.inf, v[e]) for e in range(len(v))])
    return i1, i2, b1, b2


def _route_rows(logits_t, bias):
    scores = jax.nn.sigmoid(logits_t)
    biased = scores + bias
    row = lambda a, e: a[e:e + 1, :]
    best = None
    for g in range(N_GROUPS):
        v = [row(biased, g * EXPERTS_PER_GROUP + e) for e in range(EXPERTS_PER_GROUP)]
        i1, i2, b1, b2 = _top2_of4(v)
        cand = (b1 + b2, jnp.full(i1.shape, g, jnp.int32), i1, i2)
        if best is None:
            best = cand
        else:
            better = cand[0] > best[0]
            best = tuple(jnp.where(better, c, o) for c, o in zip(cand, best))
    _, g_sel, i1, i2 = best
    lo = jnp.minimum(i1, i2)
    hi = jnp.maximum(i1, i2)
    e_lo = g_sel * EXPERTS_PER_GROUP + lo
    e_hi = g_sel * EXPERTS_PER_GROUP + hi
    s_lo = jnp.zeros_like(best[0])
    s_hi = jnp.zeros_like(best[0])
    for e in range(N_EXPERTS):
        s_lo = jnp.where(e_lo == e, row(scores, e), s_lo)
        s_hi = jnp.where(e_hi == e, row(scores, e), s_hi)
    total = s_lo + s_hi
    pair_base = jnp.where(lo == 0, 0, jnp.where(lo == 1, 3, 5))
    cls = g_sel * len(PAIRS) + pair_base + (hi - lo - 1)
    return cls, s_lo / total, s_hi / total


def _block_ranks(cls_row):
    n = cls_row.shape[1]
    cls_col = jnp.transpose(jnp.broadcast_to(cls_row, (LANES, n)))[:, 0:1]
    ii = lax.broadcasted_iota(jnp.int32, (n, n), 0)
    jj = lax.broadcasted_iota(jnp.int32, (n, n), 1)
    earlier_same = jnp.logical_and(cls_col == cls_row, ii < jj)
    rank = jnp.sum(jnp.where(earlier_same, 1.0, 0.0), axis=0, keepdims=True)
    lane = lax.broadcasted_iota(jnp.int32, (1, n), 1)
    counts = jnp.zeros((1, n), F32)
    for c in range(N_CLASSES):
        cnt = jnp.sum(jnp.where(cls_row == c, 1.0, 0.0), axis=1, keepdims=True)
        counts = jnp.where(lane == c, cnt, counts)
    return rank, counts


def _merge_call(yf, yb, rg, cb, cu, om, gt, xa, g1, a2, b2, wts, *, nj, ncb):
    rows, d = xa.shape
    tm = ROW_BLOCK
    nb = rows // tm
    halo = BF16_SUBLANES
    per_blk = tm // halo
    row = lambda i: (i, 0)
    mod = lambda i: (2 * (i // nj) + ((i % nj) >= ncb).astype(jnp.int32), 0, 0)
    prev = lambda i: (jnp.maximum(i * per_blk - 1, 0), 0)
    nxt = lambda i: (jnp.minimum((i + 1) * per_blk, rows // halo - 1), 0)
    in_specs = [pl.BlockSpec((tm, RV_W), row), pl.BlockSpec((tm, RV_W), row),
                pl.BlockSpec((tm, RV_W), row),
                pl.BlockSpec((tm, CONV_WIDTH), row), pl.BlockSpec((tm, CONV_WIDTH), row),
                pl.BlockSpec((halo, CONV_WIDTH), prev), pl.BlockSpec((halo, CONV_WIDTH), nxt),
                pl.BlockSpec((tm, MLA_O), row), pl.BlockSpec((tm, N_BRANCH * d), row),
                pl.BlockSpec((tm, d), row),
                pl.BlockSpec((None, 1, d), mod), pl.BlockSpec((None, 1, d), mod),
                pl.BlockSpec((None, 1, d), mod)]
    in_specs += [_const_spec(w.shape) for w in wts]
    return pl.pallas_call(
        functools.partial(_merge_body, nj=nj, ncb=ncb, d_model=d),
        out_shape=[jax.ShapeDtypeStruct((rows, d), F32), jax.ShapeDtypeStruct((rows, d), BF16),
                   jax.ShapeDtypeStruct((nb * ROUTE_ROWS, tm), F32)],
        grid=(nb,),
        in_specs=in_specs,
        out_specs=[pl.BlockSpec((tm, d), row), pl.BlockSpec((tm, d), row),
                   pl.BlockSpec((ROUTE_ROWS, tm), row)],
        compiler_params=_params(1),
        name="merge_out_proj",
    )(yf, yb, rg, cb, cu, cu, cu, om, gt, xa, g1, a2, b2, *wts)


def _moe_body(ea_ref, eb_ref, nu_ref, h_ref, gw_ref, w1a_ref, w3a_ref, w2a_ref, w1b_ref, w3b_ref,
              w2b_ref, o_ref, w13a_s, w2a_s, w13b_s, w2b_s):
    t = pl.program_id(0)
    used = t < nu_ref[0]
    prev = jnp.maximum(t - 1, 0)

    def refresh(e_ref, w1_ref, w3_ref, w2_ref, w13_s, w2_s):
        @pl.when(jnp.logical_and(used, jnp.logical_or(t == 0, e_ref[t] != e_ref[prev])))
        def _():
            w13_s[:, :D_EXPERT] = w1_ref[...].astype(BF16)
            w13_s[:, D_EXPERT:] = w3_ref[...].astype(BF16)
            w2_s[...] = w2_ref[...].astype(BF16)

    refresh(ea_ref, w1a_ref, w3a_ref, w2a_ref, w13a_s, w2a_s)
    refresh(eb_ref, w1b_ref, w3b_ref, w2b_ref, w13b_s, w2b_s)

    @pl.when(used)
    def _():
        h = h_ref[...]
        gw = gw_ref[...]

        def expert(w13_s, w2_s, wt):
            a = _dot(h, w13_s[...])
            act = _silu(a[:, :D_EXPERT]) * a[:, D_EXPERT:] * wt
            return _dot(act.astype(BF16), w2_s[...])

        o_ref[...] = (expert(w13a_s, w2a_s, gw[:, 0:1])
                      + expert(w13b_s, w2b_s, gw[:, 1:2])).astype(BF16)

    @pl.when(jnp.logical_not(used))
    def _():
        o_ref[...] = jnp.zeros_like(o_ref)


def _moe_call(tile_ea, tile_eb, n_used, hs, gw, w1, w3, w2, layer):
    npad, d = hs.shape
    tmo = MOE_TILE
    row = lambda t, ea, eb, nu: (t, 0)
    wa = lambda t, ea, eb, nu: (layer, ea[t], 0, 0)
    wb = lambda t, ea, eb, nu: (layer, eb[t], 0, 0)
    up = lambda im: pl.BlockSpec((None, None, d, D_EXPERT), im)
    down = lambda im: pl.BlockSpec((None, None, D_EXPERT, d), im)
    grid_spec = pltpu.PrefetchScalarGridSpec(
        num_scalar_prefetch=3,
        grid=(npad // tmo,),
        in_specs=[pl.BlockSpec((tmo, d), row), pl.BlockSpec((tmo, 2), row),
                  up(wa), up(wa), down(wa), up(wb), up(wb), down(wb)],
        out_specs=pl.BlockSpec((tmo, d), row),
        scratch_shapes=[pltpu.VMEM((d, 2 * D_EXPERT), BF16), pltpu.VMEM((D_EXPERT, d), BF16)] * 2)
    return pl.pallas_call(
        _moe_body,
        out_shape=jax.ShapeDtypeStruct((npad, d), BF16),
        grid_spec=grid_spec,
        compiler_params=_params(1),
        name="moe_experts",
    )(tile_ea, tile_eb, n_used, hs, gw, w1, w3, w2, w1, w3, w2)


def _dispatch(route, n_tok):
    tmo = MOE_TILE
    n_tiles = n_tok // tmo + N_CLASSES
    npad = n_tiles * tmo
    cls = route[:, 0, :].astype(jnp.int32)
    rank = route[:, 3, :].astype(jnp.int32)
    counts = route[:, 4, :N_CLASSES].astype(jnp.int32)
    tiles_per = (jnp.sum(counts, axis=0) + tmo - 1) // tmo
    tile_end = jnp.cumsum(tiles_per)
    offs = (tile_end - tiles_per) * tmo
    block_base = offs[None, :] + jnp.cumsum(counts, axis=0) - counts
    classes = jnp.arange(N_CLASSES, dtype=jnp.int32)
    base = jnp.sum(jnp.where(cls[:, :, None] == classes, block_base[:, None, :], 0), axis=-1)
    dest = (base + rank).reshape(-1)
    record = jnp.stack([jnp.arange(n_tok, dtype=F32), route[:, 1, :].reshape(-1),
                        route[:, 2, :].reshape(-1), jnp.zeros((n_tok,), F32)], axis=1)
    slots = jnp.zeros((npad, record.shape[1]), F32).at[dest].set(
        record, mode="promise_in_bounds", unique_indices=True)
    src = slots[:, 0].astype(jnp.int32)
    gw = slots[:, 1:3]
    tile_ids = jnp.arange(n_tiles, dtype=jnp.int32)
    tile_cls = jnp.sum((tile_end[None, :] <= tile_ids[:, None]).astype(jnp.int32), axis=1)
    tile_cls = jnp.minimum(tile_cls, N_CLASSES - 1)
    pa = jnp.asarray([p[0] for p in PAIRS], jnp.int32)
    pb = jnp.asarray([p[1] for p in PAIRS], jnp.int32)
    group = (tile_cls // len(PAIRS)) * EXPERTS_PER_GROUP
    tile_ea = group + pa[tile_cls % len(PAIRS)]
    tile_eb = group + pb[tile_cls % len(PAIRS)]
    n_used = tile_end[-1:].astype(jnp.int32)
    return src, dest, gw, tile_ea, tile_eb, n_used


def _final_body(x_ref, f_ref, g2_ref, fn_ref, *rest):
    o_ref = rest[-1]
    o_ref[...] = _rms(x_ref[...] + g2_ref[...] * f_ref[...].astype(F32)) * fn_ref[...]


def _final_call(xmid, fg, g2, final_norm, out_buf, *, n_batch, nj, ncb, b0, n_total):
    d = xmid.shape[1]
    tm = ROW_BLOCK
    njl = nj - ncb
    src = lambda b, j: (b * nj + ncb + j, 0)
    in_specs = [pl.BlockSpec((tm, d), src), pl.BlockSpec((tm, d), src),
                pl.BlockSpec((None, 1, d), lambda b, j: (2 * b + 1, 0, 0)),
                _const_spec((1, d))]
    args = [xmid, fg, g2, final_norm.reshape(1, d)]
    aliases = {}
    if out_buf is not None:
        in_specs.append(pl.BlockSpec(memory_space=pl.ANY))
        args.append(out_buf)
        aliases = {len(args) - 1: 0}
    return pl.pallas_call(
        _final_body,
        out_shape=jax.ShapeDtypeStruct((n_total * njl * tm, d), F32),
        grid=(n_batch, njl),
        in_specs=in_specs,
        out_specs=pl.BlockSpec((tm, d), lambda b, j: ((b0 + b) * njl + j, 0)),
        input_output_aliases=aliases,
        compiler_params=_params(2),
        name="final_norm",
    )(*args)


def _split_cols(w, sizes):
    out, off = [], 0
    for s in sizes:
        out.append(w[:, off:off + s])
        off += s
    out.append(w[:, off:])
    return out


def _layer_weights(w_in, w_uq, w_ukv, q_norm, kv_norm):
    wq, wk, wv, wg, wcb, wcc, wcx, wqd, wkvd, wkr, wgate = _split_cols(w_in, IN_SIZES[:-1])
    rope_lanes = lambda w: jnp.pad(w, ((0, 0), (MLA_NOPE, HEAD_PAD - MLA_NOPE - MLA_ROPE)))
    w_ext = jnp.concatenate(
        [wq, wk, wv, wg, wcb, wcc, wcx, wqd, wkvd, rope_lanes(wkr), wgate], axis=1).astype(BF16)

    uq = w_uq.reshape(MLA_Q_RANK, MLA_HEADS, MLA_NOPE + MLA_ROPE)
    tail = HEAD_PAD - MLA_NOPE - MLA_ROPE
    uq_pad = jnp.pad(uq, ((0, 0), (0, 0), (0, tail)))
    wuq = uq_pad.reshape(MLA_Q_RANK, MLA_W).astype(BF16)

    ukv = w_ukv.reshape(MLA_KV_RANK, MLA_HEADS, MLA_NOPE + MLA_V)
    wk_up = jnp.pad(ukv[..., :MLA_NOPE], ((0, 0), (0, 0), (0, HEAD_PAD - MLA_NOPE)))
    wk_up = wk_up.reshape(MLA_KV_RANK, MLA_W).astype(BF16)
    wv_pad = jnp.pad(ukv[..., MLA_NOPE:], ((0, 0), (0, 0), (0, HEAD_PAD - MLA_V)))
    wv_pad = wv_pad.reshape(MLA_KV_RANK, MLA_W).astype(BF16)
    ones_row = jnp.zeros((MLA_HEADS, HEAD_PAD), F32).at[:, MLA_V].set(1.0).reshape(1, MLA_W)
    q_gain = q_norm.astype(F32) * ((MLA_NOPE + MLA_ROPE) ** -0.5 * math.log2(math.e))
    return (w_ext, wuq, wk_up, wv_pad, q_gain.reshape(1, -1),
            kv_norm.reshape(1, -1).astype(F32), ones_row)


def _rotary_tables(n_ctx, n_lat):
    n_all = n_ctx + n_lat
    row = lax.broadcasted_iota(jnp.int32, (n_all, LANES), 0)
    lane = lax.broadcasted_iota(jnp.int32, (n_all, LANES), 1)
    is_ctx = row < n_ctx
    pos = row - n_ctx
    grid_row = pos // GRID_W
    grid_col = pos - grid_row * GRID_W

    def inv_freq(idx, half):
        return ROPE_BASE ** (-idx.astype(F32) / half)

    def table(ang, first_half, live):
        cos = jnp.where(is_ctx, 1.0, jnp.cos(ang))
        sin = jnp.where(is_ctx, 0.0, jnp.where(first_half, -jnp.sin(ang), jnp.sin(ang)))
        return jnp.where(live, cos, 0.0), jnp.where(live, sin, 0.0)

    half = RET_DK // 2
    ang = pos.astype(F32) * inv_freq(lane % half, half)
    cr, sr = table(ang, (lane % RET_DK) < half, True)

    quarter = MLA_ROPE // 4
    rl = lane - MLA_NOPE
    coord = jnp.where(rl < MLA_ROPE // 2, grid_row, grid_col)
    ang = coord.astype(F32) * inv_freq(rl % quarter, quarter)
    cm, sm = table(ang, (rl % (2 * quarter)) < quarter,
                   jnp.logical_and(rl >= 0, rl < MLA_ROPE))
    cm = jnp.where(rl < 0, 1.0, cm)
    return cr, sr, cm, sm


def _retention_consts(ret_decay):
    log_gf = jax.nn.log_sigmoid(ret_decay[0].astype(F32))
    log_gb = jax.nn.log_sigmoid(ret_decay[1].astype(F32))
    idx = jnp.arange(RET_CHUNK, dtype=F32)
    rel = idx[:, None] - idx[None, :]
    dm_f = jnp.where(rel >= 0, jnp.exp(log_gf[:, None, None] * jnp.maximum(rel, 0.0)[None]), 0.0)
    dm_b = jnp.where(rel < 0, jnp.exp(log_gb[:, None, None] * jnp.maximum(-rel, 0.0)[None]), 0.0)

    def lanes(t):
        return jnp.repeat(t, RET_DK, axis=1)

    xi_f = lanes(jnp.exp(log_gf[None, :] * (idx + 1.0)[:, None]))
    zt_f = lanes(jnp.exp(log_gf[None, :] * (RET_CHUNK - 1 - idx)[:, None]))
    xi_b = lanes(jnp.exp(log_gb[None, :] * (RET_CHUNK - idx)[:, None]))
    zt_b = lanes(jnp.exp(log_gb[None, :] * idx[:, None]))
    cd = lambda lg: jnp.broadcast_to(jnp.exp(lg * RET_CHUNK)[:, None, None], (RET_HEADS, 1, RET_DV))
    return dm_f, dm_b, xi_f, zt_f, xi_b, zt_b, cd(log_gf), cd(log_gb)


def kernel(x, c, ctx, c_ctx, w_ada, b_ada, norm1, norm2, w_in, ret_decay, ret_gn, w_ret_o, conv_w,
           w_conv_o, mla_q_norm, w_uq, mla_kv_norm, w_ukv, w_mla_o, w_out, w_router, router_bias,
           w1, w3, w2, final_norm):
    n_batch, n_lat, d = x.shape
    n_ctx = ctx.shape[1]
    depth = w_ada.shape[0]
    t_all = n_ctx + n_lat
    assert n_ctx % ROW_BLOCK == 0 and n_lat % ROW_BLOCK == 0 and n_lat % GRID_W == 0
    nj = t_all // ROW_BLOCK
    ncb = n_ctx // ROW_BLOCK
    n_chains = 2 if n_batch % 2 == 0 else 1
    bh = n_batch // n_chains
    n_tok = bh * t_all
    assert n_tok % MOE_TILE == 0
    geom = dict(n_batch=bh, nj=nj, ncb=ncb)

    cc = jnp.concatenate([c, c_ctx[None, :]], axis=0)
    cc = jnp.pad(cc, ((0, -cc.shape[0] % 8), (0, 0)))
    mod = _ada_call(cc, w_ada, b_ada)[:, :n_batch + 1].reshape(depth, n_batch + 1, N_MOD, d)
    pick = jnp.stack([jnp.full((n_batch,), n_batch, jnp.int32),
                      jnp.arange(n_batch, dtype=jnp.int32)], axis=1).reshape(-1)
    mod = mod[:, pick]

    tabs = _rotary_tables(n_ctx, n_lat)
    wr_hi, wr_lo = _split_bf16(jnp.pad(w_router.astype(F32), ((0, 0), (0, LANES - N_EXPERTS))))
    rbias = router_bias.astype(F32).reshape(N_EXPERTS, 1)
    layer_wts = [_layer_weights(w_in[l], w_uq[l], w_ukv[l], mla_q_norm[l], mla_kv_norm[l])
                 for l in range(depth)]
    merge_wts = [(ret_gn[l].reshape(1, -1).astype(F32), conv_w[l].T.astype(F32),
                  w_ret_o[l].astype(BF16), w_conv_o[l].astype(BF16), w_mla_o[l].astype(BF16),
                  w_out[l].astype(BF16), wr_hi, wr_lo, rbias) for l in range(depth)]
    ret_consts = [_retention_consts(ret_decay[l]) for l in range(depth)]
    ctx2 = ctx.reshape(n_batch * n_ctx, d)
    x2 = x.reshape(n_batch * n_lat, d)

    out = None
    for ch in range(n_chains):
        b0 = ch * bh
        x_parts = (ctx2, x2)
        for l in range(depth):
            m = mod[l, 2 * b0:2 * (b0 + bh)]
            rowvec = lambda v: v.reshape(2 * bh, 1, d)
            a1 = rowvec(norm1[l][None, :] * (1.0 + m[:, 1]))
            b1 = rowvec(m[:, 0])
            g1 = rowvec(m[:, 2])
            a2 = rowvec(norm2[l][None, :] * (1.0 + m[:, 4]))
            b2 = rowvec(m[:, 3])
            g2 = rowvec(m[:, 5])

            proj = _inproj_call(x_parts, a1, b1, tabs, layer_wts[l], b0=b0, **geom)
            rq, rk, rv, rg, cb, cu, qm, km, vm, gt, xa = proj
            yf, yb = _ret_call(rq, rk, rv, ret_consts[l], **geom)
            om = _attn_call(qm, km, vm, n_ctx=n_ctx, **geom)
            xmid, h2, route = _merge_call(yf, yb, rg, cb, cu, om, gt, xa, g1, a2, b2, merge_wts[l],
                                          nj=nj, ncb=ncb)
            route = route.reshape(-1, ROUTE_ROWS, ROW_BLOCK)
            src, dest, gw, tile_ea, tile_eb, n_used = _dispatch(route, n_tok)
            hs = h2.at[src].get(mode="promise_in_bounds")
            f_sorted = _moe_call(tile_ea, tile_eb, n_used, hs, gw, w1, w3, w2, l)
            fg = f_sorted.at[dest].get(mode="promise_in_bounds")

            if l < depth - 1:
                x_parts = (xmid, fg, g2)
            else:
                out = _final_call(xmid, fg, g2, final_norm, out, b0=b0, n_total=n_batch, **geom)
    return out.reshape(n_batch, n_lat, d)
```

```python
import functools
import math

import jax
import jax.numpy as jnp
from jax import lax
from jax.experimental import pallas as pl
from jax.experimental.pallas import tpu as pltpu

F32 = jnp.float32
BF16 = jnp.bfloat16

GRID_W = 64
RMS_EPS = 1e-6
ROPE_BASE = 10000.0
N_MOD = 6
RET_HEADS = 4
RET_DK = 64
RET_DV = 128
RET_CHUNK = 256
CONV_WIDTH = 512
MLA_HEADS = 8
MLA_Q_RANK = 384
MLA_KV_RANK = 256
MLA_NOPE = 64
MLA_ROPE = 32
MLA_V = 64
N_BRANCH = 3
N_EXPERTS = 16
N_GROUPS = 4
EXPERTS_PER_GROUP = N_EXPERTS // N_GROUPS
D_EXPERT = 512
IN_SIZES = (RET_HEADS * RET_DK, RET_HEADS * RET_DK, RET_HEADS * RET_DV, RET_HEADS * RET_DV,
            CONV_WIDTH, CONV_WIDTH, CONV_WIDTH, MLA_Q_RANK, MLA_KV_RANK, MLA_ROPE, 0)

LANES = 128
BF16_SUBLANES = 16
VMEM_LIMIT = 56 * 1024 * 1024

HEAD_PAD = LANES
ROW_BLOCK = 256
MOE_TILE = 256
ATTN_KEY_TILES = (768, 512, 256, 128)
ROUTE_ROWS = 8
PAIRS = [(a, b) for a in range(EXPERTS_PER_GROUP) for b in range(a + 1, EXPERTS_PER_GROUP)]
N_CLASSES = N_GROUPS * len(PAIRS)

RQ_W = RET_HEADS * RET_DK
RV_W = RET_HEADS * RET_DV
MLA_W = MLA_HEADS * HEAD_PAD
MLA_O = MLA_HEADS * MLA_V


def _const_spec(shape):
    nd = len(shape)
    return pl.BlockSpec(shape, lambda *_: (0,) * nd, pipeline_mode=pl.Buffered(1))


def _params(n_axes):
    return pltpu.CompilerParams(dimension_semantics=("arbitrary",) * n_axes,
                                vmem_limit_bytes=VMEM_LIMIT)


def _dot(a, b):
    return jnp.dot(a, b, preferred_element_type=F32)


def _split_bf16(a):
    hi = a.astype(BF16)
    lo = (a - hi.astype(F32)).astype(BF16)
    return hi, lo


def _silu(v):
    return v * jax.nn.sigmoid(v)


def _rms(v):
    return v * lax.rsqrt(jnp.mean(v * v, axis=-1, keepdims=True) + RMS_EPS)


def _ada_body(c_ref, w_ref, b_ref, o_ref):
    a_hi, a_lo = _split_bf16(_silu(c_ref[...]))
    w_hi, w_lo = _split_bf16(w_ref[...])
    o_ref[...] = _dot(a_hi, w_hi) + _dot(a_hi, w_lo) + _dot(a_lo, w_hi) + b_ref[...]


def _ada_call(cc, w_ada, b_ada):
    depth, d, nm = w_ada.shape
    rows = cc.shape[0]
    cb = nm // 4
    return pl.pallas_call(
        _ada_body,
        out_shape=jax.ShapeDtypeStruct((depth, rows, nm), F32),
        grid=(depth, nm // cb),
        in_specs=[pl.BlockSpec((rows, d), lambda l, n: (0, 0)),
                  pl.BlockSpec((None, d, cb), lambda l, n: (l, 0, n)),
                  pl.BlockSpec((None, 1, cb), lambda l, n: (l, 0, n))],
        out_specs=pl.BlockSpec((None, rows, cb), lambda l, n: (l, 0, n)),
        compiler_params=_params(2),
        name="ada_mod",
    )(cc, w_ada, b_ada.reshape(depth, 1, nm))


_O_RQ = 0
_O_RK = _O_RQ + RQ_W
_O_RV = _O_RK + RQ_W
_O_RG = _O_RV + RV_W
_O_CB = _O_RG + RV_W
_O_CC = _O_CB + CONV_WIDTH
_O_CX = _O_CC + CONV_WIDTH
_O_QD = _O_CX + CONV_WIDTH
_O_KVD = _O_QD + MLA_Q_RANK
_O_KR = _O_KVD + MLA_KV_RANK
_O_GT = _O_KR + LANES


def _rot_half(v, half):
    width = v.shape[1]
    lane = lax.broadcasted_iota(jnp.int32, v.shape, 1)
    first = (lane % (2 * half)) < half
    return jnp.where(first, pltpu.roll(v, width - half, 1), pltpu.roll(v, half, 1))


def _inproj_body(*refs, d_model, fused, ncb):
    xo_ref = refs[-1]
    if fused:
        xm_ref, f_ref, g2_ref = refs[:3]
        refs = refs[3:]
        x = xm_ref[...] + g2_ref[...] * f_ref[...].astype(F32)
    else:
        ctx_ref, lat_ref = refs[:2]
        refs = refs[2:]
        x = jnp.where(pl.program_id(0) < ncb, ctx_ref[...], lat_ref[...])
    xo_ref[...] = x
    (a1_ref, b1_ref, cr_ref, sr_ref, cm_ref, sm_ref, w_ref, wuq_ref, wk_ref, wv_ref, qn_ref, kvn_ref,
     ones_ref, rq_ref, rk_ref, rv_ref, rg_ref, cb_ref, cu_ref, qm_ref, km_ref, vm_ref, gt_ref) = refs[:23]
    h = (_rms(x) * a1_ref[...] + b1_ref[...]).astype(BF16)

    def mm(off, width):
        return _dot(h, w_ref[:, off:off + width])

    cr = jnp.concatenate([cr_ref[...]] * (RQ_W // cr_ref.shape[1]), axis=1)
    sr = jnp.concatenate([sr_ref[...]] * (RQ_W // sr_ref.shape[1]), axis=1)
    q = mm(_O_RQ, RQ_W)
    rq_ref[...] = (q * cr + _rot_half(q, RET_DK // 2) * sr).astype(BF16)
    k = mm(_O_RK, RQ_W)
    rk_ref[...] = ((k * cr + _rot_half(k, RET_DK // 2) * sr) * (RET_DK ** -0.5)).astype(BF16)
    rv_ref[...] = mm(_O_RV, RV_W).astype(BF16)
    rg_ref[...] = _silu(mm(_O_RG, RV_W)).astype(BF16)
    cb_ref[...] = mm(_O_CB, CONV_WIDTH).astype(BF16)
    cu_ref[...] = (mm(_O_CC, CONV_WIDTH) * mm(_O_CX, CONV_WIDTH)).astype(BF16)

    cm = cm_ref[...]
    sm = sm_ref[...]
    qn = (_rms(mm(_O_QD, MLA_Q_RANK)) * qn_ref[...]).astype(BF16)
    for hd in range(MLA_HEADS):
        lo = hd * HEAD_PAD
        qa = _dot(qn, wuq_ref[:, lo:lo + HEAD_PAD])
        qm_ref[:, lo:lo + HEAD_PAD] = (qa * cm + _rot_half(qa, MLA_ROPE // 4) * sm).astype(BF16)

    kvn = (_rms(mm(_O_KVD, MLA_KV_RANK)) * kvn_ref[...]).astype(BF16)
    kr = mm(_O_KR, HEAD_PAD)
    kr = kr * cm + _rot_half(kr, MLA_ROPE // 4) * sm
    kn = _dot(kvn, wk_ref[...])
    for hd in range(MLA_HEADS):
        lo = hd * HEAD_PAD
        km_ref[:, lo:lo + HEAD_PAD] = (kn[:, lo:lo + HEAD_PAD] + kr).astype(BF16)
    vm_ref[...] = (_dot(kvn, wv_ref[...]) + ones_ref[...]).astype(BF16)

    for br in range(N_BRANCH):
        gt_ref[:, br * d_model:(br + 1) * d_model] = jax.nn.sigmoid(
            mm(_O_GT + br * d_model, d_model)).astype(BF16)


def _inproj_call(x_parts, a1, b1, tabs, wts, *, n_batch, nj, ncb):
    fused = len(x_parts) == 3
    d = x_parts[0].shape[1]
    rows = n_batch * nj * ROW_BLOCK
    tm = ROW_BLOCK
    row = lambda j, b: (b * nj + j, 0)
    mod = lambda j, b: (2 * b + (j >= ncb).astype(jnp.int32), 0, 0)
    tab = lambda j, b: (j, 0)
    if fused:
        in_specs = [pl.BlockSpec((tm, d), row), pl.BlockSpec((tm, d), row),
                    pl.BlockSpec((None, 1, d), mod)]
    else:
        in_specs = [pl.BlockSpec((tm, d), lambda j, b: (jnp.where(j < ncb, b * ncb + j, 0), 0)),
                    pl.BlockSpec((tm, d),
                                 lambda j, b: (jnp.where(j < ncb, 0, b * (nj - ncb) + j - ncb), 0))]
    in_specs += [pl.BlockSpec((None, 1, d), mod), pl.BlockSpec((None, 1, d), mod)]
    in_specs += [pl.BlockSpec((tm, t.shape[1]), tab) for t in tabs]
    in_specs += [_const_spec(w.shape) for w in wts]
    widths = (RQ_W, RQ_W, RV_W, RV_W, CONV_WIDTH, CONV_WIDTH, MLA_W, MLA_W, MLA_W, N_BRANCH * d)
    out_shape = [jax.ShapeDtypeStruct((rows, w), BF16) for w in widths]
    out_specs = [pl.BlockSpec((tm, w), row) for w in widths]
    out_shape.append(jax.ShapeDtypeStruct((rows, d), F32))
    out_specs.append(pl.BlockSpec((tm, d), row))
    return pl.pallas_call(
        functools.partial(_inproj_body, d_model=d, fused=fused, ncb=ncb),
        out_shape=out_shape,
        grid=(nj, n_batch),
        in_specs=in_specs,
        out_specs=out_specs,
        compiler_params=_params(2),
        name="in_proj",
    )(*x_parts, a1, b1, *tabs, *wts)


def _ret_direction(q_ref, k_ref, v_ref, y_ref, s_ref, dm_ref, xi_ref, zt_ref, cd_ref, chunk_order):
    states = [s_ref[hd] for hd in range(RET_HEADS)]
    for c in chunk_order:
        rows = slice(c * RET_CHUNK, (c + 1) * RET_CHUNK)
        q = q_ref[rows, :]
        k = k_ref[rows, :]
        v = v_ref[rows, :]
        qx = (q.astype(F32) * xi_ref[...]).astype(BF16)
        kz = (k.astype(F32) * zt_ref[...]).astype(BF16)
        for hd in range(RET_HEADS):
            ks = slice(hd * RET_DK, (hd + 1) * RET_DK)
            vs = slice(hd * RET_DV, (hd + 1) * RET_DV)
            vh = v[:, vs]
            sc = lax.dot_general(q[:, ks], k[:, ks], (((1,), (1,)), ((), ())),
                                 preferred_element_type=F32)
            inner = _dot((sc * dm_ref[hd]).astype(BF16), vh)
            cross = _dot(qx[:, ks], states[hd].astype(BF16))
            y_ref[rows, vs] = inner + cross
            upd = lax.dot_general(kz[:, ks], vh, (((0,), (0,)), ((), ())),
                                  preferred_element_type=F32)
            states[hd] = cd_ref[hd] * states[hd] + upd
    for hd in range(RET_HEADS):
        s_ref[hd] = states[hd]


def _ret_body(qf_ref, kf_ref, vf_ref, qb_ref, kb_ref, vb_ref,
              dmf_ref, dmb_ref, xif_ref, ztf_ref, xib_ref, ztb_ref, cdf_ref, cdb_ref,
              yf_ref, yb_ref, sf_ref, sb_ref):
    @pl.when(pl.program_id(1) == 0)
    def _():
        sf_ref[...] = jnp.zeros_like(sf_ref)
        sb_ref[...] = jnp.zeros_like(sb_ref)

    n_chunks = ROW_BLOCK // RET_CHUNK
    _ret_direction(qf_ref, kf_ref, vf_ref, yf_ref, sf_ref, dmf_ref, xif_ref, ztf_ref, cdf_ref,
                   range(n_chunks))
    _ret_direction(qb_ref, kb_ref, vb_ref, yb_ref, sb_ref, dmb_ref, xib_ref, ztb_ref, cdb_ref,
                   range(n_chunks - 1, -1, -1))


def _ret_call(rq, rk, rv, consts, *, n_batch, nj, ncb):
    rows = rq.shape[0]
    tm = ROW_BLOCK
    fwd = lambda b, s: (b * nj + s, 0)

    def bwd(b, s):
        return (b * nj + jnp.where(s < ncb, ncb - 1 - s, nj - 1 - (s - ncb)), 0)

    specs = []
    for im in (fwd, bwd):
        specs += [pl.BlockSpec((tm, RQ_W), im), pl.BlockSpec((tm, RQ_W), im),
                  pl.BlockSpec((tm, RV_W), im)]
    specs += [_const_spec(c.shape) for c in consts]
    return pl.pallas_call(
        _ret_body,
        out_shape=[jax.ShapeDtypeStruct((rows, RV_W), F32)] * 2,
        grid=(n_batch, nj),
        in_specs=specs,
        out_specs=[pl.BlockSpec((tm, RV_W), fwd), pl.BlockSpec((tm, RV_W), bwd)],
        scratch_shapes=[pltpu.VMEM((RET_HEADS, RET_DK, RET_DV), F32)] * 2,
        compiler_params=_params(2),
        name="retention",
    )(rq, rk, rv, rq, rk, rv, *consts)


def _pick_tile(n, candidates):
    for c in candidates:
        if n % c == 0:
            return c
    raise ValueError(f"no tile for {n}")


def _attn_body(q_ref, k_ref, v_ref, o_ref, s_ref, *, n_ctx, n_all, ncb):
    tq = q_ref.shape[0]
    heads = [slice(hh * HEAD_PAD, (hh + 1) * HEAD_PAD) for hh in range(2)]

    def attend(n_keys, tk):
        nt = n_keys // tk
        qs = [q_ref[:, hs] for hs in heads]

        def qk(t, mrun):
            r0 = pl.multiple_of(t * tk, tk)
            out = []
            for hh, hs in enumerate(heads):
                s = lax.dot_general(qs[hh], k_ref[pl.ds(r0, tk), hs], (((1,), (1,)), ((), ())),
                                    preferred_element_type=F32)
                s_ref[hh, t, :, 0:tk] = s
                m = mrun[hh]
                for cc in range(tk // LANES):
                    m = jnp.maximum(m, s[:, cc * LANES:(cc + 1) * LANES])
                out.append(m)
            return tuple(out)

        mrun = lax.fori_loop(0, nt, qk, tuple(jnp.full((tq, LANES), -jnp.inf, F32) for _ in heads),
                             unroll=True)
        mrow = [jnp.max(m, axis=-1, keepdims=True) for m in mrun]

        def pv(t, accs):
            r0 = pl.multiple_of(t * tk, tk)
            out = []
            for hh, hs in enumerate(heads):
                p = jnp.exp2(s_ref[hh, t, :, 0:tk] - mrow[hh]).astype(BF16)
                out.append(accs[hh] + _dot(p, v_ref[pl.ds(r0, tk), hs]))
            return tuple(out)

        accs = lax.fori_loop(0, nt, pv, tuple(jnp.zeros((tq, HEAD_PAD), F32) for _ in heads),
                             unroll=True)
        o_ref[...] = jnp.concatenate([a[:, :MLA_V] / a[:, MLA_V:MLA_V + 1] for a in accs],
                                     axis=-1).astype(BF16)

    j = pl.program_id(2)

    @pl.when(j < ncb)
    def _():
        attend(n_ctx, _pick_tile(n_ctx, (256, 128)))

    @pl.when(j >= ncb)
    def _():
        attend(n_all, _pick_tile(n_all, ATTN_KEY_TILES))


def _attn_call(qm, km, vm, *, n_batch, nj, ncb, n_ctx):
    rows = qm.shape[0]
    tm = ROW_BLOCK
    t_all = nj * tm
    tk = _pick_tile(t_all, ATTN_KEY_TILES)
    qmap = lambda b, hp, j: (b * nj + j, hp)
    kmap = lambda b, hp, j: (b, hp)
    return pl.pallas_call(
        functools.partial(_attn_body, n_ctx=n_ctx, n_all=t_all, ncb=ncb),
        out_shape=jax.ShapeDtypeStruct((rows, MLA_O), BF16),
        grid=(n_batch, MLA_HEADS // 2, nj),
        in_specs=[pl.BlockSpec((tm, 2 * HEAD_PAD), qmap),
                  pl.BlockSpec((t_all, 2 * HEAD_PAD), kmap),
                  pl.BlockSpec((t_all, 2 * HEAD_PAD), kmap)],
        out_specs=pl.BlockSpec((tm, 2 * MLA_V), qmap),
        scratch_shapes=[pltpu.VMEM((2, t_all // tk, tm, tk), F32)],
        compiler_params=_params(3),
        name="mla_attention",
    )(qm, km, vm)


def _merge_body(yf_ref, yb_ref, rg_ref, cb_ref, cu_ref, cup_ref, cun_ref, om_ref, gt_ref, x_ref,
                g1_ref, a2_ref, b2_ref, gn_ref, cw_ref, wro_ref, wco_ref, wmo_ref, wout_ref,
                wrh_ref, wrl_ref, rb_ref, xmid_ref, h2_ref, rt_ref, lg_ref, *, nj, ncb, d_model, nb):
    tm = x_ref.shape[0]
    step = pl.program_id(0)

    @pl.when(step == 0)
    def _():
        lg_ref[...] = jnp.zeros_like(lg_ref)

    cls, w_lo, w_hi = _route_rows(jnp.transpose(lg_ref[...])[:N_EXPERTS, :], rb_ref[...])
    rank, counts = _block_ranks(cls.astype(F32))
    rt_ref[...] = jnp.concatenate(
        [cls.astype(F32), w_lo, w_hi, rank, counts, jnp.zeros((ROUTE_ROWS - 5, tm), F32)], axis=0)

    j = jnp.minimum(step, nb - 1) % nj
    seg_first = jnp.logical_or(j == 0, j == ncb)
    seg_last = jnp.logical_or(j == ncb - 1, j == nj - 1)

    y = yf_ref[...] + yb_ref[...]
    yn = jnp.concatenate([_rms(y[:, hd * RET_DV:(hd + 1) * RET_DV]) for hd in range(RET_HEADS)],
                         axis=-1) * gn_ref[...]
    y_ret = _dot((rg_ref[...].astype(F32) * yn).astype(BF16), wro_ref[...])

    u = cu_ref[...].astype(F32)
    ridx = lax.broadcasted_iota(jnp.int32, u.shape, 0)
    prev_row = cup_ref[...].astype(F32)[BF16_SUBLANES - 1:, :] * jnp.where(seg_first, 0.0, 1.0)
    next_row = cun_ref[...].astype(F32)[0:1, :] * jnp.where(seg_last, 0.0, 1.0)
    u_prev = jnp.where(ridx == 0, prev_row, pltpu.roll(u, 1, 0))
    u_next = jnp.where(ridx == tm - 1, next_row, pltpu.roll(u, tm - 1, 0))
    conv = u_prev * cw_ref[0:1, :] + u * cw_ref[1:2, :] + u_next * cw_ref[2:3, :]
    y_conv = _dot((cb_ref[...].astype(F32) * conv).astype(BF16), wco_ref[...])

    y_mla = _dot(om_ref[...], wmo_ref[...])

    merged = (gt_ref[:, 0:d_model].astype(F32) * y_ret
              + gt_ref[:, d_model:2 * d_model].astype(F32) * y_conv
              + gt_ref[:, 2 * d_model:3 * d_model].astype(F32) * y_mla)
    x_mid = x_ref[...] + g1_ref[...] * _dot(merged.astype(BF16), wout_ref[...])
    xmid_ref[...] = x_mid

    h2 = _rms(x_mid) * a2_ref[...] + b2_ref[...]
    h_hi, h_lo = _split_bf16(h2)
    lg_ref[...] = _dot(h_hi, wrh_ref[...]) + _dot(h_hi, wrl_ref[...]) + _dot(h_lo, wrh_ref[...])
    h2_ref[...] = h_hi


def _top2_of4(v):
    def first_max(rows):
        best, idx = rows[0], jnp.zeros(rows[0].shape, jnp.int32)
        for e in range(1, len(rows)):
            better = rows[e] > best
            idx = jnp.where(better, e, idx)
            best = jnp.where(better, rows[e], best)
        return best, idx

    b1, i1 = first_max(v)
    b2, i2 = first_max([jnp.where(i1 == e, -jnp.inf, v[e]) for e in range(len(v))])
    return i1, i2, b1, b2


def _route_rows(logits_t, bias):
    scores = jax.nn.sigmoid(logits_t)
    biased = scores + bias
    row = lambda a, e: a[e:e + 1, :]
    best = None
    for g in range(N_GROUPS):
        v = [row(biased, g * EXPERTS_PER_GROUP + e) for e in range(EXPERTS_PER_GROUP)]
        i1, i2, b1, b2 = _top2_of4(v)
        cand = (b1 + b2, jnp.full(i1.shape, g, jnp.int32), i1, i2)
        if best is None:
            best = cand
        else:
            better = cand[0] > best[0]
            best = tuple(jnp.where(better, c, o) for c, o in zip(cand, best))
    _, g_sel, i1, i2 = best
    lo = jnp.minimum(i1, i2)
    hi = jnp.maximum(i1, i2)
    e_lo = g_sel * EXPERTS_PER_GROUP + lo
    e_hi = g_sel * EXPERTS_PER_GROUP + hi
    s_lo = jnp.zeros_like(best[0])
    s_hi = jnp.zeros_like(best[0])
    for e in range(N_EXPERTS):
        s_lo = jnp.where(e_lo == e, row(scores, e), s_lo)
        s_hi = jnp.where(e_hi == e, row(scores, e), s_hi)
    total = s_lo + s_hi
    pair_base = jnp.where(lo == 0, 0, jnp.where(lo == 1, 3, 5))
    cls = g_sel * len(PAIRS) + pair_base + (hi - lo - 1)
    return cls, s_lo / total, s_hi / total


def _block_ranks(cls_row):
    n = cls_row.shape[1]
    cls_col = jnp.transpose(jnp.broadcast_to(cls_row, (LANES, n)))[:, 0:1]
    ii = lax.broadcasted_iota(jnp.int32, (n, n), 0)
    jj = lax.broadcasted_iota(jnp.int32, (n, n), 1)
    earlier_same = jnp.logical_and(cls_col == cls_row, ii < jj)
    rank = jnp.sum(jnp.where(earlier_same, 1.0, 0.0), axis=0, keepdims=True)
    lane = lax.broadcasted_iota(jnp.int32, (1, n), 1)
    counts = jnp.zeros((1, n), F32)
    for c in range(N_CLASSES):
        cnt = jnp.sum(jnp.where(cls_row == c, 1.0, 0.0), axis=1, keepdims=True)
        counts = jnp.where(lane == c, cnt, counts)
    return rank, counts


def _merge_call(yf, yb, rg, cb, cu, om, gt, xa, g1, a2, b2, wts, *, nj, ncb):
    rows, d = xa.shape
    tm = ROW_BLOCK
    nb = rows // tm
    halo = BF16_SUBLANES
    per_blk = tm // halo
    blk = lambda i: jnp.minimum(i, nb - 1)
    row = lambda i: (blk(i), 0)
    mod = lambda i: (2 * (blk(i) // nj) + ((blk(i) % nj) >= ncb).astype(jnp.int32), 0, 0)
    prev = lambda i: (jnp.maximum(blk(i) * per_blk - 1, 0), 0)
    nxt = lambda i: (jnp.minimum((blk(i) + 1) * per_blk, rows // halo - 1), 0)
    in_specs = [pl.BlockSpec((tm, RV_W), row), pl.BlockSpec((tm, RV_W), row),
                pl.BlockSpec((tm, RV_W), row),
                pl.BlockSpec((tm, CONV_WIDTH), row), pl.BlockSpec((tm, CONV_WIDTH), row),
                pl.BlockSpec((halo, CONV_WIDTH), prev), pl.BlockSpec((halo, CONV_WIDTH), nxt),
                pl.BlockSpec((tm, MLA_O), row), pl.BlockSpec((tm, N_BRANCH * d), row),
                pl.BlockSpec((tm, d), row),
                pl.BlockSpec((None, 1, d), mod), pl.BlockSpec((None, 1, d), mod),
                pl.BlockSpec((None, 1, d), mod)]
    in_specs += [_const_spec(w.shape) for w in wts]
    return pl.pallas_call(
        functools.partial(_merge_body, nj=nj, ncb=ncb, d_model=d, nb=nb),
        out_shape=[jax.ShapeDtypeStruct((rows, d), F32), jax.ShapeDtypeStruct((rows, d), BF16),
                   jax.ShapeDtypeStruct((nb * ROUTE_ROWS, tm), F32)],
        grid=(nb + 1,),
        in_specs=in_specs,
        out_specs=[pl.BlockSpec((tm, d), row), pl.BlockSpec((tm, d), row),
                   pl.BlockSpec((ROUTE_ROWS, tm), lambda i: (jnp.maximum(i - 1, 0), 0))],
        scratch_shapes=[pltpu.VMEM((tm, LANES), F32)],
        compiler_params=_params(1),
        name="merge_out_proj",
    )(yf, yb, rg, cb, cu, cu, cu, om, gt, xa, g1, a2, b2, *wts)


def _moe_body(ea_ref, eb_ref, nu_ref, h_ref, gw_ref, w1a_ref, w3a_ref, w2a_ref, w1b_ref, w3b_ref,
              w2b_ref, o_ref, w13a_s, w2a_s, w13b_s, w2b_s):
    t = pl.program_id(0)
    used = t < nu_ref[0]
    prev = jnp.maximum(t - 1, 0)

    def refresh(e_ref, w1_ref, w3_ref, w2_ref, w13_s, w2_s):
        @pl.when(jnp.logical_and(used, jnp.logical_or(t == 0, e_ref[t] != e_ref[prev])))
        def _():
            w13_s[:, :D_EXPERT] = w1_ref[...].astype(BF16)
            w13_s[:, D_EXPERT:] = w3_ref[...].astype(BF16)
            w2_s[...] = w2_ref[...].astype(BF16)

    refresh(ea_ref, w1a_ref, w3a_ref, w2a_ref, w13a_s, w2a_s)
    refresh(eb_ref, w1b_ref, w3b_ref, w2b_ref, w13b_s, w2b_s)

    @pl.when(used)
    def _():
        h = h_ref[...]
        gw = gw_ref[...]

        def expert(w13_s, w2_s, wt):
            a = _dot(h, w13_s[...])
            act = _silu(a[:, :D_EXPERT]) * a[:, D_EXPERT:] * wt
            return _dot(act.astype(BF16), w2_s[...])

        o_ref[...] = (expert(w13a_s, w2a_s, gw[:, 0:1])
                      + expert(w13b_s, w2b_s, gw[:, 1:2])).astype(BF16)

    @pl.when(jnp.logical_not(used))
    def _():
        o_ref[...] = jnp.zeros_like(o_ref)


def _moe_call(tile_ea, tile_eb, n_used, hs, gw, w1, w3, w2, layer):
    npad, d = hs.shape
    tmo = MOE_TILE
    row = lambda t, ea, eb, nu: (t, 0)
    wa = lambda t, ea, eb, nu: (layer, ea[t], 0, 0)
    wb = lambda t, ea, eb, nu: (layer, eb[t], 0, 0)
    up = lambda im: pl.BlockSpec((None, None, d, D_EXPERT), im)
    down = lambda im: pl.BlockSpec((None, None, D_EXPERT, d), im)
    grid_spec = pltpu.PrefetchScalarGridSpec(
        num_scalar_prefetch=3,
        grid=(npad // tmo,),
        in_specs=[pl.BlockSpec((tmo, d), row), pl.BlockSpec((tmo, 2), row),
                  up(wa), up(wa), down(wa), up(wb), up(wb), down(wb)],
        out_specs=pl.BlockSpec((tmo, d), row),
        scratch_shapes=[pltpu.VMEM((d, 2 * D_EXPERT), BF16), pltpu.VMEM((D_EXPERT, d), BF16)] * 2)
    return pl.pallas_call(
        _moe_body,
        out_shape=jax.ShapeDtypeStruct((npad, d), BF16),
        grid_spec=grid_spec,
        compiler_params=_params(1),
        name="moe_experts",
    )(tile_ea, tile_eb, n_used, hs, gw, w1, w3, w2, w1, w3, w2)


def _dispatch(route, n_tok):
    tmo = MOE_TILE
    n_tiles = n_tok // tmo + N_CLASSES
    npad = n_tiles * tmo
    cls = route[:, 0, :].astype(jnp.int32)
    rank = route[:, 3, :].astype(jnp.int32)
    counts = route[:, 4, :N_CLASSES].astype(jnp.int32)
    tiles_per = (jnp.sum(counts, axis=0) + tmo - 1) // tmo
    tile_end = jnp.cumsum(tiles_per)
    offs = (tile_end - tiles_per) * tmo
    block_base = offs[None, :] + jnp.cumsum(counts, axis=0) - counts
    classes = jnp.arange(N_CLASSES, dtype=jnp.int32)
    base = jnp.sum(jnp.where(cls[:, :, None] == classes, block_base[:, None, :], 0), axis=-1)
    dest = (base + rank).reshape(-1)
    record = jnp.stack([jnp.arange(n_tok, dtype=F32), route[:, 1, :].reshape(-1),
                        route[:, 2, :].reshape(-1), jnp.zeros((n_tok,), F32)], axis=1)
    slots = jnp.zeros((npad, record.shape[1]), F32).at[dest].set(
        record, mode="promise_in_bounds", unique_indices=True)
    src = slots[:, 0].astype(jnp.int32)
    gw = slots[:, 1:3]
    tile_ids = jnp.arange(n_tiles, dtype=jnp.int32)
    tile_cls = jnp.sum((tile_end[None, :] <= tile_ids[:, None]).astype(jnp.int32), axis=1)
    tile_cls = jnp.minimum(tile_cls, N_CLASSES - 1)
    pa = jnp.asarray([p[0] for p in PAIRS], jnp.int32)
    pb = jnp.asarray([p[1] for p in PAIRS], jnp.int32)
    group = (tile_cls // len(PAIRS)) * EXPERTS_PER_GROUP
    tile_ea = group + pa[tile_cls % len(PAIRS)]
    tile_eb = group + pb[tile_cls % len(PAIRS)]
    n_used = tile_end[-1:].astype(jnp.int32)
    return src, dest, gw, tile_ea, tile_eb, n_used


def _final_body(x_ref, f_ref, g2_ref, fn_ref, o_ref):
    o_ref[...] = _rms(x_ref[...] + g2_ref[...] * f_ref[...].astype(F32)) * fn_ref[...]


def _final_call(xmid, fg, g2, final_norm, *, n_batch, nj, ncb):
    rows, d = xmid.shape
    tm = ROW_BLOCK
    njl = nj - ncb
    src = lambda b, j: (b * nj + ncb + j, 0)
    return pl.pallas_call(
        _final_body,
        out_shape=jax.ShapeDtypeStruct((n_batch * njl * tm, d), F32),
        grid=(n_batch, njl),
        in_specs=[pl.BlockSpec((tm, d), src), pl.BlockSpec((tm, d), src),
                  pl.BlockSpec((None, 1, d), lambda b, j: (2 * b + 1, 0, 0)),
                  _const_spec((1, d))],
        out_specs=pl.BlockSpec((tm, d), lambda b, j: (b * njl + j, 0)),
        compiler_params=_params(2),
        name="final_norm",
    )(xmid, fg, g2, final_norm.reshape(1, d))


def _split_cols(w, sizes):
    out, off = [], 0
    for s in sizes:
        out.append(w[:, off:off + s])
        off += s
    out.append(w[:, off:])
    return out


def _layer_weights(w_in, w_uq, w_ukv, q_norm, kv_norm):
    wq, wk, wv, wg, wcb, wcc, wcx, wqd, wkvd, wkr, wgate = _split_cols(w_in, IN_SIZES[:-1])
    rope_lanes = lambda w: jnp.pad(w, ((0, 0), (MLA_NOPE, HEAD_PAD - MLA_NOPE - MLA_ROPE)))
    w_ext = jnp.concatenate(
        [wq, wk, wv, wg, wcb, wcc, wcx, wqd, wkvd, rope_lanes(wkr), wgate], axis=1).astype(BF16)

    uq = w_uq.reshape(MLA_Q_RANK, MLA_HEADS, MLA_NOPE + MLA_ROPE)
    tail = HEAD_PAD - MLA_NOPE - MLA_ROPE
    uq_pad = jnp.pad(uq, ((0, 0), (0, 0), (0, tail)))
    wuq = uq_pad.reshape(MLA_Q_RANK, MLA_W).astype(BF16)

    ukv = w_ukv.reshape(MLA_KV_RANK, MLA_HEADS, MLA_NOPE + MLA_V)
    wk_up = jnp.pad(ukv[..., :MLA_NOPE], ((0, 0), (0, 0), (0, HEAD_PAD - MLA_NOPE)))
    wk_up = wk_up.reshape(MLA_KV_RANK, MLA_W).astype(BF16)
    wv_pad = jnp.pad(ukv[..., MLA_NOPE:], ((0, 0), (0, 0), (0, HEAD_PAD - MLA_V)))
    wv_pad = wv_pad.reshape(MLA_KV_RANK, MLA_W).astype(BF16)
    ones_row = jnp.zeros((MLA_HEADS, HEAD_PAD), F32).at[:, MLA_V].set(1.0).reshape(1, MLA_W)
    q_gain = q_norm.astype(F32) * ((MLA_NOPE + MLA_ROPE) ** -0.5 * math.log2(math.e))
    return (w_ext, wuq, wk_up, wv_pad, q_gain.reshape(1, -1),
            kv_norm.reshape(1, -1).astype(F32), ones_row)


def _rotary_tables(n_ctx, n_lat):
    n_all = n_ctx + n_lat
    row = lax.broadcasted_iota(jnp.int32, (n_all, LANES), 0)
    lane = lax.broadcasted_iota(jnp.int32, (n_all, LANES), 1)
    is_ctx = row < n_ctx
    pos = row - n_ctx
    grid_row = pos // GRID_W
    grid_col = pos - grid_row * GRID_W

    def inv_freq(idx, half):
        return ROPE_BASE ** (-idx.astype(F32) / half)

    def table(ang, first_half, live):
        cos = jnp.where(is_ctx, 1.0, jnp.cos(ang))
        sin = jnp.where(is_ctx, 0.0, jnp.where(first_half, -jnp.sin(ang), jnp.sin(ang)))
        return jnp.where(live, cos, 0.0), jnp.where(live, sin, 0.0)

    half = RET_DK // 2
    ang = pos.astype(F32) * inv_freq(lane % half, half)
    cr, sr = table(ang, (lane % RET_DK) < half, True)

    quarter = MLA_ROPE // 4
    rl = lane - MLA_NOPE
    coord = jnp.where(rl < MLA_ROPE // 2, grid_row, grid_col)
    ang = coord.astype(F32) * inv_freq(rl % quarter, quarter)
    cm, sm = table(ang, (rl % (2 * quarter)) < quarter,
                   jnp.logical_and(rl >= 0, rl < MLA_ROPE))
    cm = jnp.where(rl < 0, 1.0, cm)
    return cr, sr, cm, sm


def _retention_consts(ret_decay):
    log_gf = jax.nn.log_sigmoid(ret_decay[0].astype(F32))
    log_gb = jax.nn.log_sigmoid(ret_decay[1].astype(F32))
    idx = jnp.arange(RET_CHUNK, dtype=F32)
    rel = idx[:, None] - idx[None, :]
    dm_f = jnp.where(rel >= 0, jnp.exp(log_gf[:, None, None] * jnp.maximum(rel, 0.0)[None]), 0.0)
    dm_b = jnp.where(rel < 0, jnp.exp(log_gb[:, None, None] * jnp.maximum(-rel, 0.0)[None]), 0.0)

    def lanes(t):
        return jnp.repeat(t, RET_DK, axis=1)

    xi_f = lanes(jnp.exp(log_gf[None, :] * (idx + 1.0)[:, None]))
    zt_f = lanes(jnp.exp(log_gf[None, :] * (RET_CHUNK - 1 - idx)[:, None]))
    xi_b = lanes(jnp.exp(log_gb[None, :] * (RET_CHUNK - idx)[:, None]))
    zt_b = lanes(jnp.exp(log_gb[None, :] * idx[:, None]))
    cd = lambda lg: jnp.broadcast_to(jnp.exp(lg * RET_CHUNK)[:, None, None], (RET_HEADS, 1, RET_DV))
    return dm_f, dm_b, xi_f, zt_f, xi_b, zt_b, cd(log_gf), cd(log_gb)


def kernel(x, c, ctx, c_ctx, w_ada, b_ada, norm1, norm2, w_in, ret_decay, ret_gn, w_ret_o, conv_w,
           w_conv_o, mla_q_norm, w_uq, mla_kv_norm, w_ukv, w_mla_o, w_out, w_router, router_bias,
           w1, w3, w2, final_norm):
    n_batch, n_lat, d = x.shape
    n_ctx = ctx.shape[1]
    depth = w_ada.shape[0]
    t_all = n_ctx + n_lat
    assert n_ctx % ROW_BLOCK == 0 and n_lat % ROW_BLOCK == 0 and n_lat % GRID_W == 0
    nj = t_all // ROW_BLOCK
    ncb = n_ctx // ROW_BLOCK
    n_tok = n_batch * t_all
    assert n_tok % MOE_TILE == 0
    geom = dict(n_batch=n_batch, nj=nj, ncb=ncb)

    cc = jnp.concatenate([c, c_ctx[None, :]], axis=0)
    cc = jnp.pad(cc, ((0, -cc.shape[0] % 8), (0, 0)))
    mod = _ada_call(cc, w_ada, b_ada)[:, :n_batch + 1].reshape(depth, n_batch + 1, N_MOD, d)
    pick = jnp.stack([jnp.full((n_batch,), n_batch, jnp.int32),
                      jnp.arange(n_batch, dtype=jnp.int32)], axis=1).reshape(-1)
    mod = mod[:, pick]

    tabs = _rotary_tables(n_ctx, n_lat)
    wr_hi, wr_lo = _split_bf16(jnp.pad(w_router.astype(F32), ((0, 0), (0, LANES - N_EXPERTS))))
    rbias = router_bias.astype(F32).reshape(N_EXPERTS, 1)

    out = None
    x_parts = (ctx.reshape(n_batch * n_ctx, d), x.reshape(n_batch * n_lat, d))
    for l in range(depth):
        m = mod[l]
        rowvec = lambda v: v.reshape(2 * n_batch, 1, d)
        a1 = rowvec(norm1[l][None, :] * (1.0 + m[:, 1]))
        b1 = rowvec(m[:, 0])
        g1 = rowvec(m[:, 2])
        a2 = rowvec(norm2[l][None, :] * (1.0 + m[:, 4]))
        b2 = rowvec(m[:, 3])
        g2 = rowvec(m[:, 5])

        wts = _layer_weights(w_in[l], w_uq[l], w_ukv[l], mla_q_norm[l], mla_kv_norm[l])
        proj = _inproj_call(x_parts, a1, b1, tabs, wts, **geom)
        rq, rk, rv, rg, cb, cu, qm, km, vm, gt, xa = proj

        yf, yb = _ret_call(rq, rk, rv, _retention_consts(ret_decay[l]), **geom)
        om = _attn_call(qm, km, vm, n_ctx=n_ctx, **geom)

        merge_wts = (ret_gn[l].reshape(1, -1).astype(F32), conv_w[l].T.astype(F32),
                     w_ret_o[l].astype(BF16), w_conv_o[l].astype(BF16), w_mla_o[l].astype(BF16),
                     w_out[l].astype(BF16), wr_hi, wr_lo, rbias)
        xmid, h2, route = _merge_call(yf, yb, rg, cb, cu, om, gt, xa, g1, a2, b2, merge_wts,
                                      nj=nj, ncb=ncb)
        route = route.reshape(-1, ROUTE_ROWS, ROW_BLOCK)
        src, dest, gw, tile_ea, tile_eb, n_used = _dispatch(route, n_tok)
        hs = h2.at[src].get(mode="promise_in_bounds")
        f_sorted = _moe_call(tile_ea, tile_eb, n_used, hs, gw, w1, w3, w2, l)
        fg = f_sorted.at[dest].get(mode="promise_in_bounds")

        if l < depth - 1:
            x_parts = (xmid, fg, g2)
        else:
            out = _final_call(xmid, fg, g2, final_norm, **geom)
    return out.reshape(n_batch, n_lat, d)
```

```python
import functools
import math

import jax
import jax.numpy as jnp
from jax import lax
from jax.experimental import pallas as pl
from jax.experimental.pallas import tpu as pltpu
from jax.experimental.pallas import tpu_sc as plsc

F32 = jnp.float32
BF16 = jnp.bfloat16

GRID_W = 64
RMS_EPS = 1e-6
ROPE_BASE = 10000.0
N_MOD = 6
RET_HEADS = 4
RET_DK = 64
RET_DV = 128
RET_CHUNK = 256
CONV_WIDTH = 512
MLA_HEADS = 8
MLA_Q_RANK = 384
MLA_KV_RANK = 256
MLA_NOPE = 64
MLA_ROPE = 32
MLA_V = 64
N_BRANCH = 3
N_EXPERTS = 16
N_GROUPS = 4
EXPERTS_PER_GROUP = N_EXPERTS // N_GROUPS
D_EXPERT = 512
IN_SIZES = (RET_HEADS * RET_DK, RET_HEADS * RET_DK, RET_HEADS * RET_DV, RET_HEADS * RET_DV,
            CONV_WIDTH, CONV_WIDTH, CONV_WIDTH, MLA_Q_RANK, MLA_KV_RANK, MLA_ROPE, 0)

LANES = 128
BF16_SUBLANES = 16
VMEM_LIMIT = 56 * 1024 * 1024
SC_CORES = 2
SC_SUBCORES = 16
SC_WORKERS = SC_CORES * SC_SUBCORES

HEAD_PAD = LANES
ROW_BLOCK = 256
MOE_TILE = 256
ATTN_KEY_TILES = (768, 512, 256, 128)
ROUTE_ROWS = 8
GATHER_CHUNK = 64
PAIRS = [(a, b) for a in range(EXPERTS_PER_GROUP) for b in range(a + 1, EXPERTS_PER_GROUP)]
N_CLASSES = N_GROUPS * len(PAIRS)

RQ_W = RET_HEADS * RET_DK
RV_W = RET_HEADS * RET_DV
MLA_W = MLA_HEADS * HEAD_PAD
MLA_O = MLA_HEADS * MLA_V


def _const_spec(shape):
    nd = len(shape)
    return pl.BlockSpec(shape, lambda *_: (0,) * nd, pipeline_mode=pl.Buffered(1))


def _params(n_axes):
    return pltpu.CompilerParams(dimension_semantics=("arbitrary",) * n_axes,
                                vmem_limit_bytes=VMEM_LIMIT)


def _dot(a, b):
    return jnp.dot(a, b, preferred_element_type=F32)


def _split_bf16(a):
    hi = a.astype(BF16)
    lo = (a - hi.astype(F32)).astype(BF16)
    return hi, lo


def _silu(v):
    return v * jax.nn.sigmoid(v)


def _rms(v):
    return v * lax.rsqrt(jnp.mean(v * v, axis=-1, keepdims=True) + RMS_EPS)


def _pack_pairs(v):
    half = v.shape[1] // 2
    bits = lambda t: lax.bitcast_convert_type(t.astype(BF16).astype(F32), jnp.uint32)
    return (bits(v[:, :half]) & jnp.uint32(0xFFFF0000)) | (bits(v[:, half:]) >> 16)


def _unpack_pairs(u):
    hi = lax.bitcast_convert_type(u & jnp.uint32(0xFFFF0000), F32)
    lo = lax.bitcast_convert_type(u << 16, F32)
    return jnp.concatenate([hi, lo], axis=1)


def _ada_body(c_ref, w_ref, b_ref, o_ref):
    a_hi, a_lo = _split_bf16(_silu(c_ref[...]))
    w_hi, w_lo = _split_bf16(w_ref[...])
    o_ref[...] = _dot(a_hi, w_hi) + _dot(a_hi, w_lo) + _dot(a_lo, w_hi) + b_ref[...]


def _ada_call(cc, w_ada, b_ada):
    depth, d, nm = w_ada.shape
    rows = cc.shape[0]
    cb = nm // 4
    return pl.pallas_call(
        _ada_body,
        out_shape=jax.ShapeDtypeStruct((depth, rows, nm), F32),
        grid=(depth, nm // cb),
        in_specs=[pl.BlockSpec((rows, d), lambda l, n: (0, 0)),
                  pl.BlockSpec((None, d, cb), lambda l, n: (l, 0, n)),
                  pl.BlockSpec((None, 1, cb), lambda l, n: (l, 0, n))],
        out_specs=pl.BlockSpec((None, rows, cb), lambda l, n: (l, 0, n)),
        compiler_params=_params(2),
        name="ada_mod",
    )(cc, w_ada, b_ada.reshape(depth, 1, nm))


_O_RQ = 0
_O_RK = _O_RQ + RQ_W
_O_RV = _O_RK + RQ_W
_O_RG = _O_RV + RV_W
_O_CB = _O_RG + RV_W
_O_CC = _O_CB + CONV_WIDTH
_O_CX = _O_CC + CONV_WIDTH
_O_QD = _O_CX + CONV_WIDTH
_O_KVD = _O_QD + MLA_Q_RANK
_O_KR = _O_KVD + MLA_KV_RANK
_O_GT = _O_KR + LANES


def _rot_half(v, half):
    width = v.shape[1]
    lane = lax.broadcasted_iota(jnp.int32, v.shape, 1)
    first = (lane % (2 * half)) < half
    return jnp.where(first, pltpu.roll(v, width - half, 1), pltpu.roll(v, half, 1))


def _inproj_body(*refs, d_model, fused, ncb):
    xo_ref = refs[-1]
    if fused:
        xm_ref, f_ref, g2_ref = refs[:3]
        refs = refs[3:]
        x = xm_ref[...] + g2_ref[...] * _unpack_pairs(f_ref[...])
    else:
        ctx_ref, lat_ref = refs[:2]
        refs = refs[2:]
        x = jnp.where(pl.program_id(0) < ncb, ctx_ref[...], lat_ref[...])
    xo_ref[...] = x
    (a1_ref, b1_ref, cr_ref, sr_ref, cm_ref, sm_ref, w_ref, wuq_ref, wk_ref, wv_ref, qn_ref, kvn_ref,
     ones_ref, rq_ref, rk_ref, rv_ref, rg_ref, cb_ref, cu_ref, qm_ref, km_ref, vm_ref, gt_ref) = refs[:23]
    h = (_rms(x) * a1_ref[...] + b1_ref[...]).astype(BF16)

    def mm(off, width):
        return _dot(h, w_ref[:, off:off + width])

    cr = jnp.concatenate([cr_ref[...]] * (RQ_W // cr_ref.shape[1]), axis=1)
    sr = jnp.concatenate([sr_ref[...]] * (RQ_W // sr_ref.shape[1]), axis=1)
    q = mm(_O_RQ, RQ_W)
    rq_ref[...] = (q * cr + _rot_half(q, RET_DK // 2) * sr).astype(BF16)
    k = mm(_O_RK, RQ_W)
    rk_ref[...] = ((k * cr + _rot_half(k, RET_DK // 2) * sr) * (RET_DK ** -0.5)).astype(BF16)
    rv_ref[...] = mm(_O_RV, RV_W).astype(BF16)
    rg_ref[...] = _silu(mm(_O_RG, RV_W)).astype(BF16)
    cb_ref[...] = mm(_O_CB, CONV_WIDTH).astype(BF16)
    cu_ref[...] = (mm(_O_CC, CONV_WIDTH) * mm(_O_CX, CONV_WIDTH)).astype(BF16)

    cm = cm_ref[...]
    sm = sm_ref[...]
    qn = (_rms(mm(_O_QD, MLA_Q_RANK)) * qn_ref[...]).astype(BF16)
    for hd in range(MLA_HEADS):
        lo = hd * HEAD_PAD
        qa = _dot(qn, wuq_ref[:, lo:lo + HEAD_PAD])
        qm_ref[:, lo:lo + HEAD_PAD] = (qa * cm + _rot_half(qa, MLA_ROPE // 4) * sm).astype(BF16)

    kvn = (_rms(mm(_O_KVD, MLA_KV_RANK)) * kvn_ref[...]).astype(BF16)
    kr = mm(_O_KR, HEAD_PAD)
    kr = kr * cm + _rot_half(kr, MLA_ROPE // 4) * sm
    kn = _dot(kvn, wk_ref[...])
    for hd in range(MLA_HEADS):
        lo = hd * HEAD_PAD
        km_ref[:, lo:lo + HEAD_PAD] = (kn[:, lo:lo + HEAD_PAD] + kr).astype(BF16)
    vm_ref[...] = (_dot(kvn, wv_ref[...]) + ones_ref[...]).astype(BF16)

    for br in range(N_BRANCH):
        gt_ref[:, br * d_model:(br + 1) * d_model] = jax.nn.sigmoid(
            mm(_O_GT + br * d_model, d_model)).astype(BF16)


def _inproj_call(x_parts, a1, b1, tabs, wts, *, n_batch, nj, ncb):
    fused = len(x_parts) == 3
    d = x_parts[0].shape[1]
    rows = n_batch * nj * ROW_BLOCK
    tm = ROW_BLOCK
    row = lambda j, b: (b * nj + j, 0)
    mod = lambda j, b: (2 * b + (j >= ncb).astype(jnp.int32), 0, 0)
    tab = lambda j, b: (j, 0)
    if fused:
        in_specs = [pl.BlockSpec((tm, d), row), pl.BlockSpec((tm, d // 2), row),
                    pl.BlockSpec((None, 1, d), mod)]
    else:
        in_specs = [pl.BlockSpec((tm, d), lambda j, b: (jnp.where(j < ncb, b * ncb + j, 0), 0)),
                    pl.BlockSpec((tm, d),
                                 lambda j, b: (jnp.where(j < ncb, 0, b * (nj - ncb) + j - ncb), 0))]
    in_specs += [pl.BlockSpec((None, 1, d), mod), pl.BlockSpec((None, 1, d), mod)]
    in_specs += [pl.BlockSpec((tm, t.shape[1]), tab) for t in tabs]
    in_specs += [_const_spec(w.shape) for w in wts]
    widths = (RQ_W, RQ_W, RV_W, RV_W, CONV_WIDTH, CONV_WIDTH, MLA_W, MLA_W, MLA_W, N_BRANCH * d)
    out_shape = [jax.ShapeDtypeStruct((rows, w), BF16) for w in widths]
    out_specs = [pl.BlockSpec((tm, w), row) for w in widths]
    out_shape.append(jax.ShapeDtypeStruct((rows, d), F32))
    out_specs.append(pl.BlockSpec((tm, d), row))
    return pl.pallas_call(
        functools.partial(_inproj_body, d_model=d, fused=fused, ncb=ncb),
        out_shape=out_shape,
        grid=(nj, n_batch),
        in_specs=in_specs,
        out_specs=out_specs,
        compiler_params=_params(2),
        name="in_proj",
    )(*x_parts, a1, b1, *tabs, *wts)


def _ret_direction(q_ref, k_ref, v_ref, y_ref, s_ref, dm_ref, xi_ref, zt_ref, cd_ref, chunk_order):
    states = [s_ref[hd] for hd in range(RET_HEADS)]
    for c in chunk_order:
        rows = slice(c * RET_CHUNK, (c + 1) * RET_CHUNK)
        q = q_ref[rows, :]
        k = k_ref[rows, :]
        v = v_ref[rows, :]
        qx = (q.astype(F32) * xi_ref[...]).astype(BF16)
        kz = (k.astype(F32) * zt_ref[...]).astype(BF16)
        for hd in range(RET_HEADS):
            ks = slice(hd * RET_DK, (hd + 1) * RET_DK)
            vs = slice(hd * RET_DV, (hd + 1) * RET_DV)
            vh = v[:, vs]
            sc = lax.dot_general(q[:, ks], k[:, ks], (((1,), (1,)), ((), ())),
                                 preferred_element_type=F32)
            inner = _dot((sc * dm_ref[hd]).astype(BF16), vh)
            cross = _dot(qx[:, ks], states[hd].astype(BF16))
            y_ref[rows, vs] = inner + cross
            upd = lax.dot_general(kz[:, ks], vh, (((0,), (0,)), ((), ())),
                                  preferred_element_type=F32)
            states[hd] = cd_ref[hd] * states[hd] + upd
    for hd in range(RET_HEADS):
        s_ref[hd] = states[hd]


def _ret_body(qf_ref, kf_ref, vf_ref, qb_ref, kb_ref, vb_ref,
              dmf_ref, dmb_ref, xif_ref, ztf_ref, xib_ref, ztb_ref, cdf_ref, cdb_ref,
              yf_ref, yb_ref, sf_ref, sb_ref):
    @pl.when(pl.program_id(1) == 0)
    def _():
        sf_ref[...] = jnp.zeros_like(sf_ref)
        sb_ref[...] = jnp.zeros_like(sb_ref)

    n_chunks = ROW_BLOCK // RET_CHUNK
    _ret_direction(qf_ref, kf_ref, vf_ref, yf_ref, sf_ref, dmf_ref, xif_ref, ztf_ref, cdf_ref,
                   range(n_chunks))
    _ret_direction(qb_ref, kb_ref, vb_ref, yb_ref, sb_ref, dmb_ref, xib_ref, ztb_ref, cdb_ref,
                   range(n_chunks - 1, -1, -1))


def _ret_call(rq, rk, rv, consts, *, n_batch, nj, ncb):
    rows = rq.shape[0]
    tm = ROW_BLOCK
    fwd = lambda b, s: (b * nj + s, 0)

    def bwd(b, s):
        return (b * nj + jnp.where(s < ncb, ncb - 1 - s, nj - 1 - (s - ncb)), 0)

    specs = []
    for im in (fwd, bwd):
        specs += [pl.BlockSpec((tm, RQ_W), im), pl.BlockSpec((tm, RQ_W), im),
                  pl.BlockSpec((tm, RV_W), im)]
    specs += [_const_spec(c.shape) for c in consts]
    return pl.pallas_call(
        _ret_body,
        out_shape=[jax.ShapeDtypeStruct((rows, RV_W), F32)] * 2,
        grid=(n_batch, nj),
        in_specs=specs,
        out_specs=[pl.BlockSpec((tm, RV_W), fwd), pl.BlockSpec((tm, RV_W), bwd)],
        scratch_shapes=[pltpu.VMEM((RET_HEADS, RET_DK, RET_DV), F32)] * 2,
        compiler_params=_params(2),
        name="retention",
    )(rq, rk, rv, rq, rk, rv, *consts)


def _pick_tile(n, candidates):
    for c in candidates:
        if n % c == 0:
            return c
    raise ValueError(f"no tile for {n}")


def _attn_body(q_ref, k_ref, v_ref, o_ref, s_ref, *, n_ctx, n_all, ncb):
    tq = q_ref.shape[0]
    heads = [slice(hh * HEAD_PAD, (hh + 1) * HEAD_PAD) for hh in range(2)]

    def attend(n_keys, tk):
        nt = n_keys // tk
        qs = [q_ref[:, hs] for hs in heads]

        def qk(t, mrun):
            r0 = pl.multiple_of(t * tk, tk)
            out = []
            for hh, hs in enumerate(heads):
                s = lax.dot_general(qs[hh], k_ref[pl.ds(r0, tk), hs], (((1,), (1,)), ((), ())),
                                    preferred_element_type=F32)
                s_ref[hh, t, :, 0:tk] = s
                m = mrun[hh]
                for cc in range(tk // LANES):
                    m = jnp.maximum(m, s[:, cc * LANES:(cc + 1) * LANES])
                out.append(m)
            return tuple(out)

        mrun = lax.fori_loop(0, nt, qk, tuple(jnp.full((tq, LANES), -jnp.inf, F32) for _ in heads),
                             unroll=True)
        mrow = [jnp.max(m, axis=-1, keepdims=True) for m in mrun]

        def pv(t, accs):
            r0 = pl.multiple_of(t * tk, tk)
            out = []
            for hh, hs in enumerate(heads):
                p = jnp.exp2(s_ref[hh, t, :, 0:tk] - mrow[hh]).astype(BF16)
                out.append(accs[hh] + _dot(p, v_ref[pl.ds(r0, tk), hs]))
            return tuple(out)

        accs = lax.fori_loop(0, nt, pv, tuple(jnp.zeros((tq, HEAD_PAD), F32) for _ in heads),
                             unroll=True)
        o_ref[...] = jnp.concatenate([a[:, :MLA_V] / a[:, MLA_V:MLA_V + 1] for a in accs],
                                     axis=-1).astype(BF16)

    j = pl.program_id(2)

    @pl.when(j < ncb)
    def _():
        attend(n_ctx, _pick_tile(n_ctx, (256, 128)))

    @pl.when(j >= ncb)
    def _():
        attend(n_all, _pick_tile(n_all, ATTN_KEY_TILES))


def _attn_call(qm, km, vm, *, n_batch, nj, ncb, n_ctx):
    rows = qm.shape[0]
    tm = ROW_BLOCK
    t_all = nj * tm
    tk = _pick_tile(t_all, ATTN_KEY_TILES)
    qmap = lambda b, hp, j: (b * nj + j, hp)
    kmap = lambda b, hp, j: (b, hp)
    return pl.pallas_call(
        functools.partial(_attn_body, n_ctx=n_ctx, n_all=t_all, ncb=ncb),
        out_shape=jax.ShapeDtypeStruct((rows, MLA_O), BF16),
        grid=(n_batch, MLA_HEADS // 2, nj),
        in_specs=[pl.BlockSpec((tm, 2 * HEAD_PAD), qmap),
                  pl.BlockSpec((t_all, 2 * HEAD_PAD), kmap),
                  pl.BlockSpec((t_all, 2 * HEAD_PAD), kmap)],
        out_specs=pl.BlockSpec((tm, 2 * MLA_V), qmap),
        scratch_shapes=[pltpu.VMEM((2, t_all // tk, tm, tk), F32)],
        compiler_params=_params(3),
        name="mla_attention",
    )(qm, km, vm)


def _merge_body(yf_ref, yb_ref, rg_ref, cb_ref, cu_ref, cup_ref, cun_ref, om_ref, gt_ref, x_ref,
                g1_ref, a2_ref, b2_ref, gn_ref, cw_ref, wro_ref, wco_ref, wmo_ref, wout_ref,
                wrh_ref, wrl_ref, rb_ref, xmid_ref, h2_ref, rt_ref, lg_ref, *, nj, ncb, d_model, nb):
    tm = x_ref.shape[0]
    step = pl.program_id(0)

    @pl.when(step == 0)
    def _():
        lg_ref[...] = jnp.zeros_like(lg_ref)

    cls, w_lo, w_hi = _route_rows(jnp.transpose(lg_ref[...])[:N_EXPERTS, :], rb_ref[...])
    rank, counts = _block_ranks(cls.astype(F32))
    rt_ref[...] = jnp.concatenate(
        [cls.astype(F32), w_lo, w_hi, rank, counts, jnp.zeros((ROUTE_ROWS - 5, tm), F32)], axis=0)

    j = jnp.minimum(step, nb - 1) % nj
    seg_first = jnp.logical_or(j == 0, j == ncb)
    seg_last = jnp.logical_or(j == ncb - 1, j == nj - 1)

    y = yf_ref[...] + yb_ref[...]
    yn = jnp.concatenate([_rms(y[:, hd * RET_DV:(hd + 1) * RET_DV]) for hd in range(RET_HEADS)],
                         axis=-1) * gn_ref[...]
    y_ret = _dot((rg_ref[...].astype(F32) * yn).astype(BF16), wro_ref[...])

    u = cu_ref[...].astype(F32)
    ridx = lax.broadcasted_iota(jnp.int32, u.shape, 0)
    prev_row = cup_ref[...].astype(F32)[BF16_SUBLANES - 1:, :] * jnp.where(seg_first, 0.0, 1.0)
    next_row = cun_ref[...].astype(F32)[0:1, :] * jnp.where(seg_last, 0.0, 1.0)
    u_prev = jnp.where(ridx == 0, prev_row, pltpu.roll(u, 1, 0))
    u_next = jnp.where(ridx == tm - 1, next_row, pltpu.roll(u, tm - 1, 0))
    conv = u_prev * cw_ref[0:1, :] + u * cw_ref[1:2, :] + u_next * cw_ref[2:3, :]
    y_conv = _dot((cb_ref[...].astype(F32) * conv).astype(BF16), wco_ref[...])

    y_mla = _dot(om_ref[...], wmo_ref[...])

    merged = (gt_ref[:, 0:d_model].astype(F32) * y_ret
              + gt_ref[:, d_model:2 * d_model].astype(F32) * y_conv
              + gt_ref[:, 2 * d_model:3 * d_model].astype(F32) * y_mla)
    x_mid = x_ref[...] + g1_ref[...] * _dot(merged.astype(BF16), wout_ref[...])
    xmid_ref[...] = x_mid

    h2 = _rms(x_mid) * a2_ref[...] + b2_ref[...]
    h_hi, h_lo = _split_bf16(h2)
    lg_ref[...] = _dot(h_hi, wrh_ref[...]) + _dot(h_hi, wrl_ref[...]) + _dot(h_lo, wrh_ref[...])
    h2_ref[...] = _pack_pairs(h_hi)


def _top2_of4(v):
    def first_max(rows):
        best, idx = rows[0], jnp.zeros(rows[0].shape, jnp.int32)
        for e in range(1, len(rows)):
            better = rows[e] > best
            idx = jnp.where(better, e, idx)
            best = jnp.where(better, rows[e], best)
        return best, idx

    b1, i1 = first_max(v)
    b2, i2 = first_max([jnp.where(i1 == e, -jnp.inf, v[e]) for e in range(len(v))])
    return i1, i2, b1, b2


def _route_rows(logits_t, bias):
    scores = jax.nn.sigmoid(logits_t)
    biased = scores + bias
    row = lambda a, e: a[e:e + 1, :]
    best = None
    for g in range(N_GROUPS):
        v = [row(biased, g * EXPERTS_PER_GROUP + e) for e in range(EXPERTS_PER_GROUP)]
        i1, i2, b1, b2 = _top2_of4(v)
        cand = (b1 + b2, jnp.full(i1.shape, g, jnp.int32), i1, i2)
        if best is None:
            best = cand
        else:
            better = cand[0] > best[0]
            best = tuple(jnp.where(better, c, o) for c, o in zip(cand, best))
    _, g_sel, i1, i2 = best
    lo = jnp.minimum(i1, i2)
    hi = jnp.maximum(i1, i2)
    e_lo = g_sel * EXPERTS_PER_GROUP + lo
    e_hi = g_sel * EXPERTS_PER_GROUP + hi
    s_lo = jnp.zeros_like(best[0])
    s_hi = jnp.zeros_like(best[0])
    for e in range(N_EXPERTS):
        s_lo = jnp.where(e_lo == e, row(scores, e), s_lo)
        s_hi = jnp.where(e_hi == e, row(scores, e), s_hi)
    total = s_lo + s_hi
    pair_base = jnp.where(lo == 0, 0, jnp.where(lo == 1, 3, 5))
    cls = g_sel * len(PAIRS) + pair_base + (hi - lo - 1)
    return cls, s_lo / total, s_hi / total


def _block_ranks(cls_row):
    n = cls_row.shape[1]
    cls_col = jnp.transpose(jnp.broadcast_to(cls_row, (LANES, n)))[:, 0:1]
    ii = lax.broadcasted_iota(jnp.int32, (n, n), 0)
    jj = lax.broadcasted_iota(jnp.int32, (n, n), 1)
    earlier_same = jnp.logical_and(cls_col == cls_row, ii < jj)
    rank = jnp.sum(jnp.where(earlier_same, 1.0, 0.0), axis=0, keepdims=True)
    lane = lax.broadcasted_iota(jnp.int32, (1, n), 1)
    counts = jnp.zeros((1, n), F32)
    for c in range(N_CLASSES):
        cnt = jnp.sum(jnp.where(cls_row == c, 1.0, 0.0), axis=1, keepdims=True)
        counts = jnp.where(lane == c, cnt, counts)
    return rank, counts


def _merge_call(yf, yb, rg, cb, cu, om, gt, xa, g1, a2, b2, wts, *, nj, ncb):
    rows, d = xa.shape
    tm = ROW_BLOCK
    nb = rows // tm
    halo = BF16_SUBLANES
    per_blk = tm // halo
    blk = lambda i: jnp.minimum(i, nb - 1)
    row = lambda i: (blk(i), 0)
    mod = lambda i: (2 * (blk(i) // nj) + ((blk(i) % nj) >= ncb).astype(jnp.int32), 0, 0)
    prev = lambda i: (jnp.maximum(blk(i) * per_blk - 1, 0), 0)
    nxt = lambda i: (jnp.minimum((blk(i) + 1) * per_blk, rows // halo - 1), 0)
    in_specs = [pl.BlockSpec((tm, RV_W), row), pl.BlockSpec((tm, RV_W), row),
                pl.BlockSpec((tm, RV_W), row),
                pl.BlockSpec((tm, CONV_WIDTH), row), pl.BlockSpec((tm, CONV_WIDTH), row),
                pl.BlockSpec((halo, CONV_WIDTH), prev), pl.BlockSpec((halo, CONV_WIDTH), nxt),
                pl.BlockSpec((tm, MLA_O), row), pl.BlockSpec((tm, N_BRANCH * d), row),
                pl.BlockSpec((tm, d), row),
                pl.BlockSpec((None, 1, d), mod), pl.BlockSpec((None, 1, d), mod),
                pl.BlockSpec((None, 1, d), mod)]
    in_specs += [_const_spec(w.shape) for w in wts]
    return pl.pallas_call(
        functools.partial(_merge_body, nj=nj, ncb=ncb, d_model=d, nb=nb),
        out_shape=[jax.ShapeDtypeStruct((rows, d), F32), jax.ShapeDtypeStruct((rows, d // 2), jnp.uint32),
                   jax.ShapeDtypeStruct((nb * ROUTE_ROWS, tm), F32)],
        grid=(nb + 1,),
        in_specs=in_specs,
        out_specs=[pl.BlockSpec((tm, d), row), pl.BlockSpec((tm, d // 2), row),
                   pl.BlockSpec((ROUTE_ROWS, tm), lambda i: (jnp.maximum(i - 1, 0), 0))],
        scratch_shapes=[pltpu.VMEM((tm, LANES), F32)],
        compiler_params=_params(1),
        name="merge_out_proj",
    )(yf, yb, rg, cb, cu, cu, cu, om, gt, xa, g1, a2, b2, *wts)


def _moe_body(ea_ref, eb_ref, nu_ref, h_ref, gw_ref, w1a_ref, w3a_ref, w2a_ref, w1b_ref, w3b_ref,
              w2b_ref, o_ref, w13a_s, w2a_s, w13b_s, w2b_s):
    t = pl.program_id(0)
    used = t < nu_ref[0]
    prev = jnp.maximum(t - 1, 0)

    def refresh(e_ref, w1_ref, w3_ref, w2_ref, w13_s, w2_s):
        @pl.when(jnp.logical_and(used, jnp.logical_or(t == 0, e_ref[t] != e_ref[prev])))
        def _():
            w13_s[:, :D_EXPERT] = w1_ref[...].astype(BF16)
            w13_s[:, D_EXPERT:] = w3_ref[...].astype(BF16)
            w2_s[...] = w2_ref[...].astype(BF16)

    refresh(ea_ref, w1a_ref, w3a_ref, w2a_ref, w13a_s, w2a_s)
    refresh(eb_ref, w1b_ref, w3b_ref, w2b_ref, w13b_s, w2b_s)

    @pl.when(used)
    def _():
        h = _unpack_pairs(h_ref[...]).astype(BF16)
        gw = gw_ref[...]

        def expert(w13_s, w2_s, wt):
            a = _dot(h, w13_s[...])
            act = _silu(a[:, :D_EXPERT]) * a[:, D_EXPERT:] * wt
            return _dot(act.astype(BF16), w2_s[...])

        o_ref[...] = _pack_pairs(expert(w13a_s, w2a_s, gw[:, 0:1])
                                 + expert(w13b_s, w2b_s, gw[:, 1:2]))

    @pl.when(jnp.logical_not(used))
    def _():
        o_ref[...] = jnp.zeros_like(o_ref)


def _moe_call(tile_ea, tile_eb, n_used, hs, gw, w1, w3, w2, layer):
    npad = hs.shape[0]
    d = w1.shape[2]
    tmo = MOE_TILE
    row = lambda t, ea, eb, nu: (t, 0)
    wa = lambda t, ea, eb, nu: (layer, ea[t], 0, 0)
    wb = lambda t, ea, eb, nu: (layer, eb[t], 0, 0)
    up = lambda im: pl.BlockSpec((None, None, d, D_EXPERT), im)
    down = lambda im: pl.BlockSpec((None, None, D_EXPERT, d), im)
    grid_spec = pltpu.PrefetchScalarGridSpec(
        num_scalar_prefetch=3,
        grid=(npad // tmo,),
        in_specs=[pl.BlockSpec((tmo, d // 2), row), pl.BlockSpec((tmo, 2), row),
                  up(wa), up(wa), down(wa), up(wb), up(wb), down(wb)],
        out_specs=pl.BlockSpec((tmo, d // 2), row),
        scratch_shapes=[pltpu.VMEM((d, 2 * D_EXPERT), BF16), pltpu.VMEM((D_EXPERT, d), BF16)] * 2)
    return pl.pallas_call(
        _moe_body,
        out_shape=jax.ShapeDtypeStruct((npad, d // 2), jnp.uint32),
        grid_spec=grid_spec,
        compiler_params=_params(1),
        name="moe_experts",
    )(tile_ea, tile_eb, n_used, hs, gw, w1, w3, w2, w1, w3, w2)


def _dispatch(route, n_tok):
    tmo = MOE_TILE
    n_tiles = n_tok // tmo + N_CLASSES
    npad = n_tiles * tmo
    cls = route[:, 0, :].astype(jnp.int32)
    rank = route[:, 3, :].astype(jnp.int32)
    counts = route[:, 4, :N_CLASSES].astype(jnp.int32)
    tiles_per = (jnp.sum(counts, axis=0) + tmo - 1) // tmo
    tile_end = jnp.cumsum(tiles_per)
    offs = (tile_end - tiles_per) * tmo
    block_base = offs[None, :] + jnp.cumsum(counts, axis=0) - counts
    classes = jnp.arange(N_CLASSES, dtype=jnp.int32)
    base = jnp.sum(jnp.where(cls[:, :, None] == classes, block_base[:, None, :], 0), axis=-1)
    dest = (base + rank).reshape(-1)
    record = jnp.stack([jnp.arange(n_tok, dtype=F32), route[:, 1, :].reshape(-1),
                        route[:, 2, :].reshape(-1), jnp.zeros((n_tok,), F32)], axis=1)
    slots = jnp.zeros((npad, record.shape[1]), F32).at[dest].set(
        record, mode="promise_in_bounds", unique_indices=True)
    src = slots[:, 0].astype(jnp.int32)
    gw = slots[:, 1:3]
    tile_ids = jnp.arange(n_tiles, dtype=jnp.int32)
    tile_cls = jnp.sum((tile_end[None, :] <= tile_ids[:, None]).astype(jnp.int32), axis=1)
    tile_cls = jnp.minimum(tile_cls, N_CLASSES - 1)
    pa = jnp.asarray([p[0] for p in PAIRS], jnp.int32)
    pb = jnp.asarray([p[1] for p in PAIRS], jnp.int32)
    group = (tile_cls // len(PAIRS)) * EXPERTS_PER_GROUP
    tile_ea = group + pa[tile_cls % len(PAIRS)]
    tile_eb = group + pb[tile_cls % len(PAIRS)]
    n_used = tile_end[-1:].astype(jnp.int32)
    return src, dest, gw, tile_ea, tile_eb, n_used


def _sc_gather_rows(table, idx):
    d = table.shape[1]
    m = idx.shape[0]
    per_worker = m // SC_WORKERS
    assert m % (SC_WORKERS * GATHER_CHUNK) == 0 and table.dtype.itemsize == 4
    mesh = plsc.VectorSubcoreMesh(core_axis_name="c", subcore_axis_name="s",
                                  num_cores=SC_CORES, num_subcores=SC_SUBCORES)

    @functools.partial(
        pl.kernel, mesh=mesh,
        out_type=jax.ShapeDtypeStruct((m, d), table.dtype),
        scratch_types=[pltpu.VMEM((GATHER_CHUNK,), jnp.int32),
                       pltpu.VMEM((GATHER_CHUNK, d), table.dtype),
                       pltpu.SemaphoreType.DMA],
        name="sc_row_gather")
    def gather(table_hbm, idx_hbm, out_hbm, idx_v, rows_v, sem):
        worker = lax.axis_index("s") * SC_CORES + lax.axis_index("c")

        @pl.loop(0, per_worker // GATHER_CHUNK)
        def _(chunk):
            off = worker * per_worker + chunk * GATHER_CHUNK
            pltpu.sync_copy(idx_hbm.at[pl.ds(off, GATHER_CHUNK)], idx_v)
            pltpu.async_copy(table_hbm.at[idx_v], rows_v, sem).wait()
            pltpu.sync_copy(rows_v, out_hbm.at[pl.ds(off, GATHER_CHUNK)])

    return gather(table, idx)


def _final_body(x_ref, f_ref, g2_ref, fn_ref, o_ref):
    o_ref[...] = _rms(x_ref[...] + g2_ref[...] * _unpack_pairs(f_ref[...])) * fn_ref[...]


def _final_call(xmid, fg, g2, final_norm, *, n_batch, nj, ncb):
    rows, d = xmid.shape
    tm = ROW_BLOCK
    njl = nj - ncb
    src = lambda b, j: (b * nj + ncb + j, 0)
    return pl.pallas_call(
        _final_body,
        out_shape=jax.ShapeDtypeStruct((n_batch * njl * tm, d), F32),
        grid=(n_batch, njl),
        in_specs=[pl.BlockSpec((tm, d), src), pl.BlockSpec((tm, d // 2), src),
                  pl.BlockSpec((None, 1, d), lambda b, j: (2 * b + 1, 0, 0)),
                  _const_spec((1, d))],
        out_specs=pl.BlockSpec((tm, d), lambda b, j: (b * njl + j, 0)),
        compiler_params=_params(2),
        name="final_norm",
    )(xmid, fg, g2, final_norm.reshape(1, d))


def _split_cols(w, sizes):
    out, off = [], 0
    for s in sizes:
        out.append(w[:, off:off + s])
        off += s
    out.append(w[:, off:])
    return out


def _layer_weights(w_in, w_uq, w_ukv, q_norm, kv_norm):
    wq, wk, wv, wg, wcb, wcc, wcx, wqd, wkvd, wkr, wgate = _split_cols(w_in, IN_SIZES[:-1])
    rope_lanes = lambda w: jnp.pad(w, ((0, 0), (MLA_NOPE, HEAD_PAD - MLA_NOPE - MLA_ROPE)))
    w_ext = jnp.concatenate(
        [wq, wk, wv, wg, wcb, wcc, wcx, wqd, wkvd, rope_lanes(wkr), wgate], axis=1).astype(BF16)

    uq = w_uq.reshape(MLA_Q_RANK, MLA_HEADS, MLA_NOPE + MLA_ROPE)
    tail = HEAD_PAD - MLA_NOPE - MLA_ROPE
    uq_pad = jnp.pad(uq, ((0, 0), (0, 0), (0, tail)))
    wuq = uq_pad.reshape(MLA_Q_RANK, MLA_W).astype(BF16)

    ukv = w_ukv.reshape(MLA_KV_RANK, MLA_HEADS, MLA_NOPE + MLA_V)
    wk_up = jnp.pad(ukv[..., :MLA_NOPE], ((0, 0), (0, 0), (0, HEAD_PAD - MLA_NOPE)))
    wk_up = wk_up.reshape(MLA_KV_RANK, MLA_W).astype(BF16)
    wv_pad = jnp.pad(ukv[..., MLA_NOPE:], ((0, 0), (0, 0), (0, HEAD_PAD - MLA_V)))
    wv_pad = wv_pad.reshape(MLA_KV_RANK, MLA_W).astype(BF16)
    ones_row = jnp.zeros((MLA_HEADS, HEAD_PAD), F32).at[:, MLA_V].set(1.0).reshape(1, MLA_W)
    q_gain = q_norm.astype(F32) * ((MLA_NOPE + MLA_ROPE) ** -0.5 * math.log2(math.e))
    return (w_ext, wuq, wk_up, wv_pad, q_gain.reshape(1, -1),
            kv_norm.reshape(1, -1).astype(F32), ones_row)


def _rotary_tables(n_ctx, n_lat):
    n_all = n_ctx + n_lat
    row = lax.broadcasted_iota(jnp.int32, (n_all, LANES), 0)
    lane = lax.broadcasted_iota(jnp.int32, (n_all, LANES), 1)
    is_ctx = row < n_ctx
    pos = row - n_ctx
    grid_row = pos // GRID_W
    grid_col = pos - grid_row * GRID_W

    def inv_freq(idx, half):
        return ROPE_BASE ** (-idx.astype(F32) / half)

    def table(ang, first_half, live):
        cos = jnp.where(is_ctx, 1.0, jnp.cos(ang))
        sin = jnp.where(is_ctx, 0.0, jnp.where(first_half, -jnp.sin(ang), jnp.sin(ang)))
        return jnp.where(live, cos, 0.0), jnp.where(live, sin, 0.0)

    half = RET_DK // 2
    ang = pos.astype(F32) * inv_freq(lane % half, half)
    cr, sr = table(ang, (lane % RET_DK) < half, True)

    quarter = MLA_ROPE // 4
    rl = lane - MLA_NOPE
    coord = jnp.where(rl < MLA_ROPE // 2, grid_row, grid_col)
    ang = coord.astype(F32) * inv_freq(rl % quarter, quarter)
    cm, sm = table(ang, (rl % (2 * quarter)) < quarter,
                   jnp.logical_and(rl >= 0, rl < MLA_ROPE))
    cm = jnp.where(rl < 0, 1.0, cm)
    return cr, sr, cm, sm


def _retention_consts(ret_decay):
    log_gf = jax.nn.log_sigmoid(ret_decay[0].astype(F32))
    log_gb = jax.nn.log_sigmoid(ret_decay[1].astype(F32))
    idx = jnp.arange(RET_CHUNK, dtype=F32)
    rel = idx[:, None] - idx[None, :]
    dm_f = jnp.where(rel >= 0, jnp.exp(log_gf[:, None, None] * jnp.maximum(rel, 0.0)[None]), 0.0)
    dm_b = jnp.where(rel < 0, jnp.exp(log_gb[:, None, None] * jnp.maximum(-rel, 0.0)[None]), 0.0)

    def lanes(t):
        return jnp.repeat(t, RET_DK, axis=1)

    xi_f = lanes(jnp.exp(log_gf[None, :] * (idx + 1.0)[:, None]))
    zt_f = lanes(jnp.exp(log_gf[None, :] * (RET_CHUNK - 1 - idx)[:, None]))
    xi_b = lanes(jnp.exp(log_gb[None, :] * (RET_CHUNK - idx)[:, None]))
    zt_b = lanes(jnp.exp(log_gb[None, :] * idx[:, None]))
    cd = lambda lg: jnp.broadcast_to(jnp.exp(lg * RET_CHUNK)[:, None, None], (RET_HEADS, 1, RET_DV))
    return dm_f, dm_b, xi_f, zt_f, xi_b, zt_b, cd(log_gf), cd(log_gb)


def kernel(x, c, ctx, c_ctx, w_ada, b_ada, norm1, norm2, w_in, ret_decay, ret_gn, w_ret_o, conv_w,
           w_conv_o, mla_q_norm, w_uq, mla_kv_norm, w_ukv, w_mla_o, w_out, w_router, router_bias,
           w1, w3, w2, final_norm):
    n_batch, n_lat, d = x.shape
    n_ctx = ctx.shape[1]
    depth = w_ada.shape[0]
    t_all = n_ctx + n_lat
    assert n_ctx % ROW_BLOCK == 0 and n_lat % ROW_BLOCK == 0 and n_lat % GRID_W == 0
    nj = t_all // ROW_BLOCK
    ncb = n_ctx // ROW_BLOCK
    n_tok = n_batch * t_all
    assert n_tok % MOE_TILE == 0
    geom = dict(n_batch=n_batch, nj=nj, ncb=ncb)

    cc = jnp.concatenate([c, c_ctx[None, :]], axis=0)
    cc = jnp.pad(cc, ((0, -cc.shape[0] % 8), (0, 0)))
    mod = _ada_call(cc, w_ada, b_ada)[:, :n_batch + 1].reshape(depth, n_batch + 1, N_MOD, d)
    pick = jnp.stack([jnp.full((n_batch,), n_batch, jnp.int32),
                      jnp.arange(n_batch, dtype=jnp.int32)], axis=1).reshape(-1)
    mod = mod[:, pick]

    tabs = _rotary_tables(n_ctx, n_lat)
    wr_hi, wr_lo = _split_bf16(jnp.pad(w_router.astype(F32), ((0, 0), (0, LANES - N_EXPERTS))))
    rbias = router_bias.astype(F32).reshape(N_EXPERTS, 1)

    out = None
    x_parts = (ctx.reshape(n_batch * n_ctx, d), x.reshape(n_batch * n_lat, d))
    for l in range(depth):
        m = mod[l]
        rowvec = lambda v: v.reshape(2 * n_batch, 1, d)
        a1 = rowvec(norm1[l][None, :] * (1.0 + m[:, 1]))
        b1 = rowvec(m[:, 0])
        g1 = rowvec(m[:, 2])
        a2 = rowvec(norm2[l][None, :] * (1.0 + m[:, 4]))
        b2 = rowvec(m[:, 3])
        g2 = rowvec(m[:, 5])

        wts = _layer_weights(w_in[l], w_uq[l], w_ukv[l], mla_q_norm[l], mla_kv_norm[l])
        proj = _inproj_call(x_parts, a1, b1, tabs, wts, **geom)
        rq, rk, rv, rg, cb, cu, qm, km, vm, gt, xa = proj

        yf, yb = _ret_call(rq, rk, rv, _retention_consts(ret_decay[l]), **geom)
        om = _attn_call(qm, km, vm, n_ctx=n_ctx, **geom)

        merge_wts = (ret_gn[l].reshape(1, -1).astype(F32), conv_w[l].T.astype(F32),
                     w_ret_o[l].astype(BF16), w_conv_o[l].astype(BF16), w_mla_o[l].astype(BF16),
                     w_out[l].astype(BF16), wr_hi, wr_lo, rbias)
        xmid, h2, route = _merge_call(yf, yb, rg, cb, cu, om, gt, xa, g1, a2, b2, merge_wts,
                                      nj=nj, ncb=ncb)
        route = route.reshape(-1, ROUTE_ROWS, ROW_BLOCK)
        src, dest, gw, tile_ea, tile_eb, n_used = _dispatch(route, n_tok)
        hs = _sc_gather_rows(h2, src)
        f_sorted = _moe_call(tile_ea, tile_eb, n_used, hs, gw, w1, w3, w2, l)
        fg = _sc_gather_rows(f_sorted, dest)

        if l < depth - 1:
            x_parts = (xmid, fg, g2)
        else:
            out = _final_call(xmid, fg, g2, final_norm, **geom)
    return out.reshape(n_batch, n_lat, d)
```

```python
import functools
import math

import jax
import jax.numpy as jnp
from jax import lax
from jax.experimental import pallas as pl
from jax.experimental.pallas import tpu as pltpu
from jax.experimental.pallas import tpu_sc as plsc

F32 = jnp.float32
BF16 = jnp.bfloat16

GRID_W = 64
RMS_EPS = 1e-6
ROPE_BASE = 10000.0
N_MOD = 6
RET_HEADS = 4
RET_DK = 64
RET_DV = 128
RET_CHUNK = 256
CONV_WIDTH = 512
MLA_HEADS = 8
MLA_Q_RANK = 384
MLA_KV_RANK = 256
MLA_NOPE = 64
MLA_ROPE = 32
MLA_V = 64
N_BRANCH = 3
N_EXPERTS = 16
N_GROUPS = 4
EXPERTS_PER_GROUP = N_EXPERTS // N_GROUPS
D_EXPERT = 512
IN_SIZES = (RET_HEADS * RET_DK, RET_HEADS * RET_DK, RET_HEADS * RET_DV, RET_HEADS * RET_DV,
            CONV_WIDTH, CONV_WIDTH, CONV_WIDTH, MLA_Q_RANK, MLA_KV_RANK, MLA_ROPE, 0)

LANES = 128
BF16_SUBLANES = 16
VMEM_LIMIT = 56 * 1024 * 1024
SC_CORES = 2
SC_SUBCORES = 16
SC_WORKERS = SC_CORES * SC_SUBCORES

HEAD_PAD = LANES
ROW_BLOCK = 256
MOE_TILE = 256
ATTN_KEY_TILES = (768, 512, 256, 128)
ROUTE_ROWS = 8
GATHER_CHUNK = 64
PAIRS = [(a, b) for a in range(EXPERTS_PER_GROUP) for b in range(a + 1, EXPERTS_PER_GROUP)]
N_CLASSES = N_GROUPS * len(PAIRS)

RQ_W = RET_HEADS * RET_DK
RV_W = RET_HEADS * RET_DV
MLA_W = MLA_HEADS * HEAD_PAD
MLA_O = MLA_HEADS * MLA_V


def _const_spec(shape):
    nd = len(shape)
    return pl.BlockSpec(shape, lambda *_: (0,) * nd, pipeline_mode=pl.Buffered(1))


def _params(n_axes):
    return pltpu.CompilerParams(dimension_semantics=("arbitrary",) * n_axes,
                                vmem_limit_bytes=VMEM_LIMIT)


def _dot(a, b):
    return jnp.dot(a, b, preferred_element_type=F32)


def _split_bf16(a):
    hi = a.astype(BF16)
    lo = (a - hi.astype(F32)).astype(BF16)
    return hi, lo


def _silu(v):
    return v * jax.nn.sigmoid(v)


def _rms(v):
    return v * lax.rsqrt(jnp.mean(v * v, axis=-1, keepdims=True) + RMS_EPS)


def _pack_pairs(v):
    half = v.shape[1] // 2
    bits = lambda t: lax.bitcast_convert_type(t.astype(BF16).astype(F32), jnp.uint32)
    return (bits(v[:, :half]) & jnp.uint32(0xFFFF0000)) | (bits(v[:, half:]) >> 16)


def _unpack_pairs(u):
    hi = lax.bitcast_convert_type(u & jnp.uint32(0xFFFF0000), F32)
    lo = lax.bitcast_convert_type(u << 16, F32)
    return jnp.concatenate([hi, lo], axis=1)


def _ada_body(c_ref, w_ref, b_ref, o_ref):
    a_hi, a_lo = _split_bf16(_silu(c_ref[...]))
    w_hi, w_lo = _split_bf16(w_ref[...])
    o_ref[...] = _dot(a_hi, w_hi) + _dot(a_hi, w_lo) + _dot(a_lo, w_hi) + b_ref[...]


def _ada_call(cc, w_ada, b_ada):
    depth, d, nm = w_ada.shape
    rows = cc.shape[0]
    cb = nm // 4
    return pl.pallas_call(
        _ada_body,
        out_shape=jax.ShapeDtypeStruct((depth, rows, nm), F32),
        grid=(depth, nm // cb),
        in_specs=[pl.BlockSpec((rows, d), lambda l, n: (0, 0)),
                  pl.BlockSpec((None, d, cb), lambda l, n: (l, 0, n)),
                  pl.BlockSpec((None, 1, cb), lambda l, n: (l, 0, n))],
        out_specs=pl.BlockSpec((None, rows, cb), lambda l, n: (l, 0, n)),
        compiler_params=_params(2),
        name="ada_mod",
    )(cc, w_ada, b_ada.reshape(depth, 1, nm))


_O_RQ = 0
_O_RK = _O_RQ + RQ_W
_O_RV = _O_RK + RQ_W
_O_RG = _O_RV + RV_W
_O_CB = _O_RG + RV_W
_O_CC = _O_CB + CONV_WIDTH
_O_CX = _O_CC + CONV_WIDTH
_O_QD = _O_CX + CONV_WIDTH
_O_KVD = _O_QD + MLA_Q_RANK
_O_KR = _O_KVD + MLA_KV_RANK
_O_GT = _O_KR + LANES


def _rot_half(v, half):
    width = v.shape[1]
    lane = lax.broadcasted_iota(jnp.int32, v.shape, 1)
    first = (lane % (2 * half)) < half
    return jnp.where(first, pltpu.roll(v, width - half, 1), pltpu.roll(v, half, 1))


def _inproj_body(*refs, d_model, fused, ncb):
    xo_ref = refs[-1]
    if fused:
        xm_ref, f_ref, g2_ref = refs[:3]
        refs = refs[3:]
        x = xm_ref[...] + g2_ref[...] * _unpack_pairs(f_ref[...])
    else:
        ctx_ref, lat_ref = refs[:2]
        refs = refs[2:]
        x = jnp.where(pl.program_id(0) < ncb, ctx_ref[...], lat_ref[...])
    xo_ref[...] = x
    (a1_ref, b1_ref, cr_ref, sr_ref, cm_ref, sm_ref, w_ref, wuq_ref, wk_ref, wv_ref, qn_ref, kvn_ref,
     ones_ref, rq_ref, rk_ref, rv_ref, rg_ref, cb_ref, cu_ref, qm_ref, km_ref, vm_ref, gt_ref) = refs[:23]
    h = (_rms(x) * a1_ref[...] + b1_ref[...]).astype(BF16)

    def mm(off, width):
        return _dot(h, w_ref[:, off:off + width])

    cr = jnp.concatenate([cr_ref[...]] * (RQ_W // cr_ref.shape[1]), axis=1)
    sr = jnp.concatenate([sr_ref[...]] * (RQ_W // sr_ref.shape[1]), axis=1)
    q = mm(_O_RQ, RQ_W)
    rq_ref[...] = (q * cr + _rot_half(q, RET_DK // 2) * sr).astype(BF16)
    k = mm(_O_RK, RQ_W)
    rk_ref[...] = ((k * cr + _rot_half(k, RET_DK // 2) * sr) * (RET_DK ** -0.5)).astype(BF16)
    rv_ref[...] = mm(_O_RV, RV_W).astype(BF16)
    rg_ref[...] = _silu(mm(_O_RG, RV_W)).astype(BF16)
    cb_ref[...] = mm(_O_CB, CONV_WIDTH).astype(BF16)
    cu_ref[...] = (mm(_O_CC, CONV_WIDTH) * mm(_O_CX, CONV_WIDTH)).astype(BF16)

    cm = cm_ref[...]
    sm = sm_ref[...]
    qn = (_rms(mm(_O_QD, MLA_Q_RANK)) * qn_ref[...]).astype(BF16)
    for hd in range(MLA_HEADS):
        lo = hd * HEAD_PAD
        qa = _dot(qn, wuq_ref[:, lo:lo + HEAD_PAD])
        qm_ref[:, lo:lo + HEAD_PAD] = (qa * cm + _rot_half(qa, MLA_ROPE // 4) * sm).astype(BF16)

    kvn = (_rms(mm(_O_KVD, MLA_KV_RANK)) * kvn_ref[...]).astype(BF16)
    kr = mm(_O_KR, HEAD_PAD)
    kr = kr * cm + _rot_half(kr, MLA_ROPE // 4) * sm
    kn = _dot(kvn, wk_ref[...])
    for hd in range(MLA_HEADS):
        lo = hd * HEAD_PAD
        km_ref[:, lo:lo + HEAD_PAD] = (kn[:, lo:lo + HEAD_PAD] + kr).astype(BF16)
    vm_ref[...] = (_dot(kvn, wv_ref[...]) + ones_ref[...]).astype(BF16)

    for br in range(N_BRANCH):
        gt_ref[:, br * d_model:(br + 1) * d_model] = jax.nn.sigmoid(
            mm(_O_GT + br * d_model, d_model)).astype(BF16)


def _inproj_call(x_parts, a1, b1, tabs, wts, *, n_batch, nj, ncb):
    fused = len(x_parts) == 3
    d = x_parts[0].shape[1]
    rows = n_batch * nj * ROW_BLOCK
    tm = ROW_BLOCK
    row = lambda j, b: (b * nj + j, 0)
    mod = lambda j, b: (2 * b + (j >= ncb).astype(jnp.int32), 0, 0)
    tab = lambda j, b: (j, 0)
    if fused:
        in_specs = [pl.BlockSpec((tm, d), row), pl.BlockSpec((tm, d // 2), row),
                    pl.BlockSpec((None, 1, d), mod)]
    else:
        in_specs = [pl.BlockSpec((tm, d), lambda j, b: (jnp.where(j < ncb, b * ncb + j, 0), 0)),
                    pl.BlockSpec((tm, d),
                                 lambda j, b: (jnp.where(j < ncb, 0, b * (nj - ncb) + j - ncb), 0))]
    in_specs += [pl.BlockSpec((None, 1, d), mod), pl.BlockSpec((None, 1, d), mod)]
    in_specs += [pl.BlockSpec((tm, t.shape[1]), tab) for t in tabs]
    in_specs += [_const_spec(w.shape) for w in wts]
    widths = (RQ_W, RQ_W, RV_W, RV_W, CONV_WIDTH, CONV_WIDTH, MLA_W, MLA_W, MLA_W, N_BRANCH * d)
    out_shape = [jax.ShapeDtypeStruct((rows, w), BF16) for w in widths]
    out_specs = [pl.BlockSpec((tm, w), row) for w in widths]
    out_shape.append(jax.ShapeDtypeStruct((rows, d), F32))
    out_specs.append(pl.BlockSpec((tm, d), row))
    return pl.pallas_call(
        functools.partial(_inproj_body, d_model=d, fused=fused, ncb=ncb),
        out_shape=out_shape,
        grid=(nj, n_batch),
        in_specs=in_specs,
        out_specs=out_specs,
        compiler_params=_params(2),
        name="in_proj",
    )(*x_parts, a1, b1, *tabs, *wts)


def _ret_direction(q_ref, k_ref, v_ref, y_ref, s_ref, dm_ref, xi_ref, zt_ref, cd_ref, chunk_order):
    states = [s_ref[hd] for hd in range(RET_HEADS)]
    for c in chunk_order:
        rows = slice(c * RET_CHUNK, (c + 1) * RET_CHUNK)
        q = q_ref[rows, :]
        k = k_ref[rows, :]
        v = v_ref[rows, :]
        qx = (q.astype(F32) * xi_ref[...]).astype(BF16)
        kz = (k.astype(F32) * zt_ref[...]).astype(BF16)
        for hd in range(RET_HEADS):
            ks = slice(hd * RET_DK, (hd + 1) * RET_DK)
            vs = slice(hd * RET_DV, (hd + 1) * RET_DV)
            vh = v[:, vs]
            sc = lax.dot_general(q[:, ks], k[:, ks], (((1,), (1,)), ((), ())),
                                 preferred_element_type=F32)
            inner = _dot((sc * dm_ref[hd]).astype(BF16), vh)
            cross = _dot(qx[:, ks], states[hd].astype(BF16))
            y_ref[rows, vs] = inner + cross
            upd = lax.dot_general(kz[:, ks], vh, (((0,), (0,)), ((), ())),
                                  preferred_element_type=F32)
            states[hd] = cd_ref[hd] * states[hd] + upd
    for hd in range(RET_HEADS):
        s_ref[hd] = states[hd]


def _ret_body(qf_ref, kf_ref, vf_ref, qb_ref, kb_ref, vb_ref,
              dmf_ref, dmb_ref, xif_ref, ztf_ref, xib_ref, ztb_ref, cdf_ref, cdb_ref,
              yf_ref, yb_ref, sf_ref, sb_ref):
    @pl.when(pl.program_id(1) == 0)
    def _():
        sf_ref[...] = jnp.zeros_like(sf_ref)
        sb_ref[...] = jnp.zeros_like(sb_ref)

    n_chunks = ROW_BLOCK // RET_CHUNK
    _ret_direction(qf_ref, kf_ref, vf_ref, yf_ref, sf_ref, dmf_ref, xif_ref, ztf_ref, cdf_ref,
                   range(n_chunks))
    _ret_direction(qb_ref, kb_ref, vb_ref, yb_ref, sb_ref, dmb_ref, xib_ref, ztb_ref, cdb_ref,
                   range(n_chunks - 1, -1, -1))


def _ret_call(rq, rk, rv, consts, *, n_batch, nj, ncb):
    rows = rq.shape[0]
    tm = ROW_BLOCK
    fwd = lambda b, s: (b * nj + s, 0)

    def bwd(b, s):
        return (b * nj + jnp.where(s < ncb, ncb - 1 - s, nj - 1 - (s - ncb)), 0)

    specs = []
    for im in (fwd, bwd):
        specs += [pl.BlockSpec((tm, RQ_W), im), pl.BlockSpec((tm, RQ_W), im),
                  pl.BlockSpec((tm, RV_W), im)]
    specs += [_const_spec(c.shape) for c in consts]
    return pl.pallas_call(
        _ret_body,
        out_shape=[jax.ShapeDtypeStruct((rows, RV_W), F32)] * 2,
        grid=(n_batch, nj),
        in_specs=specs,
        out_specs=[pl.BlockSpec((tm, RV_W), fwd), pl.BlockSpec((tm, RV_W), bwd)],
        scratch_shapes=[pltpu.VMEM((RET_HEADS, RET_DK, RET_DV), F32)] * 2,
        compiler_params=_params(2),
        name="retention",
    )(rq, rk, rv, rq, rk, rv, *consts)


def _pick_tile(n, candidates):
    for c in candidates:
        if n % c == 0:
            return c
    raise ValueError(f"no tile for {n}")


def _attn_body(q_ref, k_ref, v_ref, o_ref, s_ref, *, n_ctx, n_all, ncb):
    tq = q_ref.shape[0]
    heads = [slice(hh * HEAD_PAD, (hh + 1) * HEAD_PAD) for hh in range(2)]

    def attend(n_keys, tk):
        nt = n_keys // tk
        qs = [q_ref[:, hs] for hs in heads]

        def qk(t, mrun):
            r0 = pl.multiple_of(t * tk, tk)
            out = []
            for hh, hs in enumerate(heads):
                s = lax.dot_general(qs[hh], k_ref[pl.ds(r0, tk), hs], (((1,), (1,)), ((), ())),
                                    preferred_element_type=F32)
                s_ref[hh, t, :, 0:tk] = s
                m = mrun[hh]
                for cc in range(tk // LANES):
                    m = jnp.maximum(m, s[:, cc * LANES:(cc + 1) * LANES])
                out.append(m)
            return tuple(out)

        mrun = lax.fori_loop(0, nt, qk, tuple(jnp.full((tq, LANES), -jnp.inf, F32) for _ in heads),
                             unroll=True)
        mrow = [jnp.max(m, axis=-1, keepdims=True) for m in mrun]

        def pv(t, accs):
            r0 = pl.multiple_of(t * tk, tk)
            out = []
            for hh, hs in enumerate(heads):
                p = jnp.exp2(s_ref[hh, t, :, 0:tk] - mrow[hh]).astype(BF16)
                out.append(accs[hh] + _dot(p, v_ref[pl.ds(r0, tk), hs]))
            return tuple(out)

        accs = lax.fori_loop(0, nt, pv, tuple(jnp.zeros((tq, HEAD_PAD), F32) for _ in heads),
                             unroll=True)
        o_ref[...] = jnp.concatenate([a[:, :MLA_V] / a[:, MLA_V:MLA_V + 1] for a in accs],
                                     axis=-1).astype(BF16)

    j = pl.program_id(2)

    @pl.when(j < ncb)
    def _():
        attend(n_ctx, _pick_tile(n_ctx, (256, 128)))

    @pl.when(j >= ncb)
    def _():
        attend(n_all, _pick_tile(n_all, ATTN_KEY_TILES))


def _attn_call(qm, km, vm, *, n_batch, nj, ncb, n_ctx):
    rows = qm.shape[0]
    tm = ROW_BLOCK
    t_all = nj * tm
    tk = _pick_tile(t_all, ATTN_KEY_TILES)
    qmap = lambda b, hp, j: (b * nj + j, hp)
    kmap = lambda b, hp, j: (b, hp)
    return pl.pallas_call(
        functools.partial(_attn_body, n_ctx=n_ctx, n_all=t_all, ncb=ncb),
        out_shape=jax.ShapeDtypeStruct((rows, MLA_O), BF16),
        grid=(n_batch, MLA_HEADS // 2, nj),
        in_specs=[pl.BlockSpec((tm, 2 * HEAD_PAD), qmap),
                  pl.BlockSpec((t_all, 2 * HEAD_PAD), kmap),
                  pl.BlockSpec((t_all, 2 * HEAD_PAD), kmap)],
        out_specs=pl.BlockSpec((tm, 2 * MLA_V), qmap),
        scratch_shapes=[pltpu.VMEM((2, t_all // tk, tm, tk), F32)],
        compiler_params=_params(3),
        name="mla_attention",
    )(qm, km, vm)


def _merge_body(yf_ref, yb_ref, rg_ref, cb_ref, cu_ref, cup_ref, cun_ref, om_ref, gt_ref, x_ref,
                g1_ref, a2_ref, b2_ref, gn_ref, cw_ref, wro_ref, wco_ref, wmo_ref, wout_ref,
                wrh_ref, wrl_ref, rb_ref, xmid_ref, h2_ref, rt_ref, lg_ref, *, nj, ncb, d_model, nb):
    tm = x_ref.shape[0]
    step = pl.program_id(0)

    @pl.when(step == 0)
    def _():
        lg_ref[...] = jnp.zeros_like(lg_ref)

    cls, w_lo, w_hi = _route_rows(jnp.transpose(lg_ref[...])[:N_EXPERTS, :], rb_ref[...])
    rank, counts = _block_ranks(cls.astype(F32))
    rt_ref[...] = jnp.concatenate(
        [cls.astype(F32), w_lo, w_hi, rank, counts, jnp.zeros((ROUTE_ROWS - 5, tm), F32)], axis=0)

    j = jnp.minimum(step, nb - 1) % nj
    seg_first = jnp.logical_or(j == 0, j == ncb)
    seg_last = jnp.logical_or(j == ncb - 1, j == nj - 1)

    y = yf_ref[...] + yb_ref[...]
    yn = jnp.concatenate([_rms(y[:, hd * RET_DV:(hd + 1) * RET_DV]) for hd in range(RET_HEADS)],
                         axis=-1) * gn_ref[...]
    y_ret = _dot((rg_ref[...].astype(F32) * yn).astype(BF16), wro_ref[...])

    u = cu_ref[...].astype(F32)
    ridx = lax.broadcasted_iota(jnp.int32, u.shape, 0)
    prev_row = cup_ref[...].astype(F32)[BF16_SUBLANES - 1:, :] * jnp.where(seg_first, 0.0, 1.0)
    next_row = cun_ref[...].astype(F32)[0:1, :] * jnp.where(seg_last, 0.0, 1.0)
    u_prev = jnp.where(ridx == 0, prev_row, pltpu.roll(u, 1, 0))
    u_next = jnp.where(ridx == tm - 1, next_row, pltpu.roll(u, tm - 1, 0))
    conv = u_prev * cw_ref[0:1, :] + u * cw_ref[1:2, :] + u_next * cw_ref[2:3, :]
    y_conv = _dot((cb_ref[...].astype(F32) * conv).astype(BF16), wco_ref[...])

    y_mla = _dot(om_ref[...], wmo_ref[...])

    merged = (gt_ref[:, 0:d_model].astype(F32) * y_ret
              + gt_ref[:, d_model:2 * d_model].astype(F32) * y_conv
              + gt_ref[:, 2 * d_model:3 * d_model].astype(F32) * y_mla)
    x_mid = x_ref[...] + g1_ref[...] * _dot(merged.astype(BF16), wout_ref[...])
    xmid_ref[...] = x_mid

    h2 = _rms(x_mid) * a2_ref[...] + b2_ref[...]
    h_hi, h_lo = _split_bf16(h2)
    lg_ref[...] = _dot(h_hi, wrh_ref[...]) + _dot(h_hi, wrl_ref[...]) + _dot(h_lo, wrh_ref[...])
    h2_ref[...] = _pack_pairs(h_hi)


def _top2_of4(v):
    def first_max(rows):
        best, idx = rows[0], jnp.zeros(rows[0].shape, jnp.int32)
        for e in range(1, len(rows)):
            better = rows[e] > best
            idx = jnp.where(better, e, idx)
            best = jnp.where(better, rows[e], best)
        return best, idx

    b1, i1 = first_max(v)
    b2, i2 = first_max([jnp.where(i1 == e, -jnp.inf, v[e]) for e in range(len(v))])
    return i1, i2, b1, b2


def _route_rows(logits_t, bias):
    scores = jax.nn.sigmoid(logits_t)
    biased = scores + bias
    row = lambda a, e: a[e:e + 1, :]
    best = None
    for g in range(N_GROUPS):
        v = [row(biased, g * EXPERTS_PER_GROUP + e) for e in range(EXPERTS_PER_GROUP)]
        i1, i2, b1, b2 = _top2_of4(v)
        cand = (b1 + b2, jnp.full(i1.shape, g, jnp.int32), i1, i2)
        if best is None:
            best = cand
        else:
            better = cand[0] > best[0]
            best = tuple(jnp.where(better, c, o) for c, o in zip(cand, best))
    _, g_sel, i1, i2 = best
    lo = jnp.minimum(i1, i2)
    hi = jnp.maximum(i1, i2)
    e_lo = g_sel * EXPERTS_PER_GROUP + lo
    e_hi = g_sel * EXPERTS_PER_GROUP + hi
    s_lo = jnp.zeros_like(best[0])
    s_hi = jnp.zeros_like(best[0])
    for e in range(N_EXPERTS):
        s_lo = jnp.where(e_lo == e, row(scores, e), s_lo)
        s_hi = jnp.where(e_hi == e, row(scores, e), s_hi)
    total = s_lo + s_hi
    pair_base = jnp.where(lo == 0, 0, jnp.where(lo == 1, 3, 5))
    cls = g_sel * len(PAIRS) + pair_base + (hi - lo - 1)
    return cls, s_lo / total, s_hi / total


def _block_ranks(cls_row):
    n = cls_row.shape[1]
    cls_col = jnp.transpose(jnp.broadcast_to(cls_row, (LANES, n)))[:, 0:1]
    ii = lax.broadcasted_iota(jnp.int32, (n, n), 0)
    jj = lax.broadcasted_iota(jnp.int32, (n, n), 1)
    earlier_same = jnp.logical_and(cls_col == cls_row, ii < jj)
    rank = jnp.sum(jnp.where(earlier_same, 1.0, 0.0), axis=0, keepdims=True)
    lane = lax.broadcasted_iota(jnp.int32, (1, n), 1)
    counts = jnp.zeros((1, n), F32)
    for c in range(N_CLASSES):
        cnt = jnp.sum(jnp.where(cls_row == c, 1.0, 0.0), axis=1, keepdims=True)
        counts = jnp.where(lane == c, cnt, counts)
    return rank, counts


def _merge_call(yf, yb, rg, cb, cu, om, gt, xa, g1, a2, b2, wts, *, nj, ncb):
    rows, d = xa.shape
    tm = ROW_BLOCK
    nb = rows // tm
    halo = BF16_SUBLANES
    per_blk = tm // halo
    blk = lambda i: jnp.minimum(i, nb - 1)
    row = lambda i: (blk(i), 0)
    mod = lambda i: (2 * (blk(i) // nj) + ((blk(i) % nj) >= ncb).astype(jnp.int32), 0, 0)
    prev = lambda i: (jnp.maximum(blk(i) * per_blk - 1, 0), 0)
    nxt = lambda i: (jnp.minimum((blk(i) + 1) * per_blk, rows // halo - 1), 0)
    in_specs = [pl.BlockSpec((tm, RV_W), row), pl.BlockSpec((tm, RV_W), row),
                pl.BlockSpec((tm, RV_W), row),
                pl.BlockSpec((tm, CONV_WIDTH), row), pl.BlockSpec((tm, CONV_WIDTH), row),
                pl.BlockSpec((halo, CONV_WIDTH), prev), pl.BlockSpec((halo, CONV_WIDTH), nxt),
                pl.BlockSpec((tm, MLA_O), row), pl.BlockSpec((tm, N_BRANCH * d), row),
                pl.BlockSpec((tm, d), row),
                pl.BlockSpec((None, 1, d), mod), pl.BlockSpec((None, 1, d), mod),
                pl.BlockSpec((None, 1, d), mod)]
    in_specs += [_const_spec(w.shape) for w in wts]
    return pl.pallas_call(
        functools.partial(_merge_body, nj=nj, ncb=ncb, d_model=d, nb=nb),
        out_shape=[jax.ShapeDtypeStruct((rows, d), F32), jax.ShapeDtypeStruct((rows, d // 2), jnp.uint32),
                   jax.ShapeDtypeStruct((nb * ROUTE_ROWS, tm), F32)],
        grid=(nb + 1,),
        in_specs=in_specs,
        out_specs=[pl.BlockSpec((tm, d), row), pl.BlockSpec((tm, d // 2), row),
                   pl.BlockSpec((ROUTE_ROWS, tm), lambda i: (jnp.maximum(i - 1, 0), 0))],
        scratch_shapes=[pltpu.VMEM((tm, LANES), F32)],
        compiler_params=_params(1),
        name="merge_out_proj",
    )(yf, yb, rg, cb, cu, cu, cu, om, gt, xa, g1, a2, b2, *wts)


def _moe_body(ea_ref, eb_ref, nu_ref, h_ref, gw_ref, w1a_ref, w3a_ref, w2a_ref, w1b_ref, w3b_ref,
              w2b_ref, o_ref, w13a_s, w2a_s, w13b_s, w2b_s):
    t = pl.program_id(0)
    used = t < nu_ref[0]
    prev = jnp.maximum(t - 1, 0)

    def refresh(e_ref, w1_ref, w3_ref, w2_ref, w13_s, w2_s):
        @pl.when(jnp.logical_and(used, jnp.logical_or(t == 0, e_ref[t] != e_ref[prev])))
        def _():
            w13_s[:, :D_EXPERT] = w1_ref[...].astype(BF16)
            w13_s[:, D_EXPERT:] = w3_ref[...].astype(BF16)
            w2_s[...] = w2_ref[...].astype(BF16)

    refresh(ea_ref, w1a_ref, w3a_ref, w2a_ref, w13a_s, w2a_s)
    refresh(eb_ref, w1b_ref, w3b_ref, w2b_ref, w13b_s, w2b_s)

    @pl.when(used)
    def _():
        h = _unpack_pairs(h_ref[...]).astype(BF16)
        gw = gw_ref[...]

        def expert(w13_s, w2_s, wt):
            a = _dot(h, w13_s[...])
            act = _silu(a[:, :D_EXPERT]) * a[:, D_EXPERT:] * wt
            return _dot(act.astype(BF16), w2_s[...])

        o_ref[...] = _pack_pairs(expert(w13a_s, w2a_s, gw[:, 0:1])
                                 + expert(w13b_s, w2b_s, gw[:, 1:2]))

    @pl.when(jnp.logical_not(used))
    def _():
        o_ref[...] = jnp.zeros_like(o_ref)


def _moe_call(tile_ea, tile_eb, n_used, hs, gw, w1, w3, w2, layer):
    npad = hs.shape[0]
    d = w1.shape[2]
    tmo = MOE_TILE
    row = lambda t, ea, eb, nu: (t, 0)
    wa = lambda t, ea, eb, nu: (layer, ea[t], 0, 0)
    wb = lambda t, ea, eb, nu: (layer, eb[t], 0, 0)
    up = lambda im: pl.BlockSpec((None, None, d, D_EXPERT), im)
    down = lambda im: pl.BlockSpec((None, None, D_EXPERT, d), im)
    grid_spec = pltpu.PrefetchScalarGridSpec(
        num_scalar_prefetch=3,
        grid=(npad // tmo,),
        in_specs=[pl.BlockSpec((tmo, d // 2), row), pl.BlockSpec((tmo, 2), row),
                  up(wa), up(wa), down(wa), up(wb), up(wb), down(wb)],
        out_specs=pl.BlockSpec((tmo, d // 2), row),
        scratch_shapes=[pltpu.VMEM((d, 2 * D_EXPERT), BF16), pltpu.VMEM((D_EXPERT, d), BF16)] * 2)
    return pl.pallas_call(
        _moe_body,
        out_shape=jax.ShapeDtypeStruct((npad, d // 2), jnp.uint32),
        grid_spec=grid_spec,
        compiler_params=_params(1),
        name="moe_experts",
    )(tile_ea, tile_eb, n_used, hs, gw, w1, w3, w2, w1, w3, w2)


def _dispatch(route, n_tok):
    tmo = MOE_TILE
    n_tiles = n_tok // tmo + N_CLASSES
    npad = n_tiles * tmo
    cls = route[:, 0, :].astype(jnp.int32)
    rank = route[:, 3, :].astype(jnp.int32)
    counts = route[:, 4, :N_CLASSES].astype(jnp.int32)
    tiles_per = (jnp.sum(counts, axis=0) + tmo - 1) // tmo
    tile_end = jnp.cumsum(tiles_per)
    offs = (tile_end - tiles_per) * tmo
    block_base = offs[None, :] + jnp.cumsum(counts, axis=0) - counts
    classes = jnp.arange(N_CLASSES, dtype=jnp.int32)
    base = jnp.sum(jnp.where(cls[:, :, None] == classes, block_base[:, None, :], 0), axis=-1)
    dest = (base + rank).reshape(-1)
    record = jnp.stack([jnp.arange(n_tok, dtype=F32), route[:, 1, :].reshape(-1),
                        route[:, 2, :].reshape(-1), jnp.zeros((n_tok,), F32)], axis=1)
    spread = (jnp.arange(npad, dtype=jnp.int32) % n_tok).astype(F32)
    empty = jnp.concatenate([spread[:, None], jnp.zeros((npad, record.shape[1] - 1), F32)], axis=1)
    slots = empty.at[dest].set(record, mode="promise_in_bounds", unique_indices=True)
    src = slots[:, 0].astype(jnp.int32)
    gw = slots[:, 1:3]
    tile_ids = jnp.arange(n_tiles, dtype=jnp.int32)
    tile_cls = jnp.sum((tile_end[None, :] <= tile_ids[:, None]).astype(jnp.int32), axis=1)
    tile_cls = jnp.minimum(tile_cls, N_CLASSES - 1)
    pa = jnp.asarray([p[0] for p in PAIRS], jnp.int32)
    pb = jnp.asarray([p[1] for p in PAIRS], jnp.int32)
    group = (tile_cls // len(PAIRS)) * EXPERTS_PER_GROUP
    tile_ea = group + pa[tile_cls % len(PAIRS)]
    tile_eb = group + pb[tile_cls % len(PAIRS)]
    n_used = tile_end[-1:].astype(jnp.int32)
    return src, dest, gw, tile_ea, tile_eb, n_used


def _sc_gather_rows(table, idx):
    d = table.shape[1]
    m = idx.shape[0]
    per_worker = m // SC_WORKERS
    assert m % (SC_WORKERS * GATHER_CHUNK) == 0 and table.dtype.itemsize == 4
    mesh = plsc.VectorSubcoreMesh(core_axis_name="c", subcore_axis_name="s",
                                  num_cores=SC_CORES, num_subcores=SC_SUBCORES)

    @functools.partial(
        pl.kernel, mesh=mesh,
        out_type=jax.ShapeDtypeStruct((m, d), table.dtype),
        scratch_types=[pltpu.VMEM((GATHER_CHUNK,), jnp.int32),
                       pltpu.VMEM((GATHER_CHUNK, d), table.dtype),
                       pltpu.SemaphoreType.DMA],
        name="sc_row_gather")
    def gather(table_hbm, idx_hbm, out_hbm, idx_v, rows_v, sem):
        worker = lax.axis_index("s") * SC_CORES + lax.axis_index("c")

        @pl.loop(0, per_worker // GATHER_CHUNK)
        def _(chunk):
            off = worker * per_worker + chunk * GATHER_CHUNK
            pltpu.sync_copy(idx_hbm.at[pl.ds(off, GATHER_CHUNK)], idx_v)
            pltpu.async_copy(table_hbm.at[idx_v], rows_v, sem).wait()
            pltpu.sync_copy(rows_v, out_hbm.at[pl.ds(off, GATHER_CHUNK)])

    return gather(table, idx)


def _final_body(x_ref, f_ref, g2_ref, fn_ref, o_ref):
    o_ref[...] = _rms(x_ref[...] + g2_ref[...] * _unpack_pairs(f_ref[...])) * fn_ref[...]


def _final_call(xmid, fg, g2, final_norm, *, n_batch, nj, ncb):
    rows, d = xmid.shape
    tm = ROW_BLOCK
    njl = nj - ncb
    src = lambda b, j: (b * nj + ncb + j, 0)
    return pl.pallas_call(
        _final_body,
        out_shape=jax.ShapeDtypeStruct((n_batch * njl * tm, d), F32),
        grid=(n_batch, njl),
        in_specs=[pl.BlockSpec((tm, d), src), pl.BlockSpec((tm, d // 2), src),
                  pl.BlockSpec((None, 1, d), lambda b, j: (2 * b + 1, 0, 0)),
                  _const_spec((1, d))],
        out_specs=pl.BlockSpec((tm, d), lambda b, j: (b * njl + j, 0)),
        compiler_params=_params(2),
        name="final_norm",
    )(xmid, fg, g2, final_norm.reshape(1, d))


def _split_cols(w, sizes):
    out, off = [], 0
    for s in sizes:
        out.append(w[:, off:off + s])
        off += s
    out.append(w[:, off:])
    return out


def _layer_weights(w_in, w_uq, w_ukv, q_norm, kv_norm):
    wq, wk, wv, wg, wcb, wcc, wcx, wqd, wkvd, wkr, wgate = _split_cols(w_in, IN_SIZES[:-1])
    rope_lanes = lambda w: jnp.pad(w, ((0, 0), (MLA_NOPE, HEAD_PAD - MLA_NOPE - MLA_ROPE)))
    w_ext = jnp.concatenate(
        [wq, wk, wv, wg, wcb, wcc, wcx, wqd, wkvd, rope_lanes(wkr), wgate], axis=1).astype(BF16)

    uq = w_uq.reshape(MLA_Q_RANK, MLA_HEADS, MLA_NOPE + MLA_ROPE)
    tail = HEAD_PAD - MLA_NOPE - MLA_ROPE
    uq_pad = jnp.pad(uq, ((0, 0), (0, 0), (0, tail)))
    wuq = uq_pad.reshape(MLA_Q_RANK, MLA_W).astype(BF16)

    ukv = w_ukv.reshape(MLA_KV_RANK, MLA_HEADS, MLA_NOPE + MLA_V)
    wk_up = jnp.pad(ukv[..., :MLA_NOPE], ((0, 0), (0, 0), (0, HEAD_PAD - MLA_NOPE)))
    wk_up = wk_up.reshape(MLA_KV_RANK, MLA_W).astype(BF16)
    wv_pad = jnp.pad(ukv[..., MLA_NOPE:], ((0, 0), (0, 0), (0, HEAD_PAD - MLA_V)))
    wv_pad = wv_pad.reshape(MLA_KV_RANK, MLA_W).astype(BF16)
    ones_row = jnp.zeros((MLA_HEADS, HEAD_PAD), F32).at[:, MLA_V].set(1.0).reshape(1, MLA_W)
    q_gain = q_norm.astype(F32) * ((MLA_NOPE + MLA_ROPE) ** -0.5 * math.log2(math.e))
    return (w_ext, wuq, wk_up, wv_pad, q_gain.reshape(1, -1),
            kv_norm.reshape(1, -1).astype(F32), ones_row)


def _rotary_tables(n_ctx, n_lat):
    n_all = n_ctx + n_lat
    row = lax.broadcasted_iota(jnp.int32, (n_all, LANES), 0)
    lane = lax.broadcasted_iota(jnp.int32, (n_all, LANES), 1)
    is_ctx = row < n_ctx
    pos = row - n_ctx
    grid_row = pos // GRID_W
    grid_col = pos - grid_row * GRID_W

    def inv_freq(idx, half):
        return ROPE_BASE ** (-idx.astype(F32) / half)

    def table(ang, first_half, live):
        cos = jnp.where(is_ctx, 1.0, jnp.cos(ang))
        sin = jnp.where(is_ctx, 0.0, jnp.where(first_half, -jnp.sin(ang), jnp.sin(ang)))
        return jnp.where(live, cos, 0.0), jnp.where(live, sin, 0.0)

    half = RET_DK // 2
    ang = pos.astype(F32) * inv_freq(lane % half, half)
    cr, sr = table(ang, (lane % RET_DK) < half, True)

    quarter = MLA_ROPE // 4
    rl = lane - MLA_NOPE
    coord = jnp.where(rl < MLA_ROPE // 2, grid_row, grid_col)
    ang = coord.astype(F32) * inv_freq(rl % quarter, quarter)
    cm, sm = table(ang, (rl % (2 * quarter)) < quarter,
                   jnp.logical_and(rl >= 0, rl < MLA_ROPE))
    cm = jnp.where(rl < 0, 1.0, cm)
    return cr, sr, cm, sm


def _retention_consts(ret_decay):
    log_gf = jax.nn.log_sigmoid(ret_decay[0].astype(F32))
    log_gb = jax.nn.log_sigmoid(ret_decay[1].astype(F32))
    idx = jnp.arange(RET_CHUNK, dtype=F32)
    rel = idx[:, None] - idx[None, :]
    dm_f = jnp.where(rel >= 0, jnp.exp(log_gf[:, None, None] * jnp.maximum(rel, 0.0)[None]), 0.0)
    dm_b = jnp.where(rel < 0, jnp.exp(log_gb[:, None, None] * jnp.maximum(-rel, 0.0)[None]), 0.0)

    def lanes(t):
        return jnp.repeat(t, RET_DK, axis=1)

    xi_f = lanes(jnp.exp(log_gf[None, :] * (idx + 1.0)[:, None]))
    zt_f = lanes(jnp.exp(log_gf[None, :] * (RET_CHUNK - 1 - idx)[:, None]))
    xi_b = lanes(jnp.exp(log_gb[None, :] * (RET_CHUNK - idx)[:, None]))
    zt_b = lanes(jnp.exp(log_gb[None, :] * idx[:, None]))
    cd = lambda lg: jnp.broadcast_to(jnp.exp(lg * RET_CHUNK)[:, None, None], (RET_HEADS, 1, RET_DV))
    return dm_f, dm_b, xi_f, zt_f, xi_b, zt_b, cd(log_gf), cd(log_gb)


def kernel(x, c, ctx, c_ctx, w_ada, b_ada, norm1, norm2, w_in, ret_decay, ret_gn, w_ret_o, conv_w,
           w_conv_o, mla_q_norm, w_uq, mla_kv_norm, w_ukv, w_mla_o, w_out, w_router, router_bias,
           w1, w3, w2, final_norm):
    n_batch, n_lat, d = x.shape
    n_ctx = ctx.shape[1]
    depth = w_ada.shape[0]
    t_all = n_ctx + n_lat
    assert n_ctx % ROW_BLOCK == 0 and n_lat % ROW_BLOCK == 0 and n_lat % GRID_W == 0
    nj = t_all // ROW_BLOCK
    ncb = n_ctx // ROW_BLOCK
    n_tok = n_batch * t_all
    assert n_tok % MOE_TILE == 0
    geom = dict(n_batch=n_batch, nj=nj, ncb=ncb)

    cc = jnp.concatenate([c, c_ctx[None, :]], axis=0)
    cc = jnp.pad(cc, ((0, -cc.shape[0] % 8), (0, 0)))
    mod = _ada_call(cc, w_ada, b_ada)[:, :n_batch + 1].reshape(depth, n_batch + 1, N_MOD, d)
    pick = jnp.stack([jnp.full((n_batch,), n_batch, jnp.int32),
                      jnp.arange(n_batch, dtype=jnp.int32)], axis=1).reshape(-1)
    mod = mod[:, pick]

    tabs = _rotary_tables(n_ctx, n_lat)
    wr_hi, wr_lo = _split_bf16(jnp.pad(w_router.astype(F32), ((0, 0), (0, LANES - N_EXPERTS))))
    rbias = router_bias.astype(F32).reshape(N_EXPERTS, 1)

    out = None
    x_parts = (ctx.reshape(n_batch * n_ctx, d), x.reshape(n_batch * n_lat, d))
    for l in range(depth):
        m = mod[l]
        rowvec = lambda v: v.reshape(2 * n_batch, 1, d)
        a1 = rowvec(norm1[l][None, :] * (1.0 + m[:, 1]))
        b1 = rowvec(m[:, 0])
        g1 = rowvec(m[:, 2])
        a2 = rowvec(norm2[l][None, :] * (1.0 + m[:, 4]))
        b2 = rowvec(m[:, 3])
        g2 = rowvec(m[:, 5])

        wts = _layer_weights(w_in[l], w_uq[l], w_ukv[l], mla_q_norm[l], mla_kv_norm[l])
        proj = _inproj_call(x_parts, a1, b1, tabs, wts, **geom)
        rq, rk, rv, rg, cb, cu, qm, km, vm, gt, xa = proj

        yf, yb = _ret_call(rq, rk, rv, _retention_consts(ret_decay[l]), **geom)
        om = _attn_call(qm, km, vm, n_ctx=n_ctx, **geom)

        merge_wts = (ret_gn[l].reshape(1, -1).astype(F32), conv_w[l].T.astype(F32),
                     w_ret_o[l].astype(BF16), w_conv_o[l].astype(BF16), w_mla_o[l].astype(BF16),
                     w_out[l].astype(BF16), wr_hi, wr_lo, rbias)
        xmid, h2, route = _merge_call(yf, yb, rg, cb, cu, om, gt, xa, g1, a2, b2, merge_wts,
                                      nj=nj, ncb=ncb)
        route = route.reshape(-1, ROUTE_ROWS, ROW_BLOCK)
        src, dest, gw, tile_ea, tile_eb, n_used = _dispatch(route, n_tok)
        hs = _sc_gather_rows(h2, src)
        f_sorted = _moe_call(tile_ea, tile_eb, n_used, hs, gw, w1, w3, w2, l)
        fg = _sc_gather_rows(f_sorted, dest)

        if l < depth - 1:
            x_parts = (xmid, fg, g2)
        else:
            out = _final_call(xmid, fg, g2, final_norm, **geom)
    return out.reshape(n_batch, n_lat, d)
```

```python
import functools
import math

import jax
import jax.numpy as jnp
from jax import lax
from jax.experimental import pallas as pl
from jax.experimental.pallas import tpu as pltpu
from jax.experimental.pallas import tpu_sc as plsc

F32 = jnp.float32
BF16 = jnp.bfloat16

GRID_W = 64
RMS_EPS = 1e-6
ROPE_BASE = 10000.0
N_MOD = 6
RET_HEADS = 4
RET_DK = 64
RET_DV = 128
RET_CHUNK = 256
CONV_WIDTH = 512
MLA_HEADS = 8
MLA_Q_RANK = 384
MLA_KV_RANK = 256
MLA_NOPE = 64
MLA_ROPE = 32
MLA_V = 64
N_BRANCH = 3
N_EXPERTS = 16
N_GROUPS = 4
EXPERTS_PER_GROUP = N_EXPERTS // N_GROUPS
D_EXPERT = 512
IN_SIZES = (RET_HEADS * RET_DK, RET_HEADS * RET_DK, RET_HEADS * RET_DV, RET_HEADS * RET_DV,
            CONV_WIDTH, CONV_WIDTH, CONV_WIDTH, MLA_Q_RANK, MLA_KV_RANK, MLA_ROPE, 0)

LANES = 128
BF16_SUBLANES = 16
VMEM_LIMIT = 56 * 1024 * 1024
SC_CORES = 2
SC_SUBCORES = 16
SC_WORKERS = SC_CORES * SC_SUBCORES

HEAD_PAD = LANES
ROW_BLOCK = 256
MOE_TILE = 256
ATTN_KEY_TILES = (768, 512, 256, 128)
ROUTE_ROWS = 8
GATHER_CHUNK = 64
PAIRS = [(a, b) for a in range(EXPERTS_PER_GROUP) for b in range(a + 1, EXPERTS_PER_GROUP)]
N_CLASSES = N_GROUPS * len(PAIRS)

RQ_W = RET_HEADS * RET_DK
RV_W = RET_HEADS * RET_DV
MLA_W = MLA_HEADS * HEAD_PAD
MLA_O = MLA_HEADS * MLA_V


def _const_spec(shape):
    nd = len(shape)
    return pl.BlockSpec(shape, lambda *_: (0,) * nd, pipeline_mode=pl.Buffered(1))


def _params(n_axes):
    return pltpu.CompilerParams(dimension_semantics=("arbitrary",) * n_axes,
                                vmem_limit_bytes=VMEM_LIMIT)


def _dot(a, b):
    return jnp.dot(a, b, preferred_element_type=F32)


def _split_bf16(a):
    hi = a.astype(BF16)
    lo = (a - hi.astype(F32)).astype(BF16)
    return hi, lo


def _silu(v):
    return v * jax.nn.sigmoid(v)


def _rms(v):
    return v * lax.rsqrt(jnp.mean(v * v, axis=-1, keepdims=True) + RMS_EPS)


def _pack_pairs(v):
    half = v.shape[1] // 2
    bits = lambda t: lax.bitcast_convert_type(t.astype(BF16).astype(F32), jnp.uint32)
    return (bits(v[:, :half]) & jnp.uint32(0xFFFF0000)) | (bits(v[:, half:]) >> 16)


def _unpack_pairs(u):
    hi = lax.bitcast_convert_type(u & jnp.uint32(0xFFFF0000), F32)
    lo = lax.bitcast_convert_type(u << 16, F32)
    return jnp.concatenate([hi, lo], axis=1)


def _ada_body(c_ref, w_ref, b_ref, o_ref):
    a_hi, a_lo = _split_bf16(_silu(c_ref[...]))
    w_hi, w_lo = _split_bf16(w_ref[...])
    o_ref[...] = _dot(a_hi, w_hi) + _dot(a_hi, w_lo) + _dot(a_lo, w_hi) + b_ref[...]


def _ada_call(cc, w_ada, b_ada):
    depth, d, nm = w_ada.shape
    rows = cc.shape[0]
    cb = nm // 4
    return pl.pallas_call(
        _ada_body,
        out_shape=jax.ShapeDtypeStruct((depth, rows, nm), F32),
        grid=(depth, nm // cb),
        in_specs=[pl.BlockSpec((rows, d), lambda l, n: (0, 0)),
                  pl.BlockSpec((None, d, cb), lambda l, n: (l, 0, n)),
                  pl.BlockSpec((None, 1, cb), lambda l, n: (l, 0, n))],
        out_specs=pl.BlockSpec((None, rows, cb), lambda l, n: (l, 0, n)),
        compiler_params=_params(2),
        name="ada_mod",
    )(cc, w_ada, b_ada.reshape(depth, 1, nm))


_O_RQ = 0
_O_RK = _O_RQ + RQ_W
_O_RV = _O_RK + RQ_W
_O_RG = _O_RV + RV_W
_O_CB = _O_RG + RV_W
_O_CC = _O_CB + CONV_WIDTH
_O_CX = _O_CC + CONV_WIDTH
_O_QD = _O_CX + CONV_WIDTH
_O_KVD = _O_QD + MLA_Q_RANK
_O_KR = _O_KVD + MLA_KV_RANK
_O_GT = _O_KR + LANES


def _rot_half(v, half):
    width = v.shape[1]
    lane = lax.broadcasted_iota(jnp.int32, v.shape, 1)
    first = (lane % (2 * half)) < half
    return jnp.where(first, pltpu.roll(v, width - half, 1), pltpu.roll(v, half, 1))


def _inproj_body(*refs, d_model, fused, ncb):
    xo_ref = refs[-1]
    if fused:
        xm_ref, f_ref, g2_ref = refs[:3]
        refs = refs[3:]
        x = xm_ref[...] + g2_ref[...] * _unpack_pairs(f_ref[...])
    else:
        ctx_ref, lat_ref = refs[:2]
        refs = refs[2:]
        x = jnp.where(pl.program_id(0) < ncb, ctx_ref[...], lat_ref[...])
    xo_ref[...] = x
    (a1_ref, b1_ref, cr_ref, sr_ref, cm_ref, sm_ref, w_ref, wuq_ref, wk_ref, wv_ref, qn_ref, kvn_ref,
     ones_ref, rq_ref, rk_ref, rv_ref, rg_ref, cb_ref, cu_ref, qm_ref, km_ref, vm_ref, gt_ref) = refs[:23]
    h = (_rms(x) * a1_ref[...] + b1_ref[...]).astype(BF16)

    def mm(off, width):
        return _dot(h, w_ref[:, off:off + width])

    cr = jnp.concatenate([cr_ref[...]] * (RQ_W // cr_ref.shape[1]), axis=1)
    sr = jnp.concatenate([sr_ref[...]] * (RQ_W // sr_ref.shape[1]), axis=1)
    q = mm(_O_RQ, RQ_W)
    rq_ref[...] = (q * cr + _rot_half(q, RET_DK // 2) * sr).astype(BF16)
    k = mm(_O_RK, RQ_W)
    rk_ref[...] = ((k * cr + _rot_half(k, RET_DK // 2) * sr) * (RET_DK ** -0.5)).astype(BF16)
    rv_ref[...] = mm(_O_RV, RV_W).astype(BF16)
    rg_ref[...] = _silu(mm(_O_RG, RV_W)).astype(BF16)
    cb_ref[...] = mm(_O_CB, CONV_WIDTH).astype(BF16)
    cu_ref[...] = (mm(_O_CC, CONV_WIDTH) * mm(_O_CX, CONV_WIDTH)).astype(BF16)

    cm = cm_ref[...]
    sm = sm_ref[...]
    qn = (_rms(mm(_O_QD, MLA_Q_RANK)) * qn_ref[...]).astype(BF16)
    for hd in range(MLA_HEADS):
        lo = hd * HEAD_PAD
        qa = _dot(qn, wuq_ref[:, lo:lo + HEAD_PAD])
        qm_ref[:, lo:lo + HEAD_PAD] = (qa * cm + _rot_half(qa, MLA_ROPE // 4) * sm).astype(BF16)

    kvn = (_rms(mm(_O_KVD, MLA_KV_RANK)) * kvn_ref[...]).astype(BF16)
    kr = mm(_O_KR, HEAD_PAD)
    kr = kr * cm + _rot_half(kr, MLA_ROPE // 4) * sm
    kn = _dot(kvn, wk_ref[...])
    for hd in range(MLA_HEADS):
        lo = hd * HEAD_PAD
        km_ref[:, lo:lo + HEAD_PAD] = (kn[:, lo:lo + HEAD_PAD] + kr).astype(BF16)
    vm_ref[...] = (_dot(kvn, wv_ref[...]) + ones_ref[...]).astype(BF16)

    for br in range(N_BRANCH):
        gt_ref[:, br * d_model:(br + 1) * d_model] = jax.nn.sigmoid(
            mm(_O_GT + br * d_model, d_model)).astype(BF16)


def _inproj_call(x_parts, a1, b1, tabs, wts, *, n_batch, nj, ncb):
    fused = len(x_parts) == 3
    d = x_parts[0].shape[1]
    rows = n_batch * nj * ROW_BLOCK
    tm = ROW_BLOCK
    row = lambda j, b: (b * nj + j, 0)
    mod = lambda j, b: (2 * b + (j >= ncb).astype(jnp.int32), 0, 0)
    tab = lambda j, b: (j, 0)
    if fused:
        in_specs = [pl.BlockSpec((tm, d), row), pl.BlockSpec((tm, d // 2), row),
                    pl.BlockSpec((None, 1, d), mod)]
    else:
        in_specs = [pl.BlockSpec((tm, d), lambda j, b: (jnp.where(j < ncb, b * ncb + j, 0), 0)),
                    pl.BlockSpec((tm, d),
                                 lambda j, b: (jnp.where(j < ncb, 0, b * (nj - ncb) + j - ncb), 0))]
    in_specs += [pl.BlockSpec((None, 1, d), mod), pl.BlockSpec((None, 1, d), mod)]
    in_specs += [pl.BlockSpec((tm, t.shape[1]), tab) for t in tabs]
    in_specs += [_const_spec(w.shape) for w in wts]
    widths = (RQ_W, RQ_W, RV_W, RV_W, CONV_WIDTH, CONV_WIDTH, MLA_W, MLA_W, MLA_W, N_BRANCH * d)
    out_shape = [jax.ShapeDtypeStruct((rows, w), BF16) for w in widths]
    out_specs = [pl.BlockSpec((tm, w), row) for w in widths]
    out_shape.append(jax.ShapeDtypeStruct((rows, d), F32))
    out_specs.append(pl.BlockSpec((tm, d), row))
    return pl.pallas_call(
        functools.partial(_inproj_body, d_model=d, fused=fused, ncb=ncb),
        out_shape=out_shape,
        grid=(nj, n_batch),
        in_specs=in_specs,
        out_specs=out_specs,
        compiler_params=_params(2),
        name="in_proj",
    )(*x_parts, a1, b1, *tabs, *wts)


def _ret_direction(q_ref, k_ref, v_ref, y_ref, s_ref, dm_ref, xi_ref, zt_ref, cd_ref, chunk_order):
    states = [s_ref[hd] for hd in range(RET_HEADS)]
    for c in chunk_order:
        rows = slice(c * RET_CHUNK, (c + 1) * RET_CHUNK)
        q = q_ref[rows, :]
        k = k_ref[rows, :]
        v = v_ref[rows, :]
        qx = (q.astype(F32) * xi_ref[...]).astype(BF16)
        kz = (k.astype(F32) * zt_ref[...]).astype(BF16)
        for hd in range(RET_HEADS):
            ks = slice(hd * RET_DK, (hd + 1) * RET_DK)
            vs = slice(hd * RET_DV, (hd + 1) * RET_DV)
            vh = v[:, vs]
            sc = lax.dot_general(q[:, ks], k[:, ks], (((1,), (1,)), ((), ())),
                                 preferred_element_type=F32)
            inner = _dot((sc * dm_ref[hd]).astype(BF16), vh)
            cross = _dot(qx[:, ks], states[hd].astype(BF16))
            y_ref[rows, vs] = inner + cross
            upd = lax.dot_general(kz[:, ks], vh, (((0,), (0,)), ((), ())),
                                  preferred_element_type=F32)
            states[hd] = cd_ref[hd] * states[hd] + upd
    for hd in range(RET_HEADS):
        s_ref[hd] = states[hd]


def _ret_body(qf_ref, kf_ref, vf_ref, qb_ref, kb_ref, vb_ref,
              dmf_ref, dmb_ref, xif_ref, ztf_ref, xib_ref, ztb_ref, cdf_ref, cdb_ref,
              yf_ref, yb_ref, sf_ref, sb_ref):
    @pl.when(pl.program_id(1) == 0)
    def _():
        sf_ref[...] = jnp.zeros_like(sf_ref)
        sb_ref[...] = jnp.zeros_like(sb_ref)

    n_chunks = ROW_BLOCK // RET_CHUNK
    _ret_direction(qf_ref, kf_ref, vf_ref, yf_ref, sf_ref, dmf_ref, xif_ref, ztf_ref, cdf_ref,
                   range(n_chunks))
    _ret_direction(qb_ref, kb_ref, vb_ref, yb_ref, sb_ref, dmb_ref, xib_ref, ztb_ref, cdb_ref,
                   range(n_chunks - 1, -1, -1))


def _ret_call(rq, rk, rv, consts, *, n_batch, nj, ncb):
    rows = rq.shape[0]
    tm = ROW_BLOCK
    fwd = lambda b, s: (b * nj + s, 0)

    def bwd(b, s):
        return (b * nj + jnp.where(s < ncb, ncb - 1 - s, nj - 1 - (s - ncb)), 0)

    specs = []
    for im in (fwd, bwd):
        specs += [pl.BlockSpec((tm, RQ_W), im), pl.BlockSpec((tm, RQ_W), im),
                  pl.BlockSpec((tm, RV_W), im)]
    specs += [_const_spec(c.shape) for c in consts]
    return pl.pallas_call(
        _ret_body,
        out_shape=[jax.ShapeDtypeStruct((rows, RV_W), F32)] * 2,
        grid=(n_batch, nj),
        in_specs=specs,
        out_specs=[pl.BlockSpec((tm, RV_W), fwd), pl.BlockSpec((tm, RV_W), bwd)],
        scratch_shapes=[pltpu.VMEM((RET_HEADS, RET_DK, RET_DV), F32)] * 2,
        compiler_params=_params(2),
        name="retention",
    )(rq, rk, rv, rq, rk, rv, *consts)


def _pick_tile(n, candidates):
    for c in candidates:
        if n % c == 0:
            return c
    raise ValueError(f"no tile for {n}")


def _attn_body(q_ref, k_ref, v_ref, o_ref, s_ref, *, n_ctx, n_all, ncb):
    tq = q_ref.shape[0]
    heads = [slice(hh * HEAD_PAD, (hh + 1) * HEAD_PAD) for hh in range(2)]

    def attend(n_keys, tk):
        nt = n_keys // tk
        qs = [q_ref[:, hs] for hs in heads]

        def qk(t, mrun):
            r0 = pl.multiple_of(t * tk, tk)
            out = []
            for hh, hs in enumerate(heads):
                s = lax.dot_general(qs[hh], k_ref[pl.ds(r0, tk), hs], (((1,), (1,)), ((), ())),
                                    preferred_element_type=F32)
                s_ref[hh, t, :, 0:tk] = s
                m = mrun[hh]
                for cc in range(tk // LANES):
                    m = jnp.maximum(m, s[:, cc * LANES:(cc + 1) * LANES])
                out.append(m)
            return tuple(out)

        mrun = lax.fori_loop(0, nt, qk, tuple(jnp.full((tq, LANES), -jnp.inf, F32) for _ in heads),
                             unroll=True)
        mrow = [jnp.max(m, axis=-1, keepdims=True) for m in mrun]

        def pv(t, accs):
            r0 = pl.multiple_of(t * tk, tk)
            out = []
            for hh, hs in enumerate(heads):
                p = jnp.exp2(s_ref[hh, t, :, 0:tk] - mrow[hh]).astype(BF16)
                out.append(accs[hh] + _dot(p, v_ref[pl.ds(r0, tk), hs]))
            return tuple(out)

        accs = lax.fori_loop(0, nt, pv, tuple(jnp.zeros((tq, HEAD_PAD), F32) for _ in heads),
                             unroll=True)
        o_ref[...] = jnp.concatenate([a[:, :MLA_V] / a[:, MLA_V:MLA_V + 1] for a in accs],
                                     axis=-1).astype(BF16)

    j = pl.program_id(2)

    @pl.when(j < ncb)
    def _():
        attend(n_ctx, _pick_tile(n_ctx, (256, 128)))

    @pl.when(j >= ncb)
    def _():
        attend(n_all, _pick_tile(n_all, ATTN_KEY_TILES))


def _attn_call(qm, km, vm, *, n_batch, nj, ncb, n_ctx):
    rows = qm.shape[0]
    tm = ROW_BLOCK
    t_all = nj * tm
    tk = _pick_tile(t_all, ATTN_KEY_TILES)
    qmap = lambda b, hp, j: (b * nj + j, hp)
    kmap = lambda b, hp, j: (b, hp)
    return pl.pallas_call(
        functools.partial(_attn_body, n_ctx=n_ctx, n_all=t_all, ncb=ncb),
        out_shape=jax.ShapeDtypeStruct((rows, MLA_O), BF16),
        grid=(n_batch, MLA_HEADS // 2, nj),
        in_specs=[pl.BlockSpec((tm, 2 * HEAD_PAD), qmap),
                  pl.BlockSpec((t_all, 2 * HEAD_PAD), kmap),
                  pl.BlockSpec((t_all, 2 * HEAD_PAD), kmap)],
        out_specs=pl.BlockSpec((tm, 2 * MLA_V), qmap),
        scratch_shapes=[pltpu.VMEM((2, t_all // tk, tm, tk), F32)],
        compiler_params=_params(3),
        name="mla_attention",
    )(qm, km, vm)


def _merge_body(yf_ref, yb_ref, rg_ref, cb_ref, cu_ref, cup_ref, cun_ref, om_ref, gt_ref, x_ref,
                g1_ref, a2_ref, b2_ref, gn_ref, cw_ref, wro_ref, wco_ref, wmo_ref, wout_ref,
                wrh_ref, wrl_ref, rb_ref, xmid_ref, h2_ref, rt_ref, gw_ref, lg_ref, *, nj, ncb, d_model,
                nb):
    tm = x_ref.shape[0]
    step = pl.program_id(0)

    @pl.when(step == 0)
    def _():
        lg_ref[...] = jnp.zeros_like(lg_ref)

    cls, w_lo, w_hi = _route_rows(jnp.transpose(lg_ref[...])[:N_EXPERTS, :], rb_ref[...])
    rank, counts = _block_ranks(cls.astype(F32))
    rt_ref[...] = jnp.concatenate(
        [cls.astype(F32), rank, counts, jnp.zeros((ROUTE_ROWS - 3, tm), F32)], axis=0)
    gw_ref[...] = jnp.transpose(
        jnp.concatenate([w_lo, w_hi, jnp.zeros((LANES - 2, tm), F32)], axis=0))

    j = jnp.minimum(step, nb - 1) % nj
    seg_first = jnp.logical_or(j == 0, j == ncb)
    seg_last = jnp.logical_or(j == ncb - 1, j == nj - 1)

    y = yf_ref[...] + yb_ref[...]
    yn = jnp.concatenate([_rms(y[:, hd * RET_DV:(hd + 1) * RET_DV]) for hd in range(RET_HEADS)],
                         axis=-1) * gn_ref[...]
    y_ret = _dot((rg_ref[...].astype(F32) * yn).astype(BF16), wro_ref[...])

    u = cu_ref[...].astype(F32)
    ridx = lax.broadcasted_iota(jnp.int32, u.shape, 0)
    prev_row = cup_ref[...].astype(F32)[BF16_SUBLANES - 1:, :] * jnp.where(seg_first, 0.0, 1.0)
    next_row = cun_ref[...].astype(F32)[0:1, :] * jnp.where(seg_last, 0.0, 1.0)
    u_prev = jnp.where(ridx == 0, prev_row, pltpu.roll(u, 1, 0))
    u_next = jnp.where(ridx == tm - 1, next_row, pltpu.roll(u, tm - 1, 0))
    conv = u_prev * cw_ref[0:1, :] + u * cw_ref[1:2, :] + u_next * cw_ref[2:3, :]
    y_conv = _dot((cb_ref[...].astype(F32) * conv).astype(BF16), wco_ref[...])

    y_mla = _dot(om_ref[...], wmo_ref[...])

    merged = (gt_ref[:, 0:d_model].astype(F32) * y_ret
              + gt_ref[:, d_model:2 * d_model].astype(F32) * y_conv
              + gt_ref[:, 2 * d_model:3 * d_model].astype(F32) * y_mla)
    x_mid = x_ref[...] + g1_ref[...] * _dot(merged.astype(BF16), wout_ref[...])
    xmid_ref[...] = x_mid

    h2 = _rms(x_mid) * a2_ref[...] + b2_ref[...]
    h_hi, h_lo = _split_bf16(h2)
    lg_ref[...] = _dot(h_hi, wrh_ref[...]) + _dot(h_hi, wrl_ref[...]) + _dot(h_lo, wrh_ref[...])
    h2_ref[...] = _pack_pairs(h_hi)


def _top2_of4(v):
    def first_max(rows):
        best, idx = rows[0], jnp.zeros(rows[0].shape, jnp.int32)
        for e in range(1, len(rows)):
            better = rows[e] > best
            idx = jnp.where(better, e, idx)
            best = jnp.where(better, rows[e], best)
        return best, idx

    b1, i1 = first_max(v)
    b2, i2 = first_max([jnp.where(i1 == e, -jnp.inf, v[e]) for e in range(len(v))])
    return i1, i2, b1, b2


def _route_rows(logits_t, bias):
    scores = jax.nn.sigmoid(logits_t)
    biased = scores + bias
    row = lambda a, e: a[e:e + 1, :]
    best = None
    for g in range(N_GROUPS):
        v = [row(biased, g * EXPERTS_PER_GROUP + e) for e in range(EXPERTS_PER_GROUP)]
        i1, i2, b1, b2 = _top2_of4(v)
        cand = (b1 + b2, jnp.full(i1.shape, g, jnp.int32), i1, i2)
        if best is None:
            best = cand
        else:
            better = cand[0] > best[0]
            best = tuple(jnp.where(better, c, o) for c, o in zip(cand, best))
    _, g_sel, i1, i2 = best
    lo = jnp.minimum(i1, i2)
    hi = jnp.maximum(i1, i2)
    e_lo = g_sel * EXPERTS_PER_GROUP + lo
    e_hi = g_sel * EXPERTS_PER_GROUP + hi
    s_lo = jnp.zeros_like(best[0])
    s_hi = jnp.zeros_like(best[0])
    for e in range(N_EXPERTS):
        s_lo = jnp.where(e_lo == e, row(scores, e), s_lo)
        s_hi = jnp.where(e_hi == e, row(scores, e), s_hi)
    total = s_lo + s_hi
    pair_base = jnp.where(lo == 0, 0, jnp.where(lo == 1, 3, 5))
    cls = g_sel * len(PAIRS) + pair_base + (hi - lo - 1)
    return cls, s_lo / total, s_hi / total


def _block_ranks(cls_row):
    n = cls_row.shape[1]
    cls_col = jnp.transpose(jnp.broadcast_to(cls_row, (LANES, n)))[:, 0:1]
    ii = lax.broadcasted_iota(jnp.int32, (n, n), 0)
    jj = lax.broadcasted_iota(jnp.int32, (n, n), 1)
    earlier_same = jnp.logical_and(cls_col == cls_row, ii < jj)
    rank = jnp.sum(jnp.where(earlier_same, 1.0, 0.0), axis=0, keepdims=True)
    lane = lax.broadcasted_iota(jnp.int32, (1, n), 1)
    counts = jnp.zeros((1, n), F32)
    for c in range(N_CLASSES):
        cnt = jnp.sum(jnp.where(cls_row == c, 1.0, 0.0), axis=1, keepdims=True)
        counts = jnp.where(lane == c, cnt, counts)
    return rank, counts


def _merge_call(yf, yb, rg, cb, cu, om, gt, xa, g1, a2, b2, wts, *, nj, ncb):
    rows, d = xa.shape
    tm = ROW_BLOCK
    nb = rows // tm
    halo = BF16_SUBLANES
    per_blk = tm // halo
    blk = lambda i: jnp.minimum(i, nb - 1)
    row = lambda i: (blk(i), 0)
    mod = lambda i: (2 * (blk(i) // nj) + ((blk(i) % nj) >= ncb).astype(jnp.int32), 0, 0)
    prev = lambda i: (jnp.maximum(blk(i) * per_blk - 1, 0), 0)
    nxt = lambda i: (jnp.minimum((blk(i) + 1) * per_blk, rows // halo - 1), 0)
    in_specs = [pl.BlockSpec((tm, RV_W), row), pl.BlockSpec((tm, RV_W), row),
                pl.BlockSpec((tm, RV_W), row),
                pl.BlockSpec((tm, CONV_WIDTH), row), pl.BlockSpec((tm, CONV_WIDTH), row),
                pl.BlockSpec((halo, CONV_WIDTH), prev), pl.BlockSpec((halo, CONV_WIDTH), nxt),
                pl.BlockSpec((tm, MLA_O), row), pl.BlockSpec((tm, N_BRANCH * d), row),
                pl.BlockSpec((tm, d), row),
                pl.BlockSpec((None, 1, d), mod), pl.BlockSpec((None, 1, d), mod),
                pl.BlockSpec((None, 1, d), mod)]
    in_specs += [_const_spec(w.shape) for w in wts]
    return pl.pallas_call(
        functools.partial(_merge_body, nj=nj, ncb=ncb, d_model=d, nb=nb),
        out_shape=[jax.ShapeDtypeStruct((rows, d), F32), jax.ShapeDtypeStruct((rows, d // 2), jnp.uint32),
                   jax.ShapeDtypeStruct((nb * ROUTE_ROWS, tm), F32),
                   jax.ShapeDtypeStruct((rows, LANES), F32)],
        grid=(nb + 1,),
        in_specs=in_specs,
        out_specs=[pl.BlockSpec((tm, d), row), pl.BlockSpec((tm, d // 2), row),
                   pl.BlockSpec((ROUTE_ROWS, tm), lambda i: (jnp.maximum(i - 1, 0), 0)),
                   pl.BlockSpec((tm, LANES), lambda i: (jnp.maximum(i - 1, 0), 0))],
        scratch_shapes=[pltpu.VMEM((tm, LANES), F32)],
        compiler_params=_params(1),
        name="merge_out_proj",
    )(yf, yb, rg, cb, cu, cu, cu, om, gt, xa, g1, a2, b2, *wts)


def _moe_body(ea_ref, eb_ref, nu_ref, h_ref, gw_ref, w1a_ref, w3a_ref, w2a_ref, w1b_ref, w3b_ref,
              w2b_ref, o_ref, w13a_s, w2a_s, w13b_s, w2b_s):
    t = pl.program_id(0)
    used = t < nu_ref[0]
    prev = jnp.maximum(t - 1, 0)

    def refresh(e_ref, w1_ref, w3_ref, w2_ref, w13_s, w2_s):
        @pl.when(jnp.logical_and(used, jnp.logical_or(t == 0, e_ref[t] != e_ref[prev])))
        def _():
            w13_s[:, :D_EXPERT] = w1_ref[...].astype(BF16)
            w13_s[:, D_EXPERT:] = w3_ref[...].astype(BF16)
            w2_s[...] = w2_ref[...].astype(BF16)

    refresh(ea_ref, w1a_ref, w3a_ref, w2a_ref, w13a_s, w2a_s)
    refresh(eb_ref, w1b_ref, w3b_ref, w2b_ref, w13b_s, w2b_s)

    @pl.when(used)
    def _():
        h = _unpack_pairs(h_ref[...]).astype(BF16)
        gw = gw_ref[...]

        def expert(w13_s, w2_s, wt):
            a = _dot(h, w13_s[...])
            act = _silu(a[:, :D_EXPERT]) * a[:, D_EXPERT:] * wt
            return _dot(act.astype(BF16), w2_s[...])

        o_ref[...] = _pack_pairs(expert(w13a_s, w2a_s, gw[:, 0:1])
                                 + expert(w13b_s, w2b_s, gw[:, 1:2]))

    @pl.when(jnp.logical_not(used))
    def _():
        o_ref[...] = jnp.zeros_like(o_ref)


def _moe_call(tile_ea, tile_eb, n_used, hs, gw, w1, w3, w2, layer):
    npad = hs.shape[0]
    d = w1.shape[2]
    tmo = MOE_TILE
    row = lambda t, ea, eb, nu: (t, 0)
    wa = lambda t, ea, eb, nu: (layer, ea[t], 0, 0)
    wb = lambda t, ea, eb, nu: (layer, eb[t], 0, 0)
    up = lambda im: pl.BlockSpec((None, None, d, D_EXPERT), im)
    down = lambda im: pl.BlockSpec((None, None, D_EXPERT, d), im)
    grid_spec = pltpu.PrefetchScalarGridSpec(
        num_scalar_prefetch=3,
        grid=(npad // tmo,),
        in_specs=[pl.BlockSpec((tmo, d // 2), row), pl.BlockSpec((tmo, LANES), row),
                  up(wa), up(wa), down(wa), up(wb), up(wb), down(wb)],
        out_specs=pl.BlockSpec((tmo, d // 2), row),
        scratch_shapes=[pltpu.VMEM((d, 2 * D_EXPERT), BF16), pltpu.VMEM((D_EXPERT, d), BF16)] * 2)
    return pl.pallas_call(
        _moe_body,
        out_shape=jax.ShapeDtypeStruct((npad, d // 2), jnp.uint32),
        grid_spec=grid_spec,
        compiler_params=_params(1),
        name="moe_experts",
    )(tile_ea, tile_eb, n_used, hs, gw, w1, w3, w2, w1, w3, w2)


def _dispatch(route, n_tok):
    tmo = MOE_TILE
    n_tiles = n_tok // tmo + N_CLASSES
    npad = n_tiles * tmo
    cls = route[:, 0, :].astype(jnp.int32)
    rank = route[:, 1, :].astype(jnp.int32)
    counts = route[:, 2, :N_CLASSES].astype(jnp.int32)
    tiles_per = (jnp.sum(counts, axis=0) + tmo - 1) // tmo
    tile_end = jnp.cumsum(tiles_per)
    offs = (tile_end - tiles_per) * tmo
    block_base = offs[None, :] + jnp.cumsum(counts, axis=0) - counts
    classes = jnp.arange(N_CLASSES, dtype=jnp.int32)
    base = jnp.sum(jnp.where(cls[:, :, None] == classes, block_base[:, None, :], 0), axis=-1)
    dest = (base + rank).reshape(-1)
    tile_ids = jnp.arange(n_tiles, dtype=jnp.int32)
    tile_cls = jnp.sum((tile_end[None, :] <= tile_ids[:, None]).astype(jnp.int32), axis=1)
    tile_cls = jnp.minimum(tile_cls, N_CLASSES - 1)
    pa = jnp.asarray([p[0] for p in PAIRS], jnp.int32)
    pb = jnp.asarray([p[1] for p in PAIRS], jnp.int32)
    group = (tile_cls // len(PAIRS)) * EXPERTS_PER_GROUP
    tile_ea = group + pa[tile_cls % len(PAIRS)]
    tile_eb = group + pb[tile_cls % len(PAIRS)]
    n_used = tile_end[-1:].astype(jnp.int32)
    return dest, npad, tile_ea, tile_eb, n_used


def _sc_gather_rows(table, idx):
    d = table.shape[1]
    m = idx.shape[0]
    per_worker = m // SC_WORKERS
    assert m % (SC_WORKERS * GATHER_CHUNK) == 0 and table.dtype.itemsize == 4
    mesh = plsc.VectorSubcoreMesh(core_axis_name="c", subcore_axis_name="s",
                                  num_cores=SC_CORES, num_subcores=SC_SUBCORES)

    @functools.partial(
        pl.kernel, mesh=mesh,
        out_type=jax.ShapeDtypeStruct((m, d), table.dtype),
        scratch_types=[pltpu.VMEM((GATHER_CHUNK,), jnp.int32),
                       pltpu.VMEM((GATHER_CHUNK, d), table.dtype),
                       pltpu.SemaphoreType.DMA],
        name="sc_row_gather")
    def gather(table_hbm, idx_hbm, out_hbm, idx_v, rows_v, sem):
        worker = lax.axis_index("s") * SC_CORES + lax.axis_index("c")

        @pl.loop(0, per_worker // GATHER_CHUNK)
        def _(chunk):
            off = worker * per_worker + chunk * GATHER_CHUNK
            pltpu.sync_copy(idx_hbm.at[pl.ds(off, GATHER_CHUNK)], idx_v)
            pltpu.async_copy(table_hbm.at[idx_v], rows_v, sem).wait()
            pltpu.sync_copy(rows_v, out_hbm.at[pl.ds(off, GATHER_CHUNK)])

    return gather(table, idx)


def _sc_scatter_rows(rows, idx, n_out):
    m, d = rows.shape
    per_worker = m // SC_WORKERS
    assert m % (SC_WORKERS * GATHER_CHUNK) == 0 and rows.dtype.itemsize == 4
    mesh = plsc.VectorSubcoreMesh(core_axis_name="c", subcore_axis_name="s",
                                  num_cores=SC_CORES, num_subcores=SC_SUBCORES)

    @functools.partial(
        pl.kernel, mesh=mesh,
        out_type=jax.ShapeDtypeStruct((n_out, d), rows.dtype),
        scratch_types=[pltpu.VMEM((GATHER_CHUNK,), jnp.int32),
                       pltpu.VMEM((GATHER_CHUNK, d), rows.dtype),
                       pltpu.SemaphoreType.DMA],
        name="sc_row_scatter")
    def scatter(rows_hbm, idx_hbm, out_hbm, idx_v, rows_v, sem):
        worker = lax.axis_index("s") * SC_CORES + lax.axis_index("c")

        @pl.loop(0, per_worker // GATHER_CHUNK)
        def _(chunk):
            off = worker * per_worker + chunk * GATHER_CHUNK
            pltpu.sync_copy(idx_hbm.at[pl.ds(off, GATHER_CHUNK)], idx_v)
            pltpu.sync_copy(rows_hbm.at[pl.ds(off, GATHER_CHUNK)], rows_v)
            pltpu.async_copy(rows_v, out_hbm.at[idx_v], sem).wait()

    return scatter(rows, idx)


def _final_body(x_ref, f_ref, g2_ref, fn_ref, o_ref):
    o_ref[...] = _rms(x_ref[...] + g2_ref[...] * _unpack_pairs(f_ref[...])) * fn_ref[...]


def _final_call(xmid, fg, g2, final_norm, *, n_batch, nj, ncb):
    rows, d = xmid.shape
    tm = ROW_BLOCK
    njl = nj - ncb
    src = lambda b, j: (b * nj + ncb + j, 0)
    return pl.pallas_call(
        _final_body,
        out_shape=jax.ShapeDtypeStruct((n_batch * njl * tm, d), F32),
        grid=(n_batch, njl),
        in_specs=[pl.BlockSpec((tm, d), src), pl.BlockSpec((tm, d // 2), src),
                  pl.BlockSpec((None, 1, d), lambda b, j: (2 * b + 1, 0, 0)),
                  _const_spec((1, d))],
        out_specs=pl.BlockSpec((tm, d), lambda b, j: (b * njl + j, 0)),
        compiler_params=_params(2),
        name="final_norm",
    )(xmid, fg, g2, final_norm.reshape(1, d))


def _split_cols(w, sizes):
    out, off = [], 0
    for s in sizes:
        out.append(w[:, off:off + s])
        off += s
    out.append(w[:, off:])
    return out


def _layer_weights(w_in, w_uq, w_ukv, q_norm, kv_norm):
    wq, wk, wv, wg, wcb, wcc, wcx, wqd, wkvd, wkr, wgate = _split_cols(w_in, IN_SIZES[:-1])
    rope_lanes = lambda w: jnp.pad(w, ((0, 0), (MLA_NOPE, HEAD_PAD - MLA_NOPE - MLA_ROPE)))
    w_ext = jnp.concatenate(
        [wq, wk, wv, wg, wcb, wcc, wcx, wqd, wkvd, rope_lanes(wkr), wgate], axis=1).astype(BF16)

    uq = w_uq.reshape(MLA_Q_RANK, MLA_HEADS, MLA_NOPE + MLA_ROPE)
    tail = HEAD_PAD - MLA_NOPE - MLA_ROPE
    uq_pad = jnp.pad(uq, ((0, 0), (0, 0), (0, tail)))
    wuq = uq_pad.reshape(MLA_Q_RANK, MLA_W).astype(BF16)

    ukv = w_ukv.reshape(MLA_KV_RANK, MLA_HEADS, MLA_NOPE + MLA_V)
    wk_up = jnp.pad(ukv[..., :MLA_NOPE], ((0, 0), (0, 0), (0, HEAD_PAD - MLA_NOPE)))
    wk_up = wk_up.reshape(MLA_KV_RANK, MLA_W).astype(BF16)
    wv_pad = jnp.pad(ukv[..., MLA_NOPE:], ((0, 0), (0, 0), (0, HEAD_PAD - MLA_V)))
    wv_pad = wv_pad.reshape(MLA_KV_RANK, MLA_W).astype(BF16)
    ones_row = jnp.zeros((MLA_HEADS, HEAD_PAD), F32).at[:, MLA_V].set(1.0).reshape(1, MLA_W)
    q_gain = q_norm.astype(F32) * ((MLA_NOPE + MLA_ROPE) ** -0.5 * math.log2(math.e))
    return (w_ext, wuq, wk_up, wv_pad, q_gain.reshape(1, -1),
            kv_norm.reshape(1, -1).astype(F32), ones_row)


def _rotary_tables(n_ctx, n_lat):
    n_all = n_ctx + n_lat
    row = lax.broadcasted_iota(jnp.int32, (n_all, LANES), 0)
    lane = lax.broadcasted_iota(jnp.int32, (n_all, LANES), 1)
    is_ctx = row < n_ctx
    pos = row - n_ctx
    grid_row = pos // GRID_W
    grid_col = pos - grid_row * GRID_W

    def inv_freq(idx, half):
        return ROPE_BASE ** (-idx.astype(F32) / half)

    def table(ang, first_half, live):
        cos = jnp.where(is_ctx, 1.0, jnp.cos(ang))
        sin = jnp.where(is_ctx, 0.0, jnp.where(first_half, -jnp.sin(ang), jnp.sin(ang)))
        return jnp.where(live, cos, 0.0), jnp.where(live, sin, 0.0)

    half = RET_DK // 2
    ang = pos.astype(F32) * inv_freq(lane % half, half)
    cr, sr = table(ang, (lane % RET_DK) < half, True)

    quarter = MLA_ROPE // 4
    rl = lane - MLA_NOPE
    coord = jnp.where(rl < MLA_ROPE // 2, grid_row, grid_col)
    ang = coord.astype(F32) * inv_freq(rl % quarter, quarter)
    cm, sm = table(ang, (rl % (2 * quarter)) < quarter,
                   jnp.logical_and(rl >= 0, rl < MLA_ROPE))
    cm = jnp.where(rl < 0, 1.0, cm)
    return cr, sr, cm, sm


def _retention_consts(ret_decay):
    log_gf = jax.nn.log_sigmoid(ret_decay[0].astype(F32))
    log_gb = jax.nn.log_sigmoid(ret_decay[1].astype(F32))
    idx = jnp.arange(RET_CHUNK, dtype=F32)
    rel = idx[:, None] - idx[None, :]
    dm_f = jnp.where(rel >= 0, jnp.exp(log_gf[:, None, None] * jnp.maximum(rel, 0.0)[None]), 0.0)
    dm_b = jnp.where(rel < 0, jnp.exp(log_gb[:, None, None] * jnp.maximum(-rel, 0.0)[None]), 0.0)

    def lanes(t):
        return jnp.repeat(t, RET_DK, axis=1)

    xi_f = lanes(jnp.exp(log_gf[None, :] * (idx + 1.0)[:, None]))
    zt_f = lanes(jnp.exp(log_gf[None, :] * (RET_CHUNK - 1 - idx)[:, None]))
    xi_b = lanes(jnp.exp(log_gb[None, :] * (RET_CHUNK - idx)[:, None]))
    zt_b = lanes(jnp.exp(log_gb[None, :] * idx[:, None]))
    cd = lambda lg: jnp.broadcast_to(jnp.exp(lg * RET_CHUNK)[:, None, None], (RET_HEADS, 1, RET_DV))
    return dm_f, dm_b, xi_f, zt_f, xi_b, zt_b, cd(log_gf), cd(log_gb)


def kernel(x, c, ctx, c_ctx, w_ada, b_ada, norm1, norm2, w_in, ret_decay, ret_gn, w_ret_o, conv_w,
           w_conv_o, mla_q_norm, w_uq, mla_kv_norm, w_ukv, w_mla_o, w_out, w_router, router_bias,
           w1, w3, w2, final_norm):
    n_batch, n_lat, d = x.shape
    n_ctx = ctx.shape[1]
    depth = w_ada.shape[0]
    t_all = n_ctx + n_lat
    assert n_ctx % ROW_BLOCK == 0 and n_lat % ROW_BLOCK == 0 and n_lat % GRID_W == 0
    nj = t_all // ROW_BLOCK
    ncb = n_ctx // ROW_BLOCK
    n_tok = n_batch * t_all
    assert n_tok % MOE_TILE == 0
    geom = dict(n_batch=n_batch, nj=nj, ncb=ncb)

    cc = jnp.concatenate([c, c_ctx[None, :]], axis=0)
    cc = jnp.pad(cc, ((0, -cc.shape[0] % 8), (0, 0)))
    mod = _ada_call(cc, w_ada, b_ada)[:, :n_batch + 1].reshape(depth, n_batch + 1, N_MOD, d)
    pick = jnp.stack([jnp.full((n_batch,), n_batch, jnp.int32),
                      jnp.arange(n_batch, dtype=jnp.int32)], axis=1).reshape(-1)
    mod = mod[:, pick]

    tabs = _rotary_tables(n_ctx, n_lat)
    wr_hi, wr_lo = _split_bf16(jnp.pad(w_router.astype(F32), ((0, 0), (0, LANES - N_EXPERTS))))
    rbias = router_bias.astype(F32).reshape(N_EXPERTS, 1)

    out = None
    x_parts = (ctx.reshape(n_batch * n_ctx, d), x.reshape(n_batch * n_lat, d))
    for l in range(depth):
        m = mod[l]
        rowvec = lambda v: v.reshape(2 * n_batch, 1, d)
        a1 = rowvec(norm1[l][None, :] * (1.0 + m[:, 1]))
        b1 = rowvec(m[:, 0])
        g1 = rowvec(m[:, 2])
        a2 = rowvec(norm2[l][None, :] * (1.0 + m[:, 4]))
        b2 = rowvec(m[:, 3])
        g2 = rowvec(m[:, 5])

        wts = _layer_weights(w_in[l], w_uq[l], w_ukv[l], mla_q_norm[l], mla_kv_norm[l])
        proj = _inproj_call(x_parts, a1, b1, tabs, wts, **geom)
        rq, rk, rv, rg, cb, cu, qm, km, vm, gt, xa = proj

        yf, yb = _ret_call(rq, rk, rv, _retention_consts(ret_decay[l]), **geom)
        om = _attn_call(qm, km, vm, n_ctx=n_ctx, **geom)

        merge_wts = (ret_gn[l].reshape(1, -1).astype(F32), conv_w[l].T.astype(F32),
                     w_ret_o[l].astype(BF16), w_conv_o[l].astype(BF16), w_mla_o[l].astype(BF16),
                     w_out[l].astype(BF16), wr_hi, wr_lo, rbias)
        xmid, h2, route, gate_w = _merge_call(yf, yb, rg, cb, cu, om, gt, xa, g1, a2, b2, merge_wts,
                                      nj=nj, ncb=ncb)
        route = route.reshape(-1, ROUTE_ROWS, ROW_BLOCK)
        dest, npad, tile_ea, tile_eb, n_used = _dispatch(route, n_tok)
        hs = _sc_scatter_rows(h2, dest, npad)
        gw = _sc_scatter_rows(gate_w, dest, npad)
        f_sorted = _moe_call(tile_ea, tile_eb, n_used, hs, gw, w1, w3, w2, l)
        fg = _sc_gather_rows(f_sorted, dest)

        if l < depth - 1:
            x_parts = (xmid, fg, g2)
        else:
            out = _final_call(xmid, fg, g2, final_norm, **geom)
    return out.reshape(n_batch, n_lat, d)
```

```python
import functools
import math

import jax
import jax.numpy as jnp
from jax import lax
from jax.experimental import pallas as pl
from jax.experimental.pallas import tpu as pltpu
from jax.experimental.pallas import tpu_sc as plsc

F32 = jnp.float32
BF16 = jnp.bfloat16

GRID_W = 64
RMS_EPS = 1e-6
ROPE_BASE = 10000.0
N_MOD = 6
RET_HEADS = 4
RET_DK = 64
RET_DV = 128
RET_CHUNK = 256
CONV_WIDTH = 512
MLA_HEADS = 8
MLA_Q_RANK = 384
MLA_KV_RANK = 256
MLA_NOPE = 64
MLA_ROPE = 32
MLA_V = 64
N_BRANCH = 3
N_EXPERTS = 16
N_GROUPS = 4
EXPERTS_PER_GROUP = N_EXPERTS // N_GROUPS
D_EXPERT = 512
IN_SIZES = (RET_HEADS * RET_DK, RET_HEADS * RET_DK, RET_HEADS * RET_DV, RET_HEADS * RET_DV,
            CONV_WIDTH, CONV_WIDTH, CONV_WIDTH, MLA_Q_RANK, MLA_KV_RANK, MLA_ROPE, 0)

LANES = 128
BF16_SUBLANES = 16
VMEM_LIMIT = 56 * 1024 * 1024
SC_CORES = 2
SC_SUBCORES = 16
SC_WORKERS = SC_CORES * SC_SUBCORES

HEAD_PAD = LANES
ROW_BLOCK = 256
MOE_TILE = 256
ATTN_KEY_TILES = (768, 512, 256, 128)
ROUTE_ROWS = 8
GATHER_CHUNK = 96
PAIRS = [(a, b) for a in range(EXPERTS_PER_GROUP) for b in range(a + 1, EXPERTS_PER_GROUP)]
N_CLASSES = N_GROUPS * len(PAIRS)

RQ_W = RET_HEADS * RET_DK
RV_W = RET_HEADS * RET_DV
MLA_W = MLA_HEADS * HEAD_PAD
MLA_O = MLA_HEADS * MLA_V


def _const_spec(shape):
    nd = len(shape)
    return pl.BlockSpec(shape, lambda *_: (0,) * nd, pipeline_mode=pl.Buffered(1))


def _params(n_axes):
    return pltpu.CompilerParams(dimension_semantics=("arbitrary",) * n_axes,
                                vmem_limit_bytes=VMEM_LIMIT)


def _dot(a, b):
    return jnp.dot(a, b, preferred_element_type=F32)


def _split_bf16(a):
    hi = a.astype(BF16)
    lo = (a - hi.astype(F32)).astype(BF16)
    return hi, lo


def _silu(v):
    return v * jax.nn.sigmoid(v)


def _rms(v):
    return v * lax.rsqrt(jnp.mean(v * v, axis=-1, keepdims=True) + RMS_EPS)


def _pack_pairs(v):
    half = v.shape[1] // 2
    bits = lambda t: lax.bitcast_convert_type(t.astype(BF16).astype(F32), jnp.uint32)
    return (bits(v[:, :half]) & jnp.uint32(0xFFFF0000)) | (bits(v[:, half:]) >> 16)


def _unpack_pairs(u):
    hi = lax.bitcast_convert_type(u & jnp.uint32(0xFFFF0000), F32)
    lo = lax.bitcast_convert_type(u << 16, F32)
    return jnp.concatenate([hi, lo], axis=1)


def _ada_body(c_ref, w_ref, b_ref, o_ref):
    a_hi, a_lo = _split_bf16(_silu(c_ref[...]))
    w_hi, w_lo = _split_bf16(w_ref[...])
    o_ref[...] = _dot(a_hi, w_hi) + _dot(a_hi, w_lo) + _dot(a_lo, w_hi) + b_ref[...]


def _ada_call(cc, w_ada, b_ada):
    depth, d, nm = w_ada.shape
    rows = cc.shape[0]
    cb = nm // 4
    return pl.pallas_call(
        _ada_body,
        out_shape=jax.ShapeDtypeStruct((depth, rows, nm), F32),
        grid=(depth, nm // cb),
        in_specs=[pl.BlockSpec((rows, d), lambda l, n: (0, 0)),
                  pl.BlockSpec((None, d, cb), lambda l, n: (l, 0, n)),
                  pl.BlockSpec((None, 1, cb), lambda l, n: (l, 0, n))],
        out_specs=pl.BlockSpec((None, rows, cb), lambda l, n: (l, 0, n)),
        compiler_params=_params(2),
        name="ada_mod",
    )(cc, w_ada, b_ada.reshape(depth, 1, nm))


_O_RQ = 0
_O_RK = _O_RQ + RQ_W
_O_RV = _O_RK + RQ_W
_O_RG = _O_RV + RV_W
_O_CB = _O_RG + RV_W
_O_CC = _O_CB + CONV_WIDTH
_O_CX = _O_CC + CONV_WIDTH
_O_QD = _O_CX + CONV_WIDTH
_O_KVD = _O_QD + MLA_Q_RANK
_O_KR = _O_KVD + MLA_KV_RANK
_O_GT = _O_KR + LANES


def _rot_half(v, half):
    width = v.shape[1]
    lane = lax.broadcasted_iota(jnp.int32, v.shape, 1)
    first = (lane % (2 * half)) < half
    return jnp.where(first, pltpu.roll(v, width - half, 1), pltpu.roll(v, half, 1))


def _inproj_body(*refs, d_model, fused, ncb):
    xo_ref = refs[-1]
    if fused:
        xm_ref, f_ref, g2_ref = refs[:3]
        refs = refs[3:]
        x = xm_ref[...] + g2_ref[...] * _unpack_pairs(f_ref[...])
    else:
        ctx_ref, lat_ref = refs[:2]
        refs = refs[2:]
        x = jnp.where(pl.program_id(0) < ncb, ctx_ref[...], lat_ref[...])
    xo_ref[...] = x
    (a1_ref, b1_ref, cr_ref, sr_ref, cm_ref, sm_ref, w_ref, wuq_ref, wk_ref, wv_ref, qn_ref, kvn_ref,
     ones_ref, rq_ref, rk_ref, rv_ref, rg_ref, cb_ref, cu_ref, qm_ref, km_ref, vm_ref, gt_ref) = refs[:23]
    h = (_rms(x) * a1_ref[...] + b1_ref[...]).astype(BF16)

    def mm(off, width):
        return _dot(h, w_ref[:, off:off + width])

    cr = jnp.concatenate([cr_ref[...]] * (RQ_W // cr_ref.shape[1]), axis=1)
    sr = jnp.concatenate([sr_ref[...]] * (RQ_W // sr_ref.shape[1]), axis=1)
    q = mm(_O_RQ, RQ_W)
    rq_ref[...] = (q * cr + _rot_half(q, RET_DK // 2) * sr).astype(BF16)
    k = mm(_O_RK, RQ_W)
    rk_ref[...] = ((k * cr + _rot_half(k, RET_DK // 2) * sr) * (RET_DK ** -0.5)).astype(BF16)
    rv_ref[...] = mm(_O_RV, RV_W).astype(BF16)
    rg_ref[...] = _silu(mm(_O_RG, RV_W)).astype(BF16)
    cb_ref[...] = mm(_O_CB, CONV_WIDTH).astype(BF16)
    cu_ref[...] = (mm(_O_CC, CONV_WIDTH) * mm(_O_CX, CONV_WIDTH)).astype(BF16)

    cm = cm_ref[...]
    sm = sm_ref[...]
    qn = (_rms(mm(_O_QD, MLA_Q_RANK)) * qn_ref[...]).astype(BF16)
    for hd in range(MLA_HEADS):
        lo = hd * HEAD_PAD
        qa = _dot(qn, wuq_ref[:, lo:lo + HEAD_PAD])
        qm_ref[:, lo:lo + HEAD_PAD] = (qa * cm + _rot_half(qa, MLA_ROPE // 4) * sm).astype(BF16)

    kvn = (_rms(mm(_O_KVD, MLA_KV_RANK)) * kvn_ref[...]).astype(BF16)
    kr = mm(_O_KR, HEAD_PAD)
    kr = kr * cm + _rot_half(kr, MLA_ROPE // 4) * sm
    kn = _dot(kvn, wk_ref[...])
    for hd in range(MLA_HEADS):
        lo = hd * HEAD_PAD
        km_ref[:, lo:lo + HEAD_PAD] = (kn[:, lo:lo + HEAD_PAD] + kr).astype(BF16)
    vm_ref[...] = (_dot(kvn, wv_ref[...]) + ones_ref[...]).astype(BF16)

    for br in range(N_BRANCH):
        gt_ref[:, br * d_model:(br + 1) * d_model] = jax.nn.sigmoid(
            mm(_O_GT + br * d_model, d_model)).astype(BF16)


def _inproj_call(x_parts, a1, b1, tabs, wts, *, n_batch, nj, ncb):
    fused = len(x_parts) == 3
    d = x_parts[0].shape[1]
    rows = n_batch * nj * ROW_BLOCK
    tm = ROW_BLOCK
    row = lambda j, b: (b * nj + j, 0)
    mod = lambda j, b: (2 * b + (j >= ncb).astype(jnp.int32), 0, 0)
    tab = lambda j, b: (j, 0)
    if fused:
        in_specs = [pl.BlockSpec((tm, d), row), pl.BlockSpec((tm, d // 2), row),
                    pl.BlockSpec((None, 1, d), mod)]
    else:
        in_specs = [pl.BlockSpec((tm, d), lambda j, b: (jnp.where(j < ncb, b * ncb + j, 0), 0)),
                    pl.BlockSpec((tm, d),
                                 lambda j, b: (jnp.where(j < ncb, 0, b * (nj - ncb) + j - ncb), 0))]
    in_specs += [pl.BlockSpec((None, 1, d), mod), pl.BlockSpec((None, 1, d), mod)]
    in_specs += [pl.BlockSpec((tm, t.shape[1]), tab) for t in tabs]
    in_specs += [_const_spec(w.shape) for w in wts]
    widths = (RQ_W, RQ_W, RV_W, RV_W, CONV_WIDTH, CONV_WIDTH, MLA_W, MLA_W, MLA_W, N_BRANCH * d)
    out_shape = [jax.ShapeDtypeStruct((rows, w), BF16) for w in widths]
    out_specs = [pl.BlockSpec((tm, w), row) for w in widths]
    out_shape.append(jax.ShapeDtypeStruct((rows, d), F32))
    out_specs.append(pl.BlockSpec((tm, d), row))
    return pl.pallas_call(
        functools.partial(_inproj_body, d_model=d, fused=fused, ncb=ncb),
        out_shape=out_shape,
        grid=(nj, n_batch),
        in_specs=in_specs,
        out_specs=out_specs,
        compiler_params=_params(2),
        name="in_proj",
    )(*x_parts, a1, b1, *tabs, *wts)


def _ret_direction(q_ref, k_ref, v_ref, y_ref, s_ref, dm_ref, xi_ref, zt_ref, cd_ref, chunk_order):
    states = [s_ref[hd] for hd in range(RET_HEADS)]
    for c in chunk_order:
        rows = slice(c * RET_CHUNK, (c + 1) * RET_CHUNK)
        q = q_ref[rows, :]
        k = k_ref[rows, :]
        v = v_ref[rows, :]
        qx = (q.astype(F32) * xi_ref[...]).astype(BF16)
        kz = (k.astype(F32) * zt_ref[...]).astype(BF16)
        for hd in range(RET_HEADS):
            ks = slice(hd * RET_DK, (hd + 1) * RET_DK)
            vs = slice(hd * RET_DV, (hd + 1) * RET_DV)
            vh = v[:, vs]
            sc = lax.dot_general(q[:, ks], k[:, ks], (((1,), (1,)), ((), ())),
                                 preferred_element_type=F32)
            inner = _dot((sc * dm_ref[hd]).astype(BF16), vh)
            cross = _dot(qx[:, ks], states[hd].astype(BF16))
            y_ref[rows, vs] = inner + cross
            upd = lax.dot_general(kz[:, ks], vh, (((0,), (0,)), ((), ())),
                                  preferred_element_type=F32)
            states[hd] = cd_ref[hd] * states[hd] + upd
    for hd in range(RET_HEADS):
        s_ref[hd] = states[hd]


def _ret_body(qf_ref, kf_ref, vf_ref, qb_ref, kb_ref, vb_ref,
              dmf_ref, dmb_ref, xif_ref, ztf_ref, xib_ref, ztb_ref, cdf_ref, cdb_ref,
              yf_ref, yb_ref, sf_ref, sb_ref):
    @pl.when(pl.program_id(1) == 0)
    def _():
        sf_ref[...] = jnp.zeros_like(sf_ref)
        sb_ref[...] = jnp.zeros_like(sb_ref)

    n_chunks = ROW_BLOCK // RET_CHUNK
    _ret_direction(qf_ref, kf_ref, vf_ref, yf_ref, sf_ref, dmf_ref, xif_ref, ztf_ref, cdf_ref,
                   range(n_chunks))
    _ret_direction(qb_ref, kb_ref, vb_ref, yb_ref, sb_ref, dmb_ref, xib_ref, ztb_ref, cdb_ref,
                   range(n_chunks - 1, -1, -1))


def _ret_call(rq, rk, rv, consts, *, n_batch, nj, ncb):
    rows = rq.shape[0]
    tm = ROW_BLOCK
    fwd = lambda b, s: (b * nj + s, 0)

    def bwd(b, s):
        return (b * nj + jnp.where(s < ncb, ncb - 1 - s, nj - 1 - (s - ncb)), 0)

    specs = []
    for im in (fwd, bwd):
        specs += [pl.BlockSpec((tm, RQ_W), im), pl.BlockSpec((tm, RQ_W), im),
                  pl.BlockSpec((tm, RV_W), im)]
    specs += [_const_spec(c.shape) for c in consts]
    return pl.pallas_call(
        _ret_body,
        out_shape=[jax.ShapeDtypeStruct((rows, RV_W), F32)] * 2,
        grid=(n_batch, nj),
        in_specs=specs,
        out_specs=[pl.BlockSpec((tm, RV_W), fwd), pl.BlockSpec((tm, RV_W), bwd)],
        scratch_shapes=[pltpu.VMEM((RET_HEADS, RET_DK, RET_DV), F32)] * 2,
        compiler_params=_params(2),
        name="retention",
    )(rq, rk, rv, rq, rk, rv, *consts)


def _pick_tile(n, candidates):
    for c in candidates:
        if n % c == 0:
            return c
    raise ValueError(f"no tile for {n}")


def _attn_body(q_ref, k_ref, v_ref, o_ref, s_ref, *, n_ctx, n_all, ncb):
    tq = q_ref.shape[0]
    heads = [slice(hh * HEAD_PAD, (hh + 1) * HEAD_PAD) for hh in range(2)]

    def attend(n_keys, tk):
        nt = n_keys // tk
        qs = [q_ref[:, hs] for hs in heads]

        def qk(t, mrun):
            r0 = pl.multiple_of(t * tk, tk)
            out = []
            for hh, hs in enumerate(heads):
                s = lax.dot_general(qs[hh], k_ref[pl.ds(r0, tk), hs], (((1,), (1,)), ((), ())),
                                    preferred_element_type=F32)
                s_ref[hh, t, :, 0:tk] = s
                m = mrun[hh]
                for cc in range(tk // LANES):
                    m = jnp.maximum(m, s[:, cc * LANES:(cc + 1) * LANES])
                out.append(m)
            return tuple(out)

        mrun = lax.fori_loop(0, nt, qk, tuple(jnp.full((tq, LANES), -jnp.inf, F32) for _ in heads),
                             unroll=True)
        mrow = [jnp.max(m, axis=-1, keepdims=True) for m in mrun]

        def pv(t, accs):
            r0 = pl.multiple_of(t * tk, tk)
            out = []
            for hh, hs in enumerate(heads):
                p = jnp.exp2(s_ref[hh, t, :, 0:tk] - mrow[hh]).astype(BF16)
                out.append(accs[hh] + _dot(p, v_ref[pl.ds(r0, tk), hs]))
            return tuple(out)

        accs = lax.fori_loop(0, nt, pv, tuple(jnp.zeros((tq, HEAD_PAD), F32) for _ in heads),
                             unroll=True)
        o_ref[...] = jnp.concatenate([a[:, :MLA_V] / a[:, MLA_V:MLA_V + 1] for a in accs],
                                     axis=-1).astype(BF16)

    j = pl.program_id(2)

    @pl.when(j < ncb)
    def _():
        attend(n_ctx, _pick_tile(n_ctx, (256, 128)))

    @pl.when(j >= ncb)
    def _():
        attend(n_all, _pick_tile(n_all, ATTN_KEY_TILES))


def _attn_call(qm, km, vm, *, n_batch, nj, ncb, n_ctx):
    rows = qm.shape[0]
    tm = ROW_BLOCK
    t_all = nj * tm
    tk = _pick_tile(t_all, ATTN_KEY_TILES)
    qmap = lambda b, hp, j: (b * nj + j, hp)
    kmap = lambda b, hp, j: (b, hp)
    return pl.pallas_call(
        functools.partial(_attn_body, n_ctx=n_ctx, n_all=t_all, ncb=ncb),
        out_shape=jax.ShapeDtypeStruct((rows, MLA_O), BF16),
        grid=(n_batch, MLA_HEADS // 2, nj),
        in_specs=[pl.BlockSpec((tm, 2 * HEAD_PAD), qmap),
                  pl.BlockSpec((t_all, 2 * HEAD_PAD), kmap),
                  pl.BlockSpec((t_all, 2 * HEAD_PAD), kmap)],
        out_specs=pl.BlockSpec((tm, 2 * MLA_V), qmap),
        scratch_shapes=[pltpu.VMEM((2, t_all // tk, tm, tk), F32)],
        compiler_params=_params(3),
        name="mla_attention",
    )(qm, km, vm)


def _merge_body(yf_ref, yb_ref, rg_ref, cb_ref, cu_ref, cup_ref, cun_ref, om_ref, gt_ref, x_ref,
                g1_ref, a2_ref, b2_ref, gn_ref, cw_ref, wro_ref, wco_ref, wmo_ref, wout_ref,
                wrh_ref, wrl_ref, rb_ref, xmid_ref, h2_ref, rt_ref, gw_ref, lg_ref, *, nj, ncb, d_model,
                nb):
    tm = x_ref.shape[0]
    step = pl.program_id(0)

    @pl.when(step == 0)
    def _():
        lg_ref[...] = jnp.zeros_like(lg_ref)

    cls, w_lo, w_hi = _route_rows(jnp.transpose(lg_ref[...])[:N_EXPERTS, :], rb_ref[...])
    rank, counts = _block_ranks(cls.astype(F32))
    rt_ref[...] = jnp.concatenate(
        [cls.astype(F32), rank, counts, jnp.zeros((ROUTE_ROWS - 3, tm), F32)], axis=0)
    gw_ref[...] = jnp.transpose(
        jnp.concatenate([w_lo, w_hi, jnp.zeros((LANES - 2, tm), F32)], axis=0))

    j = jnp.minimum(step, nb - 1) % nj
    seg_first = jnp.logical_or(j == 0, j == ncb)
    seg_last = jnp.logical_or(j == ncb - 1, j == nj - 1)

    y = yf_ref[...] + yb_ref[...]
    yn = jnp.concatenate([_rms(y[:, hd * RET_DV:(hd + 1) * RET_DV]) for hd in range(RET_HEADS)],
                         axis=-1) * gn_ref[...]
    y_ret = _dot((rg_ref[...].astype(F32) * yn).astype(BF16), wro_ref[...])

    u = cu_ref[...].astype(F32)
    ridx = lax.broadcasted_iota(jnp.int32, u.shape, 0)
    prev_row = cup_ref[...].astype(F32)[BF16_SUBLANES - 1:, :] * jnp.where(seg_first, 0.0, 1.0)
    next_row = cun_ref[...].astype(F32)[0:1, :] * jnp.where(seg_last, 0.0, 1.0)
    u_prev = jnp.where(ridx == 0, prev_row, pltpu.roll(u, 1, 0))
    u_next = jnp.where(ridx == tm - 1, next_row, pltpu.roll(u, tm - 1, 0))
    conv = u_prev * cw_ref[0:1, :] + u * cw_ref[1:2, :] + u_next * cw_ref[2:3, :]
    y_conv = _dot((cb_ref[...].astype(F32) * conv).astype(BF16), wco_ref[...])

    y_mla = _dot(om_ref[...], wmo_ref[...])

    merged = (gt_ref[:, 0:d_model].astype(F32) * y_ret
              + gt_ref[:, d_model:2 * d_model].astype(F32) * y_conv
              + gt_ref[:, 2 * d_model:3 * d_model].astype(F32) * y_mla)
    x_mid = x_ref[...] + g1_ref[...] * _dot(merged.astype(BF16), wout_ref[...])
    xmid_ref[...] = x_mid

    h2 = _rms(x_mid) * a2_ref[...] + b2_ref[...]
    h_hi, h_lo = _split_bf16(h2)
    lg_ref[...] = _dot(h_hi, wrh_ref[...]) + _dot(h_hi, wrl_ref[...]) + _dot(h_lo, wrh_ref[...])
    h2_ref[...] = _pack_pairs(h_hi)


def _top2_of4(v):
    def first_max(rows):
        best, idx = rows[0], jnp.zeros(rows[0].shape, jnp.int32)
        for e in range(1, len(rows)):
            better = rows[e] > best
            idx = jnp.where(better, e, idx)
            best = jnp.where(better, rows[e], best)
        return best, idx

    b1, i1 = first_max(v)
    b2, i2 = first_max([jnp.where(i1 == e, -jnp.inf, v[e]) for e in range(len(v))])
    return i1, i2, b1, b2


def _route_rows(logits_t, bias):
    scores = jax.nn.sigmoid(logits_t)
    biased = scores + bias
    row = lambda a, e: a[e:e + 1, :]
    best = None
    for g in range(N_GROUPS):
        v = [row(biased, g * EXPERTS_PER_GROUP + e) for e in range(EXPERTS_PER_GROUP)]
        i1, i2, b1, b2 = _top2_of4(v)
        cand = (b1 + b2, jnp.full(i1.shape, g, jnp.int32), i1, i2)
        if best is None:
            best = cand
        else:
            better = cand[0] > best[0]
            best = tuple(jnp.where(better, c, o) for c, o in zip(cand, best))
    _, g_sel, i1, i2 = best
    lo = jnp.minimum(i1, i2)
    hi = jnp.maximum(i1, i2)
    e_lo = g_sel * EXPERTS_PER_GROUP + lo
    e_hi = g_sel * EXPERTS_PER_GROUP + hi
    s_lo = jnp.zeros_like(best[0])
    s_hi = jnp.zeros_like(best[0])
    for e in range(N_EXPERTS):
        s_lo = jnp.where(e_lo == e, row(scores, e), s_lo)
        s_hi = jnp.where(e_hi == e, row(scores, e), s_hi)
    total = s_lo + s_hi
    pair_base = jnp.where(lo == 0, 0, jnp.where(lo == 1, 3, 5))
    cls = g_sel * len(PAIRS) + pair_base + (hi - lo - 1)
    return cls, s_lo / total, s_hi / total


def _block_ranks(cls_row):
    n = cls_row.shape[1]
    cls_col = jnp.transpose(jnp.broadcast_to(cls_row, (LANES, n)))[:, 0:1]
    ii = lax.broadcasted_iota(jnp.int32, (n, n), 0)
    jj = lax.broadcasted_iota(jnp.int32, (n, n), 1)
    earlier_same = jnp.logical_and(cls_col == cls_row, ii < jj)
    rank = jnp.sum(jnp.where(earlier_same, 1.0, 0.0), axis=0, keepdims=True)
    lane = lax.broadcasted_iota(jnp.int32, (1, n), 1)
    counts = jnp.zeros((1, n), F32)
    for c in range(N_CLASSES):
        cnt = jnp.sum(jnp.where(cls_row == c, 1.0, 0.0), axis=1, keepdims=True)
        counts = jnp.where(lane == c, cnt, counts)
    return rank, counts


def _merge_call(yf, yb, rg, cb, cu, om, gt, xa, g1, a2, b2, wts, *, nj, ncb):
    rows, d = xa.shape
    tm = ROW_BLOCK
    nb = rows // tm
    halo = BF16_SUBLANES
    per_blk = tm // halo
    blk = lambda i: jnp.minimum(i, nb - 1)
    row = lambda i: (blk(i), 0)
    mod = lambda i: (2 * (blk(i) // nj) + ((blk(i) % nj) >= ncb).astype(jnp.int32), 0, 0)
    prev = lambda i: (jnp.maximum(blk(i) * per_blk - 1, 0), 0)
    nxt = lambda i: (jnp.minimum((blk(i) + 1) * per_blk, rows // halo - 1), 0)
    in_specs = [pl.BlockSpec((tm, RV_W), row), pl.BlockSpec((tm, RV_W), row),
                pl.BlockSpec((tm, RV_W), row),
                pl.BlockSpec((tm, CONV_WIDTH), row), pl.BlockSpec((tm, CONV_WIDTH), row),
                pl.BlockSpec((halo, CONV_WIDTH), prev), pl.BlockSpec((halo, CONV_WIDTH), nxt),
                pl.BlockSpec((tm, MLA_O), row), pl.BlockSpec((tm, N_BRANCH * d), row),
                pl.BlockSpec((tm, d), row),
                pl.BlockSpec((None, 1, d), mod), pl.BlockSpec((None, 1, d), mod),
                pl.BlockSpec((None, 1, d), mod)]
    in_specs += [_const_spec(w.shape) for w in wts]
    return pl.pallas_call(
        functools.partial(_merge_body, nj=nj, ncb=ncb, d_model=d, nb=nb),
        out_shape=[jax.ShapeDtypeStruct((rows, d), F32), jax.ShapeDtypeStruct((rows, d // 2), jnp.uint32),
                   jax.ShapeDtypeStruct((nb * ROUTE_ROWS, tm), F32),
                   jax.ShapeDtypeStruct((rows, LANES), F32)],
        grid=(nb + 1,),
        in_specs=in_specs,
        out_specs=[pl.BlockSpec((tm, d), row), pl.BlockSpec((tm, d // 2), row),
                   pl.BlockSpec((ROUTE_ROWS, tm), lambda i: (jnp.maximum(i - 1, 0), 0)),
                   pl.BlockSpec((tm, LANES), lambda i: (jnp.maximum(i - 1, 0), 0))],
        scratch_shapes=[pltpu.VMEM((tm, LANES), F32)],
        compiler_params=_params(1),
        name="merge_out_proj",
    )(yf, yb, rg, cb, cu, cu, cu, om, gt, xa, g1, a2, b2, *wts)


def _moe_body(ea_ref, eb_ref, nu_ref, h_ref, gw_ref, w1a_ref, w3a_ref, w2a_ref, w1b_ref, w3b_ref,
              w2b_ref, o_ref, w13a_s, w2a_s, w13b_s, w2b_s):
    t = pl.program_id(0)
    used = t < nu_ref[0]
    prev = jnp.maximum(t - 1, 0)

    def refresh(e_ref, w1_ref, w3_ref, w2_ref, w13_s, w2_s):
        @pl.when(jnp.logical_and(used, jnp.logical_or(t == 0, e_ref[t] != e_ref[prev])))
        def _():
            w13_s[:, :D_EXPERT] = w1_ref[...].astype(BF16)
            w13_s[:, D_EXPERT:] = w3_ref[...].astype(BF16)
            w2_s[...] = w2_ref[...].astype(BF16)

    refresh(ea_ref, w1a_ref, w3a_ref, w2a_ref, w13a_s, w2a_s)
    refresh(eb_ref, w1b_ref, w3b_ref, w2b_ref, w13b_s, w2b_s)

    @pl.when(used)
    def _():
        h = _unpack_pairs(h_ref[...]).astype(BF16)
        gw = gw_ref[...]

        def expert(w13_s, w2_s, wt):
            a = _dot(h, w13_s[...])
            act = _silu(a[:, :D_EXPERT]) * a[:, D_EXPERT:] * wt
            return _dot(act.astype(BF16), w2_s[...])

        o_ref[...] = _pack_pairs(expert(w13a_s, w2a_s, gw[:, 0:1])
                                 + expert(w13b_s, w2b_s, gw[:, 1:2]))

    @pl.when(jnp.logical_not(used))
    def _():
        o_ref[...] = jnp.zeros_like(o_ref)


def _moe_call(tile_ea, tile_eb, n_used, hs, gw, w1, w3, w2, layer):
    npad = hs.shape[0]
    d = w1.shape[2]
    tmo = MOE_TILE
    row = lambda t, ea, eb, nu: (t, 0)
    wa = lambda t, ea, eb, nu: (layer, ea[t], 0, 0)
    wb = lambda t, ea, eb, nu: (layer, eb[t], 0, 0)
    up = lambda im: pl.BlockSpec((None, None, d, D_EXPERT), im)
    down = lambda im: pl.BlockSpec((None, None, D_EXPERT, d), im)
    grid_spec = pltpu.PrefetchScalarGridSpec(
        num_scalar_prefetch=3,
        grid=(npad // tmo,),
        in_specs=[pl.BlockSpec((tmo, d // 2), row), pl.BlockSpec((tmo, LANES), row),
                  up(wa), up(wa), down(wa), up(wb), up(wb), down(wb)],
        out_specs=pl.BlockSpec((tmo, d // 2), row),
        scratch_shapes=[pltpu.VMEM((d, 2 * D_EXPERT), BF16), pltpu.VMEM((D_EXPERT, d), BF16)] * 2)
    return pl.pallas_call(
        _moe_body,
        out_shape=jax.ShapeDtypeStruct((npad, d // 2), jnp.uint32),
        grid_spec=grid_spec,
        compiler_params=_params(1),
        name="moe_experts",
    )(tile_ea, tile_eb, n_used, hs, gw, w1, w3, w2, w1, w3, w2)


def _dispatch(route, n_tok):
    tmo = MOE_TILE
    n_tiles = n_tok // tmo + N_CLASSES
    npad = n_tiles * tmo
    cls = route[:, 0, :].astype(jnp.int32)
    rank = route[:, 1, :].astype(jnp.int32)
    counts = route[:, 2, :N_CLASSES].astype(jnp.int32)
    tiles_per = (jnp.sum(counts, axis=0) + tmo - 1) // tmo
    tile_end = jnp.cumsum(tiles_per)
    offs = (tile_end - tiles_per) * tmo
    block_base = offs[None, :] + jnp.cumsum(counts, axis=0) - counts
    classes = jnp.arange(N_CLASSES, dtype=jnp.int32)
    base = jnp.sum(jnp.where(cls[:, :, None] == classes, block_base[:, None, :], 0), axis=-1)
    dest = (base + rank).reshape(-1)
    tile_ids = jnp.arange(n_tiles, dtype=jnp.int32)
    tile_cls = jnp.sum((tile_end[None, :] <= tile_ids[:, None]).astype(jnp.int32), axis=1)
    tile_cls = jnp.minimum(tile_cls, N_CLASSES - 1)
    pa = jnp.asarray([p[0] for p in PAIRS], jnp.int32)
    pb = jnp.asarray([p[1] for p in PAIRS], jnp.int32)
    group = (tile_cls // len(PAIRS)) * EXPERTS_PER_GROUP
    tile_ea = group + pa[tile_cls % len(PAIRS)]
    tile_eb = group + pb[tile_cls % len(PAIRS)]
    n_used = tile_end[-1:].astype(jnp.int32)
    return dest, npad, tile_ea, tile_eb, n_used


def _sc_mesh():
    return plsc.VectorSubcoreMesh(core_axis_name="c", subcore_axis_name="s",
                                  num_cores=SC_CORES, num_subcores=SC_SUBCORES)


def _sc_chunks(idx):
    assert idx.shape[0] % (SC_WORKERS * GATHER_CHUNK) == 0
    return idx.reshape(SC_WORKERS, -1, GATHER_CHUNK)


def _sc_gather_rows(table, idx):
    d = table.shape[1]
    idx3 = _sc_chunks(idx)
    n_chunks = idx3.shape[1]
    assert table.dtype.itemsize == 4

    @functools.partial(
        pl.kernel, mesh=_sc_mesh(),
        out_type=jax.ShapeDtypeStruct((idx.shape[0], d), table.dtype),
        scratch_types=[pltpu.VMEM((n_chunks, GATHER_CHUNK), jnp.int32),
                       pltpu.VMEM((GATHER_CHUNK, d), table.dtype),
                       pltpu.SemaphoreType.DMA],
        name="sc_row_gather")
    def gather(table_hbm, idx_hbm, out_hbm, idx_v, rows_v, sem):
        worker = lax.axis_index("s") * SC_CORES + lax.axis_index("c")
        pltpu.sync_copy(idx_hbm.at[worker], idx_v)

        @pl.loop(0, n_chunks)
        def _(chunk):
            off = (worker * n_chunks + chunk) * GATHER_CHUNK
            pltpu.async_copy(table_hbm.at[idx_v.at[chunk]], rows_v, sem).wait()
            pltpu.sync_copy(rows_v, out_hbm.at[pl.ds(off, GATHER_CHUNK)])

    return gather(table, idx3)


def _sc_scatter_rows(rows, idx, n_out):
    d = rows.shape[1]
    idx3 = _sc_chunks(idx)
    n_chunks = idx3.shape[1]
    assert rows.dtype.itemsize == 4

    @functools.partial(
        pl.kernel, mesh=_sc_mesh(),
        out_type=jax.ShapeDtypeStruct((n_out, d), rows.dtype),
        scratch_types=[pltpu.VMEM((n_chunks, GATHER_CHUNK), jnp.int32),
                       pltpu.VMEM((GATHER_CHUNK, d), rows.dtype),
                       pltpu.SemaphoreType.DMA],
        name="sc_row_scatter")
    def scatter(rows_hbm, idx_hbm, out_hbm, idx_v, rows_v, sem):
        worker = lax.axis_index("s") * SC_CORES + lax.axis_index("c")
        pltpu.sync_copy(idx_hbm.at[worker], idx_v)

        @pl.loop(0, n_chunks)
        def _(chunk):
            off = (worker * n_chunks + chunk) * GATHER_CHUNK
            pltpu.sync_copy(rows_hbm.at[pl.ds(off, GATHER_CHUNK)], rows_v)
            pltpu.async_copy(rows_v, out_hbm.at[idx_v.at[chunk]], sem).wait()

    return scatter(rows, idx3)


def _final_body(x_ref, f_ref, g2_ref, fn_ref, o_ref):
    o_ref[...] = _rms(x_ref[...] + g2_ref[...] * _unpack_pairs(f_ref[...])) * fn_ref[...]


def _final_call(xmid, fg, g2, final_norm, *, n_batch, nj, ncb):
    rows, d = xmid.shape
    tm = ROW_BLOCK
    njl = nj - ncb
    src = lambda b, j: (b * nj + ncb + j, 0)
    return pl.pallas_call(
        _final_body,
        out_shape=jax.ShapeDtypeStruct((n_batch * njl * tm, d), F32),
        grid=(n_batch, njl),
        in_specs=[pl.BlockSpec((tm, d), src), pl.BlockSpec((tm, d // 2), src),
                  pl.BlockSpec((None, 1, d), lambda b, j: (2 * b + 1, 0, 0)),
                  _const_spec((1, d))],
        out_specs=pl.BlockSpec((tm, d), lambda b, j: (b * njl + j, 0)),
        compiler_params=_params(2),
        name="final_norm",
    )(xmid, fg, g2, final_norm.reshape(1, d))


def _split_cols(w, sizes):
    out, off = [], 0
    for s in sizes:
        out.append(w[:, off:off + s])
        off += s
    out.append(w[:, off:])
    return out


def _layer_weights(w_in, w_uq, w_ukv, q_norm, kv_norm):
    wq, wk, wv, wg, wcb, wcc, wcx, wqd, wkvd, wkr, wgate = _split_cols(w_in, IN_SIZES[:-1])
    rope_lanes = lambda w: jnp.pad(w, ((0, 0), (MLA_NOPE, HEAD_PAD - MLA_NOPE - MLA_ROPE)))
    w_ext = jnp.concatenate(
        [wq, wk, wv, wg, wcb, wcc, wcx, wqd, wkvd, rope_lanes(wkr), wgate], axis=1).astype(BF16)

    uq = w_uq.reshape(MLA_Q_RANK, MLA_HEADS, MLA_NOPE + MLA_ROPE)
    tail = HEAD_PAD - MLA_NOPE - MLA_ROPE
    uq_pad = jnp.pad(uq, ((0, 0), (0, 0), (0, tail)))
    wuq = uq_pad.reshape(MLA_Q_RANK, MLA_W).astype(BF16)

    ukv = w_ukv.reshape(MLA_KV_RANK, MLA_HEADS, MLA_NOPE + MLA_V)
    wk_up = jnp.pad(ukv[..., :MLA_NOPE], ((0, 0), (0, 0), (0, HEAD_PAD - MLA_NOPE)))
    wk_up = wk_up.reshape(MLA_KV_RANK, MLA_W).astype(BF16)
    wv_pad = jnp.pad(ukv[..., MLA_NOPE:], ((0, 0), (0, 0), (0, HEAD_PAD - MLA_V)))
    wv_pad = wv_pad.reshape(MLA_KV_RANK, MLA_W).astype(BF16)
    ones_row = jnp.zeros((MLA_HEADS, HEAD_PAD), F32).at[:, MLA_V].set(1.0).reshape(1, MLA_W)
    q_gain = q_norm.astype(F32) * ((MLA_NOPE + MLA_ROPE) ** -0.5 * math.log2(math.e))
    return (w_ext, wuq, wk_up, wv_pad, q_gain.reshape(1, -1),
            kv_norm.reshape(1, -1).astype(F32), ones_row)


def _rotary_tables(n_ctx, n_lat):
    n_all = n_ctx + n_lat
    row = lax.broadcasted_iota(jnp.int32, (n_all, LANES), 0)
    lane = lax.broadcasted_iota(jnp.int32, (n_all, LANES), 1)
    is_ctx = row < n_ctx
    pos = row - n_ctx
    grid_row = pos // GRID_W
    grid_col = pos - grid_row * GRID_W

    def inv_freq(idx, half):
        return ROPE_BASE ** (-idx.astype(F32) / half)

    def table(ang, first_half, live):
        cos = jnp.where(is_ctx, 1.0, jnp.cos(ang))
        sin = jnp.where(is_ctx, 0.0, jnp.where(first_half, -jnp.sin(ang), jnp.sin(ang)))
        return jnp.where(live, cos, 0.0), jnp.where(live, sin, 0.0)

    half = RET_DK // 2
    ang = pos.astype(F32) * inv_freq(lane % half, half)
    cr, sr = table(ang, (lane % RET_DK) < half, True)

    quarter = MLA_ROPE // 4
    rl = lane - MLA_NOPE
    coord = jnp.where(rl < MLA_ROPE // 2, grid_row, grid_col)
    ang = coord.astype(F32) * inv_freq(rl % quarter, quarter)
    cm, sm = table(ang, (rl % (2 * quarter)) < quarter,
                   jnp.logical_and(rl >= 0, rl < MLA_ROPE))
    cm = jnp.where(rl < 0, 1.0, cm)
    return cr, sr, cm, sm


def _retention_consts(ret_decay):
    log_gf = jax.nn.log_sigmoid(ret_decay[0].astype(F32))
    log_gb = jax.nn.log_sigmoid(ret_decay[1].astype(F32))
    idx = jnp.arange(RET_CHUNK, dtype=F32)
    rel = idx[:, None] - idx[None, :]
    dm_f = jnp.where(rel >= 0, jnp.exp(log_gf[:, None, None] * jnp.maximum(rel, 0.0)[None]), 0.0)
    dm_b = jnp.where(rel < 0, jnp.exp(log_gb[:, None, None] * jnp.maximum(-rel, 0.0)[None]), 0.0)

    def lanes(t):
        return jnp.repeat(t, RET_DK, axis=1)

    xi_f = lanes(jnp.exp(log_gf[None, :] * (idx + 1.0)[:, None]))
    zt_f = lanes(jnp.exp(log_gf[None, :] * (RET_CHUNK - 1 - idx)[:, None]))
    xi_b = lanes(jnp.exp(log_gb[None, :] * (RET_CHUNK - idx)[:, None]))
    zt_b = lanes(jnp.exp(log_gb[None, :] * idx[:, None]))
    cd = lambda lg: jnp.broadcast_to(jnp.exp(lg * RET_CHUNK)[:, None, None], (RET_HEADS, 1, RET_DV))
    return dm_f, dm_b, xi_f, zt_f, xi_b, zt_b, cd(log_gf), cd(log_gb)


def kernel(x, c, ctx, c_ctx, w_ada, b_ada, norm1, norm2, w_in, ret_decay, ret_gn, w_ret_o, conv_w,
           w_conv_o, mla_q_norm, w_uq, mla_kv_norm, w_ukv, w_mla_o, w_out, w_router, router_bias,
           w1, w3, w2, final_norm):
    n_batch, n_lat, d = x.shape
    n_ctx = ctx.shape[1]
    depth = w_ada.shape[0]
    t_all = n_ctx + n_lat
    assert n_ctx % ROW_BLOCK == 0 and n_lat % ROW_BLOCK == 0 and n_lat % GRID_W == 0
    nj = t_all // ROW_BLOCK
    ncb = n_ctx // ROW_BLOCK
    n_tok = n_batch * t_all
    assert n_tok % MOE_TILE == 0
    geom = dict(n_batch=n_batch, nj=nj, ncb=ncb)

    cc = jnp.concatenate([c, c_ctx[None, :]], axis=0)
    cc = jnp.pad(cc, ((0, -cc.shape[0] % 8), (0, 0)))
    mod = _ada_call(cc, w_ada, b_ada)[:, :n_batch + 1].reshape(depth, n_batch + 1, N_MOD, d)
    pick = jnp.stack([jnp.full((n_batch,), n_batch, jnp.int32),
                      jnp.arange(n_batch, dtype=jnp.int32)], axis=1).reshape(-1)
    mod = mod[:, pick]

    tabs = _rotary_tables(n_ctx, n_lat)
    wr_hi, wr_lo = _split_bf16(jnp.pad(w_router.astype(F32), ((0, 0), (0, LANES - N_EXPERTS))))
    rbias = router_bias.astype(F32).reshape(N_EXPERTS, 1)

    out = None
    x_parts = (ctx.reshape(n_batch * n_ctx, d), x.reshape(n_batch * n_lat, d))
    for l in range(depth):
        m = mod[l]
        rowvec = lambda v: v.reshape(2 * n_batch, 1, d)
        a1 = rowvec(norm1[l][None, :] * (1.0 + m[:, 1]))
        b1 = rowvec(m[:, 0])
        g1 = rowvec(m[:, 2])
        a2 = rowvec(norm2[l][None, :] * (1.0 + m[:, 4]))
        b2 = rowvec(m[:, 3])
        g2 = rowvec(m[:, 5])

        wts = _layer_weights(w_in[l], w_uq[l], w_ukv[l], mla_q_norm[l], mla_kv_norm[l])
        proj = _inproj_call(x_parts, a1, b1, tabs, wts, **geom)
        rq, rk, rv, rg, cb, cu, qm, km, vm, gt, xa = proj

        yf, yb = _ret_call(rq, rk, rv, _retention_consts(ret_decay[l]), **geom)
        om = _attn_call(qm, km, vm, n_ctx=n_ctx, **geom)

        merge_wts = (ret_gn[l].reshape(1, -1).astype(F32), conv_w[l].T.astype(F32),
                     w_ret_o[l].astype(BF16), w_conv_o[l].astype(BF16), w_mla_o[l].astype(BF16),
                     w_out[l].astype(BF16), wr_hi, wr_lo, rbias)
        xmid, h2, route, gate_w = _merge_call(yf, yb, rg, cb, cu, om, gt, xa, g1, a2, b2, merge_wts,
                                      nj=nj, ncb=ncb)
        route = route.reshape(-1, ROUTE_ROWS, ROW_BLOCK)
        dest, npad, tile_ea, tile_eb, n_used = _dispatch(route, n_tok)
        hs = _sc_scatter_rows(h2, dest, npad)
        gw = _sc_scatter_rows(gate_w, dest, npad)
        f_sorted = _moe_call(tile_ea, tile_eb, n_used, hs, gw, w1, w3, w2, l)
        fg = _sc_gather_rows(f_sorted, dest)

        if l < depth - 1:
            x_parts = (xmid, fg, g2)
        else:
            out = _final_call(xmid, fg, g2, final_norm, **geom)
    return out.reshape(n_batch, n_lat, d)
```

```python
import functools
import math

import jax
import jax.numpy as jnp
from jax import lax
from jax.experimental import pallas as pl
from jax.experimental.pallas import tpu as pltpu
from jax.experimental.pallas import tpu_sc as plsc

F32 = jnp.float32
BF16 = jnp.bfloat16

GRID_W = 64
RMS_EPS = 1e-6
ROPE_BASE = 10000.0
N_MOD = 6
RET_HEADS = 4
RET_DK = 64
RET_DV = 128
RET_CHUNK = 256
CONV_WIDTH = 512
MLA_HEADS = 8
MLA_Q_RANK = 384
MLA_KV_RANK = 256
MLA_NOPE = 64
MLA_ROPE = 32
MLA_V = 64
N_BRANCH = 3
N_EXPERTS = 16
N_GROUPS = 4
EXPERTS_PER_GROUP = N_EXPERTS // N_GROUPS
D_EXPERT = 512
IN_SIZES = (RET_HEADS * RET_DK, RET_HEADS * RET_DK, RET_HEADS * RET_DV, RET_HEADS * RET_DV,
            CONV_WIDTH, CONV_WIDTH, CONV_WIDTH, MLA_Q_RANK, MLA_KV_RANK, MLA_ROPE, 0)

LANES = 128
BF16_SUBLANES = 16
VMEM_LIMIT = 56 * 1024 * 1024
SC_CORES = 2
SC_SUBCORES = 16
SC_WORKERS = SC_CORES * SC_SUBCORES

HEAD_PAD = LANES
ROW_BLOCK = 256
MOE_TILE = 256
ATTN_KEY_TILES = (768, 512, 256, 128)
ROUTE_ROWS = 8
GATHER_CHUNK = 96
PAIRS = [(a, b) for a in range(EXPERTS_PER_GROUP) for b in range(a + 1, EXPERTS_PER_GROUP)]
N_CLASSES = N_GROUPS * len(PAIRS)

RQ_W = RET_HEADS * RET_DK
RV_W = RET_HEADS * RET_DV
MLA_W = MLA_HEADS * HEAD_PAD
MLA_O = MLA_HEADS * MLA_V


def _const_spec(shape):
    nd = len(shape)
    return pl.BlockSpec(shape, lambda *_: (0,) * nd, pipeline_mode=pl.Buffered(1))


def _params(n_axes):
    return pltpu.CompilerParams(dimension_semantics=("arbitrary",) * n_axes,
                                vmem_limit_bytes=VMEM_LIMIT)


def _dot(a, b):
    return jnp.dot(a, b, preferred_element_type=F32)


def _split_bf16(a):
    hi = a.astype(BF16)
    lo = (a - hi.astype(F32)).astype(BF16)
    return hi, lo


def _silu(v):
    return v * jax.nn.sigmoid(v)


def _rms(v):
    return v * lax.rsqrt(jnp.mean(v * v, axis=-1, keepdims=True) + RMS_EPS)


def _pack_pairs(v):
    half = v.shape[1] // 2
    bits = lambda t: lax.bitcast_convert_type(t.astype(BF16).astype(F32), jnp.uint32)
    return (bits(v[:, :half]) & jnp.uint32(0xFFFF0000)) | (bits(v[:, half:]) >> 16)


def _unpack_pairs(u):
    hi = lax.bitcast_convert_type(u & jnp.uint32(0xFFFF0000), F32)
    lo = lax.bitcast_convert_type(u << 16, F32)
    return jnp.concatenate([hi, lo], axis=1)


def _ada_body(c_ref, w_ref, b_ref, o_ref):
    a_hi, a_lo = _split_bf16(_silu(c_ref[...]))
    w_hi, w_lo = _split_bf16(w_ref[...])
    o_ref[...] = _dot(a_hi, w_hi) + _dot(a_hi, w_lo) + _dot(a_lo, w_hi) + b_ref[...]


def _ada_call(cc, w_ada, b_ada):
    depth, d, nm = w_ada.shape
    rows = cc.shape[0]
    cb = nm // 4
    return pl.pallas_call(
        _ada_body,
        out_shape=jax.ShapeDtypeStruct((depth, rows, nm), F32),
        grid=(depth, nm // cb),
        in_specs=[pl.BlockSpec((rows, d), lambda l, n: (0, 0)),
                  pl.BlockSpec((None, d, cb), lambda l, n: (l, 0, n)),
                  pl.BlockSpec((None, 1, cb), lambda l, n: (l, 0, n))],
        out_specs=pl.BlockSpec((None, rows, cb), lambda l, n: (l, 0, n)),
        compiler_params=_params(2),
        name="ada_mod",
    )(cc, w_ada, b_ada.reshape(depth, 1, nm))


_O_RQ = 0
_O_RK = _O_RQ + RQ_W
_O_RV = _O_RK + RQ_W
_O_RG = _O_RV + RV_W
_O_CB = _O_RG + RV_W
_O_CC = _O_CB + CONV_WIDTH
_O_CX = _O_CC + CONV_WIDTH
_O_QD = _O_CX + CONV_WIDTH
_O_KVD = _O_QD + MLA_Q_RANK
_O_KR = _O_KVD + MLA_KV_RANK
_O_GT = _O_KR + LANES


def _rot_half(v, half):
    width = v.shape[1]
    lane = lax.broadcasted_iota(jnp.int32, v.shape, 1)
    first = (lane % (2 * half)) < half
    return jnp.where(first, pltpu.roll(v, width - half, 1), pltpu.roll(v, half, 1))


def _inproj_body(*refs, d_model, fused, ncb):
    xo_ref = refs[-1]
    if fused:
        xm_ref, f_ref, g2_ref = refs[:3]
        refs = refs[3:]
        x = xm_ref[...] + g2_ref[...] * _unpack_pairs(f_ref[...])
    else:
        ctx_ref, lat_ref = refs[:2]
        refs = refs[2:]
        x = jnp.where(pl.program_id(0) < ncb, ctx_ref[...], lat_ref[...])
    xo_ref[...] = x
    (a1_ref, b1_ref, cr_ref, sr_ref, cm_ref, sm_ref, w_ref, wuq_ref, wk_ref, wv_ref, qn_ref, kvn_ref,
     ones_ref, rq_ref, rk_ref, rv_ref, rg_ref, cb_ref, cu_ref, qm_ref, km_ref, vm_ref, gt_ref) = refs[:23]
    h = (_rms(x) * a1_ref[...] + b1_ref[...]).astype(BF16)

    def mm(off, width):
        return _dot(h, w_ref[:, off:off + width])

    cr = jnp.concatenate([cr_ref[...]] * (RQ_W // cr_ref.shape[1]), axis=1)
    sr = jnp.concatenate([sr_ref[...]] * (RQ_W // sr_ref.shape[1]), axis=1)
    q = mm(_O_RQ, RQ_W)
    rq_ref[...] = (q * cr + _rot_half(q, RET_DK // 2) * sr).astype(BF16)
    k = mm(_O_RK, RQ_W)
    rk_ref[...] = ((k * cr + _rot_half(k, RET_DK // 2) * sr) * (RET_DK ** -0.5)).astype(BF16)
    rv_ref[...] = mm(_O_RV, RV_W).astype(BF16)
    rg_ref[...] = _silu(mm(_O_RG, RV_W)).astype(BF16)
    cb_ref[...] = mm(_O_CB, CONV_WIDTH).astype(BF16)
    cu_ref[...] = (mm(_O_CC, CONV_WIDTH) * mm(_O_CX, CONV_WIDTH)).astype(BF16)

    cm = cm_ref[...]
    sm = sm_ref[...]
    qn = (_rms(mm(_O_QD, MLA_Q_RANK)) * qn_ref[...]).astype(BF16)
    for hd in range(MLA_HEADS):
        lo = hd * HEAD_PAD
        qa = _dot(qn, wuq_ref[:, lo:lo + HEAD_PAD])
        qm_ref[:, lo:lo + HEAD_PAD] = (qa * cm + _rot_half(qa, MLA_ROPE // 4) * sm).astype(BF16)

    kvn = (_rms(mm(_O_KVD, MLA_KV_RANK)) * kvn_ref[...]).astype(BF16)
    kr = mm(_O_KR, HEAD_PAD)
    kr = kr * cm + _rot_half(kr, MLA_ROPE // 4) * sm
    kn = _dot(kvn, wk_ref[...])
    for hd in range(MLA_HEADS):
        lo = hd * HEAD_PAD
        km_ref[:, lo:lo + HEAD_PAD] = (kn[:, lo:lo + HEAD_PAD] + kr).astype(BF16)
    vm_ref[...] = (_dot(kvn, wv_ref[...]) + ones_ref[...]).astype(BF16)

    for br in range(N_BRANCH):
        gt_ref[:, br * d_model:(br + 1) * d_model] = jax.nn.sigmoid(
            mm(_O_GT + br * d_model, d_model)).astype(BF16)


def _inproj_call(x_parts, a1, b1, tabs, wts, *, n_batch, nj, ncb):
    fused = len(x_parts) == 3
    d = x_parts[0].shape[1]
    rows = n_batch * nj * ROW_BLOCK
    tm = ROW_BLOCK
    row = lambda j, b: (b * nj + j, 0)
    mod = lambda j, b: (2 * b + (j >= ncb).astype(jnp.int32), 0, 0)
    tab = lambda j, b: (j, 0)
    if fused:
        in_specs = [pl.BlockSpec((tm, d), row), pl.BlockSpec((tm, d // 2), row),
                    pl.BlockSpec((None, 1, d), mod)]
    else:
        in_specs = [pl.BlockSpec((tm, d), lambda j, b: (jnp.where(j < ncb, b * ncb + j, 0), 0)),
                    pl.BlockSpec((tm, d),
                                 lambda j, b: (jnp.where(j < ncb, 0, b * (nj - ncb) + j - ncb), 0))]
    in_specs += [pl.BlockSpec((None, 1, d), mod), pl.BlockSpec((None, 1, d), mod)]
    in_specs += [pl.BlockSpec((tm, t.shape[1]), tab) for t in tabs]
    in_specs += [_const_spec(w.shape) for w in wts]
    widths = (RQ_W, RQ_W, RV_W, RV_W, CONV_WIDTH, CONV_WIDTH, MLA_W, MLA_W, MLA_W, N_BRANCH * d)
    out_shape = [jax.ShapeDtypeStruct((rows, w), BF16) for w in widths]
    out_specs = [pl.BlockSpec((tm, w), row) for w in widths]
    out_shape.append(jax.ShapeDtypeStruct((rows, d), F32))
    out_specs.append(pl.BlockSpec((tm, d), row))
    return pl.pallas_call(
        functools.partial(_inproj_body, d_model=d, fused=fused, ncb=ncb),
        out_shape=out_shape,
        grid=(nj, n_batch),
        in_specs=in_specs,
        out_specs=out_specs,
        compiler_params=_params(2),
        name="in_proj",
    )(*x_parts, a1, b1, *tabs, *wts)


def _ret_direction(q_ref, k_ref, v_ref, y_ref, s_ref, dm_ref, xi_ref, zt_ref, cd_ref, chunk_order):
    states = [s_ref[hd] for hd in range(RET_HEADS)]
    for c in chunk_order:
        rows = slice(c * RET_CHUNK, (c + 1) * RET_CHUNK)
        q = q_ref[rows, :]
        k = k_ref[rows, :]
        v = v_ref[rows, :]
        qx = (q.astype(F32) * xi_ref[...]).astype(BF16)
        kz = (k.astype(F32) * zt_ref[...]).astype(BF16)
        for hd in range(RET_HEADS):
            ks = slice(hd * RET_DK, (hd + 1) * RET_DK)
            vs = slice(hd * RET_DV, (hd + 1) * RET_DV)
            vh = v[:, vs]
            sc = lax.dot_general(q[:, ks], k[:, ks], (((1,), (1,)), ((), ())),
                                 preferred_element_type=F32)
            inner = _dot((sc * dm_ref[hd]).astype(BF16), vh)
            cross = _dot(qx[:, ks], states[hd].astype(BF16))
            y_ref[rows, vs] = inner + cross
            upd = lax.dot_general(kz[:, ks], vh, (((0,), (0,)), ((), ())),
                                  preferred_element_type=F32)
            states[hd] = cd_ref[hd] * states[hd] + upd
    for hd in range(RET_HEADS):
        s_ref[hd] = states[hd]


def _ret_body(qf_ref, kf_ref, vf_ref, qb_ref, kb_ref, vb_ref,
              dmf_ref, dmb_ref, xif_ref, ztf_ref, xib_ref, ztb_ref, cdf_ref, cdb_ref,
              yf_ref, yb_ref, sf_ref, sb_ref):
    @pl.when(pl.program_id(1) == 0)
    def _():
        sf_ref[...] = jnp.zeros_like(sf_ref)
        sb_ref[...] = jnp.zeros_like(sb_ref)

    n_chunks = ROW_BLOCK // RET_CHUNK
    _ret_direction(qf_ref, kf_ref, vf_ref, yf_ref, sf_ref, dmf_ref, xif_ref, ztf_ref, cdf_ref,
                   range(n_chunks))
    _ret_direction(qb_ref, kb_ref, vb_ref, yb_ref, sb_ref, dmb_ref, xib_ref, ztb_ref, cdb_ref,
                   range(n_chunks - 1, -1, -1))


def _ret_call(rq, rk, rv, consts, *, n_batch, nj, ncb):
    rows = rq.shape[0]
    tm = ROW_BLOCK
    fwd = lambda b, s: (b * nj + s, 0)

    def bwd(b, s):
        return (b * nj + jnp.where(s < ncb, ncb - 1 - s, nj - 1 - (s - ncb)), 0)

    specs = []
    for im in (fwd, bwd):
        specs += [pl.BlockSpec((tm, RQ_W), im), pl.BlockSpec((tm, RQ_W), im),
                  pl.BlockSpec((tm, RV_W), im)]
    specs += [_const_spec(c.shape) for c in consts]
    return pl.pallas_call(
        _ret_body,
        out_shape=[jax.ShapeDtypeStruct((rows, RV_W), F32)] * 2,
        grid=(n_batch, nj),
        in_specs=specs,
        out_specs=[pl.BlockSpec((tm, RV_W), fwd), pl.BlockSpec((tm, RV_W), bwd)],
        scratch_shapes=[pltpu.VMEM((RET_HEADS, RET_DK, RET_DV), F32)] * 2,
        compiler_params=_params(2),
        name="retention",
    )(rq, rk, rv, rq, rk, rv, *consts)


def _pick_tile(n, candidates):
    for c in candidates:
        if n % c == 0:
            return c
    raise ValueError(f"no tile for {n}")


def _attn_body(q_ref, k_ref, v_ref, o_ref, s_ref, *, n_ctx, n_all, ncb):
    tq = q_ref.shape[0]
    heads = [slice(hh * HEAD_PAD, (hh + 1) * HEAD_PAD) for hh in range(2)]

    def attend(n_keys, tk):
        nt = n_keys // tk
        qs = [q_ref[:, hs] for hs in heads]

        def qk(t, mrun):
            r0 = pl.multiple_of(t * tk, tk)
            out = []
            for hh, hs in enumerate(heads):
                s = lax.dot_general(qs[hh], k_ref[pl.ds(r0, tk), hs], (((1,), (1,)), ((), ())),
                                    preferred_element_type=F32)
                s_ref[hh, t, :, 0:tk] = s
                m = mrun[hh]
                for cc in range(tk // LANES):
                    m = jnp.maximum(m, s[:, cc * LANES:(cc + 1) * LANES])
                out.append(m)
            return tuple(out)

        mrun = lax.fori_loop(0, nt, qk, tuple(jnp.full((tq, LANES), -jnp.inf, F32) for _ in heads),
                             unroll=True)
        mrow = [jnp.max(m, axis=-1, keepdims=True) for m in mrun]

        def pv(t, accs):
            r0 = pl.multiple_of(t * tk, tk)
            out = []
            for hh, hs in enumerate(heads):
                p = jnp.exp2(s_ref[hh, t, :, 0:tk] - mrow[hh]).astype(BF16)
                out.append(accs[hh] + _dot(p, v_ref[pl.ds(r0, tk), hs]))
            return tuple(out)

        accs = lax.fori_loop(0, nt, pv, tuple(jnp.zeros((tq, HEAD_PAD), F32) for _ in heads),
                             unroll=True)
        o_ref[...] = jnp.concatenate([a[:, :MLA_V] / a[:, MLA_V:MLA_V + 1] for a in accs],
                                     axis=-1).astype(BF16)

    j = pl.program_id(2)

    @pl.when(j < ncb)
    def _():
        attend(n_ctx, _pick_tile(n_ctx, (256, 128)))

    @pl.when(j >= ncb)
    def _():
        attend(n_all, _pick_tile(n_all, ATTN_KEY_TILES))


def _attn_call(qm, km, vm, *, n_batch, nj, ncb, n_ctx):
    rows = qm.shape[0]
    tm = ROW_BLOCK
    t_all = nj * tm
    tk = _pick_tile(t_all, ATTN_KEY_TILES)
    qmap = lambda b, hp, j: (b * nj + j, hp)
    kmap = lambda b, hp, j: (b, hp)
    return pl.pallas_call(
        functools.partial(_attn_body, n_ctx=n_ctx, n_all=t_all, ncb=ncb),
        out_shape=jax.ShapeDtypeStruct((rows, MLA_O), BF16),
        grid=(n_batch, MLA_HEADS // 2, nj),
        in_specs=[pl.BlockSpec((tm, 2 * HEAD_PAD), qmap),
                  pl.BlockSpec((t_all, 2 * HEAD_PAD), kmap),
                  pl.BlockSpec((t_all, 2 * HEAD_PAD), kmap)],
        out_specs=pl.BlockSpec((tm, 2 * MLA_V), qmap),
        scratch_shapes=[pltpu.VMEM((2, t_all // tk, tm, tk), F32)],
        compiler_params=_params(3),
        name="mla_attention",
    )(qm, km, vm)


def _merge_body(yf_ref, yb_ref, rg_ref, cb_ref, cu_ref, cup_ref, cun_ref, om_ref, gt_ref, x_ref,
                g1_ref, a2_ref, b2_ref, gn_ref, cw_ref, wro_ref, wco_ref, wmo_ref, wout_ref,
                wrh_ref, wrl_ref, rb_ref, xmid_ref, h2_ref, rt_ref, gw_ref, lg_ref, *, nj, ncb, d_model,
                nb):
    tm = x_ref.shape[0]
    step = pl.program_id(0)

    @pl.when(step == 0)
    def _():
        lg_ref[...] = jnp.zeros_like(lg_ref)

    cls, w_lo, w_hi = _route_rows(jnp.transpose(lg_ref[...])[:N_EXPERTS, :], rb_ref[...])
    rank, counts = _block_ranks(cls.astype(F32))
    rt_ref[...] = jnp.concatenate(
        [cls.astype(F32), rank, counts, jnp.zeros((ROUTE_ROWS - 3, tm), F32)], axis=0)
    gw_ref[...] = jnp.transpose(
        jnp.concatenate([w_lo, w_hi, jnp.zeros((LANES - 2, tm), F32)], axis=0))

    j = jnp.minimum(step, nb - 1) % nj
    seg_first = jnp.logical_or(j == 0, j == ncb)
    seg_last = jnp.logical_or(j == ncb - 1, j == nj - 1)

    y = yf_ref[...] + yb_ref[...]
    yn = jnp.concatenate([_rms(y[:, hd * RET_DV:(hd + 1) * RET_DV]) for hd in range(RET_HEADS)],
                         axis=-1) * gn_ref[...]
    y_ret = _dot((rg_ref[...].astype(F32) * yn).astype(BF16), wro_ref[...])

    u = cu_ref[...].astype(F32)
    ridx = lax.broadcasted_iota(jnp.int32, u.shape, 0)
    prev_row = cup_ref[...].astype(F32)[BF16_SUBLANES - 1:, :] * jnp.where(seg_first, 0.0, 1.0)
    next_row = cun_ref[...].astype(F32)[0:1, :] * jnp.where(seg_last, 0.0, 1.0)
    u_prev = jnp.where(ridx == 0, prev_row, pltpu.roll(u, 1, 0))
    u_next = jnp.where(ridx == tm - 1, next_row, pltpu.roll(u, tm - 1, 0))
    conv = u_prev * cw_ref[0:1, :] + u * cw_ref[1:2, :] + u_next * cw_ref[2:3, :]
    y_conv = _dot((cb_ref[...].astype(F32) * conv).astype(BF16), wco_ref[...])

    y_mla = _dot(om_ref[...], wmo_ref[...])

    merged = (gt_ref[:, 0:d_model].astype(F32) * y_ret
              + gt_ref[:, d_model:2 * d_model].astype(F32) * y_conv
              + gt_ref[:, 2 * d_model:3 * d_model].astype(F32) * y_mla)
    x_mid = x_ref[...] + g1_ref[...] * _dot(merged.astype(BF16), wout_ref[...])
    xmid_ref[...] = x_mid

    h2 = _rms(x_mid) * a2_ref[...] + b2_ref[...]
    h_hi, h_lo = _split_bf16(h2)
    lg_ref[...] = _dot(h_hi, wrh_ref[...]) + _dot(h_hi, wrl_ref[...]) + _dot(h_lo, wrh_ref[...])
    h2_ref[...] = _pack_pairs(h_hi)


def _top2_of4(v):
    def first_max(rows):
        best, idx = rows[0], jnp.zeros(rows[0].shape, jnp.int32)
        for e in range(1, len(rows)):
            better = rows[e] > best
            idx = jnp.where(better, e, idx)
            best = jnp.where(better, rows[e], best)
        return best, idx

    b1, i1 = first_max(v)
    b2, i2 = first_max([jnp.where(i1 == e, -jnp.inf, v[e]) for e in range(len(v))])
    return i1, i2, b1, b2


def _route_rows(logits_t, bias):
    scores = jax.nn.sigmoid(logits_t)
    biased = scores + bias
    row = lambda a, e: a[e:e + 1, :]
    best = None
    for g in range(N_GROUPS):
        v = [row(biased, g * EXPERTS_PER_GROUP + e) for e in range(EXPERTS_PER_GROUP)]
        i1, i2, b1, b2 = _top2_of4(v)
        cand = (b1 + b2, jnp.full(i1.shape, g, jnp.int32), i1, i2)
        if best is None:
            best = cand
        else:
            better = cand[0] > best[0]
            best = tuple(jnp.where(better, c, o) for c, o in zip(cand, best))
    _, g_sel, i1, i2 = best
    lo = jnp.minimum(i1, i2)
    hi = jnp.maximum(i1, i2)
    e_lo = g_sel * EXPERTS_PER_GROUP + lo
    e_hi = g_sel * EXPERTS_PER_GROUP + hi
    s_lo = jnp.zeros_like(best[0])
    s_hi = jnp.zeros_like(best[0])
    for e in range(N_EXPERTS):
        s_lo = jnp.where(e_lo == e, row(scores, e), s_lo)
        s_hi = jnp.where(e_hi == e, row(scores, e), s_hi)
    total = s_lo + s_hi
    pair_base = jnp.where(lo == 0, 0, jnp.where(lo == 1, 3, 5))
    cls = g_sel * len(PAIRS) + pair_base + (hi - lo - 1)
    return cls, s_lo / total, s_hi / total


def _block_ranks(cls_row):
    n = cls_row.shape[1]
    cls_col = jnp.transpose(jnp.broadcast_to(cls_row, (LANES, n)))[:, 0:1]
    ii = lax.broadcasted_iota(jnp.int32, (n, n), 0)
    jj = lax.broadcasted_iota(jnp.int32, (n, n), 1)
    earlier_same = jnp.logical_and(cls_col == cls_row, ii < jj)
    rank = jnp.sum(jnp.where(earlier_same, 1.0, 0.0), axis=0, keepdims=True)
    lane = lax.broadcasted_iota(jnp.int32, (1, n), 1)
    counts = jnp.zeros((1, n), F32)
    for c in range(N_CLASSES):
        cnt = jnp.sum(jnp.where(cls_row == c, 1.0, 0.0), axis=1, keepdims=True)
        counts = jnp.where(lane == c, cnt, counts)
    return rank, counts


def _merge_call(yf, yb, rg, cb, cu, om, gt, xa, g1, a2, b2, wts, *, nj, ncb):
    rows, d = xa.shape
    tm = ROW_BLOCK
    nb = rows // tm
    halo = BF16_SUBLANES
    per_blk = tm // halo
    blk = lambda i: jnp.minimum(i, nb - 1)
    row = lambda i: (blk(i), 0)
    mod = lambda i: (2 * (blk(i) // nj) + ((blk(i) % nj) >= ncb).astype(jnp.int32), 0, 0)
    prev = lambda i: (jnp.maximum(blk(i) * per_blk - 1, 0), 0)
    nxt = lambda i: (jnp.minimum((blk(i) + 1) * per_blk, rows // halo - 1), 0)
    in_specs = [pl.BlockSpec((tm, RV_W), row), pl.BlockSpec((tm, RV_W), row),
                pl.BlockSpec((tm, RV_W), row),
                pl.BlockSpec((tm, CONV_WIDTH), row), pl.BlockSpec((tm, CONV_WIDTH), row),
                pl.BlockSpec((halo, CONV_WIDTH), prev), pl.BlockSpec((halo, CONV_WIDTH), nxt),
                pl.BlockSpec((tm, MLA_O), row), pl.BlockSpec((tm, N_BRANCH * d), row),
                pl.BlockSpec((tm, d), row),
                pl.BlockSpec((None, 1, d), mod), pl.BlockSpec((None, 1, d), mod),
                pl.BlockSpec((None, 1, d), mod)]
    in_specs += [_const_spec(w.shape) for w in wts]
    return pl.pallas_call(
        functools.partial(_merge_body, nj=nj, ncb=ncb, d_model=d, nb=nb),
        out_shape=[jax.ShapeDtypeStruct((rows, d), F32), jax.ShapeDtypeStruct((rows, d // 2), jnp.uint32),
                   jax.ShapeDtypeStruct((nb * ROUTE_ROWS, tm), F32),
                   jax.ShapeDtypeStruct((rows, LANES), F32)],
        grid=(nb + 1,),
        in_specs=in_specs,
        out_specs=[pl.BlockSpec((tm, d), row), pl.BlockSpec((tm, d // 2), row),
                   pl.BlockSpec((ROUTE_ROWS, tm), lambda i: (jnp.maximum(i - 1, 0), 0)),
                   pl.BlockSpec((tm, LANES), lambda i: (jnp.maximum(i - 1, 0), 0))],
        scratch_shapes=[pltpu.VMEM((tm, LANES), F32)],
        compiler_params=_params(1),
        name="merge_out_proj",
    )(yf, yb, rg, cb, cu, cu, cu, om, gt, xa, g1, a2, b2, *wts)


def _moe_body(ea_ref, eb_ref, nu_ref, h_ref, gw_ref, w1a_ref, w3a_ref, w2a_ref, w1b_ref, w3b_ref,
              w2b_ref, o_ref, w13a_s, w2a_s, w13b_s, w2b_s):
    t = pl.program_id(0)
    used = t < nu_ref[0]
    prev = jnp.maximum(t - 1, 0)

    def refresh(e_ref, w1_ref, w3_ref, w2_ref, w13_s, w2_s):
        @pl.when(jnp.logical_and(used, jnp.logical_or(t == 0, e_ref[t] != e_ref[prev])))
        def _():
            w13_s[:, :D_EXPERT] = w1_ref[...].astype(BF16)
            w13_s[:, D_EXPERT:] = w3_ref[...].astype(BF16)
            w2_s[...] = w2_ref[...].astype(BF16)

    refresh(ea_ref, w1a_ref, w3a_ref, w2a_ref, w13a_s, w2a_s)
    refresh(eb_ref, w1b_ref, w3b_ref, w2b_ref, w13b_s, w2b_s)

    @pl.when(used)
    def _():
        h = _unpack_pairs(h_ref[...]).astype(BF16)
        gw = gw_ref[...]

        def expert(w13_s, w2_s, wt):
            a = _dot(h, w13_s[...])
            act = _silu(a[:, :D_EXPERT]) * a[:, D_EXPERT:] * wt
            return _dot(act.astype(BF16), w2_s[...])

        o_ref[...] = _pack_pairs(expert(w13a_s, w2a_s, gw[:, 0:1])
                                 + expert(w13b_s, w2b_s, gw[:, 1:2]))

    @pl.when(jnp.logical_not(used))
    def _():
        o_ref[...] = jnp.zeros_like(o_ref)


def _moe_call(tile_ea, tile_eb, n_used, hs, gw, w1, w3, w2, layer):
    npad = hs.shape[0]
    d = w1.shape[2]
    tmo = MOE_TILE
    row = lambda t, ea, eb, nu: (t, 0)
    wa = lambda t, ea, eb, nu: (layer, ea[t], 0, 0)
    wb = lambda t, ea, eb, nu: (layer, eb[t], 0, 0)
    up = lambda im: pl.BlockSpec((None, None, d, D_EXPERT), im)
    down = lambda im: pl.BlockSpec((None, None, D_EXPERT, d), im)
    grid_spec = pltpu.PrefetchScalarGridSpec(
        num_scalar_prefetch=3,
        grid=(npad // tmo,),
        in_specs=[pl.BlockSpec((tmo, d // 2), row), pl.BlockSpec((tmo, LANES), row),
                  up(wa), up(wa), down(wa), up(wb), up(wb), down(wb)],
        out_specs=pl.BlockSpec((tmo, d // 2), row),
        scratch_shapes=[pltpu.VMEM((d, 2 * D_EXPERT), BF16), pltpu.VMEM((D_EXPERT, d), BF16)] * 2)
    return pl.pallas_call(
        _moe_body,
        out_shape=jax.ShapeDtypeStruct((npad, d // 2), jnp.uint32),
        grid_spec=grid_spec,
        compiler_params=_params(1),
        name="moe_experts",
    )(tile_ea, tile_eb, n_used, hs, gw, w1, w3, w2, w1, w3, w2)


def _dispatch(route, n_tok):
    tmo = MOE_TILE
    n_tiles = n_tok // tmo + N_CLASSES
    npad = n_tiles * tmo
    cls = route[:, 0, :].astype(jnp.int32)
    rank = route[:, 1, :].astype(jnp.int32)
    counts = route[:, 2, :N_CLASSES].astype(jnp.int32)
    tiles_per = (jnp.sum(counts, axis=0) + tmo - 1) // tmo
    tile_end = jnp.cumsum(tiles_per)
    offs = (tile_end - tiles_per) * tmo
    block_base = offs[None, :] + jnp.cumsum(counts, axis=0) - counts
    classes = jnp.arange(N_CLASSES, dtype=jnp.int32)
    base = jnp.sum(jnp.where(cls[:, :, None] == classes, block_base[:, None, :], 0), axis=-1)
    dest = (base + rank).reshape(-1)
    tile_ids = jnp.arange(n_tiles, dtype=jnp.int32)
    tile_cls = jnp.sum((tile_end[None, :] <= tile_ids[:, None]).astype(jnp.int32), axis=1)
    tile_cls = jnp.minimum(tile_cls, N_CLASSES - 1)
    pa = jnp.asarray([p[0] for p in PAIRS], jnp.int32)
    pb = jnp.asarray([p[1] for p in PAIRS], jnp.int32)
    group = (tile_cls // len(PAIRS)) * EXPERTS_PER_GROUP
    tile_ea = group + pa[tile_cls % len(PAIRS)]
    tile_eb = group + pb[tile_cls % len(PAIRS)]
    n_used = tile_end[-1:].astype(jnp.int32)
    return dest, npad, tile_ea, tile_eb, n_used


def _sc_mesh():
    return plsc.VectorSubcoreMesh(core_axis_name="c", subcore_axis_name="s",
                                  num_cores=SC_CORES, num_subcores=SC_SUBCORES)


def _sc_chunks(idx):
    assert idx.shape[0] % (SC_WORKERS * GATHER_CHUNK) == 0
    return idx.reshape(SC_WORKERS, -1, GATHER_CHUNK)


def _sc_pipeline(n_chunks, fetch, put):
    assert n_chunks % 2 == 0
    fetch(0, 0).start()

    @pl.loop(0, n_chunks, step=2)
    def _(c0):
        for b in (0, 1):
            c = c0 + b
            fetch(c, b).wait()

            @pl.when(c >= 1)
            def _():
                put(c - 1, 1 - b).wait()

            @pl.when(c + 1 < n_chunks)
            def _():
                fetch(c + 1, 1 - b).start()

            put(c, b).start()

    put(n_chunks - 1, 1).wait()


def _sc_scratch(n_chunks, d, dtype):
    return [pltpu.VMEM((n_chunks, GATHER_CHUNK), jnp.int32),
            pltpu.VMEM((2, GATHER_CHUNK, d), dtype),
            pltpu.SemaphoreType.DMA((2,)), pltpu.SemaphoreType.DMA((2,))]


def _sc_gather_rows(table, idx):
    d = table.shape[1]
    idx3 = _sc_chunks(idx)
    n_chunks = idx3.shape[1]
    assert table.dtype.itemsize == 4

    @functools.partial(
        pl.kernel, mesh=_sc_mesh(),
        out_type=jax.ShapeDtypeStruct((idx.shape[0], d), table.dtype),
        scratch_types=_sc_scratch(n_chunks, d, table.dtype),
        name="sc_row_gather")
    def gather(table_hbm, idx_hbm, out_hbm, idx_v, rows_v, sem_in, sem_out):
        worker = lax.axis_index("s") * SC_CORES + lax.axis_index("c")
        pltpu.sync_copy(idx_hbm.at[worker], idx_v)

        def fetch(c, b):
            return pltpu.make_async_copy(table_hbm.at[idx_v.at[c]], rows_v.at[b], sem_in.at[b])

        def put(c, b):
            off = (worker * n_chunks + c) * GATHER_CHUNK
            return pltpu.make_async_copy(rows_v.at[b], out_hbm.at[pl.ds(off, GATHER_CHUNK)],
                                         sem_out.at[b])

        _sc_pipeline(n_chunks, fetch, put)

    return gather(table, idx3)


def _sc_scatter_rows(rows, idx, n_out):
    d = rows.shape[1]
    idx3 = _sc_chunks(idx)
    n_chunks = idx3.shape[1]
    assert rows.dtype.itemsize == 4

    @functools.partial(
        pl.kernel, mesh=_sc_mesh(),
        out_type=jax.ShapeDtypeStruct((n_out, d), rows.dtype),
        scratch_types=_sc_scratch(n_chunks, d, rows.dtype),
        name="sc_row_scatter")
    def scatter(rows_hbm, idx_hbm, out_hbm, idx_v, rows_v, sem_in, sem_out):
        worker = lax.axis_index("s") * SC_CORES + lax.axis_index("c")
        pltpu.sync_copy(idx_hbm.at[worker], idx_v)

        def fetch(c, b):
            off = (worker * n_chunks + c) * GATHER_CHUNK
            return pltpu.make_async_copy(rows_hbm.at[pl.ds(off, GATHER_CHUNK)], rows_v.at[b],
                                         sem_in.at[b])

        def put(c, b):
            return pltpu.make_async_copy(rows_v.at[b], out_hbm.at[idx_v.at[c]], sem_out.at[b])

        _sc_pipeline(n_chunks, fetch, put)

    return scatter(rows, idx3)


def _final_body(x_ref, f_ref, g2_ref, fn_ref, o_ref):
    o_ref[...] = _rms(x_ref[...] + g2_ref[...] * _unpack_pairs(f_ref[...])) * fn_ref[...]


def _final_call(xmid, fg, g2, final_norm, *, n_batch, nj, ncb):
    rows, d = xmid.shape
    tm = ROW_BLOCK
    njl = nj - ncb
    src = lambda b, j: (b * nj + ncb + j, 0)
    return pl.pallas_call(
        _final_body,
        out_shape=jax.ShapeDtypeStruct((n_batch * njl * tm, d), F32),
        grid=(n_batch, njl),
        in_specs=[pl.BlockSpec((tm, d), src), pl.BlockSpec((tm, d // 2), src),
                  pl.BlockSpec((None, 1, d), lambda b, j: (2 * b + 1, 0, 0)),
                  _const_spec((1, d))],
        out_specs=pl.BlockSpec((tm, d), lambda b, j: (b * njl + j, 0)),
        compiler_params=_params(2),
        name="final_norm",
    )(xmid, fg, g2, final_norm.reshape(1, d))


def _split_cols(w, sizes):
    out, off = [], 0
    for s in sizes:
        out.append(w[:, off:off + s])
        off += s
    out.append(w[:, off:])
    return out


def _layer_weights(w_in, w_uq, w_ukv, q_norm, kv_norm):
    wq, wk, wv, wg, wcb, wcc, wcx, wqd, wkvd, wkr, wgate = _split_cols(w_in, IN_SIZES[:-1])
    rope_lanes = lambda w: jnp.pad(w, ((0, 0), (MLA_NOPE, HEAD_PAD - MLA_NOPE - MLA_ROPE)))
    w_ext = jnp.concatenate(
        [wq, wk, wv, wg, wcb, wcc, wcx, wqd, wkvd, rope_lanes(wkr), wgate], axis=1).astype(BF16)

    uq = w_uq.reshape(MLA_Q_RANK, MLA_HEADS, MLA_NOPE + MLA_ROPE)
    tail = HEAD_PAD - MLA_NOPE - MLA_ROPE
    uq_pad = jnp.pad(uq, ((0, 0), (0, 0), (0, tail)))
    wuq = uq_pad.reshape(MLA_Q_RANK, MLA_W).astype(BF16)

    ukv = w_ukv.reshape(MLA_KV_RANK, MLA_HEADS, MLA_NOPE + MLA_V)
    wk_up = jnp.pad(ukv[..., :MLA_NOPE], ((0, 0), (0, 0), (0, HEAD_PAD - MLA_NOPE)))
    wk_up = wk_up.reshape(MLA_KV_RANK, MLA_W).astype(BF16)
    wv_pad = jnp.pad(ukv[..., MLA_NOPE:], ((0, 0), (0, 0), (0, HEAD_PAD - MLA_V)))
    wv_pad = wv_pad.reshape(MLA_KV_RANK, MLA_W).astype(BF16)
    ones_row = jnp.zeros((MLA_HEADS, HEAD_PAD), F32).at[:, MLA_V].set(1.0).reshape(1, MLA_W)
    q_gain = q_norm.astype(F32) * ((MLA_NOPE + MLA_ROPE) ** -0.5 * math.log2(math.e))
    return (w_ext, wuq, wk_up, wv_pad, q_gain.reshape(1, -1),
            kv_norm.reshape(1, -1).astype(F32), ones_row)


def _rotary_tables(n_ctx, n_lat):
    n_all = n_ctx + n_lat
    row = lax.broadcasted_iota(jnp.int32, (n_all, LANES), 0)
    lane = lax.broadcasted_iota(jnp.int32, (n_all, LANES), 1)
    is_ctx = row < n_ctx
    pos = row - n_ctx
    grid_row = pos // GRID_W
    grid_col = pos - grid_row * GRID_W

    def inv_freq(idx, half):
        return ROPE_BASE ** (-idx.astype(F32) / half)

    def table(ang, first_half, live):
        cos = jnp.where(is_ctx, 1.0, jnp.cos(ang))
        sin = jnp.where(is_ctx, 0.0, jnp.where(first_half, -jnp.sin(ang), jnp.sin(ang)))
        return jnp.where(live, cos, 0.0), jnp.where(live, sin, 0.0)

    half = RET_DK // 2
    ang = pos.astype(F32) * inv_freq(lane % half, half)
    cr, sr = table(ang, (lane % RET_DK) < half, True)

    quarter = MLA_ROPE // 4
    rl = lane - MLA_NOPE
    coord = jnp.where(rl < MLA_ROPE // 2, grid_row, grid_col)
    ang = coord.astype(F32) * inv_freq(rl % quarter, quarter)
    cm, sm = table(ang, (rl % (2 * quarter)) < quarter,
                   jnp.logical_and(rl >= 0, rl < MLA_ROPE))
    cm = jnp.where(rl < 0, 1.0, cm)
    return cr, sr, cm, sm


def _retention_consts(ret_decay):
    log_gf = jax.nn.log_sigmoid(ret_decay[0].astype(F32))
    log_gb = jax.nn.log_sigmoid(ret_decay[1].astype(F32))
    idx = jnp.arange(RET_CHUNK, dtype=F32)
    rel = idx[:, None] - idx[None, :]
    dm_f = jnp.where(rel >= 0, jnp.exp(log_gf[:, None, None] * jnp.maximum(rel, 0.0)[None]), 0.0)
    dm_b = jnp.where(rel < 0, jnp.exp(log_gb[:, None, None] * jnp.maximum(-rel, 0.0)[None]), 0.0)

    def lanes(t):
        return jnp.repeat(t, RET_DK, axis=1)

    xi_f = lanes(jnp.exp(log_gf[None, :] * (idx + 1.0)[:, None]))
    zt_f = lanes(jnp.exp(log_gf[None, :] * (RET_CHUNK - 1 - idx)[:, None]))
    xi_b = lanes(jnp.exp(log_gb[None, :] * (RET_CHUNK - idx)[:, None]))
    zt_b = lanes(jnp.exp(log_gb[None, :] * idx[:, None]))
    cd = lambda lg: jnp.broadcast_to(jnp.exp(lg * RET_CHUNK)[:, None, None], (RET_HEADS, 1, RET_DV))
    return dm_f, dm_b, xi_f, zt_f, xi_b, zt_b, cd(log_gf), cd(log_gb)


def kernel(x, c, ctx, c_ctx, w_ada, b_ada, norm1, norm2, w_in, ret_decay, ret_gn, w_ret_o, conv_w,
           w_conv_o, mla_q_norm, w_uq, mla_kv_norm, w_ukv, w_mla_o, w_out, w_router, router_bias,
           w1, w3, w2, final_norm):
    n_batch, n_lat, d = x.shape
    n_ctx = ctx.shape[1]
    depth = w_ada.shape[0]
    t_all = n_ctx + n_lat
    assert n_ctx % ROW_BLOCK == 0 and n_lat % ROW_BLOCK == 0 and n_lat % GRID_W == 0
    nj = t_all // ROW_BLOCK
    ncb = n_ctx // ROW_BLOCK
    n_tok = n_batch * t_all
    assert n_tok % MOE_TILE == 0
    geom = dict(n_batch=n_batch, nj=nj, ncb=ncb)

    cc = jnp.concatenate([c, c_ctx[None, :]], axis=0)
    cc = jnp.pad(cc, ((0, -cc.shape[0] % 8), (0, 0)))
    mod = _ada_call(cc, w_ada, b_ada)[:, :n_batch + 1].reshape(depth, n_batch + 1, N_MOD, d)
    pick = jnp.stack([jnp.full((n_batch,), n_batch, jnp.int32),
                      jnp.arange(n_batch, dtype=jnp.int32)], axis=1).reshape(-1)
    mod = mod[:, pick]

    tabs = _rotary_tables(n_ctx, n_lat)
    wr_hi, wr_lo = _split_bf16(jnp.pad(w_router.astype(F32), ((0, 0), (0, LANES - N_EXPERTS))))
    rbias = router_bias.astype(F32).reshape(N_EXPERTS, 1)

    out = None
    x_parts = (ctx.reshape(n_batch * n_ctx, d), x.reshape(n_batch * n_lat, d))
    for l in range(depth):
        m = mod[l]
        rowvec = lambda v: v.reshape(2 * n_batch, 1, d)
        a1 = rowvec(norm1[l][None, :] * (1.0 + m[:, 1]))
        b1 = rowvec(m[:, 0])
        g1 = rowvec(m[:, 2])
        a2 = rowvec(norm2[l][None, :] * (1.0 + m[:, 4]))
        b2 = rowvec(m[:, 3])
        g2 = rowvec(m[:, 5])

        wts = _layer_weights(w_in[l], w_uq[l], w_ukv[l], mla_q_norm[l], mla_kv_norm[l])
        proj = _inproj_call(x_parts, a1, b1, tabs, wts, **geom)
        rq, rk, rv, rg, cb, cu, qm, km, vm, gt, xa = proj

        yf, yb = _ret_call(rq, rk, rv, _retention_consts(ret_decay[l]), **geom)
        om = _attn_call(qm, km, vm, n_ctx=n_ctx, **geom)

        merge_wts = (ret_gn[l].reshape(1, -1).astype(F32), conv_w[l].T.astype(F32),
                     w_ret_o[l].astype(BF16), w_conv_o[l].astype(BF16), w_mla_o[l].astype(BF16),
                     w_out[l].astype(BF16), wr_hi, wr_lo, rbias)
        xmid, h2, route, gate_w = _merge_call(yf, yb, rg, cb, cu, om, gt, xa, g1, a2, b2, merge_wts,
                                      nj=nj, ncb=ncb)
        route = route.reshape(-1, ROUTE_ROWS, ROW_BLOCK)
        dest, npad, tile_ea, tile_eb, n_used = _dispatch(route, n_tok)
        hs = _sc_scatter_rows(h2, dest, npad)
        gw = _sc_scatter_rows(gate_w, dest, npad)
        f_sorted = _moe_call(tile_ea, tile_eb, n_used, hs, gw, w1, w3, w2, l)
        fg = _sc_gather_rows(f_sorted, dest)

        if l < depth - 1:
            x_parts = (xmid, fg, g2)
        else:
            out = _final_call(xmid, fg, g2, final_norm, **geom)
    return out.reshape(n_batch, n_lat, d)
```

```python
import functools
import math

import jax
import jax.numpy as jnp
from jax import lax
from jax.experimental import pallas as pl
from jax.experimental.pallas import tpu as pltpu
from jax.experimental.pallas import tpu_sc as plsc

F32 = jnp.float32
BF16 = jnp.bfloat16

GRID_W = 64
RMS_EPS = 1e-6
ROPE_BASE = 10000.0
N_MOD = 6
RET_HEADS = 4
RET_DK = 64
RET_DV = 128
RET_CHUNK = 256
CONV_WIDTH = 512
MLA_HEADS = 8
MLA_Q_RANK = 384
MLA_KV_RANK = 256
MLA_NOPE = 64
MLA_ROPE = 32
MLA_V = 64
N_BRANCH = 3
N_EXPERTS = 16
N_GROUPS = 4
EXPERTS_PER_GROUP = N_EXPERTS // N_GROUPS
D_EXPERT = 512
IN_SIZES = (RET_HEADS * RET_DK, RET_HEADS * RET_DK, RET_HEADS * RET_DV, RET_HEADS * RET_DV,
            CONV_WIDTH, CONV_WIDTH, CONV_WIDTH, MLA_Q_RANK, MLA_KV_RANK, MLA_ROPE, 0)

LANES = 128
BF16_SUBLANES = 16
VMEM_LIMIT = 56 * 1024 * 1024
SC_CORES = 2
SC_SUBCORES = 16
SC_WORKERS = SC_CORES * SC_SUBCORES

HEAD_PAD = LANES
ROW_BLOCK = 256
MOE_TILE = 256
ATTN_KEY_TILES = (768, 512, 256, 128)
ROUTE_ROWS = 8
GATHER_CHUNK = 96
PAIRS = [(a, b) for a in range(EXPERTS_PER_GROUP) for b in range(a + 1, EXPERTS_PER_GROUP)]
N_CLASSES = N_GROUPS * len(PAIRS)

RQ_W = RET_HEADS * RET_DK
RV_W = RET_HEADS * RET_DV
MLA_W = MLA_HEADS * HEAD_PAD
MLA_O = MLA_HEADS * MLA_V


def _const_spec(shape):
    nd = len(shape)
    return pl.BlockSpec(shape, lambda *_: (0,) * nd, pipeline_mode=pl.Buffered(1))


def _params(n_axes):
    return pltpu.CompilerParams(dimension_semantics=("arbitrary",) * n_axes,
                                vmem_limit_bytes=VMEM_LIMIT)


def _dot(a, b):
    return jnp.dot(a, b, preferred_element_type=F32)


def _split_bf16(a):
    hi = a.astype(BF16)
    lo = (a - hi.astype(F32)).astype(BF16)
    return hi, lo


def _silu(v):
    return v * jax.nn.sigmoid(v)


def _rms(v):
    return v * lax.rsqrt(jnp.mean(v * v, axis=-1, keepdims=True) + RMS_EPS)


def _pack_pairs(v):
    half = v.shape[1] // 2
    bits = lambda t: lax.bitcast_convert_type(t.astype(BF16).astype(F32), jnp.uint32)
    return (bits(v[:, :half]) & jnp.uint32(0xFFFF0000)) | (bits(v[:, half:]) >> 16)


def _unpack_pairs(u):
    hi = lax.bitcast_convert_type(u & jnp.uint32(0xFFFF0000), F32)
    lo = lax.bitcast_convert_type(u << 16, F32)
    return jnp.concatenate([hi, lo], axis=1)


def _ada_body(c_ref, w_ref, b_ref, o_ref):
    a_hi, a_lo = _split_bf16(_silu(c_ref[...]))
    w_hi, w_lo = _split_bf16(w_ref[...])
    o_ref[...] = _dot(a_hi, w_hi) + _dot(a_hi, w_lo) + _dot(a_lo, w_hi) + b_ref[...]


def _ada_call(cc, w_ada, b_ada):
    depth, d, nm = w_ada.shape
    rows = cc.shape[0]
    cb = nm // 4
    return pl.pallas_call(
        _ada_body,
        out_shape=jax.ShapeDtypeStruct((depth, rows, nm), F32),
        grid=(depth, nm // cb),
        in_specs=[pl.BlockSpec((rows, d), lambda l, n: (0, 0)),
                  pl.BlockSpec((None, d, cb), lambda l, n: (l, 0, n)),
                  pl.BlockSpec((None, 1, cb), lambda l, n: (l, 0, n))],
        out_specs=pl.BlockSpec((None, rows, cb), lambda l, n: (l, 0, n)),
        compiler_params=_params(2),
        name="ada_mod",
    )(cc, w_ada, b_ada.reshape(depth, 1, nm))


_O_RQ = 0
_O_RK = _O_RQ + RQ_W
_O_RV = _O_RK + RQ_W
_O_RG = _O_RV + RV_W
_O_CB = _O_RG + RV_W
_O_CC = _O_CB + CONV_WIDTH
_O_CX = _O_CC + CONV_WIDTH
_O_QD = _O_CX + CONV_WIDTH
_O_KVD = _O_QD + MLA_Q_RANK
_O_KR = _O_KVD + MLA_KV_RANK
_O_GT = _O_KR + LANES


def _rot_half(v, half):
    width = v.shape[1]
    lane = lax.broadcasted_iota(jnp.int32, v.shape, 1)
    first = (lane % (2 * half)) < half
    return jnp.where(first, pltpu.roll(v, width - half, 1), pltpu.roll(v, half, 1))


def _inproj_body(*refs, d_model, fused, ncb):
    xo_ref = refs[-1]
    if fused:
        xm_ref, f_ref, g2_ref = refs[:3]
        refs = refs[3:]
        x = xm_ref[...] + g2_ref[...] * _unpack_pairs(f_ref[...])
    else:
        ctx_ref, lat_ref = refs[:2]
        refs = refs[2:]
        x = jnp.where(pl.program_id(0) < ncb, ctx_ref[...], lat_ref[...])
    xo_ref[...] = x
    (a1_ref, b1_ref, cr_ref, sr_ref, cm_ref, sm_ref, w_ref, wuq_ref, wk_ref, wv_ref, qn_ref, kvn_ref,
     ones_ref, rq_ref, rk_ref, rv_ref, rg_ref, cb_ref, cu_ref, qm_ref, km_ref, vm_ref, gt_ref) = refs[:23]
    h = (_rms(x) * a1_ref[...] + b1_ref[...]).astype(BF16)

    def mm(off, width):
        return _dot(h, w_ref[:, off:off + width])

    cr = jnp.concatenate([cr_ref[...]] * (RQ_W // cr_ref.shape[1]), axis=1)
    sr = jnp.concatenate([sr_ref[...]] * (RQ_W // sr_ref.shape[1]), axis=1)
    q = mm(_O_RQ, RQ_W)
    rq_ref[...] = (q * cr + _rot_half(q, RET_DK // 2) * sr).astype(BF16)
    k = mm(_O_RK, RQ_W)
    rk_ref[...] = ((k * cr + _rot_half(k, RET_DK // 2) * sr) * (RET_DK ** -0.5)).astype(BF16)
    rv_ref[...] = mm(_O_RV, RV_W).astype(BF16)
    rg_ref[...] = _silu(mm(_O_RG, RV_W)).astype(BF16)
    cb_ref[...] = mm(_O_CB, CONV_WIDTH).astype(BF16)
    cu_ref[...] = (mm(_O_CC, CONV_WIDTH) * mm(_O_CX, CONV_WIDTH)).astype(BF16)

    cm = cm_ref[...]
    sm = sm_ref[...]
    qn = (_rms(mm(_O_QD, MLA_Q_RANK)) * qn_ref[...]).astype(BF16)
    for hd in range(MLA_HEADS):
        lo = hd * HEAD_PAD
        qa = _dot(qn, wuq_ref[:, lo:lo + HEAD_PAD])
        qm_ref[:, lo:lo + HEAD_PAD] = (qa * cm + _rot_half(qa, MLA_ROPE // 4) * sm).astype(BF16)

    kvn = (_rms(mm(_O_KVD, MLA_KV_RANK)) * kvn_ref[...]).astype(BF16)
    kr = mm(_O_KR, HEAD_PAD)
    kr = kr * cm + _rot_half(kr, MLA_ROPE // 4) * sm
    kn = _dot(kvn, wk_ref[...])
    for hd in range(MLA_HEADS):
        lo = hd * HEAD_PAD
        km_ref[:, lo:lo + HEAD_PAD] = (kn[:, lo:lo + HEAD_PAD] + kr).astype(BF16)
    vm_ref[...] = (_dot(kvn, wv_ref[...]) + ones_ref[...]).astype(BF16)

    for br in range(N_BRANCH):
        gt_ref[:, br * d_model:(br + 1) * d_model] = jax.nn.sigmoid(
            mm(_O_GT + br * d_model, d_model)).astype(BF16)


def _inproj_call(x_parts, a1, b1, tabs, wts, *, n_batch, nj, ncb):
    fused = len(x_parts) == 3
    d = x_parts[0].shape[1]
    rows = n_batch * nj * ROW_BLOCK
    tm = ROW_BLOCK
    row = lambda j, b: (b * nj + j, 0)
    mod = lambda j, b: (2 * b + (j >= ncb).astype(jnp.int32), 0, 0)
    tab = lambda j, b: (j, 0)
    if fused:
        in_specs = [pl.BlockSpec((tm, d), row), pl.BlockSpec((tm, d // 2), row),
                    pl.BlockSpec((None, 1, d), mod)]
    else:
        in_specs = [pl.BlockSpec((tm, d), lambda j, b: (jnp.where(j < ncb, b * ncb + j, 0), 0)),
                    pl.BlockSpec((tm, d),
                                 lambda j, b: (jnp.where(j < ncb, 0, b * (nj - ncb) + j - ncb), 0))]
    in_specs += [pl.BlockSpec((None, 1, d), mod), pl.BlockSpec((None, 1, d), mod)]
    in_specs += [pl.BlockSpec((tm, t.shape[1]), tab) for t in tabs]
    in_specs += [_const_spec(w.shape) for w in wts]
    widths = (RQ_W, RQ_W, RV_W, RV_W, CONV_WIDTH, CONV_WIDTH, MLA_W, MLA_W, MLA_W, N_BRANCH * d)
    out_shape = [jax.ShapeDtypeStruct((rows, w), BF16) for w in widths]
    out_specs = [pl.BlockSpec((tm, w), row) for w in widths]
    out_shape.append(jax.ShapeDtypeStruct((rows, d), F32))
    out_specs.append(pl.BlockSpec((tm, d), row))
    return pl.pallas_call(
        functools.partial(_inproj_body, d_model=d, fused=fused, ncb=ncb),
        out_shape=out_shape,
        grid=(nj, n_batch),
        in_specs=in_specs,
        out_specs=out_specs,
        compiler_params=_params(2),
        name="in_proj",
    )(*x_parts, a1, b1, *tabs, *wts)


def _ret_direction(q_ref, k_ref, v_ref, y_ref, s_ref, dm_ref, xi_ref, zt_ref, cd_ref, chunk_order):
    states = [s_ref[hd] for hd in range(RET_HEADS)]
    for c in chunk_order:
        rows = slice(c * RET_CHUNK, (c + 1) * RET_CHUNK)
        q = q_ref[rows, :]
        k = k_ref[rows, :]
        v = v_ref[rows, :]
        qx = (q.astype(F32) * xi_ref[...]).astype(BF16)
        kz = (k.astype(F32) * zt_ref[...]).astype(BF16)
        for hd in range(RET_HEADS):
            ks = slice(hd * RET_DK, (hd + 1) * RET_DK)
            vs = slice(hd * RET_DV, (hd + 1) * RET_DV)
            vh = v[:, vs]
            sc = lax.dot_general(q[:, ks], k[:, ks], (((1,), (1,)), ((), ())),
                                 preferred_element_type=F32)
            inner = _dot((sc * dm_ref[hd]).astype(BF16), vh)
            cross = _dot(qx[:, ks], states[hd].astype(BF16))
            y_ref[rows, vs] = (inner + cross).astype(BF16)
            upd = lax.dot_general(kz[:, ks], vh, (((0,), (0,)), ((), ())),
                                  preferred_element_type=F32)
            states[hd] = cd_ref[hd] * states[hd] + upd
    for hd in range(RET_HEADS):
        s_ref[hd] = states[hd]


def _ret_body(qf_ref, kf_ref, vf_ref, qb_ref, kb_ref, vb_ref,
              dmf_ref, dmb_ref, xif_ref, ztf_ref, xib_ref, ztb_ref, cdf_ref, cdb_ref,
              yf_ref, yb_ref, sf_ref, sb_ref):
    @pl.when(pl.program_id(1) == 0)
    def _():
        sf_ref[...] = jnp.zeros_like(sf_ref)
        sb_ref[...] = jnp.zeros_like(sb_ref)

    n_chunks = ROW_BLOCK // RET_CHUNK
    _ret_direction(qf_ref, kf_ref, vf_ref, yf_ref, sf_ref, dmf_ref, xif_ref, ztf_ref, cdf_ref,
                   range(n_chunks))
    _ret_direction(qb_ref, kb_ref, vb_ref, yb_ref, sb_ref, dmb_ref, xib_ref, ztb_ref, cdb_ref,
                   range(n_chunks - 1, -1, -1))


def _ret_call(rq, rk, rv, consts, *, n_batch, nj, ncb):
    rows = rq.shape[0]
    tm = ROW_BLOCK
    fwd = lambda b, s: (b * nj + s, 0)

    def bwd(b, s):
        return (b * nj + jnp.where(s < ncb, ncb - 1 - s, nj - 1 - (s - ncb)), 0)

    specs = []
    for im in (fwd, bwd):
        specs += [pl.BlockSpec((tm, RQ_W), im), pl.BlockSpec((tm, RQ_W), im),
                  pl.BlockSpec((tm, RV_W), im)]
    specs += [_const_spec(c.shape) for c in consts]
    return pl.pallas_call(
        _ret_body,
        out_shape=[jax.ShapeDtypeStruct((rows, RV_W), BF16)] * 2,
        grid=(n_batch, nj),
        in_specs=specs,
        out_specs=[pl.BlockSpec((tm, RV_W), fwd), pl.BlockSpec((tm, RV_W), bwd)],
        scratch_shapes=[pltpu.VMEM((RET_HEADS, RET_DK, RET_DV), F32)] * 2,
        compiler_params=_params(2),
        name="retention",
    )(rq, rk, rv, rq, rk, rv, *consts)


def _pick_tile(n, candidates):
    for c in candidates:
        if n % c == 0:
            return c
    raise ValueError(f"no tile for {n}")


def _attn_body(q_ref, k_ref, v_ref, o_ref, s_ref, *, n_ctx, n_all, ncb):
    tq = q_ref.shape[0]
    heads = [slice(hh * HEAD_PAD, (hh + 1) * HEAD_PAD) for hh in range(2)]

    def attend(n_keys, tk):
        nt = n_keys // tk
        qs = [q_ref[:, hs] for hs in heads]

        def qk(t, mrun):
            r0 = pl.multiple_of(t * tk, tk)
            out = []
            for hh, hs in enumerate(heads):
                s = lax.dot_general(qs[hh], k_ref[pl.ds(r0, tk), hs], (((1,), (1,)), ((), ())),
                                    preferred_element_type=F32)
                s_ref[hh, t, :, 0:tk] = s
                m = mrun[hh]
                for cc in range(tk // LANES):
                    m = jnp.maximum(m, s[:, cc * LANES:(cc + 1) * LANES])
                out.append(m)
            return tuple(out)

        mrun = lax.fori_loop(0, nt, qk, tuple(jnp.full((tq, LANES), -jnp.inf, F32) for _ in heads),
                             unroll=True)
        mrow = [jnp.max(m, axis=-1, keepdims=True) for m in mrun]

        def pv(t, accs):
            r0 = pl.multiple_of(t * tk, tk)
            out = []
            for hh, hs in enumerate(heads):
                p = jnp.exp2(s_ref[hh, t, :, 0:tk] - mrow[hh]).astype(BF16)
                out.append(accs[hh] + _dot(p, v_ref[pl.ds(r0, tk), hs]))
            return tuple(out)

        accs = lax.fori_loop(0, nt, pv, tuple(jnp.zeros((tq, HEAD_PAD), F32) for _ in heads),
                             unroll=True)
        o_ref[...] = jnp.concatenate([a[:, :MLA_V] / a[:, MLA_V:MLA_V + 1] for a in accs],
                                     axis=-1).astype(BF16)

    j = pl.program_id(2)

    @pl.when(j < ncb)
    def _():
        attend(n_ctx, _pick_tile(n_ctx, (256, 128)))

    @pl.when(j >= ncb)
    def _():
        attend(n_all, _pick_tile(n_all, ATTN_KEY_TILES))


def _attn_call(qm, km, vm, *, n_batch, nj, ncb, n_ctx):
    rows = qm.shape[0]
    tm = ROW_BLOCK
    t_all = nj * tm
    tk = _pick_tile(t_all, ATTN_KEY_TILES)
    qmap = lambda b, hp, j: (b * nj + j, hp)
    kmap = lambda b, hp, j: (b, hp)
    return pl.pallas_call(
        functools.partial(_attn_body, n_ctx=n_ctx, n_all=t_all, ncb=ncb),
        out_shape=jax.ShapeDtypeStruct((rows, MLA_O), BF16),
        grid=(n_batch, MLA_HEADS // 2, nj),
        in_specs=[pl.BlockSpec((tm, 2 * HEAD_PAD), qmap),
                  pl.BlockSpec((t_all, 2 * HEAD_PAD), kmap),
                  pl.BlockSpec((t_all, 2 * HEAD_PAD), kmap)],
        out_specs=pl.BlockSpec((tm, 2 * MLA_V), qmap),
        scratch_shapes=[pltpu.VMEM((2, t_all // tk, tm, tk), F32)],
        compiler_params=_params(3),
        name="mla_attention",
    )(qm, km, vm)


def _merge_body(yf_ref, yb_ref, rg_ref, cb_ref, cu_ref, cup_ref, cun_ref, om_ref, gt_ref, x_ref,
                g1_ref, a2_ref, b2_ref, gn_ref, cw_ref, wro_ref, wco_ref, wmo_ref, wout_ref,
                wrh_ref, wrl_ref, rb_ref, xmid_ref, h2_ref, rt_ref, gw_ref, lg_ref, *, nj, ncb, d_model,
                nb):
    tm = x_ref.shape[0]
    step = pl.program_id(0)

    @pl.when(step == 0)
    def _():
        lg_ref[...] = jnp.zeros_like(lg_ref)

    cls, w_lo, w_hi = _route_rows(jnp.transpose(lg_ref[...])[:N_EXPERTS, :], rb_ref[...])
    rank, counts = _block_ranks(cls.astype(F32))
    rt_ref[...] = jnp.concatenate(
        [cls.astype(F32), rank, counts, jnp.zeros((ROUTE_ROWS - 3, tm), F32)], axis=0)
    gw_ref[...] = jnp.transpose(
        jnp.concatenate([w_lo, w_hi, jnp.zeros((LANES - 2, tm), F32)], axis=0))

    j = jnp.minimum(step, nb - 1) % nj
    seg_first = jnp.logical_or(j == 0, j == ncb)
    seg_last = jnp.logical_or(j == ncb - 1, j == nj - 1)

    y = yf_ref[...].astype(F32) + yb_ref[...].astype(F32)
    yn = jnp.concatenate([_rms(y[:, hd * RET_DV:(hd + 1) * RET_DV]) for hd in range(RET_HEADS)],
                         axis=-1) * gn_ref[...]
    y_ret = _dot((rg_ref[...].astype(F32) * yn).astype(BF16), wro_ref[...])

    u = cu_ref[...].astype(F32)
    ridx = lax.broadcasted_iota(jnp.int32, u.shape, 0)
    prev_row = cup_ref[...].astype(F32)[BF16_SUBLANES - 1:, :] * jnp.where(seg_first, 0.0, 1.0)
    next_row = cun_ref[...].astype(F32)[0:1, :] * jnp.where(seg_last, 0.0, 1.0)
    u_prev = jnp.where(ridx == 0, prev_row, pltpu.roll(u, 1, 0))
    u_next = jnp.where(ridx == tm - 1, next_row, pltpu.roll(u, tm - 1, 0))
    conv = u_prev * cw_ref[0:1, :] + u * cw_ref[1:2, :] + u_next * cw_ref[2:3, :]
    y_conv = _dot((cb_ref[...].astype(F32) * conv).astype(BF16), wco_ref[...])

    y_mla = _dot(om_ref[...], wmo_ref[...])

    merged = (gt_ref[:, 0:d_model].astype(F32) * y_ret
              + gt_ref[:, d_model:2 * d_model].astype(F32) * y_conv
              + gt_ref[:, 2 * d_model:3 * d_model].astype(F32) * y_mla)
    x_mid = x_ref[...] + g1_ref[...] * _dot(merged.astype(BF16), wout_ref[...])
    xmid_ref[...] = x_mid

    h2 = _rms(x_mid) * a2_ref[...] + b2_ref[...]
    h_hi, h_lo = _split_bf16(h2)
    lg_ref[...] = _dot(h_hi, wrh_ref[...]) + _dot(h_hi, wrl_ref[...]) + _dot(h_lo, wrh_ref[...])
    h2_ref[...] = _pack_pairs(h_hi)


def _top2_of4(v):
    def first_max(rows):
        best, idx = rows[0], jnp.zeros(rows[0].shape, jnp.int32)
        for e in range(1, len(rows)):
            better = rows[e] > best
            idx = jnp.where(better, e, idx)
            best = jnp.where(better, rows[e], best)
        return best, idx

    b1, i1 = first_max(v)
    b2, i2 = first_max([jnp.where(i1 == e, -jnp.inf, v[e]) for e in range(len(v))])
    return i1, i2, b1, b2


def _route_rows(logits_t, bias):
    scores = jax.nn.sigmoid(logits_t)
    biased = scores + bias
    row = lambda a, e: a[e:e + 1, :]
    best = None
    for g in range(N_GROUPS):
        v = [row(biased, g * EXPERTS_PER_GROUP + e) for e in range(EXPERTS_PER_GROUP)]
        i1, i2, b1, b2 = _top2_of4(v)
        cand = (b1 + b2, jnp.full(i1.shape, g, jnp.int32), i1, i2)
        if best is None:
            best = cand
        else:
            better = cand[0] > best[0]
            best = tuple(jnp.where(better, c, o) for c, o in zip(cand, best))
    _, g_sel, i1, i2 = best
    lo = jnp.minimum(i1, i2)
    hi = jnp.maximum(i1, i2)
    e_lo = g_sel * EXPERTS_PER_GROUP + lo
    e_hi = g_sel * EXPERTS_PER_GROUP + hi
    s_lo = jnp.zeros_like(best[0])
    s_hi = jnp.zeros_like(best[0])
    for e in range(N_EXPERTS):
        s_lo = jnp.where(e_lo == e, row(scores, e), s_lo)
        s_hi = jnp.where(e_hi == e, row(scores, e), s_hi)
    total = s_lo + s_hi
    pair_base = jnp.where(lo == 0, 0, jnp.where(lo == 1, 3, 5))
    cls = g_sel * len(PAIRS) + pair_base + (hi - lo - 1)
    return cls, s_lo / total, s_hi / total


def _block_ranks(cls_row):
    n = cls_row.shape[1]
    cls_col = jnp.transpose(jnp.broadcast_to(cls_row, (LANES, n)))[:, 0:1]
    ii = lax.broadcasted_iota(jnp.int32, (n, n), 0)
    jj = lax.broadcasted_iota(jnp.int32, (n, n), 1)
    earlier_same = jnp.logical_and(cls_col == cls_row, ii < jj)
    rank = jnp.sum(jnp.where(earlier_same, 1.0, 0.0), axis=0, keepdims=True)
    lane = lax.broadcasted_iota(jnp.int32, (1, n), 1)
    counts = jnp.zeros((1, n), F32)
    for c in range(N_CLASSES):
        cnt = jnp.sum(jnp.where(cls_row == c, 1.0, 0.0), axis=1, keepdims=True)
        counts = jnp.where(lane == c, cnt, counts)
    return rank, counts


def _merge_call(yf, yb, rg, cb, cu, om, gt, xa, g1, a2, b2, wts, *, nj, ncb):
    rows, d = xa.shape
    tm = ROW_BLOCK
    nb = rows // tm
    halo = BF16_SUBLANES
    per_blk = tm // halo
    blk = lambda i: jnp.minimum(i, nb - 1)
    row = lambda i: (blk(i), 0)
    mod = lambda i: (2 * (blk(i) // nj) + ((blk(i) % nj) >= ncb).astype(jnp.int32), 0, 0)
    prev = lambda i: (jnp.maximum(blk(i) * per_blk - 1, 0), 0)
    nxt = lambda i: (jnp.minimum((blk(i) + 1) * per_blk, rows // halo - 1), 0)
    in_specs = [pl.BlockSpec((tm, RV_W), row), pl.BlockSpec((tm, RV_W), row),
                pl.BlockSpec((tm, RV_W), row),
                pl.BlockSpec((tm, CONV_WIDTH), row), pl.BlockSpec((tm, CONV_WIDTH), row),
                pl.BlockSpec((halo, CONV_WIDTH), prev), pl.BlockSpec((halo, CONV_WIDTH), nxt),
                pl.BlockSpec((tm, MLA_O), row), pl.BlockSpec((tm, N_BRANCH * d), row),
                pl.BlockSpec((tm, d), row),
                pl.BlockSpec((None, 1, d), mod), pl.BlockSpec((None, 1, d), mod),
                pl.BlockSpec((None, 1, d), mod)]
    in_specs += [_const_spec(w.shape) for w in wts]
    return pl.pallas_call(
        functools.partial(_merge_body, nj=nj, ncb=ncb, d_model=d, nb=nb),
        out_shape=[jax.ShapeDtypeStruct((rows, d), F32), jax.ShapeDtypeStruct((rows, d // 2), jnp.uint32),
                   jax.ShapeDtypeStruct((nb * ROUTE_ROWS, tm), F32),
                   jax.ShapeDtypeStruct((rows, LANES), F32)],
        grid=(nb + 1,),
        in_specs=in_specs,
        out_specs=[pl.BlockSpec((tm, d), row), pl.BlockSpec((tm, d // 2), row),
                   pl.BlockSpec((ROUTE_ROWS, tm), lambda i: (jnp.maximum(i - 1, 0), 0)),
                   pl.BlockSpec((tm, LANES), lambda i: (jnp.maximum(i - 1, 0), 0))],
        scratch_shapes=[pltpu.VMEM((tm, LANES), F32)],
        compiler_params=_params(1),
        name="merge_out_proj",
    )(yf, yb, rg, cb, cu, cu, cu, om, gt, xa, g1, a2, b2, *wts)


def _moe_body(ea_ref, eb_ref, nu_ref, h_ref, gw_ref, w1a_ref, w3a_ref, w2a_ref, w1b_ref, w3b_ref,
              w2b_ref, o_ref, w13a_s, w2a_s, w13b_s, w2b_s):
    t = pl.program_id(0)
    used = t < nu_ref[0]
    prev = jnp.maximum(t - 1, 0)

    def refresh(e_ref, w1_ref, w3_ref, w2_ref, w13_s, w2_s):
        @pl.when(jnp.logical_and(used, jnp.logical_or(t == 0, e_ref[t] != e_ref[prev])))
        def _():
            w13_s[:, :D_EXPERT] = w1_ref[...].astype(BF16)
            w13_s[:, D_EXPERT:] = w3_ref[...].astype(BF16)
            w2_s[...] = w2_ref[...].astype(BF16)

    refresh(ea_ref, w1a_ref, w3a_ref, w2a_ref, w13a_s, w2a_s)
    refresh(eb_ref, w1b_ref, w3b_ref, w2b_ref, w13b_s, w2b_s)

    @pl.when(used)
    def _():
        h = _unpack_pairs(h_ref[...]).astype(BF16)
        gw = gw_ref[...]

        def expert(w13_s, w2_s, wt):
            a = _dot(h, w13_s[...])
            act = _silu(a[:, :D_EXPERT]) * a[:, D_EXPERT:] * wt
            return _dot(act.astype(BF16), w2_s[...])

        o_ref[...] = _pack_pairs(expert(w13a_s, w2a_s, gw[:, 0:1])
                                 + expert(w13b_s, w2b_s, gw[:, 1:2]))

    @pl.when(jnp.logical_not(used))
    def _():
        o_ref[...] = jnp.zeros_like(o_ref)


def _moe_call(tile_ea, tile_eb, n_used, hs, gw, w1, w3, w2, layer):
    npad = hs.shape[0]
    d = w1.shape[2]
    tmo = MOE_TILE
    row = lambda t, ea, eb, nu: (t, 0)
    wa = lambda t, ea, eb, nu: (layer, ea[t], 0, 0)
    wb = lambda t, ea, eb, nu: (layer, eb[t], 0, 0)
    up = lambda im: pl.BlockSpec((None, None, d, D_EXPERT), im)
    down = lambda im: pl.BlockSpec((None, None, D_EXPERT, d), im)
    grid_spec = pltpu.PrefetchScalarGridSpec(
        num_scalar_prefetch=3,
        grid=(npad // tmo,),
        in_specs=[pl.BlockSpec((tmo, d // 2), row), pl.BlockSpec((tmo, LANES), row),
                  up(wa), up(wa), down(wa), up(wb), up(wb), down(wb)],
        out_specs=pl.BlockSpec((tmo, d // 2), row),
        scratch_shapes=[pltpu.VMEM((d, 2 * D_EXPERT), BF16), pltpu.VMEM((D_EXPERT, d), BF16)] * 2)
    return pl.pallas_call(
        _moe_body,
        out_shape=jax.ShapeDtypeStruct((npad, d // 2), jnp.uint32),
        grid_spec=grid_spec,
        compiler_params=_params(1),
        name="moe_experts",
    )(tile_ea, tile_eb, n_used, hs, gw, w1, w3, w2, w1, w3, w2)


def _dispatch(route, n_tok):
    tmo = MOE_TILE
    n_tiles = n_tok // tmo + N_CLASSES
    npad = n_tiles * tmo
    cls = route[:, 0, :].astype(jnp.int32)
    rank = route[:, 1, :].astype(jnp.int32)
    counts = route[:, 2, :N_CLASSES].astype(jnp.int32)
    tiles_per = (jnp.sum(counts, axis=0) + tmo - 1) // tmo
    tile_end = jnp.cumsum(tiles_per)
    offs = (tile_end - tiles_per) * tmo
    block_base = offs[None, :] + jnp.cumsum(counts, axis=0) - counts
    classes = jnp.arange(N_CLASSES, dtype=jnp.int32)
    base = jnp.sum(jnp.where(cls[:, :, None] == classes, block_base[:, None, :], 0), axis=-1)
    dest = (base + rank).reshape(-1)
    tile_ids = jnp.arange(n_tiles, dtype=jnp.int32)
    tile_cls = jnp.sum((tile_end[None, :] <= tile_ids[:, None]).astype(jnp.int32), axis=1)
    tile_cls = jnp.minimum(tile_cls, N_CLASSES - 1)
    pa = jnp.asarray([p[0] for p in PAIRS], jnp.int32)
    pb = jnp.asarray([p[1] for p in PAIRS], jnp.int32)
    group = (tile_cls // len(PAIRS)) * EXPERTS_PER_GROUP
    tile_ea = group + pa[tile_cls % len(PAIRS)]
    tile_eb = group + pb[tile_cls % len(PAIRS)]
    n_used = tile_end[-1:].astype(jnp.int32)
    return dest, npad, tile_ea, tile_eb, n_used


def _sc_mesh():
    return plsc.VectorSubcoreMesh(core_axis_name="c", subcore_axis_name="s",
                                  num_cores=SC_CORES, num_subcores=SC_SUBCORES)


def _sc_chunks(idx):
    assert idx.shape[0] % (SC_WORKERS * GATHER_CHUNK) == 0
    return idx.reshape(SC_WORKERS, -1, GATHER_CHUNK)


def _sc_pipeline(n_chunks, fetch, put):
    assert n_chunks % 2 == 0
    fetch(0, 0).start()

    @pl.loop(0, n_chunks, step=2)
    def _(c0):
        for b in (0, 1):
            c = c0 + b
            fetch(c, b).wait()

            @pl.when(c >= 1)
            def _():
                put(c - 1, 1 - b).wait()

            @pl.when(c + 1 < n_chunks)
            def _():
                fetch(c + 1, 1 - b).start()

            put(c, b).start()

    put(n_chunks - 1, 1).wait()


def _sc_scratch(n_chunks, d, dtype):
    return [pltpu.VMEM((n_chunks, GATHER_CHUNK), jnp.int32),
            pltpu.VMEM((2, GATHER_CHUNK, d), dtype),
            pltpu.SemaphoreType.DMA((2,)), pltpu.SemaphoreType.DMA((2,))]


def _sc_gather_rows(table, idx):
    d = table.shape[1]
    idx3 = _sc_chunks(idx)
    n_chunks = idx3.shape[1]
    assert table.dtype.itemsize == 4

    @functools.partial(
        pl.kernel, mesh=_sc_mesh(),
        out_type=jax.ShapeDtypeStruct((idx.shape[0], d), table.dtype),
        scratch_types=_sc_scratch(n_chunks, d, table.dtype),
        name="sc_row_gather")
    def gather(table_hbm, idx_hbm, out_hbm, idx_v, rows_v, sem_in, sem_out):
        worker = lax.axis_index("s") * SC_CORES + lax.axis_index("c")
        pltpu.sync_copy(idx_hbm.at[worker], idx_v)

        def fetch(c, b):
            return pltpu.make_async_copy(table_hbm.at[idx_v.at[c]], rows_v.at[b], sem_in.at[b])

        def put(c, b):
            off = (worker * n_chunks + c) * GATHER_CHUNK
            return pltpu.make_async_copy(rows_v.at[b], out_hbm.at[pl.ds(off, GATHER_CHUNK)],
                                         sem_out.at[b])

        _sc_pipeline(n_chunks, fetch, put)

    return gather(table, idx3)


def _sc_scatter_rows(rows, idx, n_out):
    d = rows.shape[1]
    idx3 = _sc_chunks(idx)
    n_chunks = idx3.shape[1]
    assert rows.dtype.itemsize == 4

    @functools.partial(
        pl.kernel, mesh=_sc_mesh(),
        out_type=jax.ShapeDtypeStruct((n_out, d), rows.dtype),
        scratch_types=_sc_scratch(n_chunks, d, rows.dtype),
        name="sc_row_scatter")
    def scatter(rows_hbm, idx_hbm, out_hbm, idx_v, rows_v, sem_in, sem_out):
        worker = lax.axis_index("s") * SC_CORES + lax.axis_index("c")
        pltpu.sync_copy(idx_hbm.at[worker], idx_v)

        def fetch(c, b):
            off = (worker * n_chunks + c) * GATHER_CHUNK
            return pltpu.make_async_copy(rows_hbm.at[pl.ds(off, GATHER_CHUNK)], rows_v.at[b],
                                         sem_in.at[b])

        def put(c, b):
            return pltpu.make_async_copy(rows_v.at[b], out_hbm.at[idx_v.at[c]], sem_out.at[b])

        _sc_pipeline(n_chunks, fetch, put)

    return scatter(rows, idx3)


def _final_body(x_ref, f_ref, g2_ref, fn_ref, o_ref):
    o_ref[...] = _rms(x_ref[...] + g2_ref[...] * _unpack_pairs(f_ref[...])) * fn_ref[...]


def _final_call(xmid, fg, g2, final_norm, *, n_batch, nj, ncb):
    rows, d = xmid.shape
    tm = ROW_BLOCK
    njl = nj - ncb
    src = lambda b, j: (b * nj + ncb + j, 0)
    return pl.pallas_call(
        _final_body,
        out_shape=jax.ShapeDtypeStruct((n_batch * njl * tm, d), F32),
        grid=(n_batch, njl),
        in_specs=[pl.BlockSpec((tm, d), src), pl.BlockSpec((tm, d // 2), src),
                  pl.BlockSpec((None, 1, d), lambda b, j: (2 * b + 1, 0, 0)),
                  _const_spec((1, d))],
        out_specs=pl.BlockSpec((tm, d), lambda b, j: (b * njl + j, 0)),
        compiler_params=_params(2),
        name="final_norm",
    )(xmid, fg, g2, final_norm.reshape(1, d))


def _split_cols(w, sizes):
    out, off = [], 0
    for s in sizes:
        out.append(w[:, off:off + s])
        off += s
    out.append(w[:, off:])
    return out


def _layer_weights(w_in, w_uq, w_ukv, q_norm, kv_norm):
    wq, wk, wv, wg, wcb, wcc, wcx, wqd, wkvd, wkr, wgate = _split_cols(w_in, IN_SIZES[:-1])
    rope_lanes = lambda w: jnp.pad(w, ((0, 0), (MLA_NOPE, HEAD_PAD - MLA_NOPE - MLA_ROPE)))
    w_ext = jnp.concatenate(
        [wq, wk, wv, wg, wcb, wcc, wcx, wqd, wkvd, rope_lanes(wkr), wgate], axis=1).astype(BF16)

    uq = w_uq.reshape(MLA_Q_RANK, MLA_HEADS, MLA_NOPE + MLA_ROPE)
    tail = HEAD_PAD - MLA_NOPE - MLA_ROPE
    uq_pad = jnp.pad(uq, ((0, 0), (0, 0), (0, tail)))
    wuq = uq_pad.reshape(MLA_Q_RANK, MLA_W).astype(BF16)

    ukv = w_ukv.reshape(MLA_KV_RANK, MLA_HEADS, MLA_NOPE + MLA_V)
    wk_up = jnp.pad(ukv[..., :MLA_NOPE], ((0, 0), (0, 0), (0, HEAD_PAD - MLA_NOPE)))
    wk_up = wk_up.reshape(MLA_KV_RANK, MLA_W).astype(BF16)
    wv_pad = jnp.pad(ukv[..., MLA_NOPE:], ((0, 0), (0, 0), (0, HEAD_PAD - MLA_V)))
    wv_pad = wv_pad.reshape(MLA_KV_RANK, MLA_W).astype(BF16)
    ones_row = jnp.zeros((MLA_HEADS, HEAD_PAD), F32).at[:, MLA_V].set(1.0).reshape(1, MLA_W)
    q_gain = q_norm.astype(F32) * ((MLA_NOPE + MLA_ROPE) ** -0.5 * math.log2(math.e))
    return (w_ext, wuq, wk_up, wv_pad, q_gain.reshape(1, -1),
            kv_norm.reshape(1, -1).astype(F32), ones_row)


def _rotary_tables(n_ctx, n_lat):
    n_all = n_ctx + n_lat
    row = lax.broadcasted_iota(jnp.int32, (n_all, LANES), 0)
    lane = lax.broadcasted_iota(jnp.int32, (n_all, LANES), 1)
    is_ctx = row < n_ctx
    pos = row - n_ctx
    grid_row = pos // GRID_W
    grid_col = pos - grid_row * GRID_W

    def inv_freq(idx, half):
        return ROPE_BASE ** (-idx.astype(F32) / half)

    def table(ang, first_half, live):
        cos = jnp.where(is_ctx, 1.0, jnp.cos(ang))
        sin = jnp.where(is_ctx, 0.0, jnp.where(first_half, -jnp.sin(ang), jnp.sin(ang)))
        return jnp.where(live, cos, 0.0), jnp.where(live, sin, 0.0)

    half = RET_DK // 2
    ang = pos.astype(F32) * inv_freq(lane % half, half)
    cr, sr = table(ang, (lane % RET_DK) < half, True)

    quarter = MLA_ROPE // 4
    rl = lane - MLA_NOPE
    coord = jnp.where(rl < MLA_ROPE // 2, grid_row, grid_col)
    ang = coord.astype(F32) * inv_freq(rl % quarter, quarter)
    cm, sm = table(ang, (rl % (2 * quarter)) < quarter,
                   jnp.logical_and(rl >= 0, rl < MLA_ROPE))
    cm = jnp.where(rl < 0, 1.0, cm)
    return cr, sr, cm, sm


def _retention_consts(ret_decay):
    log_gf = jax.nn.log_sigmoid(ret_decay[0].astype(F32))
    log_gb = jax.nn.log_sigmoid(ret_decay[1].astype(F32))
    idx = jnp.arange(RET_CHUNK, dtype=F32)
    rel = idx[:, None] - idx[None, :]
    dm_f = jnp.where(rel >= 0, jnp.exp(log_gf[:, None, None] * jnp.maximum(rel, 0.0)[None]), 0.0)
    dm_b = jnp.where(rel < 0, jnp.exp(log_gb[:, None, None] * jnp.maximum(-rel, 0.0)[None]), 0.0)

    def lanes(t):
        return jnp.repeat(t, RET_DK, axis=1)

    xi_f = lanes(jnp.exp(log_gf[None, :] * (idx + 1.0)[:, None]))
    zt_f = lanes(jnp.exp(log_gf[None, :] * (RET_CHUNK - 1 - idx)[:, None]))
    xi_b = lanes(jnp.exp(log_gb[None, :] * (RET_CHUNK - idx)[:, None]))
    zt_b = lanes(jnp.exp(log_gb[None, :] * idx[:, None]))
    cd = lambda lg: jnp.broadcast_to(jnp.exp(lg * RET_CHUNK)[:, None, None], (RET_HEADS, 1, RET_DV))
    return dm_f, dm_b, xi_f, zt_f, xi_b, zt_b, cd(log_gf), cd(log_gb)


def kernel(x, c, ctx, c_ctx, w_ada, b_ada, norm1, norm2, w_in, ret_decay, ret_gn, w_ret_o, conv_w,
           w_conv_o, mla_q_norm, w_uq, mla_kv_norm, w_ukv, w_mla_o, w_out, w_router, router_bias,
           w1, w3, w2, final_norm):
    n_batch, n_lat, d = x.shape
    n_ctx = ctx.shape[1]
    depth = w_ada.shape[0]
    t_all = n_ctx + n_lat
    assert n_ctx % ROW_BLOCK == 0 and n_lat % ROW_BLOCK == 0 and n_lat % GRID_W == 0
    nj = t_all // ROW_BLOCK
    ncb = n_ctx // ROW_BLOCK
    n_tok = n_batch * t_all
    assert n_tok % MOE_TILE == 0
    geom = dict(n_batch=n_batch, nj=nj, ncb=ncb)

    cc = jnp.concatenate([c, c_ctx[None, :]], axis=0)
    cc = jnp.pad(cc, ((0, -cc.shape[0] % 8), (0, 0)))
    mod = _ada_call(cc, w_ada, b_ada)[:, :n_batch + 1].reshape(depth, n_batch + 1, N_MOD, d)
    pick = jnp.stack([jnp.full((n_batch,), n_batch, jnp.int32),
                      jnp.arange(n_batch, dtype=jnp.int32)], axis=1).reshape(-1)
    mod = mod[:, pick]

    tabs = _rotary_tables(n_ctx, n_lat)
    wr_hi, wr_lo = _split_bf16(jnp.pad(w_router.astype(F32), ((0, 0), (0, LANES - N_EXPERTS))))
    rbias = router_bias.astype(F32).reshape(N_EXPERTS, 1)

    out = None
    x_parts = (ctx.reshape(n_batch * n_ctx, d), x.reshape(n_batch * n_lat, d))
    for l in range(depth):
        m = mod[l]
        rowvec = lambda v: v.reshape(2 * n_batch, 1, d)
        a1 = rowvec(norm1[l][None, :] * (1.0 + m[:, 1]))
        b1 = rowvec(m[:, 0])
        g1 = rowvec(m[:, 2])
        a2 = rowvec(norm2[l][None, :] * (1.0 + m[:, 4]))
        b2 = rowvec(m[:, 3])
        g2 = rowvec(m[:, 5])

        wts = _layer_weights(w_in[l], w_uq[l], w_ukv[l], mla_q_norm[l], mla_kv_norm[l])
        proj = _inproj_call(x_parts, a1, b1, tabs, wts, **geom)
        rq, rk, rv, rg, cb, cu, qm, km, vm, gt, xa = proj

        yf, yb = _ret_call(rq, rk, rv, _retention_consts(ret_decay[l]), **geom)
        om = _attn_call(qm, km, vm, n_ctx=n_ctx, **geom)

        merge_wts = (ret_gn[l].reshape(1, -1).astype(F32), conv_w[l].T.astype(F32),
                     w_ret_o[l].astype(BF16), w_conv_o[l].astype(BF16), w_mla_o[l].astype(BF16),
                     w_out[l].astype(BF16), wr_hi, wr_lo, rbias)
        xmid, h2, route, gate_w = _merge_call(yf, yb, rg, cb, cu, om, gt, xa, g1, a2, b2, merge_wts,
                                      nj=nj, ncb=ncb)
        route = route.reshape(-1, ROUTE_ROWS, ROW_BLOCK)
        dest, npad, tile_ea, tile_eb, n_used = _dispatch(route, n_tok)
        hs = _sc_scatter_rows(h2, dest, npad)
        gw = _sc_scatter_rows(gate_w, dest, npad)
        f_sorted = _moe_call(tile_ea, tile_eb, n_used, hs, gw, w1, w3, w2, l)
        fg = _sc_gather_rows(f_sorted, dest)

        if l < depth - 1:
            x_parts = (xmid, fg, g2)
        else:
            out = _final_call(xmid, fg, g2, final_norm, **geom)
    return out.reshape(n_batch, n_lat, d)
```

```python
import functools
import math

import jax
import jax.numpy as jnp
from jax import lax
from jax.experimental import pallas as pl
from jax.experimental.pallas import tpu as pltpu
from jax.experimental.pallas import tpu_sc as plsc

F32 = jnp.float32
BF16 = jnp.bfloat16

GRID_W = 64
RMS_EPS = 1e-6
ROPE_BASE = 10000.0
N_MOD = 6
RET_HEADS = 4
RET_DK = 64
RET_DV = 128
RET_CHUNK = 256
CONV_WIDTH = 512
MLA_HEADS = 8
MLA_Q_RANK = 384
MLA_KV_RANK = 256
MLA_NOPE = 64
MLA_ROPE = 32
MLA_V = 64
N_BRANCH = 3
N_EXPERTS = 16
N_GROUPS = 4
EXPERTS_PER_GROUP = N_EXPERTS // N_GROUPS
D_EXPERT = 512
IN_SIZES = (RET_HEADS * RET_DK, RET_HEADS * RET_DK, RET_HEADS * RET_DV, RET_HEADS * RET_DV,
            CONV_WIDTH, CONV_WIDTH, CONV_WIDTH, MLA_Q_RANK, MLA_KV_RANK, MLA_ROPE, 0)

LANES = 128
BF16_SUBLANES = 16
VMEM_LIMIT = 56 * 1024 * 1024
SC_CORES = 2
SC_SUBCORES = 16
SC_WORKERS = SC_CORES * SC_SUBCORES

HEAD_PAD = LANES
ROW_BLOCK = 256
MOE_TILE = 512
ATTN_KEY_TILES = (768, 512, 256, 128)
ROUTE_ROWS = 8
GATHER_CHUNK = 96
PAIRS = [(a, b) for a in range(EXPERTS_PER_GROUP) for b in range(a + 1, EXPERTS_PER_GROUP)]
N_CLASSES = N_GROUPS * len(PAIRS)

RQ_W = RET_HEADS * RET_DK
RV_W = RET_HEADS * RET_DV
MLA_W = MLA_HEADS * HEAD_PAD
MLA_O = MLA_HEADS * MLA_V


def _const_spec(shape):
    nd = len(shape)
    return pl.BlockSpec(shape, lambda *_: (0,) * nd, pipeline_mode=pl.Buffered(1))


def _params(n_axes):
    return pltpu.CompilerParams(dimension_semantics=("arbitrary",) * n_axes,
                                vmem_limit_bytes=VMEM_LIMIT)


def _dot(a, b):
    return jnp.dot(a, b, preferred_element_type=F32)


def _split_bf16(a):
    hi = a.astype(BF16)
    lo = (a - hi.astype(F32)).astype(BF16)
    return hi, lo


def _silu(v):
    return v * jax.nn.sigmoid(v)


def _rms(v):
    return v * lax.rsqrt(jnp.mean(v * v, axis=-1, keepdims=True) + RMS_EPS)


def _pack_pairs(v):
    half = v.shape[1] // 2
    bits = lambda t: lax.bitcast_convert_type(t.astype(BF16).astype(F32), jnp.uint32)
    return (bits(v[:, :half]) & jnp.uint32(0xFFFF0000)) | (bits(v[:, half:]) >> 16)


def _unpack_pairs(u):
    hi = lax.bitcast_convert_type(u & jnp.uint32(0xFFFF0000), F32)
    lo = lax.bitcast_convert_type(u << 16, F32)
    return jnp.concatenate([hi, lo], axis=1)


def _ada_body(c_ref, w_ref, b_ref, o_ref):
    a_hi, a_lo = _split_bf16(_silu(c_ref[...]))
    w_hi, w_lo = _split_bf16(w_ref[...])
    o_ref[...] = _dot(a_hi, w_hi) + _dot(a_hi, w_lo) + _dot(a_lo, w_hi) + b_ref[...]


def _ada_call(cc, w_ada, b_ada):
    depth, d, nm = w_ada.shape
    rows = cc.shape[0]
    cb = nm // 4
    return pl.pallas_call(
        _ada_body,
        out_shape=jax.ShapeDtypeStruct((depth, rows, nm), F32),
        grid=(depth, nm // cb),
        in_specs=[pl.BlockSpec((rows, d), lambda l, n: (0, 0)),
                  pl.BlockSpec((None, d, cb), lambda l, n: (l, 0, n)),
                  pl.BlockSpec((None, 1, cb), lambda l, n: (l, 0, n))],
        out_specs=pl.BlockSpec((None, rows, cb), lambda l, n: (l, 0, n)),
        compiler_params=_params(2),
        name="ada_mod",
    )(cc, w_ada, b_ada.reshape(depth, 1, nm))


_O_RQ = 0
_O_RK = _O_RQ + RQ_W
_O_RV = _O_RK + RQ_W
_O_RG = _O_RV + RV_W
_O_CB = _O_RG + RV_W
_O_CC = _O_CB + CONV_WIDTH
_O_CX = _O_CC + CONV_WIDTH
_O_QD = _O_CX + CONV_WIDTH
_O_KVD = _O_QD + MLA_Q_RANK
_O_KR = _O_KVD + MLA_KV_RANK
_O_GT = _O_KR + LANES


def _rot_half(v, half):
    width = v.shape[1]
    lane = lax.broadcasted_iota(jnp.int32, v.shape, 1)
    first = (lane % (2 * half)) < half
    return jnp.where(first, pltpu.roll(v, width - half, 1), pltpu.roll(v, half, 1))


def _inproj_body(*refs, d_model, fused, ncb):
    xo_ref = refs[-1]
    if fused:
        xm_ref, f_ref, g2_ref = refs[:3]
        refs = refs[3:]
        x = xm_ref[...] + g2_ref[...] * _unpack_pairs(f_ref[...])
    else:
        ctx_ref, lat_ref = refs[:2]
        refs = refs[2:]
        x = jnp.where(pl.program_id(0) < ncb, ctx_ref[...], lat_ref[...])
    xo_ref[...] = x
    (a1_ref, b1_ref, cr_ref, sr_ref, cm_ref, sm_ref, w_ref, wuq_ref, wk_ref, wv_ref, qn_ref, kvn_ref,
     ones_ref, rq_ref, rk_ref, rv_ref, rg_ref, cb_ref, cu_ref, qm_ref, km_ref, vm_ref, gt_ref) = refs[:23]
    h = (_rms(x) * a1_ref[...] + b1_ref[...]).astype(BF16)

    def mm(off, width):
        return _dot(h, w_ref[:, off:off + width])

    cr = jnp.concatenate([cr_ref[...]] * (RQ_W // cr_ref.shape[1]), axis=1)
    sr = jnp.concatenate([sr_ref[...]] * (RQ_W // sr_ref.shape[1]), axis=1)
    q = mm(_O_RQ, RQ_W)
    rq_ref[...] = (q * cr + _rot_half(q, RET_DK // 2) * sr).astype(BF16)
    k = mm(_O_RK, RQ_W)
    rk_ref[...] = ((k * cr + _rot_half(k, RET_DK // 2) * sr) * (RET_DK ** -0.5)).astype(BF16)
    rv_ref[...] = mm(_O_RV, RV_W).astype(BF16)
    rg_ref[...] = _silu(mm(_O_RG, RV_W)).astype(BF16)
    cb_ref[...] = mm(_O_CB, CONV_WIDTH).astype(BF16)
    cu_ref[...] = (mm(_O_CC, CONV_WIDTH) * mm(_O_CX, CONV_WIDTH)).astype(BF16)

    cm = cm_ref[...]
    sm = sm_ref[...]
    qn = (_rms(mm(_O_QD, MLA_Q_RANK)) * qn_ref[...]).astype(BF16)
    for hd in range(MLA_HEADS):
        lo = hd * HEAD_PAD
        qa = _dot(qn, wuq_ref[:, lo:lo + HEAD_PAD])
        qm_ref[:, lo:lo + HEAD_PAD] = (qa * cm + _rot_half(qa, MLA_ROPE // 4) * sm).astype(BF16)

    kvn = (_rms(mm(_O_KVD, MLA_KV_RANK)) * kvn_ref[...]).astype(BF16)
    kr = mm(_O_KR, HEAD_PAD)
    kr = kr * cm + _rot_half(kr, MLA_ROPE // 4) * sm
    kn = _dot(kvn, wk_ref[...])
    for hd in range(MLA_HEADS):
        lo = hd * HEAD_PAD
        km_ref[:, lo:lo + HEAD_PAD] = (kn[:, lo:lo + HEAD_PAD] + kr).astype(BF16)
    vm_ref[...] = (_dot(kvn, wv_ref[...]) + ones_ref[...]).astype(BF16)

    for br in range(N_BRANCH):
        gt_ref[:, br * d_model:(br + 1) * d_model] = jax.nn.sigmoid(
            mm(_O_GT + br * d_model, d_model)).astype(BF16)


def _inproj_call(x_parts, a1, b1, tabs, wts, *, n_batch, nj, ncb):
    fused = len(x_parts) == 3
    d = x_parts[0].shape[1]
    rows = n_batch * nj * ROW_BLOCK
    tm = ROW_BLOCK
    row = lambda j, b: (b * nj + j, 0)
    mod = lambda j, b: (2 * b + (j >= ncb).astype(jnp.int32), 0, 0)
    tab = lambda j, b: (j, 0)
    if fused:
        in_specs = [pl.BlockSpec((tm, d), row), pl.BlockSpec((tm, d // 2), row),
                    pl.BlockSpec((None, 1, d), mod)]
    else:
        in_specs = [pl.BlockSpec((tm, d), lambda j, b: (jnp.where(j < ncb, b * ncb + j, 0), 0)),
                    pl.BlockSpec((tm, d),
                                 lambda j, b: (jnp.where(j < ncb, 0, b * (nj - ncb) + j - ncb), 0))]
    in_specs += [pl.BlockSpec((None, 1, d), mod), pl.BlockSpec((None, 1, d), mod)]
    in_specs += [pl.BlockSpec((tm, t.shape[1]), tab) for t in tabs]
    in_specs += [_const_spec(w.shape) for w in wts]
    widths = (RQ_W, RQ_W, RV_W, RV_W, CONV_WIDTH, CONV_WIDTH, MLA_W, MLA_W, MLA_W, N_BRANCH * d)
    out_shape = [jax.ShapeDtypeStruct((rows, w), BF16) for w in widths]
    out_specs = [pl.BlockSpec((tm, w), row) for w in widths]
    out_shape.append(jax.ShapeDtypeStruct((rows, d), F32))
    out_specs.append(pl.BlockSpec((tm, d), row))
    return pl.pallas_call(
        functools.partial(_inproj_body, d_model=d, fused=fused, ncb=ncb),
        out_shape=out_shape,
        grid=(nj, n_batch),
        in_specs=in_specs,
        out_specs=out_specs,
        compiler_params=_params(2),
        name="in_proj",
    )(*x_parts, a1, b1, *tabs, *wts)


def _ret_direction(q_ref, k_ref, v_ref, y_ref, s_ref, dm_ref, xi_ref, zt_ref, cd_ref, chunk_order):
    states = [s_ref[hd] for hd in range(RET_HEADS)]
    for c in chunk_order:
        rows = slice(c * RET_CHUNK, (c + 1) * RET_CHUNK)
        q = q_ref[rows, :]
        k = k_ref[rows, :]
        v = v_ref[rows, :]
        qx = (q.astype(F32) * xi_ref[...]).astype(BF16)
        kz = (k.astype(F32) * zt_ref[...]).astype(BF16)
        for hd in range(RET_HEADS):
            ks = slice(hd * RET_DK, (hd + 1) * RET_DK)
            vs = slice(hd * RET_DV, (hd + 1) * RET_DV)
            vh = v[:, vs]
            sc = lax.dot_general(q[:, ks], k[:, ks], (((1,), (1,)), ((), ())),
                                 preferred_element_type=F32)
            inner = _dot((sc * dm_ref[hd]).astype(BF16), vh)
            cross = _dot(qx[:, ks], states[hd].astype(BF16))
            y_ref[rows, vs] = (inner + cross).astype(BF16)
            upd = lax.dot_general(kz[:, ks], vh, (((0,), (0,)), ((), ())),
                                  preferred_element_type=F32)
            states[hd] = cd_ref[hd] * states[hd] + upd
    for hd in range(RET_HEADS):
        s_ref[hd] = states[hd]


def _ret_body(qf_ref, kf_ref, vf_ref, qb_ref, kb_ref, vb_ref,
              dmf_ref, dmb_ref, xif_ref, ztf_ref, xib_ref, ztb_ref, cdf_ref, cdb_ref,
              yf_ref, yb_ref, sf_ref, sb_ref):
    @pl.when(pl.program_id(1) == 0)
    def _():
        sf_ref[...] = jnp.zeros_like(sf_ref)
        sb_ref[...] = jnp.zeros_like(sb_ref)

    n_chunks = ROW_BLOCK // RET_CHUNK
    _ret_direction(qf_ref, kf_ref, vf_ref, yf_ref, sf_ref, dmf_ref, xif_ref, ztf_ref, cdf_ref,
                   range(n_chunks))
    _ret_direction(qb_ref, kb_ref, vb_ref, yb_ref, sb_ref, dmb_ref, xib_ref, ztb_ref, cdb_ref,
                   range(n_chunks - 1, -1, -1))


def _ret_call(rq, rk, rv, consts, *, n_batch, nj, ncb):
    rows = rq.shape[0]
    tm = ROW_BLOCK
    fwd = lambda b, s: (b * nj + s, 0)

    def bwd(b, s):
        return (b * nj + jnp.where(s < ncb, ncb - 1 - s, nj - 1 - (s - ncb)), 0)

    specs = []
    for im in (fwd, bwd):
        specs += [pl.BlockSpec((tm, RQ_W), im), pl.BlockSpec((tm, RQ_W), im),
                  pl.BlockSpec((tm, RV_W), im)]
    specs += [_const_spec(c.shape) for c in consts]
    return pl.pallas_call(
        _ret_body,
        out_shape=[jax.ShapeDtypeStruct((rows, RV_W), BF16)] * 2,
        grid=(n_batch, nj),
        in_specs=specs,
        out_specs=[pl.BlockSpec((tm, RV_W), fwd), pl.BlockSpec((tm, RV_W), bwd)],
        scratch_shapes=[pltpu.VMEM((RET_HEADS, RET_DK, RET_DV), F32)] * 2,
        compiler_params=_params(2),
        name="retention",
    )(rq, rk, rv, rq, rk, rv, *consts)


def _pick_tile(n, candidates):
    for c in candidates:
        if n % c == 0:
            return c
    raise ValueError(f"no tile for {n}")


def _attn_body(q_ref, k_ref, v_ref, o_ref, s_ref, *, n_ctx, n_all, ncb):
    tq = q_ref.shape[0]
    heads = [slice(hh * HEAD_PAD, (hh + 1) * HEAD_PAD) for hh in range(2)]

    def attend(n_keys, tk):
        nt = n_keys // tk
        qs = [q_ref[:, hs] for hs in heads]

        def qk(t, mrun):
            r0 = pl.multiple_of(t * tk, tk)
            out = []
            for hh, hs in enumerate(heads):
                s = lax.dot_general(qs[hh], k_ref[pl.ds(r0, tk), hs], (((1,), (1,)), ((), ())),
                                    preferred_element_type=F32)
                s_ref[hh, t, :, 0:tk] = s
                m = mrun[hh]
                for cc in range(tk // LANES):
                    m = jnp.maximum(m, s[:, cc * LANES:(cc + 1) * LANES])
                out.append(m)
            return tuple(out)

        mrun = lax.fori_loop(0, nt, qk, tuple(jnp.full((tq, LANES), -jnp.inf, F32) for _ in heads),
                             unroll=True)
        mrow = [jnp.max(m, axis=-1, keepdims=True) for m in mrun]

        def pv(t, accs):
            r0 = pl.multiple_of(t * tk, tk)
            out = []
            for hh, hs in enumerate(heads):
                p = jnp.exp2(s_ref[hh, t, :, 0:tk] - mrow[hh]).astype(BF16)
                out.append(accs[hh] + _dot(p, v_ref[pl.ds(r0, tk), hs]))
            return tuple(out)

        accs = lax.fori_loop(0, nt, pv, tuple(jnp.zeros((tq, HEAD_PAD), F32) for _ in heads),
                             unroll=True)
        o_ref[...] = jnp.concatenate([a[:, :MLA_V] / a[:, MLA_V:MLA_V + 1] for a in accs],
                                     axis=-1).astype(BF16)

    j = pl.program_id(2)

    @pl.when(j < ncb)
    def _():
        attend(n_ctx, _pick_tile(n_ctx, (256, 128)))

    @pl.when(j >= ncb)
    def _():
        attend(n_all, _pick_tile(n_all, ATTN_KEY_TILES))


def _attn_call(qm, km, vm, *, n_batch, nj, ncb, n_ctx):
    rows = qm.shape[0]
    tm = ROW_BLOCK
    t_all = nj * tm
    tk = _pick_tile(t_all, ATTN_KEY_TILES)
    qmap = lambda b, hp, j: (b * nj + j, hp)
    kmap = lambda b, hp, j: (b, hp)
    return pl.pallas_call(
        functools.partial(_attn_body, n_ctx=n_ctx, n_all=t_all, ncb=ncb),
        out_shape=jax.ShapeDtypeStruct((rows, MLA_O), BF16),
        grid=(n_batch, MLA_HEADS // 2, nj),
        in_specs=[pl.BlockSpec((tm, 2 * HEAD_PAD), qmap),
                  pl.BlockSpec((t_all, 2 * HEAD_PAD), kmap),
                  pl.BlockSpec((t_all, 2 * HEAD_PAD), kmap)],
        out_specs=pl.BlockSpec((tm, 2 * MLA_V), qmap),
        scratch_shapes=[pltpu.VMEM((2, t_all // tk, tm, tk), F32)],
        compiler_params=_params(3),
        name="mla_attention",
    )(qm, km, vm)


def _merge_body(yf_ref, yb_ref, rg_ref, cb_ref, cu_ref, cup_ref, cun_ref, om_ref, gt_ref, x_ref,
                g1_ref, a2_ref, b2_ref, gn_ref, cw_ref, wro_ref, wco_ref, wmo_ref, wout_ref,
                wrh_ref, wrl_ref, rb_ref, xmid_ref, h2_ref, rt_ref, gw_ref, lg_ref, *, nj, ncb, d_model,
                nb):
    tm = x_ref.shape[0]
    step = pl.program_id(0)

    @pl.when(step == 0)
    def _():
        lg_ref[...] = jnp.zeros_like(lg_ref)

    cls, w_lo, w_hi = _route_rows(jnp.transpose(lg_ref[...])[:N_EXPERTS, :], rb_ref[...])
    rank, counts = _block_ranks(cls.astype(F32))
    rt_ref[...] = jnp.concatenate(
        [cls.astype(F32), rank, counts, jnp.zeros((ROUTE_ROWS - 3, tm), F32)], axis=0)
    gw_ref[...] = jnp.transpose(
        jnp.concatenate([w_lo, w_hi, jnp.zeros((LANES - 2, tm), F32)], axis=0))

    j = jnp.minimum(step, nb - 1) % nj
    seg_first = jnp.logical_or(j == 0, j == ncb)
    seg_last = jnp.logical_or(j == ncb - 1, j == nj - 1)

    y = yf_ref[...].astype(F32) + yb_ref[...].astype(F32)
    yn = jnp.concatenate([_rms(y[:, hd * RET_DV:(hd + 1) * RET_DV]) for hd in range(RET_HEADS)],
                         axis=-1) * gn_ref[...]
    y_ret = _dot((rg_ref[...].astype(F32) * yn).astype(BF16), wro_ref[...])

    u = cu_ref[...].astype(F32)
    ridx = lax.broadcasted_iota(jnp.int32, u.shape, 0)
    prev_row = cup_ref[...].astype(F32)[BF16_SUBLANES - 1:, :] * jnp.where(seg_first, 0.0, 1.0)
    next_row = cun_ref[...].astype(F32)[0:1, :] * jnp.where(seg_last, 0.0, 1.0)
    u_prev = jnp.where(ridx == 0, prev_row, pltpu.roll(u, 1, 0))
    u_next = jnp.where(ridx == tm - 1, next_row, pltpu.roll(u, tm - 1, 0))
    conv = u_prev * cw_ref[0:1, :] + u * cw_ref[1:2, :] + u_next * cw_ref[2:3, :]
    y_conv = _dot((cb_ref[...].astype(F32) * conv).astype(BF16), wco_ref[...])

    y_mla = _dot(om_ref[...], wmo_ref[...])

    merged = (gt_ref[:, 0:d_model].astype(F32) * y_ret
              + gt_ref[:, d_model:2 * d_model].astype(F32) * y_conv
              + gt_ref[:, 2 * d_model:3 * d_model].astype(F32) * y_mla)
    x_mid = x_ref[...] + g1_ref[...] * _dot(merged.astype(BF16), wout_ref[...])
    xmid_ref[...] = x_mid

    h2 = _rms(x_mid) * a2_ref[...] + b2_ref[...]
    h_hi, h_lo = _split_bf16(h2)
    lg_ref[...] = _dot(h_hi, wrh_ref[...]) + _dot(h_hi, wrl_ref[...]) + _dot(h_lo, wrh_ref[...])
    h2_ref[...] = _pack_pairs(h_hi)


def _top2_of4(v):
    def first_max(rows):
        best, idx = rows[0], jnp.zeros(rows[0].shape, jnp.int32)
        for e in range(1, len(rows)):
            better = rows[e] > best
            idx = jnp.where(better, e, idx)
            best = jnp.where(better, rows[e], best)
        return best, idx

    b1, i1 = first_max(v)
    b2, i2 = first_max([jnp.where(i1 == e, -jnp.inf, v[e]) for e in range(len(v))])
    return i1, i2, b1, b2


def _route_rows(logits_t, bias):
    scores = jax.nn.sigmoid(logits_t)
    biased = scores + bias
    row = lambda a, e: a[e:e + 1, :]
    best = None
    for g in range(N_GROUPS):
        v = [row(biased, g * EXPERTS_PER_GROUP + e) for e in range(EXPERTS_PER_GROUP)]
        i1, i2, b1, b2 = _top2_of4(v)
        cand = (b1 + b2, jnp.full(i1.shape, g, jnp.int32), i1, i2)
        if best is None:
            best = cand
        else:
            better = cand[0] > best[0]
            best = tuple(jnp.where(better, c, o) for c, o in zip(cand, best))
    _, g_sel, i1, i2 = best
    lo = jnp.minimum(i1, i2)
    hi = jnp.maximum(i1, i2)
    e_lo = g_sel * EXPERTS_PER_GROUP + lo
    e_hi = g_sel * EXPERTS_PER_GROUP + hi
    s_lo = jnp.zeros_like(best[0])
    s_hi = jnp.zeros_like(best[0])
    for e in range(N_EXPERTS):
        s_lo = jnp.where(e_lo == e, row(scores, e), s_lo)
        s_hi = jnp.where(e_hi == e, row(scores, e), s_hi)
    total = s_lo + s_hi
    pair_base = jnp.where(lo == 0, 0, jnp.where(lo == 1, 3, 5))
    cls = g_sel * len(PAIRS) + pair_base + (hi - lo - 1)
    return cls, s_lo / total, s_hi / total


def _block_ranks(cls_row):
    n = cls_row.shape[1]
    cls_col = jnp.transpose(jnp.broadcast_to(cls_row, (LANES, n)))[:, 0:1]
    ii = lax.broadcasted_iota(jnp.int32, (n, n), 0)
    jj = lax.broadcasted_iota(jnp.int32, (n, n), 1)
    earlier_same = jnp.logical_and(cls_col == cls_row, ii < jj)
    rank = jnp.sum(jnp.where(earlier_same, 1.0, 0.0), axis=0, keepdims=True)
    lane = lax.broadcasted_iota(jnp.int32, (1, n), 1)
    counts = jnp.zeros((1, n), F32)
    for c in range(N_CLASSES):
        cnt = jnp.sum(jnp.where(cls_row == c, 1.0, 0.0), axis=1, keepdims=True)
        counts = jnp.where(lane == c, cnt, counts)
    return rank, counts


def _merge_call(yf, yb, rg, cb, cu, om, gt, xa, g1, a2, b2, wts, *, nj, ncb):
    rows, d = xa.shape
    tm = ROW_BLOCK
    nb = rows // tm
    halo = BF16_SUBLANES
    per_blk = tm // halo
    blk = lambda i: jnp.minimum(i, nb - 1)
    row = lambda i: (blk(i), 0)
    mod = lambda i: (2 * (blk(i) // nj) + ((blk(i) % nj) >= ncb).astype(jnp.int32), 0, 0)
    prev = lambda i: (jnp.maximum(blk(i) * per_blk - 1, 0), 0)
    nxt = lambda i: (jnp.minimum((blk(i) + 1) * per_blk, rows // halo - 1), 0)
    in_specs = [pl.BlockSpec((tm, RV_W), row), pl.BlockSpec((tm, RV_W), row),
                pl.BlockSpec((tm, RV_W), row),
                pl.BlockSpec((tm, CONV_WIDTH), row), pl.BlockSpec((tm, CONV_WIDTH), row),
                pl.BlockSpec((halo, CONV_WIDTH), prev), pl.BlockSpec((halo, CONV_WIDTH), nxt),
                pl.BlockSpec((tm, MLA_O), row), pl.BlockSpec((tm, N_BRANCH * d), row),
                pl.BlockSpec((tm, d), row),
                pl.BlockSpec((None, 1, d), mod), pl.BlockSpec((None, 1, d), mod),
                pl.BlockSpec((None, 1, d), mod)]
    in_specs += [_const_spec(w.shape) for w in wts]
    return pl.pallas_call(
        functools.partial(_merge_body, nj=nj, ncb=ncb, d_model=d, nb=nb),
        out_shape=[jax.ShapeDtypeStruct((rows, d), F32), jax.ShapeDtypeStruct((rows, d // 2), jnp.uint32),
                   jax.ShapeDtypeStruct((nb * ROUTE_ROWS, tm), F32),
                   jax.ShapeDtypeStruct((rows, LANES), F32)],
        grid=(nb + 1,),
        in_specs=in_specs,
        out_specs=[pl.BlockSpec((tm, d), row), pl.BlockSpec((tm, d // 2), row),
                   pl.BlockSpec((ROUTE_ROWS, tm), lambda i: (jnp.maximum(i - 1, 0), 0)),
                   pl.BlockSpec((tm, LANES), lambda i: (jnp.maximum(i - 1, 0), 0))],
        scratch_shapes=[pltpu.VMEM((tm, LANES), F32)],
        compiler_params=_params(1),
        name="merge_out_proj",
    )(yf, yb, rg, cb, cu, cu, cu, om, gt, xa, g1, a2, b2, *wts)


def _moe_body(ea_ref, eb_ref, nu_ref, h_ref, gw_ref, w1a_ref, w3a_ref, w2a_ref, w1b_ref, w3b_ref,
              w2b_ref, o_ref, w13a_s, w2a_s, w13b_s, w2b_s):
    t = pl.program_id(0)
    used = t < nu_ref[0]
    prev = jnp.maximum(t - 1, 0)

    def refresh(e_ref, w1_ref, w3_ref, w2_ref, w13_s, w2_s):
        @pl.when(jnp.logical_and(used, jnp.logical_or(t == 0, e_ref[t] != e_ref[prev])))
        def _():
            w13_s[:, :D_EXPERT] = w1_ref[...].astype(BF16)
            w13_s[:, D_EXPERT:] = w3_ref[...].astype(BF16)
            w2_s[...] = w2_ref[...].astype(BF16)

    refresh(ea_ref, w1a_ref, w3a_ref, w2a_ref, w13a_s, w2a_s)
    refresh(eb_ref, w1b_ref, w3b_ref, w2b_ref, w13b_s, w2b_s)

    @pl.when(used)
    def _():
        h = _unpack_pairs(h_ref[...]).astype(BF16)
        gw = gw_ref[...]

        def expert(w13_s, w2_s, wt):
            a = _dot(h, w13_s[...])
            act = _silu(a[:, :D_EXPERT]) * a[:, D_EXPERT:] * wt
            return _dot(act.astype(BF16), w2_s[...])

        o_ref[...] = _pack_pairs(expert(w13a_s, w2a_s, gw[:, 0:1])
                                 + expert(w13b_s, w2b_s, gw[:, 1:2]))

    @pl.when(jnp.logical_not(used))
    def _():
        o_ref[...] = jnp.zeros_like(o_ref)


def _moe_call(tile_ea, tile_eb, n_used, hs, gw, w1, w3, w2, layer):
    npad = hs.shape[0]
    d = w1.shape[2]
    tmo = MOE_TILE
    row = lambda t, ea, eb, nu: (t, 0)
    wa = lambda t, ea, eb, nu: (layer, ea[t], 0, 0)
    wb = lambda t, ea, eb, nu: (layer, eb[t], 0, 0)
    up = lambda im: pl.BlockSpec((None, None, d, D_EXPERT), im)
    down = lambda im: pl.BlockSpec((None, None, D_EXPERT, d), im)
    grid_spec = pltpu.PrefetchScalarGridSpec(
        num_scalar_prefetch=3,
        grid=(npad // tmo,),
        in_specs=[pl.BlockSpec((tmo, d // 2), row), pl.BlockSpec((tmo, LANES), row),
                  up(wa), up(wa), down(wa), up(wb), up(wb), down(wb)],
        out_specs=pl.BlockSpec((tmo, d // 2), row),
        scratch_shapes=[pltpu.VMEM((d, 2 * D_EXPERT), BF16), pltpu.VMEM((D_EXPERT, d), BF16)] * 2)
    return pl.pallas_call(
        _moe_body,
        out_shape=jax.ShapeDtypeStruct((npad, d // 2), jnp.uint32),
        grid_spec=grid_spec,
        compiler_params=_params(1),
        name="moe_experts",
    )(tile_ea, tile_eb, n_used, hs, gw, w1, w3, w2, w1, w3, w2)


def _dispatch(route, n_tok):
    tmo = MOE_TILE
    n_tiles = n_tok // tmo + N_CLASSES
    npad = n_tiles * tmo
    cls = route[:, 0, :].astype(jnp.int32)
    rank = route[:, 1, :].astype(jnp.int32)
    counts = route[:, 2, :N_CLASSES].astype(jnp.int32)
    tiles_per = (jnp.sum(counts, axis=0) + tmo - 1) // tmo
    tile_end = jnp.cumsum(tiles_per)
    offs = (tile_end - tiles_per) * tmo
    block_base = offs[None, :] + jnp.cumsum(counts, axis=0) - counts
    classes = jnp.arange(N_CLASSES, dtype=jnp.int32)
    base = jnp.sum(jnp.where(cls[:, :, None] == classes, block_base[:, None, :], 0), axis=-1)
    dest = (base + rank).reshape(-1)
    tile_ids = jnp.arange(n_tiles, dtype=jnp.int32)
    tile_cls = jnp.sum((tile_end[None, :] <= tile_ids[:, None]).astype(jnp.int32), axis=1)
    tile_cls = jnp.minimum(tile_cls, N_CLASSES - 1)
    pa = jnp.asarray([p[0] for p in PAIRS], jnp.int32)
    pb = jnp.asarray([p[1] for p in PAIRS], jnp.int32)
    group = (tile_cls // len(PAIRS)) * EXPERTS_PER_GROUP
    tile_ea = group + pa[tile_cls % len(PAIRS)]
    tile_eb = group + pb[tile_cls % len(PAIRS)]
    n_used = tile_end[-1:].astype(jnp.int32)
    return dest, npad, tile_ea, tile_eb, n_used


def _sc_mesh():
    return plsc.VectorSubcoreMesh(core_axis_name="c", subcore_axis_name="s",
                                  num_cores=SC_CORES, num_subcores=SC_SUBCORES)


def _sc_chunks(idx):
    assert idx.shape[0] % (SC_WORKERS * GATHER_CHUNK) == 0
    return idx.reshape(SC_WORKERS, -1, GATHER_CHUNK)


def _sc_pipeline(n_chunks, fetch, put):
    assert n_chunks % 2 == 0
    fetch(0, 0).start()

    @pl.loop(0, n_chunks, step=2)
    def _(c0):
        for b in (0, 1):
            c = c0 + b
            fetch(c, b).wait()

            @pl.when(c >= 1)
            def _():
                put(c - 1, 1 - b).wait()

            @pl.when(c + 1 < n_chunks)
            def _():
                fetch(c + 1, 1 - b).start()

            put(c, b).start()

    put(n_chunks - 1, 1).wait()


def _sc_scratch(n_chunks, d, dtype):
    return [pltpu.VMEM((n_chunks, GATHER_CHUNK), jnp.int32),
            pltpu.VMEM((2, GATHER_CHUNK, d), dtype),
            pltpu.SemaphoreType.DMA((2,)), pltpu.SemaphoreType.DMA((2,))]


def _sc_gather_rows(table, idx):
    d = table.shape[1]
    idx3 = _sc_chunks(idx)
    n_chunks = idx3.shape[1]
    assert table.dtype.itemsize == 4

    @functools.partial(
        pl.kernel, mesh=_sc_mesh(),
        out_type=jax.ShapeDtypeStruct((idx.shape[0], d), table.dtype),
        scratch_types=_sc_scratch(n_chunks, d, table.dtype),
        name="sc_row_gather")
    def gather(table_hbm, idx_hbm, out_hbm, idx_v, rows_v, sem_in, sem_out):
        worker = lax.axis_index("s") * SC_CORES + lax.axis_index("c")
        pltpu.sync_copy(idx_hbm.at[worker], idx_v)

        def fetch(c, b):
            return pltpu.make_async_copy(table_hbm.at[idx_v.at[c]], rows_v.at[b], sem_in.at[b])

        def put(c, b):
            off = (worker * n_chunks + c) * GATHER_CHUNK
            return pltpu.make_async_copy(rows_v.at[b], out_hbm.at[pl.ds(off, GATHER_CHUNK)],
                                         sem_out.at[b])

        _sc_pipeline(n_chunks, fetch, put)

    return gather(table, idx3)


def _sc_scatter_rows(rows, idx, n_out):
    d = rows.shape[1]
    idx3 = _sc_chunks(idx)
    n_chunks = idx3.shape[1]
    assert rows.dtype.itemsize == 4

    @functools.partial(
        pl.kernel, mesh=_sc_mesh(),
        out_type=jax.ShapeDtypeStruct((n_out, d), rows.dtype),
        scratch_types=_sc_scratch(n_chunks, d, rows.dtype),
        name="sc_row_scatter")
    def scatter(rows_hbm, idx_hbm, out_hbm, idx_v, rows_v, sem_in, sem_out):
        worker = lax.axis_index("s") * SC_CORES + lax.axis_index("c")
        pltpu.sync_copy(idx_hbm.at[worker], idx_v)

        def fetch(c, b):
            off = (worker * n_chunks + c) * GATHER_CHUNK
            return pltpu.make_async_copy(rows_hbm.at[pl.ds(off, GATHER_CHUNK)], rows_v.at[b],
                                         sem_in.at[b])

        def put(c, b):
            return pltpu.make_async_copy(rows_v.at[b], out_hbm.at[idx_v.at[c]], sem_out.at[b])

        _sc_pipeline(n_chunks, fetch, put)

    return scatter(rows, idx3)


def _final_body(x_ref, f_ref, g2_ref, fn_ref, o_ref):
    o_ref[...] = _rms(x_ref[...] + g2_ref[...] * _unpack_pairs(f_ref[...])) * fn_ref[...]


def _final_call(xmid, fg, g2, final_norm, *, n_batch, nj, ncb):
    rows, d = xmid.shape
    tm = ROW_BLOCK
    njl = nj - ncb
    src = lambda b, j: (b * nj + ncb + j, 0)
    return pl.pallas_call(
        _final_body,
        out_shape=jax.ShapeDtypeStruct((n_batch * njl * tm, d), F32),
        grid=(n_batch, njl),
        in_specs=[pl.BlockSpec((tm, d), src), pl.BlockSpec((tm, d // 2), src),
                  pl.BlockSpec((None, 1, d), lambda b, j: (2 * b + 1, 0, 0)),
                  _const_spec((1, d))],
        out_specs=pl.BlockSpec((tm, d), lambda b, j: (b * njl + j, 0)),
        compiler_params=_params(2),
        name="final_norm",
    )(xmid, fg, g2, final_norm.reshape(1, d))


def _split_cols(w, sizes):
    out, off = [], 0
    for s in sizes:
        out.append(w[:, off:off + s])
        off += s
    out.append(w[:, off:])
    return out


def _layer_weights(w_in, w_uq, w_ukv, q_norm, kv_norm):
    wq, wk, wv, wg, wcb, wcc, wcx, wqd, wkvd, wkr, wgate = _split_cols(w_in, IN_SIZES[:-1])
    rope_lanes = lambda w: jnp.pad(w, ((0, 0), (MLA_NOPE, HEAD_PAD - MLA_NOPE - MLA_ROPE)))
    w_ext = jnp.concatenate(
        [wq, wk, wv, wg, wcb, wcc, wcx, wqd, wkvd, rope_lanes(wkr), wgate], axis=1).astype(BF16)

    uq = w_uq.reshape(MLA_Q_RANK, MLA_HEADS, MLA_NOPE + MLA_ROPE)
    tail = HEAD_PAD - MLA_NOPE - MLA_ROPE
    uq_pad = jnp.pad(uq, ((0, 0), (0, 0), (0, tail)))
    wuq = uq_pad.reshape(MLA_Q_RANK, MLA_W).astype(BF16)

    ukv = w_ukv.reshape(MLA_KV_RANK, MLA_HEADS, MLA_NOPE + MLA_V)
    wk_up = jnp.pad(ukv[..., :MLA_NOPE], ((0, 0), (0, 0), (0, HEAD_PAD - MLA_NOPE)))
    wk_up = wk_up.reshape(MLA_KV_RANK, MLA_W).astype(BF16)
    wv_pad = jnp.pad(ukv[..., MLA_NOPE:], ((0, 0), (0, 0), (0, HEAD_PAD - MLA_V)))
    wv_pad = wv_pad.reshape(MLA_KV_RANK, MLA_W).astype(BF16)
    ones_row = jnp.zeros((MLA_HEADS, HEAD_PAD), F32).at[:, MLA_V].set(1.0).reshape(1, MLA_W)
    q_gain = q_norm.astype(F32) * ((MLA_NOPE + MLA_ROPE) ** -0.5 * math.log2(math.e))
    return (w_ext, wuq, wk_up, wv_pad, q_gain.reshape(1, -1),
            kv_norm.reshape(1, -1).astype(F32), ones_row)


def _rotary_tables(n_ctx, n_lat):
    n_all = n_ctx + n_lat
    row = lax.broadcasted_iota(jnp.int32, (n_all, LANES), 0)
    lane = lax.broadcasted_iota(jnp.int32, (n_all, LANES), 1)
    is_ctx = row < n_ctx
    pos = row - n_ctx
    grid_row = pos // GRID_W
    grid_col = pos - grid_row * GRID_W

    def inv_freq(idx, half):
        return ROPE_BASE ** (-idx.astype(F32) / half)

    def table(ang, first_half, live):
        cos = jnp.where(is_ctx, 1.0, jnp.cos(ang))
        sin = jnp.where(is_ctx, 0.0, jnp.where(first_half, -jnp.sin(ang), jnp.sin(ang)))
        return jnp.where(live, cos, 0.0), jnp.where(live, sin, 0.0)

    half = RET_DK // 2
    ang = pos.astype(F32) * inv_freq(lane % half, half)
    cr, sr = table(ang, (lane % RET_DK) < half, True)

    quarter = MLA_ROPE // 4
    rl = lane - MLA_NOPE
    coord = jnp.where(rl < MLA_ROPE // 2, grid_row, grid_col)
    ang = coord.astype(F32) * inv_freq(rl % quarter, quarter)
    cm, sm = table(ang, (rl % (2 * quarter)) < quarter,
                   jnp.logical_and(rl >= 0, rl < MLA_ROPE))
    cm = jnp.where(rl < 0, 1.0, cm)
    return cr, sr, cm, sm


def _retention_consts(ret_decay):
    log_gf = jax.nn.log_sigmoid(ret_decay[0].astype(F32))
    log_gb = jax.nn.log_sigmoid(ret_decay[1].astype(F32))
    idx = jnp.arange(RET_CHUNK, dtype=F32)
    rel = idx[:, None] - idx[None, :]
    dm_f = jnp.where(rel >= 0, jnp.exp(log_gf[:, None, None] * jnp.maximum(rel, 0.0)[None]), 0.0)
    dm_b = jnp.where(rel < 0, jnp.exp(log_gb[:, None, None] * jnp.maximum(-rel, 0.0)[None]), 0.0)

    def lanes(t):
        return jnp.repeat(t, RET_DK, axis=1)

    xi_f = lanes(jnp.exp(log_gf[None, :] * (idx + 1.0)[:, None]))
    zt_f = lanes(jnp.exp(log_gf[None, :] * (RET_CHUNK - 1 - idx)[:, None]))
    xi_b = lanes(jnp.exp(log_gb[None, :] * (RET_CHUNK - idx)[:, None]))
    zt_b = lanes(jnp.exp(log_gb[None, :] * idx[:, None]))
    cd = lambda lg: jnp.broadcast_to(jnp.exp(lg * RET_CHUNK)[:, None, None], (RET_HEADS, 1, RET_DV))
    return dm_f, dm_b, xi_f, zt_f, xi_b, zt_b, cd(log_gf), cd(log_gb)


def kernel(x, c, ctx, c_ctx, w_ada, b_ada, norm1, norm2, w_in, ret_decay, ret_gn, w_ret_o, conv_w,
           w_conv_o, mla_q_norm, w_uq, mla_kv_norm, w_ukv, w_mla_o, w_out, w_router, router_bias,
           w1, w3, w2, final_norm):
    n_batch, n_lat, d = x.shape
    n_ctx = ctx.shape[1]
    depth = w_ada.shape[0]
    t_all = n_ctx + n_lat
    assert n_ctx % ROW_BLOCK == 0 and n_lat % ROW_BLOCK == 0 and n_lat % GRID_W == 0
    nj = t_all // ROW_BLOCK
    ncb = n_ctx // ROW_BLOCK
    n_tok = n_batch * t_all
    assert n_tok % MOE_TILE == 0
    geom = dict(n_batch=n_batch, nj=nj, ncb=ncb)

    cc = jnp.concatenate([c, c_ctx[None, :]], axis=0)
    cc = jnp.pad(cc, ((0, -cc.shape[0] % 8), (0, 0)))
    mod = _ada_call(cc, w_ada, b_ada)[:, :n_batch + 1].reshape(depth, n_batch + 1, N_MOD, d)
    pick = jnp.stack([jnp.full((n_batch,), n_batch, jnp.int32),
                      jnp.arange(n_batch, dtype=jnp.int32)], axis=1).reshape(-1)
    mod = mod[:, pick]

    tabs = _rotary_tables(n_ctx, n_lat)
    wr_hi, wr_lo = _split_bf16(jnp.pad(w_router.astype(F32), ((0, 0), (0, LANES - N_EXPERTS))))
    rbias = router_bias.astype(F32).reshape(N_EXPERTS, 1)

    out = None
    x_parts = (ctx.reshape(n_batch * n_ctx, d), x.reshape(n_batch * n_lat, d))
    for l in range(depth):
        m = mod[l]
        rowvec = lambda v: v.reshape(2 * n_batch, 1, d)
        a1 = rowvec(norm1[l][None, :] * (1.0 + m[:, 1]))
        b1 = rowvec(m[:, 0])
        g1 = rowvec(m[:, 2])
        a2 = rowvec(norm2[l][None, :] * (1.0 + m[:, 4]))
        b2 = rowvec(m[:, 3])
        g2 = rowvec(m[:, 5])

        wts = _layer_weights(w_in[l], w_uq[l], w_ukv[l], mla_q_norm[l], mla_kv_norm[l])
        proj = _inproj_call(x_parts, a1, b1, tabs, wts, **geom)
        rq, rk, rv, rg, cb, cu, qm, km, vm, gt, xa = proj

        yf, yb = _ret_call(rq, rk, rv, _retention_consts(ret_decay[l]), **geom)
        om = _attn_call(qm, km, vm, n_ctx=n_ctx, **geom)

        merge_wts = (ret_gn[l].reshape(1, -1).astype(F32), conv_w[l].T.astype(F32),
                     w_ret_o[l].astype(BF16), w_conv_o[l].astype(BF16), w_mla_o[l].astype(BF16),
                     w_out[l].astype(BF16), wr_hi, wr_lo, rbias)
        xmid, h2, route, gate_w = _merge_call(yf, yb, rg, cb, cu, om, gt, xa, g1, a2, b2, merge_wts,
                                      nj=nj, ncb=ncb)
        route = route.reshape(-1, ROUTE_ROWS, ROW_BLOCK)
        dest, npad, tile_ea, tile_eb, n_used = _dispatch(route, n_tok)
        hs = _sc_scatter_rows(h2, dest, npad)
        gw = _sc_scatter_rows(gate_w, dest, npad)
        f_sorted = _moe_call(tile_ea, tile_eb, n_used, hs, gw, w1, w3, w2, l)
        fg = _sc_gather_rows(f_sorted, dest)

        if l < depth - 1:
            x_parts = (xmid, fg, g2)
        else:
            out = _final_call(xmid, fg, g2, final_norm, **geom)
    return out.reshape(n_batch, n_lat, d)
```

```python
import functools
import math

import jax
import jax.numpy as jnp
from jax import lax
from jax.experimental import pallas as pl
from jax.experimental.pallas import tpu as pltpu
from jax.experimental.pallas import tpu_sc as plsc

F32 = jnp.float32
BF16 = jnp.bfloat16

GRID_W = 64
RMS_EPS = 1e-6
ROPE_BASE = 10000.0
N_MOD = 6
RET_HEADS = 4
RET_DK = 64
RET_DV = 128
RET_CHUNK = 256
CONV_WIDTH = 512
MLA_HEADS = 8
MLA_Q_RANK = 384
MLA_KV_RANK = 256
MLA_NOPE = 64
MLA_ROPE = 32
MLA_V = 64
N_BRANCH = 3
N_EXPERTS = 16
N_GROUPS = 4
EXPERTS_PER_GROUP = N_EXPERTS // N_GROUPS
D_EXPERT = 512
IN_SIZES = (RET_HEADS * RET_DK, RET_HEADS * RET_DK, RET_HEADS * RET_DV, RET_HEADS * RET_DV,
            CONV_WIDTH, CONV_WIDTH, CONV_WIDTH, MLA_Q_RANK, MLA_KV_RANK, MLA_ROPE)

LANES = 128
BF16_SUBLANES = 16
VMEM_LIMIT = 56 * 1024 * 1024
SC_CORES = 2
SC_SUBCORES = 16
SC_WORKERS = SC_CORES * SC_SUBCORES

HEAD_PAD = LANES
ROW_BLOCK = 256
MOE_TILE = 512
ATTN_KEY_TILES = (768, 512, 256, 128)
ROUTE_ROWS = 8
RET_BATCH_GROUP = 4
GATHER_CHUNK = 96
PAIRS = [(a, b) for a in range(EXPERTS_PER_GROUP) for b in range(a + 1, EXPERTS_PER_GROUP)]
N_CLASSES = N_GROUPS * len(PAIRS)

RQ_W = RET_HEADS * RET_DK
RV_W = RET_HEADS * RET_DV
MLA_W = MLA_HEADS * HEAD_PAD
MLA_O = MLA_HEADS * MLA_V


def _const_spec(shape):
    nd = len(shape)
    return pl.BlockSpec(shape, lambda *_: (0,) * nd, pipeline_mode=pl.Buffered(1))


def _params(n_axes):
    return pltpu.CompilerParams(dimension_semantics=("arbitrary",) * n_axes,
                                vmem_limit_bytes=VMEM_LIMIT)


def _dot(a, b):
    return jnp.dot(a, b, preferred_element_type=F32)


def _split_bf16(a):
    hi = a.astype(BF16)
    lo = (a - hi.astype(F32)).astype(BF16)
    return hi, lo


def _silu(v):
    return v * jax.nn.sigmoid(v)


def _rms(v):
    return v * lax.rsqrt(jnp.mean(v * v, axis=-1, keepdims=True) + RMS_EPS)


def _pack_pairs(v):
    half = v.shape[1] // 2
    bits = lambda t: lax.bitcast_convert_type(t.astype(BF16).astype(F32), jnp.uint32)
    return (bits(v[:, :half]) & jnp.uint32(0xFFFF0000)) | (bits(v[:, half:]) >> 16)


def _unpack_pairs(u):
    hi = lax.bitcast_convert_type(u & jnp.uint32(0xFFFF0000), F32)
    lo = lax.bitcast_convert_type(u << 16, F32)
    return jnp.concatenate([hi, lo], axis=1)


def _ada_body(c_ref, w_ref, b_ref, o_ref):
    a_hi, a_lo = _split_bf16(_silu(c_ref[...]))
    w_hi, w_lo = _split_bf16(w_ref[...])
    o_ref[...] = _dot(a_hi, w_hi) + _dot(a_hi, w_lo) + _dot(a_lo, w_hi) + b_ref[...]


def _ada_call(cc, w_ada, b_ada):
    depth, d, nm = w_ada.shape
    rows = cc.shape[0]
    cb = nm // 4
    return pl.pallas_call(
        _ada_body,
        out_shape=jax.ShapeDtypeStruct((depth, rows, nm), F32),
        grid=(depth, nm // cb),
        in_specs=[pl.BlockSpec((rows, d), lambda l, n: (0, 0)),
                  pl.BlockSpec((None, d, cb), lambda l, n: (l, 0, n)),
                  pl.BlockSpec((None, 1, cb), lambda l, n: (l, 0, n))],
        out_specs=pl.BlockSpec((None, rows, cb), lambda l, n: (l, 0, n)),
        compiler_params=_params(2),
        name="ada_mod",
    )(cc, w_ada, b_ada.reshape(depth, 1, nm))


_O_RQ = 0
_O_RK = _O_RQ + RQ_W
_O_RV = _O_RK + RQ_W
_O_RG = _O_RV + RV_W
_O_CB = _O_RG + RV_W
_O_CC = _O_CB + CONV_WIDTH
_O_CX = _O_CC + CONV_WIDTH
_O_QD = _O_CX + CONV_WIDTH
_O_KVD = _O_QD + MLA_Q_RANK
_O_KR = _O_KVD + MLA_KV_RANK
_O_GT = _O_KR + LANES


def _rot_half(v, half):
    width = v.shape[1]
    lane = lax.broadcasted_iota(jnp.int32, v.shape, 1)
    first = (lane % (2 * half)) < half
    return jnp.where(first, pltpu.roll(v, width - half, 1), pltpu.roll(v, half, 1))


def _inproj_body(*refs, d_model, fused, ncb):
    xo_ref = refs[-1]
    if fused:
        xm_ref, f_ref, g2_ref = refs[:3]
        refs = refs[3:]
        x = xm_ref[...] + g2_ref[...] * _unpack_pairs(f_ref[...])
    else:
        ctx_ref, lat_ref = refs[:2]
        refs = refs[2:]
        x = jnp.where(pl.program_id(0) < ncb, ctx_ref[...], lat_ref[...])
    xo_ref[...] = x
    (a1_ref, b1_ref, cr_ref, sr_ref, cm_ref, sm_ref, w_ref, wuq_ref, wk_ref, wv_ref, qn_ref, kvn_ref,
     ones_ref, rq_ref, rk_ref, rv_ref, rg_ref, cb_ref, cu_ref, qm_ref, km_ref, vm_ref, gt_ref) = refs[:23]
    h = (_rms(x) * a1_ref[...] + b1_ref[...]).astype(BF16)

    def mm(off, width):
        return _dot(h, w_ref[:, off:off + width])

    cr = jnp.concatenate([cr_ref[...]] * (RQ_W // cr_ref.shape[1]), axis=1)
    sr = jnp.concatenate([sr_ref[...]] * (RQ_W // sr_ref.shape[1]), axis=1)
    q = mm(_O_RQ, RQ_W)
    rq_ref[...] = (q * cr + _rot_half(q, RET_DK // 2) * sr).astype(BF16)
    k = mm(_O_RK, RQ_W)
    rk_ref[...] = ((k * cr + _rot_half(k, RET_DK // 2) * sr) * (RET_DK ** -0.5)).astype(BF16)
    rv_ref[...] = mm(_O_RV, RV_W).astype(BF16)
    rg_ref[...] = _silu(mm(_O_RG, RV_W)).astype(BF16)
    cb_ref[...] = mm(_O_CB, CONV_WIDTH).astype(BF16)
    cu_ref[...] = (mm(_O_CC, CONV_WIDTH) * mm(_O_CX, CONV_WIDTH)).astype(BF16)

    cm = cm_ref[...]
    sm = sm_ref[...]
    qn = (_rms(mm(_O_QD, MLA_Q_RANK)) * qn_ref[...]).astype(BF16)
    for hd in range(MLA_HEADS):
        lo = hd * HEAD_PAD
        qa = _dot(qn, wuq_ref[:, lo:lo + HEAD_PAD])
        qm_ref[:, lo:lo + HEAD_PAD] = (qa * cm + _rot_half(qa, MLA_ROPE // 4) * sm).astype(BF16)

    kvn = (_rms(mm(_O_KVD, MLA_KV_RANK)) * kvn_ref[...]).astype(BF16)
    kr = mm(_O_KR, HEAD_PAD)
    kr = kr * cm + _rot_half(kr, MLA_ROPE // 4) * sm
    kn = _dot(kvn, wk_ref[...])
    for hd in range(MLA_HEADS):
        lo = hd * HEAD_PAD
        km_ref[:, lo:lo + HEAD_PAD] = (kn[:, lo:lo + HEAD_PAD] + kr).astype(BF16)
    vm_ref[...] = (_dot(kvn, wv_ref[...]) + ones_ref[...]).astype(BF16)

    for br in range(N_BRANCH):
        gt_ref[:, br * d_model:(br + 1) * d_model] = jax.nn.sigmoid(
            mm(_O_GT + br * d_model, d_model)).astype(BF16)


def _inproj_call(x_parts, a1, b1, tabs, wts, *, n_batch, nj, ncb):
    fused = len(x_parts) == 3
    d = x_parts[0].shape[1]
    rows = n_batch * nj * ROW_BLOCK
    tm = ROW_BLOCK
    row = lambda j, b: (b * nj + j, 0)
    mod = lambda j, b: (2 * b + (j >= ncb).astype(jnp.int32), 0, 0)
    tab = lambda j, b: (j, 0)
    if fused:
        in_specs = [pl.BlockSpec((tm, d), row), pl.BlockSpec((tm, d // 2), row),
                    pl.BlockSpec((None, 1, d), mod)]
    else:
        in_specs = [pl.BlockSpec((tm, d), lambda j, b: (jnp.where(j < ncb, b * ncb + j, 0), 0)),
                    pl.BlockSpec((tm, d),
                                 lambda j, b: (jnp.where(j < ncb, 0, b * (nj - ncb) + j - ncb), 0))]
    in_specs += [pl.BlockSpec((None, 1, d), mod), pl.BlockSpec((None, 1, d), mod)]
    in_specs += [pl.BlockSpec((tm, t.shape[1]), tab) for t in tabs]
    in_specs += [_const_spec(w.shape) for w in wts]
    widths = (RQ_W, RQ_W, RV_W, RV_W, CONV_WIDTH, CONV_WIDTH, MLA_W, MLA_W, MLA_W, N_BRANCH * d)
    out_shape = [jax.ShapeDtypeStruct((rows, w), BF16) for w in widths]
    out_specs = [pl.BlockSpec((tm, w), row) for w in widths]
    out_shape.append(jax.ShapeDtypeStruct((rows, d), F32))
    out_specs.append(pl.BlockSpec((tm, d), row))
    return pl.pallas_call(
        functools.partial(_inproj_body, d_model=d, fused=fused, ncb=ncb),
        out_shape=out_shape,
        grid=(nj, n_batch),
        in_specs=in_specs,
        out_specs=out_specs,
        compiler_params=_params(2),
        name="in_proj",
    )(*x_parts, a1, b1, *tabs, *wts)


def _ret_direction(q_ref, k_ref, v_ref, y_ref, s_ref, dm_ref, xi_ref, zt_ref, cd_ref, chunk_order):
    states = [s_ref[hd] for hd in range(RET_HEADS)]
    for c in chunk_order:
        rows = slice(c * RET_CHUNK, (c + 1) * RET_CHUNK)
        q = q_ref[rows, :]
        k = k_ref[rows, :]
        v = v_ref[rows, :]
        qx = (q.astype(F32) * xi_ref[...]).astype(BF16)
        kz = (k.astype(F32) * zt_ref[...]).astype(BF16)
        for hd in range(RET_HEADS):
            ks = slice(hd * RET_DK, (hd + 1) * RET_DK)
            vs = slice(hd * RET_DV, (hd + 1) * RET_DV)
            vh = v[:, vs]
            sc = lax.dot_general(q[:, ks], k[:, ks], (((1,), (1,)), ((), ())),
                                 preferred_element_type=F32)
            inner = _dot((sc * dm_ref[hd]).astype(BF16), vh)
            cross = _dot(qx[:, ks], states[hd].astype(BF16))
            y_ref[rows, vs] = (inner + cross).astype(BF16)
            upd = lax.dot_general(kz[:, ks], vh, (((0,), (0,)), ((), ())),
                                  preferred_element_type=F32)
            states[hd] = cd_ref[hd] * states[hd] + upd
    for hd in range(RET_HEADS):
        s_ref[hd] = states[hd]


def _ret_body(qf_ref, kf_ref, vf_ref, qb_ref, kb_ref, vb_ref,
              dmf_ref, dmb_ref, xif_ref, ztf_ref, xib_ref, ztb_ref, cdf_ref, cdb_ref,
              yf_ref, yb_ref, sf_ref, sb_ref):
    @pl.when(pl.program_id(1) == 0)
    def _():
        sf_ref[...] = jnp.zeros_like(sf_ref)
        sb_ref[...] = jnp.zeros_like(sb_ref)

    n_chunks = ROW_BLOCK // RET_CHUNK
    for g in range(qf_ref.shape[0]):
        _ret_direction(qf_ref.at[g], kf_ref.at[g], vf_ref.at[g], yf_ref.at[g], sf_ref.at[g],
                       dmf_ref, xif_ref, ztf_ref, cdf_ref, range(n_chunks))
        _ret_direction(qb_ref.at[g], kb_ref.at[g], vb_ref.at[g], yb_ref.at[g], sb_ref.at[g],
                       dmb_ref, xib_ref, ztb_ref, cdb_ref, range(n_chunks - 1, -1, -1))


def _ret_call(rq, rk, rv, consts, *, n_batch, nj, ncb):
    rows = rq.shape[0]
    tm = ROW_BLOCK
    grp = math.gcd(n_batch, RET_BATCH_GROUP)
    fwd = lambda b, s: (b, s, 0)

    def bwd(b, s):
        return (b, jnp.where(s < ncb, ncb - 1 - s, nj - 1 - (s - ncb)), 0)

    per_batch = lambda a: a.reshape(n_batch, nj * tm, a.shape[1])
    specs = []
    for im in (fwd, bwd):
        specs += [pl.BlockSpec((grp, tm, RQ_W), im), pl.BlockSpec((grp, tm, RQ_W), im),
                  pl.BlockSpec((grp, tm, RV_W), im)]
    specs += [_const_spec(c.shape) for c in consts]
    q3, k3, v3 = per_batch(rq), per_batch(rk), per_batch(rv)
    yf, yb = pl.pallas_call(
        _ret_body,
        out_shape=[jax.ShapeDtypeStruct((n_batch, nj * tm, RV_W), BF16)] * 2,
        grid=(n_batch // grp, nj),
        in_specs=specs,
        out_specs=[pl.BlockSpec((grp, tm, RV_W), fwd), pl.BlockSpec((grp, tm, RV_W), bwd)],
        scratch_shapes=[pltpu.VMEM((grp, RET_HEADS, RET_DK, RET_DV), F32)] * 2,
        compiler_params=_params(2),
        name="retention",
    )(q3, k3, v3, q3, k3, v3, *consts)
    return yf.reshape(rows, RV_W), yb.reshape(rows, RV_W)


def _pick_tile(n, candidates):
    for c in candidates:
        if n % c == 0:
            return c
    raise ValueError(f"no tile for {n}")


def _attn_body(q_ref, k_ref, v_ref, o_ref, s_ref, *, n_ctx, n_all, ncb):
    tq = q_ref.shape[0]
    heads = [slice(hh * HEAD_PAD, (hh + 1) * HEAD_PAD) for hh in range(2)]

    def attend(n_keys, tk):
        nt = n_keys // tk
        qs = [q_ref[:, hs] for hs in heads]

        def qk(t, mrun):
            r0 = pl.multiple_of(t * tk, tk)
            out = []
            for hh, hs in enumerate(heads):
                s = lax.dot_general(qs[hh], k_ref[pl.ds(r0, tk), hs], (((1,), (1,)), ((), ())),
                                    preferred_element_type=F32)
                s_ref[hh, t, :, 0:tk] = s
                m = mrun[hh]
                for cc in range(tk // LANES):
                    m = jnp.maximum(m, s[:, cc * LANES:(cc + 1) * LANES])
                out.append(m)
            return tuple(out)

        mrun = lax.fori_loop(0, nt, qk, tuple(jnp.full((tq, LANES), -jnp.inf, F32) for _ in heads),
                             unroll=True)
        mrow = [jnp.max(m, axis=-1, keepdims=True) for m in mrun]

        def pv(t, accs):
            r0 = pl.multiple_of(t * tk, tk)
            out = []
            for hh, hs in enumerate(heads):
                p = jnp.exp2(s_ref[hh, t, :, 0:tk] - mrow[hh]).astype(BF16)
                out.append(accs[hh] + _dot(p, v_ref[pl.ds(r0, tk), hs]))
            return tuple(out)

        accs = lax.fori_loop(0, nt, pv, tuple(jnp.zeros((tq, HEAD_PAD), F32) for _ in heads),
                             unroll=True)
        o_ref[...] = jnp.concatenate([a[:, :MLA_V] / a[:, MLA_V:MLA_V + 1] for a in accs],
                                     axis=-1).astype(BF16)

    j = pl.program_id(2)

    @pl.when(j < ncb)
    def _():
        attend(n_ctx, _pick_tile(n_ctx, (256, 128)))

    @pl.when(j >= ncb)
    def _():
        attend(n_all, _pick_tile(n_all, ATTN_KEY_TILES))


def _attn_call(qm, km, vm, *, n_batch, nj, ncb, n_ctx):
    rows = qm.shape[0]
    tm = ROW_BLOCK
    t_all = nj * tm
    tk = _pick_tile(t_all, ATTN_KEY_TILES)
    qmap = lambda b, hp, j: (b * nj + j, hp)
    kmap = lambda b, hp, j: (b, hp)
    return pl.pallas_call(
        functools.partial(_attn_body, n_ctx=n_ctx, n_all=t_all, ncb=ncb),
        out_shape=jax.ShapeDtypeStruct((rows, MLA_O), BF16),
        grid=(n_batch, MLA_HEADS // 2, nj),
        in_specs=[pl.BlockSpec((tm, 2 * HEAD_PAD), qmap),
                  pl.BlockSpec((t_all, 2 * HEAD_PAD), kmap),
                  pl.BlockSpec((t_all, 2 * HEAD_PAD), kmap)],
        out_specs=pl.BlockSpec((tm, 2 * MLA_V), qmap),
        scratch_shapes=[pltpu.VMEM((2, t_all // tk, tm, tk), F32)],
        compiler_params=_params(3),
        name="mla_attention",
    )(qm, km, vm)


def _merge_body(yf_ref, yb_ref, rg_ref, cb_ref, cu_ref, cup_ref, cun_ref, om_ref, gt_ref, x_ref,
                g1_ref, a2_ref, b2_ref, gn_ref, cw_ref, wro_ref, wco_ref, wmo_ref, wout_ref,
                wrh_ref, wrl_ref, rb_ref, xmid_ref, h2_ref, rt_ref, gw_ref, lg_ref, *, nj, ncb, d_model,
                nb):
    tm = x_ref.shape[0]
    step = pl.program_id(0)

    @pl.when(step == 0)
    def _():
        lg_ref[...] = jnp.zeros_like(lg_ref)

    cls, w_lo, w_hi = _route_rows(jnp.transpose(lg_ref[...])[:N_EXPERTS, :], rb_ref[...])
    rank, counts = _block_ranks(cls.astype(F32))
    rt_ref[...] = jnp.concatenate(
        [cls.astype(F32), rank, counts, jnp.zeros((ROUTE_ROWS - 3, tm), F32)], axis=0)
    gw_ref[...] = jnp.transpose(
        jnp.concatenate([w_lo, w_hi, jnp.zeros((LANES - 2, tm), F32)], axis=0))

    j = jnp.minimum(step, nb - 1) % nj
    seg_first = jnp.logical_or(j == 0, j == ncb)
    seg_last = jnp.logical_or(j == ncb - 1, j == nj - 1)

    y = yf_ref[...].astype(F32) + yb_ref[...].astype(F32)
    yn = jnp.concatenate([_rms(y[:, hd * RET_DV:(hd + 1) * RET_DV]) for hd in range(RET_HEADS)],
                         axis=-1) * gn_ref[...]
    y_ret = _dot((rg_ref[...].astype(F32) * yn).astype(BF16), wro_ref[...])

    u = cu_ref[...].astype(F32)
    ridx = lax.broadcasted_iota(jnp.int32, u.shape, 0)
    prev_row = cup_ref[...].astype(F32)[BF16_SUBLANES - 1:, :] * jnp.where(seg_first, 0.0, 1.0)
    next_row = cun_ref[...].astype(F32)[0:1, :] * jnp.where(seg_last, 0.0, 1.0)
    u_prev = jnp.where(ridx == 0, prev_row, pltpu.roll(u, 1, 0))
    u_next = jnp.where(ridx == tm - 1, next_row, pltpu.roll(u, tm - 1, 0))
    conv = u_prev * cw_ref[0:1, :] + u * cw_ref[1:2, :] + u_next * cw_ref[2:3, :]
    y_conv = _dot((cb_ref[...].astype(F32) * conv).astype(BF16), wco_ref[...])

    y_mla = _dot(om_ref[...], wmo_ref[...])

    merged = (gt_ref[:, 0:d_model].astype(F32) * y_ret
              + gt_ref[:, d_model:2 * d_model].astype(F32) * y_conv
              + gt_ref[:, 2 * d_model:3 * d_model].astype(F32) * y_mla)
    x_mid = x_ref[...] + g1_ref[...] * _dot(merged.astype(BF16), wout_ref[...])
    xmid_ref[...] = x_mid

    h2 = _rms(x_mid) * a2_ref[...] + b2_ref[...]
    h_hi, h_lo = _split_bf16(h2)
    lg_ref[...] = _dot(h_hi, wrh_ref[...]) + _dot(h_hi, wrl_ref[...]) + _dot(h_lo, wrh_ref[...])
    h2_ref[...] = _pack_pairs(h_hi)


def _top2_of4(v):
    def first_max(rows):
        best, idx = rows[0], jnp.zeros(rows[0].shape, jnp.int32)
        for e in range(1, len(rows)):
            better = rows[e] > best
            idx = jnp.where(better, e, idx)
            best = jnp.where(better, rows[e], best)
        return best, idx

    b1, i1 = first_max(v)
    b2, i2 = first_max([jnp.where(i1 == e, -jnp.inf, v[e]) for e in range(len(v))])
    return i1, i2, b1, b2


def _route_rows(logits_t, bias):
    scores = jax.nn.sigmoid(logits_t)
    biased = scores + bias
    row = lambda a, e: a[e:e + 1, :]
    best = None
    for g in range(N_GROUPS):
        v = [row(biased, g * EXPERTS_PER_GROUP + e) for e in range(EXPERTS_PER_GROUP)]
        i1, i2, b1, b2 = _top2_of4(v)
        cand = (b1 + b2, jnp.full(i1.shape, g, jnp.int32), i1, i2)
        if best is None:
            best = cand
        else:
            better = cand[0] > best[0]
            best = tuple(jnp.where(better, c, o) for c, o in zip(cand, best))
    _, g_sel, i1, i2 = best
    lo = jnp.minimum(i1, i2)
    hi = jnp.maximum(i1, i2)
    e_lo = g_sel * EXPERTS_PER_GROUP + lo
    e_hi = g_sel * EXPERTS_PER_GROUP + hi
    s_lo = jnp.zeros_like(best[0])
    s_hi = jnp.zeros_like(best[0])
    for e in range(N_EXPERTS):
        s_lo = jnp.where(e_lo == e, row(scores, e), s_lo)
        s_hi = jnp.where(e_hi == e, row(scores, e), s_hi)
    total = s_lo + s_hi
    pair_base = jnp.where(lo == 0, 0, jnp.where(lo == 1, 3, 5))
    cls = g_sel * len(PAIRS) + pair_base + (hi - lo - 1)
    return cls, s_lo / total, s_hi / total


def _block_ranks(cls_row):
    n = cls_row.shape[1]
    cls_col = jnp.transpose(jnp.broadcast_to(cls_row, (LANES, n)))[:, 0:1]
    ii = lax.broadcasted_iota(jnp.int32, (n, n), 0)
    jj = lax.broadcasted_iota(jnp.int32, (n, n), 1)
    earlier_same = jnp.logical_and(cls_col == cls_row, ii < jj)
    rank = jnp.sum(jnp.where(earlier_same, 1.0, 0.0), axis=0, keepdims=True)
    lane = lax.broadcasted_iota(jnp.int32, (1, n), 1)
    counts = jnp.zeros((1, n), F32)
    for c in range(N_CLASSES):
        cnt = jnp.sum(jnp.where(cls_row == c, 1.0, 0.0), axis=1, keepdims=True)
        counts = jnp.where(lane == c, cnt, counts)
    return rank, counts


def _merge_call(yf, yb, rg, cb, cu, om, gt, xa, g1, a2, b2, wts, *, nj, ncb):
    rows, d = xa.shape
    tm = ROW_BLOCK
    nb = rows // tm
    halo = BF16_SUBLANES
    per_blk = tm // halo
    blk = lambda i: jnp.minimum(i, nb - 1)
    row = lambda i: (blk(i), 0)
    mod = lambda i: (2 * (blk(i) // nj) + ((blk(i) % nj) >= ncb).astype(jnp.int32), 0, 0)
    prev = lambda i: (jnp.maximum(blk(i) * per_blk - 1, 0), 0)
    nxt = lambda i: (jnp.minimum((blk(i) + 1) * per_blk, rows // halo - 1), 0)
    in_specs = [pl.BlockSpec((tm, RV_W), row), pl.BlockSpec((tm, RV_W), row),
                pl.BlockSpec((tm, RV_W), row),
                pl.BlockSpec((tm, CONV_WIDTH), row), pl.BlockSpec((tm, CONV_WIDTH), row),
                pl.BlockSpec((halo, CONV_WIDTH), prev), pl.BlockSpec((halo, CONV_WIDTH), nxt),
                pl.BlockSpec((tm, MLA_O), row), pl.BlockSpec((tm, N_BRANCH * d), row),
                pl.BlockSpec((tm, d), row),
                pl.BlockSpec((None, 1, d), mod), pl.BlockSpec((None, 1, d), mod),
                pl.BlockSpec((None, 1, d), mod)]
    in_specs += [_const_spec(w.shape) for w in wts]
    return pl.pallas_call(
        functools.partial(_merge_body, nj=nj, ncb=ncb, d_model=d, nb=nb),
        out_shape=[jax.ShapeDtypeStruct((rows, d), F32), jax.ShapeDtypeStruct((rows, d // 2), jnp.uint32),
                   jax.ShapeDtypeStruct((nb * ROUTE_ROWS, tm), F32),
                   jax.ShapeDtypeStruct((rows, LANES), F32)],
        grid=(nb + 1,),
        in_specs=in_specs,
        out_specs=[pl.BlockSpec((tm, d), row), pl.BlockSpec((tm, d // 2), row),
                   pl.BlockSpec((ROUTE_ROWS, tm), lambda i: (jnp.maximum(i - 1, 0), 0)),
                   pl.BlockSpec((tm, LANES), lambda i: (jnp.maximum(i - 1, 0), 0))],
        scratch_shapes=[pltpu.VMEM((tm, LANES), F32)],
        compiler_params=_params(1),
        name="merge_out_proj",
    )(yf, yb, rg, cb, cu, cu, cu, om, gt, xa, g1, a2, b2, *wts)


def _moe_body(ea_ref, eb_ref, nu_ref, h_ref, gw_ref, w1a_ref, w3a_ref, w2a_ref, w1b_ref, w3b_ref,
              w2b_ref, o_ref, w13a_s, w2a_s, w13b_s, w2b_s):
    t = pl.program_id(0)
    used = t < nu_ref[0]
    prev = jnp.maximum(t - 1, 0)

    def refresh(e_ref, w1_ref, w3_ref, w2_ref, w13_s, w2_s):
        @pl.when(jnp.logical_and(used, jnp.logical_or(t == 0, e_ref[t] != e_ref[prev])))
        def _():
            w13_s[:, :D_EXPERT] = w1_ref[...].astype(BF16)
            w13_s[:, D_EXPERT:] = w3_ref[...].astype(BF16)
            w2_s[...] = w2_ref[...].astype(BF16)

    refresh(ea_ref, w1a_ref, w3a_ref, w2a_ref, w13a_s, w2a_s)
    refresh(eb_ref, w1b_ref, w3b_ref, w2b_ref, w13b_s, w2b_s)

    @pl.when(used)
    def _():
        h = _unpack_pairs(h_ref[...]).astype(BF16)
        gw = gw_ref[...]

        def expert(w13_s, w2_s, wt):
            a = _dot(h, w13_s[...])
            act = _silu(a[:, :D_EXPERT]) * a[:, D_EXPERT:] * wt
            return _dot(act.astype(BF16), w2_s[...])

        o_ref[...] = _pack_pairs(expert(w13a_s, w2a_s, gw[:, 0:1])
                                 + expert(w13b_s, w2b_s, gw[:, 1:2]))

    @pl.when(jnp.logical_not(used))
    def _():
        o_ref[...] = jnp.zeros_like(o_ref)


def _moe_call(tile_ea, tile_eb, n_used, hs, gw, w1, w3, w2, layer):
    npad = hs.shape[0]
    d = w1.shape[2]
    tmo = MOE_TILE
    row = lambda t, ea, eb, nu: (t, 0)
    wa = lambda t, ea, eb, nu: (layer, ea[t], 0, 0)
    wb = lambda t, ea, eb, nu: (layer, eb[t], 0, 0)
    up = lambda im: pl.BlockSpec((None, None, d, D_EXPERT), im)
    down = lambda im: pl.BlockSpec((None, None, D_EXPERT, d), im)
    grid_spec = pltpu.PrefetchScalarGridSpec(
        num_scalar_prefetch=3,
        grid=(npad // tmo,),
        in_specs=[pl.BlockSpec((tmo, d // 2), row), pl.BlockSpec((tmo, LANES), row),
                  up(wa), up(wa), down(wa), up(wb), up(wb), down(wb)],
        out_specs=pl.BlockSpec((tmo, d // 2), row),
        scratch_shapes=[pltpu.VMEM((d, 2 * D_EXPERT), BF16), pltpu.VMEM((D_EXPERT, d), BF16)] * 2)
    return pl.pallas_call(
        _moe_body,
        out_shape=jax.ShapeDtypeStruct((npad, d // 2), jnp.uint32),
        grid_spec=grid_spec,
        compiler_params=_params(1),
        name="moe_experts",
    )(tile_ea, tile_eb, n_used, hs, gw, w1, w3, w2, w1, w3, w2)


def _dispatch(route, n_tok):
    tmo = MOE_TILE
    n_tiles = n_tok // tmo + N_CLASSES
    npad = n_tiles * tmo
    cls = route[:, 0, :].astype(jnp.int32)
    rank = route[:, 1, :].astype(jnp.int32)
    counts = route[:, 2, :N_CLASSES].astype(jnp.int32)
    tiles_per = (jnp.sum(counts, axis=0) + tmo - 1) // tmo
    tile_end = jnp.cumsum(tiles_per)
    offs = (tile_end - tiles_per) * tmo
    block_base = offs[None, :] + jnp.cumsum(counts, axis=0) - counts
    classes = jnp.arange(N_CLASSES, dtype=jnp.int32)
    base = jnp.sum(jnp.where(cls[:, :, None] == classes, block_base[:, None, :], 0), axis=-1)
    dest = (base + rank).reshape(-1)
    tile_ids = jnp.arange(n_tiles, dtype=jnp.int32)
    tile_cls = jnp.sum((tile_end[None, :] <= tile_ids[:, None]).astype(jnp.int32), axis=1)
    tile_cls = jnp.minimum(tile_cls, N_CLASSES - 1)
    pa = jnp.asarray([p[0] for p in PAIRS], jnp.int32)
    pb = jnp.asarray([p[1] for p in PAIRS], jnp.int32)
    group = (tile_cls // len(PAIRS)) * EXPERTS_PER_GROUP
    tile_ea = group + pa[tile_cls % len(PAIRS)]
    tile_eb = group + pb[tile_cls % len(PAIRS)]
    n_used = tile_end[-1:].astype(jnp.int32)
    return dest, npad, tile_ea, tile_eb, n_used


def _sc_mesh():
    return plsc.VectorSubcoreMesh(core_axis_name="c", subcore_axis_name="s",
                                  num_cores=SC_CORES, num_subcores=SC_SUBCORES)


def _sc_chunks(idx):
    assert idx.shape[0] % (SC_WORKERS * GATHER_CHUNK) == 0
    return idx.reshape(SC_WORKERS, -1, GATHER_CHUNK)


def _sc_pipeline(n_chunks, fetch, put):
    assert n_chunks % 2 == 0
    fetch(0, 0).start()

    @pl.loop(0, n_chunks, step=2)
    def _(c0):
        for b in (0, 1):
            c = c0 + b
            fetch(c, b).wait()

            @pl.when(c >= 1)
            def _():
                put(c - 1, 1 - b).wait()

            @pl.when(c + 1 < n_chunks)
            def _():
                fetch(c + 1, 1 - b).start()

            put(c, b).start()

    put(n_chunks - 1, 1).wait()


def _sc_scratch(n_chunks, d, dtype):
    return [pltpu.VMEM((n_chunks, GATHER_CHUNK), jnp.int32),
            pltpu.VMEM((2, GATHER_CHUNK, d), dtype),
            pltpu.SemaphoreType.DMA((2,)), pltpu.SemaphoreType.DMA((2,))]


def _sc_gather_rows(table, idx):
    d = table.shape[1]
    idx3 = _sc_chunks(idx)
    n_chunks = idx3.shape[1]
    assert table.dtype.itemsize == 4

    @functools.partial(
        pl.kernel, mesh=_sc_mesh(),
        out_type=jax.ShapeDtypeStruct((idx.shape[0], d), table.dtype),
        scratch_types=_sc_scratch(n_chunks, d, table.dtype),
        name="sc_row_gather")
    def gather(table_hbm, idx_hbm, out_hbm, idx_v, rows_v, sem_in, sem_out):
        worker = lax.axis_index("s") * SC_CORES + lax.axis_index("c")
        pltpu.sync_copy(idx_hbm.at[worker], idx_v)

        def fetch(c, b):
            return pltpu.make_async_copy(table_hbm.at[idx_v.at[c]], rows_v.at[b], sem_in.at[b])

        def put(c, b):
            off = (worker * n_chunks + c) * GATHER_CHUNK
            return pltpu.make_async_copy(rows_v.at[b], out_hbm.at[pl.ds(off, GATHER_CHUNK)],
                                         sem_out.at[b])

        _sc_pipeline(n_chunks, fetch, put)

    return gather(table, idx3)


def _sc_scatter_rows(rows, idx, n_out):
    d = rows.shape[1]
    idx3 = _sc_chunks(idx)
    n_chunks = idx3.shape[1]
    assert rows.dtype.itemsize == 4

    @functools.partial(
        pl.kernel, mesh=_sc_mesh(),
        out_type=jax.ShapeDtypeStruct((n_out, d), rows.dtype),
        scratch_types=_sc_scratch(n_chunks, d, rows.dtype),
        name="sc_row_scatter")
    def scatter(rows_hbm, idx_hbm, out_hbm, idx_v, rows_v, sem_in, sem_out):
        worker = lax.axis_index("s") * SC_CORES + lax.axis_index("c")
        pltpu.sync_copy(idx_hbm.at[worker], idx_v)

        def fetch(c, b):
            off = (worker * n_chunks + c) * GATHER_CHUNK
            return pltpu.make_async_copy(rows_hbm.at[pl.ds(off, GATHER_CHUNK)], rows_v.at[b],
                                         sem_in.at[b])

        def put(c, b):
            return pltpu.make_async_copy(rows_v.at[b], out_hbm.at[idx_v.at[c]], sem_out.at[b])

        _sc_pipeline(n_chunks, fetch, put)

    return scatter(rows, idx3)


def _final_body(x_ref, f_ref, g2_ref, fn_ref, o_ref):
    o_ref[...] = _rms(x_ref[...] + g2_ref[...] * _unpack_pairs(f_ref[...])) * fn_ref[...]


def _final_call(xmid, fg, g2, final_norm, *, n_batch, nj, ncb):
    rows, d = xmid.shape
    tm = ROW_BLOCK
    njl = nj - ncb
    src = lambda b, j: (b * nj + ncb + j, 0)
    return pl.pallas_call(
        _final_body,
        out_shape=jax.ShapeDtypeStruct((n_batch * njl * tm, d), F32),
        grid=(n_batch, njl),
        in_specs=[pl.BlockSpec((tm, d), src), pl.BlockSpec((tm, d // 2), src),
                  pl.BlockSpec((None, 1, d), lambda b, j: (2 * b + 1, 0, 0)),
                  _const_spec((1, d))],
        out_specs=pl.BlockSpec((tm, d), lambda b, j: (b * njl + j, 0)),
        compiler_params=_params(2),
        name="final_norm",
    )(xmid, fg, g2, final_norm.reshape(1, d))


def _split_cols(w, sizes):
    out, off = [], 0
    for s in sizes:
        out.append(w[:, off:off + s])
        off += s
    out.append(w[:, off:])
    return out


def _layer_weights(w_in, w_uq, w_ukv, q_norm, kv_norm):
    wq, wk, wv, wg, wcb, wcc, wcx, wqd, wkvd, wkr, wgate = _split_cols(w_in, IN_SIZES)
    rope_lanes = lambda w: jnp.pad(w, ((0, 0), (MLA_NOPE, HEAD_PAD - MLA_NOPE - MLA_ROPE)))
    w_ext = jnp.concatenate(
        [wq, wk, wv, wg, wcb, wcc, wcx, wqd, wkvd, rope_lanes(wkr), wgate], axis=1).astype(BF16)

    uq = w_uq.reshape(MLA_Q_RANK, MLA_HEADS, MLA_NOPE + MLA_ROPE)
    tail = HEAD_PAD - MLA_NOPE - MLA_ROPE
    uq_pad = jnp.pad(uq, ((0, 0), (0, 0), (0, tail)))
    wuq = uq_pad.reshape(MLA_Q_RANK, MLA_W).astype(BF16)

    ukv = w_ukv.reshape(MLA_KV_RANK, MLA_HEADS, MLA_NOPE + MLA_V)
    wk_up = jnp.pad(ukv[..., :MLA_NOPE], ((0, 0), (0, 0), (0, HEAD_PAD - MLA_NOPE)))
    wk_up = wk_up.reshape(MLA_KV_RANK, MLA_W).astype(BF16)
    wv_pad = jnp.pad(ukv[..., MLA_NOPE:], ((0, 0), (0, 0), (0, HEAD_PAD - MLA_V)))
    wv_pad = wv_pad.reshape(MLA_KV_RANK, MLA_W).astype(BF16)
    ones_row = jnp.zeros((MLA_HEADS, HEAD_PAD), F32).at[:, MLA_V].set(1.0).reshape(1, MLA_W)
    q_gain = q_norm.astype(F32) * ((MLA_NOPE + MLA_ROPE) ** -0.5 * math.log2(math.e))
    return (w_ext, wuq, wk_up, wv_pad, q_gain.reshape(1, -1),
            kv_norm.reshape(1, -1).astype(F32), ones_row)


def _rotary_tables(n_ctx, n_lat):
    n_all = n_ctx + n_lat
    row = lax.broadcasted_iota(jnp.int32, (n_all, LANES), 0)
    lane = lax.broadcasted_iota(jnp.int32, (n_all, LANES), 1)
    is_ctx = row < n_ctx
    pos = row - n_ctx
    grid_row = pos // GRID_W
    grid_col = pos - grid_row * GRID_W

    def inv_freq(idx, half):
        return ROPE_BASE ** (-idx.astype(F32) / half)

    def table(ang, first_half, live):
        cos = jnp.where(is_ctx, 1.0, jnp.cos(ang))
        sin = jnp.where(is_ctx, 0.0, jnp.where(first_half, -jnp.sin(ang), jnp.sin(ang)))
        return jnp.where(live, cos, 0.0), jnp.where(live, sin, 0.0)

    half = RET_DK // 2
    ang = pos.astype(F32) * inv_freq(lane % half, half)
    cr, sr = table(ang, (lane % RET_DK) < half, True)

    quarter = MLA_ROPE // 4
    rl = lane - MLA_NOPE
    coord = jnp.where(rl < MLA_ROPE // 2, grid_row, grid_col)
    ang = coord.astype(F32) * inv_freq(rl % quarter, quarter)
    cm, sm = table(ang, (rl % (2 * quarter)) < quarter,
                   jnp.logical_and(rl >= 0, rl < MLA_ROPE))
    cm = jnp.where(rl < 0, 1.0, cm)
    return cr, sr, cm, sm


def _retention_consts(ret_decay):
    log_gf = jax.nn.log_sigmoid(ret_decay[0].astype(F32))
    log_gb = jax.nn.log_sigmoid(ret_decay[1].astype(F32))
    idx = jnp.arange(RET_CHUNK, dtype=F32)
    rel = idx[:, None] - idx[None, :]
    dm_f = jnp.where(rel >= 0, jnp.exp(log_gf[:, None, None] * jnp.maximum(rel, 0.0)[None]), 0.0)
    dm_b = jnp.where(rel < 0, jnp.exp(log_gb[:, None, None] * jnp.maximum(-rel, 0.0)[None]), 0.0)

    def lanes(t):
        return jnp.repeat(t, RET_DK, axis=1)

    xi_f = lanes(jnp.exp(log_gf[None, :] * (idx + 1.0)[:, None]))
    zt_f = lanes(jnp.exp(log_gf[None, :] * (RET_CHUNK - 1 - idx)[:, None]))
    xi_b = lanes(jnp.exp(log_gb[None, :] * (RET_CHUNK - idx)[:, None]))
    zt_b = lanes(jnp.exp(log_gb[None, :] * idx[:, None]))
    cd = lambda lg: jnp.broadcast_to(jnp.exp(lg * RET_CHUNK)[:, None, None], (RET_HEADS, 1, RET_DV))
    return dm_f, dm_b, xi_f, zt_f, xi_b, zt_b, cd(log_gf), cd(log_gb)


def kernel(x, c, ctx, c_ctx, w_ada, b_ada, norm1, norm2, w_in, ret_decay, ret_gn, w_ret_o, conv_w,
           w_conv_o, mla_q_norm, w_uq, mla_kv_norm, w_ukv, w_mla_o, w_out, w_router, router_bias,
           w1, w3, w2, final_norm):
    n_batch, n_lat, d = x.shape
    n_ctx = ctx.shape[1]
    depth = w_ada.shape[0]
    t_all = n_ctx + n_lat
    assert n_ctx % ROW_BLOCK == 0 and n_lat % ROW_BLOCK == 0 and n_lat % GRID_W == 0
    nj = t_all // ROW_BLOCK
    ncb = n_ctx // ROW_BLOCK
    n_tok = n_batch * t_all
    assert n_tok % MOE_TILE == 0
    geom = dict(n_batch=n_batch, nj=nj, ncb=ncb)

    cc = jnp.concatenate([c, c_ctx[None, :]], axis=0)
    cc = jnp.pad(cc, ((0, -cc.shape[0] % 8), (0, 0)))
    mod = _ada_call(cc, w_ada, b_ada)[:, :n_batch + 1].reshape(depth, n_batch + 1, N_MOD, d)
    pick = jnp.stack([jnp.full((n_batch,), n_batch, jnp.int32),
                      jnp.arange(n_batch, dtype=jnp.int32)], axis=1).reshape(-1)
    mod = mod[:, pick]

    tabs = _rotary_tables(n_ctx, n_lat)
    wr_hi, wr_lo = _split_bf16(jnp.pad(w_router.astype(F32), ((0, 0), (0, LANES - N_EXPERTS))))
    rbias = router_bias.astype(F32).reshape(N_EXPERTS, 1)

    out = None
    x_parts = (ctx.reshape(n_batch * n_ctx, d), x.reshape(n_batch * n_lat, d))
    for l in range(depth):
        m = mod[l]
        rowvec = lambda v: v.reshape(2 * n_batch, 1, d)
        a1 = rowvec(norm1[l][None, :] * (1.0 + m[:, 1]))
        b1 = rowvec(m[:, 0])
        g1 = rowvec(m[:, 2])
        a2 = rowvec(norm2[l][None, :] * (1.0 + m[:, 4]))
        b2 = rowvec(m[:, 3])
        g2 = rowvec(m[:, 5])

        wts = _layer_weights(w_in[l], w_uq[l], w_ukv[l], mla_q_norm[l], mla_kv_norm[l])
        proj = _inproj_call(x_parts, a1, b1, tabs, wts, **geom)
        rq, rk, rv, rg, cb, cu, qm, km, vm, gt, xa = proj

        yf, yb = _ret_call(rq, rk, rv, _retention_consts(ret_decay[l]), **geom)
        om = _attn_call(qm, km, vm, n_ctx=n_ctx, **geom)

        merge_wts = (ret_gn[l].reshape(1, -1).astype(F32), conv_w[l].T.astype(F32),
                     w_ret_o[l].astype(BF16), w_conv_o[l].astype(BF16), w_mla_o[l].astype(BF16),
                     w_out[l].astype(BF16), wr_hi, wr_lo, rbias)
        xmid, h2, route, gate_w = _merge_call(yf, yb, rg, cb, cu, om, gt, xa, g1, a2, b2, merge_wts,
                                      nj=nj, ncb=ncb)
        route = route.reshape(-1, ROUTE_ROWS, ROW_BLOCK)
        dest, npad, tile_ea, tile_eb, n_used = _dispatch(route, n_tok)
        hs = _sc_scatter_rows(h2, dest, npad)
        gw = _sc_scatter_rows(gate_w, dest, npad)
        f_sorted = _moe_call(tile_ea, tile_eb, n_used, hs, gw, w1, w3, w2, l)
        fg = _sc_gather_rows(f_sorted, dest)

        if l < depth - 1:
            x_parts = (xmid, fg, g2)
        else:
            out = _final_call(xmid, fg, g2, final_norm, **geom)
    return out.reshape(n_batch, n_lat, d)
```

```python
import functools
import math

import jax
import jax.numpy as jnp
from jax import lax
from jax.experimental import pallas as pl
from jax.experimental.pallas import tpu as pltpu
from jax.experimental.pallas import tpu_sc as plsc

F32 = jnp.float32
BF16 = jnp.bfloat16

GRID_W = 64
RMS_EPS = 1e-6
ROPE_BASE = 10000.0
N_MOD = 6
RET_HEADS = 4
RET_DK = 64
RET_DV = 128
RET_CHUNK = 256
CONV_WIDTH = 512
MLA_HEADS = 8
MLA_Q_RANK = 384
MLA_KV_RANK = 256
MLA_NOPE = 64
MLA_ROPE = 32
MLA_V = 64
N_BRANCH = 3
N_EXPERTS = 16
N_GROUPS = 4
EXPERTS_PER_GROUP = N_EXPERTS // N_GROUPS
D_EXPERT = 512
IN_SIZES = (RET_HEADS * RET_DK, RET_HEADS * RET_DK, RET_HEADS * RET_DV, RET_HEADS * RET_DV,
            CONV_WIDTH, CONV_WIDTH, CONV_WIDTH, MLA_Q_RANK, MLA_KV_RANK, MLA_ROPE)

LANES = 128
BF16_SUBLANES = 16
VMEM_LIMIT = 56 * 1024 * 1024
SC_CORES = 2
SC_SUBCORES = 16
SC_WORKERS = SC_CORES * SC_SUBCORES

HEAD_PAD = LANES
ROW_BLOCK = 256
MOE_TILE = 512
ATTN_KEY_TILES = (768, 512, 256, 128)
ROUTE_ROWS = 8
RET_BATCH_GROUP = 4
INPROJ_BATCH_GROUP = 2
GATHER_CHUNK = 96
PAIRS = [(a, b) for a in range(EXPERTS_PER_GROUP) for b in range(a + 1, EXPERTS_PER_GROUP)]
N_CLASSES = N_GROUPS * len(PAIRS)

RQ_W = RET_HEADS * RET_DK
RV_W = RET_HEADS * RET_DV
MLA_W = MLA_HEADS * HEAD_PAD
MLA_O = MLA_HEADS * MLA_V


def _const_spec(shape):
    nd = len(shape)
    return pl.BlockSpec(shape, lambda *_: (0,) * nd, pipeline_mode=pl.Buffered(1))


def _params(n_axes):
    return pltpu.CompilerParams(dimension_semantics=("arbitrary",) * n_axes,
                                vmem_limit_bytes=VMEM_LIMIT)


def _dot(a, b):
    return jnp.dot(a, b, preferred_element_type=F32)


def _split_bf16(a):
    hi = a.astype(BF16)
    lo = (a - hi.astype(F32)).astype(BF16)
    return hi, lo


def _silu(v):
    return v * jax.nn.sigmoid(v)


def _rms(v):
    return v * lax.rsqrt(jnp.mean(v * v, axis=-1, keepdims=True) + RMS_EPS)


def _pack_pairs(v):
    half = v.shape[1] // 2
    bits = lambda t: lax.bitcast_convert_type(t.astype(BF16).astype(F32), jnp.uint32)
    return (bits(v[:, :half]) & jnp.uint32(0xFFFF0000)) | (bits(v[:, half:]) >> 16)


def _unpack_pairs(u):
    hi = lax.bitcast_convert_type(u & jnp.uint32(0xFFFF0000), F32)
    lo = lax.bitcast_convert_type(u << 16, F32)
    return jnp.concatenate([hi, lo], axis=1)


def _ada_body(c_ref, w_ref, b_ref, o_ref):
    a_hi, a_lo = _split_bf16(_silu(c_ref[...]))
    w_hi, w_lo = _split_bf16(w_ref[...])
    o_ref[...] = _dot(a_hi, w_hi) + _dot(a_hi, w_lo) + _dot(a_lo, w_hi) + b_ref[...]


def _ada_call(cc, w_ada, b_ada):
    depth, d, nm = w_ada.shape
    rows = cc.shape[0]
    cb = nm // 4
    return pl.pallas_call(
        _ada_body,
        out_shape=jax.ShapeDtypeStruct((depth, rows, nm), F32),
        grid=(depth, nm // cb),
        in_specs=[pl.BlockSpec((rows, d), lambda l, n: (0, 0)),
                  pl.BlockSpec((None, d, cb), lambda l, n: (l, 0, n)),
                  pl.BlockSpec((None, 1, cb), lambda l, n: (l, 0, n))],
        out_specs=pl.BlockSpec((None, rows, cb), lambda l, n: (l, 0, n)),
        compiler_params=_params(2),
        name="ada_mod",
    )(cc, w_ada, b_ada.reshape(depth, 1, nm))


_O_RQ = 0
_O_RK = _O_RQ + RQ_W
_O_RV = _O_RK + RQ_W
_O_RG = _O_RV + RV_W
_O_CB = _O_RG + RV_W
_O_CC = _O_CB + CONV_WIDTH
_O_CX = _O_CC + CONV_WIDTH
_O_QD = _O_CX + CONV_WIDTH
_O_KVD = _O_QD + MLA_Q_RANK
_O_KR = _O_KVD + MLA_KV_RANK
_O_GT = _O_KR + LANES


def _rot_half(v, half):
    width = v.shape[1]
    lane = lax.broadcasted_iota(jnp.int32, v.shape, 1)
    first = (lane % (2 * half)) < half
    return jnp.where(first, pltpu.roll(v, width - half, 1), pltpu.roll(v, half, 1))


def _inproj_body(*refs, d_model, fused, ncb):
    n_lead = 3 if fused else 2
    lead, refs = refs[:n_lead], refs[n_lead:]
    shared, outs = refs[:13], refs[13:]
    a1_ref, b1_ref = shared[:2]
    for g in range(a1_ref.shape[0]):
        if fused:
            xm_ref, f_ref, g2_ref = lead
            x = xm_ref[g] + g2_ref[g] * _unpack_pairs(f_ref[g])
        else:
            ctx_ref, lat_ref = lead
            x = jnp.where(pl.program_id(0) < ncb, ctx_ref[g], lat_ref[g])
        _inproj_block(x, a1_ref[g], b1_ref[g], *shared[2:], *[o.at[g] for o in outs],
                      d_model=d_model)


def _inproj_block(x, a1, b1, cr_ref, sr_ref, cm_ref, sm_ref, w_ref, wuq_ref, wk_ref, wv_ref, qn_ref,
                  kvn_ref, ones_ref, rq_ref, rk_ref, rv_ref, rg_ref, cb_ref, cu_ref, qm_ref, km_ref,
                  vm_ref, gt_ref, xo_ref, *, d_model):
    xo_ref[...] = x
    h = (_rms(x) * a1 + b1).astype(BF16)

    def mm(off, width):
        return _dot(h, w_ref[:, off:off + width])

    cr = jnp.concatenate([cr_ref[...]] * (RQ_W // cr_ref.shape[1]), axis=1)
    sr = jnp.concatenate([sr_ref[...]] * (RQ_W // sr_ref.shape[1]), axis=1)
    q = mm(_O_RQ, RQ_W)
    rq_ref[...] = (q * cr + _rot_half(q, RET_DK // 2) * sr).astype(BF16)
    k = mm(_O_RK, RQ_W)
    rk_ref[...] = ((k * cr + _rot_half(k, RET_DK // 2) * sr) * (RET_DK ** -0.5)).astype(BF16)
    rv_ref[...] = mm(_O_RV, RV_W).astype(BF16)
    rg_ref[...] = _silu(mm(_O_RG, RV_W)).astype(BF16)
    cb_ref[...] = mm(_O_CB, CONV_WIDTH).astype(BF16)
    cu_ref[...] = (mm(_O_CC, CONV_WIDTH) * mm(_O_CX, CONV_WIDTH)).astype(BF16)

    cm = cm_ref[...]
    sm = sm_ref[...]
    qn = (_rms(mm(_O_QD, MLA_Q_RANK)) * qn_ref[...]).astype(BF16)
    for hd in range(MLA_HEADS):
        lo = hd * HEAD_PAD
        qa = _dot(qn, wuq_ref[:, lo:lo + HEAD_PAD])
        qm_ref[:, lo:lo + HEAD_PAD] = (qa * cm + _rot_half(qa, MLA_ROPE // 4) * sm).astype(BF16)

    kvn = (_rms(mm(_O_KVD, MLA_KV_RANK)) * kvn_ref[...]).astype(BF16)
    kr = mm(_O_KR, HEAD_PAD)
    kr = kr * cm + _rot_half(kr, MLA_ROPE // 4) * sm
    kn = _dot(kvn, wk_ref[...])
    for hd in range(MLA_HEADS):
        lo = hd * HEAD_PAD
        km_ref[:, lo:lo + HEAD_PAD] = (kn[:, lo:lo + HEAD_PAD] + kr).astype(BF16)
    vm_ref[...] = (_dot(kvn, wv_ref[...]) + ones_ref[...]).astype(BF16)

    for br in range(N_BRANCH):
        gt_ref[:, br * d_model:(br + 1) * d_model] = jax.nn.sigmoid(
            mm(_O_GT + br * d_model, d_model)).astype(BF16)


def _inproj_call(x_parts, a1, b1, tabs, wts, *, n_batch, nj, ncb):
    fused = len(x_parts) == 3
    d = x_parts[0].shape[-1]
    tm = ROW_BLOCK
    t_all = nj * tm
    rows = n_batch * t_all
    grp = math.gcd(n_batch, INPROJ_BATCH_GROUP)
    per_batch = lambda a: a.reshape(n_batch, -1, a.shape[-1])
    row = lambda j, b: (b, j, 0)
    mod = lambda j, b: (b, (j >= ncb).astype(jnp.int32), 0, 0)
    mod_spec = pl.BlockSpec((grp, None, 1, d), mod)
    as_mod = lambda a: a.reshape(n_batch, 2, 1, d)
    tab = lambda j, b: (j, 0)
    if fused:
        xm, f, g2 = x_parts
        lead = (per_batch(xm), per_batch(f), as_mod(g2))
        in_specs = [pl.BlockSpec((grp, tm, d), row), pl.BlockSpec((grp, tm, d // 2), row), mod_spec]
    else:
        lead = x_parts
        in_specs = [pl.BlockSpec((grp, tm, d), lambda j, b: (b, jnp.where(j < ncb, j, 0), 0)),
                    pl.BlockSpec((grp, tm, d), lambda j, b: (b, jnp.where(j < ncb, 0, j - ncb), 0))]
    in_specs += [mod_spec, mod_spec]
    in_specs += [pl.BlockSpec((tm, t.shape[1]), tab) for t in tabs]
    in_specs += [_const_spec(w.shape) for w in wts]
    widths = (RQ_W, RQ_W, RV_W, RV_W, CONV_WIDTH, CONV_WIDTH, MLA_W, MLA_W, MLA_W, N_BRANCH * d)
    out_shape = [jax.ShapeDtypeStruct((n_batch, t_all, w), BF16) for w in widths]
    out_specs = [pl.BlockSpec((grp, tm, w), row) for w in widths]
    out_shape.append(jax.ShapeDtypeStruct((n_batch, t_all, d), F32))
    out_specs.append(pl.BlockSpec((grp, tm, d), row))
    outs = pl.pallas_call(
        functools.partial(_inproj_body, d_model=d, fused=fused, ncb=ncb),
        out_shape=out_shape,
        grid=(nj, n_batch // grp),
        in_specs=in_specs,
        out_specs=out_specs,
        compiler_params=_params(2),
        name="in_proj",
    )(*lead, as_mod(a1), as_mod(b1), *tabs, *wts)
    return [o.reshape(rows, o.shape[-1]) for o in outs]


def _ret_direction(q_ref, k_ref, v_ref, y_ref, s_ref, dm_ref, xi_ref, zt_ref, cd_ref, chunk_order):
    states = [s_ref[hd] for hd in range(RET_HEADS)]
    for c in chunk_order:
        rows = slice(c * RET_CHUNK, (c + 1) * RET_CHUNK)
        q = q_ref[rows, :]
        k = k_ref[rows, :]
        v = v_ref[rows, :]
        qx = (q.astype(F32) * xi_ref[...]).astype(BF16)
        kz = (k.astype(F32) * zt_ref[...]).astype(BF16)
        for hd in range(RET_HEADS):
            ks = slice(hd * RET_DK, (hd + 1) * RET_DK)
            vs = slice(hd * RET_DV, (hd + 1) * RET_DV)
            vh = v[:, vs]
            sc = lax.dot_general(q[:, ks], k[:, ks], (((1,), (1,)), ((), ())),
                                 preferred_element_type=F32)
            inner = _dot((sc * dm_ref[hd]).astype(BF16), vh)
            cross = _dot(qx[:, ks], states[hd].astype(BF16))
            y_ref[rows, vs] = (inner + cross).astype(BF16)
            upd = lax.dot_general(kz[:, ks], vh, (((0,), (0,)), ((), ())),
                                  preferred_element_type=F32)
            states[hd] = cd_ref[hd] * states[hd] + upd
    for hd in range(RET_HEADS):
        s_ref[hd] = states[hd]


def _ret_body(qf_ref, kf_ref, vf_ref, qb_ref, kb_ref, vb_ref,
              dmf_ref, dmb_ref, xif_ref, ztf_ref, xib_ref, ztb_ref, cdf_ref, cdb_ref,
              yf_ref, yb_ref, sf_ref, sb_ref):
    @pl.when(pl.program_id(1) == 0)
    def _():
        sf_ref[...] = jnp.zeros_like(sf_ref)
        sb_ref[...] = jnp.zeros_like(sb_ref)

    n_chunks = ROW_BLOCK // RET_CHUNK
    for g in range(qf_ref.shape[0]):
        _ret_direction(qf_ref.at[g], kf_ref.at[g], vf_ref.at[g], yf_ref.at[g], sf_ref.at[g],
                       dmf_ref, xif_ref, ztf_ref, cdf_ref, range(n_chunks))
        _ret_direction(qb_ref.at[g], kb_ref.at[g], vb_ref.at[g], yb_ref.at[g], sb_ref.at[g],
                       dmb_ref, xib_ref, ztb_ref, cdb_ref, range(n_chunks - 1, -1, -1))


def _ret_call(rq, rk, rv, consts, *, n_batch, nj, ncb):
    rows = rq.shape[0]
    tm = ROW_BLOCK
    grp = math.gcd(n_batch, RET_BATCH_GROUP)
    fwd = lambda b, s: (b, s, 0)

    def bwd(b, s):
        return (b, jnp.where(s < ncb, ncb - 1 - s, nj - 1 - (s - ncb)), 0)

    per_batch = lambda a: a.reshape(n_batch, nj * tm, a.shape[1])
    specs = []
    for im in (fwd, bwd):
        specs += [pl.BlockSpec((grp, tm, RQ_W), im), pl.BlockSpec((grp, tm, RQ_W), im),
                  pl.BlockSpec((grp, tm, RV_W), im)]
    specs += [_const_spec(c.shape) for c in consts]
    q3, k3, v3 = per_batch(rq), per_batch(rk), per_batch(rv)
    yf, yb = pl.pallas_call(
        _ret_body,
        out_shape=[jax.ShapeDtypeStruct((n_batch, nj * tm, RV_W), BF16)] * 2,
        grid=(n_batch // grp, nj),
        in_specs=specs,
        out_specs=[pl.BlockSpec((grp, tm, RV_W), fwd), pl.BlockSpec((grp, tm, RV_W), bwd)],
        scratch_shapes=[pltpu.VMEM((grp, RET_HEADS, RET_DK, RET_DV), F32)] * 2,
        compiler_params=_params(2),
        name="retention",
    )(q3, k3, v3, q3, k3, v3, *consts)
    return yf.reshape(rows, RV_W), yb.reshape(rows, RV_W)


def _pick_tile(n, candidates):
    for c in candidates:
        if n % c == 0:
            return c
    raise ValueError(f"no tile for {n}")


def _attn_body(q_ref, k_ref, v_ref, o_ref, s_ref, *, n_ctx, n_all, ncb):
    tq = q_ref.shape[0]
    heads = [slice(hh * HEAD_PAD, (hh + 1) * HEAD_PAD) for hh in range(2)]

    def attend(n_keys, tk):
        nt = n_keys // tk
        qs = [q_ref[:, hs] for hs in heads]

        def qk(t, mrun):
            r0 = pl.multiple_of(t * tk, tk)
            out = []
            for hh, hs in enumerate(heads):
                s = lax.dot_general(qs[hh], k_ref[pl.ds(r0, tk), hs], (((1,), (1,)), ((), ())),
                                    preferred_element_type=F32)
                s_ref[hh, t, :, 0:tk] = s
                m = mrun[hh]
                for cc in range(tk // LANES):
                    m = jnp.maximum(m, s[:, cc * LANES:(cc + 1) * LANES])
                out.append(m)
            return tuple(out)

        mrun = lax.fori_loop(0, nt, qk, tuple(jnp.full((tq, LANES), -jnp.inf, F32) for _ in heads),
                             unroll=True)
        mrow = [jnp.max(m, axis=-1, keepdims=True) for m in mrun]

        def pv(t, accs):
            r0 = pl.multiple_of(t * tk, tk)
            out = []
            for hh, hs in enumerate(heads):
                p = jnp.exp2(s_ref[hh, t, :, 0:tk] - mrow[hh]).astype(BF16)
                out.append(accs[hh] + _dot(p, v_ref[pl.ds(r0, tk), hs]))
            return tuple(out)

        accs = lax.fori_loop(0, nt, pv, tuple(jnp.zeros((tq, HEAD_PAD), F32) for _ in heads),
                             unroll=True)
        o_ref[...] = jnp.concatenate([a[:, :MLA_V] / a[:, MLA_V:MLA_V + 1] for a in accs],
                                     axis=-1).astype(BF16)

    j = pl.program_id(2)

    @pl.when(j < ncb)
    def _():
        attend(n_ctx, _pick_tile(n_ctx, (256, 128)))

    @pl.when(j >= ncb)
    def _():
        attend(n_all, _pick_tile(n_all, ATTN_KEY_TILES))


def _attn_call(qm, km, vm, *, n_batch, nj, ncb, n_ctx):
    rows = qm.shape[0]
    tm = ROW_BLOCK
    t_all = nj * tm
    tk = _pick_tile(t_all, ATTN_KEY_TILES)
    qmap = lambda b, hp, j: (b * nj + j, hp)
    kmap = lambda b, hp, j: (b, hp)
    return pl.pallas_call(
        functools.partial(_attn_body, n_ctx=n_ctx, n_all=t_all, ncb=ncb),
        out_shape=jax.ShapeDtypeStruct((rows, MLA_O), BF16),
        grid=(n_batch, MLA_HEADS // 2, nj),
        in_specs=[pl.BlockSpec((tm, 2 * HEAD_PAD), qmap),
                  pl.BlockSpec((t_all, 2 * HEAD_PAD), kmap),
                  pl.BlockSpec((t_all, 2 * HEAD_PAD), kmap)],
        out_specs=pl.BlockSpec((tm, 2 * MLA_V), qmap),
        scratch_shapes=[pltpu.VMEM((2, t_all // tk, tm, tk), F32)],
        compiler_params=_params(3),
        name="mla_attention",
    )(qm, km, vm)


def _merge_body(yf_ref, yb_ref, rg_ref, cb_ref, cu_ref, cup_ref, cun_ref, om_ref, gt_ref, x_ref,
                g1_ref, a2_ref, b2_ref, gn_ref, cw_ref, wro_ref, wco_ref, wmo_ref, wout_ref,
                wrh_ref, wrl_ref, rb_ref, xmid_ref, h2_ref, rt_ref, gw_ref, lg_ref, *, nj, ncb, d_model,
                nb):
    tm = x_ref.shape[0]
    step = pl.program_id(0)

    @pl.when(step == 0)
    def _():
        lg_ref[...] = jnp.zeros_like(lg_ref)

    cls, w_lo, w_hi = _route_rows(jnp.transpose(lg_ref[...])[:N_EXPERTS, :], rb_ref[...])
    rank, counts = _block_ranks(cls.astype(F32))
    rt_ref[...] = jnp.concatenate(
        [cls.astype(F32), rank, counts, jnp.zeros((ROUTE_ROWS - 3, tm), F32)], axis=0)
    gw_ref[...] = jnp.transpose(
        jnp.concatenate([w_lo, w_hi, jnp.zeros((LANES - 2, tm), F32)], axis=0))

    j = jnp.minimum(step, nb - 1) % nj
    seg_first = jnp.logical_or(j == 0, j == ncb)
    seg_last = jnp.logical_or(j == ncb - 1, j == nj - 1)

    y = yf_ref[...].astype(F32) + yb_ref[...].astype(F32)
    yn = jnp.concatenate([_rms(y[:, hd * RET_DV:(hd + 1) * RET_DV]) for hd in range(RET_HEADS)],
                         axis=-1) * gn_ref[...]
    y_ret = _dot((rg_ref[...].astype(F32) * yn).astype(BF16), wro_ref[...])

    u = cu_ref[...].astype(F32)
    ridx = lax.broadcasted_iota(jnp.int32, u.shape, 0)
    prev_row = cup_ref[...].astype(F32)[BF16_SUBLANES - 1:, :] * jnp.where(seg_first, 0.0, 1.0)
    next_row = cun_ref[...].astype(F32)[0:1, :] * jnp.where(seg_last, 0.0, 1.0)
    u_prev = jnp.where(ridx == 0, prev_row, pltpu.roll(u, 1, 0))
    u_next = jnp.where(ridx == tm - 1, next_row, pltpu.roll(u, tm - 1, 0))
    conv = u_prev * cw_ref[0:1, :] + u * cw_ref[1:2, :] + u_next * cw_ref[2:3, :]
    y_conv = _dot((cb_ref[...].astype(F32) * conv).astype(BF16), wco_ref[...])

    y_mla = _dot(om_ref[...], wmo_ref[...])

    merged = (gt_ref[:, 0:d_model].astype(F32) * y_ret
              + gt_ref[:, d_model:2 * d_model].astype(F32) * y_conv
              + gt_ref[:, 2 * d_model:3 * d_model].astype(F32) * y_mla)
    x_mid = x_ref[...] + g1_ref[...] * _dot(merged.astype(BF16), wout_ref[...])
    xmid_ref[...] = x_mid

    h2 = _rms(x_mid) * a2_ref[...] + b2_ref[...]
    h_hi, h_lo = _split_bf16(h2)
    lg_ref[...] = _dot(h_hi, wrh_ref[...]) + _dot(h_hi, wrl_ref[...]) + _dot(h_lo, wrh_ref[...])
    h2_ref[...] = _pack_pairs(h_hi)


def _top2_of4(v):
    def first_max(rows):
        best, idx = rows[0], jnp.zeros(rows[0].shape, jnp.int32)
        for e in range(1, len(rows)):
            better = rows[e] > best
            idx = jnp.where(better, e, idx)
            best = jnp.where(better, rows[e], best)
        return best, idx

    b1, i1 = first_max(v)
    b2, i2 = first_max([jnp.where(i1 == e, -jnp.inf, v[e]) for e in range(len(v))])
    return i1, i2, b1, b2


def _route_rows(logits_t, bias):
    scores = jax.nn.sigmoid(logits_t)
    biased = scores + bias
    row = lambda a, e: a[e:e + 1, :]
    best = None
    for g in range(N_GROUPS):
        v = [row(biased, g * EXPERTS_PER_GROUP + e) for e in range(EXPERTS_PER_GROUP)]
        i1, i2, b1, b2 = _top2_of4(v)
        cand = (b1 + b2, jnp.full(i1.shape, g, jnp.int32), i1, i2)
        if best is None:
            best = cand
        else:
            better = cand[0] > best[0]
            best = tuple(jnp.where(better, c, o) for c, o in zip(cand, best))
    _, g_sel, i1, i2 = best
    lo = jnp.minimum(i1, i2)
    hi = jnp.maximum(i1, i2)
    e_lo = g_sel * EXPERTS_PER_GROUP + lo
    e_hi = g_sel * EXPERTS_PER_GROUP + hi
    s_lo = jnp.zeros_like(best[0])
    s_hi = jnp.zeros_like(best[0])
    for e in range(N_EXPERTS):
        s_lo = jnp.where(e_lo == e, row(scores, e), s_lo)
        s_hi = jnp.where(e_hi == e, row(scores, e), s_hi)
    total = s_lo + s_hi
    pair_base = jnp.where(lo == 0, 0, jnp.where(lo == 1, 3, 5))
    cls = g_sel * len(PAIRS) + pair_base + (hi - lo - 1)
    return cls, s_lo / total, s_hi / total


def _block_ranks(cls_row):
    n = cls_row.shape[1]
    cls_col = jnp.transpose(jnp.broadcast_to(cls_row, (LANES, n)))[:, 0:1]
    ii = lax.broadcasted_iota(jnp.int32, (n, n), 0)
    jj = lax.broadcasted_iota(jnp.int32, (n, n), 1)
    earlier_same = jnp.logical_and(cls_col == cls_row, ii < jj)
    rank = jnp.sum(jnp.where(earlier_same, 1.0, 0.0), axis=0, keepdims=True)
    lane = lax.broadcasted_iota(jnp.int32, (1, n), 1)
    counts = jnp.zeros((1, n), F32)
    for c in range(N_CLASSES):
        cnt = jnp.sum(jnp.where(cls_row == c, 1.0, 0.0), axis=1, keepdims=True)
        counts = jnp.where(lane == c, cnt, counts)
    return rank, counts


def _merge_call(yf, yb, rg, cb, cu, om, gt, xa, g1, a2, b2, wts, *, nj, ncb):
    rows, d = xa.shape
    tm = ROW_BLOCK
    nb = rows // tm
    halo = BF16_SUBLANES
    per_blk = tm // halo
    blk = lambda i: jnp.minimum(i, nb - 1)
    row = lambda i: (blk(i), 0)
    mod = lambda i: (2 * (blk(i) // nj) + ((blk(i) % nj) >= ncb).astype(jnp.int32), 0, 0)
    prev = lambda i: (jnp.maximum(blk(i) * per_blk - 1, 0), 0)
    nxt = lambda i: (jnp.minimum((blk(i) + 1) * per_blk, rows // halo - 1), 0)
    in_specs = [pl.BlockSpec((tm, RV_W), row), pl.BlockSpec((tm, RV_W), row),
                pl.BlockSpec((tm, RV_W), row),
                pl.BlockSpec((tm, CONV_WIDTH), row), pl.BlockSpec((tm, CONV_WIDTH), row),
                pl.BlockSpec((halo, CONV_WIDTH), prev), pl.BlockSpec((halo, CONV_WIDTH), nxt),
                pl.BlockSpec((tm, MLA_O), row), pl.BlockSpec((tm, N_BRANCH * d), row),
                pl.BlockSpec((tm, d), row),
                pl.BlockSpec((None, 1, d), mod), pl.BlockSpec((None, 1, d), mod),
                pl.BlockSpec((None, 1, d), mod)]
    in_specs += [_const_spec(w.shape) for w in wts]
    return pl.pallas_call(
        functools.partial(_merge_body, nj=nj, ncb=ncb, d_model=d, nb=nb),
        out_shape=[jax.ShapeDtypeStruct((rows, d), F32), jax.ShapeDtypeStruct((rows, d // 2), jnp.uint32),
                   jax.ShapeDtypeStruct((nb * ROUTE_ROWS, tm), F32),
                   jax.ShapeDtypeStruct((rows, LANES), F32)],
        grid=(nb + 1,),
        in_specs=in_specs,
        out_specs=[pl.BlockSpec((tm, d), row), pl.BlockSpec((tm, d // 2), row),
                   pl.BlockSpec((ROUTE_ROWS, tm), lambda i: (jnp.maximum(i - 1, 0), 0)),
                   pl.BlockSpec((tm, LANES), lambda i: (jnp.maximum(i - 1, 0), 0))],
        scratch_shapes=[pltpu.VMEM((tm, LANES), F32)],
        compiler_params=_params(1),
        name="merge_out_proj",
    )(yf, yb, rg, cb, cu, cu, cu, om, gt, xa, g1, a2, b2, *wts)


def _moe_body(ea_ref, eb_ref, nu_ref, h_ref, gw_ref, w1a_ref, w3a_ref, w2a_ref, w1b_ref, w3b_ref,
              w2b_ref, o_ref, w13a_s, w2a_s, w13b_s, w2b_s):
    t = pl.program_id(0)
    used = t < nu_ref[0]
    prev = jnp.maximum(t - 1, 0)

    def refresh(e_ref, w1_ref, w3_ref, w2_ref, w13_s, w2_s):
        @pl.when(jnp.logical_and(used, jnp.logical_or(t == 0, e_ref[t] != e_ref[prev])))
        def _():
            w13_s[:, :D_EXPERT] = w1_ref[...].astype(BF16)
            w13_s[:, D_EXPERT:] = w3_ref[...].astype(BF16)
            w2_s[...] = w2_ref[...].astype(BF16)

    refresh(ea_ref, w1a_ref, w3a_ref, w2a_ref, w13a_s, w2a_s)
    refresh(eb_ref, w1b_ref, w3b_ref, w2b_ref, w13b_s, w2b_s)

    @pl.when(used)
    def _():
        h = _unpack_pairs(h_ref[...]).astype(BF16)
        gw = gw_ref[...]

        def expert(w13_s, w2_s, wt):
            a = _dot(h, w13_s[...])
            act = _silu(a[:, :D_EXPERT]) * a[:, D_EXPERT:] * wt
            return _dot(act.astype(BF16), w2_s[...])

        o_ref[...] = _pack_pairs(expert(w13a_s, w2a_s, gw[:, 0:1])
                                 + expert(w13b_s, w2b_s, gw[:, 1:2]))

    @pl.when(jnp.logical_not(used))
    def _():
        o_ref[...] = jnp.zeros_like(o_ref)


def _moe_call(tile_ea, tile_eb, n_used, hs, gw, w1, w3, w2, layer):
    npad = hs.shape[0]
    d = w1.shape[2]
    tmo = MOE_TILE
    row = lambda t, ea, eb, nu: (t, 0)
    wa = lambda t, ea, eb, nu: (layer, ea[t], 0, 0)
    wb = lambda t, ea, eb, nu: (layer, eb[t], 0, 0)
    up = lambda im: pl.BlockSpec((None, None, d, D_EXPERT), im)
    down = lambda im: pl.BlockSpec((None, None, D_EXPERT, d), im)
    grid_spec = pltpu.PrefetchScalarGridSpec(
        num_scalar_prefetch=3,
        grid=(npad // tmo,),
        in_specs=[pl.BlockSpec((tmo, d // 2), row), pl.BlockSpec((tmo, LANES), row),
                  up(wa), up(wa), down(wa), up(wb), up(wb), down(wb)],
        out_specs=pl.BlockSpec((tmo, d // 2), row),
        scratch_shapes=[pltpu.VMEM((d, 2 * D_EXPERT), BF16), pltpu.VMEM((D_EXPERT, d), BF16)] * 2)
    return pl.pallas_call(
        _moe_body,
        out_shape=jax.ShapeDtypeStruct((npad, d // 2), jnp.uint32),
        grid_spec=grid_spec,
        compiler_params=_params(1),
        name="moe_experts",
    )(tile_ea, tile_eb, n_used, hs, gw, w1, w3, w2, w1, w3, w2)


def _dispatch(route, n_tok):
    tmo = MOE_TILE
    n_tiles = n_tok // tmo + N_CLASSES
    npad = n_tiles * tmo
    cls = route[:, 0, :].astype(jnp.int32)
    rank = route[:, 1, :].astype(jnp.int32)
    counts = route[:, 2, :N_CLASSES].astype(jnp.int32)
    tiles_per = (jnp.sum(counts, axis=0) + tmo - 1) // tmo
    tile_end = jnp.cumsum(tiles_per)
    offs = (tile_end - tiles_per) * tmo
    block_base = offs[None, :] + jnp.cumsum(counts, axis=0) - counts
    classes = jnp.arange(N_CLASSES, dtype=jnp.int32)
    base = jnp.sum(jnp.where(cls[:, :, None] == classes, block_base[:, None, :], 0), axis=-1)
    dest = (base + rank).reshape(-1)
    tile_ids = jnp.arange(n_tiles, dtype=jnp.int32)
    tile_cls = jnp.sum((tile_end[None, :] <= tile_ids[:, None]).astype(jnp.int32), axis=1)
    tile_cls = jnp.minimum(tile_cls, N_CLASSES - 1)
    pa = jnp.asarray([p[0] for p in PAIRS], jnp.int32)
    pb = jnp.asarray([p[1] for p in PAIRS], jnp.int32)
    group = (tile_cls // len(PAIRS)) * EXPERTS_PER_GROUP
    tile_ea = group + pa[tile_cls % len(PAIRS)]
    tile_eb = group + pb[tile_cls % len(PAIRS)]
    n_used = tile_end[-1:].astype(jnp.int32)
    return dest, npad, tile_ea, tile_eb, n_used


def _sc_mesh():
    return plsc.VectorSubcoreMesh(core_axis_name="c", subcore_axis_name="s",
                                  num_cores=SC_CORES, num_subcores=SC_SUBCORES)


def _sc_chunks(idx):
    assert idx.shape[0] % (SC_WORKERS * GATHER_CHUNK) == 0
    return idx.reshape(SC_WORKERS, -1, GATHER_CHUNK)


def _sc_pipeline(n_chunks, fetch, put):
    assert n_chunks % 2 == 0
    fetch(0, 0).start()

    @pl.loop(0, n_chunks, step=2)
    def _(c0):
        for b in (0, 1):
            c = c0 + b
            fetch(c, b).wait()

            @pl.when(c >= 1)
            def _():
                put(c - 1, 1 - b).wait()

            @pl.when(c + 1 < n_chunks)
            def _():
                fetch(c + 1, 1 - b).start()

            put(c, b).start()

    put(n_chunks - 1, 1).wait()


def _sc_scratch(n_chunks, d, dtype):
    return [pltpu.VMEM((n_chunks, GATHER_CHUNK), jnp.int32),
            pltpu.VMEM((2, GATHER_CHUNK, d), dtype),
            pltpu.SemaphoreType.DMA((2,)), pltpu.SemaphoreType.DMA((2,))]


def _sc_gather_rows(table, idx):
    d = table.shape[1]
    idx3 = _sc_chunks(idx)
    n_chunks = idx3.shape[1]
    assert table.dtype.itemsize == 4

    @functools.partial(
        pl.kernel, mesh=_sc_mesh(),
        out_type=jax.ShapeDtypeStruct((idx.shape[0], d), table.dtype),
        scratch_types=_sc_scratch(n_chunks, d, table.dtype),
        name="sc_row_gather")
    def gather(table_hbm, idx_hbm, out_hbm, idx_v, rows_v, sem_in, sem_out):
        worker = lax.axis_index("s") * SC_CORES + lax.axis_index("c")
        pltpu.sync_copy(idx_hbm.at[worker], idx_v)

        def fetch(c, b):
            return pltpu.make_async_copy(table_hbm.at[idx_v.at[c]], rows_v.at[b], sem_in.at[b])

        def put(c, b):
            off = (worker * n_chunks + c) * GATHER_CHUNK
            return pltpu.make_async_copy(rows_v.at[b], out_hbm.at[pl.ds(off, GATHER_CHUNK)],
                                         sem_out.at[b])

        _sc_pipeline(n_chunks, fetch, put)

    return gather(table, idx3)


def _sc_scatter_rows(rows, idx, n_out):
    d = rows.shape[1]
    idx3 = _sc_chunks(idx)
    n_chunks = idx3.shape[1]
    assert rows.dtype.itemsize == 4

    @functools.partial(
        pl.kernel, mesh=_sc_mesh(),
        out_type=jax.ShapeDtypeStruct((n_out, d), rows.dtype),
        scratch_types=_sc_scratch(n_chunks, d, rows.dtype),
        name="sc_row_scatter")
    def scatter(rows_hbm, idx_hbm, out_hbm, idx_v, rows_v, sem_in, sem_out):
        worker = lax.axis_index("s") * SC_CORES + lax.axis_index("c")
        pltpu.sync_copy(idx_hbm.at[worker], idx_v)

        def fetch(c, b):
            off = (worker * n_chunks + c) * GATHER_CHUNK
            return pltpu.make_async_copy(rows_hbm.at[pl.ds(off, GATHER_CHUNK)], rows_v.at[b],
                                         sem_in.at[b])

        def put(c, b):
            return pltpu.make_async_copy(rows_v.at[b], out_hbm.at[idx_v.at[c]], sem_out.at[b])

        _sc_pipeline(n_chunks, fetch, put)

    return scatter(rows, idx3)


def _final_body(x_ref, f_ref, g2_ref, fn_ref, o_ref):
    o_ref[...] = _rms(x_ref[...] + g2_ref[...] * _unpack_pairs(f_ref[...])) * fn_ref[...]


def _final_call(xmid, fg, g2, final_norm, *, n_batch, nj, ncb):
    rows, d = xmid.shape
    tm = ROW_BLOCK
    njl = nj - ncb
    src = lambda b, j: (b * nj + ncb + j, 0)
    return pl.pallas_call(
        _final_body,
        out_shape=jax.ShapeDtypeStruct((n_batch * njl * tm, d), F32),
        grid=(n_batch, njl),
        in_specs=[pl.BlockSpec((tm, d), src), pl.BlockSpec((tm, d // 2), src),
                  pl.BlockSpec((None, 1, d), lambda b, j: (2 * b + 1, 0, 0)),
                  _const_spec((1, d))],
        out_specs=pl.BlockSpec((tm, d), lambda b, j: (b * njl + j, 0)),
        compiler_params=_params(2),
        name="final_norm",
    )(xmid, fg, g2, final_norm.reshape(1, d))


def _split_cols(w, sizes):
    out, off = [], 0
    for s in sizes:
        out.append(w[:, off:off + s])
        off += s
    out.append(w[:, off:])
    return out


def _layer_weights(w_in, w_uq, w_ukv, q_norm, kv_norm):
    wq, wk, wv, wg, wcb, wcc, wcx, wqd, wkvd, wkr, wgate = _split_cols(w_in, IN_SIZES)
    rope_lanes = lambda w: jnp.pad(w, ((0, 0), (MLA_NOPE, HEAD_PAD - MLA_NOPE - MLA_ROPE)))
    w_ext = jnp.concatenate(
        [wq, wk, wv, wg, wcb, wcc, wcx, wqd, wkvd, rope_lanes(wkr), wgate], axis=1).astype(BF16)

    uq = w_uq.reshape(MLA_Q_RANK, MLA_HEADS, MLA_NOPE + MLA_ROPE)
    tail = HEAD_PAD - MLA_NOPE - MLA_ROPE
    uq_pad = jnp.pad(uq, ((0, 0), (0, 0), (0, tail)))
    wuq = uq_pad.reshape(MLA_Q_RANK, MLA_W).astype(BF16)

    ukv = w_ukv.reshape(MLA_KV_RANK, MLA_HEADS, MLA_NOPE + MLA_V)
    wk_up = jnp.pad(ukv[..., :MLA_NOPE], ((0, 0), (0, 0), (0, HEAD_PAD - MLA_NOPE)))
    wk_up = wk_up.reshape(MLA_KV_RANK, MLA_W).astype(BF16)
    wv_pad = jnp.pad(ukv[..., MLA_NOPE:], ((0, 0), (0, 0), (0, HEAD_PAD - MLA_V)))
    wv_pad = wv_pad.reshape(MLA_KV_RANK, MLA_W).astype(BF16)
    ones_row = jnp.zeros((MLA_HEADS, HEAD_PAD), F32).at[:, MLA_V].set(1.0).reshape(1, MLA_W)
    q_gain = q_norm.astype(F32) * ((MLA_NOPE + MLA_ROPE) ** -0.5 * math.log2(math.e))
    return (w_ext, wuq, wk_up, wv_pad, q_gain.reshape(1, -1),
            kv_norm.reshape(1, -1).astype(F32), ones_row)


def _rotary_tables(n_ctx, n_lat):
    n_all = n_ctx + n_lat
    row = lax.broadcasted_iota(jnp.int32, (n_all, LANES), 0)
    lane = lax.broadcasted_iota(jnp.int32, (n_all, LANES), 1)
    is_ctx = row < n_ctx
    pos = row - n_ctx
    grid_row = pos // GRID_W
    grid_col = pos - grid_row * GRID_W

    def inv_freq(idx, half):
        return ROPE_BASE ** (-idx.astype(F32) / half)

    def table(ang, first_half, live):
        cos = jnp.where(is_ctx, 1.0, jnp.cos(ang))
        sin = jnp.where(is_ctx, 0.0, jnp.where(first_half, -jnp.sin(ang), jnp.sin(ang)))
        return jnp.where(live, cos, 0.0), jnp.where(live, sin, 0.0)

    half = RET_DK // 2
    ang = pos.astype(F32) * inv_freq(lane % half, half)
    cr, sr = table(ang, (lane % RET_DK) < half, True)

    quarter = MLA_ROPE // 4
    rl = lane - MLA_NOPE
    coord = jnp.where(rl < MLA_ROPE // 2, grid_row, grid_col)
    ang = coord.astype(F32) * inv_freq(rl % quarter, quarter)
    cm, sm = table(ang, (rl % (2 * quarter)) < quarter,
                   jnp.logical_and(rl >= 0, rl < MLA_ROPE))
    cm = jnp.where(rl < 0, 1.0, cm)
    return cr, sr, cm, sm


def _retention_consts(ret_decay):
    log_gf = jax.nn.log_sigmoid(ret_decay[0].astype(F32))
    log_gb = jax.nn.log_sigmoid(ret_decay[1].astype(F32))
    idx = jnp.arange(RET_CHUNK, dtype=F32)
    rel = idx[:, None] - idx[None, :]
    dm_f = jnp.where(rel >= 0, jnp.exp(log_gf[:, None, None] * jnp.maximum(rel, 0.0)[None]), 0.0)
    dm_b = jnp.where(rel < 0, jnp.exp(log_gb[:, None, None] * jnp.maximum(-rel, 0.0)[None]), 0.0)

    def lanes(t):
        return jnp.repeat(t, RET_DK, axis=1)

    xi_f = lanes(jnp.exp(log_gf[None, :] * (idx + 1.0)[:, None]))
    zt_f = lanes(jnp.exp(log_gf[None, :] * (RET_CHUNK - 1 - idx)[:, None]))
    xi_b = lanes(jnp.exp(log_gb[None, :] * (RET_CHUNK - idx)[:, None]))
    zt_b = lanes(jnp.exp(log_gb[None, :] * idx[:, None]))
    cd = lambda lg: jnp.broadcast_to(jnp.exp(lg * RET_CHUNK)[:, None, None], (RET_HEADS, 1, RET_DV))
    return dm_f, dm_b, xi_f, zt_f, xi_b, zt_b, cd(log_gf), cd(log_gb)


def kernel(x, c, ctx, c_ctx, w_ada, b_ada, norm1, norm2, w_in, ret_decay, ret_gn, w_ret_o, conv_w,
           w_conv_o, mla_q_norm, w_uq, mla_kv_norm, w_ukv, w_mla_o, w_out, w_router, router_bias,
           w1, w3, w2, final_norm):
    n_batch, n_lat, d = x.shape
    n_ctx = ctx.shape[1]
    depth = w_ada.shape[0]
    t_all = n_ctx + n_lat
    assert n_ctx % ROW_BLOCK == 0 and n_lat % ROW_BLOCK == 0 and n_lat % GRID_W == 0
    nj = t_all // ROW_BLOCK
    ncb = n_ctx // ROW_BLOCK
    n_tok = n_batch * t_all
    assert n_tok % MOE_TILE == 0
    geom = dict(n_batch=n_batch, nj=nj, ncb=ncb)

    cc = jnp.concatenate([c, c_ctx[None, :]], axis=0)
    cc = jnp.pad(cc, ((0, -cc.shape[0] % 8), (0, 0)))
    mod = _ada_call(cc, w_ada, b_ada)[:, :n_batch + 1].reshape(depth, n_batch + 1, N_MOD, d)
    pick = jnp.stack([jnp.full((n_batch,), n_batch, jnp.int32),
                      jnp.arange(n_batch, dtype=jnp.int32)], axis=1).reshape(-1)
    mod = mod[:, pick]

    tabs = _rotary_tables(n_ctx, n_lat)
    wr_hi, wr_lo = _split_bf16(jnp.pad(w_router.astype(F32), ((0, 0), (0, LANES - N_EXPERTS))))
    rbias = router_bias.astype(F32).reshape(N_EXPERTS, 1)

    out = None
    x_parts = (ctx, x)
    for l in range(depth):
        m = mod[l]
        rowvec = lambda v: v.reshape(2 * n_batch, 1, d)
        a1 = rowvec(norm1[l][None, :] * (1.0 + m[:, 1]))
        b1 = rowvec(m[:, 0])
        g1 = rowvec(m[:, 2])
        a2 = rowvec(norm2[l][None, :] * (1.0 + m[:, 4]))
        b2 = rowvec(m[:, 3])
        g2 = rowvec(m[:, 5])

        wts = _layer_weights(w_in[l], w_uq[l], w_ukv[l], mla_q_norm[l], mla_kv_norm[l])
        proj = _inproj_call(x_parts, a1, b1, tabs, wts, **geom)
        rq, rk, rv, rg, cb, cu, qm, km, vm, gt, xa = proj

        yf, yb = _ret_call(rq, rk, rv, _retention_consts(ret_decay[l]), **geom)
        om = _attn_call(qm, km, vm, n_ctx=n_ctx, **geom)

        merge_wts = (ret_gn[l].reshape(1, -1).astype(F32), conv_w[l].T.astype(F32),
                     w_ret_o[l].astype(BF16), w_conv_o[l].astype(BF16), w_mla_o[l].astype(BF16),
                     w_out[l].astype(BF16), wr_hi, wr_lo, rbias)
        xmid, h2, route, gate_w = _merge_call(yf, yb, rg, cb, cu, om, gt, xa, g1, a2, b2, merge_wts,
                                      nj=nj, ncb=ncb)
        route = route.reshape(-1, ROUTE_ROWS, ROW_BLOCK)
        dest, npad, tile_ea, tile_eb, n_used = _dispatch(route, n_tok)
        hs = _sc_scatter_rows(h2, dest, npad)
        gw = _sc_scatter_rows(gate_w, dest, npad)
        f_sorted = _moe_call(tile_ea, tile_eb, n_used, hs, gw, w1, w3, w2, l)
        fg = _sc_gather_rows(f_sorted, dest)

        if l < depth - 1:
            x_parts = (xmid, fg, g2)
        else:
            out = _final_call(xmid, fg, g2, final_norm, **geom)
    return out.reshape(n_batch, n_lat, d)
```

```python
import functools
import math

import jax
import jax.numpy as jnp
from jax import lax
from jax.experimental import pallas as pl
from jax.experimental.pallas import tpu as pltpu
from jax.experimental.pallas import tpu_sc as plsc

F32 = jnp.float32
BF16 = jnp.bfloat16

GRID_W = 64
RMS_EPS = 1e-6
ROPE_BASE = 10000.0
N_MOD = 6
RET_HEADS = 4
RET_DK = 64
RET_DV = 128
RET_CHUNK = 256
CONV_WIDTH = 512
MLA_HEADS = 8
MLA_Q_RANK = 384
MLA_KV_RANK = 256
MLA_NOPE = 64
MLA_ROPE = 32
MLA_V = 64
N_BRANCH = 3
N_EXPERTS = 16
N_GROUPS = 4
EXPERTS_PER_GROUP = N_EXPERTS // N_GROUPS
D_EXPERT = 512
IN_SIZES = (RET_HEADS * RET_DK, RET_HEADS * RET_DK, RET_HEADS * RET_DV, RET_HEADS * RET_DV,
            CONV_WIDTH, CONV_WIDTH, CONV_WIDTH, MLA_Q_RANK, MLA_KV_RANK, MLA_ROPE)

LANES = 128
BF16_SUBLANES = 16
VMEM_LIMIT = 56 * 1024 * 1024
SC_CORES = 2
SC_SUBCORES = 16
SC_WORKERS = SC_CORES * SC_SUBCORES

HEAD_PAD = LANES
ROW_BLOCK = 256
MOE_TILE = 512
ATTN_KEY_TILES = (768, 512, 256, 128)
ROUTE_ROWS = 8
RET_BATCH_GROUP = 4
INPROJ_BATCH_GROUP = 2
MERGE_BATCH_GROUP = 2
GATHER_CHUNK = 96
PAIRS = [(a, b) for a in range(EXPERTS_PER_GROUP) for b in range(a + 1, EXPERTS_PER_GROUP)]
N_CLASSES = N_GROUPS * len(PAIRS)

RQ_W = RET_HEADS * RET_DK
RV_W = RET_HEADS * RET_DV
MLA_W = MLA_HEADS * HEAD_PAD
MLA_O = MLA_HEADS * MLA_V


def _const_spec(shape):
    nd = len(shape)
    return pl.BlockSpec(shape, lambda *_: (0,) * nd, pipeline_mode=pl.Buffered(1))


def _params(n_axes):
    return pltpu.CompilerParams(dimension_semantics=("arbitrary",) * n_axes,
                                vmem_limit_bytes=VMEM_LIMIT)


def _dot(a, b):
    return jnp.dot(a, b, preferred_element_type=F32)


def _split_bf16(a):
    hi = a.astype(BF16)
    lo = (a - hi.astype(F32)).astype(BF16)
    return hi, lo


def _silu(v):
    return v * jax.nn.sigmoid(v)


def _rms(v):
    return v * lax.rsqrt(jnp.mean(v * v, axis=-1, keepdims=True) + RMS_EPS)


def _pack_pairs(v):
    half = v.shape[1] // 2
    bits = lambda t: lax.bitcast_convert_type(t.astype(BF16).astype(F32), jnp.uint32)
    return (bits(v[:, :half]) & jnp.uint32(0xFFFF0000)) | (bits(v[:, half:]) >> 16)


def _unpack_pairs(u):
    hi = lax.bitcast_convert_type(u & jnp.uint32(0xFFFF0000), F32)
    lo = lax.bitcast_convert_type(u << 16, F32)
    return jnp.concatenate([hi, lo], axis=1)


def _ada_body(c_ref, w_ref, b_ref, o_ref):
    a_hi, a_lo = _split_bf16(_silu(c_ref[...]))
    w_hi, w_lo = _split_bf16(w_ref[...])
    o_ref[...] = _dot(a_hi, w_hi) + _dot(a_hi, w_lo) + _dot(a_lo, w_hi) + b_ref[...]


def _ada_call(cc, w_ada, b_ada):
    depth, d, nm = w_ada.shape
    rows = cc.shape[0]
    cb = nm // 4
    return pl.pallas_call(
        _ada_body,
        out_shape=jax.ShapeDtypeStruct((depth, rows, nm), F32),
        grid=(depth, nm // cb),
        in_specs=[pl.BlockSpec((rows, d), lambda l, n: (0, 0)),
                  pl.BlockSpec((None, d, cb), lambda l, n: (l, 0, n)),
                  pl.BlockSpec((None, 1, cb), lambda l, n: (l, 0, n))],
        out_specs=pl.BlockSpec((None, rows, cb), lambda l, n: (l, 0, n)),
        compiler_params=_params(2),
        name="ada_mod",
    )(cc, w_ada, b_ada.reshape(depth, 1, nm))


_O_RQ = 0
_O_RK = _O_RQ + RQ_W
_O_RV = _O_RK + RQ_W
_O_RG = _O_RV + RV_W
_O_CB = _O_RG + RV_W
_O_CC = _O_CB + CONV_WIDTH
_O_CX = _O_CC + CONV_WIDTH
_O_QD = _O_CX + CONV_WIDTH
_O_KVD = _O_QD + MLA_Q_RANK
_O_KR = _O_KVD + MLA_KV_RANK
_O_GT = _O_KR + LANES


def _rot_half(v, half):
    width = v.shape[1]
    lane = lax.broadcasted_iota(jnp.int32, v.shape, 1)
    first = (lane % (2 * half)) < half
    return jnp.where(first, pltpu.roll(v, width - half, 1), pltpu.roll(v, half, 1))


def _inproj_body(*refs, d_model, fused, ncb):
    n_lead = 3 if fused else 2
    lead, refs = refs[:n_lead], refs[n_lead:]
    shared, outs = refs[:13], refs[13:]
    a1_ref, b1_ref = shared[:2]
    for g in range(a1_ref.shape[0]):
        if fused:
            xm_ref, f_ref, g2_ref = lead
            x = xm_ref[g] + g2_ref[g] * _unpack_pairs(f_ref[g])
        else:
            ctx_ref, lat_ref = lead
            x = jnp.where(pl.program_id(0) < ncb, ctx_ref[g], lat_ref[g])
        _inproj_block(x, a1_ref[g], b1_ref[g], *shared[2:], *[o.at[g] for o in outs],
                      d_model=d_model)


def _inproj_block(x, a1, b1, cr_ref, sr_ref, cm_ref, sm_ref, w_ref, wuq_ref, wk_ref, wv_ref, qn_ref,
                  kvn_ref, ones_ref, rq_ref, rk_ref, rv_ref, rg_ref, cb_ref, cu_ref, qm_ref, km_ref,
                  vm_ref, gt_ref, xo_ref, *, d_model):
    xo_ref[...] = x
    h = (_rms(x) * a1 + b1).astype(BF16)

    def mm(off, width):
        return _dot(h, w_ref[:, off:off + width])

    cr = jnp.concatenate([cr_ref[...]] * (RQ_W // cr_ref.shape[1]), axis=1)
    sr = jnp.concatenate([sr_ref[...]] * (RQ_W // sr_ref.shape[1]), axis=1)
    q = mm(_O_RQ, RQ_W)
    rq_ref[...] = (q * cr + _rot_half(q, RET_DK // 2) * sr).astype(BF16)
    k = mm(_O_RK, RQ_W)
    rk_ref[...] = ((k * cr + _rot_half(k, RET_DK // 2) * sr) * (RET_DK ** -0.5)).astype(BF16)
    rv_ref[...] = mm(_O_RV, RV_W).astype(BF16)
    rg_ref[...] = _silu(mm(_O_RG, RV_W)).astype(BF16)
    cb_ref[...] = mm(_O_CB, CONV_WIDTH).astype(BF16)
    cu_ref[...] = (mm(_O_CC, CONV_WIDTH) * mm(_O_CX, CONV_WIDTH)).astype(BF16)

    cm = cm_ref[...]
    sm = sm_ref[...]
    qn = (_rms(mm(_O_QD, MLA_Q_RANK)) * qn_ref[...]).astype(BF16)
    for hd in range(MLA_HEADS):
        lo = hd * HEAD_PAD
        qa = _dot(qn, wuq_ref[:, lo:lo + HEAD_PAD])
        qm_ref[:, lo:lo + HEAD_PAD] = (qa * cm + _rot_half(qa, MLA_ROPE // 4) * sm).astype(BF16)

    kvn = (_rms(mm(_O_KVD, MLA_KV_RANK)) * kvn_ref[...]).astype(BF16)
    kr = mm(_O_KR, HEAD_PAD)
    kr = kr * cm + _rot_half(kr, MLA_ROPE // 4) * sm
    kn = _dot(kvn, wk_ref[...])
    for hd in range(MLA_HEADS):
        lo = hd * HEAD_PAD
        km_ref[:, lo:lo + HEAD_PAD] = (kn[:, lo:lo + HEAD_PAD] + kr).astype(BF16)
    vm_ref[...] = (_dot(kvn, wv_ref[...]) + ones_ref[...]).astype(BF16)

    for br in range(N_BRANCH):
        gt_ref[:, br * d_model:(br + 1) * d_model] = jax.nn.sigmoid(
            mm(_O_GT + br * d_model, d_model)).astype(BF16)


def _inproj_call(x_parts, a1, b1, tabs, wts, *, n_batch, nj, ncb):
    fused = len(x_parts) == 3
    d = x_parts[0].shape[-1]
    tm = ROW_BLOCK
    t_all = nj * tm
    rows = n_batch * t_all
    grp = math.gcd(n_batch, INPROJ_BATCH_GROUP)
    per_batch = lambda a: a.reshape(n_batch, -1, a.shape[-1])
    row = lambda j, b: (b, j, 0)
    mod = lambda j, b: (b, (j >= ncb).astype(jnp.int32), 0, 0)
    mod_spec = pl.BlockSpec((grp, None, 1, d), mod)
    as_mod = lambda a: a.reshape(n_batch, 2, 1, d)
    tab = lambda j, b: (j, 0)
    if fused:
        xm, f, g2 = x_parts
        lead = (per_batch(xm), per_batch(f), as_mod(g2))
        in_specs = [pl.BlockSpec((grp, tm, d), row), pl.BlockSpec((grp, tm, d // 2), row), mod_spec]
    else:
        lead = x_parts
        in_specs = [pl.BlockSpec((grp, tm, d), lambda j, b: (b, jnp.where(j < ncb, j, 0), 0)),
                    pl.BlockSpec((grp, tm, d), lambda j, b: (b, jnp.where(j < ncb, 0, j - ncb), 0))]
    in_specs += [mod_spec, mod_spec]
    in_specs += [pl.BlockSpec((tm, t.shape[1]), tab) for t in tabs]
    in_specs += [_const_spec(w.shape) for w in wts]
    widths = (RQ_W, RQ_W, RV_W, RV_W, CONV_WIDTH, CONV_WIDTH, MLA_W, MLA_W, MLA_W, N_BRANCH * d)
    out_shape = [jax.ShapeDtypeStruct((n_batch, t_all, w), BF16) for w in widths]
    out_specs = [pl.BlockSpec((grp, tm, w), row) for w in widths]
    out_shape.append(jax.ShapeDtypeStruct((n_batch, t_all, d), F32))
    out_specs.append(pl.BlockSpec((grp, tm, d), row))
    outs = pl.pallas_call(
        functools.partial(_inproj_body, d_model=d, fused=fused, ncb=ncb),
        out_shape=out_shape,
        grid=(nj, n_batch // grp),
        in_specs=in_specs,
        out_specs=out_specs,
        compiler_params=_params(2),
        name="in_proj",
    )(*lead, as_mod(a1), as_mod(b1), *tabs, *wts)
    return [o.reshape(rows, o.shape[-1]) for o in outs]


def _ret_direction(q_ref, k_ref, v_ref, y_ref, s_ref, dm_ref, xi_ref, zt_ref, cd_ref, chunk_order):
    states = [s_ref[hd] for hd in range(RET_HEADS)]
    for c in chunk_order:
        rows = slice(c * RET_CHUNK, (c + 1) * RET_CHUNK)
        q = q_ref[rows, :]
        k = k_ref[rows, :]
        v = v_ref[rows, :]
        qx = (q.astype(F32) * xi_ref[...]).astype(BF16)
        kz = (k.astype(F32) * zt_ref[...]).astype(BF16)
        for hd in range(RET_HEADS):
            ks = slice(hd * RET_DK, (hd + 1) * RET_DK)
            vs = slice(hd * RET_DV, (hd + 1) * RET_DV)
            vh = v[:, vs]
            sc = lax.dot_general(q[:, ks], k[:, ks], (((1,), (1,)), ((), ())),
                                 preferred_element_type=F32)
            inner = _dot((sc * dm_ref[hd]).astype(BF16), vh)
            cross = _dot(qx[:, ks], states[hd].astype(BF16))
            y_ref[rows, vs] = (inner + cross).astype(BF16)
            upd = lax.dot_general(kz[:, ks], vh, (((0,), (0,)), ((), ())),
                                  preferred_element_type=F32)
            states[hd] = cd_ref[hd] * states[hd] + upd
    for hd in range(RET_HEADS):
        s_ref[hd] = states[hd]


def _ret_body(qf_ref, kf_ref, vf_ref, qb_ref, kb_ref, vb_ref,
              dmf_ref, dmb_ref, xif_ref, ztf_ref, xib_ref, ztb_ref, cdf_ref, cdb_ref,
              yf_ref, yb_ref, sf_ref, sb_ref):
    @pl.when(pl.program_id(1) == 0)
    def _():
        sf_ref[...] = jnp.zeros_like(sf_ref)
        sb_ref[...] = jnp.zeros_like(sb_ref)

    n_chunks = ROW_BLOCK // RET_CHUNK
    for g in range(qf_ref.shape[0]):
        _ret_direction(qf_ref.at[g], kf_ref.at[g], vf_ref.at[g], yf_ref.at[g], sf_ref.at[g],
                       dmf_ref, xif_ref, ztf_ref, cdf_ref, range(n_chunks))
        _ret_direction(qb_ref.at[g], kb_ref.at[g], vb_ref.at[g], yb_ref.at[g], sb_ref.at[g],
                       dmb_ref, xib_ref, ztb_ref, cdb_ref, range(n_chunks - 1, -1, -1))


def _ret_call(rq, rk, rv, consts, *, n_batch, nj, ncb):
    rows = rq.shape[0]
    tm = ROW_BLOCK
    grp = math.gcd(n_batch, RET_BATCH_GROUP)
    fwd = lambda b, s: (b, s, 0)

    def bwd(b, s):
        return (b, jnp.where(s < ncb, ncb - 1 - s, nj - 1 - (s - ncb)), 0)

    per_batch = lambda a: a.reshape(n_batch, nj * tm, a.shape[1])
    specs = []
    for im in (fwd, bwd):
        specs += [pl.BlockSpec((grp, tm, RQ_W), im), pl.BlockSpec((grp, tm, RQ_W), im),
                  pl.BlockSpec((grp, tm, RV_W), im)]
    specs += [_const_spec(c.shape) for c in consts]
    q3, k3, v3 = per_batch(rq), per_batch(rk), per_batch(rv)
    yf, yb = pl.pallas_call(
        _ret_body,
        out_shape=[jax.ShapeDtypeStruct((n_batch, nj * tm, RV_W), BF16)] * 2,
        grid=(n_batch // grp, nj),
        in_specs=specs,
        out_specs=[pl.BlockSpec((grp, tm, RV_W), fwd), pl.BlockSpec((grp, tm, RV_W), bwd)],
        scratch_shapes=[pltpu.VMEM((grp, RET_HEADS, RET_DK, RET_DV), F32)] * 2,
        compiler_params=_params(2),
        name="retention",
    )(q3, k3, v3, q3, k3, v3, *consts)
    return yf.reshape(rows, RV_W), yb.reshape(rows, RV_W)


def _pick_tile(n, candidates):
    for c in candidates:
        if n % c == 0:
            return c
    raise ValueError(f"no tile for {n}")


def _attn_body(q_ref, k_ref, v_ref, o_ref, s_ref, *, n_ctx, n_all, ncb):
    tq = q_ref.shape[0]
    heads = [slice(hh * HEAD_PAD, (hh + 1) * HEAD_PAD) for hh in range(2)]

    def attend(n_keys, tk):
        nt = n_keys // tk
        qs = [q_ref[:, hs] for hs in heads]

        def qk(t, mrun):
            r0 = pl.multiple_of(t * tk, tk)
            out = []
            for hh, hs in enumerate(heads):
                s = lax.dot_general(qs[hh], k_ref[pl.ds(r0, tk), hs], (((1,), (1,)), ((), ())),
                                    preferred_element_type=F32)
                s_ref[hh, t, :, 0:tk] = s
                m = mrun[hh]
                for cc in range(tk // LANES):
                    m = jnp.maximum(m, s[:, cc * LANES:(cc + 1) * LANES])
                out.append(m)
            return tuple(out)

        mrun = lax.fori_loop(0, nt, qk, tuple(jnp.full((tq, LANES), -jnp.inf, F32) for _ in heads),
                             unroll=True)
        mrow = [jnp.max(m, axis=-1, keepdims=True) for m in mrun]

        def pv(t, accs):
            r0 = pl.multiple_of(t * tk, tk)
            out = []
            for hh, hs in enumerate(heads):
                p = jnp.exp2(s_ref[hh, t, :, 0:tk] - mrow[hh]).astype(BF16)
                out.append(accs[hh] + _dot(p, v_ref[pl.ds(r0, tk), hs]))
            return tuple(out)

        accs = lax.fori_loop(0, nt, pv, tuple(jnp.zeros((tq, HEAD_PAD), F32) for _ in heads),
                             unroll=True)
        o_ref[...] = jnp.concatenate([a[:, :MLA_V] / a[:, MLA_V:MLA_V + 1] for a in accs],
                                     axis=-1).astype(BF16)

    j = pl.program_id(2)

    @pl.when(j < ncb)
    def _():
        attend(n_ctx, _pick_tile(n_ctx, (256, 128)))

    @pl.when(j >= ncb)
    def _():
        attend(n_all, _pick_tile(n_all, ATTN_KEY_TILES))


def _attn_call(qm, km, vm, *, n_batch, nj, ncb, n_ctx):
    rows = qm.shape[0]
    tm = ROW_BLOCK
    t_all = nj * tm
    tk = _pick_tile(t_all, ATTN_KEY_TILES)
    qmap = lambda b, hp, j: (b * nj + j, hp)
    kmap = lambda b, hp, j: (b, hp)
    return pl.pallas_call(
        functools.partial(_attn_body, n_ctx=n_ctx, n_all=t_all, ncb=ncb),
        out_shape=jax.ShapeDtypeStruct((rows, MLA_O), BF16),
        grid=(n_batch, MLA_HEADS // 2, nj),
        in_specs=[pl.BlockSpec((tm, 2 * HEAD_PAD), qmap),
                  pl.BlockSpec((t_all, 2 * HEAD_PAD), kmap),
                  pl.BlockSpec((t_all, 2 * HEAD_PAD), kmap)],
        out_specs=pl.BlockSpec((tm, 2 * MLA_V), qmap),
        scratch_shapes=[pltpu.VMEM((2, t_all // tk, tm, tk), F32)],
        compiler_params=_params(3),
        name="mla_attention",
    )(qm, km, vm)


def _merge_body(yf_ref, yb_ref, rg_ref, cb_ref, cu_ref, cup_ref, cun_ref, om_ref, gt_ref, x_ref,
                g1_ref, a2_ref, b2_ref, gn_ref, cw_ref, wro_ref, wco_ref, wmo_ref, wout_ref,
                wrh_ref, wrl_ref, rb_ref, xmid_ref, h2_ref, rt_ref, gw_ref, lg_ref, *, nj, ncb, d_model):
    tm = x_ref.shape[1]

    @pl.when(jnp.logical_and(pl.program_id(0) == 0, pl.program_id(1) == 0))
    def _():
        lg_ref[...] = jnp.zeros_like(lg_ref)

    j = jnp.minimum(pl.program_id(1), nj - 1)
    seg_first = jnp.logical_or(j == 0, j == ncb)
    seg_last = jnp.logical_or(j == ncb - 1, j == nj - 1)

    for g in range(x_ref.shape[0]):
        cls, w_lo, w_hi = _route_rows(jnp.transpose(lg_ref[g])[:N_EXPERTS, :], rb_ref[...])
        rank, counts = _block_ranks(cls.astype(F32))
        rt_ref[g] = jnp.concatenate(
            [cls.astype(F32), rank, counts, jnp.zeros((ROUTE_ROWS - 3, tm), F32)], axis=0)
        gw_ref[g] = jnp.transpose(
            jnp.concatenate([w_lo, w_hi, jnp.zeros((LANES - 2, tm), F32)], axis=0))

        y = yf_ref[g].astype(F32) + yb_ref[g].astype(F32)
        yn = jnp.concatenate([_rms(y[:, hd * RET_DV:(hd + 1) * RET_DV])
                              for hd in range(RET_HEADS)], axis=-1) * gn_ref[...]
        y_ret = _dot((rg_ref[g].astype(F32) * yn).astype(BF16), wro_ref[...])

        u = cu_ref[g].astype(F32)
        ridx = lax.broadcasted_iota(jnp.int32, u.shape, 0)
        prev_row = cup_ref[g].astype(F32)[BF16_SUBLANES - 1:, :] * jnp.where(seg_first, 0.0, 1.0)
        next_row = cun_ref[g].astype(F32)[0:1, :] * jnp.where(seg_last, 0.0, 1.0)
        u_prev = jnp.where(ridx == 0, prev_row, pltpu.roll(u, 1, 0))
        u_next = jnp.where(ridx == tm - 1, next_row, pltpu.roll(u, tm - 1, 0))
        conv = u_prev * cw_ref[0:1, :] + u * cw_ref[1:2, :] + u_next * cw_ref[2:3, :]
        y_conv = _dot((cb_ref[g].astype(F32) * conv).astype(BF16), wco_ref[...])

        y_mla = _dot(om_ref[g], wmo_ref[...])

        merged = (gt_ref[g, :, 0:d_model].astype(F32) * y_ret
                  + gt_ref[g, :, d_model:2 * d_model].astype(F32) * y_conv
                  + gt_ref[g, :, 2 * d_model:3 * d_model].astype(F32) * y_mla)
        x_mid = x_ref[g] + g1_ref[g] * _dot(merged.astype(BF16), wout_ref[...])
        xmid_ref[g] = x_mid

        h2 = _rms(x_mid) * a2_ref[g] + b2_ref[g]
        h_hi, h_lo = _split_bf16(h2)
        lg_ref[g] = _dot(h_hi, wrh_ref[...]) + _dot(h_hi, wrl_ref[...]) + _dot(h_lo, wrh_ref[...])
        h2_ref[g] = _pack_pairs(h_hi)


def _top2_of4(v):
    def first_max(rows):
        best, idx = rows[0], jnp.zeros(rows[0].shape, jnp.int32)
        for e in range(1, len(rows)):
            better = rows[e] > best
            idx = jnp.where(better, e, idx)
            best = jnp.where(better, rows[e], best)
        return best, idx

    b1, i1 = first_max(v)
    b2, i2 = first_max([jnp.where(i1 == e, -jnp.inf, v[e]) for e in range(len(v))])
    return i1, i2, b1, b2


def _route_rows(logits_t, bias):
    scores = jax.nn.sigmoid(logits_t)
    biased = scores + bias
    row = lambda a, e: a[e:e + 1, :]
    best = None
    for g in range(N_GROUPS):
        v = [row(biased, g * EXPERTS_PER_GROUP + e) for e in range(EXPERTS_PER_GROUP)]
        i1, i2, b1, b2 = _top2_of4(v)
        cand = (b1 + b2, jnp.full(i1.shape, g, jnp.int32), i1, i2)
        if best is None:
            best = cand
        else:
            better = cand[0] > best[0]
            best = tuple(jnp.where(better, c, o) for c, o in zip(cand, best))
    _, g_sel, i1, i2 = best
    lo = jnp.minimum(i1, i2)
    hi = jnp.maximum(i1, i2)
    e_lo = g_sel * EXPERTS_PER_GROUP + lo
    e_hi = g_sel * EXPERTS_PER_GROUP + hi
    s_lo = jnp.zeros_like(best[0])
    s_hi = jnp.zeros_like(best[0])
    for e in range(N_EXPERTS):
        s_lo = jnp.where(e_lo == e, row(scores, e), s_lo)
        s_hi = jnp.where(e_hi == e, row(scores, e), s_hi)
    total = s_lo + s_hi
    pair_base = jnp.where(lo == 0, 0, jnp.where(lo == 1, 3, 5))
    cls = g_sel * len(PAIRS) + pair_base + (hi - lo - 1)
    return cls, s_lo / total, s_hi / total


def _block_ranks(cls_row):
    n = cls_row.shape[1]
    cls_col = jnp.transpose(jnp.broadcast_to(cls_row, (LANES, n)))[:, 0:1]
    ii = lax.broadcasted_iota(jnp.int32, (n, n), 0)
    jj = lax.broadcasted_iota(jnp.int32, (n, n), 1)
    earlier_same = jnp.logical_and(cls_col == cls_row, ii < jj)
    rank = jnp.sum(jnp.where(earlier_same, 1.0, 0.0), axis=0, keepdims=True)
    lane = lax.broadcasted_iota(jnp.int32, (1, n), 1)
    counts = jnp.zeros((1, n), F32)
    for c in range(N_CLASSES):
        cnt = jnp.sum(jnp.where(cls_row == c, 1.0, 0.0), axis=1, keepdims=True)
        counts = jnp.where(lane == c, cnt, counts)
    return rank, counts


def _merge_call(yf, yb, rg, cb, cu, om, gt, xa, g1, a2, b2, wts, *, n_batch, nj, ncb):
    rows, d = xa.shape
    tm = ROW_BLOCK
    t_all = nj * tm
    nb = rows // tm
    halo = BF16_SUBLANES
    per_blk = tm // halo
    grp = math.gcd(n_batch, MERGE_BATCH_GROUP)
    per_batch = lambda a: a.reshape(n_batch, t_all, a.shape[-1])
    as_mod = lambda a: a.reshape(n_batch, 2, 1, d)
    blk = lambda j: jnp.minimum(j, nj - 1)
    row = lambda b, j: (b, blk(j), 0)
    late = lambda b, j: (b, jnp.maximum(j - 1, 0), 0)
    mod = lambda b, j: (b, (blk(j) >= ncb).astype(jnp.int32), 0, 0)
    prev = lambda b, j: (b, jnp.maximum(blk(j) * per_blk - 1, 0), 0)
    nxt = lambda b, j: (b, jnp.minimum((blk(j) + 1) * per_blk, t_all // halo - 1), 0)
    wide = lambda w, im=row: pl.BlockSpec((grp, tm, w), im)
    mod_spec = pl.BlockSpec((grp, None, 1, d), mod)
    in_specs = [wide(RV_W), wide(RV_W), wide(RV_W), wide(CONV_WIDTH), wide(CONV_WIDTH),
                pl.BlockSpec((grp, halo, CONV_WIDTH), prev), pl.BlockSpec((grp, halo, CONV_WIDTH), nxt),
                wide(MLA_O), wide(N_BRANCH * d), wide(d), mod_spec, mod_spec, mod_spec]
    in_specs += [_const_spec(w.shape) for w in wts]
    cu3 = per_batch(cu)
    xmid, h2, route, gate_w = pl.pallas_call(
        functools.partial(_merge_body, nj=nj, ncb=ncb, d_model=d),
        out_shape=[jax.ShapeDtypeStruct((n_batch, t_all, d), F32),
                   jax.ShapeDtypeStruct((n_batch, t_all, d // 2), jnp.uint32),
                   jax.ShapeDtypeStruct((n_batch, nj * ROUTE_ROWS, tm), F32),
                   jax.ShapeDtypeStruct((n_batch, t_all, LANES), F32)],
        grid=(n_batch // grp, nj + 1),
        in_specs=in_specs,
        out_specs=[wide(d), wide(d // 2), pl.BlockSpec((grp, ROUTE_ROWS, tm), late),
                   wide(LANES, late)],
        scratch_shapes=[pltpu.VMEM((grp, tm, LANES), F32)],
        compiler_params=_params(2),
        name="merge_out_proj",
    )(per_batch(yf), per_batch(yb), per_batch(rg), per_batch(cb), cu3, cu3, cu3, per_batch(om),
      per_batch(gt), per_batch(xa), as_mod(g1), as_mod(a2), as_mod(b2), *wts)
    return (xmid.reshape(rows, d), h2.reshape(rows, d // 2), route.reshape(nb, ROUTE_ROWS, tm),
            gate_w.reshape(rows, LANES))


def _moe_body(ea_ref, eb_ref, nu_ref, h_ref, gw_ref, w1a_ref, w3a_ref, w2a_ref, w1b_ref, w3b_ref,
              w2b_ref, o_ref, w13a_s, w2a_s, w13b_s, w2b_s):
    t = pl.program_id(0)
    used = t < nu_ref[0]
    prev = jnp.maximum(t - 1, 0)

    def refresh(e_ref, w1_ref, w3_ref, w2_ref, w13_s, w2_s):
        @pl.when(jnp.logical_and(used, jnp.logical_or(t == 0, e_ref[t] != e_ref[prev])))
        def _():
            w13_s[:, :D_EXPERT] = w1_ref[...].astype(BF16)
            w13_s[:, D_EXPERT:] = w3_ref[...].astype(BF16)
            w2_s[...] = w2_ref[...].astype(BF16)

    refresh(ea_ref, w1a_ref, w3a_ref, w2a_ref, w13a_s, w2a_s)
    refresh(eb_ref, w1b_ref, w3b_ref, w2b_ref, w13b_s, w2b_s)

    @pl.when(used)
    def _():
        h = _unpack_pairs(h_ref[...]).astype(BF16)
        gw = gw_ref[...]

        def expert(w13_s, w2_s, wt):
            a = _dot(h, w13_s[...])
            act = _silu(a[:, :D_EXPERT]) * a[:, D_EXPERT:] * wt
            return _dot(act.astype(BF16), w2_s[...])

        o_ref[...] = _pack_pairs(expert(w13a_s, w2a_s, gw[:, 0:1])
                                 + expert(w13b_s, w2b_s, gw[:, 1:2]))

    @pl.when(jnp.logical_not(used))
    def _():
        o_ref[...] = jnp.zeros_like(o_ref)


def _moe_call(tile_ea, tile_eb, n_used, hs, gw, w1, w3, w2, layer):
    npad = hs.shape[0]
    d = w1.shape[2]
    tmo = MOE_TILE
    row = lambda t, ea, eb, nu: (t, 0)
    wa = lambda t, ea, eb, nu: (layer, ea[t], 0, 0)
    wb = lambda t, ea, eb, nu: (layer, eb[t], 0, 0)
    up = lambda im: pl.BlockSpec((None, None, d, D_EXPERT), im)
    down = lambda im: pl.BlockSpec((None, None, D_EXPERT, d), im)
    grid_spec = pltpu.PrefetchScalarGridSpec(
        num_scalar_prefetch=3,
        grid=(npad // tmo,),
        in_specs=[pl.BlockSpec((tmo, d // 2), row), pl.BlockSpec((tmo, LANES), row),
                  up(wa), up(wa), down(wa), up(wb), up(wb), down(wb)],
        out_specs=pl.BlockSpec((tmo, d // 2), row),
        scratch_shapes=[pltpu.VMEM((d, 2 * D_EXPERT), BF16), pltpu.VMEM((D_EXPERT, d), BF16)] * 2)
    return pl.pallas_call(
        _moe_body,
        out_shape=jax.ShapeDtypeStruct((npad, d // 2), jnp.uint32),
        grid_spec=grid_spec,
        compiler_params=_params(1),
        name="moe_experts",
    )(tile_ea, tile_eb, n_used, hs, gw, w1, w3, w2, w1, w3, w2)


def _dispatch(route, n_tok):
    tmo = MOE_TILE
    n_tiles = n_tok // tmo + N_CLASSES
    npad = n_tiles * tmo
    cls = route[:, 0, :].astype(jnp.int32)
    rank = route[:, 1, :].astype(jnp.int32)
    counts = route[:, 2, :N_CLASSES].astype(jnp.int32)
    tiles_per = (jnp.sum(counts, axis=0) + tmo - 1) // tmo
    tile_end = jnp.cumsum(tiles_per)
    offs = (tile_end - tiles_per) * tmo
    block_base = offs[None, :] + jnp.cumsum(counts, axis=0) - counts
    classes = jnp.arange(N_CLASSES, dtype=jnp.int32)
    base = jnp.sum(jnp.where(cls[:, :, None] == classes, block_base[:, None, :], 0), axis=-1)
    dest = (base + rank).reshape(-1)
    tile_ids = jnp.arange(n_tiles, dtype=jnp.int32)
    tile_cls = jnp.sum((tile_end[None, :] <= tile_ids[:, None]).astype(jnp.int32), axis=1)
    tile_cls = jnp.minimum(tile_cls, N_CLASSES - 1)
    pa = jnp.asarray([p[0] for p in PAIRS], jnp.int32)
    pb = jnp.asarray([p[1] for p in PAIRS], jnp.int32)
    group = (tile_cls // len(PAIRS)) * EXPERTS_PER_GROUP
    tile_ea = group + pa[tile_cls % len(PAIRS)]
    tile_eb = group + pb[tile_cls % len(PAIRS)]
    n_used = tile_end[-1:].astype(jnp.int32)
    return dest, npad, tile_ea, tile_eb, n_used


def _sc_mesh():
    return plsc.VectorSubcoreMesh(core_axis_name="c", subcore_axis_name="s",
                                  num_cores=SC_CORES, num_subcores=SC_SUBCORES)


def _sc_chunks(idx):
    assert idx.shape[0] % (SC_WORKERS * GATHER_CHUNK) == 0
    return idx.reshape(SC_WORKERS, -1, GATHER_CHUNK)


def _sc_pipeline(n_chunks, fetch, put):
    assert n_chunks % 2 == 0
    fetch(0, 0).start()

    @pl.loop(0, n_chunks, step=2)
    def _(c0):
        for b in (0, 1):
            c = c0 + b
            fetch(c, b).wait()

            @pl.when(c >= 1)
            def _():
                put(c - 1, 1 - b).wait()

            @pl.when(c + 1 < n_chunks)
            def _():
                fetch(c + 1, 1 - b).start()

            put(c, b).start()

    put(n_chunks - 1, 1).wait()


def _sc_scratch(n_chunks, d, dtype):
    return [pltpu.VMEM((n_chunks, GATHER_CHUNK), jnp.int32),
            pltpu.VMEM((2, GATHER_CHUNK, d), dtype),
            pltpu.SemaphoreType.DMA((2,)), pltpu.SemaphoreType.DMA((2,))]


def _sc_gather_rows(table, idx):
    d = table.shape[1]
    idx3 = _sc_chunks(idx)
    n_chunks = idx3.shape[1]
    assert table.dtype.itemsize == 4

    @functools.partial(
        pl.kernel, mesh=_sc_mesh(),
        out_type=jax.ShapeDtypeStruct((idx.shape[0], d), table.dtype),
        scratch_types=_sc_scratch(n_chunks, d, table.dtype),
        name="sc_row_gather")
    def gather(table_hbm, idx_hbm, out_hbm, idx_v, rows_v, sem_in, sem_out):
        worker = lax.axis_index("s") * SC_CORES + lax.axis_index("c")
        pltpu.sync_copy(idx_hbm.at[worker], idx_v)

        def fetch(c, b):
            return pltpu.make_async_copy(table_hbm.at[idx_v.at[c]], rows_v.at[b], sem_in.at[b])

        def put(c, b):
            off = (worker * n_chunks + c) * GATHER_CHUNK
            return pltpu.make_async_copy(rows_v.at[b], out_hbm.at[pl.ds(off, GATHER_CHUNK)],
                                         sem_out.at[b])

        _sc_pipeline(n_chunks, fetch, put)

    return gather(table, idx3)


def _sc_scatter_rows(rows, idx, n_out):
    d = rows.shape[1]
    idx3 = _sc_chunks(idx)
    n_chunks = idx3.shape[1]
    assert rows.dtype.itemsize == 4

    @functools.partial(
        pl.kernel, mesh=_sc_mesh(),
        out_type=jax.ShapeDtypeStruct((n_out, d), rows.dtype),
        scratch_types=_sc_scratch(n_chunks, d, rows.dtype),
        name="sc_row_scatter")
    def scatter(rows_hbm, idx_hbm, out_hbm, idx_v, rows_v, sem_in, sem_out):
        worker = lax.axis_index("s") * SC_CORES + lax.axis_index("c")
        pltpu.sync_copy(idx_hbm.at[worker], idx_v)

        def fetch(c, b):
            off = (worker * n_chunks + c) * GATHER_CHUNK
            return pltpu.make_async_copy(rows_hbm.at[pl.ds(off, GATHER_CHUNK)], rows_v.at[b],
                                         sem_in.at[b])

        def put(c, b):
            return pltpu.make_async_copy(rows_v.at[b], out_hbm.at[idx_v.at[c]], sem_out.at[b])

        _sc_pipeline(n_chunks, fetch, put)

    return scatter(rows, idx3)


def _final_body(x_ref, f_ref, g2_ref, fn_ref, o_ref):
    o_ref[...] = _rms(x_ref[...] + g2_ref[...] * _unpack_pairs(f_ref[...])) * fn_ref[...]


def _final_call(xmid, fg, g2, final_norm, *, n_batch, nj, ncb):
    rows, d = xmid.shape
    tm = ROW_BLOCK
    njl = nj - ncb
    src = lambda b, j: (b * nj + ncb + j, 0)
    return pl.pallas_call(
        _final_body,
        out_shape=jax.ShapeDtypeStruct((n_batch * njl * tm, d), F32),
        grid=(n_batch, njl),
        in_specs=[pl.BlockSpec((tm, d), src), pl.BlockSpec((tm, d // 2), src),
                  pl.BlockSpec((None, 1, d), lambda b, j: (2 * b + 1, 0, 0)),
                  _const_spec((1, d))],
        out_specs=pl.BlockSpec((tm, d), lambda b, j: (b * njl + j, 0)),
        compiler_params=_params(2),
        name="final_norm",
    )(xmid, fg, g2, final_norm.reshape(1, d))


def _split_cols(w, sizes):
    out, off = [], 0
    for s in sizes:
        out.append(w[:, off:off + s])
        off += s
    out.append(w[:, off:])
    return out


def _layer_weights(w_in, w_uq, w_ukv, q_norm, kv_norm):
    wq, wk, wv, wg, wcb, wcc, wcx, wqd, wkvd, wkr, wgate = _split_cols(w_in, IN_SIZES)
    rope_lanes = lambda w: jnp.pad(w, ((0, 0), (MLA_NOPE, HEAD_PAD - MLA_NOPE - MLA_ROPE)))
    w_ext = jnp.concatenate(
        [wq, wk, wv, wg, wcb, wcc, wcx, wqd, wkvd, rope_lanes(wkr), wgate], axis=1).astype(BF16)

    uq = w_uq.reshape(MLA_Q_RANK, MLA_HEADS, MLA_NOPE + MLA_ROPE)
    tail = HEAD_PAD - MLA_NOPE - MLA_ROPE
    uq_pad = jnp.pad(uq, ((0, 0), (0, 0), (0, tail)))
    wuq = uq_pad.reshape(MLA_Q_RANK, MLA_W).astype(BF16)

    ukv = w_ukv.reshape(MLA_KV_RANK, MLA_HEADS, MLA_NOPE + MLA_V)
    wk_up = jnp.pad(ukv[..., :MLA_NOPE], ((0, 0), (0, 0), (0, HEAD_PAD - MLA_NOPE)))
    wk_up = wk_up.reshape(MLA_KV_RANK, MLA_W).astype(BF16)
    wv_pad = jnp.pad(ukv[..., MLA_NOPE:], ((0, 0), (0, 0), (0, HEAD_PAD - MLA_V)))
    wv_pad = wv_pad.reshape(MLA_KV_RANK, MLA_W).astype(BF16)
    ones_row = jnp.zeros((MLA_HEADS, HEAD_PAD), F32).at[:, MLA_V].set(1.0).reshape(1, MLA_W)
    q_gain = q_norm.astype(F32) * ((MLA_NOPE + MLA_ROPE) ** -0.5 * math.log2(math.e))
    return (w_ext, wuq, wk_up, wv_pad, q_gain.reshape(1, -1),
            kv_norm.reshape(1, -1).astype(F32), ones_row)


def _rotary_tables(n_ctx, n_lat):
    n_all = n_ctx + n_lat
    row = lax.broadcasted_iota(jnp.int32, (n_all, LANES), 0)
    lane = lax.broadcasted_iota(jnp.int32, (n_all, LANES), 1)
    is_ctx = row < n_ctx
    pos = row - n_ctx
    grid_row = pos // GRID_W
    grid_col = pos - grid_row * GRID_W

    def inv_freq(idx, half):
        return ROPE_BASE ** (-idx.astype(F32) / half)

    def table(ang, first_half, live):
        cos = jnp.where(is_ctx, 1.0, jnp.cos(ang))
        sin = jnp.where(is_ctx, 0.0, jnp.where(first_half, -jnp.sin(ang), jnp.sin(ang)))
        return jnp.where(live, cos, 0.0), jnp.where(live, sin, 0.0)

    half = RET_DK // 2
    ang = pos.astype(F32) * inv_freq(lane % half, half)
    cr, sr = table(ang, (lane % RET_DK) < half, True)

    quarter = MLA_ROPE // 4
    rl = lane - MLA_NOPE
    coord = jnp.where(rl < MLA_ROPE // 2, grid_row, grid_col)
    ang = coord.astype(F32) * inv_freq(rl % quarter, quarter)
    cm, sm = table(ang, (rl % (2 * quarter)) < quarter,
                   jnp.logical_and(rl >= 0, rl < MLA_ROPE))
    cm = jnp.where(rl < 0, 1.0, cm)
    return cr, sr, cm, sm


def _retention_consts(ret_decay):
    log_gf = jax.nn.log_sigmoid(ret_decay[0].astype(F32))
    log_gb = jax.nn.log_sigmoid(ret_decay[1].astype(F32))
    idx = jnp.arange(RET_CHUNK, dtype=F32)
    rel = idx[:, None] - idx[None, :]
    dm_f = jnp.where(rel >= 0, jnp.exp(log_gf[:, None, None] * jnp.maximum(rel, 0.0)[None]), 0.0)
    dm_b = jnp.where(rel < 0, jnp.exp(log_gb[:, None, None] * jnp.maximum(-rel, 0.0)[None]), 0.0)

    def lanes(t):
        return jnp.repeat(t, RET_DK, axis=1)

    xi_f = lanes(jnp.exp(log_gf[None, :] * (idx + 1.0)[:, None]))
    zt_f = lanes(jnp.exp(log_gf[None, :] * (RET_CHUNK - 1 - idx)[:, None]))
    xi_b = lanes(jnp.exp(log_gb[None, :] * (RET_CHUNK - idx)[:, None]))
    zt_b = lanes(jnp.exp(log_gb[None, :] * idx[:, None]))
    cd = lambda lg: jnp.broadcast_to(jnp.exp(lg * RET_CHUNK)[:, None, None], (RET_HEADS, 1, RET_DV))
    return dm_f, dm_b, xi_f, zt_f, xi_b, zt_b, cd(log_gf), cd(log_gb)


def kernel(x, c, ctx, c_ctx, w_ada, b_ada, norm1, norm2, w_in, ret_decay, ret_gn, w_ret_o, conv_w,
           w_conv_o, mla_q_norm, w_uq, mla_kv_norm, w_ukv, w_mla_o, w_out, w_router, router_bias,
           w1, w3, w2, final_norm):
    n_batch, n_lat, d = x.shape
    n_ctx = ctx.shape[1]
    depth = w_ada.shape[0]
    t_all = n_ctx + n_lat
    assert n_ctx % ROW_BLOCK == 0 and n_lat % ROW_BLOCK == 0 and n_lat % GRID_W == 0
    nj = t_all // ROW_BLOCK
    ncb = n_ctx // ROW_BLOCK
    n_tok = n_batch * t_all
    assert n_tok % MOE_TILE == 0
    geom = dict(n_batch=n_batch, nj=nj, ncb=ncb)

    cc = jnp.concatenate([c, c_ctx[None, :]], axis=0)
    cc = jnp.pad(cc, ((0, -cc.shape[0] % 8), (0, 0)))
    mod = _ada_call(cc, w_ada, b_ada)[:, :n_batch + 1].reshape(depth, n_batch + 1, N_MOD, d)
    pick = jnp.stack([jnp.full((n_batch,), n_batch, jnp.int32),
                      jnp.arange(n_batch, dtype=jnp.int32)], axis=1).reshape(-1)
    mod = mod[:, pick]

    tabs = _rotary_tables(n_ctx, n_lat)
    wr_hi, wr_lo = _split_bf16(jnp.pad(w_router.astype(F32), ((0, 0), (0, LANES - N_EXPERTS))))
    rbias = router_bias.astype(F32).reshape(N_EXPERTS, 1)

    out = None
    x_parts = (ctx, x)
    for l in range(depth):
        m = mod[l]
        rowvec = lambda v: v.reshape(2 * n_batch, 1, d)
        a1 = rowvec(norm1[l][None, :] * (1.0 + m[:, 1]))
        b1 = rowvec(m[:, 0])
        g1 = rowvec(m[:, 2])
        a2 = rowvec(norm2[l][None, :] * (1.0 + m[:, 4]))
        b2 = rowvec(m[:, 3])
        g2 = rowvec(m[:, 5])

        wts = _layer_weights(w_in[l], w_uq[l], w_ukv[l], mla_q_norm[l], mla_kv_norm[l])
        proj = _inproj_call(x_parts, a1, b1, tabs, wts, **geom)
        rq, rk, rv, rg, cb, cu, qm, km, vm, gt, xa = proj

        yf, yb = _ret_call(rq, rk, rv, _retention_consts(ret_decay[l]), **geom)
        om = _attn_call(qm, km, vm, n_ctx=n_ctx, **geom)

        merge_wts = (ret_gn[l].reshape(1, -1).astype(F32), conv_w[l].T.astype(F32),
                     w_ret_o[l].astype(BF16), w_conv_o[l].astype(BF16), w_mla_o[l].astype(BF16),
                     w_out[l].astype(BF16), wr_hi, wr_lo, rbias)
        xmid, h2, route, gate_w = _merge_call(yf, yb, rg, cb, cu, om, gt, xa, g1, a2, b2, merge_wts,
                                      **geom)
        dest, npad, tile_ea, tile_eb, n_used = _dispatch(route, n_tok)
        hs = _sc_scatter_rows(h2, dest, npad)
        gw = _sc_scatter_rows(gate_w, dest, npad)
        f_sorted = _moe_call(tile_ea, tile_eb, n_used, hs, gw, w1, w3, w2, l)
        fg = _sc_gather_rows(f_sorted, dest)

        if l < depth - 1:
            x_parts = (xmid, fg, g2)
        else:
            out = _final_call(xmid, fg, g2, final_norm, **geom)
    return out.reshape(n_batch, n_lat, d)
```

```python
import functools
import math

import jax
import jax.numpy as jnp
from jax import lax
from jax.experimental import pallas as pl
from jax.experimental.pallas import tpu as pltpu
from jax.experimental.pallas import tpu_sc as plsc

F32 = jnp.float32
BF16 = jnp.bfloat16

GRID_W = 64
RMS_EPS = 1e-6
ROPE_BASE = 10000.0
N_MOD = 6
RET_HEADS = 4
RET_DK = 64
RET_DV = 128
RET_CHUNK = 256
CONV_WIDTH = 512
MLA_HEADS = 8
MLA_Q_RANK = 384
MLA_KV_RANK = 256
MLA_NOPE = 64
MLA_ROPE = 32
MLA_V = 64
N_BRANCH = 3
N_EXPERTS = 16
N_GROUPS = 4
EXPERTS_PER_GROUP = N_EXPERTS // N_GROUPS
D_EXPERT = 512
IN_SIZES = (RET_HEADS * RET_DK, RET_HEADS * RET_DK, RET_HEADS * RET_DV, RET_HEADS * RET_DV,
            CONV_WIDTH, CONV_WIDTH, CONV_WIDTH, MLA_Q_RANK, MLA_KV_RANK, MLA_ROPE)

LANES = 128
BF16_SUBLANES = 16
VMEM_LIMIT = 56 * 1024 * 1024
SC_CORES = 2
SC_SUBCORES = 16
SC_WORKERS = SC_CORES * SC_SUBCORES

HEAD_PAD = LANES
ROW_BLOCK = 256
MOE_TILE = 512
ATTN_KEY_TILES = (768, 512, 256, 128)
ROUTE_ROWS = 8
RET_BATCH_GROUP = 4
INPROJ_BATCH_GROUP = 2
MERGE_BATCH_GROUP = 2
FINAL_BATCH_GROUP = 4
GATHER_CHUNK = 96
PAIRS = [(a, b) for a in range(EXPERTS_PER_GROUP) for b in range(a + 1, EXPERTS_PER_GROUP)]
N_CLASSES = N_GROUPS * len(PAIRS)

RQ_W = RET_HEADS * RET_DK
RV_W = RET_HEADS * RET_DV
MLA_W = MLA_HEADS * HEAD_PAD
MLA_O = MLA_HEADS * MLA_V


def _const_spec(shape):
    nd = len(shape)
    return pl.BlockSpec(shape, lambda *_: (0,) * nd, pipeline_mode=pl.Buffered(1))


def _params(n_axes):
    return pltpu.CompilerParams(dimension_semantics=("arbitrary",) * n_axes,
                                vmem_limit_bytes=VMEM_LIMIT)


def _dot(a, b):
    return jnp.dot(a, b, preferred_element_type=F32)


def _split_bf16(a):
    hi = a.astype(BF16)
    lo = (a - hi.astype(F32)).astype(BF16)
    return hi, lo


def _silu(v):
    return v * jax.nn.sigmoid(v)


def _rms(v):
    return v * lax.rsqrt(jnp.mean(v * v, axis=-1, keepdims=True) + RMS_EPS)


def _pack_pairs(v):
    half = v.shape[1] // 2
    bits = lambda t: lax.bitcast_convert_type(t.astype(BF16).astype(F32), jnp.uint32)
    return (bits(v[:, :half]) & jnp.uint32(0xFFFF0000)) | (bits(v[:, half:]) >> 16)


def _unpack_pairs(u):
    hi = lax.bitcast_convert_type(u & jnp.uint32(0xFFFF0000), F32)
    lo = lax.bitcast_convert_type(u << 16, F32)
    return jnp.concatenate([hi, lo], axis=1)


def _ada_body(c_ref, w_ref, b_ref, o_ref):
    a_hi, a_lo = _split_bf16(_silu(c_ref[...]))
    w_hi, w_lo = _split_bf16(w_ref[...])
    o_ref[...] = _dot(a_hi, w_hi) + _dot(a_hi, w_lo) + _dot(a_lo, w_hi) + b_ref[...]


def _ada_call(cc, w_ada, b_ada):
    depth, d, nm = w_ada.shape
    rows = cc.shape[0]
    cb = nm // 4
    return pl.pallas_call(
        _ada_body,
        out_shape=jax.ShapeDtypeStruct((depth, rows, nm), F32),
        grid=(depth, nm // cb),
        in_specs=[pl.BlockSpec((rows, d), lambda l, n: (0, 0)),
                  pl.BlockSpec((None, d, cb), lambda l, n: (l, 0, n)),
                  pl.BlockSpec((None, 1, cb), lambda l, n: (l, 0, n))],
        out_specs=pl.BlockSpec((None, rows, cb), lambda l, n: (l, 0, n)),
        compiler_params=_params(2),
        name="ada_mod",
    )(cc, w_ada, b_ada.reshape(depth, 1, nm))


_O_RQ = 0
_O_RK = _O_RQ + RQ_W
_O_RV = _O_RK + RQ_W
_O_RG = _O_RV + RV_W
_O_CB = _O_RG + RV_W
_O_CC = _O_CB + CONV_WIDTH
_O_CX = _O_CC + CONV_WIDTH
_O_QD = _O_CX + CONV_WIDTH
_O_KVD = _O_QD + MLA_Q_RANK
_O_KR = _O_KVD + MLA_KV_RANK
_O_GT = _O_KR + LANES


def _rot_half(v, half):
    width = v.shape[1]
    lane = lax.broadcasted_iota(jnp.int32, v.shape, 1)
    first = (lane % (2 * half)) < half
    return jnp.where(first, pltpu.roll(v, width - half, 1), pltpu.roll(v, half, 1))


def _inproj_body(*refs, d_model, fused, ncb):
    n_lead = 3 if fused else 2
    lead, refs = refs[:n_lead], refs[n_lead:]
    shared, outs = refs[:13], refs[13:]
    a1_ref, b1_ref = shared[:2]
    for g in range(a1_ref.shape[0]):
        if fused:
            xm_ref, f_ref, g2_ref = lead
            x = xm_ref[g] + g2_ref[g] * _unpack_pairs(f_ref[g])
        else:
            ctx_ref, lat_ref = lead
            x = jnp.where(pl.program_id(0) < ncb, ctx_ref[g], lat_ref[g])
        _inproj_block(x, a1_ref[g], b1_ref[g], *shared[2:], *[o.at[g] for o in outs],
                      d_model=d_model)


def _inproj_block(x, a1, b1, cr_ref, sr_ref, cm_ref, sm_ref, w_ref, wuq_ref, wk_ref, wv_ref, qn_ref,
                  kvn_ref, ones_ref, rq_ref, rk_ref, rv_ref, rg_ref, cb_ref, cu_ref, qm_ref, km_ref,
                  vm_ref, gt_ref, xo_ref, *, d_model):
    xo_ref[...] = x
    h = (_rms(x) * a1 + b1).astype(BF16)

    def mm(off, width):
        return _dot(h, w_ref[:, off:off + width])

    cr = jnp.concatenate([cr_ref[...]] * (RQ_W // cr_ref.shape[1]), axis=1)
    sr = jnp.concatenate([sr_ref[...]] * (RQ_W // sr_ref.shape[1]), axis=1)
    q = mm(_O_RQ, RQ_W)
    rq_ref[...] = (q * cr + _rot_half(q, RET_DK // 2) * sr).astype(BF16)
    k = mm(_O_RK, RQ_W)
    rk_ref[...] = ((k * cr + _rot_half(k, RET_DK // 2) * sr) * (RET_DK ** -0.5)).astype(BF16)
    rv_ref[...] = mm(_O_RV, RV_W).astype(BF16)
    rg_ref[...] = _silu(mm(_O_RG, RV_W)).astype(BF16)
    cb_ref[...] = mm(_O_CB, CONV_WIDTH).astype(BF16)
    cu_ref[...] = (mm(_O_CC, CONV_WIDTH) * mm(_O_CX, CONV_WIDTH)).astype(BF16)

    cm = cm_ref[...]
    sm = sm_ref[...]
    qn = (_rms(mm(_O_QD, MLA_Q_RANK)) * qn_ref[...]).astype(BF16)
    for hd in range(MLA_HEADS):
        lo = hd * HEAD_PAD
        qa = _dot(qn, wuq_ref[:, lo:lo + HEAD_PAD])
        qm_ref[:, lo:lo + HEAD_PAD] = (qa * cm + _rot_half(qa, MLA_ROPE // 4) * sm).astype(BF16)

    kvn = (_rms(mm(_O_KVD, MLA_KV_RANK)) * kvn_ref[...]).astype(BF16)
    kr = mm(_O_KR, HEAD_PAD)
    kr = kr * cm + _rot_half(kr, MLA_ROPE // 4) * sm
    kn = _dot(kvn, wk_ref[...])
    for hd in range(MLA_HEADS):
        lo = hd * HEAD_PAD
        km_ref[:, lo:lo + HEAD_PAD] = (kn[:, lo:lo + HEAD_PAD] + kr).astype(BF16)
    vm_ref[...] = (_dot(kvn, wv_ref[...]) + ones_ref[...]).astype(BF16)

    for br in range(N_BRANCH):
        gt_ref[:, br * d_model:(br + 1) * d_model] = jax.nn.sigmoid(
            mm(_O_GT + br * d_model, d_model)).astype(BF16)


def _inproj_call(x_parts, a1, b1, tabs, wts, *, n_batch, nj, ncb):
    fused = len(x_parts) == 3
    d = x_parts[0].shape[-1]
    tm = ROW_BLOCK
    t_all = nj * tm
    rows = n_batch * t_all
    grp = math.gcd(n_batch, INPROJ_BATCH_GROUP)
    per_batch = lambda a: a.reshape(n_batch, -1, a.shape[-1])
    row = lambda j, b: (b, j, 0)
    mod = lambda j, b: (b, (j >= ncb).astype(jnp.int32), 0, 0)
    mod_spec = pl.BlockSpec((grp, None, 1, d), mod)
    as_mod = lambda a: a.reshape(n_batch, 2, 1, d)
    tab = lambda j, b: (j, 0)
    if fused:
        xm, f, g2 = x_parts
        lead = (per_batch(xm), per_batch(f), as_mod(g2))
        in_specs = [pl.BlockSpec((grp, tm, d), row), pl.BlockSpec((grp, tm, d // 2), row), mod_spec]
    else:
        lead = x_parts
        in_specs = [pl.BlockSpec((grp, tm, d), lambda j, b: (b, jnp.where(j < ncb, j, 0), 0)),
                    pl.BlockSpec((grp, tm, d), lambda j, b: (b, jnp.where(j < ncb, 0, j - ncb), 0))]
    in_specs += [mod_spec, mod_spec]
    in_specs += [pl.BlockSpec((tm, t.shape[1]), tab) for t in tabs]
    in_specs += [_const_spec(w.shape) for w in wts]
    widths = (RQ_W, RQ_W, RV_W, RV_W, CONV_WIDTH, CONV_WIDTH, MLA_W, MLA_W, MLA_W, N_BRANCH * d)
    out_shape = [jax.ShapeDtypeStruct((n_batch, t_all, w), BF16) for w in widths]
    out_specs = [pl.BlockSpec((grp, tm, w), row) for w in widths]
    out_shape.append(jax.ShapeDtypeStruct((n_batch, t_all, d), F32))
    out_specs.append(pl.BlockSpec((grp, tm, d), row))
    outs = pl.pallas_call(
        functools.partial(_inproj_body, d_model=d, fused=fused, ncb=ncb),
        out_shape=out_shape,
        grid=(nj, n_batch // grp),
        in_specs=in_specs,
        out_specs=out_specs,
        compiler_params=_params(2),
        name="in_proj",
    )(*lead, as_mod(a1), as_mod(b1), *tabs, *wts)
    return [o.reshape(rows, o.shape[-1]) for o in outs]


def _ret_direction(q_ref, k_ref, v_ref, y_ref, s_ref, dm_ref, xi_ref, zt_ref, cd_ref, chunk_order):
    states = [s_ref[hd] for hd in range(RET_HEADS)]
    for c in chunk_order:
        rows = slice(c * RET_CHUNK, (c + 1) * RET_CHUNK)
        q = q_ref[rows, :]
        k = k_ref[rows, :]
        v = v_ref[rows, :]
        qx = (q.astype(F32) * xi_ref[...]).astype(BF16)
        kz = (k.astype(F32) * zt_ref[...]).astype(BF16)
        for hd in range(RET_HEADS):
            ks = slice(hd * RET_DK, (hd + 1) * RET_DK)
            vs = slice(hd * RET_DV, (hd + 1) * RET_DV)
            vh = v[:, vs]
            sc = lax.dot_general(q[:, ks], k[:, ks], (((1,), (1,)), ((), ())),
                                 preferred_element_type=F32)
            inner = _dot((sc * dm_ref[hd]).astype(BF16), vh)
            cross = _dot(qx[:, ks], states[hd].astype(BF16))
            y_ref[rows, vs] = (inner + cross).astype(BF16)
            upd = lax.dot_general(kz[:, ks], vh, (((0,), (0,)), ((), ())),
                                  preferred_element_type=F32)
            states[hd] = cd_ref[hd] * states[hd] + upd
    for hd in range(RET_HEADS):
        s_ref[hd] = states[hd]


def _ret_body(qf_ref, kf_ref, vf_ref, qb_ref, kb_ref, vb_ref,
              dmf_ref, dmb_ref, xif_ref, ztf_ref, xib_ref, ztb_ref, cdf_ref, cdb_ref,
              yf_ref, yb_ref, sf_ref, sb_ref):
    @pl.when(pl.program_id(1) == 0)
    def _():
        sf_ref[...] = jnp.zeros_like(sf_ref)
        sb_ref[...] = jnp.zeros_like(sb_ref)

    n_chunks = ROW_BLOCK // RET_CHUNK
    for g in range(qf_ref.shape[0]):
        _ret_direction(qf_ref.at[g], kf_ref.at[g], vf_ref.at[g], yf_ref.at[g], sf_ref.at[g],
                       dmf_ref, xif_ref, ztf_ref, cdf_ref, range(n_chunks))
        _ret_direction(qb_ref.at[g], kb_ref.at[g], vb_ref.at[g], yb_ref.at[g], sb_ref.at[g],
                       dmb_ref, xib_ref, ztb_ref, cdb_ref, range(n_chunks - 1, -1, -1))


def _ret_call(rq, rk, rv, consts, *, n_batch, nj, ncb):
    rows = rq.shape[0]
    tm = ROW_BLOCK
    grp = math.gcd(n_batch, RET_BATCH_GROUP)
    fwd = lambda b, s: (b, s, 0)

    def bwd(b, s):
        return (b, jnp.where(s < ncb, ncb - 1 - s, nj - 1 - (s - ncb)), 0)

    per_batch = lambda a: a.reshape(n_batch, nj * tm, a.shape[1])
    specs = []
    for im in (fwd, bwd):
        specs += [pl.BlockSpec((grp, tm, RQ_W), im), pl.BlockSpec((grp, tm, RQ_W), im),
                  pl.BlockSpec((grp, tm, RV_W), im)]
    specs += [_const_spec(c.shape) for c in consts]
    q3, k3, v3 = per_batch(rq), per_batch(rk), per_batch(rv)
    yf, yb = pl.pallas_call(
        _ret_body,
        out_shape=[jax.ShapeDtypeStruct((n_batch, nj * tm, RV_W), BF16)] * 2,
        grid=(n_batch // grp, nj),
        in_specs=specs,
        out_specs=[pl.BlockSpec((grp, tm, RV_W), fwd), pl.BlockSpec((grp, tm, RV_W), bwd)],
        scratch_shapes=[pltpu.VMEM((grp, RET_HEADS, RET_DK, RET_DV), F32)] * 2,
        compiler_params=_params(2),
        name="retention",
    )(q3, k3, v3, q3, k3, v3, *consts)
    return yf.reshape(rows, RV_W), yb.reshape(rows, RV_W)


def _pick_tile(n, candidates):
    for c in candidates:
        if n % c == 0:
            return c
    raise ValueError(f"no tile for {n}")


def _attn_body(q_ref, k_ref, v_ref, o_ref, s_ref, *, n_ctx, n_all, ncb):
    tq = q_ref.shape[0]
    heads = [slice(hh * HEAD_PAD, (hh + 1) * HEAD_PAD) for hh in range(2)]

    def attend(n_keys, tk):
        nt = n_keys // tk
        qs = [q_ref[:, hs] for hs in heads]

        def qk(t, mrun):
            r0 = pl.multiple_of(t * tk, tk)
            out = []
            for hh, hs in enumerate(heads):
                s = lax.dot_general(qs[hh], k_ref[pl.ds(r0, tk), hs], (((1,), (1,)), ((), ())),
                                    preferred_element_type=F32)
                s_ref[hh, t, :, 0:tk] = s
                m = mrun[hh]
                for cc in range(tk // LANES):
                    m = jnp.maximum(m, s[:, cc * LANES:(cc + 1) * LANES])
                out.append(m)
            return tuple(out)

        mrun = lax.fori_loop(0, nt, qk, tuple(jnp.full((tq, LANES), -jnp.inf, F32) for _ in heads),
                             unroll=True)
        mrow = [jnp.max(m, axis=-1, keepdims=True) for m in mrun]

        def pv(t, accs):
            r0 = pl.multiple_of(t * tk, tk)
            out = []
            for hh, hs in enumerate(heads):
                p = jnp.exp2(s_ref[hh, t, :, 0:tk] - mrow[hh]).astype(BF16)
                out.append(accs[hh] + _dot(p, v_ref[pl.ds(r0, tk), hs]))
            return tuple(out)

        accs = lax.fori_loop(0, nt, pv, tuple(jnp.zeros((tq, HEAD_PAD), F32) for _ in heads),
                             unroll=True)
        o_ref[...] = jnp.concatenate([a[:, :MLA_V] / a[:, MLA_V:MLA_V + 1] for a in accs],
                                     axis=-1).astype(BF16)

    j = pl.program_id(2)

    @pl.when(j < ncb)
    def _():
        attend(n_ctx, _pick_tile(n_ctx, (256, 128)))

    @pl.when(j >= ncb)
    def _():
        attend(n_all, _pick_tile(n_all, ATTN_KEY_TILES))


def _attn_call(qm, km, vm, *, n_batch, nj, ncb, n_ctx):
    rows = qm.shape[0]
    tm = ROW_BLOCK
    t_all = nj * tm
    tk = _pick_tile(t_all, ATTN_KEY_TILES)
    qmap = lambda b, hp, j: (b * nj + j, hp)
    kmap = lambda b, hp, j: (b, hp)
    return pl.pallas_call(
        functools.partial(_attn_body, n_ctx=n_ctx, n_all=t_all, ncb=ncb),
        out_shape=jax.ShapeDtypeStruct((rows, MLA_O), BF16),
        grid=(n_batch, MLA_HEADS // 2, nj),
        in_specs=[pl.BlockSpec((tm, 2 * HEAD_PAD), qmap),
                  pl.BlockSpec((t_all, 2 * HEAD_PAD), kmap),
                  pl.BlockSpec((t_all, 2 * HEAD_PAD), kmap)],
        out_specs=pl.BlockSpec((tm, 2 * MLA_V), qmap),
        scratch_shapes=[pltpu.VMEM((2, t_all // tk, tm, tk), F32)],
        compiler_params=_params(3),
        name="mla_attention",
    )(qm, km, vm)


def _merge_body(yf_ref, yb_ref, rg_ref, cb_ref, cu_ref, cup_ref, cun_ref, om_ref, gt_ref, x_ref,
                g1_ref, a2_ref, b2_ref, gn_ref, cw_ref, wro_ref, wco_ref, wmo_ref, wout_ref,
                wrh_ref, wrl_ref, rb_ref, xmid_ref, h2_ref, rt_ref, gw_ref, lg_ref, *, nj, ncb, d_model):
    tm = x_ref.shape[1]

    @pl.when(jnp.logical_and(pl.program_id(0) == 0, pl.program_id(1) == 0))
    def _():
        lg_ref[...] = jnp.zeros_like(lg_ref)

    j = jnp.minimum(pl.program_id(1), nj - 1)
    seg_first = jnp.logical_or(j == 0, j == ncb)
    seg_last = jnp.logical_or(j == ncb - 1, j == nj - 1)

    for g in range(x_ref.shape[0]):
        cls, w_lo, w_hi = _route_rows(jnp.transpose(lg_ref[g])[:N_EXPERTS, :], rb_ref[...])
        rank, counts = _block_ranks(cls.astype(F32))
        rt_ref[g] = jnp.concatenate(
            [cls.astype(F32), rank, counts, jnp.zeros((ROUTE_ROWS - 3, tm), F32)], axis=0)
        gw_ref[g] = jnp.transpose(
            jnp.concatenate([w_lo, w_hi, jnp.zeros((LANES - 2, tm), F32)], axis=0))

        y = yf_ref[g].astype(F32) + yb_ref[g].astype(F32)
        yn = jnp.concatenate([_rms(y[:, hd * RET_DV:(hd + 1) * RET_DV])
                              for hd in range(RET_HEADS)], axis=-1) * gn_ref[...]
        y_ret = _dot((rg_ref[g].astype(F32) * yn).astype(BF16), wro_ref[...])

        u = cu_ref[g].astype(F32)
        ridx = lax.broadcasted_iota(jnp.int32, u.shape, 0)
        prev_row = cup_ref[g].astype(F32)[BF16_SUBLANES - 1:, :] * jnp.where(seg_first, 0.0, 1.0)
        next_row = cun_ref[g].astype(F32)[0:1, :] * jnp.where(seg_last, 0.0, 1.0)
        u_prev = jnp.where(ridx == 0, prev_row, pltpu.roll(u, 1, 0))
        u_next = jnp.where(ridx == tm - 1, next_row, pltpu.roll(u, tm - 1, 0))
        conv = u_prev * cw_ref[0:1, :] + u * cw_ref[1:2, :] + u_next * cw_ref[2:3, :]
        y_conv = _dot((cb_ref[g].astype(F32) * conv).astype(BF16), wco_ref[...])

        y_mla = _dot(om_ref[g], wmo_ref[...])

        merged = (gt_ref[g, :, 0:d_model].astype(F32) * y_ret
                  + gt_ref[g, :, d_model:2 * d_model].astype(F32) * y_conv
                  + gt_ref[g, :, 2 * d_model:3 * d_model].astype(F32) * y_mla)
        x_mid = x_ref[g] + g1_ref[g] * _dot(merged.astype(BF16), wout_ref[...])
        xmid_ref[g] = x_mid

        h2 = _rms(x_mid) * a2_ref[g] + b2_ref[g]
        h_hi, h_lo = _split_bf16(h2)
        lg_ref[g] = _dot(h_hi, wrh_ref[...]) + _dot(h_hi, wrl_ref[...]) + _dot(h_lo, wrh_ref[...])
        h2_ref[g] = _pack_pairs(h_hi)


def _top2_of4(v):
    def first_max(rows):
        best, idx = rows[0], jnp.zeros(rows[0].shape, jnp.int32)
        for e in range(1, len(rows)):
            better = rows[e] > best
            idx = jnp.where(better, e, idx)
            best = jnp.where(better, rows[e], best)
        return best, idx

    b1, i1 = first_max(v)
    b2, i2 = first_max([jnp.where(i1 == e, -jnp.inf, v[e]) for e in range(len(v))])
    return i1, i2, b1, b2


def _route_rows(logits_t, bias):
    scores = jax.nn.sigmoid(logits_t)
    biased = scores + bias
    row = lambda a, e: a[e:e + 1, :]
    best = None
    for g in range(N_GROUPS):
        v = [row(biased, g * EXPERTS_PER_GROUP + e) for e in range(EXPERTS_PER_GROUP)]
        i1, i2, b1, b2 = _top2_of4(v)
        cand = (b1 + b2, jnp.full(i1.shape, g, jnp.int32), i1, i2)
        if best is None:
            best = cand
        else:
            better = cand[0] > best[0]
            best = tuple(jnp.where(better, c, o) for c, o in zip(cand, best))
    _, g_sel, i1, i2 = best
    lo = jnp.minimum(i1, i2)
    hi = jnp.maximum(i1, i2)
    e_lo = g_sel * EXPERTS_PER_GROUP + lo
    e_hi = g_sel * EXPERTS_PER_GROUP + hi
    s_lo = jnp.zeros_like(best[0])
    s_hi = jnp.zeros_like(best[0])
    for e in range(N_EXPERTS):
        s_lo = jnp.where(e_lo == e, row(scores, e), s_lo)
        s_hi = jnp.where(e_hi == e, row(scores, e), s_hi)
    total = s_lo + s_hi
    pair_base = jnp.where(lo == 0, 0, jnp.where(lo == 1, 3, 5))
    cls = g_sel * len(PAIRS) + pair_base + (hi - lo - 1)
    return cls, s_lo / total, s_hi / total


def _block_ranks(cls_row):
    n = cls_row.shape[1]
    cls_col = jnp.transpose(jnp.broadcast_to(cls_row, (LANES, n)))[:, 0:1]
    ii = lax.broadcasted_iota(jnp.int32, (n, n), 0)
    jj = lax.broadcasted_iota(jnp.int32, (n, n), 1)
    earlier_same = jnp.logical_and(cls_col == cls_row, ii < jj)
    rank = jnp.sum(jnp.where(earlier_same, 1.0, 0.0), axis=0, keepdims=True)
    lane = lax.broadcasted_iota(jnp.int32, (1, n), 1)
    counts = jnp.zeros((1, n), F32)
    for c in range(N_CLASSES):
        cnt = jnp.sum(jnp.where(cls_row == c, 1.0, 0.0), axis=1, keepdims=True)
        counts = jnp.where(lane == c, cnt, counts)
    return rank, counts


def _merge_call(yf, yb, rg, cb, cu, om, gt, xa, g1, a2, b2, wts, *, n_batch, nj, ncb):
    rows, d = xa.shape
    tm = ROW_BLOCK
    t_all = nj * tm
    nb = rows // tm
    halo = BF16_SUBLANES
    per_blk = tm // halo
    grp = math.gcd(n_batch, MERGE_BATCH_GROUP)
    per_batch = lambda a: a.reshape(n_batch, t_all, a.shape[-1])
    as_mod = lambda a: a.reshape(n_batch, 2, 1, d)
    blk = lambda j: jnp.minimum(j, nj - 1)
    row = lambda b, j: (b, blk(j), 0)
    late = lambda b, j: (b, jnp.maximum(j - 1, 0), 0)
    mod = lambda b, j: (b, (blk(j) >= ncb).astype(jnp.int32), 0, 0)
    prev = lambda b, j: (b, jnp.maximum(blk(j) * per_blk - 1, 0), 0)
    nxt = lambda b, j: (b, jnp.minimum((blk(j) + 1) * per_blk, t_all // halo - 1), 0)
    wide = lambda w, im=row: pl.BlockSpec((grp, tm, w), im)
    mod_spec = pl.BlockSpec((grp, None, 1, d), mod)
    in_specs = [wide(RV_W), wide(RV_W), wide(RV_W), wide(CONV_WIDTH), wide(CONV_WIDTH),
                pl.BlockSpec((grp, halo, CONV_WIDTH), prev), pl.BlockSpec((grp, halo, CONV_WIDTH), nxt),
                wide(MLA_O), wide(N_BRANCH * d), wide(d), mod_spec, mod_spec, mod_spec]
    in_specs += [_const_spec(w.shape) for w in wts]
    cu3 = per_batch(cu)
    xmid, h2, route, gate_w = pl.pallas_call(
        functools.partial(_merge_body, nj=nj, ncb=ncb, d_model=d),
        out_shape=[jax.ShapeDtypeStruct((n_batch, t_all, d), F32),
                   jax.ShapeDtypeStruct((n_batch, t_all, d // 2), jnp.uint32),
                   jax.ShapeDtypeStruct((n_batch, nj * ROUTE_ROWS, tm), F32),
                   jax.ShapeDtypeStruct((n_batch, t_all, LANES), F32)],
        grid=(n_batch // grp, nj + 1),
        in_specs=in_specs,
        out_specs=[wide(d), wide(d // 2), pl.BlockSpec((grp, ROUTE_ROWS, tm), late),
                   wide(LANES, late)],
        scratch_shapes=[pltpu.VMEM((grp, tm, LANES), F32)],
        compiler_params=_params(2),
        name="merge_out_proj",
    )(per_batch(yf), per_batch(yb), per_batch(rg), per_batch(cb), cu3, cu3, cu3, per_batch(om),
      per_batch(gt), per_batch(xa), as_mod(g1), as_mod(a2), as_mod(b2), *wts)
    return (xmid.reshape(rows, d), h2.reshape(rows, d // 2), route.reshape(nb, ROUTE_ROWS, tm),
            gate_w.reshape(rows, LANES))


def _moe_body(ea_ref, eb_ref, nu_ref, h_ref, gw_ref, w1a_ref, w3a_ref, w2a_ref, w1b_ref, w3b_ref,
              w2b_ref, o_ref, w13a_s, w2a_s, w13b_s, w2b_s):
    t = pl.program_id(0)
    used = t < nu_ref[0]
    prev = jnp.maximum(t - 1, 0)

    def refresh(e_ref, w1_ref, w3_ref, w2_ref, w13_s, w2_s):
        @pl.when(jnp.logical_and(used, jnp.logical_or(t == 0, e_ref[t] != e_ref[prev])))
        def _():
            w13_s[:, :D_EXPERT] = w1_ref[...].astype(BF16)
            w13_s[:, D_EXPERT:] = w3_ref[...].astype(BF16)
            w2_s[...] = w2_ref[...].astype(BF16)

    refresh(ea_ref, w1a_ref, w3a_ref, w2a_ref, w13a_s, w2a_s)
    refresh(eb_ref, w1b_ref, w3b_ref, w2b_ref, w13b_s, w2b_s)

    @pl.when(used)
    def _():
        h = _unpack_pairs(h_ref[...]).astype(BF16)
        gw = gw_ref[...]

        def expert(w13_s, w2_s, wt):
            a = _dot(h, w13_s[...])
            act = _silu(a[:, :D_EXPERT]) * a[:, D_EXPERT:] * wt
            return _dot(act.astype(BF16), w2_s[...])

        o_ref[...] = _pack_pairs(expert(w13a_s, w2a_s, gw[:, 0:1])
                                 + expert(w13b_s, w2b_s, gw[:, 1:2]))

    @pl.when(jnp.logical_not(used))
    def _():
        o_ref[...] = jnp.zeros_like(o_ref)


def _moe_call(tile_ea, tile_eb, n_used, hs, gw, w1, w3, w2, layer):
    npad = hs.shape[0]
    d = w1.shape[2]
    tmo = MOE_TILE
    row = lambda t, ea, eb, nu: (t, 0)
    wa = lambda t, ea, eb, nu: (layer, ea[t], 0, 0)
    wb = lambda t, ea, eb, nu: (layer, eb[t], 0, 0)
    up = lambda im: pl.BlockSpec((None, None, d, D_EXPERT), im)
    down = lambda im: pl.BlockSpec((None, None, D_EXPERT, d), im)
    grid_spec = pltpu.PrefetchScalarGridSpec(
        num_scalar_prefetch=3,
        grid=(npad // tmo,),
        in_specs=[pl.BlockSpec((tmo, d // 2), row), pl.BlockSpec((tmo, LANES), row),
                  up(wa), up(wa), down(wa), up(wb), up(wb), down(wb)],
        out_specs=pl.BlockSpec((tmo, d // 2), row),
        scratch_shapes=[pltpu.VMEM((d, 2 * D_EXPERT), BF16), pltpu.VMEM((D_EXPERT, d), BF16)] * 2)
    return pl.pallas_call(
        _moe_body,
        out_shape=jax.ShapeDtypeStruct((npad, d // 2), jnp.uint32),
        grid_spec=grid_spec,
        compiler_params=_params(1),
        name="moe_experts",
    )(tile_ea, tile_eb, n_used, hs, gw, w1, w3, w2, w1, w3, w2)


def _dispatch(route, n_tok):
    tmo = MOE_TILE
    n_tiles = n_tok // tmo + N_CLASSES
    npad = n_tiles * tmo
    cls = route[:, 0, :].astype(jnp.int32)
    rank = route[:, 1, :].astype(jnp.int32)
    counts = route[:, 2, :N_CLASSES].astype(jnp.int32)
    tiles_per = (jnp.sum(counts, axis=0) + tmo - 1) // tmo
    tile_end = jnp.cumsum(tiles_per)
    offs = (tile_end - tiles_per) * tmo
    block_base = offs[None, :] + jnp.cumsum(counts, axis=0) - counts
    classes = jnp.arange(N_CLASSES, dtype=jnp.int32)
    base = jnp.sum(jnp.where(cls[:, :, None] == classes, block_base[:, None, :], 0), axis=-1)
    dest = (base + rank).reshape(-1)
    tile_ids = jnp.arange(n_tiles, dtype=jnp.int32)
    tile_cls = jnp.sum((tile_end[None, :] <= tile_ids[:, None]).astype(jnp.int32), axis=1)
    tile_cls = jnp.minimum(tile_cls, N_CLASSES - 1)
    pa = jnp.asarray([p[0] for p in PAIRS], jnp.int32)
    pb = jnp.asarray([p[1] for p in PAIRS], jnp.int32)
    group = (tile_cls // len(PAIRS)) * EXPERTS_PER_GROUP
    tile_ea = group + pa[tile_cls % len(PAIRS)]
    tile_eb = group + pb[tile_cls % len(PAIRS)]
    n_used = tile_end[-1:].astype(jnp.int32)
    return dest, npad, tile_ea, tile_eb, n_used


def _sc_mesh():
    return plsc.VectorSubcoreMesh(core_axis_name="c", subcore_axis_name="s",
                                  num_cores=SC_CORES, num_subcores=SC_SUBCORES)


def _sc_chunks(idx):
    assert idx.shape[0] % (SC_WORKERS * GATHER_CHUNK) == 0
    return idx.reshape(SC_WORKERS, -1, GATHER_CHUNK)


def _sc_pipeline(n_chunks, fetch, put):
    assert n_chunks % 2 == 0
    fetch(0, 0).start()

    @pl.loop(0, n_chunks, step=2)
    def _(c0):
        for b in (0, 1):
            c = c0 + b
            fetch(c, b).wait()

            @pl.when(c >= 1)
            def _():
                put(c - 1, 1 - b).wait()

            @pl.when(c + 1 < n_chunks)
            def _():
                fetch(c + 1, 1 - b).start()

            put(c, b).start()

    put(n_chunks - 1, 1).wait()


def _sc_scratch(n_chunks, d, dtype):
    return [pltpu.VMEM((n_chunks, GATHER_CHUNK), jnp.int32),
            pltpu.VMEM((2, GATHER_CHUNK, d), dtype),
            pltpu.SemaphoreType.DMA((2,)), pltpu.SemaphoreType.DMA((2,))]


def _sc_gather_rows(table, idx):
    d = table.shape[1]
    idx3 = _sc_chunks(idx)
    n_chunks = idx3.shape[1]
    assert table.dtype.itemsize == 4

    @functools.partial(
        pl.kernel, mesh=_sc_mesh(),
        out_type=jax.ShapeDtypeStruct((idx.shape[0], d), table.dtype),
        scratch_types=_sc_scratch(n_chunks, d, table.dtype),
        name="sc_row_gather")
    def gather(table_hbm, idx_hbm, out_hbm, idx_v, rows_v, sem_in, sem_out):
        worker = lax.axis_index("s") * SC_CORES + lax.axis_index("c")
        pltpu.sync_copy(idx_hbm.at[worker], idx_v)

        def fetch(c, b):
            return pltpu.make_async_copy(table_hbm.at[idx_v.at[c]], rows_v.at[b], sem_in.at[b])

        def put(c, b):
            off = (worker * n_chunks + c) * GATHER_CHUNK
            return pltpu.make_async_copy(rows_v.at[b], out_hbm.at[pl.ds(off, GATHER_CHUNK)],
                                         sem_out.at[b])

        _sc_pipeline(n_chunks, fetch, put)

    return gather(table, idx3)


def _sc_scatter_rows(rows, idx, n_out):
    d = rows.shape[1]
    idx3 = _sc_chunks(idx)
    n_chunks = idx3.shape[1]
    assert rows.dtype.itemsize == 4

    @functools.partial(
        pl.kernel, mesh=_sc_mesh(),
        out_type=jax.ShapeDtypeStruct((n_out, d), rows.dtype),
        scratch_types=_sc_scratch(n_chunks, d, rows.dtype),
        name="sc_row_scatter")
    def scatter(rows_hbm, idx_hbm, out_hbm, idx_v, rows_v, sem_in, sem_out):
        worker = lax.axis_index("s") * SC_CORES + lax.axis_index("c")
        pltpu.sync_copy(idx_hbm.at[worker], idx_v)

        def fetch(c, b):
            off = (worker * n_chunks + c) * GATHER_CHUNK
            return pltpu.make_async_copy(rows_hbm.at[pl.ds(off, GATHER_CHUNK)], rows_v.at[b],
                                         sem_in.at[b])

        def put(c, b):
            return pltpu.make_async_copy(rows_v.at[b], out_hbm.at[idx_v.at[c]], sem_out.at[b])

        _sc_pipeline(n_chunks, fetch, put)

    return scatter(rows, idx3)


def _final_body(x_ref, f_ref, g2_ref, fn_ref, o_ref):
    for g in range(x_ref.shape[0]):
        o_ref[g] = _rms(x_ref[g] + g2_ref[g] * _unpack_pairs(f_ref[g])) * fn_ref[...]


def _final_call(xmid, fg, g2, final_norm, *, n_batch, nj, ncb):
    d = xmid.shape[1]
    tm = ROW_BLOCK
    njl = nj - ncb
    grp = math.gcd(n_batch, FINAL_BATCH_GROUP)
    per_batch = lambda a: a.reshape(n_batch, nj * tm, a.shape[-1])
    src = lambda b, j: (b, ncb + j, 0)
    out = pl.pallas_call(
        _final_body,
        out_shape=jax.ShapeDtypeStruct((n_batch, njl * tm, d), F32),
        grid=(n_batch // grp, njl),
        in_specs=[pl.BlockSpec((grp, tm, d), src), pl.BlockSpec((grp, tm, d // 2), src),
                  pl.BlockSpec((grp, None, 1, d), lambda b, j: (b, 1, 0, 0)),
                  _const_spec((1, d))],
        out_specs=pl.BlockSpec((grp, tm, d), lambda b, j: (b, j, 0)),
        compiler_params=_params(2),
        name="final_norm",
    )(per_batch(xmid), per_batch(fg), g2.reshape(n_batch, 2, 1, d), final_norm.reshape(1, d))
    return out.reshape(n_batch * njl * tm, d)


def _split_cols(w, sizes):
    out, off = [], 0
    for s in sizes:
        out.append(w[:, off:off + s])
        off += s
    out.append(w[:, off:])
    return out


def _layer_weights(w_in, w_uq, w_ukv, q_norm, kv_norm):
    wq, wk, wv, wg, wcb, wcc, wcx, wqd, wkvd, wkr, wgate = _split_cols(w_in, IN_SIZES)
    rope_lanes = lambda w: jnp.pad(w, ((0, 0), (MLA_NOPE, HEAD_PAD - MLA_NOPE - MLA_ROPE)))
    w_ext = jnp.concatenate(
        [wq, wk, wv, wg, wcb, wcc, wcx, wqd, wkvd, rope_lanes(wkr), wgate], axis=1).astype(BF16)

    uq = w_uq.reshape(MLA_Q_RANK, MLA_HEADS, MLA_NOPE + MLA_ROPE)
    tail = HEAD_PAD - MLA_NOPE - MLA_ROPE
    uq_pad = jnp.pad(uq, ((0, 0), (0, 0), (0, tail)))
    wuq = uq_pad.reshape(MLA_Q_RANK, MLA_W).astype(BF16)

    ukv = w_ukv.reshape(MLA_KV_RANK, MLA_HEADS, MLA_NOPE + MLA_V)
    wk_up = jnp.pad(ukv[..., :MLA_NOPE], ((0, 0), (0, 0), (0, HEAD_PAD - MLA_NOPE)))
    wk_up = wk_up.reshape(MLA_KV_RANK, MLA_W).astype(BF16)
    wv_pad = jnp.pad(ukv[..., MLA_NOPE:], ((0, 0), (0, 0), (0, HEAD_PAD - MLA_V)))
    wv_pad = wv_pad.reshape(MLA_KV_RANK, MLA_W).astype(BF16)
    ones_row = jnp.zeros((MLA_HEADS, HEAD_PAD), F32).at[:, MLA_V].set(1.0).reshape(1, MLA_W)
    q_gain = q_norm.astype(F32) * ((MLA_NOPE + MLA_ROPE) ** -0.5 * math.log2(math.e))
    return (w_ext, wuq, wk_up, wv_pad, q_gain.reshape(1, -1),
            kv_norm.reshape(1, -1).astype(F32), ones_row)


def _rotary_tables(n_ctx, n_lat):
    n_all = n_ctx + n_lat
    row = lax.broadcasted_iota(jnp.int32, (n_all, LANES), 0)
    lane = lax.broadcasted_iota(jnp.int32, (n_all, LANES), 1)
    is_ctx = row < n_ctx
    pos = row - n_ctx
    grid_row = pos // GRID_W
    grid_col = pos - grid_row * GRID_W

    def inv_freq(idx, half):
        return ROPE_BASE ** (-idx.astype(F32) / half)

    def table(ang, first_half, live):
        cos = jnp.where(is_ctx, 1.0, jnp.cos(ang))
        sin = jnp.where(is_ctx, 0.0, jnp.where(first_half, -jnp.sin(ang), jnp.sin(ang)))
        return jnp.where(live, cos, 0.0), jnp.where(live, sin, 0.0)

    half = RET_DK // 2
    ang = pos.astype(F32) * inv_freq(lane % half, half)
    cr, sr = table(ang, (lane % RET_DK) < half, True)

    quarter = MLA_ROPE // 4
    rl = lane - MLA_NOPE
    coord = jnp.where(rl < MLA_ROPE // 2, grid_row, grid_col)
    ang = coord.astype(F32) * inv_freq(rl % quarter, quarter)
    cm, sm = table(ang, (rl % (2 * quarter)) < quarter,
                   jnp.logical_and(rl >= 0, rl < MLA_ROPE))
    cm = jnp.where(rl < 0, 1.0, cm)
    return cr, sr, cm, sm


def _retention_consts(ret_decay):
    log_gf = jax.nn.log_sigmoid(ret_decay[0].astype(F32))
    log_gb = jax.nn.log_sigmoid(ret_decay[1].astype(F32))
    idx = jnp.arange(RET_CHUNK, dtype=F32)
    rel = idx[:, None] - idx[None, :]
    dm_f = jnp.where(rel >= 0, jnp.exp(log_gf[:, None, None] * jnp.maximum(rel, 0.0)[None]), 0.0)
    dm_b = jnp.where(rel < 0, jnp.exp(log_gb[:, None, None] * jnp.maximum(-rel, 0.0)[None]), 0.0)

    def lanes(t):
        return jnp.repeat(t, RET_DK, axis=1)

    xi_f = lanes(jnp.exp(log_gf[None, :] * (idx + 1.0)[:, None]))
    zt_f = lanes(jnp.exp(log_gf[None, :] * (RET_CHUNK - 1 - idx)[:, None]))
    xi_b = lanes(jnp.exp(log_gb[None, :] * (RET_CHUNK - idx)[:, None]))
    zt_b = lanes(jnp.exp(log_gb[None, :] * idx[:, None]))
    cd = lambda lg: jnp.broadcast_to(jnp.exp(lg * RET_CHUNK)[:, None, None], (RET_HEADS, 1, RET_DV))
    return dm_f, dm_b, xi_f, zt_f, xi_b, zt_b, cd(log_gf), cd(log_gb)


def kernel(x, c, ctx, c_ctx, w_ada, b_ada, norm1, norm2, w_in, ret_decay, ret_gn, w_ret_o, conv_w,
           w_conv_o, mla_q_norm, w_uq, mla_kv_norm, w_ukv, w_mla_o, w_out, w_router, router_bias,
           w1, w3, w2, final_norm):
    n_batch, n_lat, d = x.shape
    n_ctx = ctx.shape[1]
    depth = w_ada.shape[0]
    t_all = n_ctx + n_lat
    assert n_ctx % ROW_BLOCK == 0 and n_lat % ROW_BLOCK == 0 and n_lat % GRID_W == 0
    nj = t_all // ROW_BLOCK
    ncb = n_ctx // ROW_BLOCK
    n_tok = n_batch * t_all
    assert n_tok % MOE_TILE == 0
    geom = dict(n_batch=n_batch, nj=nj, ncb=ncb)

    cc = jnp.concatenate([c, c_ctx[None, :]], axis=0)
    cc = jnp.pad(cc, ((0, -cc.shape[0] % 8), (0, 0)))
    mod = _ada_call(cc, w_ada, b_ada)[:, :n_batch + 1].reshape(depth, n_batch + 1, N_MOD, d)
    pick = jnp.stack([jnp.full((n_batch,), n_batch, jnp.int32),
                      jnp.arange(n_batch, dtype=jnp.int32)], axis=1).reshape(-1)
    mod = mod[:, pick]

    tabs = _rotary_tables(n_ctx, n_lat)
    wr_hi, wr_lo = _split_bf16(jnp.pad(w_router.astype(F32), ((0, 0), (0, LANES - N_EXPERTS))))
    rbias = router_bias.astype(F32).reshape(N_EXPERTS, 1)

    out = None
    x_parts = (ctx, x)
    for l in range(depth):
        m = mod[l]
        rowvec = lambda v: v.reshape(2 * n_batch, 1, d)
        a1 = rowvec(norm1[l][None, :] * (1.0 + m[:, 1]))
        b1 = rowvec(m[:, 0])
        g1 = rowvec(m[:, 2])
        a2 = rowvec(norm2[l][None, :] * (1.0 + m[:, 4]))
        b2 = rowvec(m[:, 3])
        g2 = rowvec(m[:, 5])

        wts = _layer_weights(w_in[l], w_uq[l], w_ukv[l], mla_q_norm[l], mla_kv_norm[l])
        proj = _inproj_call(x_parts, a1, b1, tabs, wts, **geom)
        rq, rk, rv, rg, cb, cu, qm, km, vm, gt, xa = proj

        yf, yb = _ret_call(rq, rk, rv, _retention_consts(ret_decay[l]), **geom)
        om = _attn_call(qm, km, vm, n_ctx=n_ctx, **geom)

        merge_wts = (ret_gn[l].reshape(1, -1).astype(F32), conv_w[l].T.astype(F32),
                     w_ret_o[l].astype(BF16), w_conv_o[l].astype(BF16), w_mla_o[l].astype(BF16),
                     w_out[l].astype(BF16), wr_hi, wr_lo, rbias)
        xmid, h2, route, gate_w = _merge_call(yf, yb, rg, cb, cu, om, gt, xa, g1, a2, b2, merge_wts,
                                      **geom)
        dest, npad, tile_ea, tile_eb, n_used = _dispatch(route, n_tok)
        hs = _sc_scatter_rows(h2, dest, npad)
        gw = _sc_scatter_rows(gate_w, dest, npad)
        f_sorted = _moe_call(tile_ea, tile_eb, n_used, hs, gw, w1, w3, w2, l)
        fg = _sc_gather_rows(f_sorted, dest)

        if l < depth - 1:
            x_parts = (xmid, fg, g2)
        else:
            out = _final_call(xmid, fg, g2, final_norm, **geom)
    return out.reshape(n_batch, n_lat, d)
```

```python
import functools
import math

import jax
import jax.numpy as jnp
from jax import lax
from jax.experimental import pallas as pl
from jax.experimental.pallas import tpu as pltpu
from jax.experimental.pallas import tpu_sc as plsc

F32 = jnp.float32
BF16 = jnp.bfloat16

GRID_W = 64
RMS_EPS = 1e-6
ROPE_BASE = 10000.0
N_MOD = 6
RET_HEADS = 4
RET_DK = 64
RET_DV = 128
RET_CHUNK = 256
CONV_WIDTH = 512
MLA_HEADS = 8
MLA_Q_RANK = 384
MLA_KV_RANK = 256
MLA_NOPE = 64
MLA_ROPE = 32
MLA_V = 64
N_BRANCH = 3
N_EXPERTS = 16
N_GROUPS = 4
EXPERTS_PER_GROUP = N_EXPERTS // N_GROUPS
D_EXPERT = 512
IN_SIZES = (RET_HEADS * RET_DK, RET_HEADS * RET_DK, RET_HEADS * RET_DV, RET_HEADS * RET_DV,
            CONV_WIDTH, CONV_WIDTH, CONV_WIDTH, MLA_Q_RANK, MLA_KV_RANK, MLA_ROPE)

LANES = 128
BF16_SUBLANES = 16
VMEM_LIMIT = 56 * 1024 * 1024
SC_CORES = 2
SC_SUBCORES = 16
SC_WORKERS = SC_CORES * SC_SUBCORES

HEAD_PAD = LANES
ROW_BLOCK = 256
MOE_TILE = 512
ATTN_KEY_TILES = (1408, 768, 512, 256, 128)
ROUTE_ROWS = 8
RET_BATCH_GROUP = 4
INPROJ_BATCH_GROUP = 2
MERGE_BATCH_GROUP = 2
FINAL_BATCH_GROUP = 4
GATHER_CHUNK = 96
PAIRS = [(a, b) for a in range(EXPERTS_PER_GROUP) for b in range(a + 1, EXPERTS_PER_GROUP)]
N_CLASSES = N_GROUPS * len(PAIRS)

RQ_W = RET_HEADS * RET_DK
RV_W = RET_HEADS * RET_DV
MLA_W = MLA_HEADS * HEAD_PAD
MLA_O = MLA_HEADS * MLA_V


def _const_spec(shape):
    nd = len(shape)
    return pl.BlockSpec(shape, lambda *_: (0,) * nd, pipeline_mode=pl.Buffered(1))


def _params(n_axes):
    return pltpu.CompilerParams(dimension_semantics=("arbitrary",) * n_axes,
                                vmem_limit_bytes=VMEM_LIMIT)


def _dot(a, b):
    return jnp.dot(a, b, preferred_element_type=F32)


def _split_bf16(a):
    hi = a.astype(BF16)
    lo = (a - hi.astype(F32)).astype(BF16)
    return hi, lo


def _silu(v):
    return v * jax.nn.sigmoid(v)


def _rms(v):
    return v * lax.rsqrt(jnp.mean(v * v, axis=-1, keepdims=True) + RMS_EPS)


def _pack_pairs(v):
    half = v.shape[1] // 2
    bits = lambda t: lax.bitcast_convert_type(t.astype(BF16).astype(F32), jnp.uint32)
    return (bits(v[:, :half]) & jnp.uint32(0xFFFF0000)) | (bits(v[:, half:]) >> 16)


def _unpack_pairs(u):
    hi = lax.bitcast_convert_type(u & jnp.uint32(0xFFFF0000), F32)
    lo = lax.bitcast_convert_type(u << 16, F32)
    return jnp.concatenate([hi, lo], axis=1)


def _ada_body(c_ref, w_ref, b_ref, o_ref):
    a_hi, a_lo = _split_bf16(_silu(c_ref[...]))
    w_hi, w_lo = _split_bf16(w_ref[...])
    o_ref[...] = _dot(a_hi, w_hi) + _dot(a_hi, w_lo) + _dot(a_lo, w_hi) + b_ref[...]


def _ada_call(cc, w_ada, b_ada):
    depth, d, nm = w_ada.shape
    rows = cc.shape[0]
    cb = nm // 4
    return pl.pallas_call(
        _ada_body,
        out_shape=jax.ShapeDtypeStruct((depth, rows, nm), F32),
        grid=(depth, nm // cb),
        in_specs=[pl.BlockSpec((rows, d), lambda l, n: (0, 0)),
                  pl.BlockSpec((None, d, cb), lambda l, n: (l, 0, n)),
                  pl.BlockSpec((None, 1, cb), lambda l, n: (l, 0, n))],
        out_specs=pl.BlockSpec((None, rows, cb), lambda l, n: (l, 0, n)),
        compiler_params=_params(2),
        name="ada_mod",
    )(cc, w_ada, b_ada.reshape(depth, 1, nm))


_O_RQ = 0
_O_RK = _O_RQ + RQ_W
_O_RV = _O_RK + RQ_W
_O_RG = _O_RV + RV_W
_O_CB = _O_RG + RV_W
_O_CC = _O_CB + CONV_WIDTH
_O_CX = _O_CC + CONV_WIDTH
_O_QD = _O_CX + CONV_WIDTH
_O_KVD = _O_QD + MLA_Q_RANK
_O_KR = _O_KVD + MLA_KV_RANK
_O_GT = _O_KR + LANES


def _rot_half(v, half):
    width = v.shape[1]
    lane = lax.broadcasted_iota(jnp.int32, v.shape, 1)
    first = (lane % (2 * half)) < half
    return jnp.where(first, pltpu.roll(v, width - half, 1), pltpu.roll(v, half, 1))


def _inproj_body(*refs, d_model, fused, ncb):
    n_lead = 3 if fused else 2
    lead, refs = refs[:n_lead], refs[n_lead:]
    shared, outs = refs[:13], refs[13:]
    a1_ref, b1_ref = shared[:2]
    for g in range(a1_ref.shape[0]):
        if fused:
            xm_ref, f_ref, g2_ref = lead
            x = xm_ref[g] + g2_ref[g] * _unpack_pairs(f_ref[g])
        else:
            ctx_ref, lat_ref = lead
            x = jnp.where(pl.program_id(0) < ncb, ctx_ref[g], lat_ref[g])
        _inproj_block(x, a1_ref[g], b1_ref[g], *shared[2:], *[o.at[g] for o in outs],
                      d_model=d_model)


def _inproj_block(x, a1, b1, cr_ref, sr_ref, cm_ref, sm_ref, w_ref, wuq_ref, wk_ref, wv_ref, qn_ref,
                  kvn_ref, ones_ref, rq_ref, rk_ref, rv_ref, rg_ref, cb_ref, cu_ref, qm_ref, km_ref,
                  vm_ref, gt_ref, xo_ref, *, d_model):
    xo_ref[...] = x
    h = (_rms(x) * a1 + b1).astype(BF16)

    def mm(off, width):
        return _dot(h, w_ref[:, off:off + width])

    cr = jnp.concatenate([cr_ref[...]] * (RQ_W // cr_ref.shape[1]), axis=1)
    sr = jnp.concatenate([sr_ref[...]] * (RQ_W // sr_ref.shape[1]), axis=1)
    q = mm(_O_RQ, RQ_W)
    rq_ref[...] = (q * cr + _rot_half(q, RET_DK // 2) * sr).astype(BF16)
    k = mm(_O_RK, RQ_W)
    rk_ref[...] = ((k * cr + _rot_half(k, RET_DK // 2) * sr) * (RET_DK ** -0.5)).astype(BF16)
    rv_ref[...] = mm(_O_RV, RV_W).astype(BF16)
    rg_ref[...] = _silu(mm(_O_RG, RV_W)).astype(BF16)
    cb_ref[...] = mm(_O_CB, CONV_WIDTH).astype(BF16)
    cu_ref[...] = (mm(_O_CC, CONV_WIDTH) * mm(_O_CX, CONV_WIDTH)).astype(BF16)

    cm = cm_ref[...]
    sm = sm_ref[...]
    qn = (_rms(mm(_O_QD, MLA_Q_RANK)) * qn_ref[...]).astype(BF16)
    for hd in range(MLA_HEADS):
        lo = hd * HEAD_PAD
        qa = _dot(qn, wuq_ref[:, lo:lo + HEAD_PAD])
        qm_ref[:, lo:lo + HEAD_PAD] = (qa * cm + _rot_half(qa, MLA_ROPE // 4) * sm).astype(BF16)

    kvn = (_rms(mm(_O_KVD, MLA_KV_RANK)) * kvn_ref[...]).astype(BF16)
    kr = mm(_O_KR, HEAD_PAD)
    kr = kr * cm + _rot_half(kr, MLA_ROPE // 4) * sm
    kn = _dot(kvn, wk_ref[...])
    for hd in range(MLA_HEADS):
        lo = hd * HEAD_PAD
        km_ref[:, lo:lo + HEAD_PAD] = (kn[:, lo:lo + HEAD_PAD] + kr).astype(BF16)
    vm_ref[...] = (_dot(kvn, wv_ref[...]) + ones_ref[...]).astype(BF16)

    for br in range(N_BRANCH):
        gt_ref[:, br * d_model:(br + 1) * d_model] = jax.nn.sigmoid(
            mm(_O_GT + br * d_model, d_model)).astype(BF16)


def _inproj_call(x_parts, a1, b1, tabs, wts, *, n_batch, nj, ncb):
    fused = len(x_parts) == 3
    d = x_parts[0].shape[-1]
    tm = ROW_BLOCK
    t_all = nj * tm
    rows = n_batch * t_all
    grp = math.gcd(n_batch, INPROJ_BATCH_GROUP)
    per_batch = lambda a: a.reshape(n_batch, -1, a.shape[-1])
    row = lambda j, b: (b, j, 0)
    mod = lambda j, b: (b, (j >= ncb).astype(jnp.int32), 0, 0)
    mod_spec = pl.BlockSpec((grp, None, 1, d), mod)
    as_mod = lambda a: a.reshape(n_batch, 2, 1, d)
    tab = lambda j, b: (j, 0)
    if fused:
        xm, f, g2 = x_parts
        lead = (per_batch(xm), per_batch(f), as_mod(g2))
        in_specs = [pl.BlockSpec((grp, tm, d), row), pl.BlockSpec((grp, tm, d // 2), row), mod_spec]
    else:
        lead = x_parts
        in_specs = [pl.BlockSpec((grp, tm, d), lambda j, b: (b, jnp.where(j < ncb, j, 0), 0)),
                    pl.BlockSpec((grp, tm, d), lambda j, b: (b, jnp.where(j < ncb, 0, j - ncb), 0))]
    in_specs += [mod_spec, mod_spec]
    in_specs += [pl.BlockSpec((tm, t.shape[1]), tab) for t in tabs]
    in_specs += [_const_spec(w.shape) for w in wts]
    widths = (RQ_W, RQ_W, RV_W, RV_W, CONV_WIDTH, CONV_WIDTH, MLA_W, MLA_W, MLA_W, N_BRANCH * d)
    out_shape = [jax.ShapeDtypeStruct((n_batch, t_all, w), BF16) for w in widths]
    out_specs = [pl.BlockSpec((grp, tm, w), row) for w in widths]
    out_shape.append(jax.ShapeDtypeStruct((n_batch, t_all, d), F32))
    out_specs.append(pl.BlockSpec((grp, tm, d), row))
    outs = pl.pallas_call(
        functools.partial(_inproj_body, d_model=d, fused=fused, ncb=ncb),
        out_shape=out_shape,
        grid=(nj, n_batch // grp),
        in_specs=in_specs,
        out_specs=out_specs,
        compiler_params=_params(2),
        name="in_proj",
    )(*lead, as_mod(a1), as_mod(b1), *tabs, *wts)
    return [o.reshape(rows, o.shape[-1]) for o in outs]


def _ret_direction(q_ref, k_ref, v_ref, y_ref, s_ref, dm_ref, xi_ref, zt_ref, cd_ref, chunk_order):
    states = [s_ref[hd] for hd in range(RET_HEADS)]
    for c in chunk_order:
        rows = slice(c * RET_CHUNK, (c + 1) * RET_CHUNK)
        q = q_ref[rows, :]
        k = k_ref[rows, :]
        v = v_ref[rows, :]
        qx = (q.astype(F32) * xi_ref[...]).astype(BF16)
        kz = (k.astype(F32) * zt_ref[...]).astype(BF16)
        for hd in range(RET_HEADS):
            ks = slice(hd * RET_DK, (hd + 1) * RET_DK)
            vs = slice(hd * RET_DV, (hd + 1) * RET_DV)
            vh = v[:, vs]
            sc = lax.dot_general(q[:, ks], k[:, ks], (((1,), (1,)), ((), ())),
                                 preferred_element_type=F32)
            inner = _dot((sc * dm_ref[hd]).astype(BF16), vh)
            cross = _dot(qx[:, ks], states[hd].astype(BF16))
            y_ref[rows, vs] = (inner + cross).astype(BF16)
            upd = lax.dot_general(kz[:, ks], vh, (((0,), (0,)), ((), ())),
                                  preferred_element_type=F32)
            states[hd] = cd_ref[hd] * states[hd] + upd
    for hd in range(RET_HEADS):
        s_ref[hd] = states[hd]


def _ret_body(qf_ref, kf_ref, vf_ref, qb_ref, kb_ref, vb_ref,
              dmf_ref, dmb_ref, xif_ref, ztf_ref, xib_ref, ztb_ref, cdf_ref, cdb_ref,
              yf_ref, yb_ref, sf_ref, sb_ref):
    @pl.when(pl.program_id(1) == 0)
    def _():
        sf_ref[...] = jnp.zeros_like(sf_ref)
        sb_ref[...] = jnp.zeros_like(sb_ref)

    n_chunks = ROW_BLOCK // RET_CHUNK
    for g in range(qf_ref.shape[0]):
        _ret_direction(qf_ref.at[g], kf_ref.at[g], vf_ref.at[g], yf_ref.at[g], sf_ref.at[g],
                       dmf_ref, xif_ref, ztf_ref, cdf_ref, range(n_chunks))
        _ret_direction(qb_ref.at[g], kb_ref.at[g], vb_ref.at[g], yb_ref.at[g], sb_ref.at[g],
                       dmb_ref, xib_ref, ztb_ref, cdb_ref, range(n_chunks - 1, -1, -1))


def _ret_call(rq, rk, rv, consts, *, n_batch, nj, ncb):
    rows = rq.shape[0]
    tm = ROW_BLOCK
    grp = math.gcd(n_batch, RET_BATCH_GROUP)
    fwd = lambda b, s: (b, s, 0)

    def bwd(b, s):
        return (b, jnp.where(s < ncb, ncb - 1 - s, nj - 1 - (s - ncb)), 0)

    per_batch = lambda a: a.reshape(n_batch, nj * tm, a.shape[1])
    specs = []
    for im in (fwd, bwd):
        specs += [pl.BlockSpec((grp, tm, RQ_W), im), pl.BlockSpec((grp, tm, RQ_W), im),
                  pl.BlockSpec((grp, tm, RV_W), im)]
    specs += [_const_spec(c.shape) for c in consts]
    q3, k3, v3 = per_batch(rq), per_batch(rk), per_batch(rv)
    yf, yb = pl.pallas_call(
        _ret_body,
        out_shape=[jax.ShapeDtypeStruct((n_batch, nj * tm, RV_W), BF16)] * 2,
        grid=(n_batch // grp, nj),
        in_specs=specs,
        out_specs=[pl.BlockSpec((grp, tm, RV_W), fwd), pl.BlockSpec((grp, tm, RV_W), bwd)],
        scratch_shapes=[pltpu.VMEM((grp, RET_HEADS, RET_DK, RET_DV), F32)] * 2,
        compiler_params=_params(2),
        name="retention",
    )(q3, k3, v3, q3, k3, v3, *consts)
    return yf.reshape(rows, RV_W), yb.reshape(rows, RV_W)


def _pick_tile(n, candidates):
    for c in candidates:
        if n % c == 0:
            return c
    raise ValueError(f"no tile for {n}")


def _attn_body(q_ref, k_ref, v_ref, o_ref, s_ref, *, n_ctx, n_all, ncb):
    tq = q_ref.shape[0]
    heads = [slice(hh * HEAD_PAD, (hh + 1) * HEAD_PAD) for hh in range(2)]

    def attend(n_keys, tk):
        nt = n_keys // tk
        qs = [q_ref[:, hs] for hs in heads]

        def qk(t, mrun):
            r0 = pl.multiple_of(t * tk, tk)
            out = []
            for hh, hs in enumerate(heads):
                s = lax.dot_general(qs[hh], k_ref[pl.ds(r0, tk), hs], (((1,), (1,)), ((), ())),
                                    preferred_element_type=F32)
                s_ref[hh, t, :, 0:tk] = s
                m = mrun[hh]
                for cc in range(tk // LANES):
                    m = jnp.maximum(m, s[:, cc * LANES:(cc + 1) * LANES])
                out.append(m)
            return tuple(out)

        mrun = lax.fori_loop(0, nt, qk, tuple(jnp.full((tq, LANES), -jnp.inf, F32) for _ in heads),
                             unroll=True)
        mrow = [jnp.max(m, axis=-1, keepdims=True) for m in mrun]

        def pv(t, accs):
            r0 = pl.multiple_of(t * tk, tk)
            out = []
            for hh, hs in enumerate(heads):
                p = jnp.exp2(s_ref[hh, t, :, 0:tk] - mrow[hh]).astype(BF16)
                out.append(accs[hh] + _dot(p, v_ref[pl.ds(r0, tk), hs]))
            return tuple(out)

        accs = lax.fori_loop(0, nt, pv, tuple(jnp.zeros((tq, HEAD_PAD), F32) for _ in heads),
                             unroll=True)
        o_ref[...] = jnp.concatenate([a[:, :MLA_V] / a[:, MLA_V:MLA_V + 1] for a in accs],
                                     axis=-1).astype(BF16)

    j = pl.program_id(2)

    @pl.when(j < ncb)
    def _():
        attend(n_ctx, _pick_tile(n_ctx, (256, 128)))

    @pl.when(j >= ncb)
    def _():
        attend(n_all, _pick_tile(n_all, ATTN_KEY_TILES))


def _attn_call(qm, km, vm, *, n_batch, nj, ncb, n_ctx):
    rows = qm.shape[0]
    tm = ROW_BLOCK
    t_all = nj * tm
    tk = _pick_tile(t_all, ATTN_KEY_TILES)
    qmap = lambda b, hp, j: (b * nj + j, hp)
    kmap = lambda b, hp, j: (b, hp)
    return pl.pallas_call(
        functools.partial(_attn_body, n_ctx=n_ctx, n_all=t_all, ncb=ncb),
        out_shape=jax.ShapeDtypeStruct((rows, MLA_O), BF16),
        grid=(n_batch, MLA_HEADS // 2, nj),
        in_specs=[pl.BlockSpec((tm, 2 * HEAD_PAD), qmap),
                  pl.BlockSpec((t_all, 2 * HEAD_PAD), kmap),
                  pl.BlockSpec((t_all, 2 * HEAD_PAD), kmap)],
        out_specs=pl.BlockSpec((tm, 2 * MLA_V), qmap),
        scratch_shapes=[pltpu.VMEM((2, t_all // tk, tm, tk), F32)],
        compiler_params=_params(3),
        name="mla_attention",
    )(qm, km, vm)


def _merge_body(yf_ref, yb_ref, rg_ref, cb_ref, cu_ref, cup_ref, cun_ref, om_ref, gt_ref, x_ref,
                g1_ref, a2_ref, b2_ref, gn_ref, cw_ref, wro_ref, wco_ref, wmo_ref, wout_ref,
                wrh_ref, wrl_ref, rb_ref, xmid_ref, h2_ref, rt_ref, gw_ref, lg_ref, *, nj, ncb, d_model):
    tm = x_ref.shape[1]

    @pl.when(jnp.logical_and(pl.program_id(0) == 0, pl.program_id(1) == 0))
    def _():
        lg_ref[...] = jnp.zeros_like(lg_ref)

    j = jnp.minimum(pl.program_id(1), nj - 1)
    seg_first = jnp.logical_or(j == 0, j == ncb)
    seg_last = jnp.logical_or(j == ncb - 1, j == nj - 1)

    for g in range(x_ref.shape[0]):
        cls, w_lo, w_hi = _route_rows(jnp.transpose(lg_ref[g])[:N_EXPERTS, :], rb_ref[...])
        rank, counts = _block_ranks(cls.astype(F32))
        rt_ref[g] = jnp.concatenate(
            [cls.astype(F32), rank, counts, jnp.zeros((ROUTE_ROWS - 3, tm), F32)], axis=0)
        gw_ref[g] = jnp.transpose(
            jnp.concatenate([w_lo, w_hi, jnp.zeros((LANES - 2, tm), F32)], axis=0))

        y = yf_ref[g].astype(F32) + yb_ref[g].astype(F32)
        yn = jnp.concatenate([_rms(y[:, hd * RET_DV:(hd + 1) * RET_DV])
                              for hd in range(RET_HEADS)], axis=-1) * gn_ref[...]
        y_ret = _dot((rg_ref[g].astype(F32) * yn).astype(BF16), wro_ref[...])

        u = cu_ref[g].astype(F32)
        ridx = lax.broadcasted_iota(jnp.int32, u.shape, 0)
        prev_row = cup_ref[g].astype(F32)[BF16_SUBLANES - 1:, :] * jnp.where(seg_first, 0.0, 1.0)
        next_row = cun_ref[g].astype(F32)[0:1, :] * jnp.where(seg_last, 0.0, 1.0)
        u_prev = jnp.where(ridx == 0, prev_row, pltpu.roll(u, 1, 0))
        u_next = jnp.where(ridx == tm - 1, next_row, pltpu.roll(u, tm - 1, 0))
        conv = u_prev * cw_ref[0:1, :] + u * cw_ref[1:2, :] + u_next * cw_ref[2:3, :]
        y_conv = _dot((cb_ref[g].astype(F32) * conv).astype(BF16), wco_ref[...])

        y_mla = _dot(om_ref[g], wmo_ref[...])

        merged = (gt_ref[g, :, 0:d_model].astype(F32) * y_ret
                  + gt_ref[g, :, d_model:2 * d_model].astype(F32) * y_conv
                  + gt_ref[g, :, 2 * d_model:3 * d_model].astype(F32) * y_mla)
        x_mid = x_ref[g] + g1_ref[g] * _dot(merged.astype(BF16), wout_ref[...])
        xmid_ref[g] = x_mid

        h2 = _rms(x_mid) * a2_ref[g] + b2_ref[g]
        h_hi, h_lo = _split_bf16(h2)
        lg_ref[g] = _dot(h_hi, wrh_ref[...]) + _dot(h_hi, wrl_ref[...]) + _dot(h_lo, wrh_ref[...])
        h2_ref[g] = _pack_pairs(h_hi)


def _top2_of4(v):
    def first_max(rows):
        best, idx = rows[0], jnp.zeros(rows[0].shape, jnp.int32)
        for e in range(1, len(rows)):
            better = rows[e] > best
            idx = jnp.where(better, e, idx)
            best = jnp.where(better, rows[e], best)
        return best, idx

    b1, i1 = first_max(v)
    b2, i2 = first_max([jnp.where(i1 == e, -jnp.inf, v[e]) for e in range(len(v))])
    return i1, i2, b1, b2


def _route_rows(logits_t, bias):
    scores = jax.nn.sigmoid(logits_t)
    biased = scores + bias
    row = lambda a, e: a[e:e + 1, :]
    best = None
    for g in range(N_GROUPS):
        v = [row(biased, g * EXPERTS_PER_GROUP + e) for e in range(EXPERTS_PER_GROUP)]
        i1, i2, b1, b2 = _top2_of4(v)
        cand = (b1 + b2, jnp.full(i1.shape, g, jnp.int32), i1, i2)
        if best is None:
            best = cand
        else:
            better = cand[0] > best[0]
            best = tuple(jnp.where(better, c, o) for c, o in zip(cand, best))
    _, g_sel, i1, i2 = best
    lo = jnp.minimum(i1, i2)
    hi = jnp.maximum(i1, i2)
    e_lo = g_sel * EXPERTS_PER_GROUP + lo
    e_hi = g_sel * EXPERTS_PER_GROUP + hi
    s_lo = jnp.zeros_like(best[0])
    s_hi = jnp.zeros_like(best[0])
    for e in range(N_EXPERTS):
        s_lo = jnp.where(e_lo == e, row(scores, e), s_lo)
        s_hi = jnp.where(e_hi == e, row(scores, e), s_hi)
    total = s_lo + s_hi
    pair_base = jnp.where(lo == 0, 0, jnp.where(lo == 1, 3, 5))
    cls = g_sel * len(PAIRS) + pair_base + (hi - lo - 1)
    return cls, s_lo / total, s_hi / total


def _block_ranks(cls_row):
    n = cls_row.shape[1]
    cls_col = jnp.transpose(jnp.broadcast_to(cls_row, (LANES, n)))[:, 0:1]
    ii = lax.broadcasted_iota(jnp.int32, (n, n), 0)
    jj = lax.broadcasted_iota(jnp.int32, (n, n), 1)
    earlier_same = jnp.logical_and(cls_col == cls_row, ii < jj)
    rank = jnp.sum(jnp.where(earlier_same, 1.0, 0.0), axis=0, keepdims=True)
    lane = lax.broadcasted_iota(jnp.int32, (1, n), 1)
    counts = jnp.zeros((1, n), F32)
    for c in range(N_CLASSES):
        cnt = jnp.sum(jnp.where(cls_row == c, 1.0, 0.0), axis=1, keepdims=True)
        counts = jnp.where(lane == c, cnt, counts)
    return rank, counts


def _merge_call(yf, yb, rg, cb, cu, om, gt, xa, g1, a2, b2, wts, *, n_batch, nj, ncb):
    rows, d = xa.shape
    tm = ROW_BLOCK
    t_all = nj * tm
    nb = rows // tm
    halo = BF16_SUBLANES
    per_blk = tm // halo
    grp = math.gcd(n_batch, MERGE_BATCH_GROUP)
    per_batch = lambda a: a.reshape(n_batch, t_all, a.shape[-1])
    as_mod = lambda a: a.reshape(n_batch, 2, 1, d)
    blk = lambda j: jnp.minimum(j, nj - 1)
    row = lambda b, j: (b, blk(j), 0)
    late = lambda b, j: (b, jnp.maximum(j - 1, 0), 0)
    mod = lambda b, j: (b, (blk(j) >= ncb).astype(jnp.int32), 0, 0)
    prev = lambda b, j: (b, jnp.maximum(blk(j) * per_blk - 1, 0), 0)
    nxt = lambda b, j: (b, jnp.minimum((blk(j) + 1) * per_blk, t_all // halo - 1), 0)
    wide = lambda w, im=row: pl.BlockSpec((grp, tm, w), im)
    mod_spec = pl.BlockSpec((grp, None, 1, d), mod)
    in_specs = [wide(RV_W), wide(RV_W), wide(RV_W), wide(CONV_WIDTH), wide(CONV_WIDTH),
                pl.BlockSpec((grp, halo, CONV_WIDTH), prev), pl.BlockSpec((grp, halo, CONV_WIDTH), nxt),
                wide(MLA_O), wide(N_BRANCH * d), wide(d), mod_spec, mod_spec, mod_spec]
    in_specs += [_const_spec(w.shape) for w in wts]
    cu3 = per_batch(cu)
    xmid, h2, route, gate_w = pl.pallas_call(
        functools.partial(_merge_body, nj=nj, ncb=ncb, d_model=d),
        out_shape=[jax.ShapeDtypeStruct((n_batch, t_all, d), F32),
                   jax.ShapeDtypeStruct((n_batch, t_all, d // 2), jnp.uint32),
                   jax.ShapeDtypeStruct((n_batch, nj * ROUTE_ROWS, tm), F32),
                   jax.ShapeDtypeStruct((n_batch, t_all, LANES), F32)],
        grid=(n_batch // grp, nj + 1),
        in_specs=in_specs,
        out_specs=[wide(d), wide(d // 2), pl.BlockSpec((grp, ROUTE_ROWS, tm), late),
                   wide(LANES, late)],
        scratch_shapes=[pltpu.VMEM((grp, tm, LANES), F32)],
        compiler_params=_params(2),
        name="merge_out_proj",
    )(per_batch(yf), per_batch(yb), per_batch(rg), per_batch(cb), cu3, cu3, cu3, per_batch(om),
      per_batch(gt), per_batch(xa), as_mod(g1), as_mod(a2), as_mod(b2), *wts)
    return (xmid.reshape(rows, d), h2.reshape(rows, d // 2), route.reshape(nb, ROUTE_ROWS, tm),
            gate_w.reshape(rows, LANES))


def _moe_body(ea_ref, eb_ref, nu_ref, h_ref, gw_ref, w1a_ref, w3a_ref, w2a_ref, w1b_ref, w3b_ref,
              w2b_ref, o_ref, w13a_s, w2a_s, w13b_s, w2b_s):
    t = pl.program_id(0)
    used = t < nu_ref[0]
    prev = jnp.maximum(t - 1, 0)

    def refresh(e_ref, w1_ref, w3_ref, w2_ref, w13_s, w2_s):
        @pl.when(jnp.logical_and(used, jnp.logical_or(t == 0, e_ref[t] != e_ref[prev])))
        def _():
            w13_s[:, :D_EXPERT] = w1_ref[...].astype(BF16)
            w13_s[:, D_EXPERT:] = w3_ref[...].astype(BF16)
            w2_s[...] = w2_ref[...].astype(BF16)

    refresh(ea_ref, w1a_ref, w3a_ref, w2a_ref, w13a_s, w2a_s)
    refresh(eb_ref, w1b_ref, w3b_ref, w2b_ref, w13b_s, w2b_s)

    @pl.when(used)
    def _():
        h = _unpack_pairs(h_ref[...]).astype(BF16)
        gw = gw_ref[...]

        def expert(w13_s, w2_s, wt):
            a = _dot(h, w13_s[...])
            act = _silu(a[:, :D_EXPERT]) * a[:, D_EXPERT:] * wt
            return _dot(act.astype(BF16), w2_s[...])

        o_ref[...] = _pack_pairs(expert(w13a_s, w2a_s, gw[:, 0:1])
                                 + expert(w13b_s, w2b_s, gw[:, 1:2]))

    @pl.when(jnp.logical_not(used))
    def _():
        o_ref[...] = jnp.zeros_like(o_ref)


def _moe_call(tile_ea, tile_eb, n_used, hs, gw, w1, w3, w2, layer):
    npad = hs.shape[0]
    d = w1.shape[2]
    tmo = MOE_TILE
    row = lambda t, ea, eb, nu: (t, 0)
    wa = lambda t, ea, eb, nu: (layer, ea[t], 0, 0)
    wb = lambda t, ea, eb, nu: (layer, eb[t], 0, 0)
    up = lambda im: pl.BlockSpec((None, None, d, D_EXPERT), im)
    down = lambda im: pl.BlockSpec((None, None, D_EXPERT, d), im)
    grid_spec = pltpu.PrefetchScalarGridSpec(
        num_scalar_prefetch=3,
        grid=(npad // tmo,),
        in_specs=[pl.BlockSpec((tmo, d // 2), row), pl.BlockSpec((tmo, LANES), row),
                  up(wa), up(wa), down(wa), up(wb), up(wb), down(wb)],
        out_specs=pl.BlockSpec((tmo, d // 2), row),
        scratch_shapes=[pltpu.VMEM((d, 2 * D_EXPERT), BF16), pltpu.VMEM((D_EXPERT, d), BF16)] * 2)
    return pl.pallas_call(
        _moe_body,
        out_shape=jax.ShapeDtypeStruct((npad, d // 2), jnp.uint32),
        grid_spec=grid_spec,
        compiler_params=_params(1),
        name="moe_experts",
    )(tile_ea, tile_eb, n_used, hs, gw, w1, w3, w2, w1, w3, w2)


def _dispatch(route, n_tok):
    tmo = MOE_TILE
    n_tiles = n_tok // tmo + N_CLASSES
    npad = n_tiles * tmo
    cls = route[:, 0, :].astype(jnp.int32)
    rank = route[:, 1, :].astype(jnp.int32)
    counts = route[:, 2, :N_CLASSES].astype(jnp.int32)
    tiles_per = (jnp.sum(counts, axis=0) + tmo - 1) // tmo
    tile_end = jnp.cumsum(tiles_per)
    offs = (tile_end - tiles_per) * tmo
    block_base = offs[None, :] + jnp.cumsum(counts, axis=0) - counts
    classes = jnp.arange(N_CLASSES, dtype=jnp.int32)
    base = jnp.sum(jnp.where(cls[:, :, None] == classes, block_base[:, None, :], 0), axis=-1)
    dest = (base + rank).reshape(-1)
    tile_ids = jnp.arange(n_tiles, dtype=jnp.int32)
    tile_cls = jnp.sum((tile_end[None, :] <= tile_ids[:, None]).astype(jnp.int32), axis=1)
    tile_cls = jnp.minimum(tile_cls, N_CLASSES - 1)
    pa = jnp.asarray([p[0] for p in PAIRS], jnp.int32)
    pb = jnp.asarray([p[1] for p in PAIRS], jnp.int32)
    group = (tile_cls // len(PAIRS)) * EXPERTS_PER_GROUP
    tile_ea = group + pa[tile_cls % len(PAIRS)]
    tile_eb = group + pb[tile_cls % len(PAIRS)]
    n_used = tile_end[-1:].astype(jnp.int32)
    return dest, npad, tile_ea, tile_eb, n_used


def _sc_mesh():
    return plsc.VectorSubcoreMesh(core_axis_name="c", subcore_axis_name="s",
                                  num_cores=SC_CORES, num_subcores=SC_SUBCORES)


def _sc_chunks(idx):
    assert idx.shape[0] % (SC_WORKERS * GATHER_CHUNK) == 0
    return idx.reshape(SC_WORKERS, -1, GATHER_CHUNK)


def _sc_pipeline(n_chunks, fetch, put):
    assert n_chunks % 2 == 0
    fetch(0, 0).start()

    @pl.loop(0, n_chunks, step=2)
    def _(c0):
        for b in (0, 1):
            c = c0 + b
            fetch(c, b).wait()

            @pl.when(c >= 1)
            def _():
                put(c - 1, 1 - b).wait()

            @pl.when(c + 1 < n_chunks)
            def _():
                fetch(c + 1, 1 - b).start()

            put(c, b).start()

    put(n_chunks - 1, 1).wait()


def _sc_scratch(n_chunks, d, dtype):
    return [pltpu.VMEM((n_chunks, GATHER_CHUNK), jnp.int32),
            pltpu.VMEM((2, GATHER_CHUNK, d), dtype),
            pltpu.SemaphoreType.DMA((2,)), pltpu.SemaphoreType.DMA((2,))]


def _sc_gather_rows(table, idx):
    d = table.shape[1]
    idx3 = _sc_chunks(idx)
    n_chunks = idx3.shape[1]
    assert table.dtype.itemsize == 4

    @functools.partial(
        pl.kernel, mesh=_sc_mesh(),
        out_type=jax.ShapeDtypeStruct((idx.shape[0], d), table.dtype),
        scratch_types=_sc_scratch(n_chunks, d, table.dtype),
        name="sc_row_gather")
    def gather(table_hbm, idx_hbm, out_hbm, idx_v, rows_v, sem_in, sem_out):
        worker = lax.axis_index("s") * SC_CORES + lax.axis_index("c")
        pltpu.sync_copy(idx_hbm.at[worker], idx_v)

        def fetch(c, b):
            return pltpu.make_async_copy(table_hbm.at[idx_v.at[c]], rows_v.at[b], sem_in.at[b])

        def put(c, b):
            off = (worker * n_chunks + c) * GATHER_CHUNK
            return pltpu.make_async_copy(rows_v.at[b], out_hbm.at[pl.ds(off, GATHER_CHUNK)],
                                         sem_out.at[b])

        _sc_pipeline(n_chunks, fetch, put)

    return gather(table, idx3)


def _sc_scatter_rows(rows, idx, n_out):
    d = rows.shape[1]
    idx3 = _sc_chunks(idx)
    n_chunks = idx3.shape[1]
    assert rows.dtype.itemsize == 4

    @functools.partial(
        pl.kernel, mesh=_sc_mesh(),
        out_type=jax.ShapeDtypeStruct((n_out, d), rows.dtype),
        scratch_types=_sc_scratch(n_chunks, d, rows.dtype),
        name="sc_row_scatter")
    def scatter(rows_hbm, idx_hbm, out_hbm, idx_v, rows_v, sem_in, sem_out):
        worker = lax.axis_index("s") * SC_CORES + lax.axis_index("c")
        pltpu.sync_copy(idx_hbm.at[worker], idx_v)

        def fetch(c, b):
            off = (worker * n_chunks + c) * GATHER_CHUNK
            return pltpu.make_async_copy(rows_hbm.at[pl.ds(off, GATHER_CHUNK)], rows_v.at[b],
                                         sem_in.at[b])

        def put(c, b):
            return pltpu.make_async_copy(rows_v.at[b], out_hbm.at[idx_v.at[c]], sem_out.at[b])

        _sc_pipeline(n_chunks, fetch, put)

    return scatter(rows, idx3)


def _final_body(x_ref, f_ref, g2_ref, fn_ref, o_ref):
    for g in range(x_ref.shape[0]):
        o_ref[g] = _rms(x_ref[g] + g2_ref[g] * _unpack_pairs(f_ref[g])) * fn_ref[...]


def _final_call(xmid, fg, g2, final_norm, *, n_batch, nj, ncb):
    d = xmid.shape[1]
    tm = ROW_BLOCK
    njl = nj - ncb
    grp = math.gcd(n_batch, FINAL_BATCH_GROUP)
    per_batch = lambda a: a.reshape(n_batch, nj * tm, a.shape[-1])
    src = lambda b, j: (b, ncb + j, 0)
    out = pl.pallas_call(
        _final_body,
        out_shape=jax.ShapeDtypeStruct((n_batch, njl * tm, d), F32),
        grid=(n_batch // grp, njl),
        in_specs=[pl.BlockSpec((grp, tm, d), src), pl.BlockSpec((grp, tm, d // 2), src),
                  pl.BlockSpec((grp, None, 1, d), lambda b, j: (b, 1, 0, 0)),
                  _const_spec((1, d))],
        out_specs=pl.BlockSpec((grp, tm, d), lambda b, j: (b, j, 0)),
        compiler_params=_params(2),
        name="final_norm",
    )(per_batch(xmid), per_batch(fg), g2.reshape(n_batch, 2, 1, d), final_norm.reshape(1, d))
    return out.reshape(n_batch * njl * tm, d)


def _split_cols(w, sizes):
    out, off = [], 0
    for s in sizes:
        out.append(w[:, off:off + s])
        off += s
    out.append(w[:, off:])
    return out


def _layer_weights(w_in, w_uq, w_ukv, q_norm, kv_norm):
    wq, wk, wv, wg, wcb, wcc, wcx, wqd, wkvd, wkr, wgate = _split_cols(w_in, IN_SIZES)
    rope_lanes = lambda w: jnp.pad(w, ((0, 0), (MLA_NOPE, HEAD_PAD - MLA_NOPE - MLA_ROPE)))
    w_ext = jnp.concatenate(
        [wq, wk, wv, wg, wcb, wcc, wcx, wqd, wkvd, rope_lanes(wkr), wgate], axis=1).astype(BF16)

    uq = w_uq.reshape(MLA_Q_RANK, MLA_HEADS, MLA_NOPE + MLA_ROPE)
    tail = HEAD_PAD - MLA_NOPE - MLA_ROPE
    uq_pad = jnp.pad(uq, ((0, 0), (0, 0), (0, tail)))
    wuq = uq_pad.reshape(MLA_Q_RANK, MLA_W).astype(BF16)

    ukv = w_ukv.reshape(MLA_KV_RANK, MLA_HEADS, MLA_NOPE + MLA_V)
    wk_up = jnp.pad(ukv[..., :MLA_NOPE], ((0, 0), (0, 0), (0, HEAD_PAD - MLA_NOPE)))
    wk_up = wk_up.reshape(MLA_KV_RANK, MLA_W).astype(BF16)
    wv_pad = jnp.pad(ukv[..., MLA_NOPE:], ((0, 0), (0, 0), (0, HEAD_PAD - MLA_V)))
    wv_pad = wv_pad.reshape(MLA_KV_RANK, MLA_W).astype(BF16)
    ones_row = jnp.zeros((MLA_HEADS, HEAD_PAD), F32).at[:, MLA_V].set(1.0).reshape(1, MLA_W)
    q_gain = q_norm.astype(F32) * ((MLA_NOPE + MLA_ROPE) ** -0.5 * math.log2(math.e))
    return (w_ext, wuq, wk_up, wv_pad, q_gain.reshape(1, -1),
            kv_norm.reshape(1, -1).astype(F32), ones_row)


def _rotary_tables(n_ctx, n_lat):
    n_all = n_ctx + n_lat
    row = lax.broadcasted_iota(jnp.int32, (n_all, LANES), 0)
    lane = lax.broadcasted_iota(jnp.int32, (n_all, LANES), 1)
    is_ctx = row < n_ctx
    pos = row - n_ctx
    grid_row = pos // GRID_W
    grid_col = pos - grid_row * GRID_W

    def inv_freq(idx, half):
        return ROPE_BASE ** (-idx.astype(F32) / half)

    def table(ang, first_half, live):
        cos = jnp.where(is_ctx, 1.0, jnp.cos(ang))
        sin = jnp.where(is_ctx, 0.0, jnp.where(first_half, -jnp.sin(ang), jnp.sin(ang)))
        return jnp.where(live, cos, 0.0), jnp.where(live, sin, 0.0)

    half = RET_DK // 2
    ang = pos.astype(F32) * inv_freq(lane % half, half)
    cr, sr = table(ang, (lane % RET_DK) < half, True)

    quarter = MLA_ROPE // 4
    rl = lane - MLA_NOPE
    coord = jnp.where(rl < MLA_ROPE // 2, grid_row, grid_col)
    ang = coord.astype(F32) * inv_freq(rl % quarter, quarter)
    cm, sm = table(ang, (rl % (2 * quarter)) < quarter,
                   jnp.logical_and(rl >= 0, rl < MLA_ROPE))
    cm = jnp.where(rl < 0, 1.0, cm)
    return cr, sr, cm, sm


def _retention_consts(ret_decay):
    log_gf = jax.nn.log_sigmoid(ret_decay[0].astype(F32))
    log_gb = jax.nn.log_sigmoid(ret_decay[1].astype(F32))
    idx = jnp.arange(RET_CHUNK, dtype=F32)
    rel = idx[:, None] - idx[None, :]
    dm_f = jnp.where(rel >= 0, jnp.exp(log_gf[:, None, None] * jnp.maximum(rel, 0.0)[None]), 0.0)
    dm_b = jnp.where(rel < 0, jnp.exp(log_gb[:, None, None] * jnp.maximum(-rel, 0.0)[None]), 0.0)

    def lanes(t):
        return jnp.repeat(t, RET_DK, axis=1)

    xi_f = lanes(jnp.exp(log_gf[None, :] * (idx + 1.0)[:, None]))
    zt_f = lanes(jnp.exp(log_gf[None, :] * (RET_CHUNK - 1 - idx)[:, None]))
    xi_b = lanes(jnp.exp(log_gb[None, :] * (RET_CHUNK - idx)[:, None]))
    zt_b = lanes(jnp.exp(log_gb[None, :] * idx[:, None]))
    cd = lambda lg: jnp.broadcast_to(jnp.exp(lg * RET_CHUNK)[:, None, None], (RET_HEADS, 1, RET_DV))
    return dm_f, dm_b, xi_f, zt_f, xi_b, zt_b, cd(log_gf), cd(log_gb)


def kernel(x, c, ctx, c_ctx, w_ada, b_ada, norm1, norm2, w_in, ret_decay, ret_gn, w_ret_o, conv_w,
           w_conv_o, mla_q_norm, w_uq, mla_kv_norm, w_ukv, w_mla_o, w_out, w_router, router_bias,
           w1, w3, w2, final_norm):
    n_batch, n_lat, d = x.shape
    n_ctx = ctx.shape[1]
    depth = w_ada.shape[0]
    t_all = n_ctx + n_lat
    assert n_ctx % ROW_BLOCK == 0 and n_lat % ROW_BLOCK == 0 and n_lat % GRID_W == 0
    nj = t_all // ROW_BLOCK
    ncb = n_ctx // ROW_BLOCK
    n_tok = n_batch * t_all
    assert n_tok % MOE_TILE == 0
    geom = dict(n_batch=n_batch, nj=nj, ncb=ncb)

    cc = jnp.concatenate([c, c_ctx[None, :]], axis=0)
    cc = jnp.pad(cc, ((0, -cc.shape[0] % 8), (0, 0)))
    mod = _ada_call(cc, w_ada, b_ada)[:, :n_batch + 1].reshape(depth, n_batch + 1, N_MOD, d)
    pick = jnp.stack([jnp.full((n_batch,), n_batch, jnp.int32),
                      jnp.arange(n_batch, dtype=jnp.int32)], axis=1).reshape(-1)
    mod = mod[:, pick]

    tabs = _rotary_tables(n_ctx, n_lat)
    wr_hi, wr_lo = _split_bf16(jnp.pad(w_router.astype(F32), ((0, 0), (0, LANES - N_EXPERTS))))
    rbias = router_bias.astype(F32).reshape(N_EXPERTS, 1)

    out = None
    x_parts = (ctx, x)
    for l in range(depth):
        m = mod[l]
        rowvec = lambda v: v.reshape(2 * n_batch, 1, d)
        a1 = rowvec(norm1[l][None, :] * (1.0 + m[:, 1]))
        b1 = rowvec(m[:, 0])
        g1 = rowvec(m[:, 2])
        a2 = rowvec(norm2[l][None, :] * (1.0 + m[:, 4]))
        b2 = rowvec(m[:, 3])
        g2 = rowvec(m[:, 5])

        wts = _layer_weights(w_in[l], w_uq[l], w_ukv[l], mla_q_norm[l], mla_kv_norm[l])
        proj = _inproj_call(x_parts, a1, b1, tabs, wts, **geom)
        rq, rk, rv, rg, cb, cu, qm, km, vm, gt, xa = proj

        yf, yb = _ret_call(rq, rk, rv, _retention_consts(ret_decay[l]), **geom)
        om = _attn_call(qm, km, vm, n_ctx=n_ctx, **geom)

        merge_wts = (ret_gn[l].reshape(1, -1).astype(F32), conv_w[l].T.astype(F32),
                     w_ret_o[l].astype(BF16), w_conv_o[l].astype(BF16), w_mla_o[l].astype(BF16),
                     w_out[l].astype(BF16), wr_hi, wr_lo, rbias)
        xmid, h2, route, gate_w = _merge_call(yf, yb, rg, cb, cu, om, gt, xa, g1, a2, b2, merge_wts,
                                      **geom)
        dest, npad, tile_ea, tile_eb, n_used = _dispatch(route, n_tok)
        hs = _sc_scatter_rows(h2, dest, npad)
        gw = _sc_scatter_rows(gate_w, dest, npad)
        f_sorted = _moe_call(tile_ea, tile_eb, n_used, hs, gw, w1, w3, w2, l)
        fg = _sc_gather_rows(f_sorted, dest)

        if l < depth - 1:
            x_parts = (xmid, fg, g2)
        else:
            out = _final_call(xmid, fg, g2, final_norm, **geom)
    return out.reshape(n_batch, n_lat, d)
```

```python
import functools
import math

import jax
import jax.numpy as jnp
from jax import lax
from jax.experimental import pallas as pl
from jax.experimental.pallas import tpu as pltpu
from jax.experimental.pallas import tpu_sc as plsc

F32 = jnp.float32
BF16 = jnp.bfloat16

GRID_W = 64
RMS_EPS = 1e-6
ROPE_BASE = 10000.0
N_MOD = 6
RET_HEADS = 4
RET_DK = 64
RET_DV = 128
RET_CHUNK = 256
CONV_WIDTH = 512
MLA_HEADS = 8
MLA_Q_RANK = 384
MLA_KV_RANK = 256
MLA_NOPE = 64
MLA_ROPE = 32
MLA_V = 64
N_BRANCH = 3
N_EXPERTS = 16
N_GROUPS = 4
EXPERTS_PER_GROUP = N_EXPERTS // N_GROUPS
D_EXPERT = 512
IN_SIZES = (RET_HEADS * RET_DK, RET_HEADS * RET_DK, RET_HEADS * RET_DV, RET_HEADS * RET_DV,
            CONV_WIDTH, CONV_WIDTH, CONV_WIDTH, MLA_Q_RANK, MLA_KV_RANK, MLA_ROPE)

LANES = 128
BF16_SUBLANES = 16
VMEM_LIMIT = 56 * 1024 * 1024
SC_CORES = 2
SC_SUBCORES = 16
SC_WORKERS = SC_CORES * SC_SUBCORES

HEAD_PAD = LANES
ROW_BLOCK = 256
MOE_TILE = 512
ATTN_KEY_TILES = (2816, 1408, 768, 512, 256, 128)
ROUTE_ROWS = 8
RET_BATCH_GROUP = 4
INPROJ_BATCH_GROUP = 2
MERGE_BATCH_GROUP = 2
FINAL_BATCH_GROUP = 4
GATHER_CHUNK = 96
PAIRS = [(a, b) for a in range(EXPERTS_PER_GROUP) for b in range(a + 1, EXPERTS_PER_GROUP)]
N_CLASSES = N_GROUPS * len(PAIRS)

RQ_W = RET_HEADS * RET_DK
RV_W = RET_HEADS * RET_DV
MLA_W = MLA_HEADS * HEAD_PAD
MLA_O = MLA_HEADS * MLA_V


def _const_spec(shape):
    nd = len(shape)
    return pl.BlockSpec(shape, lambda *_: (0,) * nd, pipeline_mode=pl.Buffered(1))


def _params(n_axes):
    return pltpu.CompilerParams(dimension_semantics=("arbitrary",) * n_axes,
                                vmem_limit_bytes=VMEM_LIMIT)


def _dot(a, b):
    return jnp.dot(a, b, preferred_element_type=F32)


def _split_bf16(a):
    hi = a.astype(BF16)
    lo = (a - hi.astype(F32)).astype(BF16)
    return hi, lo


def _silu(v):
    return v * jax.nn.sigmoid(v)


def _rms(v):
    return v * lax.rsqrt(jnp.mean(v * v, axis=-1, keepdims=True) + RMS_EPS)


def _pack_pairs(v):
    half = v.shape[1] // 2
    bits = lambda t: lax.bitcast_convert_type(t.astype(BF16).astype(F32), jnp.uint32)
    return (bits(v[:, :half]) & jnp.uint32(0xFFFF0000)) | (bits(v[:, half:]) >> 16)


def _unpack_pairs(u):
    hi = lax.bitcast_convert_type(u & jnp.uint32(0xFFFF0000), F32)
    lo = lax.bitcast_convert_type(u << 16, F32)
    return jnp.concatenate([hi, lo], axis=1)


def _ada_body(c_ref, w_ref, b_ref, o_ref):
    a_hi, a_lo = _split_bf16(_silu(c_ref[...]))
    w_hi, w_lo = _split_bf16(w_ref[...])
    o_ref[...] = _dot(a_hi, w_hi) + _dot(a_hi, w_lo) + _dot(a_lo, w_hi) + b_ref[...]


def _ada_call(cc, w_ada, b_ada):
    depth, d, nm = w_ada.shape
    rows = cc.shape[0]
    cb = nm // 4
    return pl.pallas_call(
        _ada_body,
        out_shape=jax.ShapeDtypeStruct((depth, rows, nm), F32),
        grid=(depth, nm // cb),
        in_specs=[pl.BlockSpec((rows, d), lambda l, n: (0, 0)),
                  pl.BlockSpec((None, d, cb), lambda l, n: (l, 0, n)),
                  pl.BlockSpec((None, 1, cb), lambda l, n: (l, 0, n))],
        out_specs=pl.BlockSpec((None, rows, cb), lambda l, n: (l, 0, n)),
        compiler_params=_params(2),
        name="ada_mod",
    )(cc, w_ada, b_ada.reshape(depth, 1, nm))


_O_RQ = 0
_O_RK = _O_RQ + RQ_W
_O_RV = _O_RK + RQ_W
_O_RG = _O_RV + RV_W
_O_CB = _O_RG + RV_W
_O_CC = _O_CB + CONV_WIDTH
_O_CX = _O_CC + CONV_WIDTH
_O_QD = _O_CX + CONV_WIDTH
_O_KVD = _O_QD + MLA_Q_RANK
_O_KR = _O_KVD + MLA_KV_RANK
_O_GT = _O_KR + LANES


def _rot_half(v, half):
    width = v.shape[1]
    lane = lax.broadcasted_iota(jnp.int32, v.shape, 1)
    first = (lane % (2 * half)) < half
    return jnp.where(first, pltpu.roll(v, width - half, 1), pltpu.roll(v, half, 1))


def _inproj_body(*refs, d_model, fused, ncb):
    n_lead = 3 if fused else 2
    lead, refs = refs[:n_lead], refs[n_lead:]
    shared, outs = refs[:13], refs[13:]
    a1_ref, b1_ref = shared[:2]
    for g in range(a1_ref.shape[0]):
        if fused:
            xm_ref, f_ref, g2_ref = lead
            x = xm_ref[g] + g2_ref[g] * _unpack_pairs(f_ref[g])
        else:
            ctx_ref, lat_ref = lead
            x = jnp.where(pl.program_id(0) < ncb, ctx_ref[g], lat_ref[g])
        _inproj_block(x, a1_ref[g], b1_ref[g], *shared[2:], *[o.at[g] for o in outs],
                      d_model=d_model)


def _inproj_block(x, a1, b1, cr_ref, sr_ref, cm_ref, sm_ref, w_ref, wuq_ref, wk_ref, wv_ref, qn_ref,
                  kvn_ref, ones_ref, rq_ref, rk_ref, rv_ref, rg_ref, cb_ref, cu_ref, qm_ref, km_ref,
                  vm_ref, gt_ref, xo_ref, *, d_model):
    xo_ref[...] = x
    h = (_rms(x) * a1 + b1).astype(BF16)

    def mm(off, width):
        return _dot(h, w_ref[:, off:off + width])

    cr = jnp.concatenate([cr_ref[...]] * (RQ_W // cr_ref.shape[1]), axis=1)
    sr = jnp.concatenate([sr_ref[...]] * (RQ_W // sr_ref.shape[1]), axis=1)
    q = mm(_O_RQ, RQ_W)
    rq_ref[...] = (q * cr + _rot_half(q, RET_DK // 2) * sr).astype(BF16)
    k = mm(_O_RK, RQ_W)
    rk_ref[...] = ((k * cr + _rot_half(k, RET_DK // 2) * sr) * (RET_DK ** -0.5)).astype(BF16)
    rv_ref[...] = mm(_O_RV, RV_W).astype(BF16)
    rg_ref[...] = _silu(mm(_O_RG, RV_W)).astype(BF16)
    cb_ref[...] = mm(_O_CB, CONV_WIDTH).astype(BF16)
    cu_ref[...] = (mm(_O_CC, CONV_WIDTH) * mm(_O_CX, CONV_WIDTH)).astype(BF16)

    cm = cm_ref[...]
    sm = sm_ref[...]
    qn = (_rms(mm(_O_QD, MLA_Q_RANK)) * qn_ref[...]).astype(BF16)
    for hd in range(MLA_HEADS):
        lo = hd * HEAD_PAD
        qa = _dot(qn, wuq_ref[:, lo:lo + HEAD_PAD])
        qm_ref[:, lo:lo + HEAD_PAD] = (qa * cm + _rot_half(qa, MLA_ROPE // 4) * sm).astype(BF16)

    kvn = (_rms(mm(_O_KVD, MLA_KV_RANK)) * kvn_ref[...]).astype(BF16)
    kr = mm(_O_KR, HEAD_PAD)
    kr = kr * cm + _rot_half(kr, MLA_ROPE // 4) * sm
    kn = _dot(kvn, wk_ref[...])
    for hd in range(MLA_HEADS):
        lo = hd * HEAD_PAD
        km_ref[:, lo:lo + HEAD_PAD] = (kn[:, lo:lo + HEAD_PAD] + kr).astype(BF16)
    vm_ref[...] = (_dot(kvn, wv_ref[...]) + ones_ref[...]).astype(BF16)

    for br in range(N_BRANCH):
        gt_ref[:, br * d_model:(br + 1) * d_model] = jax.nn.sigmoid(
            mm(_O_GT + br * d_model, d_model)).astype(BF16)


def _inproj_call(x_parts, a1, b1, tabs, wts, *, n_batch, nj, ncb):
    fused = len(x_parts) == 3
    d = x_parts[0].shape[-1]
    tm = ROW_BLOCK
    t_all = nj * tm
    rows = n_batch * t_all
    grp = math.gcd(n_batch, INPROJ_BATCH_GROUP)
    per_batch = lambda a: a.reshape(n_batch, -1, a.shape[-1])
    row = lambda j, b: (b, j, 0)
    mod = lambda j, b: (b, (j >= ncb).astype(jnp.int32), 0, 0)
    mod_spec = pl.BlockSpec((grp, None, 1, d), mod)
    as_mod = lambda a: a.reshape(n_batch, 2, 1, d)
    tab = lambda j, b: (j, 0)
    if fused:
        xm, f, g2 = x_parts
        lead = (per_batch(xm), per_batch(f), as_mod(g2))
        in_specs = [pl.BlockSpec((grp, tm, d), row), pl.BlockSpec((grp, tm, d // 2), row), mod_spec]
    else:
        lead = x_parts
        in_specs = [pl.BlockSpec((grp, tm, d), lambda j, b: (b, jnp.where(j < ncb, j, 0), 0)),
                    pl.BlockSpec((grp, tm, d), lambda j, b: (b, jnp.where(j < ncb, 0, j - ncb), 0))]
    in_specs += [mod_spec, mod_spec]
    in_specs += [pl.BlockSpec((tm, t.shape[1]), tab) for t in tabs]
    in_specs += [_const_spec(w.shape) for w in wts]
    widths = (RQ_W, RQ_W, RV_W, RV_W, CONV_WIDTH, CONV_WIDTH, MLA_W, MLA_W, MLA_W, N_BRANCH * d)
    out_shape = [jax.ShapeDtypeStruct((n_batch, t_all, w), BF16) for w in widths]
    out_specs = [pl.BlockSpec((grp, tm, w), row) for w in widths]
    out_shape.append(jax.ShapeDtypeStruct((n_batch, t_all, d), F32))
    out_specs.append(pl.BlockSpec((grp, tm, d), row))
    outs = pl.pallas_call(
        functools.partial(_inproj_body, d_model=d, fused=fused, ncb=ncb),
        out_shape=out_shape,
        grid=(nj, n_batch // grp),
        in_specs=in_specs,
        out_specs=out_specs,
        compiler_params=_params(2),
        name="in_proj",
    )(*lead, as_mod(a1), as_mod(b1), *tabs, *wts)
    return [o.reshape(rows, o.shape[-1]) for o in outs]


def _ret_direction(q_ref, k_ref, v_ref, y_ref, s_ref, dm_ref, xi_ref, zt_ref, cd_ref, chunk_order):
    states = [s_ref[hd] for hd in range(RET_HEADS)]
    for c in chunk_order:
        rows = slice(c * RET_CHUNK, (c + 1) * RET_CHUNK)
        q = q_ref[rows, :]
        k = k_ref[rows, :]
        v = v_ref[rows, :]
        qx = (q.astype(F32) * xi_ref[...]).astype(BF16)
        kz = (k.astype(F32) * zt_ref[...]).astype(BF16)
        for hd in range(RET_HEADS):
            ks = slice(hd * RET_DK, (hd + 1) * RET_DK)
            vs = slice(hd * RET_DV, (hd + 1) * RET_DV)
            vh = v[:, vs]
            sc = lax.dot_general(q[:, ks], k[:, ks], (((1,), (1,)), ((), ())),
                                 preferred_element_type=F32)
            inner = _dot((sc * dm_ref[hd]).astype(BF16), vh)
            cross = _dot(qx[:, ks], states[hd].astype(BF16))
            y_ref[rows, vs] = (inner + cross).astype(BF16)
            upd = lax.dot_general(kz[:, ks], vh, (((0,), (0,)), ((), ())),
                                  preferred_element_type=F32)
            states[hd] = cd_ref[hd] * states[hd] + upd
    for hd in range(RET_HEADS):
        s_ref[hd] = states[hd]


def _ret_body(qf_ref, kf_ref, vf_ref, qb_ref, kb_ref, vb_ref,
              dmf_ref, dmb_ref, xif_ref, ztf_ref, xib_ref, ztb_ref, cdf_ref, cdb_ref,
              yf_ref, yb_ref, sf_ref, sb_ref):
    @pl.when(pl.program_id(1) == 0)
    def _():
        sf_ref[...] = jnp.zeros_like(sf_ref)
        sb_ref[...] = jnp.zeros_like(sb_ref)

    n_chunks = ROW_BLOCK // RET_CHUNK
    for g in range(qf_ref.shape[0]):
        _ret_direction(qf_ref.at[g], kf_ref.at[g], vf_ref.at[g], yf_ref.at[g], sf_ref.at[g],
                       dmf_ref, xif_ref, ztf_ref, cdf_ref, range(n_chunks))
        _ret_direction(qb_ref.at[g], kb_ref.at[g], vb_ref.at[g], yb_ref.at[g], sb_ref.at[g],
                       dmb_ref, xib_ref, ztb_ref, cdb_ref, range(n_chunks - 1, -1, -1))


def _ret_call(rq, rk, rv, consts, *, n_batch, nj, ncb):
    rows = rq.shape[0]
    tm = ROW_BLOCK
    grp = math.gcd(n_batch, RET_BATCH_GROUP)
    fwd = lambda b, s: (b, s, 0)

    def bwd(b, s):
        return (b, jnp.where(s < ncb, ncb - 1 - s, nj - 1 - (s - ncb)), 0)

    per_batch = lambda a: a.reshape(n_batch, nj * tm, a.shape[1])
    specs = []
    for im in (fwd, bwd):
        specs += [pl.BlockSpec((grp, tm, RQ_W), im), pl.BlockSpec((grp, tm, RQ_W), im),
                  pl.BlockSpec((grp, tm, RV_W), im)]
    specs += [_const_spec(c.shape) for c in consts]
    q3, k3, v3 = per_batch(rq), per_batch(rk), per_batch(rv)
    yf, yb = pl.pallas_call(
        _ret_body,
        out_shape=[jax.ShapeDtypeStruct((n_batch, nj * tm, RV_W), BF16)] * 2,
        grid=(n_batch // grp, nj),
        in_specs=specs,
        out_specs=[pl.BlockSpec((grp, tm, RV_W), fwd), pl.BlockSpec((grp, tm, RV_W), bwd)],
        scratch_shapes=[pltpu.VMEM((grp, RET_HEADS, RET_DK, RET_DV), F32)] * 2,
        compiler_params=_params(2),
        name="retention",
    )(q3, k3, v3, q3, k3, v3, *consts)
    return yf.reshape(rows, RV_W), yb.reshape(rows, RV_W)


def _pick_tile(n, candidates):
    for c in candidates:
        if n % c == 0:
            return c
    raise ValueError(f"no tile for {n}")


def _attn_body(q_ref, k_ref, v_ref, o_ref, s_ref, *, n_ctx, n_all, ncb):
    tq = q_ref.shape[0]
    heads = [slice(hh * HEAD_PAD, (hh + 1) * HEAD_PAD) for hh in range(2)]

    def attend(n_keys, tk):
        nt = n_keys // tk
        qs = [q_ref[:, hs] for hs in heads]

        def qk(t, mrun):
            r0 = pl.multiple_of(t * tk, tk)
            out = []
            for hh, hs in enumerate(heads):
                s = lax.dot_general(qs[hh], k_ref[pl.ds(r0, tk), hs], (((1,), (1,)), ((), ())),
                                    preferred_element_type=F32)
                s_ref[hh, t, :, 0:tk] = s
                m = mrun[hh]
                for cc in range(tk // LANES):
                    m = jnp.maximum(m, s[:, cc * LANES:(cc + 1) * LANES])
                out.append(m)
            return tuple(out)

        mrun = lax.fori_loop(0, nt, qk, tuple(jnp.full((tq, LANES), -jnp.inf, F32) for _ in heads),
                             unroll=True)
        mrow = [jnp.max(m, axis=-1, keepdims=True) for m in mrun]

        def pv(t, accs):
            r0 = pl.multiple_of(t * tk, tk)
            out = []
            for hh, hs in enumerate(heads):
                p = jnp.exp2(s_ref[hh, t, :, 0:tk] - mrow[hh]).astype(BF16)
                out.append(accs[hh] + _dot(p, v_ref[pl.ds(r0, tk), hs]))
            return tuple(out)

        accs = lax.fori_loop(0, nt, pv, tuple(jnp.zeros((tq, HEAD_PAD), F32) for _ in heads),
                             unroll=True)
        o_ref[...] = jnp.concatenate([a[:, :MLA_V] / a[:, MLA_V:MLA_V + 1] for a in accs],
                                     axis=-1).astype(BF16)

    j = pl.program_id(2)

    @pl.when(j < ncb)
    def _():
        attend(n_ctx, _pick_tile(n_ctx, (256, 128)))

    @pl.when(j >= ncb)
    def _():
        attend(n_all, _pick_tile(n_all, ATTN_KEY_TILES))


def _attn_call(qm, km, vm, *, n_batch, nj, ncb, n_ctx):
    rows = qm.shape[0]
    tm = ROW_BLOCK
    t_all = nj * tm
    tk = _pick_tile(t_all, ATTN_KEY_TILES)
    qmap = lambda b, hp, j: (b * nj + j, hp)
    kmap = lambda b, hp, j: (b, hp)
    return pl.pallas_call(
        functools.partial(_attn_body, n_ctx=n_ctx, n_all=t_all, ncb=ncb),
        out_shape=jax.ShapeDtypeStruct((rows, MLA_O), BF16),
        grid=(n_batch, MLA_HEADS // 2, nj),
        in_specs=[pl.BlockSpec((tm, 2 * HEAD_PAD), qmap),
                  pl.BlockSpec((t_all, 2 * HEAD_PAD), kmap),
                  pl.BlockSpec((t_all, 2 * HEAD_PAD), kmap)],
        out_specs=pl.BlockSpec((tm, 2 * MLA_V), qmap),
        scratch_shapes=[pltpu.VMEM((2, t_all // tk, tm, tk), F32)],
        compiler_params=_params(3),
        name="mla_attention",
    )(qm, km, vm)


def _merge_body(yf_ref, yb_ref, rg_ref, cb_ref, cu_ref, cup_ref, cun_ref, om_ref, gt_ref, x_ref,
                g1_ref, a2_ref, b2_ref, gn_ref, cw_ref, wro_ref, wco_ref, wmo_ref, wout_ref,
                wrh_ref, wrl_ref, rb_ref, xmid_ref, h2_ref, rt_ref, gw_ref, lg_ref, *, nj, ncb, d_model):
    tm = x_ref.shape[1]

    @pl.when(jnp.logical_and(pl.program_id(0) == 0, pl.program_id(1) == 0))
    def _():
        lg_ref[...] = jnp.zeros_like(lg_ref)

    j = jnp.minimum(pl.program_id(1), nj - 1)
    seg_first = jnp.logical_or(j == 0, j == ncb)
    seg_last = jnp.logical_or(j == ncb - 1, j == nj - 1)

    for g in range(x_ref.shape[0]):
        cls, w_lo, w_hi = _route_rows(jnp.transpose(lg_ref[g])[:N_EXPERTS, :], rb_ref[...])
        rank, counts = _block_ranks(cls.astype(F32))
        rt_ref[g] = jnp.concatenate(
            [cls.astype(F32), rank, counts, jnp.zeros((ROUTE_ROWS - 3, tm), F32)], axis=0)
        gw_ref[g] = jnp.transpose(
            jnp.concatenate([w_lo, w_hi, jnp.zeros((LANES - 2, tm), F32)], axis=0))

        y = yf_ref[g].astype(F32) + yb_ref[g].astype(F32)
        yn = jnp.concatenate([_rms(y[:, hd * RET_DV:(hd + 1) * RET_DV])
                              for hd in range(RET_HEADS)], axis=-1) * gn_ref[...]
        y_ret = _dot((rg_ref[g].astype(F32) * yn).astype(BF16), wro_ref[...])

        u = cu_ref[g].astype(F32)
        ridx = lax.broadcasted_iota(jnp.int32, u.shape, 0)
        prev_row = cup_ref[g].astype(F32)[BF16_SUBLANES - 1:, :] * jnp.where(seg_first, 0.0, 1.0)
        next_row = cun_ref[g].astype(F32)[0:1, :] * jnp.where(seg_last, 0.0, 1.0)
        u_prev = jnp.where(ridx == 0, prev_row, pltpu.roll(u, 1, 0))
        u_next = jnp.where(ridx == tm - 1, next_row, pltpu.roll(u, tm - 1, 0))
        conv = u_prev * cw_ref[0:1, :] + u * cw_ref[1:2, :] + u_next * cw_ref[2:3, :]
        y_conv = _dot((cb_ref[g].astype(F32) * conv).astype(BF16), wco_ref[...])

        y_mla = _dot(om_ref[g], wmo_ref[...])

        merged = (gt_ref[g, :, 0:d_model].astype(F32) * y_ret
                  + gt_ref[g, :, d_model:2 * d_model].astype(F32) * y_conv
                  + gt_ref[g, :, 2 * d_model:3 * d_model].astype(F32) * y_mla)
        x_mid = x_ref[g] + g1_ref[g] * _dot(merged.astype(BF16), wout_ref[...])
        xmid_ref[g] = x_mid

        h2 = _rms(x_mid) * a2_ref[g] + b2_ref[g]
        h_hi, h_lo = _split_bf16(h2)
        lg_ref[g] = _dot(h_hi, wrh_ref[...]) + _dot(h_hi, wrl_ref[...]) + _dot(h_lo, wrh_ref[...])
        h2_ref[g] = _pack_pairs(h_hi)


def _top2_of4(v):
    def first_max(rows):
        best, idx = rows[0], jnp.zeros(rows[0].shape, jnp.int32)
        for e in range(1, len(rows)):
            better = rows[e] > best
            idx = jnp.where(better, e, idx)
            best = jnp.where(better, rows[e], best)
        return best, idx

    b1, i1 = first_max(v)
    b2, i2 = first_max([jnp.where(i1 == e, -jnp.inf, v[e]) for e in range(len(v))])
    return i1, i2, b1, b2


def _route_rows(logits_t, bias):
    scores = jax.nn.sigmoid(logits_t)
    biased = scores + bias
    row = lambda a, e: a[e:e + 1, :]
    best = None
    for g in range(N_GROUPS):
        v = [row(biased, g * EXPERTS_PER_GROUP + e) for e in range(EXPERTS_PER_GROUP)]
        i1, i2, b1, b2 = _top2_of4(v)
        cand = (b1 + b2, jnp.full(i1.shape, g, jnp.int32), i1, i2)
        if best is None:
            best = cand
        else:
            better = cand[0] > best[0]
            best = tuple(jnp.where(better, c, o) for c, o in zip(cand, best))
    _, g_sel, i1, i2 = best
    lo = jnp.minimum(i1, i2)
    hi = jnp.maximum(i1, i2)
    e_lo = g_sel * EXPERTS_PER_GROUP + lo
    e_hi = g_sel * EXPERTS_PER_GROUP + hi
    s_lo = jnp.zeros_like(best[0])
    s_hi = jnp.zeros_like(best[0])
    for e in range(N_EXPERTS):
        s_lo = jnp.where(e_lo == e, row(scores, e), s_lo)
        s_hi = jnp.where(e_hi == e, row(scores, e), s_hi)
    total = s_lo + s_hi
    pair_base = jnp.where(lo == 0, 0, jnp.where(lo == 1, 3, 5))
    cls = g_sel * len(PAIRS) + pair_base + (hi - lo - 1)
    return cls, s_lo / total, s_hi / total


def _block_ranks(cls_row):
    n = cls_row.shape[1]
    cls_col = jnp.transpose(jnp.broadcast_to(cls_row, (LANES, n)))[:, 0:1]
    ii = lax.broadcasted_iota(jnp.int32, (n, n), 0)
    jj = lax.broadcasted_iota(jnp.int32, (n, n), 1)
    earlier_same = jnp.logical_and(cls_col == cls_row, ii < jj)
    rank = jnp.sum(jnp.where(earlier_same, 1.0, 0.0), axis=0, keepdims=True)
    lane = lax.broadcasted_iota(jnp.int32, (1, n), 1)
    counts = jnp.zeros((1, n), F32)
    for c in range(N_CLASSES):
        cnt = jnp.sum(jnp.where(cls_row == c, 1.0, 0.0), axis=1, keepdims=True)
        counts = jnp.where(lane == c, cnt, counts)
    return rank, counts


def _merge_call(yf, yb, rg, cb, cu, om, gt, xa, g1, a2, b2, wts, *, n_batch, nj, ncb):
    rows, d = xa.shape
    tm = ROW_BLOCK
    t_all = nj * tm
    nb = rows // tm
    halo = BF16_SUBLANES
    per_blk = tm // halo
    grp = math.gcd(n_batch, MERGE_BATCH_GROUP)
    per_batch = lambda a: a.reshape(n_batch, t_all, a.shape[-1])
    as_mod = lambda a: a.reshape(n_batch, 2, 1, d)
    blk = lambda j: jnp.minimum(j, nj - 1)
    row = lambda b, j: (b, blk(j), 0)
    late = lambda b, j: (b, jnp.maximum(j - 1, 0), 0)
    mod = lambda b, j: (b, (blk(j) >= ncb).astype(jnp.int32), 0, 0)
    prev = lambda b, j: (b, jnp.maximum(blk(j) * per_blk - 1, 0), 0)
    nxt = lambda b, j: (b, jnp.minimum((blk(j) + 1) * per_blk, t_all // halo - 1), 0)
    wide = lambda w, im=row: pl.BlockSpec((grp, tm, w), im)
    mod_spec = pl.BlockSpec((grp, None, 1, d), mod)
    in_specs = [wide(RV_W), wide(RV_W), wide(RV_W), wide(CONV_WIDTH), wide(CONV_WIDTH),
                pl.BlockSpec((grp, halo, CONV_WIDTH), prev), pl.BlockSpec((grp, halo, CONV_WIDTH), nxt),
                wide(MLA_O), wide(N_BRANCH * d), wide(d), mod_spec, mod_spec, mod_spec]
    in_specs += [_const_spec(w.shape) for w in wts]
    cu3 = per_batch(cu)
    xmid, h2, route, gate_w = pl.pallas_call(
        functools.partial(_merge_body, nj=nj, ncb=ncb, d_model=d),
        out_shape=[jax.ShapeDtypeStruct((n_batch, t_all, d), F32),
                   jax.ShapeDtypeStruct((n_batch, t_all, d // 2), jnp.uint32),
                   jax.ShapeDtypeStruct((n_batch, nj * ROUTE_ROWS, tm), F32),
                   jax.ShapeDtypeStruct((n_batch, t_all, LANES), F32)],
        grid=(n_batch // grp, nj + 1),
        in_specs=in_specs,
        out_specs=[wide(d), wide(d // 2), pl.BlockSpec((grp, ROUTE_ROWS, tm), late),
                   wide(LANES, late)],
        scratch_shapes=[pltpu.VMEM((grp, tm, LANES), F32)],
        compiler_params=_params(2),
        name="merge_out_proj",
    )(per_batch(yf), per_batch(yb), per_batch(rg), per_batch(cb), cu3, cu3, cu3, per_batch(om),
      per_batch(gt), per_batch(xa), as_mod(g1), as_mod(a2), as_mod(b2), *wts)
    return (xmid.reshape(rows, d), h2.reshape(rows, d // 2), route.reshape(nb, ROUTE_ROWS, tm),
            gate_w.reshape(rows, LANES))


def _moe_body(ea_ref, eb_ref, nu_ref, h_ref, gw_ref, w1a_ref, w3a_ref, w2a_ref, w1b_ref, w3b_ref,
              w2b_ref, o_ref, w13a_s, w2a_s, w13b_s, w2b_s):
    t = pl.program_id(0)
    used = t < nu_ref[0]
    prev = jnp.maximum(t - 1, 0)

    def refresh(e_ref, w1_ref, w3_ref, w2_ref, w13_s, w2_s):
        @pl.when(jnp.logical_and(used, jnp.logical_or(t == 0, e_ref[t] != e_ref[prev])))
        def _():
            w13_s[:, :D_EXPERT] = w1_ref[...].astype(BF16)
            w13_s[:, D_EXPERT:] = w3_ref[...].astype(BF16)
            w2_s[...] = w2_ref[...].astype(BF16)

    refresh(ea_ref, w1a_ref, w3a_ref, w2a_ref, w13a_s, w2a_s)
    refresh(eb_ref, w1b_ref, w3b_ref, w2b_ref, w13b_s, w2b_s)

    @pl.when(used)
    def _():
        h = _unpack_pairs(h_ref[...]).astype(BF16)
        gw = gw_ref[...]

        def expert(w13_s, w2_s, wt):
            a = _dot(h, w13_s[...])
            act = _silu(a[:, :D_EXPERT]) * a[:, D_EXPERT:] * wt
            return _dot(act.astype(BF16), w2_s[...])

        o_ref[...] = _pack_pairs(expert(w13a_s, w2a_s, gw[:, 0:1])
                                 + expert(w13b_s, w2b_s, gw[:, 1:2]))

    @pl.when(jnp.logical_not(used))
    def _():
        o_ref[...] = jnp.zeros_like(o_ref)


def _moe_call(tile_ea, tile_eb, n_used, hs, gw, w1, w3, w2, layer):
    npad = hs.shape[0]
    d = w1.shape[2]
    tmo = MOE_TILE
    row = lambda t, ea, eb, nu: (t, 0)
    wa = lambda t, ea, eb, nu: (layer, ea[t], 0, 0)
    wb = lambda t, ea, eb, nu: (layer, eb[t], 0, 0)
    up = lambda im: pl.BlockSpec((None, None, d, D_EXPERT), im)
    down = lambda im: pl.BlockSpec((None, None, D_EXPERT, d), im)
    grid_spec = pltpu.PrefetchScalarGridSpec(
        num_scalar_prefetch=3,
        grid=(npad // tmo,),
        in_specs=[pl.BlockSpec((tmo, d // 2), row), pl.BlockSpec((tmo, LANES), row),
                  up(wa), up(wa), down(wa), up(wb), up(wb), down(wb)],
        out_specs=pl.BlockSpec((tmo, d // 2), row),
        scratch_shapes=[pltpu.VMEM((d, 2 * D_EXPERT), BF16), pltpu.VMEM((D_EXPERT, d), BF16)] * 2)
    return pl.pallas_call(
        _moe_body,
        out_shape=jax.ShapeDtypeStruct((npad, d // 2), jnp.uint32),
        grid_spec=grid_spec,
        compiler_params=_params(1),
        name="moe_experts",
    )(tile_ea, tile_eb, n_used, hs, gw, w1, w3, w2, w1, w3, w2)


def _dispatch(route, n_tok):
    tmo = MOE_TILE
    n_tiles = n_tok // tmo + N_CLASSES
    npad = n_tiles * tmo
    cls = route[:, 0, :].astype(jnp.int32)
    rank = route[:, 1, :].astype(jnp.int32)
    counts = route[:, 2, :N_CLASSES].astype(jnp.int32)
    tiles_per = (jnp.sum(counts, axis=0) + tmo - 1) // tmo
    tile_end = jnp.cumsum(tiles_per)
    offs = (tile_end - tiles_per) * tmo
    block_base = offs[None, :] + jnp.cumsum(counts, axis=0) - counts
    classes = jnp.arange(N_CLASSES, dtype=jnp.int32)
    base = jnp.sum(jnp.where(cls[:, :, None] == classes, block_base[:, None, :], 0), axis=-1)
    dest = (base + rank).reshape(-1)
    tile_ids = jnp.arange(n_tiles, dtype=jnp.int32)
    tile_cls = jnp.sum((tile_end[None, :] <= tile_ids[:, None]).astype(jnp.int32), axis=1)
    tile_cls = jnp.minimum(tile_cls, N_CLASSES - 1)
    pa = jnp.asarray([p[0] for p in PAIRS], jnp.int32)
    pb = jnp.asarray([p[1] for p in PAIRS], jnp.int32)
    group = (tile_cls // len(PAIRS)) * EXPERTS_PER_GROUP
    tile_ea = group + pa[tile_cls % len(PAIRS)]
    tile_eb = group + pb[tile_cls % len(PAIRS)]
    n_used = tile_end[-1:].astype(jnp.int32)
    return dest, npad, tile_ea, tile_eb, n_used


def _sc_mesh():
    return plsc.VectorSubcoreMesh(core_axis_name="c", subcore_axis_name="s",
                                  num_cores=SC_CORES, num_subcores=SC_SUBCORES)


def _sc_chunks(idx):
    assert idx.shape[0] % (SC_WORKERS * GATHER_CHUNK) == 0
    return idx.reshape(SC_WORKERS, -1, GATHER_CHUNK)


def _sc_pipeline(n_chunks, fetch, put):
    assert n_chunks % 2 == 0
    fetch(0, 0).start()

    @pl.loop(0, n_chunks, step=2)
    def _(c0):
        for b in (0, 1):
            c = c0 + b
            fetch(c, b).wait()

            @pl.when(c >= 1)
            def _():
                put(c - 1, 1 - b).wait()

            @pl.when(c + 1 < n_chunks)
            def _():
                fetch(c + 1, 1 - b).start()

            put(c, b).start()

    put(n_chunks - 1, 1).wait()


def _sc_scratch(n_chunks, d, dtype):
    return [pltpu.VMEM((n_chunks, GATHER_CHUNK), jnp.int32),
            pltpu.VMEM((2, GATHER_CHUNK, d), dtype),
            pltpu.SemaphoreType.DMA((2,)), pltpu.SemaphoreType.DMA((2,))]


def _sc_gather_rows(table, idx):
    d = table.shape[1]
    idx3 = _sc_chunks(idx)
    n_chunks = idx3.shape[1]
    assert table.dtype.itemsize == 4

    @functools.partial(
        pl.kernel, mesh=_sc_mesh(),
        out_type=jax.ShapeDtypeStruct((idx.shape[0], d), table.dtype),
        scratch_types=_sc_scratch(n_chunks, d, table.dtype),
        name="sc_row_gather")
    def gather(table_hbm, idx_hbm, out_hbm, idx_v, rows_v, sem_in, sem_out):
        worker = lax.axis_index("s") * SC_CORES + lax.axis_index("c")
        pltpu.sync_copy(idx_hbm.at[worker], idx_v)

        def fetch(c, b):
            return pltpu.make_async_copy(table_hbm.at[idx_v.at[c]], rows_v.at[b], sem_in.at[b])

        def put(c, b):
            off = (worker * n_chunks + c) * GATHER_CHUNK
            return pltpu.make_async_copy(rows_v.at[b], out_hbm.at[pl.ds(off, GATHER_CHUNK)],
                                         sem_out.at[b])

        _sc_pipeline(n_chunks, fetch, put)

    return gather(table, idx3)


def _sc_scatter_rows(rows, idx, n_out):
    d = rows.shape[1]
    idx3 = _sc_chunks(idx)
    n_chunks = idx3.shape[1]
    assert rows.dtype.itemsize == 4

    @functools.partial(
        pl.kernel, mesh=_sc_mesh(),
        out_type=jax.ShapeDtypeStruct((n_out, d), rows.dtype),
        scratch_types=_sc_scratch(n_chunks, d, rows.dtype),
        name="sc_row_scatter")
    def scatter(rows_hbm, idx_hbm, out_hbm, idx_v, rows_v, sem_in, sem_out):
        worker = lax.axis_index("s") * SC_CORES + lax.axis_index("c")
        pltpu.sync_copy(idx_hbm.at[worker], idx_v)

        def fetch(c, b):
            off = (worker * n_chunks + c) * GATHER_CHUNK
            return pltpu.make_async_copy(rows_hbm.at[pl.ds(off, GATHER_CHUNK)], rows_v.at[b],
                                         sem_in.at[b])

        def put(c, b):
            return pltpu.make_async_copy(rows_v.at[b], out_hbm.at[idx_v.at[c]], sem_out.at[b])

        _sc_pipeline(n_chunks, fetch, put)

    return scatter(rows, idx3)


def _final_body(x_ref, f_ref, g2_ref, fn_ref, o_ref):
    for g in range(x_ref.shape[0]):
        o_ref[g] = _rms(x_ref[g] + g2_ref[g] * _unpack_pairs(f_ref[g])) * fn_ref[...]


def _final_call(xmid, fg, g2, final_norm, *, n_batch, nj, ncb):
    d = xmid.shape[1]
    tm = ROW_BLOCK
    njl = nj - ncb
    grp = math.gcd(n_batch, FINAL_BATCH_GROUP)
    per_batch = lambda a: a.reshape(n_batch, nj * tm, a.shape[-1])
    src = lambda b, j: (b, ncb + j, 0)
    out = pl.pallas_call(
        _final_body,
        out_shape=jax.ShapeDtypeStruct((n_batch, njl * tm, d), F32),
        grid=(n_batch // grp, njl),
        in_specs=[pl.BlockSpec((grp, tm, d), src), pl.BlockSpec((grp, tm, d // 2), src),
                  pl.BlockSpec((grp, None, 1, d), lambda b, j: (b, 1, 0, 0)),
                  _const_spec((1, d))],
        out_specs=pl.BlockSpec((grp, tm, d), lambda b, j: (b, j, 0)),
        compiler_params=_params(2),
        name="final_norm",
    )(per_batch(xmid), per_batch(fg), g2.reshape(n_batch, 2, 1, d), final_norm.reshape(1, d))
    return out.reshape(n_batch * njl * tm, d)


def _split_cols(w, sizes):
    out, off = [], 0
    for s in sizes:
        out.append(w[:, off:off + s])
        off += s
    out.append(w[:, off:])
    return out


def _layer_weights(w_in, w_uq, w_ukv, q_norm, kv_norm):
    wq, wk, wv, wg, wcb, wcc, wcx, wqd, wkvd, wkr, wgate = _split_cols(w_in, IN_SIZES)
    rope_lanes = lambda w: jnp.pad(w, ((0, 0), (MLA_NOPE, HEAD_PAD - MLA_NOPE - MLA_ROPE)))
    w_ext = jnp.concatenate(
        [wq, wk, wv, wg, wcb, wcc, wcx, wqd, wkvd, rope_lanes(wkr), wgate], axis=1).astype(BF16)

    uq = w_uq.reshape(MLA_Q_RANK, MLA_HEADS, MLA_NOPE + MLA_ROPE)
    tail = HEAD_PAD - MLA_NOPE - MLA_ROPE
    uq_pad = jnp.pad(uq, ((0, 0), (0, 0), (0, tail)))
    wuq = uq_pad.reshape(MLA_Q_RANK, MLA_W).astype(BF16)

    ukv = w_ukv.reshape(MLA_KV_RANK, MLA_HEADS, MLA_NOPE + MLA_V)
    wk_up = jnp.pad(ukv[..., :MLA_NOPE], ((0, 0), (0, 0), (0, HEAD_PAD - MLA_NOPE)))
    wk_up = wk_up.reshape(MLA_KV_RANK, MLA_W).astype(BF16)
    wv_pad = jnp.pad(ukv[..., MLA_NOPE:], ((0, 0), (0, 0), (0, HEAD_PAD - MLA_V)))
    wv_pad = wv_pad.reshape(MLA_KV_RANK, MLA_W).astype(BF16)
    ones_row = jnp.zeros((MLA_HEADS, HEAD_PAD), F32).at[:, MLA_V].set(1.0).reshape(1, MLA_W)
    q_gain = q_norm.astype(F32) * ((MLA_NOPE + MLA_ROPE) ** -0.5 * math.log2(math.e))
    return (w_ext, wuq, wk_up, wv_pad, q_gain.reshape(1, -1),
            kv_norm.reshape(1, -1).astype(F32), ones_row)


def _rotary_tables(n_ctx, n_lat):
    n_all = n_ctx + n_lat
    row = lax.broadcasted_iota(jnp.int32, (n_all, LANES), 0)
    lane = lax.broadcasted_iota(jnp.int32, (n_all, LANES), 1)
    is_ctx = row < n_ctx
    pos = row - n_ctx
    grid_row = pos // GRID_W
    grid_col = pos - grid_row * GRID_W

    def inv_freq(idx, half):
        return ROPE_BASE ** (-idx.astype(F32) / half)

    def table(ang, first_half, live):
        cos = jnp.where(is_ctx, 1.0, jnp.cos(ang))
        sin = jnp.where(is_ctx, 0.0, jnp.where(first_half, -jnp.sin(ang), jnp.sin(ang)))
        return jnp.where(live, cos, 0.0), jnp.where(live, sin, 0.0)

    half = RET_DK // 2
    ang = pos.astype(F32) * inv_freq(lane % half, half)
    cr, sr = table(ang, (lane % RET_DK) < half, True)

    quarter = MLA_ROPE // 4
    rl = lane - MLA_NOPE
    coord = jnp.where(rl < MLA_ROPE // 2, grid_row, grid_col)
    ang = coord.astype(F32) * inv_freq(rl % quarter, quarter)
    cm, sm = table(ang, (rl % (2 * quarter)) < quarter,
                   jnp.logical_and(rl >= 0, rl < MLA_ROPE))
    cm = jnp.where(rl < 0, 1.0, cm)
    return cr, sr, cm, sm


def _retention_consts(ret_decay):
    log_gf = jax.nn.log_sigmoid(ret_decay[0].astype(F32))
    log_gb = jax.nn.log_sigmoid(ret_decay[1].astype(F32))
    idx = jnp.arange(RET_CHUNK, dtype=F32)
    rel = idx[:, None] - idx[None, :]
    dm_f = jnp.where(rel >= 0, jnp.exp(log_gf[:, None, None] * jnp.maximum(rel, 0.0)[None]), 0.0)
    dm_b = jnp.where(rel < 0, jnp.exp(log_gb[:, None, None] * jnp.maximum(-rel, 0.0)[None]), 0.0)

    def lanes(t):
        return jnp.repeat(t, RET_DK, axis=1)

    xi_f = lanes(jnp.exp(log_gf[None, :] * (idx + 1.0)[:, None]))
    zt_f = lanes(jnp.exp(log_gf[None, :] * (RET_CHUNK - 1 - idx)[:, None]))
    xi_b = lanes(jnp.exp(log_gb[None, :] * (RET_CHUNK - idx)[:, None]))
    zt_b = lanes(jnp.exp(log_gb[None, :] * idx[:, None]))
    cd = lambda lg: jnp.broadcast_to(jnp.exp(lg * RET_CHUNK)[:, None, None], (RET_HEADS, 1, RET_DV))
    return dm_f, dm_b, xi_f, zt_f, xi_b, zt_b, cd(log_gf), cd(log_gb)


def kernel(x, c, ctx, c_ctx, w_ada, b_ada, norm1, norm2, w_in, ret_decay, ret_gn, w_ret_o, conv_w,
           w_conv_o, mla_q_norm, w_uq, mla_kv_norm, w_ukv, w_mla_o, w_out, w_router, router_bias,
           w1, w3, w2, final_norm):
    n_batch, n_lat, d = x.shape
    n_ctx = ctx.shape[1]
    depth = w_ada.shape[0]
    t_all = n_ctx + n_lat
    assert n_ctx % ROW_BLOCK == 0 and n_lat % ROW_BLOCK == 0 and n_lat % GRID_W == 0
    nj = t_all // ROW_BLOCK
    ncb = n_ctx // ROW_BLOCK
    n_tok = n_batch * t_all
    assert n_tok % MOE_TILE == 0
    geom = dict(n_batch=n_batch, nj=nj, ncb=ncb)

    cc = jnp.concatenate([c, c_ctx[None, :]], axis=0)
    cc = jnp.pad(cc, ((0, -cc.shape[0] % 8), (0, 0)))
    mod = _ada_call(cc, w_ada, b_ada)[:, :n_batch + 1].reshape(depth, n_batch + 1, N_MOD, d)
    pick = jnp.stack([jnp.full((n_batch,), n_batch, jnp.int32),
                      jnp.arange(n_batch, dtype=jnp.int32)], axis=1).reshape(-1)
    mod = mod[:, pick]

    tabs = _rotary_tables(n_ctx, n_lat)
    wr_hi, wr_lo = _split_bf16(jnp.pad(w_router.astype(F32), ((0, 0), (0, LANES - N_EXPERTS))))
    rbias = router_bias.astype(F32).reshape(N_EXPERTS, 1)

    out = None
    x_parts = (ctx, x)
    for l in range(depth):
        m = mod[l]
        rowvec = lambda v: v.reshape(2 * n_batch, 1, d)
        a1 = rowvec(norm1[l][None, :] * (1.0 + m[:, 1]))
        b1 = rowvec(m[:, 0])
        g1 = rowvec(m[:, 2])
        a2 = rowvec(norm2[l][None, :] * (1.0 + m[:, 4]))
        b2 = rowvec(m[:, 3])
        g2 = rowvec(m[:, 5])

        wts = _layer_weights(w_in[l], w_uq[l], w_ukv[l], mla_q_norm[l], mla_kv_norm[l])
        proj = _inproj_call(x_parts, a1, b1, tabs, wts, **geom)
        rq, rk, rv, rg, cb, cu, qm, km, vm, gt, xa = proj

        yf, yb = _ret_call(rq, rk, rv, _retention_consts(ret_decay[l]), **geom)
        om = _attn_call(qm, km, vm, n_ctx=n_ctx, **geom)

        merge_wts = (ret_gn[l].reshape(1, -1).astype(F32), conv_w[l].T.astype(F32),
                     w_ret_o[l].astype(BF16), w_conv_o[l].astype(BF16), w_mla_o[l].astype(BF16),
                     w_out[l].astype(BF16), wr_hi, wr_lo, rbias)
        xmid, h2, route, gate_w = _merge_call(yf, yb, rg, cb, cu, om, gt, xa, g1, a2, b2, merge_wts,
                                      **geom)
        dest, npad, tile_ea, tile_eb, n_used = _dispatch(route, n_tok)
        hs = _sc_scatter_rows(h2, dest, npad)
        gw = _sc_scatter_rows(gate_w, dest, npad)
        f_sorted = _moe_call(tile_ea, tile_eb, n_used, hs, gw, w1, w3, w2, l)
        fg = _sc_gather_rows(f_sorted, dest)

        if l < depth - 1:
            x_parts = (xmid, fg, g2)
        else:
            out = _final_call(xmid, fg, g2, final_norm, **geom)
    return out.reshape(n_batch, n_lat, d)
```
